```python
import jax, jax.numpy as jnp
from jax import lax
import numpy as np

D_MODEL = 1024
BATCH = 8
SEQ = 4096
DEPTH = 2

GRID_W = 64
D_MIX = D_MODEL
GROUP_W = D_MIX // 4
HEAD_DIM = 64
CONV_A_WIDTH = 31
GQA_HEADS = GROUP_W // HEAD_DIM
GQA_KV_HEADS = 2
CHUNK = 128
SGU_GROUPS = GROUP_W // HEAD_DIM
SGU_GROUP_DIM = GROUP_W // SGU_GROUPS
MLA_HEADS = GROUP_W // HEAD_DIM
MLA_Q_LORA = 3 * D_MODEL // 16
MLA_KV_LORA = D_MODEL // 8
MLA_NOPE = 64
MLA_ROPE = 32
MLA_V = GROUP_W // MLA_HEADS
Q_BLOCK = 128
ROPE_THETA = 10000.0
D_FF = 2816
FFN_CONV_WIDTH = 3
DEEPNORM_ALPHA = (2 * DEPTH) ** 0.25
DEEPNORM_BETA = (8 * DEPTH) ** -0.25
LN_EPS = 1e-5
RMS_EPS = 1e-6

SPLIT_SIZES = (2 * GROUP_W,
               GQA_HEADS * HEAD_DIM,
               GQA_KV_HEADS * HEAD_DIM,
               GQA_KV_HEADS * HEAD_DIM,
               2 * GROUP_W,
               MLA_Q_LORA,
               MLA_KV_LORA,
               MLA_ROPE)
D_IN_PROJ = sum(SPLIT_SIZES)
SPLIT_POINTS = [sum(SPLIT_SIZES[:i + 1]) for i in range(len(SPLIT_SIZES) - 1)]

kernel_name = 'hymba_style_hybrid_encoder'


def layer_norm(x, g, b):
    xf = x.astype(jnp.float32)
    mu = jnp.mean(xf, axis=-1, keepdims=True)
    xc = xf - mu
    var = jnp.mean(xc * xc, axis=-1, keepdims=True)
    return (xc * lax.rsqrt(var + LN_EPS) * g + b).astype(x.dtype)


def rms_norm(x, g):
    xf = x.astype(jnp.float32)
    ms = jnp.mean(xf * xf, axis=-1, keepdims=True)
    return (xf * lax.rsqrt(ms + RMS_EPS) * g).astype(x.dtype)


def depthwise_conv(x, w, b):
    k = w.shape[0]
    c = x.shape[-1]
    y = lax.conv_general_dilated(x, w[:, None, :], window_strides=(1,),
                                 padding=[(k // 2, k // 2)],
                                 dimension_numbers=('NWC', 'WIO', 'NWC'),
                                 feature_group_count=c)
    return y + b


def rope_1d(x, pos):
    d = x.shape[-1]
    half = d // 2
    inv_freq = ROPE_THETA ** (-jnp.arange(half, dtype=jnp.float32) / half)
    ang = pos[:, None] * inv_freq[None, :]
    cos = jnp.cos(ang)[:, None, :]
    sin = jnp.sin(ang)[:, None, :]
    xf = x.astype(jnp.float32)
    x1, x2 = xf[..., :half], xf[..., half:]
    return jnp.concatenate([x1 * cos - x2 * sin, x2 * cos + x1 * sin], axis=-1).astype(x.dtype)


def rope_2d(x, row, col):
    h = x.shape[-1] // 2
    return jnp.concatenate([rope_1d(x[..., :h], row), rope_1d(x[..., h:], col)], axis=-1)


def blocked_attention(q, k, v, scale):
    bsz, seq, hk, g, dk = q.shape
    nb = seq // Q_BLOCK
    qb = q.reshape(bsz, nb, Q_BLOCK, hk, g, dk).transpose(1, 0, 2, 3, 4, 5)

    def one_block(qi):
        s = jnp.einsum('bqhgd,bshd->bhgqs', qi, k, preferred_element_type=jnp.float32) * scale
        p = jax.nn.softmax(s, axis=-1).astype(v.dtype)
        return jnp.einsum('bhgqs,bshd->bqhgd', p, v)

    out = lax.map(one_block, qb)
    return out.transpose(1, 0, 2, 3, 4, 5).reshape(bsz, seq, hk, g, v.shape[-1])


def _fwd_setup_inputs(seed: int = 0) -> dict:
    key = jax.random.key(seed)
    ks = iter(jax.random.split(key, 32))

    def nrm(shape, scale):
        return scale * jax.random.normal(next(ks), shape, dtype=jnp.float32)

    L = DEPTH
    return {
        'x': nrm((BATCH, SEQ, D_MODEL), 1.0),
        'ln_in_g': 1.0 + nrm((D_MODEL,), 0.02),
        'ln_in_b': nrm((D_MODEL,), 0.02),
        'w_in': nrm((L, D_MODEL, D_IN_PROJ), D_MODEL ** -0.5),
        'conv_a_w': nrm((L, CONV_A_WIDTH, GROUP_W), CONV_A_WIDTH ** -0.5),
        'conv_a_b': nrm((L, GROUP_W), 0.02),
        'ln_a_g': 1.0 + nrm((L, GROUP_W), 0.02),
        'ln_a_b': nrm((L, GROUP_W), 0.02),
        'qk_norm_q': 1.0 + nrm((L, HEAD_DIM), 0.02),
        'qk_norm_k': 1.0 + nrm((L, HEAD_DIM), 0.02),
        'sgu_ln_g': 1.0 + nrm((L, GROUP_W), 0.02),
        'sgu_ln_b': nrm((L, GROUP_W), 0.02),
        'sgu_w': nrm((L, SGU_GROUPS, CHUNK, CHUNK), CHUNK ** -0.5),
        'sgu_b': 1.0 + nrm((L, SGU_GROUPS, CHUNK), 0.02),
        'mla_q_norm': 1.0 + nrm((L, MLA_Q_LORA), 0.02),
        'mla_w_uq': nrm((L, MLA_Q_LORA, MLA_HEADS * (MLA_NOPE + MLA_ROPE)), MLA_Q_LORA ** -0.5),
        'mla_kv_norm': 1.0 + nrm((L, MLA_KV_LORA), 0.02),
        'mla_w_ukv': nrm((L, MLA_KV_LORA, MLA_HEADS * (MLA_NOPE + MLA_V)), MLA_KV_LORA ** -0.5),
        'w_out': nrm((L, D_MIX, D_MODEL), DEEPNORM_BETA * D_MIX ** -0.5),
        'ln_mix_g': 1.0 + nrm((L, D_MODEL), 0.02),
        'ln_mix_b': nrm((L, D_MODEL), 0.02),
        'ffn_w_up': nrm((L, D_MODEL, 2 * D_FF), D_MODEL ** -0.5),
        'ffn_conv_w': nrm((L, FFN_CONV_WIDTH, 2 * D_FF), FFN_CONV_WIDTH ** -0.5),
        'ffn_conv_b': nrm((L, 2 * D_FF), 0.02),
        'ffn_w_down': nrm((L, D_FF, D_MODEL), DEEPNORM_BETA * D_FF ** -0.5),
        'ln_ffn_g': 1.0 + nrm((L, D_MODEL), 0.02),
        'ln_ffn_b': nrm((L, D_MODEL), 0.02),
    }


def _fwd_reference(x, ln_in_g, ln_in_b, w_in, conv_a_w, conv_a_b, ln_a_g, ln_a_b,
              qk_norm_q, qk_norm_k, sgu_ln_g, sgu_ln_b, sgu_w, sgu_b,
              mla_q_norm, mla_w_uq, mla_kv_norm, mla_w_ukv, w_out, ln_mix_g, ln_mix_b,
              ffn_w_up, ffn_conv_w, ffn_conv_b, ffn_w_down, ln_ffn_g, ln_ffn_b):
    bsz, seq, _ = x.shape
    rows = seq // GRID_W
    row = jnp.repeat(jnp.arange(rows, dtype=jnp.float32), GRID_W)
    col = jnp.tile(jnp.arange(GRID_W, dtype=jnp.float32), rows)

    h = layer_norm(x, ln_in_g, ln_in_b)
    for l in range(DEPTH):
        proj = h @ w_in[l]
        a_in, b_q, b_k, b_v, c_in, d_cq, d_ckv, d_kr = jnp.split(proj, SPLIT_POINTS, axis=-1)

        a = a_in[..., :GROUP_W] * jax.nn.sigmoid(a_in[..., GROUP_W:])
        a = depthwise_conv(a, conv_a_w[l], conv_a_b[l])
        o_a = jax.nn.silu(layer_norm(a, ln_a_g[l], ln_a_b[l]))

        q = rms_norm(b_q.reshape(bsz, seq, GQA_HEADS, HEAD_DIM), qk_norm_q[l])
        k = rms_norm(b_k.reshape(bsz, seq, GQA_KV_HEADS, HEAD_DIM), qk_norm_k[l])
        v = b_v.reshape(bsz, seq, GQA_KV_HEADS, HEAD_DIM)
        q = rope_2d(q, row, col).reshape(bsz, seq, GQA_KV_HEADS, GQA_HEADS // GQA_KV_HEADS, HEAD_DIM)
        k = rope_2d(k, row, col)
        o_b = blocked_attention(q, k, v, HEAD_DIM ** -0.5).reshape(bsz, seq, GROUP_W)

        c = jax.nn.gelu(c_in)
        u, sv = c[..., :GROUP_W], c[..., GROUP_W:]
        sv = layer_norm(sv, sgu_ln_g[l], sgu_ln_b[l])
        sv = sv.reshape(bsz, seq // CHUNK, CHUNK, SGU_GROUPS, SGU_GROUP_DIM)
        sv = jnp.einsum('gpq,bnqgc->bnpgc', sgu_w[l], sv) + sgu_b[l].T[:, :, None]
        o_c = u * sv.reshape(bsz, seq, GROUP_W)

        qd = (rms_norm(d_cq, mla_q_norm[l]) @ mla_w_uq[l]).reshape(bsz, seq, MLA_HEADS, MLA_NOPE + MLA_ROPE)
        kvd = (rms_norm(d_ckv, mla_kv_norm[l]) @ mla_w_ukv[l]).reshape(bsz, seq, MLA_HEADS, MLA_NOPE + MLA_V)
        q_nope, q_rope = qd[..., :MLA_NOPE], qd[..., MLA_NOPE:]
        k_nope, v_d = kvd[..., :MLA_NOPE], kvd[..., MLA_NOPE:]
        k_rope = rope_2d(d_kr[:, :, None, :], row, col)
        q_full = jnp.concatenate([q_nope, rope_2d(q_rope, row, col)], axis=-1)
        k_full = jnp.concatenate([k_nope, jnp.broadcast_to(k_rope, (bsz, seq, MLA_HEADS, MLA_ROPE))], axis=-1)
        o_d = blocked_attention(q_full[:, :, :, None, :], k_full, v_d,
                                (MLA_NOPE + MLA_ROPE) ** -0.5).reshape(bsz, seq, GROUP_W)

        mix = jnp.concatenate([o_a, o_b, o_c, o_d], axis=-1) @ w_out[l]
        h = layer_norm(DEEPNORM_ALPHA * h + mix, ln_mix_g[l], ln_mix_b[l])

        up = depthwise_conv(h @ ffn_w_up[l], ffn_conv_w[l], ffn_conv_b[l])
        f = (jax.nn.silu(up[..., :D_FF]) * up[..., D_FF:]) @ ffn_w_down[l]
        h = layer_norm(DEEPNORM_ALPHA * h + f, ln_ffn_g[l], ln_ffn_b[l])
    return h


import jax as _jax
import jax.numpy as _jnp

TWIN_FORMAT = 'train_step'
FWD_PARAMS = ['x', 'ln_in_g', 'ln_in_b', 'w_in', 'conv_a_w', 'conv_a_b', 'ln_a_g', 'ln_a_b', 'qk_norm_q', 'qk_norm_k', 'sgu_ln_g', 'sgu_ln_b', 'sgu_w', 'sgu_b', 'mla_q_norm', 'mla_w_uq', 'mla_kv_norm', 'mla_w_ukv', 'w_out', 'ln_mix_g', 'ln_mix_b', 'ffn_w_up', 'ffn_conv_w', 'ffn_conv_b', 'ffn_w_down', 'ln_ffn_g', 'ln_ffn_b']
TWIN_WEIGHTS = ['ln_in_g', 'ln_in_b', 'w_in', 'conv_a_w', 'conv_a_b', 'ln_a_g', 'ln_a_b', 'qk_norm_q', 'qk_norm_k', 'sgu_ln_g', 'sgu_ln_b', 'sgu_w', 'sgu_b', 'mla_q_norm', 'mla_w_uq', 'mla_kv_norm', 'mla_w_ukv', 'w_out', 'ln_mix_g', 'ln_mix_b', 'ffn_w_up', 'ffn_conv_w', 'ffn_conv_b', 'ffn_w_down', 'ln_ffn_g', 'ln_ffn_b']
TWIN_DIFF_INPUT = 'x'
TWIN_INPUTS = ['x', 'ln_in_g', 'ln_in_b', 'w_in', 'conv_a_w', 'conv_a_b', 'ln_a_g', 'ln_a_b', 'qk_norm_q', 'qk_norm_k', 'sgu_ln_g', 'sgu_ln_b', 'sgu_w', 'sgu_b', 'mla_q_norm', 'mla_w_uq', 'mla_kv_norm', 'mla_w_ukv', 'w_out', 'ln_mix_g', 'ln_mix_b', 'ffn_w_up', 'ffn_conv_w', 'ffn_conv_b', 'ffn_w_down', 'ln_ffn_g', 'ln_ffn_b', 'loss_target', 'm_ln_in_g', 'm_ln_in_b', 'm_w_in', 'm_conv_a_w', 'm_conv_a_b', 'm_ln_a_g', 'm_ln_a_b', 'm_qk_norm_q', 'm_qk_norm_k', 'm_sgu_ln_g', 'm_sgu_ln_b', 'm_sgu_w', 'm_sgu_b', 'm_mla_q_norm', 'm_mla_w_uq', 'm_mla_kv_norm', 'm_mla_w_ukv', 'm_w_out', 'm_ln_mix_g', 'm_ln_mix_b', 'm_ffn_w_up', 'm_ffn_conv_w', 'm_ffn_conv_b', 'm_ffn_w_down', 'm_ln_ffn_g', 'm_ln_ffn_b', 'v_ln_in_g', 'v_ln_in_b', 'v_w_in', 'v_conv_a_w', 'v_conv_a_b', 'v_ln_a_g', 'v_ln_a_b', 'v_qk_norm_q', 'v_qk_norm_k', 'v_sgu_ln_g', 'v_sgu_ln_b', 'v_sgu_w', 'v_sgu_b', 'v_mla_q_norm', 'v_mla_w_uq', 'v_mla_kv_norm', 'v_mla_w_ukv', 'v_w_out', 'v_ln_mix_g', 'v_ln_mix_b', 'v_ffn_w_up', 'v_ffn_conv_w', 'v_ffn_conv_b', 'v_ffn_w_down', 'v_ln_ffn_g', 'v_ln_ffn_b']
TWIN_OUTPUTS = ['loss', 'grad_x', 'grad_ln_in_g', 'grad_ln_in_b', 'grad_w_in', 'grad_conv_a_w', 'grad_conv_a_b', 'grad_ln_a_g', 'grad_ln_a_b', 'grad_qk_norm_q', 'grad_qk_norm_k', 'grad_sgu_ln_g', 'grad_sgu_ln_b', 'grad_sgu_w', 'grad_sgu_b', 'grad_mla_q_norm', 'grad_mla_w_uq', 'grad_mla_kv_norm', 'grad_mla_w_ukv', 'grad_w_out', 'grad_ln_mix_g', 'grad_ln_mix_b', 'grad_ffn_w_up', 'grad_ffn_conv_w', 'grad_ffn_conv_b', 'grad_ffn_w_down', 'grad_ln_ffn_g', 'grad_ln_ffn_b', 'delta_ln_in_g', 'delta_ln_in_b', 'delta_w_in', 'delta_conv_a_w', 'delta_conv_a_b', 'delta_ln_a_g', 'delta_ln_a_b', 'delta_qk_norm_q', 'delta_qk_norm_k', 'delta_sgu_ln_g', 'delta_sgu_ln_b', 'delta_sgu_w', 'delta_sgu_b', 'delta_mla_q_norm', 'delta_mla_w_uq', 'delta_mla_kv_norm', 'delta_mla_w_ukv', 'delta_w_out', 'delta_ln_mix_g', 'delta_ln_mix_b', 'delta_ffn_w_up', 'delta_ffn_conv_w', 'delta_ffn_conv_b', 'delta_ffn_w_down', 'delta_ln_ffn_g', 'delta_ln_ffn_b', 'new_m_ln_in_g', 'new_m_ln_in_b', 'new_m_w_in', 'new_m_conv_a_w', 'new_m_conv_a_b', 'new_m_ln_a_g', 'new_m_ln_a_b', 'new_m_qk_norm_q', 'new_m_qk_norm_k', 'new_m_sgu_ln_g', 'new_m_sgu_ln_b', 'new_m_sgu_w', 'new_m_sgu_b', 'new_m_mla_q_norm', 'new_m_mla_w_uq', 'new_m_mla_kv_norm', 'new_m_mla_w_ukv', 'new_m_w_out', 'new_m_ln_mix_g', 'new_m_ln_mix_b', 'new_m_ffn_w_up', 'new_m_ffn_conv_w', 'new_m_ffn_conv_b', 'new_m_ffn_w_down', 'new_m_ln_ffn_g', 'new_m_ln_ffn_b', 'new_v_ln_in_g', 'new_v_ln_in_b', 'new_v_w_in', 'new_v_conv_a_w', 'new_v_conv_a_b', 'new_v_ln_a_g', 'new_v_ln_a_b', 'new_v_qk_norm_q', 'new_v_qk_norm_k', 'new_v_sgu_ln_g', 'new_v_sgu_ln_b', 'new_v_sgu_w', 'new_v_sgu_b', 'new_v_mla_q_norm', 'new_v_mla_w_uq', 'new_v_mla_kv_norm', 'new_v_mla_w_ukv', 'new_v_w_out', 'new_v_ln_mix_g', 'new_v_ln_mix_b', 'new_v_ffn_w_up', 'new_v_ffn_conv_w', 'new_v_ffn_conv_b', 'new_v_ffn_w_down', 'new_v_ln_ffn_g', 'new_v_ln_ffn_b']
TWIN_LEAF_KINDS = {'loss': 'loss', 'grad_x': 'grad_x', 'grad_ln_in_g': 'grad_w', 'grad_ln_in_b': 'grad_w', 'grad_w_in': 'grad_w', 'grad_conv_a_w': 'grad_w', 'grad_conv_a_b': 'grad_w', 'grad_ln_a_g': 'grad_w', 'grad_ln_a_b': 'grad_w', 'grad_qk_norm_q': 'grad_w', 'grad_qk_norm_k': 'grad_w', 'grad_sgu_ln_g': 'grad_w', 'grad_sgu_ln_b': 'grad_w', 'grad_sgu_w': 'grad_w', 'grad_sgu_b': 'grad_w', 'grad_mla_q_norm': 'grad_w', 'grad_mla_w_uq': 'grad_w', 'grad_mla_kv_norm': 'grad_w', 'grad_mla_w_ukv': 'grad_w', 'grad_w_out': 'grad_w', 'grad_ln_mix_g': 'grad_w', 'grad_ln_mix_b': 'grad_w', 'grad_ffn_w_up': 'grad_w', 'grad_ffn_conv_w': 'grad_w', 'grad_ffn_conv_b': 'grad_w', 'grad_ffn_w_down': 'grad_w', 'grad_ln_ffn_g': 'grad_w', 'grad_ln_ffn_b': 'grad_w', 'delta_ln_in_g': 'delta_w', 'delta_ln_in_b': 'delta_w', 'delta_w_in': 'delta_w', 'delta_conv_a_w': 'delta_w', 'delta_conv_a_b': 'delta_w', 'delta_ln_a_g': 'delta_w', 'delta_ln_a_b': 'delta_w', 'delta_qk_norm_q': 'delta_w', 'delta_qk_norm_k': 'delta_w', 'delta_sgu_ln_g': 'delta_w', 'delta_sgu_ln_b': 'delta_w', 'delta_sgu_w': 'delta_w', 'delta_sgu_b': 'delta_w', 'delta_mla_q_norm': 'delta_w', 'delta_mla_w_uq': 'delta_w', 'delta_mla_kv_norm': 'delta_w', 'delta_mla_w_ukv': 'delta_w', 'delta_w_out': 'delta_w', 'delta_ln_mix_g': 'delta_w', 'delta_ln_mix_b': 'delta_w', 'delta_ffn_w_up': 'delta_w', 'delta_ffn_conv_w': 'delta_w', 'delta_ffn_conv_b': 'delta_w', 'delta_ffn_w_down': 'delta_w', 'delta_ln_ffn_g': 'delta_w', 'delta_ln_ffn_b': 'delta_w', 'new_m_ln_in_g': 'new_m', 'new_m_ln_in_b': 'new_m', 'new_m_w_in': 'new_m', 'new_m_conv_a_w': 'new_m', 'new_m_conv_a_b': 'new_m', 'new_m_ln_a_g': 'new_m', 'new_m_ln_a_b': 'new_m', 'new_m_qk_norm_q': 'new_m', 'new_m_qk_norm_k': 'new_m', 'new_m_sgu_ln_g': 'new_m', 'new_m_sgu_ln_b': 'new_m', 'new_m_sgu_w': 'new_m', 'new_m_sgu_b': 'new_m', 'new_m_mla_q_norm': 'new_m', 'new_m_mla_w_uq': 'new_m', 'new_m_mla_kv_norm': 'new_m', 'new_m_mla_w_ukv': 'new_m', 'new_m_w_out': 'new_m', 'new_m_ln_mix_g': 'new_m', 'new_m_ln_mix_b': 'new_m', 'new_m_ffn_w_up': 'new_m', 'new_m_ffn_conv_w': 'new_m', 'new_m_ffn_conv_b': 'new_m', 'new_m_ffn_w_down': 'new_m', 'new_m_ln_ffn_g': 'new_m', 'new_m_ln_ffn_b': 'new_m', 'new_v_ln_in_g': 'new_v', 'new_v_ln_in_b': 'new_v', 'new_v_w_in': 'new_v', 'new_v_conv_a_w': 'new_v', 'new_v_conv_a_b': 'new_v', 'new_v_ln_a_g': 'new_v', 'new_v_ln_a_b': 'new_v', 'new_v_qk_norm_q': 'new_v', 'new_v_qk_norm_k': 'new_v', 'new_v_sgu_ln_g': 'new_v', 'new_v_sgu_ln_b': 'new_v', 'new_v_sgu_w': 'new_v', 'new_v_sgu_b': 'new_v', 'new_v_mla_q_norm': 'new_v', 'new_v_mla_w_uq': 'new_v', 'new_v_mla_kv_norm': 'new_v', 'new_v_mla_w_ukv': 'new_v', 'new_v_w_out': 'new_v', 'new_v_ln_mix_g': 'new_v', 'new_v_ln_mix_b': 'new_v', 'new_v_ffn_w_up': 'new_v', 'new_v_ffn_conv_w': 'new_v', 'new_v_ffn_conv_b': 'new_v', 'new_v_ffn_w_down': 'new_v', 'new_v_ln_ffn_g': 'new_v', 'new_v_ln_ffn_b': 'new_v'}


def _forward(args):
    return _fwd_reference(*[args[k] for k in FWD_PARAMS])


def _output_shape():
    out = _jax.eval_shape(lambda: _forward(_fwd_setup_inputs(0)))
    return out.shape, out.dtype

N_MICROBATCH = 1
ADAM_LR = 0.001
ADAM_B1 = 0.9
ADAM_B2 = 0.999
ADAM_EPS = 1e-08
ADAM_WD = 0.01
ADAM_STEP = 10
PER_EXAMPLE_BATCH_AXIS = {'x': 0, 'loss_target': 0}
SHARED_INPUTS = []
_WEIGHT_DTYPES = {'ln_in_g': _jnp.float32, 'ln_in_b': _jnp.float32, 'w_in': _jnp.float32, 'conv_a_w': _jnp.float32, 'conv_a_b': _jnp.float32, 'ln_a_g': _jnp.float32, 'ln_a_b': _jnp.float32, 'qk_norm_q': _jnp.float32, 'qk_norm_k': _jnp.float32, 'sgu_ln_g': _jnp.float32, 'sgu_ln_b': _jnp.float32, 'sgu_w': _jnp.float32, 'sgu_b': _jnp.float32, 'mla_q_norm': _jnp.float32, 'mla_w_uq': _jnp.float32, 'mla_kv_norm': _jnp.float32, 'mla_w_ukv': _jnp.float32, 'w_out': _jnp.float32, 'ln_mix_g': _jnp.float32, 'ln_mix_b': _jnp.float32, 'ffn_w_up': _jnp.float32, 'ffn_conv_w': _jnp.float32, 'ffn_conv_b': _jnp.float32, 'ffn_w_down': _jnp.float32, 'ln_ffn_g': _jnp.float32, 'ln_ffn_b': _jnp.float32}
MOMENT_SCALE = {'ln_in_g': 8.811419e-01, 'ln_in_b': 4.541316e-01, 'w_in': 3.316842e-02, 'conv_a_w': 3.945687e-02, 'conv_a_b': 1.495974e-01, 'ln_a_g': 6.110943e-02, 'ln_a_b': 8.239390e-02, 'qk_norm_q': 1.351797e-02, 'qk_norm_k': 1.318132e-02, 'sgu_ln_g': 4.256234e-02, 'sgu_ln_b': 4.903081e-02, 'sgu_w': 2.944947e-02, 'sgu_b': 2.828982e-02, 'mla_q_norm': 9.406644e-03, 'mla_w_uq': 6.479195e-03, 'mla_kv_norm': 1.869161e-02, 'mla_w_ukv': 8.170041e-03, 'w_out': 8.176650e-02, 'ln_mix_g': 9.891115e-01, 'ln_mix_b': 4.500696e-01, 'ffn_w_up': 2.355066e-02, 'ffn_conv_w': 2.334329e-02, 'ffn_conv_b': 2.556794e-02, 'ffn_w_down': 7.651691e-02, 'ln_ffn_g': 2.265175e+01, 'ln_ffn_b': 2.016770e+00}


def _to_microbatches(a, axis):
    t = _jnp.moveaxis(a, axis, 0)
    t = t.reshape((N_MICROBATCH, t.shape[0] // N_MICROBATCH) + t.shape[1:])
    return _jnp.moveaxis(t, 1, axis + 1)


def setup_inputs(seed: int = 0) -> dict:
    inp = _fwd_setup_inputs(seed)
    key = _jax.random.fold_in(_jax.random.key(seed), 7919)
    shape, _ = _output_shape()
    out = dict(inp)
    out["loss_target"] = _jax.random.normal(_jax.random.fold_in(key, 0), shape, _jnp.float32)
    for i, name in enumerate(TWIN_WEIGHTS):
        w = inp[name].astype(_jnp.float32)
        if MOMENT_SCALE is None:
            s = _jnp.sqrt(_jnp.mean(_jnp.square(w)) + 1e-30)
        else:
            s = MOMENT_SCALE[name]
        km, kv = _jax.random.split(_jax.random.fold_in(key, i + 1))
        out[name] = w
        out["m_" + name] = s * _jax.random.normal(km, w.shape, _jnp.float32)
        out["v_" + name] = (s * s) * _jax.random.uniform(kv, w.shape, _jnp.float32, 0.5, 1.5)
    if N_MICROBATCH > 1:
        for name, axis in PER_EXAMPLE_BATCH_AXIS.items():
            out[name] = _to_microbatches(out[name], axis)
    return {'x': out['x'], 'ln_in_g': out['ln_in_g'], 'ln_in_b': out['ln_in_b'], 'w_in': out['w_in'], 'conv_a_w': out['conv_a_w'], 'conv_a_b': out['conv_a_b'], 'ln_a_g': out['ln_a_g'], 'ln_a_b': out['ln_a_b'], 'qk_norm_q': out['qk_norm_q'], 'qk_norm_k': out['qk_norm_k'], 'sgu_ln_g': out['sgu_ln_g'], 'sgu_ln_b': out['sgu_ln_b'], 'sgu_w': out['sgu_w'], 'sgu_b': out['sgu_b'], 'mla_q_norm': out['mla_q_norm'], 'mla_w_uq': out['mla_w_uq'], 'mla_kv_norm': out['mla_kv_norm'], 'mla_w_ukv': out['mla_w_ukv'], 'w_out': out['w_out'], 'ln_mix_g': out['ln_mix_g'], 'ln_mix_b': out['ln_mix_b'], 'ffn_w_up': out['ffn_w_up'], 'ffn_conv_w': out['ffn_conv_w'], 'ffn_conv_b': out['ffn_conv_b'], 'ffn_w_down': out['ffn_w_down'], 'ln_ffn_g': out['ln_ffn_g'], 'ln_ffn_b': out['ln_ffn_b'], 'loss_target': out['loss_target'], 'm_ln_in_g': out['m_ln_in_g'], 'm_ln_in_b': out['m_ln_in_b'], 'm_w_in': out['m_w_in'], 'm_conv_a_w': out['m_conv_a_w'], 'm_conv_a_b': out['m_conv_a_b'], 'm_ln_a_g': out['m_ln_a_g'], 'm_ln_a_b': out['m_ln_a_b'], 'm_qk_norm_q': out['m_qk_norm_q'], 'm_qk_norm_k': out['m_qk_norm_k'], 'm_sgu_ln_g': out['m_sgu_ln_g'], 'm_sgu_ln_b': out['m_sgu_ln_b'], 'm_sgu_w': out['m_sgu_w'], 'm_sgu_b': out['m_sgu_b'], 'm_mla_q_norm': out['m_mla_q_norm'], 'm_mla_w_uq': out['m_mla_w_uq'], 'm_mla_kv_norm': out['m_mla_kv_norm'], 'm_mla_w_ukv': out['m_mla_w_ukv'], 'm_w_out': out['m_w_out'], 'm_ln_mix_g': out['m_ln_mix_g'], 'm_ln_mix_b': out['m_ln_mix_b'], 'm_ffn_w_up': out['m_ffn_w_up'], 'm_ffn_conv_w': out['m_ffn_conv_w'], 'm_ffn_conv_b': out['m_ffn_conv_b'], 'm_ffn_w_down': out['m_ffn_w_down'], 'm_ln_ffn_g': out['m_ln_ffn_g'], 'm_ln_ffn_b': out['m_ln_ffn_b'], 'v_ln_in_g': out['v_ln_in_g'], 'v_ln_in_b': out['v_ln_in_b'], 'v_w_in': out['v_w_in'], 'v_conv_a_w': out['v_conv_a_w'], 'v_conv_a_b': out['v_conv_a_b'], 'v_ln_a_g': out['v_ln_a_g'], 'v_ln_a_b': out['v_ln_a_b'], 'v_qk_norm_q': out['v_qk_norm_q'], 'v_qk_norm_k': out['v_qk_norm_k'], 'v_sgu_ln_g': out['v_sgu_ln_g'], 'v_sgu_ln_b': out['v_sgu_ln_b'], 'v_sgu_w': out['v_sgu_w'], 'v_sgu_b': out['v_sgu_b'], 'v_mla_q_norm': out['v_mla_q_norm'], 'v_mla_w_uq': out['v_mla_w_uq'], 'v_mla_kv_norm': out['v_mla_kv_norm'], 'v_mla_w_ukv': out['v_mla_w_ukv'], 'v_w_out': out['v_w_out'], 'v_ln_mix_g': out['v_ln_mix_g'], 'v_ln_mix_b': out['v_ln_mix_b'], 'v_ffn_w_up': out['v_ffn_w_up'], 'v_ffn_conv_w': out['v_ffn_conv_w'], 'v_ffn_conv_b': out['v_ffn_conv_b'], 'v_ffn_w_down': out['v_ffn_w_down'], 'v_ln_ffn_g': out['v_ln_ffn_g'], 'v_ln_ffn_b': out['v_ln_ffn_b']}


def _loss(weights, diff, rest, loss_target):
    with _jax.named_scope("forward"):
        args = {**rest, TWIN_DIFF_INPUT: diff, **{k: w.astype(_WEIGHT_DTYPES[k]) for k, w in weights.items()}}
        y = _forward(args)
    with _jax.named_scope("loss_head"):
        err = _jnp.square(y.astype(_jnp.float32) - loss_target)
        return 0.5 * _jnp.sum(_jnp.mean(err, axis=-1)) if err.ndim else 0.5 * err


def _adamw(w, g, m, v):
    m = ADAM_B1 * m + (1.0 - ADAM_B1) * g
    v = ADAM_B2 * v + (1.0 - ADAM_B2) * _jnp.square(g)
    m_hat = m / (1.0 - ADAM_B1 ** ADAM_STEP)
    v_hat = v / (1.0 - ADAM_B2 ** ADAM_STEP)
    delta = -ADAM_LR * (m_hat / (_jnp.sqrt(v_hat) + ADAM_EPS) + ADAM_WD * w)
    return delta, m, v


def reference(x, ln_in_g, ln_in_b, w_in, conv_a_w, conv_a_b, ln_a_g, ln_a_b, qk_norm_q, qk_norm_k, sgu_ln_g, sgu_ln_b, sgu_w, sgu_b, mla_q_norm, mla_w_uq, mla_kv_norm, mla_w_ukv, w_out, ln_mix_g, ln_mix_b, ffn_w_up, ffn_conv_w, ffn_conv_b, ffn_w_down, ln_ffn_g, ln_ffn_b, loss_target, m_ln_in_g, m_ln_in_b, m_w_in, m_conv_a_w, m_conv_a_b, m_ln_a_g, m_ln_a_b, m_qk_norm_q, m_qk_norm_k, m_sgu_ln_g, m_sgu_ln_b, m_sgu_w, m_sgu_b, m_mla_q_norm, m_mla_w_uq, m_mla_kv_norm, m_mla_w_ukv, m_w_out, m_ln_mix_g, m_ln_mix_b, m_ffn_w_up, m_ffn_conv_w, m_ffn_conv_b, m_ffn_w_down, m_ln_ffn_g, m_ln_ffn_b, v_ln_in_g, v_ln_in_b, v_w_in, v_conv_a_w, v_conv_a_b, v_ln_a_g, v_ln_a_b, v_qk_norm_q, v_qk_norm_k, v_sgu_ln_g, v_sgu_ln_b, v_sgu_w, v_sgu_b, v_mla_q_norm, v_mla_w_uq, v_mla_kv_norm, v_mla_w_ukv, v_w_out, v_ln_mix_g, v_ln_mix_b, v_ffn_w_up, v_ffn_conv_w, v_ffn_conv_b, v_ffn_w_down, v_ln_ffn_g, v_ln_ffn_b):
    given = dict(x=x, ln_in_g=ln_in_g, ln_in_b=ln_in_b, w_in=w_in, conv_a_w=conv_a_w, conv_a_b=conv_a_b, ln_a_g=ln_a_g, ln_a_b=ln_a_b, qk_norm_q=qk_norm_q, qk_norm_k=qk_norm_k, sgu_ln_g=sgu_ln_g, sgu_ln_b=sgu_ln_b, sgu_w=sgu_w, sgu_b=sgu_b, mla_q_norm=mla_q_norm, mla_w_uq=mla_w_uq, mla_kv_norm=mla_kv_norm, mla_w_ukv=mla_w_ukv, w_out=w_out, ln_mix_g=ln_mix_g, ln_mix_b=ln_mix_b, ffn_w_up=ffn_w_up, ffn_conv_w=ffn_conv_w, ffn_conv_b=ffn_conv_b, ffn_w_down=ffn_w_down, ln_ffn_g=ln_ffn_g, ln_ffn_b=ln_ffn_b, loss_target=loss_target, m_ln_in_g=m_ln_in_g, m_ln_in_b=m_ln_in_b, m_w_in=m_w_in, m_conv_a_w=m_conv_a_w, m_conv_a_b=m_conv_a_b, m_ln_a_g=m_ln_a_g, m_ln_a_b=m_ln_a_b, m_qk_norm_q=m_qk_norm_q, m_qk_norm_k=m_qk_norm_k, m_sgu_ln_g=m_sgu_ln_g, m_sgu_ln_b=m_sgu_ln_b, m_sgu_w=m_sgu_w, m_sgu_b=m_sgu_b, m_mla_q_norm=m_mla_q_norm, m_mla_w_uq=m_mla_w_uq, m_mla_kv_norm=m_mla_kv_norm, m_mla_w_ukv=m_mla_w_ukv, m_w_out=m_w_out, m_ln_mix_g=m_ln_mix_g, m_ln_mix_b=m_ln_mix_b, m_ffn_w_up=m_ffn_w_up, m_ffn_conv_w=m_ffn_conv_w, m_ffn_conv_b=m_ffn_conv_b, m_ffn_w_down=m_ffn_w_down, m_ln_ffn_g=m_ln_ffn_g, m_ln_ffn_b=m_ln_ffn_b, v_ln_in_g=v_ln_in_g, v_ln_in_b=v_ln_in_b, v_w_in=v_w_in, v_conv_a_w=v_conv_a_w, v_conv_a_b=v_conv_a_b, v_ln_a_g=v_ln_a_g, v_ln_a_b=v_ln_a_b, v_qk_norm_q=v_qk_norm_q, v_qk_norm_k=v_qk_norm_k, v_sgu_ln_g=v_sgu_ln_g, v_sgu_ln_b=v_sgu_ln_b, v_sgu_w=v_sgu_w, v_sgu_b=v_sgu_b, v_mla_q_norm=v_mla_q_norm, v_mla_w_uq=v_mla_w_uq, v_mla_kv_norm=v_mla_kv_norm, v_mla_w_ukv=v_mla_w_ukv, v_w_out=v_w_out, v_ln_mix_g=v_ln_mix_g, v_ln_mix_b=v_ln_mix_b, v_ffn_w_up=v_ffn_w_up, v_ffn_conv_w=v_ffn_conv_w, v_ffn_conv_b=v_ffn_conv_b, v_ffn_w_down=v_ffn_w_down, v_ln_ffn_g=v_ln_ffn_g, v_ln_ffn_b=v_ln_ffn_b)
    weights = {n: given[n] for n in TWIN_WEIGHTS}
    shared = {n: given[n] for n in SHARED_INPUTS}
    per_example = {n: given[n] for n in ['x']}
    grad_fn = _jax.value_and_grad(_loss, argnums=(0, 1))

    def one_microbatch(ex, loss_target):
        ex = dict(ex)
        diff = ex.pop(TWIN_DIFF_INPUT)
        return grad_fn(weights, diff, {**shared, **ex}, loss_target)

    if N_MICROBATCH == 1:
        loss, (grad_w, grad_x) = one_microbatch(per_example, given["loss_target"])
    else:
        def body(carry, xs):
            loss_sum, grad_sum = carry
            l_k, (gw_k, gx_k) = one_microbatch(xs[0], xs[1])
            with _jax.named_scope("update"):
                return (loss_sum + l_k, _jax.tree.map(_jnp.add, grad_sum, gw_k)), gx_k

        init = (_jnp.zeros((), _jnp.float32), _jax.tree.map(_jnp.zeros_like, weights))
        (loss, grad_w), grad_x = _jax.lax.scan(body, init, (per_example, given["loss_target"]))
    with _jax.named_scope("update"):
        delta_w, new_m, new_v = {}, {}, {}
        for n in TWIN_WEIGHTS:
            delta_w[n], new_m[n], new_v[n] = _adamw(weights[n], grad_w[n], given["m_" + n], given["v_" + n])
    return (loss, grad_x, *[grad_w[n] for n in TWIN_WEIGHTS], *[delta_w[n] for n in TWIN_WEIGHTS],
            *[new_m[n] for n in TWIN_WEIGHTS], *[new_v[n] for n in TWIN_WEIGHTS])
```

```python
import functools
import math

import jax
import jax.numpy as jnp
from jax import lax
from jax.experimental import pallas as pl
from jax.experimental.pallas import tpu as pltpu

F32 = jnp.float32
MXU_DTYPE = jnp.bfloat16
COMM_DTYPE = jnp.bfloat16

N_DEV = 8
D_MODEL = 1024
DEPTH = 2
GRID_W = 64
GROUP_W = 256
HEAD_DIM = 64
CONV_A_WIDTH = 31
CONV_A_HALO = 16
GQA_HEADS = 4
GQA_KV_HEADS = 2
CHUNK = 128
SGU_GROUPS = 4
MLA_HEADS = 4
MLA_Q_LORA = 192
MLA_KV_LORA = 128
MLA_NOPE = 64
MLA_ROPE = 32
MLA_V = 64
MLA_DK_PAD = 128
ROPE_THETA = 10000.0
D_FF = 2816
DEEPNORM_ALPHA = (2 * DEPTH) ** 0.25
LN_EPS = 1e-5
RMS_EPS = 1e-6
D_IN_PROJ = 1888

ADAM_LR = 0.001
ADAM_B1 = 0.9
ADAM_B2 = 0.999
ADAM_EPS = 1e-08
ADAM_WD = 0.01
ADAM_STEP = 10

WEIGHTS = ['ln_in_g', 'ln_in_b', 'w_in', 'conv_a_w', 'conv_a_b', 'ln_a_g', 'ln_a_b', 'qk_norm_q', 'qk_norm_k',
           'sgu_ln_g', 'sgu_ln_b', 'sgu_w', 'sgu_b', 'mla_q_norm', 'mla_w_uq', 'mla_kv_norm', 'mla_w_ukv', 'w_out',
           'ln_mix_g', 'ln_mix_b', 'ffn_w_up', 'ffn_conv_w', 'ffn_conv_b', 'ffn_w_down', 'ln_ffn_g', 'ln_ffn_b']
SHARDED = {'w_in': 2, 'conv_a_w': 2, 'mla_w_uq': 2, 'mla_w_ukv': 2, 'w_out': 1, 'ffn_w_up': 2, 'ffn_conv_w': 2,
           'ffn_w_down': 1}
MATMUL_WEIGHTS = ('w_in', 'w_out', 'ffn_w_up', 'ffn_w_down')
REPLICATED = [n for n in WEIGHTS if n not in SHARDED]

ROW_TILE = 256
VMEM_LIMIT = 56 * 1024 * 1024


def _rawdot(a, b, ca, cb):
    return lax.dot_general(a.astype(MXU_DTYPE), b.astype(MXU_DTYPE), (((ca,), (cb,)), ((), ())),
                           preferred_element_type=F32)


@jax.custom_vjp
def mm_nn(a, b):
    return _rawdot(a, b, 1, 0)


def _mm_nn_fwd(a, b):
    return _rawdot(a, b, 1, 0), (a, b)


def _mm_nn_bwd(res, dy):
    a, b = res
    return _rawdot(dy, b, 1, 1), _rawdot(a, dy, 0, 0)


mm_nn.defvjp(_mm_nn_fwd, _mm_nn_bwd)


@jax.custom_vjp
def mm_nt(a, b):
    return _rawdot(a, b, 1, 1)


def _mm_nt_fwd(a, b):
    return _rawdot(a, b, 1, 1), (a, b)


def _mm_nt_bwd(res, dy):
    a, b = res
    return _rawdot(dy, b, 1, 0), _rawdot(dy, a, 0, 0)


mm_nt.defvjp(_mm_nt_fwd, _mm_nt_bwd)


def _pick_tile(d, cands):
    for c in cands:
        if d % c == 0:
            return c
    return d


def matmul(a, b, mode, out_dtype, name):
    if mode == 'nn':
        (m, k), (k2, n) = a.shape, b.shape
    elif mode == 'nt':
        (m, k), (n, k2) = a.shape, b.shape
    else:
        (k, m), (k2, n) = a.shape, b.shape
    assert k == k2, (a.shape, b.shape, mode)
    tm = _pick_tile(m, (512, 256, 128))
    tn = _pick_tile(n, (512, 256, 128))
    tk = _pick_tile(k, (1024, 512, 256, 128))
    nk = k // tk
    ca = 0 if mode == 'tn' else 1
    cb = 1 if mode == 'nt' else 0
    a_spec = pl.BlockSpec((tk, tm), lambda i, j, kk: (kk, i)) if mode == 'tn' else pl.BlockSpec((tm, tk), lambda i, j, kk: (i, kk))
    b_spec = pl.BlockSpec((tn, tk), lambda i, j, kk: (j, kk)) if mode == 'nt' else pl.BlockSpec((tk, tn), lambda i, j, kk: (kk, j))

    def body(a_ref, b_ref, o_ref, acc_ref):
        kk = pl.program_id(2)

        @pl.when(kk == 0)
        def _():
            acc_ref[...] = jnp.zeros_like(acc_ref)

        acc_ref[...] += _rawdot(a_ref[...], b_ref[...], ca, cb)

        @pl.when(kk == nk - 1)
        def _():
            o_ref[...] = acc_ref[...].astype(o_ref.dtype)

    return pl.pallas_call(
        body, grid=(m // tm, n // tn, nk), in_specs=[a_spec, b_spec],
        out_specs=pl.BlockSpec((tm, tn), lambda i, j, kk: (i, j)),
        out_shape=jax.ShapeDtypeStruct((m, n), out_dtype),
        scratch_shapes=[pltpu.VMEM((tm, tn), F32)],
        compiler_params=pltpu.CompilerParams(dimension_semantics=("parallel", "parallel", "arbitrary"),
                                             vmem_limit_bytes=VMEM_LIMIT),
        name=name)(a, b)


class Op:
    def __init__(self, arr, block, imap, grad=False, acc=False, first=None, gdtype=F32, gshape=None, gimap=None):
        self.arr, self.block, self.imap = arr, block, imap
        self.grad, self.acc, self.first, self.gdtype = grad, acc, first, gdtype
        self.gshape = arr.shape if gshape is None else gshape
        self.gimap = imap if gimap is None else gimap


def _row_op(arr, tm, grad=False, gdtype=F32):
    return Op(arr, (tm, arr.shape[1]), lambda i: (i, 0), grad=grad, gdtype=gdtype)


def _par_op(arr, grad=False):
    nd = arr.ndim
    return Op(arr, arr.shape, lambda i: (0,) * nd, grad=grad, acc=True, first=lambda ids: ids[0] == 0)


def _load(ref):
    v = ref[...]
    return v.astype(F32) if jnp.issubdtype(v.dtype, jnp.floating) else v


def stage_fwd(name, fn, ops, outs, grid):
    n_in = len(ops)

    def body(*refs):
        res = fn(*[_load(r) for r in refs[:n_in]])
        for r, o in zip(refs[n_in:], res):
            r[...] = o.astype(r.dtype)

    return pl.pallas_call(
        body, grid=grid, in_specs=[pl.BlockSpec(o.block, o.imap) for o in ops],
        out_specs=[pl.BlockSpec(b, im) for (_, _, b, im) in outs],
        out_shape=[jax.ShapeDtypeStruct(s, d) for (s, d, _, _) in outs],
        compiler_params=pltpu.CompilerParams(dimension_semantics=("parallel",) * len(grid),
                                             vmem_limit_bytes=VMEM_LIMIT),
        name=name)(*[o.arr for o in ops])


def stage_bwd(name, fn, ops, cts, grid, value_acc=False):
    n_in = len(ops)
    ct_flat = [c for group in cts if group is not None for c in group]
    n_ct = len(ct_flat)
    diff = [i for i, o in enumerate(ops) if o.grad]
    any_acc = value_acc or any(ops[i].acc for i in diff)
    ngrid = len(grid)

    def body(*refs):
        ids = [pl.program_id(a) for a in range(ngrid)]
        vals = [_load(r) for r in refs[:n_in]]
        ct_refs = refs[n_in:n_in + n_ct]
        out_refs = refs[n_in + n_ct:]

        def f(*dv):
            full = list(vals)
            for i, v in zip(diff, dv):
                full[i] = v
            return tuple(fn(*full))

        res, vjp = jax.vjp(f, *[vals[i] for i in diff])
        ct, pos = [], 0
        for group, r in zip(cts, res):
            if group is None:
                ct.append(jnp.ones_like(r))
            else:
                tot = None
                for _ in group:
                    c = ct_refs[pos][...].astype(F32)
                    tot = c if tot is None else tot + c
                    pos += 1
                ct.append(tot)
        grads = vjp(tuple(ct))
        for i, g, r in zip(diff, grads, out_refs):
            if ops[i].acc:
                @pl.when(ops[i].first(ids))
                def _(r=r):
                    r[...] = jnp.zeros_like(r)

                r[...] += g.astype(r.dtype)
            else:
                r[...] = g.astype(r.dtype)
        if value_acc:
            r = out_refs[len(diff)]

            @pl.when(ids[0] == 0)
            def _():
                r[...] = jnp.zeros_like(r)

            r[...] += res[0]

    in_specs = [pl.BlockSpec(o.block, o.imap) for o in ops] + [pl.BlockSpec(b, im) for (_, b, im) in ct_flat]
    out_specs = [pl.BlockSpec(ops[i].block, ops[i].gimap) for i in diff]
    out_shape = [jax.ShapeDtypeStruct(ops[i].gshape, ops[i].gdtype) for i in diff]
    if value_acc:
        out_specs.append(pl.BlockSpec((1, 1), lambda *ids: (0, 0)))
        out_shape.append(jax.ShapeDtypeStruct((1, 1), F32))
    sem = ("arbitrary",) * ngrid if any_acc else ("parallel",) * ngrid
    return pl.pallas_call(
        body, grid=grid, in_specs=in_specs, out_specs=out_specs, out_shape=out_shape,
        compiler_params=pltpu.CompilerParams(dimension_semantics=sem, vmem_limit_bytes=VMEM_LIMIT),
        name=name)(*[o.arr for o in ops], *[a for (a, _, _) in ct_flat])


def _sigmoid(x):
    return 1.0 / (1.0 + jnp.exp(-x))


def _silu(x):
    return x * _sigmoid(x)


def _gelu_tanh(x):
    return 0.5 * x * (1.0 + jnp.tanh(math.sqrt(2.0 / math.pi) * (x + 0.044715 * (x * x * x))))


def _ln(x, g, b):
    mu = jnp.mean(x, axis=-1, keepdims=True)
    xc = x - mu
    var = jnp.mean(xc * xc, axis=-1, keepdims=True)
    return xc * lax.rsqrt(var + LN_EPS) * g + b


def _rms(x, g):
    ms = jnp.mean(x * x, axis=-1, keepdims=True)
    return x * lax.rsqrt(ms + RMS_EPS) * g


def _rope1(x, c, s):
    half = x.shape[-1] // 2
    x1, x2 = x[:, :half], x[:, half:]
    return jnp.concatenate([x1 * c - x2 * s, x2 * c + x1 * s], axis=-1)


def _rope2(x, tab):
    h = x.shape[-1] // 2
    q = h // 2
    return jnp.concatenate([_rope1(x[:, :h], tab[:, 0:q], tab[:, q:2 * q]),
                            _rope1(x[:, h:], tab[:, 2 * q:3 * q], tab[:, 3 * q:4 * q])], axis=-1)


def fn_ln(x, g, b):
    return (_ln(x, g, b),)


def fn_pre(proj, tab_b, tab_d, qng, kng, sg, sb, sw, sbt, mqn, wuq, mkvn, wukv):
    tm = proj.shape[0]
    a_in, b_q, b_k, b_v = proj[:, 0:512], proj[:, 512:768], proj[:, 768:896], proj[:, 896:1024]
    c_in, d_cq, d_ckv, d_kr = proj[:, 1024:1536], proj[:, 1536:1728], proj[:, 1728:1856], proj[:, 1856:1888]
    aglu = a_in[:, :GROUP_W] * _sigmoid(a_in[:, GROUP_W:])
    q = jnp.concatenate([_rope2(_rms(b_q[:, 64 * h:64 * h + 64], qng), tab_b) for h in range(GQA_HEADS)], axis=-1)
    k = jnp.concatenate([_rope2(_rms(b_k[:, 64 * h:64 * h + 64], kng), tab_b) for h in range(GQA_KV_HEADS)], axis=-1)
    c = _gelu_tanh(c_in)
    u, sv = c[:, :GROUP_W], _ln(c[:, GROUP_W:], sg, sb)
    rows = []
    for n in range(tm // CHUNK):
        svn = sv[CHUNK * n:CHUNK * (n + 1)]
        rows.append(jnp.concatenate(
            [mm_nn(sw[CHUNK * g:CHUNK * (g + 1)], svn[:, 64 * g:64 * g + 64]) + sbt[:, g:g + 1]
             for g in range(SGU_GROUPS)], axis=-1))
    o_c = u * jnp.concatenate(rows, axis=0)
    qd = mm_nn(_rms(d_cq, mqn), wuq)
    kvd = mm_nn(_rms(d_ckv, mkvn), wukv)
    k_rope = _rope2(d_kr, tab_d)
    zpad = jnp.zeros((tm, MLA_DK_PAD - MLA_NOPE - MLA_ROPE), F32)
    qf, kf, vd = [], [], []
    for h in range(MLA_HEADS):
        qh = qd[:, 96 * h:96 * h + 96]
        qf += [qh[:, :MLA_NOPE], _rope2(qh[:, MLA_NOPE:], tab_d), zpad]
        kf += [kvd[:, 128 * h:128 * h + MLA_NOPE], k_rope, zpad]
        vd.append(kvd[:, 128 * h + MLA_NOPE:128 * h + 128])
    return (aglu, q, k, b_v, o_c, jnp.concatenate(qf, axis=-1), jnp.concatenate(kf, axis=-1),
            jnp.concatenate(vd, axis=-1))


def fn_aconv(win, w, b, g, beta):
    tm = win.shape[0] - 2 * CONV_A_HALO
    off = CONV_A_HALO - CONV_A_WIDTH // 2
    acc = None
    for kk in range(CONV_A_WIDTH):
        term = win[off + kk:off + kk + tm] * w[kk:kk + 1, :]
        acc = term if acc is None else acc + term
    return (_silu(_ln(acc + b, g, beta)),)


def fn_attn(scale, q, k, v):
    s = mm_nt(q, k) * scale
    m = jnp.max(s, axis=-1, keepdims=True)
    e = jnp.exp(s - m)
    p = e / jnp.sum(e, axis=-1, keepdims=True)
    return (mm_nn(p, v),)


def fn_resln(h, r, g, b):
    return (_ln(DEEPNORM_ALPHA * h + r, g, b),)


def _shift_down(x):
    return jnp.concatenate([jnp.zeros((8, x.shape[1]), F32), x], axis=0)[7:7 + x.shape[0]]


def _shift_up(x):
    return jnp.concatenate([x, jnp.zeros((8, x.shape[1]), F32)], axis=0)[1:1 + x.shape[0]]


def fn_ffnconv(u1, u2, w1, w2, b1, b2):
    c1 = _shift_down(u1) * w1[0:1] + u1 * w1[1:2] + _shift_up(u1) * w1[2:3] + b1
    c2 = _shift_down(u2) * w2[0:1] + u2 * w2[1:2] + _shift_up(u2) * w2[2:3] + b2
    return (_silu(c1) * c2,)


def fn_final(h, r, t, g, b):
    y = _ln(DEEPNORM_ALPHA * h + r, g, b)
    err = (y - t) * (y - t)
    return (0.5 * jnp.sum(jnp.mean(err, axis=-1, keepdims=True), axis=0, keepdims=True),)


def _rope_tables(seq):
    pos = jnp.arange(seq, dtype=jnp.int32)
    row = (pos // GRID_W).astype(F32)
    col = (pos % GRID_W).astype(F32)

    def tab(half):
        inv = ROPE_THETA ** (-jnp.arange(half, dtype=F32) / half)
        ar, ac = row[:, None] * inv[None, :], col[:, None] * inv[None, :]
        return jnp.concatenate([jnp.cos(ar), jnp.sin(ar), jnp.cos(ac), jnp.sin(ac)], axis=-1)

    return tab(HEAD_DIM // 4), tab(MLA_ROPE // 4)


def _pre_ops(proj, tabs, kp, grad):
    tm = ROW_TILE
    ops = [_row_op(proj, tm, grad=grad, gdtype=MXU_DTYPE), _row_op(tabs[0], tm), _row_op(tabs[1], tm)]
    ops += [_par_op(kp[n], grad=grad) for n in ('qng', 'kng', 'sg', 'sb', 'sw', 'sbt', 'mqn', 'wuq', 'mkvn', 'wukv')]
    return ops


PRE_OUT_WIDTHS = (GROUP_W, GROUP_W, 128, 128, GROUP_W, MLA_HEADS * MLA_DK_PAD, MLA_HEADS * MLA_DK_PAD, GROUP_W)


def pre_fwd(proj, tabs, kp, tag):
    seq = proj.shape[0]
    tm = ROW_TILE
    dts = (F32,) + (MXU_DTYPE,) * 7
    outs = [((seq, w), dt, (tm, w), lambda i: (i, 0)) for w, dt in zip(PRE_OUT_WIDTHS, dts)]
    return stage_fwd("pre_fwd" + tag, fn_pre, _pre_ops(proj, tabs, kp, False), outs, (seq // tm,))


def pre_bwd(proj, tabs, kp, cts, tag):
    seq = proj.shape[0]
    tm = ROW_TILE
    ct = [[(c, (tm, c.shape[1]), lambda i: (i, 0))] for c in cts]
    return stage_bwd("pre_bwd" + tag, fn_pre, _pre_ops(proj, tabs, kp, True), ct, (seq // tm,))


def _aconv_ops(kp, grad):
    return [_par_op(kp[n], grad=grad) for n in ('caw', 'cab', 'lag', 'lab')]


def aconv_fwd(aglu_pad, kp, tag):
    seq = aglu_pad.shape[0] - 2 * CONV_A_HALO
    tm = ROW_TILE
    n_par = 4

    def body(x_ref, *refs):
        i = pl.program_id(0)
        win = x_ref[pl.ds(pl.multiple_of(i * tm, tm), tm + 2 * CONV_A_HALO), :]
        (o,) = fn_aconv(win, *[_load(r) for r in refs[:n_par]])
        refs[n_par][...] = o.astype(refs[n_par].dtype)

    pars = _aconv_ops(kp, False)
    return pl.pallas_call(
        body, grid=(seq // tm,),
        in_specs=[pl.BlockSpec(aglu_pad.shape, lambda i: (0, 0))] + [pl.BlockSpec(o.block, o.imap) for o in pars],
        out_specs=pl.BlockSpec((tm, GROUP_W), lambda i: (i, 0)),
        out_shape=jax.ShapeDtypeStruct((seq, GROUP_W), MXU_DTYPE),
        compiler_params=pltpu.CompilerParams(dimension_semantics=("parallel",), vmem_limit_bytes=VMEM_LIMIT),
        name="aconv_fwd" + tag)(aglu_pad, *[o.arr for o in pars])


def aconv_bwd(aglu_pad, kp, d_oa, tag):
    seq = aglu_pad.shape[0] - 2 * CONV_A_HALO
    tm = ROW_TILE
    n_par = 4

    def body(x_ref, *refs):
        i = pl.program_id(0)
        rows = pl.ds(pl.multiple_of(i * tm, tm), tm + 2 * CONV_A_HALO)
        pars = [_load(r) for r in refs[:n_par]]
        ct = refs[n_par][...].astype(F32)
        outs = refs[n_par + 1:]
        _, vjp = jax.vjp(lambda *a: fn_aconv(*a), x_ref[rows, :], *pars)
        grads = vjp((ct,))

        @pl.when(i == 0)
        def _():
            for r in outs:
                r[...] = jnp.zeros_like(r)

        outs[0][rows, :] += grads[0]
        for r, g in zip(outs[1:], grads[1:]):
            r[...] += g

    pars = _aconv_ops(kp, True)
    whole = pl.BlockSpec(aglu_pad.shape, lambda i: (0, 0))
    par_specs = [pl.BlockSpec(o.block, o.imap) for o in pars]
    return pl.pallas_call(
        body, grid=(seq // tm,),
        in_specs=[whole] + par_specs + [pl.BlockSpec((tm, GROUP_W), lambda i: (i, 0))],
        out_specs=[whole] + par_specs,
        out_shape=[jax.ShapeDtypeStruct(aglu_pad.shape, F32)] + [jax.ShapeDtypeStruct(o.arr.shape, F32) for o in pars],
        compiler_params=pltpu.CompilerParams(dimension_semantics=("arbitrary",), vmem_limit_bytes=VMEM_LIMIT),
        name="aconv_bwd" + tag)(aglu_pad, *[o.arr for o in pars], d_oa)


def _attn_ops(q3, k3, v3, group, tq, grad):
    dk, dv = q3.shape[2], v3.shape[2]
    seq = q3.shape[1]
    first = lambda ids: jnp.logical_and(ids[0] % group == 0, ids[1] == 0)
    return [Op(q3, (None, tq, dk), lambda h, i: (h, i, 0), grad=grad),
            Op(k3, (None, seq, dk), lambda h, i: (h // group, 0, 0), grad=grad, acc=True, first=first),
            Op(v3, (None, seq, dv), lambda h, i: (h // group, 0, 0), grad=grad, acc=True, first=first)]


def attn_fwd(q3, k3, v3, scale, tag):
    heads, seq, _ = q3.shape
    group = heads // k3.shape[0]
    tq = ROW_TILE
    dv = v3.shape[2]
    outs = [((heads, seq, dv), MXU_DTYPE, (None, tq, dv), lambda h, i: (h, i, 0))]
    return stage_fwd("attn_fwd" + tag, functools.partial(fn_attn, scale), _attn_ops(q3, k3, v3, group, tq, False),
                     outs, (heads, seq // tq))[0]


def attn_bwd(q3, k3, v3, do3, scale, tag):
    heads, seq, _ = q3.shape
    group = heads // k3.shape[0]
    tq = ROW_TILE
    dv = v3.shape[2]
    ct = [[(do3, (None, tq, dv), lambda h, i: (h, i, 0))]]
    return stage_bwd("attn_bwd" + tag, functools.partial(fn_attn, scale), _attn_ops(q3, k3, v3, group, tq, True),
                     ct, (heads, seq // tq))


def resln_fwd(h, r, g, b, tag):
    seq, d = h.shape
    tm = ROW_TILE
    ops = [_row_op(h, tm), _row_op(r, tm), _par_op(g), _par_op(b)]
    outs = [((seq, d), F32, (tm, d), lambda i: (i, 0))]
    return stage_fwd("resln_fwd" + tag, fn_resln, ops, outs, (seq // tm,))[0]


def resln_bwd(h, r, g, b, dys, tag):
    seq, d = h.shape
    tm = ROW_TILE
    ops = [_row_op(h, tm, grad=True), _row_op(r, tm, grad=True, gdtype=MXU_DTYPE), _par_op(g, grad=True),
           _par_op(b, grad=True)]
    ct = [[(dy, (tm, d), lambda i: (i, 0)) for dy in dys]]
    return stage_bwd("resln_bwd" + tag, fn_resln, ops, ct, (seq // tm,))


def _ffnconv_ops(up, w, b, grad):
    seq = up.shape[0]
    nblk = D_FF // 128
    lo, hi = (lambda j: (0, j)), (lambda j: (0, j + nblk))
    half = lambda a: dict(gshape=(a.shape[0], D_FF), gimap=lo)
    return [Op(up, (seq, 128), lo, grad=grad, gdtype=MXU_DTYPE, **half(up)),
            Op(up, (seq, 128), hi, grad=grad, gdtype=MXU_DTYPE, **half(up)),
            Op(w, (3, 128), lo, grad=grad, **half(w)), Op(w, (3, 128), hi, grad=grad, **half(w)),
            Op(b, (1, 128), lo, grad=grad, **half(b)), Op(b, (1, 128), hi, grad=grad, **half(b))]


def ffnconv_fwd(up, w, b, tag):
    seq = up.shape[0]
    outs = [((seq, D_FF), MXU_DTYPE, (seq, 128), lambda j: (0, j))]
    return stage_fwd("ffnconv_fwd" + tag, fn_ffnconv, _ffnconv_ops(up, w, b, False), outs, (D_FF // 128,))[0]


def ffnconv_bwd(up, w, b, dact, tag):
    seq = up.shape[0]
    ct = [[(dact, (seq, 128), lambda j: (0, j))]]
    du1, du2, dw1, dw2, db1, db2 = stage_bwd("ffnconv_bwd" + tag, fn_ffnconv, _ffnconv_ops(up, w, b, True), ct,
                                             (D_FF // 128,))
    cat = lambda a, b_: jnp.concatenate([a, b_], axis=-1)
    return cat(du1, du2), cat(dw1, dw2), cat(db1, db2)


def final_bwd(h, r, t, g, b, tag):
    seq, d = h.shape
    tm = ROW_TILE
    ops = [_row_op(h, tm, grad=True), _row_op(r, tm, grad=True, gdtype=MXU_DTYPE), _row_op(t, tm),
           _par_op(g, grad=True), _par_op(b, grad=True)]
    return stage_bwd("final_bwd" + tag, fn_final, ops, [None], (seq // tm,), value_acc=True)


def _layer_params(wts, l):
    row = lambda a: a.reshape(1, -1)
    return dict(
        qng=row(wts['qk_norm_q'][l]), kng=row(wts['qk_norm_k'][l]),
        sg=row(wts['sgu_ln_g'][l]), sb=row(wts['sgu_ln_b'][l]),
        sw=wts['sgu_w'][l].reshape(SGU_GROUPS * CHUNK, CHUNK), sbt=wts['sgu_b'][l].T,
        mqn=row(wts['mla_q_norm'][l]), wuq=wts['mla_w_uq'][l], mkvn=row(wts['mla_kv_norm'][l]),
        wukv=wts['mla_w_ukv'][l],
        caw=wts['conv_a_w'][l], cab=row(wts['conv_a_b'][l]), lag=row(wts['ln_a_g'][l]), lab=row(wts['ln_a_b'][l]),
        lmg=row(wts['ln_mix_g'][l]), lmb=row(wts['ln_mix_b'][l]),
        fcw=wts['ffn_conv_w'][l], fcb=row(wts['ffn_conv_b'][l]),
        lfg=row(wts['ln_ffn_g'][l]), lfb=row(wts['ln_ffn_b'][l]))


def _to_heads(a, heads):
    seq = a.shape[0]
    return a.reshape(seq, heads, -1).transpose(1, 0, 2)


def _from_heads(a3):
    return a3.transpose(1, 0, 2).reshape(a3.shape[1], -1)


def local_step(x, target, wts, mats):
    seq = x.shape[0]
    tm = ROW_TILE
    tabs = _rope_tables(seq)
    scale_b = HEAD_DIM ** -0.5
    scale_d = (MLA_NOPE + MLA_ROPE) ** -0.5
    ln_in_g, ln_in_b = wts['ln_in_g'].reshape(1, -1), wts['ln_in_b'].reshape(1, -1)

    h = stage_fwd("ln_in_fwd", fn_ln, [_row_op(x, tm), _par_op(ln_in_g), _par_op(ln_in_b)],
                  [((seq, D_MODEL), F32, (tm, D_MODEL), lambda i: (i, 0))], (seq // tm,))[0]
    saved = []
    for l in range(DEPTH):
        tag = f"_l{l}"
        kp, unprep = jax.vjp(lambda w: _layer_params(w, l), wts)
        m = mats[l]
        proj = matmul(h, m['w_in'], 'nn', F32, "mm_proj" + tag)
        aglu, q, k, v, o_c, qf, kf, vd = pre_fwd(proj, tabs, kp, tag)
        aglu_pad = jnp.pad(aglu, ((CONV_A_HALO, CONV_A_HALO), (0, 0)))
        o_a = aconv_fwd(aglu_pad, kp, tag)
        q3, k3, v3 = _to_heads(q, GQA_HEADS), _to_heads(k, GQA_KV_HEADS), _to_heads(v, GQA_KV_HEADS)
        o_b3 = attn_fwd(q3, k3, v3, scale_b, "_b" + tag)
        qd3, kd3, vd3 = _to_heads(qf, MLA_HEADS), _to_heads(kf, MLA_HEADS), _to_heads(vd, MLA_HEADS)
        o_d3 = attn_fwd(qd3, kd3, vd3, scale_d, "_d" + tag)
        o_cat = jnp.concatenate([o_a, _from_heads(o_b3), o_c, _from_heads(o_d3)], axis=-1)
        mix = matmul(o_cat, m['w_out'], 'nn', F32, "mm_mix" + tag)
        h1 = resln_fwd(h, mix, kp['lmg'], kp['lmb'], "_mix" + tag)
        up = matmul(h1, m['ffn_w_up'], 'nn', F32, "mm_up" + tag)
        act = ffnconv_fwd(up, kp['fcw'], kp['fcb'], tag)
        f = matmul(act, m['ffn_w_down'], 'nn', F32, "mm_down" + tag)
        saved.append(dict(kp=kp, unprep=unprep, h=h, proj=proj, aglu_pad=aglu_pad, q3=q3, k3=k3, v3=v3, qd3=qd3,
                          kd3=kd3, vd3=vd3, o_cat=o_cat, mix=mix, h1=h1, up=up, act=act, f=f))
        if l + 1 < DEPTH:
            h = resln_fwd(h1, f, kp['lfg'], kp['lfb'], "_ffn" + tag)

    grads = {}
    small_acc = None
    dh_parts = None
    loss = None
    for l in reversed(range(DEPTH)):
        tag = f"_l{l}"
        s = saved[l]
        kp, m = s['kp'], mats[l]
        dkp = {}
        if l == DEPTH - 1:
            dh1_a, df, dkp['lfg'], dkp['lfb'], loss = final_bwd(s['h1'], s['f'], target, kp['lfg'], kp['lfb'], tag)
        else:
            dh1_a, df, dkp['lfg'], dkp['lfb'] = resln_bwd(s['h1'], s['f'], kp['lfg'], kp['lfb'], dh_parts, "_ffn" + tag)
        dact = matmul(df, m['ffn_w_down'], 'nt', F32, "mm_dact" + tag)
        g_down = matmul(s['act'], df, 'tn', F32, "mm_gdown" + tag)
        dup, dkp['fcw'], dkp['fcb'] = ffnconv_bwd(s['up'], kp['fcw'], kp['fcb'], dact, tag)
        dh1_b = matmul(dup, m['ffn_w_up'], 'nt', F32, "mm_dh1" + tag)
        g_up = matmul(s['h1'], dup, 'tn', F32, "mm_gup" + tag)
        dh_a, dmix, dkp['lmg'], dkp['lmb'] = resln_bwd(s['h'], s['mix'], kp['lmg'], kp['lmb'], [dh1_a, dh1_b],
                                                       "_mix" + tag)
        do_cat = matmul(dmix, m['w_out'], 'nt', F32, "mm_docat" + tag)
        g_out = matmul(s['o_cat'], dmix, 'tn', F32, "mm_gout" + tag)
        do_a, do_b, do_c, do_d = (do_cat[:, GROUP_W * j:GROUP_W * (j + 1)] for j in range(4))
        dq3, dk3, dv3 = attn_bwd(s['q3'], s['k3'], s['v3'], _to_heads(do_b, GQA_HEADS), scale_b, "_b" + tag)
        dqd3, dkd3, dvd3 = attn_bwd(s['qd3'], s['kd3'], s['vd3'], _to_heads(do_d, MLA_HEADS), scale_d, "_d" + tag)
        daglu_pad, dkp['caw'], dkp['cab'], dkp['lag'], dkp['lab'] = aconv_bwd(s['aglu_pad'], kp, do_a, tag)
        cts = [daglu_pad[CONV_A_HALO:CONV_A_HALO + seq], _from_heads(dq3), _from_heads(dk3), _from_heads(dv3), do_c,
               _from_heads(dqd3), _from_heads(dkd3), _from_heads(dvd3)]
        pre_g = pre_bwd(s['proj'], tabs, kp, cts, tag)
        dproj = pre_g[0]
        for n, g in zip(('qng', 'kng', 'sg', 'sb', 'sw', 'sbt', 'mqn', 'wuq', 'mkvn', 'wukv'), pre_g[1:]):
            dkp[n] = g
        dh_b = matmul(dproj, m['w_in'], 'nt', F32, "mm_dh" + tag)
        g_in = matmul(s['h'], dproj, 'tn', F32, "mm_gin" + tag)
        dh_parts = [dh_a, dh_b]
        (dw,) = s['unprep'](dkp)
        small_acc = dw if small_acc is None else jax.tree.map(jnp.add, small_acc, dw)
        grads[l] = dict(w_in=g_in, w_out=g_out, ffn_w_up=g_up, ffn_w_down=g_down)

    dx, dg, db = stage_bwd("ln_in_bwd", fn_ln,
                           [_row_op(x, tm, grad=True), _par_op(ln_in_g, grad=True), _par_op(ln_in_b, grad=True)],
                           [[(p, (tm, D_MODEL), lambda i: (i, 0)) for p in dh_parts]], (seq // tm,))
    out = dict(small_acc)
    out['ln_in_g'], out['ln_in_b'] = dg.reshape(-1), db.reshape(-1)
    for n in MATMUL_WEIGHTS:
        out[n] = jnp.stack([grads[l][n] for l in range(DEPTH)], axis=0)
    return loss, dx, out


def _peer(x, y, c, r):
    return ((1 - x) if r & 4 else x, (1 - y) if r & 2 else y, (1 - c) if r & 1 else c)


def exchange(arrs, gather, name):
    n_t = len(arrs)
    hbm = pl.BlockSpec(memory_space=pltpu.HBM)

    def body(*refs):
        ins, outs = refs[:n_t], refs[n_t:2 * n_t]
        send_sems, recv_sems, local_sems = refs[2 * n_t:]
        x, y, c = lax.axis_index("x"), lax.axis_index("y"), lax.axis_index("c")
        me = 4 * x + 2 * y + c

        def remote(t, r):
            px, py, pc = _peer(x, y, c, r)
            peer = 4 * px + 2 * py + pc
            src = ins[t] if gather else ins[t].at[peer]
            send = pltpu.make_async_remote_copy(src_ref=src, dst_ref=outs[t].at[me], send_sem=send_sems.at[t, r - 1],
                                                recv_sem=recv_sems.at[t, r - 1], device_id=(px, py, pc),
                                                device_id_type=pl.DeviceIdType.MESH)
            recv = pltpu.make_async_remote_copy(src_ref=src, dst_ref=outs[t].at[peer], send_sem=send_sems.at[t, r - 1],
                                                recv_sem=recv_sems.at[t, r - 1], device_id=(px, py, pc),
                                                device_id_type=pl.DeviceIdType.MESH)
            return send, recv

        locs = []
        for t in range(n_t):
            loc = pltpu.make_async_copy(ins[t] if gather else ins[t].at[me], outs[t].at[me], local_sems.at[t])
            loc.start()
            locs.append(loc)
        for t in range(n_t):
            for r in range(1, N_DEV):
                remote(t, r)[0].start()
        for t in range(n_t):
            for r in range(1, N_DEV):
                send, recv = remote(t, r)
                recv.wait_recv()
                send.wait_send()
        for loc in locs:
            loc.wait()

    out_shape = [jax.ShapeDtypeStruct(((N_DEV,) + a.shape) if gather else a.shape, a.dtype) for a in arrs]
    return pl.pallas_call(
        body, in_specs=[hbm] * n_t, out_specs=[hbm] * n_t, out_shape=out_shape,
        scratch_shapes=[pltpu.SemaphoreType.DMA((n_t, N_DEV - 1)), pltpu.SemaphoreType.DMA((n_t, N_DEV - 1)),
                        pltpu.SemaphoreType.DMA((n_t,))],
        name=name)(*arrs)


def adamw(parts, w, m, v, name):
    n_l, n_r, n_c = w.shape
    tr = n_r
    if n_r % 8 == 0:
        for cand in (512, 256, 128, 64, 32, 16, 8):
            if n_r % cand == 0 and cand * n_c * 4 <= 512 * 1024:
                tr = cand
                break
    c1 = 1.0 - ADAM_B1 ** ADAM_STEP
    c2 = 1.0 - ADAM_B2 ** ADAM_STEP

    def body(p_ref, w_ref, m_ref, v_ref, g_ref, d_ref, nm_ref, nv_ref):
        g = p_ref[0, 0].astype(F32)
        for s in range(1, N_DEV):
            g = g + p_ref[s, 0].astype(F32)
        w_, m_, v_ = w_ref[0], m_ref[0], v_ref[0]
        nm = ADAM_B1 * m_ + (1.0 - ADAM_B1) * g
        nv = ADAM_B2 * v_ + (1.0 - ADAM_B2) * (g * g)
        g_ref[0] = g
        nm_ref[0] = nm
        nv_ref[0] = nv
        d_ref[0] = -ADAM_LR * ((nm / c1) / (jnp.sqrt(nv / c2) + ADAM_EPS) + ADAM_WD * w_)

    blk = pl.BlockSpec((1, tr, n_c), lambda l, r: (l, r, 0))
    return pl.pallas_call(
        body, grid=(n_l, n_r // tr),
        in_specs=[pl.BlockSpec((N_DEV, 1, tr, n_c), lambda l, r: (0, l, r, 0)), blk, blk, blk],
        out_specs=[blk] * 4, out_shape=[jax.ShapeDtypeStruct(w.shape, F32)] * 4,
        compiler_params=pltpu.CompilerParams(dimension_semantics=("parallel", "parallel"),
                                             vmem_limit_bytes=VMEM_LIMIT),
        name=name)(parts, w, m, v)


PACK_UNIT = 1024
PACK_ROWS = 256


def _pack(pieces):
    rows = []
    for p in pieces:
        flat = p.reshape(-1).astype(F32)
        pad = (-flat.shape[0]) % PACK_UNIT
        rows.append(jnp.pad(flat, (0, pad)).reshape(-1, 128))
    n = sum(r.shape[0] for r in rows)
    rows.append(jnp.zeros(((-n) % PACK_ROWS, 128), F32))
    return jnp.concatenate(rows, axis=0)


def _unpack(packed, shapes):
    out, r0 = [], 0
    for shp in shapes:
        n = math.prod(shp)
        nr = -(-n // PACK_UNIT) * (PACK_UNIT // 128)
        out.append(packed[r0:r0 + nr].reshape(-1)[:n].reshape(shp))
        r0 += nr
    return out


def _shard_slots(g, axis):
    if axis == 1:
        return g.reshape(g.shape[0], N_DEV, g.shape[1] // N_DEV, g.shape[2]).transpose(1, 0, 2, 3)
    return g.reshape(g.shape[0], g.shape[1], N_DEV, g.shape[2] // N_DEV).transpose(2, 0, 1, 3)


def _unshard(slots, axis):
    if axis == 1:
        return slots.transpose(1, 0, 2, 3).reshape(slots.shape[1], -1, slots.shape[3])
    return slots.transpose(1, 2, 0, 3).reshape(slots.shape[1], slots.shape[2], -1)


def kernel(x, ln_in_g, ln_in_b, w_in, conv_a_w, conv_a_b, ln_a_g, ln_a_b, qk_norm_q, qk_norm_k, sgu_ln_g, sgu_ln_b, sgu_w, sgu_b, mla_q_norm, mla_w_uq, mla_kv_norm, mla_w_ukv, w_out, ln_mix_g, ln_mix_b, ffn_w_up, ffn_conv_w, ffn_conv_b, ffn_w_down, ln_ffn_g, ln_ffn_b, loss_target, m_ln_in_g, m_ln_in_b, m_w_in, m_conv_a_w, m_conv_a_b, m_ln_a_g, m_ln_a_b, m_qk_norm_q, m_qk_norm_k, m_sgu_ln_g, m_sgu_ln_b, m_sgu_w, m_sgu_b, m_mla_q_norm, m_mla_w_uq, m_mla_kv_norm, m_mla_w_ukv, m_w_out, m_ln_mix_g, m_ln_mix_b, m_ffn_w_up, m_ffn_conv_w, m_ffn_conv_b, m_ffn_w_down, m_ln_ffn_g, m_ln_ffn_b, v_ln_in_g, v_ln_in_b, v_w_in, v_conv_a_w, v_conv_a_b, v_ln_a_g, v_ln_a_b, v_qk_norm_q, v_qk_norm_k, v_sgu_ln_g, v_sgu_ln_b, v_sgu_w, v_sgu_b, v_mla_q_norm, v_mla_w_uq, v_mla_kv_norm, v_mla_w_ukv, v_w_out, v_ln_mix_g, v_ln_mix_b, v_ffn_w_up, v_ffn_conv_w, v_ffn_conv_b, v_ffn_w_down, v_ln_ffn_g, v_ln_ffn_b):
    local = dict(ln_in_g=ln_in_g, ln_in_b=ln_in_b, w_in=w_in, conv_a_w=conv_a_w, conv_a_b=conv_a_b, ln_a_g=ln_a_g, ln_a_b=ln_a_b, qk_norm_q=qk_norm_q, qk_norm_k=qk_norm_k, sgu_ln_g=sgu_ln_g, sgu_ln_b=sgu_ln_b, sgu_w=sgu_w, sgu_b=sgu_b, mla_q_norm=mla_q_norm, mla_w_uq=mla_w_uq, mla_kv_norm=mla_kv_norm, mla_w_ukv=mla_w_ukv, w_out=w_out, ln_mix_g=ln_mix_g, ln_mix_b=ln_mix_b, ffn_w_up=ffn_w_up, ffn_conv_w=ffn_conv_w, ffn_conv_b=ffn_conv_b, ffn_w_down=ffn_w_down, ln_ffn_g=ln_ffn_g, ln_ffn_b=ln_ffn_b)
    mom = dict(ln_in_g=m_ln_in_g, ln_in_b=m_ln_in_b, w_in=m_w_in, conv_a_w=m_conv_a_w, conv_a_b=m_conv_a_b, ln_a_g=m_ln_a_g, ln_a_b=m_ln_a_b, qk_norm_q=m_qk_norm_q, qk_norm_k=m_qk_norm_k, sgu_ln_g=m_sgu_ln_g, sgu_ln_b=m_sgu_ln_b, sgu_w=m_sgu_w, sgu_b=m_sgu_b, mla_q_norm=m_mla_q_norm, mla_w_uq=m_mla_w_uq, mla_kv_norm=m_mla_kv_norm, mla_w_ukv=m_mla_w_ukv, w_out=m_w_out, ln_mix_g=m_ln_mix_g, ln_mix_b=m_ln_mix_b, ffn_w_up=m_ffn_w_up, ffn_conv_w=m_ffn_conv_w, ffn_conv_b=m_ffn_conv_b, ffn_w_down=m_ffn_w_down, ln_ffn_g=m_ln_ffn_g, ln_ffn_b=m_ln_ffn_b)
    var = dict(ln_in_g=v_ln_in_g, ln_in_b=v_ln_in_b, w_in=v_w_in, conv_a_w=v_conv_a_w, conv_a_b=v_conv_a_b, ln_a_g=v_ln_a_g, ln_a_b=v_ln_a_b, qk_norm_q=v_qk_norm_q, qk_norm_k=v_qk_norm_k, sgu_ln_g=v_sgu_ln_g, sgu_ln_b=v_sgu_ln_b, sgu_w=v_sgu_w, sgu_b=v_sgu_b, mla_q_norm=v_mla_q_norm, mla_w_uq=v_mla_w_uq, mla_kv_norm=v_mla_kv_norm, mla_w_ukv=v_mla_w_ukv, w_out=v_w_out, ln_mix_g=v_ln_mix_g, ln_mix_b=v_ln_mix_b, ffn_w_up=v_ffn_w_up, ffn_conv_w=v_ffn_conv_w, ffn_conv_b=v_ffn_conv_b, ffn_w_down=v_ffn_w_down, ln_ffn_g=v_ln_ffn_g, ln_ffn_b=v_ln_ffn_b)

    names = list(SHARDED)
    send = [local[n].astype(COMM_DTYPE) if n in MATMUL_WEIGHTS else local[n] for n in names]
    gathered = exchange(send, True, "gather_weights")
    full = {n: _unshard(g, SHARDED[n]) for n, g in zip(names, gathered)}
    wts = {n: (full[n] if n in SHARDED else local[n]) for n in WEIGHTS if n not in MATMUL_WEIGHTS}
    mats = [{n: full[n][l].astype(MXU_DTYPE) for n in MATMUL_WEIGHTS} for l in range(DEPTH)]

    loss, dx, grads = local_step(x[0], loss_target[0], wts, mats)

    parts = exchange([_shard_slots(grads[n], SHARDED[n]).astype(COMM_DTYPE if n in MATMUL_WEIGHTS else F32)
                      for n in names], False, "scatter_grads")
    res = {}
    for n, p in zip(names, parts):
        res[n] = adamw(p, local[n], mom[n], var[n], "adamw_" + n)

    pieces = [grads[n] for n in REPLICATED] + [loss]
    packed = _pack(pieces)
    (gath,) = exchange([packed], True, "gather_small")
    zeros1 = jnp.zeros((1, 1), F32)
    pk = lambda d: _pack([d[n] for n in REPLICATED] + [zeros1])[None]
    g_s, d_s, m_s, v_s = adamw(gath[:, None], pk(local), pk(mom), pk(var), "adamw_small")
    shapes = [local[n].shape for n in REPLICATED] + [(1, 1)]
    unp = [_unpack(a[0], shapes) for a in (g_s, d_s, m_s, v_s)]
    for j, n in enumerate(REPLICATED):
        res[n] = tuple(u[j] for u in unp)
    loss_total = unp[0][-1].reshape(())

    return (loss_total, dx[None], *[res[n][0] for n in WEIGHTS], *[res[n][1] for n in WEIGHTS],
            *[res[n][2] for n in WEIGHTS], *[res[n][3] for n in WEIGHTS])
```

```python
import functools
import math

import jax
import jax.numpy as jnp
from jax import lax
from jax.experimental import pallas as pl
from jax.experimental.pallas import tpu as pltpu

F32 = jnp.float32
MXU_DTYPE = jnp.bfloat16
COMM_DTYPE = jnp.bfloat16

N_DEV = 8
D_MODEL = 1024
DEPTH = 2
GRID_W = 64
GROUP_W = 256
HEAD_DIM = 64
CONV_A_WIDTH = 31
CONV_A_HALO = 16
GQA_HEADS = 4
GQA_KV_HEADS = 2
CHUNK = 128
SGU_GROUPS = 4
MLA_HEADS = 4
MLA_Q_LORA = 192
MLA_KV_LORA = 128
MLA_NOPE = 64
MLA_ROPE = 32
MLA_V = 64
MLA_DK_PAD = 128
ROPE_THETA = 10000.0
D_FF = 2816
DEEPNORM_ALPHA = (2 * DEPTH) ** 0.25
LN_EPS = 1e-5
RMS_EPS = 1e-6
D_IN_PROJ = 1888

ADAM_LR = 0.001
ADAM_B1 = 0.9
ADAM_B2 = 0.999
ADAM_EPS = 1e-08
ADAM_WD = 0.01
ADAM_STEP = 10

WEIGHTS = ['ln_in_g', 'ln_in_b', 'w_in', 'conv_a_w', 'conv_a_b', 'ln_a_g', 'ln_a_b', 'qk_norm_q', 'qk_norm_k',
           'sgu_ln_g', 'sgu_ln_b', 'sgu_w', 'sgu_b', 'mla_q_norm', 'mla_w_uq', 'mla_kv_norm', 'mla_w_ukv', 'w_out',
           'ln_mix_g', 'ln_mix_b', 'ffn_w_up', 'ffn_conv_w', 'ffn_conv_b', 'ffn_w_down', 'ln_ffn_g', 'ln_ffn_b']
SHARDED = {'w_in': 2, 'conv_a_w': 2, 'mla_w_uq': 2, 'mla_w_ukv': 2, 'w_out': 1, 'ffn_w_up': 2, 'ffn_conv_w': 2,
           'ffn_w_down': 1}
MATMUL_WEIGHTS = ('w_in', 'w_out', 'ffn_w_up', 'ffn_w_down')
REPLICATED = [n for n in WEIGHTS if n not in SHARDED]

ROW_TILE = 256
VMEM_LIMIT = 56 * 1024 * 1024


def _rawdot(a, b, ca, cb):
    return lax.dot_general(a.astype(MXU_DTYPE), b.astype(MXU_DTYPE), (((ca,), (cb,)), ((), ())),
                           preferred_element_type=F32)


@jax.custom_vjp
def mm_nn(a, b):
    return _rawdot(a, b, 1, 0)


def _mm_nn_fwd(a, b):
    return _rawdot(a, b, 1, 0), (a, b)


def _mm_nn_bwd(res, dy):
    a, b = res
    return _rawdot(dy, b, 1, 1), _rawdot(a, dy, 0, 0)


mm_nn.defvjp(_mm_nn_fwd, _mm_nn_bwd)


@jax.custom_vjp
def mm_nt(a, b):
    return _rawdot(a, b, 1, 1)


def _mm_nt_fwd(a, b):
    return _rawdot(a, b, 1, 1), (a, b)


def _mm_nt_bwd(res, dy):
    a, b = res
    return _rawdot(dy, b, 1, 0), _rawdot(dy, a, 0, 0)


mm_nt.defvjp(_mm_nt_fwd, _mm_nt_bwd)


def _pick_tile(d, cands):
    for c in cands:
        if d % c == 0:
            return c
    return d


def matmul(a, b, mode, out_dtype, name):
    if mode == 'nn':
        (m, k), (k2, n) = a.shape, b.shape
    elif mode == 'nt':
        (m, k), (n, k2) = a.shape, b.shape
    else:
        (k, m), (k2, n) = a.shape, b.shape
    assert k == k2, (a.shape, b.shape, mode)
    tm = _pick_tile(m, (1024, 512, 256, 128))
    tn = _pick_tile(n, (512, 256, 128))
    tk = _pick_tile(k, (2816, 2048, 1024, 512, 256, 128))
    nk = k // tk
    ca = 0 if mode == 'tn' else 1
    cb = 1 if mode == 'nt' else 0
    a_spec = pl.BlockSpec((tk, tm), lambda i, j, kk: (kk, i)) if mode == 'tn' else pl.BlockSpec((tm, tk), lambda i, j, kk: (i, kk))
    b_spec = pl.BlockSpec((tn, tk), lambda i, j, kk: (j, kk)) if mode == 'nt' else pl.BlockSpec((tk, tn), lambda i, j, kk: (kk, j))

    def body(a_ref, b_ref, o_ref, acc_ref):
        kk = pl.program_id(2)

        @pl.when(kk == 0)
        def _():
            acc_ref[...] = jnp.zeros_like(acc_ref)

        acc_ref[...] += _rawdot(a_ref[...], b_ref[...], ca, cb)

        @pl.when(kk == nk - 1)
        def _():
            o_ref[...] = acc_ref[...].astype(o_ref.dtype)

    return pl.pallas_call(
        body, grid=(m // tm, n // tn, nk), in_specs=[a_spec, b_spec],
        out_specs=pl.BlockSpec((tm, tn), lambda i, j, kk: (i, j)),
        out_shape=jax.ShapeDtypeStruct((m, n), out_dtype),
        scratch_shapes=[pltpu.VMEM((tm, tn), F32)],
        compiler_params=pltpu.CompilerParams(dimension_semantics=("parallel", "parallel", "arbitrary"),
                                             vmem_limit_bytes=VMEM_LIMIT),
        name=name)(a, b)


class Op:
    def __init__(self, arr, block, imap, grad=False, acc=False, first=None, gdtype=F32, gshape=None, gimap=None):
        self.arr, self.block, self.imap = arr, block, imap
        self.grad, self.acc, self.first, self.gdtype = grad, acc, first, gdtype
        self.gshape = arr.shape if gshape is None else gshape
        self.gimap = imap if gimap is None else gimap


def _row_op(arr, tm, grad=False, gdtype=F32):
    return Op(arr, (tm, arr.shape[1]), lambda i: (i, 0), grad=grad, gdtype=gdtype)


def _par_op(arr, grad=False):
    nd = arr.ndim
    return Op(arr, arr.shape, lambda i: (0,) * nd, grad=grad, acc=True, first=lambda ids: ids[0] == 0)


def _load(ref):
    v = ref[...]
    return v.astype(F32) if jnp.issubdtype(v.dtype, jnp.floating) else v


def stage_fwd(name, fn, ops, outs, grid):
    n_in = len(ops)

    def body(*refs):
        res = fn(*[_load(r) for r in refs[:n_in]])
        for r, o in zip(refs[n_in:], res):
            r[...] = o.astype(r.dtype)

    return pl.pallas_call(
        body, grid=grid, in_specs=[pl.BlockSpec(o.block, o.imap) for o in ops],
        out_specs=[pl.BlockSpec(b, im) for (_, _, b, im) in outs],
        out_shape=[jax.ShapeDtypeStruct(s, d) for (s, d, _, _) in outs],
        compiler_params=pltpu.CompilerParams(dimension_semantics=("parallel",) * len(grid),
                                             vmem_limit_bytes=VMEM_LIMIT),
        name=name)(*[o.arr for o in ops])


def stage_bwd(name, fn, ops, cts, grid, value_acc=False):
    n_in = len(ops)
    ct_flat = [c for group in cts if group is not None for c in group]
    n_ct = len(ct_flat)
    diff = [i for i, o in enumerate(ops) if o.grad]
    any_acc = value_acc or any(ops[i].acc for i in diff)
    ngrid = len(grid)

    def body(*refs):
        ids = [pl.program_id(a) for a in range(ngrid)]
        vals = [_load(r) for r in refs[:n_in]]
        ct_refs = refs[n_in:n_in + n_ct]
        out_refs = refs[n_in + n_ct:]

        def f(*dv):
            full = list(vals)
            for i, v in zip(diff, dv):
                full[i] = v
            return tuple(fn(*full))

        res, vjp = jax.vjp(f, *[vals[i] for i in diff])
        ct, pos = [], 0
        for group, r in zip(cts, res):
            if group is None:
                ct.append(jnp.ones_like(r))
            else:
                tot = None
                for _ in group:
                    c = ct_refs[pos][...].astype(F32)
                    tot = c if tot is None else tot + c
                    pos += 1
                ct.append(tot)
        grads = vjp(tuple(ct))
        for i, g, r in zip(diff, grads, out_refs):
            if ops[i].acc:
                @pl.when(ops[i].first(ids))
                def _(r=r):
                    r[...] = jnp.zeros_like(r)

                r[...] += g.astype(r.dtype)
            else:
                r[...] = g.astype(r.dtype)
        if value_acc:
            r = out_refs[len(diff)]

            @pl.when(ids[0] == 0)
            def _():
                r[...] = jnp.zeros_like(r)

            r[...] += res[0]

    in_specs = [pl.BlockSpec(o.block, o.imap) for o in ops] + [pl.BlockSpec(b, im) for (_, b, im) in ct_flat]
    out_specs = [pl.BlockSpec(ops[i].block, ops[i].gimap) for i in diff]
    out_shape = [jax.ShapeDtypeStruct(ops[i].gshape, ops[i].gdtype) for i in diff]
    if value_acc:
        out_specs.append(pl.BlockSpec((1, 1), lambda *ids: (0, 0)))
        out_shape.append(jax.ShapeDtypeStruct((1, 1), F32))
    sem = ("arbitrary",) * ngrid if any_acc else ("parallel",) * ngrid
    return pl.pallas_call(
        body, grid=grid, in_specs=in_specs, out_specs=out_specs, out_shape=out_shape,
        compiler_params=pltpu.CompilerParams(dimension_semantics=sem, vmem_limit_bytes=VMEM_LIMIT),
        name=name)(*[o.arr for o in ops], *[a for (a, _, _) in ct_flat])


def _sigmoid(x):
    return 1.0 / (1.0 + jnp.exp(-x))


def _silu(x):
    return x * _sigmoid(x)


def _gelu_tanh(x):
    return 0.5 * x * (1.0 + jnp.tanh(math.sqrt(2.0 / math.pi) * (x + 0.044715 * (x * x * x))))


def _ln(x, g, b):
    mu = jnp.mean(x, axis=-1, keepdims=True)
    xc = x - mu
    var = jnp.mean(xc * xc, axis=-1, keepdims=True)
    return xc * lax.rsqrt(var + LN_EPS) * g + b


def _rms(x, g):
    ms = jnp.mean(x * x, axis=-1, keepdims=True)
    return x * lax.rsqrt(ms + RMS_EPS) * g


def _rope1(x, c, s):
    half = x.shape[-1] // 2
    x1, x2 = x[:, :half], x[:, half:]
    return jnp.concatenate([x1 * c - x2 * s, x2 * c + x1 * s], axis=-1)


def _rope2(x, tab):
    h = x.shape[-1] // 2
    q = h // 2
    return jnp.concatenate([_rope1(x[:, :h], tab[:, 0:q], tab[:, q:2 * q]),
                            _rope1(x[:, h:], tab[:, 2 * q:3 * q], tab[:, 3 * q:4 * q])], axis=-1)


def fn_ln(x, g, b):
    return (_ln(x, g, b),)


def _twice(fn):
    def f(*a):
        (y,) = fn(*a)
        return y, y
    return f


def fn_pre(proj, tab_b, tab_d, qng, kng, sg, sb, sw, sbt, mqn, wuq, mkvn, wukv):
    tm = proj.shape[0]
    a_in, b_q, b_k, b_v = proj[:, 0:512], proj[:, 512:768], proj[:, 768:896], proj[:, 896:1024]
    c_in, d_cq, d_ckv, d_kr = proj[:, 1024:1536], proj[:, 1536:1728], proj[:, 1728:1856], proj[:, 1856:1888]
    aglu = a_in[:, :GROUP_W] * _sigmoid(a_in[:, GROUP_W:])
    q = jnp.concatenate([_rope2(_rms(b_q[:, 64 * h:64 * h + 64], qng), tab_b) for h in range(GQA_HEADS)], axis=-1)
    k = jnp.concatenate([_rope2(_rms(b_k[:, 64 * h:64 * h + 64], kng), tab_b) for h in range(GQA_KV_HEADS)], axis=-1)
    c = _gelu_tanh(c_in)
    u, sv = c[:, :GROUP_W], _ln(c[:, GROUP_W:], sg, sb)
    rows = []
    for n in range(tm // CHUNK):
        svn = sv[CHUNK * n:CHUNK * (n + 1)]
        rows.append(jnp.concatenate(
            [mm_nn(sw[CHUNK * g:CHUNK * (g + 1)], svn[:, 64 * g:64 * g + 64]) + sbt[:, g:g + 1]
             for g in range(SGU_GROUPS)], axis=-1))
    o_c = u * jnp.concatenate(rows, axis=0)
    qd = mm_nn(_rms(d_cq, mqn), wuq)
    kvd = mm_nn(_rms(d_ckv, mkvn), wukv)
    k_rope = _rope2(d_kr, tab_d)
    zpad = jnp.zeros((tm, MLA_DK_PAD - MLA_NOPE - MLA_ROPE), F32)
    qf, kf, vd = [], [], []
    for h in range(MLA_HEADS):
        qh = qd[:, 96 * h:96 * h + 96]
        qf += [qh[:, :MLA_NOPE], _rope2(qh[:, MLA_NOPE:], tab_d), zpad]
        kf += [kvd[:, 128 * h:128 * h + MLA_NOPE], k_rope, zpad]
        vd.append(kvd[:, 128 * h + MLA_NOPE:128 * h + 128])
    return (aglu, q, k, b_v, o_c, jnp.concatenate(qf, axis=-1), jnp.concatenate(kf, axis=-1),
            jnp.concatenate(vd, axis=-1))


def fn_aconv(win, w, b, g, beta):
    tm = win.shape[0] - 2 * CONV_A_HALO
    off = CONV_A_HALO - CONV_A_WIDTH // 2
    acc = None
    for kk in range(CONV_A_WIDTH):
        term = win[off + kk:off + kk + tm] * w[kk:kk + 1, :]
        acc = term if acc is None else acc + term
    return (_silu(_ln(acc + b, g, beta)),)


def fn_attn(scale, q, k, v):
    s = mm_nt(q, k) * scale
    m = jnp.max(s, axis=-1, keepdims=True)
    e = jnp.exp(s - m)
    p = e / jnp.sum(e, axis=-1, keepdims=True)
    return (mm_nn(p, v),)


def fn_resln(h, r, g, b):
    return (_ln(DEEPNORM_ALPHA * h + r, g, b),)


def _shift_down(x):
    return jnp.concatenate([jnp.zeros((8, x.shape[1]), F32), x], axis=0)[7:7 + x.shape[0]]


def _shift_up(x):
    return jnp.concatenate([x, jnp.zeros((8, x.shape[1]), F32)], axis=0)[1:1 + x.shape[0]]


def fn_ffnconv(u1, u2, w1, w2, b1, b2):
    c1 = _shift_down(u1) * w1[0:1] + u1 * w1[1:2] + _shift_up(u1) * w1[2:3] + b1
    c2 = _shift_down(u2) * w2[0:1] + u2 * w2[1:2] + _shift_up(u2) * w2[2:3] + b2
    return (_silu(c1) * c2,)


def fn_final(h, r, t, g, b):
    y = _ln(DEEPNORM_ALPHA * h + r, g, b)
    err = (y - t) * (y - t)
    return (0.5 * jnp.sum(jnp.mean(err, axis=-1, keepdims=True), axis=0, keepdims=True),)


def _rope_tables(seq):
    pos = jnp.arange(seq, dtype=jnp.int32)
    row = (pos // GRID_W).astype(F32)
    col = (pos % GRID_W).astype(F32)

    def tab(half):
        inv = ROPE_THETA ** (-jnp.arange(half, dtype=F32) / half)
        ar, ac = row[:, None] * inv[None, :], col[:, None] * inv[None, :]
        return jnp.concatenate([jnp.cos(ar), jnp.sin(ar), jnp.cos(ac), jnp.sin(ac)], axis=-1)

    return tab(HEAD_DIM // 4), tab(MLA_ROPE // 4)


def _pre_ops(proj, tabs, kp, grad):
    tm = ROW_TILE
    ops = [_row_op(proj, tm, grad=grad, gdtype=MXU_DTYPE), _row_op(tabs[0], tm), _row_op(tabs[1], tm)]
    ops += [_par_op(kp[n], grad=grad) for n in ('qng', 'kng', 'sg', 'sb', 'sw', 'sbt', 'mqn', 'wuq', 'mkvn', 'wukv')]
    return ops


PRE_OUT_WIDTHS = (GROUP_W, GROUP_W, 128, 128, GROUP_W, MLA_HEADS * MLA_DK_PAD, MLA_HEADS * MLA_DK_PAD, GROUP_W)


def pre_fwd(proj, tabs, kp, tag):
    seq = proj.shape[0]
    tm = ROW_TILE
    dts = (F32,) + (MXU_DTYPE,) * 7
    outs = [((seq, w), dt, (tm, w), lambda i: (i, 0)) for w, dt in zip(PRE_OUT_WIDTHS, dts)]
    return stage_fwd("pre_fwd" + tag, fn_pre, _pre_ops(proj, tabs, kp, False), outs, (seq // tm,))


def pre_bwd(proj, tabs, kp, cts, tag):
    seq = proj.shape[0]
    tm = ROW_TILE
    ct = [[(c, (tm, c.shape[1]), lambda i: (i, 0))] for c in cts]
    return stage_bwd("pre_bwd" + tag, fn_pre, _pre_ops(proj, tabs, kp, True), ct, (seq // tm,))


def _aconv_ops(kp, grad):
    return [_par_op(kp[n], grad=grad) for n in ('caw', 'cab', 'lag', 'lab')]


def aconv_fwd(aglu_pad, kp, tag):
    seq = aglu_pad.shape[0] - 2 * CONV_A_HALO
    tm = ROW_TILE
    n_par = 4

    def body(x_ref, *refs):
        i = pl.program_id(0)
        win = x_ref[pl.ds(pl.multiple_of(i * tm, tm), tm + 2 * CONV_A_HALO), :]
        (o,) = fn_aconv(win, *[_load(r) for r in refs[:n_par]])
        refs[n_par][...] = o.astype(refs[n_par].dtype)

    pars = _aconv_ops(kp, False)
    return pl.pallas_call(
        body, grid=(seq // tm,),
        in_specs=[pl.BlockSpec(aglu_pad.shape, lambda i: (0, 0))] + [pl.BlockSpec(o.block, o.imap) for o in pars],
        out_specs=pl.BlockSpec((tm, GROUP_W), lambda i: (i, 0)),
        out_shape=jax.ShapeDtypeStruct((seq, GROUP_W), MXU_DTYPE),
        compiler_params=pltpu.CompilerParams(dimension_semantics=("parallel",), vmem_limit_bytes=VMEM_LIMIT),
        name="aconv_fwd" + tag)(aglu_pad, *[o.arr for o in pars])


def aconv_bwd(aglu_pad, kp, d_oa, tag):
    seq = aglu_pad.shape[0] - 2 * CONV_A_HALO
    tm = ROW_TILE
    n_par = 4

    def body(x_ref, *refs):
        i = pl.program_id(0)
        rows = pl.ds(pl.multiple_of(i * tm, tm), tm + 2 * CONV_A_HALO)
        pars = [_load(r) for r in refs[:n_par]]
        ct = refs[n_par][...].astype(F32)
        outs = refs[n_par + 1:]
        _, vjp = jax.vjp(lambda *a: fn_aconv(*a), x_ref[rows, :], *pars)
        grads = vjp((ct,))

        @pl.when(i == 0)
        def _():
            for r in outs:
                r[...] = jnp.zeros_like(r)

        outs[0][rows, :] += grads[0]
        for r, g in zip(outs[1:], grads[1:]):
            r[...] += g

    pars = _aconv_ops(kp, True)
    whole = pl.BlockSpec(aglu_pad.shape, lambda i: (0, 0))
    par_specs = [pl.BlockSpec(o.block, o.imap) for o in pars]
    return pl.pallas_call(
        body, grid=(seq // tm,),
        in_specs=[whole] + par_specs + [pl.BlockSpec((tm, GROUP_W), lambda i: (i, 0))],
        out_specs=[whole] + par_specs,
        out_shape=[jax.ShapeDtypeStruct(aglu_pad.shape, F32)] + [jax.ShapeDtypeStruct(o.arr.shape, F32) for o in pars],
        compiler_params=pltpu.CompilerParams(dimension_semantics=("arbitrary",), vmem_limit_bytes=VMEM_LIMIT),
        name="aconv_bwd" + tag)(aglu_pad, *[o.arr for o in pars], d_oa)


def _attn_ops(q3, k3, v3, group, tq, grad):
    dk, dv = q3.shape[2], v3.shape[2]
    seq = q3.shape[1]
    first = lambda ids: jnp.logical_and(ids[0] % group == 0, ids[1] == 0)
    return [Op(q3, (None, tq, dk), lambda h, i: (h, i, 0), grad=grad),
            Op(k3, (None, seq, dk), lambda h, i: (h // group, 0, 0), grad=grad, acc=True, first=first),
            Op(v3, (None, seq, dv), lambda h, i: (h // group, 0, 0), grad=grad, acc=True, first=first)]


def attn_fwd(q3, k3, v3, scale, tag):
    heads, seq, _ = q3.shape
    group = heads // k3.shape[0]
    tq = ROW_TILE
    dv = v3.shape[2]
    outs = [((heads, seq, dv), MXU_DTYPE, (None, tq, dv), lambda h, i: (h, i, 0))]
    return stage_fwd("attn_fwd" + tag, functools.partial(fn_attn, scale), _attn_ops(q3, k3, v3, group, tq, False),
                     outs, (heads, seq // tq))[0]


def attn_bwd(q3, k3, v3, do3, scale, tag):
    heads, seq, _ = q3.shape
    group = heads // k3.shape[0]
    tq = ROW_TILE
    dv = v3.shape[2]
    ct = [[(do3, (None, tq, dv), lambda h, i: (h, i, 0))]]
    return stage_bwd("attn_bwd" + tag, functools.partial(fn_attn, scale), _attn_ops(q3, k3, v3, group, tq, True),
                     ct, (heads, seq // tq))


def resln_fwd(h, r, g, b, tag):
    seq, d = h.shape
    tm = ROW_TILE
    ops = [_row_op(h, tm), _row_op(r, tm), _par_op(g), _par_op(b)]
    outs = [((seq, d), dt, (tm, d), lambda i: (i, 0)) for dt in (F32, MXU_DTYPE)]
    return stage_fwd("resln_fwd" + tag, _twice(fn_resln), ops, outs, (seq // tm,))


def resln_bwd(h, r, g, b, dys, tag):
    seq, d = h.shape
    tm = ROW_TILE
    ops = [_row_op(h, tm, grad=True), _row_op(r, tm, grad=True, gdtype=MXU_DTYPE), _par_op(g, grad=True),
           _par_op(b, grad=True)]
    ct = [[(dy, (tm, d), lambda i: (i, 0)) for dy in dys]]
    return stage_bwd("resln_bwd" + tag, fn_resln, ops, ct, (seq // tm,))


def _ffnconv_ops(up, w, b, grad):
    seq = up.shape[0]
    nblk = D_FF // 128
    lo, hi = (lambda j: (0, j)), (lambda j: (0, j + nblk))
    half = lambda a: dict(gshape=(a.shape[0], D_FF), gimap=lo)
    return [Op(up, (seq, 128), lo, grad=grad, gdtype=MXU_DTYPE, **half(up)),
            Op(up, (seq, 128), hi, grad=grad, gdtype=MXU_DTYPE, **half(up)),
            Op(w, (3, 128), lo, grad=grad, **half(w)), Op(w, (3, 128), hi, grad=grad, **half(w)),
            Op(b, (1, 128), lo, grad=grad, **half(b)), Op(b, (1, 128), hi, grad=grad, **half(b))]


def ffnconv_fwd(up, w, b, tag):
    seq = up.shape[0]
    outs = [((seq, D_FF), MXU_DTYPE, (seq, 128), lambda j: (0, j))]
    return stage_fwd("ffnconv_fwd" + tag, fn_ffnconv, _ffnconv_ops(up, w, b, False), outs, (D_FF // 128,))[0]


def ffnconv_bwd(up, w, b, dact, tag):
    seq = up.shape[0]
    ct = [[(dact, (seq, 128), lambda j: (0, j))]]
    du1, du2, dw1, dw2, db1, db2 = stage_bwd("ffnconv_bwd" + tag, fn_ffnconv, _ffnconv_ops(up, w, b, True), ct,
                                             (D_FF // 128,))
    cat = lambda a, b_: jnp.concatenate([a, b_], axis=-1)
    return cat(du1, du2), cat(dw1, dw2), cat(db1, db2)


def final_bwd(h, r, t, g, b, tag):
    seq, d = h.shape
    tm = ROW_TILE
    ops = [_row_op(h, tm, grad=True), _row_op(r, tm, grad=True, gdtype=MXU_DTYPE), _row_op(t, tm),
           _par_op(g, grad=True), _par_op(b, grad=True)]
    return stage_bwd("final_bwd" + tag, fn_final, ops, [None], (seq // tm,), value_acc=True)


def _layer_params(wts, l):
    row = lambda a: a.reshape(1, -1)
    return dict(
        qng=row(wts['qk_norm_q'][l]), kng=row(wts['qk_norm_k'][l]),
        sg=row(wts['sgu_ln_g'][l]), sb=row(wts['sgu_ln_b'][l]),
        sw=wts['sgu_w'][l].reshape(SGU_GROUPS * CHUNK, CHUNK), sbt=wts['sgu_b'][l].T,
        mqn=row(wts['mla_q_norm'][l]), wuq=wts['mla_w_uq'][l], mkvn=row(wts['mla_kv_norm'][l]),
        wukv=wts['mla_w_ukv'][l],
        caw=wts['conv_a_w'][l], cab=row(wts['conv_a_b'][l]), lag=row(wts['ln_a_g'][l]), lab=row(wts['ln_a_b'][l]),
        lmg=row(wts['ln_mix_g'][l]), lmb=row(wts['ln_mix_b'][l]),
        fcw=wts['ffn_conv_w'][l], fcb=row(wts['ffn_conv_b'][l]),
        lfg=row(wts['ln_ffn_g'][l]), lfb=row(wts['ln_ffn_b'][l]))


def _to_heads(a, heads):
    seq = a.shape[0]
    return a.reshape(seq, heads, -1).transpose(1, 0, 2)


def _from_heads(a3):
    return a3.transpose(1, 0, 2).reshape(a3.shape[1], -1)


def local_step(x, target, wts, mat, hook):
    seq = x.shape[0]
    tm = ROW_TILE
    tabs = _rope_tables(seq)
    scale_b = HEAD_DIM ** -0.5
    scale_d = (MLA_NOPE + MLA_ROPE) ** -0.5
    ln_in_g, ln_in_b = wts['ln_in_g'].reshape(1, -1), wts['ln_in_b'].reshape(1, -1)

    h, h_m = stage_fwd("ln_in_fwd", _twice(fn_ln), [_row_op(x, tm), _par_op(ln_in_g), _par_op(ln_in_b)],
                       [((seq, D_MODEL), dt, (tm, D_MODEL), lambda i: (i, 0)) for dt in (F32, MXU_DTYPE)],
                       (seq // tm,))
    saved = []
    for l in range(DEPTH):
        tag = f"_l{l}"
        kp, unprep = jax.vjp(lambda w: _layer_params(w, l), wts)
        m = {'w_in': mat(l, 'w_in', h_m)}
        proj = matmul(h_m, m['w_in'], 'nn', F32, "mm_proj" + tag)
        aglu, q, k, v, o_c, qf, kf, vd = pre_fwd(proj, tabs, kp, tag)
        aglu_pad = jnp.pad(aglu, ((CONV_A_HALO, CONV_A_HALO), (0, 0)))
        o_a = aconv_fwd(aglu_pad, kp, tag)
        q3, k3, v3 = _to_heads(q, GQA_HEADS), _to_heads(k, GQA_KV_HEADS), _to_heads(v, GQA_KV_HEADS)
        o_b3 = attn_fwd(q3, k3, v3, scale_b, "_b" + tag)
        qd3, kd3, vd3 = _to_heads(qf, MLA_HEADS), _to_heads(kf, MLA_HEADS), _to_heads(vd, MLA_HEADS)
        o_d3 = attn_fwd(qd3, kd3, vd3, scale_d, "_d" + tag)
        o_cat = jnp.concatenate([o_a, _from_heads(o_b3), o_c, _from_heads(o_d3)], axis=-1)
        m['w_out'] = mat(l, 'w_out', o_cat)
        mix = matmul(o_cat, m['w_out'], 'nn', F32, "mm_mix" + tag)
        h1, h1_m = resln_fwd(h, mix, kp['lmg'], kp['lmb'], "_mix" + tag)
        m['ffn_w_up'] = mat(l, 'ffn_w_up', h1_m)
        up = matmul(h1_m, m['ffn_w_up'], 'nn', F32, "mm_up" + tag)
        act = ffnconv_fwd(up, kp['fcw'], kp['fcb'], tag)
        m['ffn_w_down'] = mat(l, 'ffn_w_down', act)
        f = matmul(act, m['ffn_w_down'], 'nn', F32, "mm_down" + tag)
        saved.append(dict(kp=kp, unprep=unprep, m=m, h=h, h_m=h_m, h1_m=h1_m, proj=proj, aglu_pad=aglu_pad, q3=q3, k3=k3, v3=v3, qd3=qd3,
                          kd3=kd3, vd3=vd3, o_cat=o_cat, mix=mix, h1=h1, up=up, act=act, f=f))
        if l + 1 < DEPTH:
            h, h_m = resln_fwd(h1, f, kp['lfg'], kp['lfb'], "_ffn" + tag)

    after = lambda a, tok: a if tok is None else a + tok
    small_acc = None
    dh_parts = None
    loss = None
    tok = None
    g_mix = None
    for l in reversed(range(DEPTH)):
        tag = f"_l{l}"
        s = saved[l]
        kp, m = s['kp'], s['m']
        dkp = {}
        lfg = after(kp['lfg'], tok)
        if l == DEPTH - 1:
            dh1_a, df, dkp['lfg'], dkp['lfb'], loss = final_bwd(s['h1'], s['f'], target, lfg, kp['lfb'], tag)
        else:
            dh1_a, df, dkp['lfg'], dkp['lfb'] = resln_bwd(s['h1'], s['f'], lfg, kp['lfb'], dh_parts, "_ffn" + tag)
        dact = matmul(df, m['ffn_w_down'], 'nt', F32, "mm_dact" + tag)
        g_down = matmul(s['act'], df, 'tn', F32, "mm_gdown" + tag)
        dup, dkp['fcw'], dkp['fcb'] = ffnconv_bwd(s['up'], kp['fcw'], kp['fcb'], dact, tag)
        dh1_b = matmul(dup, m['ffn_w_up'], 'nt', F32, "mm_dh1" + tag)
        g_up = matmul(s['h1_m'], dup, 'tn', F32, "mm_gup" + tag)
        tok = hook(f"ffn{l}", {('ffn_w_down', l): g_down, ('ffn_w_up', l): g_up})
        dh_a, dmix, dkp['lmg'], dkp['lmb'] = resln_bwd(s['h'], s['mix'], after(kp['lmg'], tok), kp['lmb'],
                                                       [dh1_a, dh1_b], "_mix" + tag)
        do_cat = matmul(dmix, m['w_out'], 'nt', F32, "mm_docat" + tag)
        g_out = matmul(s['o_cat'], dmix, 'tn', F32, "mm_gout" + tag)
        do_a, do_b, do_c, do_d = (do_cat[:, GROUP_W * j:GROUP_W * (j + 1)] for j in range(4))
        dq3, dk3, dv3 = attn_bwd(s['q3'], s['k3'], s['v3'], _to_heads(do_b, GQA_HEADS), scale_b, "_b" + tag)
        dqd3, dkd3, dvd3 = attn_bwd(s['qd3'], s['kd3'], s['vd3'], _to_heads(do_d, MLA_HEADS), scale_d, "_d" + tag)
        daglu_pad, dkp['caw'], dkp['cab'], dkp['lag'], dkp['lab'] = aconv_bwd(s['aglu_pad'], kp, do_a, tag)
        cts = [daglu_pad[CONV_A_HALO:CONV_A_HALO + seq], _from_heads(dq3), _from_heads(dk3), _from_heads(dv3), do_c,
               _from_heads(dqd3), _from_heads(dkd3), _from_heads(dvd3)]
        pre_g = pre_bwd(s['proj'], tabs, kp, cts, tag)
        dproj = pre_g[0]
        for n, g in zip(('qng', 'kng', 'sg', 'sb', 'sw', 'sbt', 'mqn', 'wuq', 'mkvn', 'wukv'), pre_g[1:]):
            dkp[n] = g
        dh_b = matmul(dproj, m['w_in'], 'nt', F32, "mm_dh" + tag)
        g_in = matmul(s['h_m'], dproj, 'tn', F32, "mm_gin" + tag)
        dh_parts = [dh_a, dh_b]
        (dw,) = s['unprep'](dkp)
        small_acc = dw if small_acc is None else jax.tree.map(jnp.add, small_acc, dw)
        g_mix = {('w_out', l): g_out, ('w_in', l): g_in}
        if l > 0:
            tok = hook(f"mix{l}", g_mix)

    g_mix.update({(n, None): small_acc[n] for n in SHARDED if n not in MATMUL_WEIGHTS})
    tok = hook("last", g_mix)
    dx, dg, db = stage_bwd("ln_in_bwd", fn_ln,
                           [_row_op(x, tm, grad=True), _par_op(after(ln_in_g, tok), grad=True),
                            _par_op(ln_in_b, grad=True)],
                           [[(p, (tm, D_MODEL), lambda i: (i, 0)) for p in dh_parts]], (seq // tm,))
    out = {n: small_acc[n] for n in REPLICATED}
    out['ln_in_g'], out['ln_in_b'] = dg.reshape(-1), db.reshape(-1)
    return loss, dx, out


def _peer(x, y, c, r):
    return ((1 - x) if r & 4 else x, (1 - y) if r & 2 else y, (1 - c) if r & 1 else c)


def exchange(arrs, gather, name):
    n_t = len(arrs)
    hbm = pl.BlockSpec(memory_space=pltpu.HBM)

    def body(*refs):
        ins, outs = refs[:n_t], refs[n_t:2 * n_t]
        send_sems, recv_sems, local_sems = refs[2 * n_t:]
        x, y, c = lax.axis_index("x"), lax.axis_index("y"), lax.axis_index("c")
        me = 4 * x + 2 * y + c

        def remote(t, r):
            px, py, pc = _peer(x, y, c, r)
            peer = 4 * px + 2 * py + pc
            src = ins[t] if gather else ins[t].at[peer]
            send = pltpu.make_async_remote_copy(src_ref=src, dst_ref=outs[t].at[me], send_sem=send_sems.at[t, r - 1],
                                                recv_sem=recv_sems.at[t, r - 1], device_id=(px, py, pc),
                                                device_id_type=pl.DeviceIdType.MESH)
            recv = pltpu.make_async_remote_copy(src_ref=src, dst_ref=outs[t].at[peer], send_sem=send_sems.at[t, r - 1],
                                                recv_sem=recv_sems.at[t, r - 1], device_id=(px, py, pc),
                                                device_id_type=pl.DeviceIdType.MESH)
            return send, recv

        locs = []
        for t in range(n_t):
            loc = pltpu.make_async_copy(ins[t] if gather else ins[t].at[me], outs[t].at[me], local_sems.at[t])
            loc.start()
            locs.append(loc)
        for t in range(n_t):
            for r in range(1, N_DEV):
                remote(t, r)[0].start()
        for t in range(n_t):
            for r in range(1, N_DEV):
                send, recv = remote(t, r)
                recv.wait_recv()
                send.wait_send()
        for loc in locs:
            loc.wait()

    out_shape = [jax.ShapeDtypeStruct(((N_DEV,) + a.shape) if gather else a.shape, a.dtype) for a in arrs]
    return pl.pallas_call(
        body, in_specs=[hbm] * n_t, out_specs=[hbm] * n_t, out_shape=out_shape,
        scratch_shapes=[pltpu.SemaphoreType.DMA((n_t, N_DEV - 1)), pltpu.SemaphoreType.DMA((n_t, N_DEV - 1)),
                        pltpu.SemaphoreType.DMA((n_t,))],
        name=name)(*arrs)


def _exchange_copy(src_ref, land_ref, send_sems, recv_sems, k, gather, x, y, c, r):
    px, py, pc = _peer(x, y, c, r)
    me, peer = 4 * x + 2 * y + c, 4 * px + 2 * py + pc
    src = src_ref if gather else src_ref.at[peer]
    mk = lambda dst: pltpu.make_async_remote_copy(
        src_ref=src, dst_ref=dst, send_sem=send_sems.at[k * (N_DEV - 1) + r - 1],
        recv_sem=recv_sems.at[k * (N_DEV - 1) + r - 1],
        device_id=(px, py, pc), device_id_type=pl.DeviceIdType.MESH)
    return mk(land_ref.at[me]), mk(land_ref.at[peer])


_HBM_SPEC = pl.BlockSpec(memory_space=pltpu.HBM)
_SEM_SPEC = pl.BlockSpec(memory_space=pltpu.SEMAPHORE)


def exchange_start(srcs, gather, groups, name):
    n_t = len(srcs)
    lands =[lax.empty(((N_DEV,) + s.shape) if gt else s.shape, s.dtype) for s, gt in zip(srcs, gather)]

    def body(*refs):
        src_refs, land_refs = refs[:n_t], refs[n_t:2 * n_t]
        sem_refs = refs[2 * n_t:2 * n_t + 2 * len(groups)]
        token = refs[-1]
        x, y, c = lax.axis_index("x"), lax.axis_index("y"), lax.axis_index("c")
        for gi, g in enumerate(groups):
            for k, t in enumerate(g):
                for r in range(1, N_DEV):
                    _exchange_copy(src_refs[t], land_refs[t], sem_refs[2 * gi], sem_refs[2 * gi + 1], k, gather[t],
                                   x, y, c, r)[0].start()
        token[...] = jnp.zeros_like(token)

    sem_shapes = []
    for g in groups:
        sem_shapes += [pltpu.SemaphoreType.DMA((len(g) * (N_DEV - 1),))] * 2
    hbm_shapes = [pltpu.HBM(a.shape, a.dtype) for a in list(srcs) + lands]
    n_sem = len(sem_shapes)
    res = pl.pallas_call(
        body, name=name,
        out_shape=tuple(sem_shapes + hbm_shapes + [jax.ShapeDtypeStruct((8, 128), F32)]),
        in_specs=[_HBM_SPEC] * (2 * n_t),
        out_specs=tuple([_SEM_SPEC] * n_sem + [_HBM_SPEC] * (2 * n_t) + [pl.BlockSpec(memory_space=pltpu.VMEM)]),
        input_output_aliases={i: n_sem + i for i in range(2 * n_t)},
        compiler_params=pltpu.CompilerParams(has_side_effects=pltpu.SideEffectType.DATAFLOW_SIDE_EFFECTING),
    )(*[pltpu.with_memory_space_constraint(a, pltpu.HBM) for a in list(srcs) + lands])
    sems = [(res[2 * gi], res[2 * gi + 1]) for gi in range(len(groups))]
    return sems, list(res[n_sem:n_sem + n_t]), list(res[n_sem + n_t:n_sem + 2 * n_t]), res[-1][0, 0]


def exchange_wait(sems, srcs, lands, gather, after, name):
    n_t = len(srcs)

    def body(*refs):
        src_refs, land_refs = refs[:n_t], refs[n_t:2 * n_t]
        send_sems, recv_sems = refs[2 * n_t], refs[2 * n_t + 1]
        x, y, c = lax.axis_index("x"), lax.axis_index("y"), lax.axis_index("c")
        for k in range(n_t):
            for r in range(1, N_DEV):
                send, recv = _exchange_copy(src_refs[k], land_refs[k], send_sems, recv_sems, k, gather[k], x, y, c, r)
                send.wait_send()
                recv.wait_recv()

    res = pl.pallas_call(
        body, name=name,
        out_shape=tuple(pltpu.HBM(a.shape, a.dtype) for a in list(srcs) + list(lands)),
        in_specs=[_HBM_SPEC] * (2 * n_t) + [_SEM_SPEC, _SEM_SPEC, pl.BlockSpec(memory_space=pl.ANY)],
        out_specs=tuple([_HBM_SPEC] * (2 * n_t)),
        input_output_aliases={i: i for i in range(2 * n_t)},
        compiler_params=pltpu.CompilerParams(has_side_effects=pltpu.SideEffectType.DATAFLOW_SIDE_EFFECTING),
    )(*srcs, *lands, sems[0], sems[1], after)
    return list(res[:n_t]), list(res[n_t:])


def adamw(parts, w, m, v, name):
    n_l, n_r, n_c = w.shape
    tr = n_r
    if n_r % 8 == 0:
        for cand in (512, 256, 128, 64, 32, 16, 8):
            if n_r % cand == 0 and cand * n_c * 4 <= 512 * 1024:
                tr = cand
                break
    c1 = 1.0 - ADAM_B1 ** ADAM_STEP
    c2 = 1.0 - ADAM_B2 ** ADAM_STEP
    per_layer = isinstance(parts, (list, tuple))
    n_p = n_l if per_layer else 1
    n_rb = n_r // tr

    def update(g, w_ref, m_ref, v_ref, g_ref, d_ref, nm_ref, nv_ref):
        w_, m_, v_ = w_ref[0], m_ref[0], v_ref[0]
        nm = ADAM_B1 * m_ + (1.0 - ADAM_B1) * g
        nv = ADAM_B2 * v_ + (1.0 - ADAM_B2) * (g * g)
        g_ref[0] = g
        nm_ref[0] = nm
        nv_ref[0] = nv
        d_ref[0] = -ADAM_LR * ((nm / c1) / (jnp.sqrt(nv / c2) + ADAM_EPS) + ADAM_WD * w_)

    def body(*refs):
        p_refs, rest = refs[:n_p], refs[n_p:]
        if not per_layer:
            g = p_refs[0][0, 0].astype(F32)
            for s in range(1, N_DEV):
                g = g + p_refs[0][s, 0].astype(F32)
            update(g, *rest)
        else:
            for lay in range(n_l):
                @pl.when(pl.program_id(0) == lay)
                def _(lay=lay):
                    g = p_refs[lay][0].astype(F32)
                    for s in range(1, N_DEV):
                        g = g + p_refs[lay][s].astype(F32)
                    update(g, *rest)

    blk = pl.BlockSpec((1, tr, n_c), lambda l, r: (l, r, 0))
    if per_layer:
        def p_spec(lay):
            park = 0 if lay > 0 else n_rb - 1
            return pl.BlockSpec((N_DEV, tr, n_c), lambda l, r: (0, jnp.where(l == lay, r, park), 0))
        p_specs, p_args = [p_spec(lay) for lay in range(n_l)], list(parts)
    else:
        p_specs, p_args = [pl.BlockSpec((N_DEV, 1, tr, n_c), lambda l, r: (0, l, r, 0))], [parts]
    return pl.pallas_call(
        body, grid=(n_l, n_rb), in_specs=p_specs + [blk, blk, blk],
        out_specs=[blk] * 4, out_shape=[jax.ShapeDtypeStruct(w.shape, F32)] * 4,
        compiler_params=pltpu.CompilerParams(dimension_semantics=("arbitrary", "arbitrary"),
                                             vmem_limit_bytes=VMEM_LIMIT),
        name=name)(*p_args, w, m, v)


PACK_UNIT = 1024
PACK_ROWS = 256


def _pack(pieces):
    rows = []
    for p in pieces:
        flat = p.reshape(-1).astype(F32)
        pad = (-flat.shape[0]) % PACK_UNIT
        rows.append(jnp.pad(flat, (0, pad)).reshape(-1, 128))
    n = sum(r.shape[0] for r in rows)
    rows.append(jnp.zeros(((-n) % PACK_ROWS, 128), F32))
    return jnp.concatenate(rows, axis=0)


def _unpack(packed, shapes):
    out, r0 = [], 0
    for shp in shapes:
        n = math.prod(shp)
        nr = -(-n // PACK_UNIT) * (PACK_UNIT // 128)
        out.append(packed[r0:r0 + nr].reshape(-1)[:n].reshape(shp))
        r0 += nr
    return out


def _shard_slots(g, axis):
    if axis == 1:
        return g.reshape(g.shape[0], N_DEV, g.shape[1] // N_DEV, g.shape[2]).transpose(1, 0, 2, 3)
    return g.reshape(g.shape[0], g.shape[1], N_DEV, g.shape[2] // N_DEV).transpose(2, 0, 1, 3)


def _unshard(slots, axis):
    if axis == 1:
        return slots.transpose(1, 0, 2, 3).reshape(slots.shape[1], -1, slots.shape[3])
    return slots.transpose(1, 2, 0, 3).reshape(slots.shape[1], slots.shape[2], -1)


def kernel(x, ln_in_g, ln_in_b, w_in, conv_a_w, conv_a_b, ln_a_g, ln_a_b, qk_norm_q, qk_norm_k, sgu_ln_g, sgu_ln_b, sgu_w, sgu_b, mla_q_norm, mla_w_uq, mla_kv_norm, mla_w_ukv, w_out, ln_mix_g, ln_mix_b, ffn_w_up, ffn_conv_w, ffn_conv_b, ffn_w_down, ln_ffn_g, ln_ffn_b, loss_target, m_ln_in_g, m_ln_in_b, m_w_in, m_conv_a_w, m_conv_a_b, m_ln_a_g, m_ln_a_b, m_qk_norm_q, m_qk_norm_k, m_sgu_ln_g, m_sgu_ln_b, m_sgu_w, m_sgu_b, m_mla_q_norm, m_mla_w_uq, m_mla_kv_norm, m_mla_w_ukv, m_w_out, m_ln_mix_g, m_ln_mix_b, m_ffn_w_up, m_ffn_conv_w, m_ffn_conv_b, m_ffn_w_down, m_ln_ffn_g, m_ln_ffn_b, v_ln_in_g, v_ln_in_b, v_w_in, v_conv_a_w, v_conv_a_b, v_ln_a_g, v_ln_a_b, v_qk_norm_q, v_qk_norm_k, v_sgu_ln_g, v_sgu_ln_b, v_sgu_w, v_sgu_b, v_mla_q_norm, v_mla_w_uq, v_mla_kv_norm, v_mla_w_ukv, v_w_out, v_ln_mix_g, v_ln_mix_b, v_ffn_w_up, v_ffn_conv_w, v_ffn_conv_b, v_ffn_w_down, v_ln_ffn_g, v_ln_ffn_b):
    local = dict(ln_in_g=ln_in_g, ln_in_b=ln_in_b, w_in=w_in, conv_a_w=conv_a_w, conv_a_b=conv_a_b, ln_a_g=ln_a_g, ln_a_b=ln_a_b, qk_norm_q=qk_norm_q, qk_norm_k=qk_norm_k, sgu_ln_g=sgu_ln_g, sgu_ln_b=sgu_ln_b, sgu_w=sgu_w, sgu_b=sgu_b, mla_q_norm=mla_q_norm, mla_w_uq=mla_w_uq, mla_kv_norm=mla_kv_norm, mla_w_ukv=mla_w_ukv, w_out=w_out, ln_mix_g=ln_mix_g, ln_mix_b=ln_mix_b, ffn_w_up=ffn_w_up, ffn_conv_w=ffn_conv_w, ffn_conv_b=ffn_conv_b, ffn_w_down=ffn_w_down, ln_ffn_g=ln_ffn_g, ln_ffn_b=ln_ffn_b)
    mom = dict(ln_in_g=m_ln_in_g, ln_in_b=m_ln_in_b, w_in=m_w_in, conv_a_w=m_conv_a_w, conv_a_b=m_conv_a_b, ln_a_g=m_ln_a_g, ln_a_b=m_ln_a_b, qk_norm_q=m_qk_norm_q, qk_norm_k=m_qk_norm_k, sgu_ln_g=m_sgu_ln_g, sgu_ln_b=m_sgu_ln_b, sgu_w=m_sgu_w, sgu_b=m_sgu_b, mla_q_norm=m_mla_q_norm, mla_w_uq=m_mla_w_uq, mla_kv_norm=m_mla_kv_norm, mla_w_ukv=m_mla_w_ukv, w_out=m_w_out, ln_mix_g=m_ln_mix_g, ln_mix_b=m_ln_mix_b, ffn_w_up=m_ffn_w_up, ffn_conv_w=m_ffn_conv_w, ffn_conv_b=m_ffn_conv_b, ffn_w_down=m_ffn_w_down, ln_ffn_g=m_ln_ffn_g, ln_ffn_b=m_ln_ffn_b)
    var = dict(ln_in_g=v_ln_in_g, ln_in_b=v_ln_in_b, w_in=v_w_in, conv_a_w=v_conv_a_w, conv_a_b=v_conv_a_b, ln_a_g=v_ln_a_g, ln_a_b=v_ln_a_b, qk_norm_q=v_qk_norm_q, qk_norm_k=v_qk_norm_k, sgu_ln_g=v_sgu_ln_g, sgu_ln_b=v_sgu_ln_b, sgu_w=v_sgu_w, sgu_b=v_sgu_b, mla_q_norm=v_mla_q_norm, mla_w_uq=v_mla_w_uq, mla_kv_norm=v_mla_kv_norm, mla_w_ukv=v_mla_w_ukv, w_out=v_w_out, ln_mix_g=v_ln_mix_g, ln_mix_b=v_ln_mix_b, ffn_w_up=v_ffn_w_up, ffn_conv_w=v_ffn_conv_w, ffn_conv_b=v_ffn_conv_b, ffn_w_down=v_ffn_w_down, ln_ffn_g=v_ln_ffn_g, ln_ffn_b=v_ln_ffn_b)

    me = 4 * lax.axis_index("x") + 2 * lax.axis_index("y") + lax.axis_index("c")

    def own_slot(slots, block):
        return lax.dynamic_update_slice(slots, block[None], (me,) + (0,) * block.ndim)

    small_sharded = [n for n in SHARDED if n not in MATMUL_WEIGHTS]
    big_order = [(n, l) for l in range(DEPTH) for n in MATMUL_WEIGHTS]
    srcs = [local['w_in'][0].astype(COMM_DTYPE)] + [local[n] for n in small_sharded]
    srcs += [local[n][l].astype(COMM_DTYPE) for (n, l) in big_order[1:]]
    n_first = 1 + len(small_sharded)
    groups = [list(range(n_first))] + [[n_first + j] for j in range(len(big_order) - 1)]
    g_sems, g_srcs, g_lands, tok0 = exchange_start(srcs, [True] * len(srcs), groups, "gather_start")
    pending = {key: gi for gi, key in enumerate(big_order)}

    def finish(gi, after):
        idx = groups[gi]
        _, lands = exchange_wait(g_sems[gi], [g_srcs[t] for t in idx], [g_lands[t] for t in idx], [True] * len(idx),
                                 after, f"gather_wait{gi}")
        return [own_slot(ld, srcs[t]) for ld, t in zip(lands, idx)]

    first = finish(0, local['ln_in_g'] + tok0)
    wts = {n: local[n] for n in REPLICATED}
    wts['ln_in_g'] = local['ln_in_g'] + tok0
    for n, slots in zip(small_sharded, first[1:]):
        wts[n] = _unshard(slots, SHARDED[n])

    def unshard_layer(slots, n):
        if SHARDED[n] == 1:
            return slots.reshape(-1, slots.shape[2])
        return slots.transpose(1, 0, 2).reshape(slots.shape[1], -1)

    def mat(l, n, after):
        gi = pending[(n, l)]
        slots = first[0] if gi == 0 else finish(gi, after)[0]
        return unshard_layer(slots, n).astype(MXU_DTYPE)

    started = []

    def hook(key, grads):
        tensors = []
        for (n, l), g in grads.items():
            if l is None:
                tensors.append(((n, l), _shard_slots(g, SHARDED[n])))
            else:
                tensors.append(((n, l), _shard_slots(g[None], SHARDED[n])[:, 0].astype(COMM_DTYPE)))
        sems, s_srcs, s_lands, tok = exchange_start([a for _, a in tensors], [False] * len(tensors),
                                                    [list(range(len(tensors)))], "scatter_start_" + key)
        started.append((key, [k for k, _ in tensors], sems[0], s_srcs, s_lands))
        return tok

    loss, dx, grads = local_step(x[0], loss_target[0], wts, mat, hook)

    parts = {}
    for key, keys, sems, s_srcs, s_lands in started:
        s_out, lands = exchange_wait(sems, s_srcs, s_lands, [False] * len(keys), dx, "scatter_wait_" + key)
        for k, so, ld in zip(keys, s_out, lands):
            parts[k] = own_slot(ld, lax.dynamic_index_in_dim(so, me, 0, keepdims=False))
    res = {}
    for n in SHARDED:
        p = [parts[(n, l)] for l in range(DEPTH)] if n in MATMUL_WEIGHTS else parts[(n, None)]
        res[n] = adamw(p, local[n], mom[n], var[n], "adamw_" + n)

    pieces = [grads[n] for n in REPLICATED] + [loss]
    packed = _pack(pieces)
    (gath,) = exchange([packed], True, "gather_small")
    zeros1 = jnp.zeros((1, 1), F32)
    pk = lambda d: _pack([d[n] for n in REPLICATED] + [zeros1])[None]
    g_s, d_s, m_s, v_s = adamw(gath[:, None], pk(local), pk(mom), pk(var), "adamw_small")
    shapes = [local[n].shape for n in REPLICATED] + [(1, 1)]
    unp = [_unpack(a[0], shapes) for a in (g_s, d_s, m_s, v_s)]
    for j, n in enumerate(REPLICATED):
        res[n] = tuple(u[j] for u in unp)
    loss_total = unp[0][-1].reshape(())

    return (loss_total, dx[None], *[res[n][0] for n in WEIGHTS], *[res[n][1] for n in WEIGHTS],
            *[res[n][2] for n in WEIGHTS], *[res[n][3] for n in WEIGHTS])
```

```python
import functools
import math

import jax
import jax.numpy as jnp
from jax import lax
from jax.experimental import pallas as pl
from jax.experimental.pallas import tpu as pltpu

F32 = jnp.float32
MXU_DTYPE = jnp.bfloat16
COMM_DTYPE = jnp.bfloat16

N_DEV = 8
D_MODEL = 1024
DEPTH = 2
GRID_W = 64
GROUP_W = 256
HEAD_DIM = 64
CONV_A_WIDTH = 31
CONV_A_HALO = 16
GQA_HEADS = 4
GQA_KV_HEADS = 2
CHUNK = 128
SGU_GROUPS = 4
MLA_HEADS = 4
MLA_Q_LORA = 192
MLA_KV_LORA = 128
MLA_NOPE = 64
MLA_ROPE = 32
MLA_V = 64
MLA_DK_PAD = 128
ROPE_THETA = 10000.0
D_FF = 2816
DEEPNORM_ALPHA = (2 * DEPTH) ** 0.25
LN_EPS = 1e-5
RMS_EPS = 1e-6
D_IN_PROJ = 1888

ADAM_LR = 0.001
ADAM_B1 = 0.9
ADAM_B2 = 0.999
ADAM_EPS = 1e-08
ADAM_WD = 0.01
ADAM_STEP = 10

WEIGHTS = ['ln_in_g', 'ln_in_b', 'w_in', 'conv_a_w', 'conv_a_b', 'ln_a_g', 'ln_a_b', 'qk_norm_q', 'qk_norm_k',
           'sgu_ln_g', 'sgu_ln_b', 'sgu_w', 'sgu_b', 'mla_q_norm', 'mla_w_uq', 'mla_kv_norm', 'mla_w_ukv', 'w_out',
           'ln_mix_g', 'ln_mix_b', 'ffn_w_up', 'ffn_conv_w', 'ffn_conv_b', 'ffn_w_down', 'ln_ffn_g', 'ln_ffn_b']
SHARDED = {'w_in': 2, 'conv_a_w': 2, 'mla_w_uq': 2, 'mla_w_ukv': 2, 'w_out': 1, 'ffn_w_up': 2, 'ffn_conv_w': 2,
           'ffn_w_down': 1}
MATMUL_WEIGHTS = ('w_in', 'w_out', 'ffn_w_up', 'ffn_w_down')
REPLICATED = [n for n in WEIGHTS if n not in SHARDED]

ROW_TILE = 256
VMEM_LIMIT = 56 * 1024 * 1024


def _rawdot(a, b, ca, cb):
    return lax.dot_general(a.astype(MXU_DTYPE), b.astype(MXU_DTYPE), (((ca,), (cb,)), ((), ())),
                           preferred_element_type=F32)


@jax.custom_vjp
def mm_nn(a, b):
    return _rawdot(a, b, 1, 0)


def _mm_nn_fwd(a, b):
    return _rawdot(a, b, 1, 0), (a, b)


def _mm_nn_bwd(res, dy):
    a, b = res
    return _rawdot(dy, b, 1, 1), _rawdot(a, dy, 0, 0)


mm_nn.defvjp(_mm_nn_fwd, _mm_nn_bwd)


@jax.custom_vjp
def mm_nt(a, b):
    return _rawdot(a, b, 1, 1)


def _mm_nt_fwd(a, b):
    return _rawdot(a, b, 1, 1), (a, b)


def _mm_nt_bwd(res, dy):
    a, b = res
    return _rawdot(dy, b, 1, 0), _rawdot(dy, a, 0, 0)


mm_nt.defvjp(_mm_nt_fwd, _mm_nt_bwd)


def _pick_tile(d, cands):
    for c in cands:
        if d % c == 0:
            return c
    return d


def matmul(a, b, mode, out_dtype, name):
    if mode == 'nn':
        (m, k), (k2, n) = a.shape, b.shape
    elif mode == 'nt':
        (m, k), (n, k2) = a.shape, b.shape
    else:
        (k, m), (k2, n) = a.shape, b.shape
    assert k == k2, (a.shape, b.shape, mode)
    tm = _pick_tile(m, (1024, 512, 256, 128))
    tn = _pick_tile(n, (512, 256, 128))
    tk = _pick_tile(k, (2816, 2048, 1024, 512, 256, 128))
    nk = k // tk
    ca = 0 if mode == 'tn' else 1
    cb = 1 if mode == 'nt' else 0
    a_spec = pl.BlockSpec((tk, tm), lambda i, j, kk: (kk, i)) if mode == 'tn' else pl.BlockSpec((tm, tk), lambda i, j, kk: (i, kk))
    b_spec = pl.BlockSpec((tn, tk), lambda i, j, kk: (j, kk)) if mode == 'nt' else pl.BlockSpec((tk, tn), lambda i, j, kk: (kk, j))

    def body(a_ref, b_ref, o_ref, acc_ref):
        kk = pl.program_id(2)

        @pl.when(kk == 0)
        def _():
            acc_ref[...] = jnp.zeros_like(acc_ref)

        acc_ref[...] += _rawdot(a_ref[...], b_ref[...], ca, cb)

        @pl.when(kk == nk - 1)
        def _():
            o_ref[...] = acc_ref[...].astype(o_ref.dtype)

    return pl.pallas_call(
        body, grid=(m // tm, n // tn, nk), in_specs=[a_spec, b_spec],
        out_specs=pl.BlockSpec((tm, tn), lambda i, j, kk: (i, j)),
        out_shape=jax.ShapeDtypeStruct((m, n), out_dtype),
        scratch_shapes=[pltpu.VMEM((tm, tn), F32)],
        compiler_params=pltpu.CompilerParams(dimension_semantics=("parallel", "parallel", "arbitrary"),
                                             vmem_limit_bytes=VMEM_LIMIT),
        name=name)(a, b)


class Op:
    def __init__(self, arr, block, imap, grad=False, acc=False, first=None, gdtype=F32, gshape=None, gimap=None):
        self.arr, self.block, self.imap = arr, block, imap
        self.grad, self.acc, self.first, self.gdtype = grad, acc, first, gdtype
        self.gshape = arr.shape if gshape is None else gshape
        self.gimap = imap if gimap is None else gimap


def _row_op(arr, tm, grad=False, gdtype=F32):
    return Op(arr, (tm, arr.shape[1]), lambda i: (i, 0), grad=grad, gdtype=gdtype)


def _par_op(arr, grad=False):
    nd = arr.ndim
    return Op(arr, arr.shape, lambda i: (0,) * nd, grad=grad, acc=True, first=lambda ids: ids[0] == 0)


def _load(ref):
    v = ref[...]
    return v.astype(F32) if jnp.issubdtype(v.dtype, jnp.floating) else v


def stage_fwd(name, fn, ops, outs, grid):
    n_in = len(ops)

    def body(*refs):
        res = fn(*[_load(r) for r in refs[:n_in]])
        for r, o in zip(refs[n_in:], res):
            r[...] = o.astype(r.dtype)

    return pl.pallas_call(
        body, grid=grid, in_specs=[pl.BlockSpec(o.block, o.imap) for o in ops],
        out_specs=[pl.BlockSpec(b, im) for (_, _, b, im) in outs],
        out_shape=[jax.ShapeDtypeStruct(s, d) for (s, d, _, _) in outs],
        compiler_params=pltpu.CompilerParams(dimension_semantics=("parallel",) * len(grid),
                                             vmem_limit_bytes=VMEM_LIMIT),
        name=name)(*[o.arr for o in ops])


def stage_bwd(name, fn, ops, cts, grid, value_acc=False):
    n_in = len(ops)
    ct_flat = [c for group in cts if group is not None for c in group]
    n_ct = len(ct_flat)
    diff = [i for i, o in enumerate(ops) if o.grad]
    any_acc = value_acc or any(ops[i].acc for i in diff)
    ngrid = len(grid)

    def body(*refs):
        ids = [pl.program_id(a) for a in range(ngrid)]
        vals = [_load(r) for r in refs[:n_in]]
        ct_refs = refs[n_in:n_in + n_ct]
        out_refs = refs[n_in + n_ct:]

        def f(*dv):
            full = list(vals)
            for i, v in zip(diff, dv):
                full[i] = v
            return tuple(fn(*full))

        res, vjp = jax.vjp(f, *[vals[i] for i in diff])
        ct, pos = [], 0
        for group, r in zip(cts, res):
            if group is None:
                ct.append(jnp.ones_like(r))
            else:
                tot = None
                for _ in group:
                    c = ct_refs[pos][...].astype(F32)
                    tot = c if tot is None else tot + c
                    pos += 1
                ct.append(tot)
        grads = vjp(tuple(ct))
        for i, g, r in zip(diff, grads, out_refs):
            if ops[i].acc:
                @pl.when(ops[i].first(ids))
                def _(r=r):
                    r[...] = jnp.zeros_like(r)

                r[...] += g.astype(r.dtype)
            else:
                r[...] = g.astype(r.dtype)
        if value_acc:
            r = out_refs[len(diff)]

            @pl.when(ids[0] == 0)
            def _():
                r[...] = jnp.zeros_like(r)

            r[...] += res[0]

    in_specs = [pl.BlockSpec(o.block, o.imap) for o in ops] + [pl.BlockSpec(b, im) for (_, b, im) in ct_flat]
    out_specs = [pl.BlockSpec(ops[i].block, ops[i].gimap) for i in diff]
    out_shape = [jax.ShapeDtypeStruct(ops[i].gshape, ops[i].gdtype) for i in diff]
    if value_acc:
        out_specs.append(pl.BlockSpec((1, 1), lambda *ids: (0, 0)))
        out_shape.append(jax.ShapeDtypeStruct((1, 1), F32))
    sem = ("arbitrary",) * ngrid if any_acc else ("parallel",) * ngrid
    return pl.pallas_call(
        body, grid=grid, in_specs=in_specs, out_specs=out_specs, out_shape=out_shape,
        compiler_params=pltpu.CompilerParams(dimension_semantics=sem, vmem_limit_bytes=VMEM_LIMIT),
        name=name)(*[o.arr for o in ops], *[a for (a, _, _) in ct_flat])


def _sigmoid(x):
    return 1.0 / (1.0 + jnp.exp(-x))


def _silu(x):
    return x * _sigmoid(x)


def _gelu_tanh(x):
    return 0.5 * x * (1.0 + jnp.tanh(math.sqrt(2.0 / math.pi) * (x + 0.044715 * (x * x * x))))


def _ln(x, g, b):
    mu = jnp.mean(x, axis=-1, keepdims=True)
    xc = x - mu
    var = jnp.mean(xc * xc, axis=-1, keepdims=True)
    return xc * lax.rsqrt(var + LN_EPS) * g + b


def _rms(x, g):
    ms = jnp.mean(x * x, axis=-1, keepdims=True)
    return x * lax.rsqrt(ms + RMS_EPS) * g


def _rope1(x, c, s):
    half = x.shape[-1] // 2
    x1, x2 = x[:, :half], x[:, half:]
    return jnp.concatenate([x1 * c - x2 * s, x2 * c + x1 * s], axis=-1)


def _rope2(x, tab):
    h = x.shape[-1] // 2
    q = h // 2
    return jnp.concatenate([_rope1(x[:, :h], tab[:, 0:q], tab[:, q:2 * q]),
                            _rope1(x[:, h:], tab[:, 2 * q:3 * q], tab[:, 3 * q:4 * q])], axis=-1)


def fn_ln(x, g, b):
    return (_ln(x, g, b),)


def _twice(fn):
    def f(*a):
        (y,) = fn(*a)
        return y, y
    return f


def fn_pre(proj, tab_b, tab_d, qng, kng, sg, sb, sw, sbt, mqn, wuq, mkvn, wukv):
    tm = proj.shape[0]
    a_in, b_q, b_k, b_v = proj[:, 0:512], proj[:, 512:768], proj[:, 768:896], proj[:, 896:1024]
    c_in, d_cq, d_ckv, d_kr = proj[:, 1024:1536], proj[:, 1536:1728], proj[:, 1728:1856], proj[:, 1856:1888]
    aglu = a_in[:, :GROUP_W] * _sigmoid(a_in[:, GROUP_W:])
    q = jnp.concatenate([_rope2(_rms(b_q[:, 64 * h:64 * h + 64], qng), tab_b) for h in range(GQA_HEADS)], axis=-1)
    k = jnp.concatenate([_rope2(_rms(b_k[:, 64 * h:64 * h + 64], kng), tab_b) for h in range(GQA_KV_HEADS)], axis=-1)
    c = _gelu_tanh(c_in)
    u, sv = c[:, :GROUP_W], _ln(c[:, GROUP_W:], sg, sb)
    rows = []
    for n in range(tm // CHUNK):
        svn = sv[CHUNK * n:CHUNK * (n + 1)]
        rows.append(jnp.concatenate(
            [mm_nn(sw[CHUNK * g:CHUNK * (g + 1)], svn[:, 64 * g:64 * g + 64]) + sbt[:, g:g + 1]
             for g in range(SGU_GROUPS)], axis=-1))
    o_c = u * jnp.concatenate(rows, axis=0)
    qd = mm_nn(_rms(d_cq, mqn), wuq)
    kvd = mm_nn(_rms(d_ckv, mkvn), wukv)
    k_rope = _rope2(d_kr, tab_d)
    zpad = jnp.zeros((tm, MLA_DK_PAD - MLA_NOPE - MLA_ROPE), F32)
    qf, kf, vd = [], [], []
    for h in range(MLA_HEADS):
        qh = qd[:, 96 * h:96 * h + 96]
        qf += [qh[:, :MLA_NOPE], _rope2(qh[:, MLA_NOPE:], tab_d), zpad]
        kf += [kvd[:, 128 * h:128 * h + MLA_NOPE], k_rope, zpad]
        vd.append(kvd[:, 128 * h + MLA_NOPE:128 * h + 128])
    return (aglu, q, k, b_v, o_c, jnp.concatenate(qf, axis=-1), jnp.concatenate(kf, axis=-1),
            jnp.concatenate(vd, axis=-1))


def fn_aconv(win, w, b, g, beta):
    tm = win.shape[0] - 2 * CONV_A_HALO
    off = CONV_A_HALO - CONV_A_WIDTH // 2
    acc = None
    for kk in range(CONV_A_WIDTH):
        term = win[off + kk:off + kk + tm] * w[kk:kk + 1, :]
        acc = term if acc is None else acc + term
    return (_silu(_ln(acc + b, g, beta)),)


def fn_resln(h, r, g, b):
    return (_ln(DEEPNORM_ALPHA * h + r, g, b),)


def _shift_down(x):
    return jnp.concatenate([jnp.zeros((8, x.shape[1]), F32), x], axis=0)[7:7 + x.shape[0]]


def _shift_up(x):
    return jnp.concatenate([x, jnp.zeros((8, x.shape[1]), F32)], axis=0)[1:1 + x.shape[0]]


def fn_ffnconv(u1, u2, w1, w2, b1, b2):
    c1 = _shift_down(u1) * w1[0:1] + u1 * w1[1:2] + _shift_up(u1) * w1[2:3] + b1
    c2 = _shift_down(u2) * w2[0:1] + u2 * w2[1:2] + _shift_up(u2) * w2[2:3] + b2
    return (_silu(c1) * c2,)


def fn_final(h, r, t, g, b):
    y = _ln(DEEPNORM_ALPHA * h + r, g, b)
    err = (y - t) * (y - t)
    return (0.5 * jnp.sum(jnp.mean(err, axis=-1, keepdims=True), axis=0, keepdims=True),)


def _rope_tables(seq):
    pos = jnp.arange(seq, dtype=jnp.int32)
    row = (pos // GRID_W).astype(F32)
    col = (pos % GRID_W).astype(F32)

    def tab(half):
        inv = ROPE_THETA ** (-jnp.arange(half, dtype=F32) / half)
        ar, ac = row[:, None] * inv[None, :], col[:, None] * inv[None, :]
        return jnp.concatenate([jnp.cos(ar), jnp.sin(ar), jnp.cos(ac), jnp.sin(ac)], axis=-1)

    return tab(HEAD_DIM // 4), tab(MLA_ROPE // 4)


def _pre_ops(proj, tabs, kp, grad):
    tm = ROW_TILE
    ops = [_row_op(proj, tm, grad=grad, gdtype=MXU_DTYPE), _row_op(tabs[0], tm), _row_op(tabs[1], tm)]
    ops += [_par_op(kp[n], grad=grad) for n in ('qng', 'kng', 'sg', 'sb', 'sw', 'sbt', 'mqn', 'wuq', 'mkvn', 'wukv')]
    return ops


PRE_OUT_WIDTHS = (GROUP_W, GROUP_W, 128, 128, GROUP_W, MLA_HEADS * MLA_DK_PAD, MLA_HEADS * MLA_DK_PAD, GROUP_W)


def pre_fwd(proj, tabs, kp, tag):
    seq = proj.shape[0]
    tm = ROW_TILE
    dts = (F32,) + (MXU_DTYPE,) * 7
    outs = [((seq, w), dt, (tm, w), lambda i: (i, 0)) for w, dt in zip(PRE_OUT_WIDTHS, dts)]
    return stage_fwd("pre_fwd" + tag, fn_pre, _pre_ops(proj, tabs, kp, False), outs, (seq // tm,))


def pre_bwd(proj, tabs, kp, cts, tag):
    seq = proj.shape[0]
    tm = ROW_TILE
    ct = [[(c, (tm, c.shape[1]), lambda i: (i, 0))] for c in cts]
    return stage_bwd("pre_bwd" + tag, fn_pre, _pre_ops(proj, tabs, kp, True), ct, (seq // tm,))


def _aconv_ops(kp, grad):
    return [_par_op(kp[n], grad=grad) for n in ('caw', 'cab', 'lag', 'lab')]


def aconv_fwd(aglu_pad, kp, tag):
    seq = aglu_pad.shape[0] - 2 * CONV_A_HALO
    tm = ROW_TILE
    n_par = 4

    def body(x_ref, *refs):
        i = pl.program_id(0)
        win = x_ref[pl.ds(pl.multiple_of(i * tm, tm), tm + 2 * CONV_A_HALO), :]
        (o,) = fn_aconv(win, *[_load(r) for r in refs[:n_par]])
        refs[n_par][...] = o.astype(refs[n_par].dtype)

    pars = _aconv_ops(kp, False)
    return pl.pallas_call(
        body, grid=(seq // tm,),
        in_specs=[pl.BlockSpec(aglu_pad.shape, lambda i: (0, 0))] + [pl.BlockSpec(o.block, o.imap) for o in pars],
        out_specs=pl.BlockSpec((tm, GROUP_W), lambda i: (i, 0)),
        out_shape=jax.ShapeDtypeStruct((seq, GROUP_W), MXU_DTYPE),
        compiler_params=pltpu.CompilerParams(dimension_semantics=("parallel",), vmem_limit_bytes=VMEM_LIMIT),
        name="aconv_fwd" + tag)(aglu_pad, *[o.arr for o in pars])


def aconv_bwd(aglu_pad, kp, d_oa, tag):
    seq = aglu_pad.shape[0] - 2 * CONV_A_HALO
    tm = ROW_TILE
    n_par = 4

    def body(x_ref, *refs):
        i = pl.program_id(0)
        rows = pl.ds(pl.multiple_of(i * tm, tm), tm + 2 * CONV_A_HALO)
        pars = [_load(r) for r in refs[:n_par]]
        ct = refs[n_par][...].astype(F32)
        outs = refs[n_par + 1:]
        _, vjp = jax.vjp(lambda *a: fn_aconv(*a), x_ref[rows, :], *pars)
        grads = vjp((ct,))

        @pl.when(i == 0)
        def _():
            for r in outs:
                r[...] = jnp.zeros_like(r)

        outs[0][rows, :] += grads[0]
        for r, g in zip(outs[1:], grads[1:]):
            r[...] += g

    pars = _aconv_ops(kp, True)
    whole = pl.BlockSpec(aglu_pad.shape, lambda i: (0, 0))
    par_specs = [pl.BlockSpec(o.block, o.imap) for o in pars]
    return pl.pallas_call(
        body, grid=(seq // tm,),
        in_specs=[whole] + par_specs + [pl.BlockSpec((tm, GROUP_W), lambda i: (i, 0))],
        out_specs=[whole] + par_specs,
        out_shape=[jax.ShapeDtypeStruct(aglu_pad.shape, F32)] + [jax.ShapeDtypeStruct(o.arr.shape, F32) for o in pars],
        compiler_params=pltpu.CompilerParams(dimension_semantics=("arbitrary",), vmem_limit_bytes=VMEM_LIMIT),
        name="aconv_bwd" + tag)(aglu_pad, *[o.arr for o in pars], d_oa)


ATTN_TQ = 256
ATTN_TK = 512


def attn_fwd(q3, k3, v3, scale, tag):
    heads, seq, dk = q3.shape
    group = heads // k3.shape[0]
    dv = v3.shape[2]
    tq, tk = min(ATTN_TQ, seq), min(ATTN_TK, seq)

    n_chunks = seq // tk
    log2e = math.log2(math.e)

    def body(q_ref, k_ref, v_ref, o_ref, lse_ref):
        q = q_ref[...]
        scores = lambda c: _rawdot(q, k_ref[pl.ds(c * tk, tk), :], 1, 1)
        m, l, acc = jnp.full((tq, 1), -jnp.inf, F32), jnp.zeros((tq, 1), F32), jnp.zeros((tq, dv), F32)
        s_next = scores(0)
        for c in range(n_chunks):
            s_cur, s_next = s_next, (scores(c + 1) if c + 1 < n_chunks else None)
            t = s_cur * (scale * log2e)
            m_new = jnp.maximum(m, jnp.max(t, axis=-1, keepdims=True))
            alpha = jnp.exp2(m - m_new)
            p = jnp.exp2(t - m_new)
            l = alpha * l + jnp.sum(p, axis=-1, keepdims=True)
            acc = alpha * acc + _rawdot(p, v_ref[pl.ds(c * tk, tk), :], 1, 0)
            m = m_new
        o_ref[...] = acc * (1.0 / l)
        lse_ref[...] = m * (1.0 / log2e) + jnp.log(l)

    return pl.pallas_call(
        body, grid=(heads, seq // tq),
        in_specs=[pl.BlockSpec((None, tq, dk), lambda h, i: (h, i, 0)),
                  pl.BlockSpec((None, seq, dk), lambda h, i: (h // group, 0, 0)),
                  pl.BlockSpec((None, seq, dv), lambda h, i: (h // group, 0, 0))],
        out_specs=[pl.BlockSpec((None, tq, dv), lambda h, i: (h, i, 0)),
                   pl.BlockSpec((None, tq, 1), lambda h, i: (h, i, 0))],
        out_shape=[jax.ShapeDtypeStruct((heads, seq, dv), F32), jax.ShapeDtypeStruct((heads, seq, 1), F32)],
        compiler_params=pltpu.CompilerParams(dimension_semantics=("parallel", "parallel"),
                                             vmem_limit_bytes=VMEM_LIMIT),
        name="attn_fwd" + tag)(q3, k3, v3)


def attn_bwd(q3, k3, v3, o3, lse3, do3, scale, tag):
    heads, seq, dk = q3.shape
    group = heads // k3.shape[0]
    dv = v3.shape[2]
    tq, tk = min(ATTN_TQ, seq), min(ATTN_TK, seq)
    n_chunks = seq // tk
    log2e = math.log2(math.e)

    def body(q_ref, k_ref, v_ref, o_ref, lse_ref, do_ref, dq_ref, dk_ref, dv_ref):
        @pl.when(jnp.logical_and(pl.program_id(0) % group == 0, pl.program_id(1) == 0))
        def _():
            dk_ref[...] = jnp.zeros_like(dk_ref)
            dv_ref[...] = jnp.zeros_like(dv_ref)

        q, do = q_ref[...], do_ref[...]
        dob = do.astype(MXU_DTYPE)
        delta = jnp.sum(do * o_ref[...], axis=-1, keepdims=True)
        lse2 = lse_ref[...] * log2e
        rows = lambda c: pl.ds(c * tk, tk)
        products = lambda c: (_rawdot(q, k_ref[rows(c), :], 1, 1), _rawdot(dob, v_ref[rows(c), :], 1, 1))
        dq = jnp.zeros((tq, dk), F32)
        nxt = products(0)
        for c in range(n_chunks):
            (s_cur, dp_cur), nxt = nxt, (products(c + 1) if c + 1 < n_chunks else None)
            p = jnp.exp2(s_cur * (scale * log2e) - lse2)
            ds = (p * ((dp_cur - delta) * scale)).astype(MXU_DTYPE)
            dv_ref[rows(c), :] += _rawdot(p, dob, 0, 0)
            dk_ref[rows(c), :] += _rawdot(ds, q, 0, 0)
            dq = dq + _rawdot(ds, k_ref[rows(c), :], 1, 0)
        dq_ref[...] = dq

    qspec = lambda d: pl.BlockSpec((None, tq, d), lambda h, i: (h, i, 0))
    kvspec = lambda d: pl.BlockSpec((None, seq, d), lambda h, i: (h // group, 0, 0))
    return pl.pallas_call(
        body, grid=(heads, seq // tq),
        in_specs=[qspec(dk), kvspec(dk), kvspec(dv), qspec(dv), qspec(1), qspec(dv)],
        out_specs=[qspec(dk), kvspec(dk), kvspec(dv)],
        out_shape=[jax.ShapeDtypeStruct(q3.shape, F32), jax.ShapeDtypeStruct(k3.shape, F32),
                   jax.ShapeDtypeStruct(v3.shape, F32)],
        compiler_params=pltpu.CompilerParams(dimension_semantics=("arbitrary", "arbitrary"),
                                             vmem_limit_bytes=VMEM_LIMIT),
        name="attn_bwd" + tag)(q3, k3, v3, o3, lse3, do3)


def resln_fwd(h, r, g, b, tag):
    seq, d = h.shape
    tm = ROW_TILE
    ops = [_row_op(h, tm), _row_op(r, tm), _par_op(g), _par_op(b)]
    outs = [((seq, d), dt, (tm, d), lambda i: (i, 0)) for dt in (F32, MXU_DTYPE)]
    return stage_fwd("resln_fwd" + tag, _twice(fn_resln), ops, outs, (seq // tm,))


def resln_bwd(h, r, g, b, dys, tag):
    seq, d = h.shape
    tm = ROW_TILE
    ops = [_row_op(h, tm, grad=True), _row_op(r, tm, grad=True, gdtype=MXU_DTYPE), _par_op(g, grad=True),
           _par_op(b, grad=True)]
    ct = [[(dy, (tm, d), lambda i: (i, 0)) for dy in dys]]
    return stage_bwd("resln_bwd" + tag, fn_resln, ops, ct, (seq // tm,))


def _ffnconv_ops(up, w, b, grad):
    seq = up.shape[0]
    nblk = D_FF // 128
    lo, hi = (lambda j: (0, j)), (lambda j: (0, j + nblk))
    half = lambda a: dict(gshape=(a.shape[0], D_FF), gimap=lo)
    return [Op(up, (seq, 128), lo, grad=grad, gdtype=MXU_DTYPE, **half(up)),
            Op(up, (seq, 128), hi, grad=grad, gdtype=MXU_DTYPE, **half(up)),
            Op(w, (3, 128), lo, grad=grad, **half(w)), Op(w, (3, 128), hi, grad=grad, **half(w)),
            Op(b, (1, 128), lo, grad=grad, **half(b)), Op(b, (1, 128), hi, grad=grad, **half(b))]


def ffnconv_fwd(up, w, b, tag):
    seq = up.shape[0]
    outs = [((seq, D_FF), MXU_DTYPE, (seq, 128), lambda j: (0, j))]
    return stage_fwd("ffnconv_fwd" + tag, fn_ffnconv, _ffnconv_ops(up, w, b, False), outs, (D_FF // 128,))[0]


def ffnconv_bwd(up, w, b, dact, tag):
    seq = up.shape[0]
    ct = [[(dact, (seq, 128), lambda j: (0, j))]]
    du1, du2, dw1, dw2, db1, db2 = stage_bwd("ffnconv_bwd" + tag, fn_ffnconv, _ffnconv_ops(up, w, b, True), ct,
                                             (D_FF // 128,))
    cat = lambda a, b_: jnp.concatenate([a, b_], axis=-1)
    return cat(du1, du2), cat(dw1, dw2), cat(db1, db2)


def final_bwd(h, r, t, g, b, tag):
    seq, d = h.shape
    tm = ROW_TILE
    ops = [_row_op(h, tm, grad=True), _row_op(r, tm, grad=True, gdtype=MXU_DTYPE), _row_op(t, tm),
           _par_op(g, grad=True), _par_op(b, grad=True)]
    return stage_bwd("final_bwd" + tag, fn_final, ops, [None], (seq // tm,), value_acc=True)


def _layer_params(wts, l):
    row = lambda a: a.reshape(1, -1)
    return dict(
        qng=row(wts['qk_norm_q'][l]), kng=row(wts['qk_norm_k'][l]),
        sg=row(wts['sgu_ln_g'][l]), sb=row(wts['sgu_ln_b'][l]),
        sw=wts['sgu_w'][l].reshape(SGU_GROUPS * CHUNK, CHUNK), sbt=wts['sgu_b'][l].T,
        mqn=row(wts['mla_q_norm'][l]), wuq=wts['mla_w_uq'][l], mkvn=row(wts['mla_kv_norm'][l]),
        wukv=wts['mla_w_ukv'][l],
        caw=wts['conv_a_w'][l], cab=row(wts['conv_a_b'][l]), lag=row(wts['ln_a_g'][l]), lab=row(wts['ln_a_b'][l]),
        lmg=row(wts['ln_mix_g'][l]), lmb=row(wts['ln_mix_b'][l]),
        fcw=wts['ffn_conv_w'][l], fcb=row(wts['ffn_conv_b'][l]),
        lfg=row(wts['ln_ffn_g'][l]), lfb=row(wts['ln_ffn_b'][l]))


def _to_heads(a, heads):
    seq = a.shape[0]
    return a.reshape(seq, heads, -1).transpose(1, 0, 2)


def _from_heads(a3):
    return a3.transpose(1, 0, 2).reshape(a3.shape[1], -1)


def local_step(x, target, wts, mat, hook):
    seq = x.shape[0]
    tm = ROW_TILE
    tabs = _rope_tables(seq)
    scale_b = HEAD_DIM ** -0.5
    scale_d = (MLA_NOPE + MLA_ROPE) ** -0.5
    ln_in_g, ln_in_b = wts['ln_in_g'].reshape(1, -1), wts['ln_in_b'].reshape(1, -1)

    h, h_m = stage_fwd("ln_in_fwd", _twice(fn_ln), [_row_op(x, tm), _par_op(ln_in_g), _par_op(ln_in_b)],
                       [((seq, D_MODEL), dt, (tm, D_MODEL), lambda i: (i, 0)) for dt in (F32, MXU_DTYPE)],
                       (seq // tm,))
    saved = []
    for l in range(DEPTH):
        tag = f"_l{l}"
        kp, unprep = jax.vjp(lambda w: _layer_params(w, l), wts)
        m = {'w_in': mat(l, 'w_in', h_m)}
        proj = matmul(h_m, m['w_in'], 'nn', F32, "mm_proj" + tag)
        aglu, q, k, v, o_c, qf, kf, vd = pre_fwd(proj, tabs, kp, tag)
        aglu_pad = jnp.pad(aglu, ((CONV_A_HALO, CONV_A_HALO), (0, 0)))
        o_a = aconv_fwd(aglu_pad, kp, tag)
        q3, k3, v3 = _to_heads(q, GQA_HEADS), _to_heads(k, GQA_KV_HEADS), _to_heads(v, GQA_KV_HEADS)
        o_b3, lse_b3 = attn_fwd(q3, k3, v3, scale_b, "_b" + tag)
        qd3, kd3, vd3 = _to_heads(qf, MLA_HEADS), _to_heads(kf, MLA_HEADS), _to_heads(vd, MLA_HEADS)
        o_d3, lse_d3 = attn_fwd(qd3, kd3, vd3, scale_d, "_d" + tag)
        o_cat = jnp.concatenate([o_a, _from_heads(o_b3).astype(MXU_DTYPE), o_c,
                                 _from_heads(o_d3).astype(MXU_DTYPE)], axis=-1)
        m['w_out'] = mat(l, 'w_out', o_cat)
        mix = matmul(o_cat, m['w_out'], 'nn', F32, "mm_mix" + tag)
        h1, h1_m = resln_fwd(h, mix, kp['lmg'], kp['lmb'], "_mix" + tag)
        m['ffn_w_up'] = mat(l, 'ffn_w_up', h1_m)
        up = matmul(h1_m, m['ffn_w_up'], 'nn', F32, "mm_up" + tag)
        act = ffnconv_fwd(up, kp['fcw'], kp['fcb'], tag)
        m['ffn_w_down'] = mat(l, 'ffn_w_down', act)
        f = matmul(act, m['ffn_w_down'], 'nn', F32, "mm_down" + tag)
        saved.append(dict(kp=kp, unprep=unprep, m=m, h=h, h_m=h_m, h1_m=h1_m, proj=proj, o_b3=o_b3, lse_b3=lse_b3,
                          o_d3=o_d3, lse_d3=lse_d3, aglu_pad=aglu_pad, q3=q3, k3=k3, v3=v3, qd3=qd3,
                          kd3=kd3, vd3=vd3, o_cat=o_cat, mix=mix, h1=h1, up=up, act=act, f=f))
        if l + 1 < DEPTH:
            h, h_m = resln_fwd(h1, f, kp['lfg'], kp['lfb'], "_ffn" + tag)

    after = lambda a, tok: a if tok is None else a + tok
    small_acc = None
    dh_parts = None
    loss = None
    tok = None
    g_mix = None
    for l in reversed(range(DEPTH)):
        tag = f"_l{l}"
        s = saved[l]
        kp, m = s['kp'], s['m']
        dkp = {}
        lfg = after(kp['lfg'], tok)
        if l == DEPTH - 1:
            dh1_a, df, dkp['lfg'], dkp['lfb'], loss = final_bwd(s['h1'], s['f'], target, lfg, kp['lfb'], tag)
        else:
            dh1_a, df, dkp['lfg'], dkp['lfb'] = resln_bwd(s['h1'], s['f'], lfg, kp['lfb'], dh_parts, "_ffn" + tag)
        dact = matmul(df, m['ffn_w_down'], 'nt', F32, "mm_dact" + tag)
        g_down = matmul(s['act'], df, 'tn', F32, "mm_gdown" + tag)
        dup, dkp['fcw'], dkp['fcb'] = ffnconv_bwd(s['up'], kp['fcw'], kp['fcb'], dact, tag)
        dh1_b = matmul(dup, m['ffn_w_up'], 'nt', F32, "mm_dh1" + tag)
        g_up = matmul(s['h1_m'], dup, 'tn', F32, "mm_gup" + tag)
        tok = hook(f"ffn{l}", {('ffn_w_down', l): g_down, ('ffn_w_up', l): g_up})
        dh_a, dmix, dkp['lmg'], dkp['lmb'] = resln_bwd(s['h'], s['mix'], after(kp['lmg'], tok), kp['lmb'],
                                                       [dh1_a, dh1_b], "_mix" + tag)
        do_cat = matmul(dmix, m['w_out'], 'nt', F32, "mm_docat" + tag)
        g_out = matmul(s['o_cat'], dmix, 'tn', F32, "mm_gout" + tag)
        do_a, do_b, do_c, do_d = (do_cat[:, GROUP_W * j:GROUP_W * (j + 1)] for j in range(4))
        dq3, dk3, dv3 = attn_bwd(s['q3'], s['k3'], s['v3'], s['o_b3'], s['lse_b3'], _to_heads(do_b, GQA_HEADS),
                                 scale_b, "_b" + tag)
        dqd3, dkd3, dvd3 = attn_bwd(s['qd3'], s['kd3'], s['vd3'], s['o_d3'], s['lse_d3'],
                                    _to_heads(do_d, MLA_HEADS), scale_d, "_d" + tag)
        daglu_pad, dkp['caw'], dkp['cab'], dkp['lag'], dkp['lab'] = aconv_bwd(s['aglu_pad'], kp, do_a, tag)
        cts = [daglu_pad[CONV_A_HALO:CONV_A_HALO + seq], _from_heads(dq3), _from_heads(dk3), _from_heads(dv3), do_c,
               _from_heads(dqd3), _from_heads(dkd3), _from_heads(dvd3)]
        pre_g = pre_bwd(s['proj'], tabs, kp, cts, tag)
        dproj = pre_g[0]
        for n, g in zip(('qng', 'kng', 'sg', 'sb', 'sw', 'sbt', 'mqn', 'wuq', 'mkvn', 'wukv'), pre_g[1:]):
            dkp[n] = g
        dh_b = matmul(dproj, m['w_in'], 'nt', F32, "mm_dh" + tag)
        g_in = matmul(s['h_m'], dproj, 'tn', F32, "mm_gin" + tag)
        dh_parts = [dh_a, dh_b]
        (dw,) = s['unprep'](dkp)
        small_acc = dw if small_acc is None else jax.tree.map(jnp.add, small_acc, dw)
        g_mix = {('w_out', l): g_out, ('w_in', l): g_in}
        if l > 0:
            tok = hook(f"mix{l}", g_mix)

    g_mix.update({(n, None): small_acc[n] for n in SHARDED if n not in MATMUL_WEIGHTS})
    tok = hook("last", g_mix)
    dx, dg, db = stage_bwd("ln_in_bwd", fn_ln,
                           [_row_op(x, tm, grad=True), _par_op(after(ln_in_g, tok), grad=True),
                            _par_op(ln_in_b, grad=True)],
                           [[(p, (tm, D_MODEL), lambda i: (i, 0)) for p in dh_parts]], (seq // tm,))
    out = {n: small_acc[n] for n in REPLICATED}
    out['ln_in_g'], out['ln_in_b'] = dg.reshape(-1), db.reshape(-1)
    return loss, dx, out


def _peer(x, y, c, r):
    return ((1 - x) if r & 4 else x, (1 - y) if r & 2 else y, (1 - c) if r & 1 else c)


def exchange(arrs, gather, name):
    n_t = len(arrs)
    hbm = pl.BlockSpec(memory_space=pltpu.HBM)

    def body(*refs):
        ins, outs = refs[:n_t], refs[n_t:2 * n_t]
        send_sems, recv_sems, local_sems = refs[2 * n_t:]
        x, y, c = lax.axis_index("x"), lax.axis_index("y"), lax.axis_index("c")
        me = 4 * x + 2 * y + c

        def remote(t, r):
            px, py, pc = _peer(x, y, c, r)
            peer = 4 * px + 2 * py + pc
            src = ins[t] if gather else ins[t].at[peer]
            send = pltpu.make_async_remote_copy(src_ref=src, dst_ref=outs[t].at[me], send_sem=send_sems.at[t, r - 1],
                                                recv_sem=recv_sems.at[t, r - 1], device_id=(px, py, pc),
                                                device_id_type=pl.DeviceIdType.MESH)
            recv = pltpu.make_async_remote_copy(src_ref=src, dst_ref=outs[t].at[peer], send_sem=send_sems.at[t, r - 1],
                                                recv_sem=recv_sems.at[t, r - 1], device_id=(px, py, pc),
                                                device_id_type=pl.DeviceIdType.MESH)
            return send, recv

        locs = []
        for t in range(n_t):
            loc = pltpu.make_async_copy(ins[t] if gather else ins[t].at[me], outs[t].at[me], local_sems.at[t])
            loc.start()
            locs.append(loc)
        for t in range(n_t):
            for r in range(1, N_DEV):
                remote(t, r)[0].start()
        for t in range(n_t):
            for r in range(1, N_DEV):
                send, recv = remote(t, r)
                recv.wait_recv()
                send.wait_send()
        for loc in locs:
            loc.wait()

    out_shape = [jax.ShapeDtypeStruct(((N_DEV,) + a.shape) if gather else a.shape, a.dtype) for a in arrs]
    return pl.pallas_call(
        body, in_specs=[hbm] * n_t, out_specs=[hbm] * n_t, out_shape=out_shape,
        scratch_shapes=[pltpu.SemaphoreType.DMA((n_t, N_DEV - 1)), pltpu.SemaphoreType.DMA((n_t, N_DEV - 1)),
                        pltpu.SemaphoreType.DMA((n_t,))],
        name=name)(*arrs)


def _exchange_copy(src_ref, land_ref, send_sems, recv_sems, k, gather, x, y, c, r):
    px, py, pc = _peer(x, y, c, r)
    me, peer = 4 * x + 2 * y + c, 4 * px + 2 * py + pc
    src = src_ref if gather else src_ref.at[peer]
    mk = lambda dst: pltpu.make_async_remote_copy(
        src_ref=src, dst_ref=dst, send_sem=send_sems.at[k * (N_DEV - 1) + r - 1],
        recv_sem=recv_sems.at[k * (N_DEV - 1) + r - 1],
        device_id=(px, py, pc), device_id_type=pl.DeviceIdType.MESH)
    return mk(land_ref.at[me]), mk(land_ref.at[peer])


_HBM_SPEC = pl.BlockSpec(memory_space=pltpu.HBM)
_SEM_SPEC = pl.BlockSpec(memory_space=pltpu.SEMAPHORE)


def exchange_start(srcs, gather, groups, name):
    n_t = len(srcs)
    lands =[lax.empty(((N_DEV,) + s.shape) if gt else s.shape, s.dtype) for s, gt in zip(srcs, gather)]

    def body(*refs):
        src_refs, land_refs = refs[:n_t], refs[n_t:2 * n_t]
        sem_refs = refs[2 * n_t:2 * n_t + 2 * len(groups)]
        token = refs[-1]
        x, y, c = lax.axis_index("x"), lax.axis_index("y"), lax.axis_index("c")
        for gi, g in enumerate(groups):
            for k, t in enumerate(g):
                for r in range(1, N_DEV):
                    _exchange_copy(src_refs[t], land_refs[t], sem_refs[2 * gi], sem_refs[2 * gi + 1], k, gather[t],
                                   x, y, c, r)[0].start()
        token[...] = jnp.zeros_like(token)

    sem_shapes = []
    for g in groups:
        sem_shapes += [pltpu.SemaphoreType.DMA((len(g) * (N_DEV - 1),))] * 2
    hbm_shapes = [pltpu.HBM(a.shape, a.dtype) for a in list(srcs) + lands]
    n_sem = len(sem_shapes)
    res = pl.pallas_call(
        body, name=name,
        out_shape=tuple(sem_shapes + hbm_shapes + [jax.ShapeDtypeStruct((8, 128), F32)]),
        in_specs=[_HBM_SPEC] * (2 * n_t),
        out_specs=tuple([_SEM_SPEC] * n_sem + [_HBM_SPEC] * (2 * n_t) + [pl.BlockSpec(memory_space=pltpu.VMEM)]),
        input_output_aliases={i: n_sem + i for i in range(2 * n_t)},
        compiler_params=pltpu.CompilerParams(has_side_effects=pltpu.SideEffectType.DATAFLOW_SIDE_EFFECTING),
    )(*[pltpu.with_memory_space_constraint(a, pltpu.HBM) for a in list(srcs) + lands])
    sems = [(res[2 * gi], res[2 * gi + 1]) for gi in range(len(groups))]
    return sems, list(res[n_sem:n_sem + n_t]), list(res[n_sem + n_t:n_sem + 2 * n_t]), res[-1][0, 0]


def exchange_wait(sems, srcs, lands, gather, after, name):
    n_t = len(srcs)

    def body(*refs):
        src_refs, land_refs = refs[:n_t], refs[n_t:2 * n_t]
        send_sems, recv_sems = refs[2 * n_t], refs[2 * n_t + 1]
        x, y, c = lax.axis_index("x"), lax.axis_index("y"), lax.axis_index("c")
        for k in range(n_t):
            for r in range(1, N_DEV):
                send, recv = _exchange_copy(src_refs[k], land_refs[k], send_sems, recv_sems, k, gather[k], x, y, c, r)
                send.wait_send()
                recv.wait_recv()

    res = pl.pallas_call(
        body, name=name,
        out_shape=tuple(pltpu.HBM(a.shape, a.dtype) for a in list(srcs) + list(lands)),
        in_specs=[_HBM_SPEC] * (2 * n_t) + [_SEM_SPEC, _SEM_SPEC, pl.BlockSpec(memory_space=pl.ANY)],
        out_specs=tuple([_HBM_SPEC] * (2 * n_t)),
        input_output_aliases={i: i for i in range(2 * n_t)},
        compiler_params=pltpu.CompilerParams(has_side_effects=pltpu.SideEffectType.DATAFLOW_SIDE_EFFECTING),
    )(*srcs, *lands, sems[0], sems[1], after)
    return list(res[:n_t]), list(res[n_t:])


def adamw(parts, w, m, v, name):
    n_l, n_r, n_c = w.shape
    tr = n_r
    if n_r % 8 == 0:
        for cand in (512, 256, 128, 64, 32, 16, 8):
            if n_r % cand == 0 and cand * n_c * 4 <= 512 * 1024:
                tr = cand
                break
    c1 = 1.0 - ADAM_B1 ** ADAM_STEP
    c2 = 1.0 - ADAM_B2 ** ADAM_STEP
    per_layer = isinstance(parts, (list, tuple))
    n_p = n_l if per_layer else 1
    n_rb = n_r // tr

    def update(g, w_ref, m_ref, v_ref, g_ref, d_ref, nm_ref, nv_ref):
        w_, m_, v_ = w_ref[0], m_ref[0], v_ref[0]
        nm = ADAM_B1 * m_ + (1.0 - ADAM_B1) * g
        nv = ADAM_B2 * v_ + (1.0 - ADAM_B2) * (g * g)
        g_ref[0] = g
        nm_ref[0] = nm
        nv_ref[0] = nv
        d_ref[0] = -ADAM_LR * ((nm / c1) / (jnp.sqrt(nv / c2) + ADAM_EPS) + ADAM_WD * w_)

    def body(*refs):
        p_refs, rest = refs[:n_p], refs[n_p:]
        if not per_layer:
            g = p_refs[0][0, 0].astype(F32)
            for s in range(1, N_DEV):
                g = g + p_refs[0][s, 0].astype(F32)
            update(g, *rest)
        else:
            for lay in range(n_l):
                @pl.when(pl.program_id(0) == lay)
                def _(lay=lay):
                    g = p_refs[lay][0].astype(F32)
                    for s in range(1, N_DEV):
                        g = g + p_refs[lay][s].astype(F32)
                    update(g, *rest)

    blk = pl.BlockSpec((1, tr, n_c), lambda l, r: (l, r, 0))
    if per_layer:
        def p_spec(lay):
            park = 0 if lay > 0 else n_rb - 1
            return pl.BlockSpec((N_DEV, tr, n_c), lambda l, r: (0, jnp.where(l == lay, r, park), 0))
        p_specs, p_args = [p_spec(lay) for lay in range(n_l)], list(parts)
    else:
        p_specs, p_args = [pl.BlockSpec((N_DEV, 1, tr, n_c), lambda l, r: (0, l, r, 0))], [parts]
    return pl.pallas_call(
        body, grid=(n_l, n_rb), in_specs=p_specs + [blk, blk, blk],
        out_specs=[blk] * 4, out_shape=[jax.ShapeDtypeStruct(w.shape, F32)] * 4,
        compiler_params=pltpu.CompilerParams(dimension_semantics=("arbitrary", "arbitrary"),
                                             vmem_limit_bytes=VMEM_LIMIT),
        name=name)(*p_args, w, m, v)


PACK_UNIT = 1024
PACK_ROWS = 256


def _pack(pieces):
    rows = []
    for p in pieces:
        flat = p.reshape(-1).astype(F32)
        pad = (-flat.shape[0]) % PACK_UNIT
        rows.append(jnp.pad(flat, (0, pad)).reshape(-1, 128))
    n = sum(r.shape[0] for r in rows)
    rows.append(jnp.zeros(((-n) % PACK_ROWS, 128), F32))
    return jnp.concatenate(rows, axis=0)


def _unpack(packed, shapes):
    out, r0 = [], 0
    for shp in shapes:
        n = math.prod(shp)
        nr = -(-n // PACK_UNIT) * (PACK_UNIT // 128)
        out.append(packed[r0:r0 + nr].reshape(-1)[:n].reshape(shp))
        r0 += nr
    return out


def _shard_slots(g, axis):
    if axis == 1:
        return g.reshape(g.shape[0], N_DEV, g.shape[1] // N_DEV, g.shape[2]).transpose(1, 0, 2, 3)
    return g.reshape(g.shape[0], g.shape[1], N_DEV, g.shape[2] // N_DEV).transpose(2, 0, 1, 3)


def _unshard(slots, axis):
    if axis == 1:
        return slots.transpose(1, 0, 2, 3).reshape(slots.shape[1], -1, slots.shape[3])
    return slots.transpose(1, 2, 0, 3).reshape(slots.shape[1], slots.shape[2], -1)


def kernel(x, ln_in_g, ln_in_b, w_in, conv_a_w, conv_a_b, ln_a_g, ln_a_b, qk_norm_q, qk_norm_k, sgu_ln_g, sgu_ln_b, sgu_w, sgu_b, mla_q_norm, mla_w_uq, mla_kv_norm, mla_w_ukv, w_out, ln_mix_g, ln_mix_b, ffn_w_up, ffn_conv_w, ffn_conv_b, ffn_w_down, ln_ffn_g, ln_ffn_b, loss_target, m_ln_in_g, m_ln_in_b, m_w_in, m_conv_a_w, m_conv_a_b, m_ln_a_g, m_ln_a_b, m_qk_norm_q, m_qk_norm_k, m_sgu_ln_g, m_sgu_ln_b, m_sgu_w, m_sgu_b, m_mla_q_norm, m_mla_w_uq, m_mla_kv_norm, m_mla_w_ukv, m_w_out, m_ln_mix_g, m_ln_mix_b, m_ffn_w_up, m_ffn_conv_w, m_ffn_conv_b, m_ffn_w_down, m_ln_ffn_g, m_ln_ffn_b, v_ln_in_g, v_ln_in_b, v_w_in, v_conv_a_w, v_conv_a_b, v_ln_a_g, v_ln_a_b, v_qk_norm_q, v_qk_norm_k, v_sgu_ln_g, v_sgu_ln_b, v_sgu_w, v_sgu_b, v_mla_q_norm, v_mla_w_uq, v_mla_kv_norm, v_mla_w_ukv, v_w_out, v_ln_mix_g, v_ln_mix_b, v_ffn_w_up, v_ffn_conv_w, v_ffn_conv_b, v_ffn_w_down, v_ln_ffn_g, v_ln_ffn_b):
    local = dict(ln_in_g=ln_in_g, ln_in_b=ln_in_b, w_in=w_in, conv_a_w=conv_a_w, conv_a_b=conv_a_b, ln_a_g=ln_a_g, ln_a_b=ln_a_b, qk_norm_q=qk_norm_q, qk_norm_k=qk_norm_k, sgu_ln_g=sgu_ln_g, sgu_ln_b=sgu_ln_b, sgu_w=sgu_w, sgu_b=sgu_b, mla_q_norm=mla_q_norm, mla_w_uq=mla_w_uq, mla_kv_norm=mla_kv_norm, mla_w_ukv=mla_w_ukv, w_out=w_out, ln_mix_g=ln_mix_g, ln_mix_b=ln_mix_b, ffn_w_up=ffn_w_up, ffn_conv_w=ffn_conv_w, ffn_conv_b=ffn_conv_b, ffn_w_down=ffn_w_down, ln_ffn_g=ln_ffn_g, ln_ffn_b=ln_ffn_b)
    mom = dict(ln_in_g=m_ln_in_g, ln_in_b=m_ln_in_b, w_in=m_w_in, conv_a_w=m_conv_a_w, conv_a_b=m_conv_a_b, ln_a_g=m_ln_a_g, ln_a_b=m_ln_a_b, qk_norm_q=m_qk_norm_q, qk_norm_k=m_qk_norm_k, sgu_ln_g=m_sgu_ln_g, sgu_ln_b=m_sgu_ln_b, sgu_w=m_sgu_w, sgu_b=m_sgu_b, mla_q_norm=m_mla_q_norm, mla_w_uq=m_mla_w_uq, mla_kv_norm=m_mla_kv_norm, mla_w_ukv=m_mla_w_ukv, w_out=m_w_out, ln_mix_g=m_ln_mix_g, ln_mix_b=m_ln_mix_b, ffn_w_up=m_ffn_w_up, ffn_conv_w=m_ffn_conv_w, ffn_conv_b=m_ffn_conv_b, ffn_w_down=m_ffn_w_down, ln_ffn_g=m_ln_ffn_g, ln_ffn_b=m_ln_ffn_b)
    var = dict(ln_in_g=v_ln_in_g, ln_in_b=v_ln_in_b, w_in=v_w_in, conv_a_w=v_conv_a_w, conv_a_b=v_conv_a_b, ln_a_g=v_ln_a_g, ln_a_b=v_ln_a_b, qk_norm_q=v_qk_norm_q, qk_norm_k=v_qk_norm_k, sgu_ln_g=v_sgu_ln_g, sgu_ln_b=v_sgu_ln_b, sgu_w=v_sgu_w, sgu_b=v_sgu_b, mla_q_norm=v_mla_q_norm, mla_w_uq=v_mla_w_uq, mla_kv_norm=v_mla_kv_norm, mla_w_ukv=v_mla_w_ukv, w_out=v_w_out, ln_mix_g=v_ln_mix_g, ln_mix_b=v_ln_mix_b, ffn_w_up=v_ffn_w_up, ffn_conv_w=v_ffn_conv_w, ffn_conv_b=v_ffn_conv_b, ffn_w_down=v_ffn_w_down, ln_ffn_g=v_ln_ffn_g, ln_ffn_b=v_ln_ffn_b)

    me = 4 * lax.axis_index("x") + 2 * lax.axis_index("y") + lax.axis_index("c")

    def own_slot(slots, block):
        return lax.dynamic_update_slice(slots, block[None], (me,) + (0,) * block.ndim)

    small_sharded = [n for n in SHARDED if n not in MATMUL_WEIGHTS]
    big_order = [(n, l) for l in range(DEPTH) for n in MATMUL_WEIGHTS]
    srcs = [local['w_in'][0].astype(COMM_DTYPE)] + [local[n] for n in small_sharded]
    srcs += [local[n][l].astype(COMM_DTYPE) for (n, l) in big_order[1:]]
    n_first = 1 + len(small_sharded)
    groups = [list(range(n_first))] + [[n_first + j] for j in range(len(big_order) - 1)]
    g_sems, g_srcs, g_lands, tok0 = exchange_start(srcs, [True] * len(srcs), groups, "gather_start")
    pending = {key: gi for gi, key in enumerate(big_order)}

    def finish(gi, after):
        idx = groups[gi]
        _, lands = exchange_wait(g_sems[gi], [g_srcs[t] for t in idx], [g_lands[t] for t in idx], [True] * len(idx),
                                 after, f"gather_wait{gi}")
        return [own_slot(ld, srcs[t]) for ld, t in zip(lands, idx)]

    first = finish(0, local['ln_in_g'] + tok0)
    wts = {n: local[n] for n in REPLICATED}
    wts['ln_in_g'] = local['ln_in_g'] + tok0
    for n, slots in zip(small_sharded, first[1:]):
        wts[n] = _unshard(slots, SHARDED[n])

    def unshard_layer(slots, n):
        if SHARDED[n] == 1:
            return slots.reshape(-1, slots.shape[2])
        return slots.transpose(1, 0, 2).reshape(slots.shape[1], -1)

    def mat(l, n, after):
        gi = pending[(n, l)]
        slots = first[0] if gi == 0 else finish(gi, after)[0]
        return unshard_layer(slots, n).astype(MXU_DTYPE)

    started = []

    def hook(key, grads):
        tensors = []
        for (n, l), g in grads.items():
            if l is None:
                tensors.append(((n, l), _shard_slots(g, SHARDED[n])))
            else:
                tensors.append(((n, l), _shard_slots(g[None], SHARDED[n])[:, 0].astype(COMM_DTYPE)))
        sems, s_srcs, s_lands, tok = exchange_start([a for _, a in tensors], [False] * len(tensors),
                                                    [list(range(len(tensors)))], "scatter_start_" + key)
        started.append((key, [k for k, _ in tensors], sems[0], s_srcs, s_lands))
        return tok

    loss, dx, grads = local_step(x[0], loss_target[0], wts, mat, hook)

    parts = {}
    for key, keys, sems, s_srcs, s_lands in started:
        s_out, lands = exchange_wait(sems, s_srcs, s_lands, [False] * len(keys), dx, "scatter_wait_" + key)
        for k, so, ld in zip(keys, s_out, lands):
            parts[k] = own_slot(ld, lax.dynamic_index_in_dim(so, me, 0, keepdims=False))
    res = {}
    for n in SHARDED:
        p = [parts[(n, l)] for l in range(DEPTH)] if n in MATMUL_WEIGHTS else parts[(n, None)]
        res[n] = adamw(p, local[n], mom[n], var[n], "adamw_" + n)

    pieces = [grads[n] for n in REPLICATED] + [loss]
    packed = _pack(pieces)
    (gath,) = exchange([packed], True, "gather_small")
    zeros1 = jnp.zeros((1, 1), F32)
    pk = lambda d: _pack([d[n] for n in REPLICATED] + [zeros1])[None]
    g_s, d_s, m_s, v_s = adamw(gath[:, None], pk(local), pk(mom), pk(var), "adamw_small")
    shapes = [local[n].shape for n in REPLICATED] + [(1, 1)]
    unp = [_unpack(a[0], shapes) for a in (g_s, d_s, m_s, v_s)]
    for j, n in enumerate(REPLICATED):
        res[n] = tuple(u[j] for u in unp)
    loss_total = unp[0][-1].reshape(())

    return (loss_total, dx[None], *[res[n][0] for n in WEIGHTS], *[res[n][1] for n in WEIGHTS],
            *[res[n][2] for n in WEIGHTS], *[res[n][3] for n in WEIGHTS])
```

```python
import functools
import math

import jax
import jax.numpy as jnp
from jax import lax
from jax.experimental import pallas as pl
from jax.experimental.pallas import tpu as pltpu

F32 = jnp.float32
MXU_DTYPE = jnp.bfloat16
COMM_DTYPE = jnp.bfloat16

N_DEV = 8
D_MODEL = 1024
DEPTH = 2
GRID_W = 64
GROUP_W = 256
HEAD_DIM = 64
CONV_A_WIDTH = 31
CONV_A_HALO = 16
GQA_HEADS = 4
GQA_KV_HEADS = 2
CHUNK = 128
SGU_GROUPS = 4
MLA_HEADS = 4
MLA_Q_LORA = 192
MLA_KV_LORA = 128
MLA_NOPE = 64
MLA_ROPE = 32
MLA_V = 64
MLA_DK_PAD = 128
ROPE_THETA = 10000.0
D_FF = 2816
DEEPNORM_ALPHA = (2 * DEPTH) ** 0.25
LN_EPS = 1e-5
RMS_EPS = 1e-6
D_IN_PROJ = 1888

ADAM_LR = 0.001
ADAM_B1 = 0.9
ADAM_B2 = 0.999
ADAM_EPS = 1e-08
ADAM_WD = 0.01
ADAM_STEP = 10

WEIGHTS = ['ln_in_g', 'ln_in_b', 'w_in', 'conv_a_w', 'conv_a_b', 'ln_a_g', 'ln_a_b', 'qk_norm_q', 'qk_norm_k',
           'sgu_ln_g', 'sgu_ln_b', 'sgu_w', 'sgu_b', 'mla_q_norm', 'mla_w_uq', 'mla_kv_norm', 'mla_w_ukv', 'w_out',
           'ln_mix_g', 'ln_mix_b', 'ffn_w_up', 'ffn_conv_w', 'ffn_conv_b', 'ffn_w_down', 'ln_ffn_g', 'ln_ffn_b']
SHARDED = {'w_in': 2, 'conv_a_w': 2, 'mla_w_uq': 2, 'mla_w_ukv': 2, 'w_out': 1, 'ffn_w_up': 2, 'ffn_conv_w': 2,
           'ffn_w_down': 1}
MATMUL_WEIGHTS = ('w_in', 'w_out', 'ffn_w_up', 'ffn_w_down')
REPLICATED = [n for n in WEIGHTS if n not in SHARDED]

ROW_TILE = 256
VMEM_LIMIT = 56 * 1024 * 1024


def _rawdot(a, b, ca, cb):
    return lax.dot_general(a.astype(MXU_DTYPE), b.astype(MXU_DTYPE), (((ca,), (cb,)), ((), ())),
                           preferred_element_type=F32)


@jax.custom_vjp
def mm_nn(a, b):
    return _rawdot(a, b, 1, 0)


def _mm_nn_fwd(a, b):
    return _rawdot(a, b, 1, 0), (a, b)


def _mm_nn_bwd(res, dy):
    a, b = res
    return _rawdot(dy, b, 1, 1), _rawdot(a, dy, 0, 0)


mm_nn.defvjp(_mm_nn_fwd, _mm_nn_bwd)


@jax.custom_vjp
def mm_nt(a, b):
    return _rawdot(a, b, 1, 1)


def _mm_nt_fwd(a, b):
    return _rawdot(a, b, 1, 1), (a, b)


def _mm_nt_bwd(res, dy):
    a, b = res
    return _rawdot(dy, b, 1, 0), _rawdot(dy, a, 0, 0)


mm_nt.defvjp(_mm_nt_fwd, _mm_nt_bwd)


def _pick_tile(d, cands):
    for c in cands:
        if d % c == 0:
            return c
    return d


def matmul(a, b, mode, out_dtype, name):
    if mode == 'nn':
        (m, k), (k2, n) = a.shape, b.shape
    elif mode == 'nt':
        (m, k), (n, k2) = a.shape, b.shape
    else:
        (k, m), (k2, n) = a.shape, b.shape
    assert k == k2, (a.shape, b.shape, mode)
    tm = _pick_tile(m, (1024, 1408, 512, 256, 128))
    tn = _pick_tile(n, (512, 256, 128))
    tk = _pick_tile(k, (2816, 2048, 1024, 512, 256, 128))
    nk = k // tk
    ca = 0 if mode == 'tn' else 1
    cb = 1 if mode == 'nt' else 0
    a_spec = pl.BlockSpec((tk, tm), lambda i, j, kk: (kk, i)) if mode == 'tn' else pl.BlockSpec((tm, tk), lambda i, j, kk: (i, kk))
    b_spec = pl.BlockSpec((tn, tk), lambda i, j, kk: (j, kk)) if mode == 'nt' else pl.BlockSpec((tk, tn), lambda i, j, kk: (kk, j))

    def body(a_ref, b_ref, o_ref, acc_ref):
        kk = pl.program_id(2)

        @pl.when(kk == 0)
        def _():
            acc_ref[...] = jnp.zeros_like(acc_ref)

        acc_ref[...] += _rawdot(a_ref[...], b_ref[...], ca, cb)

        @pl.when(kk == nk - 1)
        def _():
            o_ref[...] = acc_ref[...].astype(o_ref.dtype)

    return pl.pallas_call(
        body, grid=(m // tm, n // tn, nk), in_specs=[a_spec, b_spec],
        out_specs=pl.BlockSpec((tm, tn), lambda i, j, kk: (i, j)),
        out_shape=jax.ShapeDtypeStruct((m, n), out_dtype),
        scratch_shapes=[pltpu.VMEM((tm, tn), F32)],
        compiler_params=pltpu.CompilerParams(dimension_semantics=("parallel", "parallel", "arbitrary"),
                                             vmem_limit_bytes=VMEM_LIMIT),
        name=name)(a, b)


class Op:
    def __init__(self, arr, block, imap, grad=False, acc=False, first=None, gdtype=F32, gshape=None, gimap=None):
        self.arr, self.block, self.imap = arr, block, imap
        self.grad, self.acc, self.first, self.gdtype = grad, acc, first, gdtype
        self.gshape = arr.shape if gshape is None else gshape
        self.gimap = imap if gimap is None else gimap


def _row_op(arr, tm, grad=False, gdtype=F32):
    return Op(arr, (tm, arr.shape[1]), lambda i: (i, 0), grad=grad, gdtype=gdtype)


def _par_op(arr, grad=False):
    nd = arr.ndim
    return Op(arr, arr.shape, lambda i: (0,) * nd, grad=grad, acc=True, first=lambda ids: ids[0] == 0)


def _load(ref):
    v = ref[...]
    return v.astype(F32) if jnp.issubdtype(v.dtype, jnp.floating) else v


def stage_fwd(name, fn, ops, outs, grid):
    n_in = len(ops)

    def body(*refs):
        res = fn(*[_load(r) for r in refs[:n_in]])
        for r, o in zip(refs[n_in:], res):
            r[...] = o.astype(r.dtype)

    return pl.pallas_call(
        body, grid=grid, in_specs=[pl.BlockSpec(o.block, o.imap) for o in ops],
        out_specs=[pl.BlockSpec(b, im) for (_, _, b, im) in outs],
        out_shape=[jax.ShapeDtypeStruct(s, d) for (s, d, _, _) in outs],
        compiler_params=pltpu.CompilerParams(dimension_semantics=("parallel",) * len(grid),
                                             vmem_limit_bytes=VMEM_LIMIT),
        name=name)(*[o.arr for o in ops])


def stage_bwd(name, fn, ops, cts, grid, value_acc=False):
    n_in = len(ops)
    ct_flat = [c for group in cts if group is not None for c in group]
    n_ct = len(ct_flat)
    diff = [i for i, o in enumerate(ops) if o.grad]
    any_acc = value_acc or any(ops[i].acc for i in diff)
    ngrid = len(grid)

    def body(*refs):
        ids = [pl.program_id(a) for a in range(ngrid)]
        vals = [_load(r) for r in refs[:n_in]]
        ct_refs = refs[n_in:n_in + n_ct]
        out_refs = refs[n_in + n_ct:]

        def f(*dv):
            full = list(vals)
            for i, v in zip(diff, dv):
                full[i] = v
            return tuple(fn(*full))

        res, vjp = jax.vjp(f, *[vals[i] for i in diff])
        ct, pos = [], 0
        for group, r in zip(cts, res):
            if group is None:
                ct.append(jnp.ones_like(r))
            else:
                tot = None
                for _ in group:
                    c = ct_refs[pos][...].astype(F32)
                    tot = c if tot is None else tot + c
                    pos += 1
                ct.append(tot)
        grads = vjp(tuple(ct))
        for i, g, r in zip(diff, grads, out_refs):
            if ops[i].acc:
                @pl.when(ops[i].first(ids))
                def _(r=r):
                    r[...] = jnp.zeros_like(r)

                r[...] += g.astype(r.dtype)
            else:
                r[...] = g.astype(r.dtype)
        if value_acc:
            r = out_refs[len(diff)]

            @pl.when(ids[0] == 0)
            def _():
                r[...] = jnp.zeros_like(r)

            r[...] += res[0]

    in_specs = [pl.BlockSpec(o.block, o.imap) for o in ops] + [pl.BlockSpec(b, im) for (_, b, im) in ct_flat]
    out_specs = [pl.BlockSpec(ops[i].block, ops[i].gimap) for i in diff]
    out_shape = [jax.ShapeDtypeStruct(ops[i].gshape, ops[i].gdtype) for i in diff]
    if value_acc:
        out_specs.append(pl.BlockSpec((1, 1), lambda *ids: (0, 0)))
        out_shape.append(jax.ShapeDtypeStruct((1, 1), F32))
    sem = ("arbitrary",) * ngrid if any_acc else ("parallel",) * ngrid
    return pl.pallas_call(
        body, grid=grid, in_specs=in_specs, out_specs=out_specs, out_shape=out_shape,
        compiler_params=pltpu.CompilerParams(dimension_semantics=sem, vmem_limit_bytes=VMEM_LIMIT),
        name=name)(*[o.arr for o in ops], *[a for (a, _, _) in ct_flat])


def _sigmoid(x):
    return 1.0 / (1.0 + jnp.exp(-x))


def _silu(x):
    return x * _sigmoid(x)


def _gelu_tanh(x):
    return 0.5 * x * (1.0 + jnp.tanh(math.sqrt(2.0 / math.pi) * (x + 0.044715 * (x * x * x))))


def _ln(x, g, b):
    mu = jnp.mean(x, axis=-1, keepdims=True)
    xc = x - mu
    var = jnp.mean(xc * xc, axis=-1, keepdims=True)
    return xc * lax.rsqrt(var + LN_EPS) * g + b


def _rms(x, g):
    ms = jnp.mean(x * x, axis=-1, keepdims=True)
    return x * lax.rsqrt(ms + RMS_EPS) * g


def _swap_halves(x, half):
    width = x.shape[-1]
    lane = lax.broadcasted_iota(jnp.int32, x.shape, 1)
    return jnp.where(lane % (2 * half) < half, pltpu.roll(x, width - half, 1), pltpu.roll(x, half, 1))


def _make_swap(half):
    @jax.custom_vjp
    def swap(x):
        return _swap_halves(x, half)

    swap.defvjp(lambda x: (_swap_halves(x, half), None), lambda _, dy: (_swap_halves(dy, half),))
    return swap


_swap16, _swap8 = _make_swap(16), _make_swap(8)


def _rope(x, cos, sin_signed, swap):
    return x * cos + swap(x) * sin_signed


def _dot_f32(a, b):
    return jnp.dot(a, b, preferred_element_type=F32, precision=lax.Precision.HIGHEST)


def fn_ln(x, g, b):
    return (_ln(x, g, b),)


def _twice(fn):
    def f(*a):
        (y,) = fn(*a)
        return y, y
    return f


PROJ_W = 2048
P_A, P_Q, P_K, P_V, P_C, P_CQ, P_CKV, P_KR = 0, 512, 768, 896, 1024, 1536, 1792, 1920
CQ_PAD = 256
_CQ_END = P_CQ + MLA_Q_LORA


def _pad_w_in(w):
    z = lambda n: jnp.zeros((w.shape[0], n), w.dtype)
    return jnp.concatenate([w[:, :_CQ_END], z(P_CKV - _CQ_END), w[:, _CQ_END:], z(PROJ_W - P_KR - MLA_ROPE)], axis=1)


def _unpad_w_in(g):
    return jnp.concatenate([g[:, :_CQ_END], g[:, P_CKV:P_KR + MLA_ROPE]], axis=1)


def fn_pre(proj, tab_q, tab_d, seg, place, qng, kng, sg, sb, sw, sbt, mqn, wuq, mkvn, wukv):
    tm = proj.shape[0]
    aglu = proj[:, P_A:P_A + GROUP_W] * _sigmoid(proj[:, P_A + GROUP_W:P_Q])
    b_q, b_k, b_v = proj[:, P_Q:P_K], proj[:, P_K:P_V], proj[:, P_V:P_C]
    cos_q, sin_q = tab_q[:, :GROUP_W], tab_q[:, GROUP_W:]
    q = b_q * lax.rsqrt(_dot_f32(b_q * b_q, seg) + RMS_EPS) * qng
    q = _rope(q, cos_q, sin_q, _swap16)
    k = b_k * lax.rsqrt(_dot_f32(b_k * b_k, seg[:128, :128]) + RMS_EPS) * kng
    k = _rope(k, cos_q[:, :128], sin_q[:, :128], _swap16)
    c = _gelu_tanh(proj[:, P_C:P_CQ])
    u, sv = c[:, :GROUP_W], _ln(c[:, GROUP_W:], sg, sb)
    group = lax.broadcasted_iota(jnp.int32, (CHUNK, GROUP_W), 1) // HEAD_DIM
    rows = []
    for n in range(tm // CHUNK):
        svn = sv[CHUNK * n:CHUNK * (n + 1)]
        acc = jnp.zeros((CHUNK, GROUP_W), F32)
        for g in range(SGU_GROUPS):
            acc = acc + jnp.where(group == g, mm_nn(sw[CHUNK * g:CHUNK * (g + 1)], svn) + sbt[:, g:g + 1], 0.0)
        rows.append(acc)
    o_c = u * jnp.concatenate(rows, axis=0)
    d_cq, d_ckv, d_kr = proj[:, P_CQ:P_CKV], proj[:, P_CKV:P_KR], proj[:, P_KR:PROJ_W]
    cqn = d_cq * lax.rsqrt(jnp.sum(d_cq * d_cq, axis=-1, keepdims=True) * (1.0 / MLA_Q_LORA) + RMS_EPS) * mqn
    cos_d = jnp.concatenate([tab_d[:, :MLA_DK_PAD]] * MLA_HEADS, axis=-1)
    sin_d = jnp.concatenate([tab_d[:, MLA_DK_PAD:]] * MLA_HEADS, axis=-1)
    qf = _rope(mm_nn(cqn, wuq), cos_d, sin_d, _swap8)
    kvd = mm_nn(_rms(d_ckv, mkvn), wukv)
    kf = _rope(kvd[:, :MLA_HEADS * MLA_DK_PAD] + _dot_f32(d_kr, place), cos_d, sin_d, _swap8)
    return aglu, q, k, b_v, o_c, qf, kf, kvd[:, MLA_HEADS * MLA_DK_PAD:]


def fn_aconv(win, w, b, g, beta):
    tm = win.shape[0] - 2 * CONV_A_HALO
    off = CONV_A_HALO - CONV_A_WIDTH // 2
    acc = None
    for kk in range(CONV_A_WIDTH):
        term = win[off + kk:off + kk + tm] * w[kk:kk + 1, :]
        acc = term if acc is None else acc + term
    return (_silu(_ln(acc + b, g, beta)),)


def fn_resln(h, r, g, b):
    return (_ln(DEEPNORM_ALPHA * h + r, g, b),)


def _shift_down(x):
    return jnp.concatenate([jnp.zeros((8, x.shape[1]), F32), x], axis=0)[7:7 + x.shape[0]]


def _shift_up(x):
    return jnp.concatenate([x, jnp.zeros((8, x.shape[1]), F32)], axis=0)[1:1 + x.shape[0]]


def fn_ffnconv(u1, u2, w1, w2, b1, b2):
    c1 = _shift_down(u1) * w1[0:1] + u1 * w1[1:2] + _shift_up(u1) * w1[2:3] + b1
    c2 = _shift_down(u2) * w2[0:1] + u2 * w2[1:2] + _shift_up(u2) * w2[2:3] + b2
    return (_silu(c1) * c2,)


def fn_final(h, r, t, g, b):
    y = _ln(DEEPNORM_ALPHA * h + r, g, b)
    err = (y - t) * (y - t)
    return (0.5 * jnp.sum(jnp.mean(err, axis=-1, keepdims=True), axis=0, keepdims=True),)


def _rope_tables(seq):
    pos = jnp.arange(seq, dtype=jnp.int32)
    row = (pos // GRID_W).astype(F32)
    col = (pos % GRID_W).astype(F32)

    def tab(half):
        inv = ROPE_THETA ** (-jnp.arange(half, dtype=F32) / half)
        ar, ac = row[:, None] * inv[None, :], col[:, None] * inv[None, :]
        cos = jnp.concatenate([jnp.cos(ar)] * 2 + [jnp.cos(ac)] * 2, axis=-1)
        sin = jnp.concatenate([-jnp.sin(ar), jnp.sin(ar), -jnp.sin(ac), jnp.sin(ac)], axis=-1)
        return cos, sin

    cos_b, sin_b = tab(HEAD_DIM // 4)
    tab_q = jnp.concatenate([cos_b] * GQA_HEADS + [sin_b] * GQA_HEADS, axis=-1)
    cos_r, sin_r = tab(MLA_ROPE // 4)
    ones = jnp.ones((seq, MLA_NOPE), F32)
    zpad = MLA_DK_PAD - MLA_NOPE - MLA_ROPE
    tab_d = jnp.concatenate([ones, cos_r, jnp.ones((seq, zpad), F32), 0.0 * ones, sin_r, jnp.zeros((seq, zpad), F32)],
                            axis=-1)
    lane = jnp.arange(GROUP_W)
    seg = jnp.where(lane[:, None] // HEAD_DIM == lane[None, :] // HEAD_DIM, 1.0 / HEAD_DIM, 0.0).astype(F32)
    src, dst = jnp.arange(128)[:, None], jnp.arange(MLA_HEADS * MLA_DK_PAD)[None, :]
    place = jnp.where((src < MLA_ROPE) & (dst % MLA_DK_PAD == MLA_NOPE + src), 1.0, 0.0).astype(F32)
    return tab_q, tab_d, seg, place


def _pre_ops(proj, tabs, kp, grad):
    tm = ROW_TILE
    ops = [_row_op(proj, tm, grad=grad, gdtype=MXU_DTYPE), _row_op(tabs[0], tm), _row_op(tabs[1], tm),
           _par_op(tabs[2]), _par_op(tabs[3])]
    ops += [_par_op(kp[n], grad=grad) for n in ('qng', 'kng', 'sg', 'sb', 'sw', 'sbt', 'mqn', 'wuq', 'mkvn', 'wukv')]
    return ops


PRE_OUT_WIDTHS = (GROUP_W, GROUP_W, 128, 128, GROUP_W, MLA_HEADS * MLA_DK_PAD, MLA_HEADS * MLA_DK_PAD, GROUP_W)


def pre_fwd(proj, tabs, kp, tag):
    seq = proj.shape[0]
    tm = ROW_TILE
    dts = (F32,) + (MXU_DTYPE,) * 7
    outs = [((seq, w), dt, (tm, w), lambda i: (i, 0)) for w, dt in zip(PRE_OUT_WIDTHS, dts)]
    return stage_fwd("pre_fwd" + tag, fn_pre, _pre_ops(proj, tabs, kp, False), outs, (seq // tm,))


def pre_bwd(proj, tabs, kp, cts, tag):
    seq = proj.shape[0]
    tm = ROW_TILE
    ct = [[(c, (tm, c.shape[1]), lambda i: (i, 0))] for c in cts]
    return stage_bwd("pre_bwd" + tag, fn_pre, _pre_ops(proj, tabs, kp, True), ct, (seq // tm,))


def _aconv_ops(kp, grad):
    return [_par_op(kp[n], grad=grad) for n in ('caw', 'cab', 'lag', 'lab')]


def aconv_fwd(aglu_pad, kp, tag):
    seq = aglu_pad.shape[0] - 2 * CONV_A_HALO
    tm = ROW_TILE
    n_par = 4

    def body(x_ref, *refs):
        i = pl.program_id(0)
        win = x_ref[pl.ds(pl.multiple_of(i * tm, tm), tm + 2 * CONV_A_HALO), :]
        (o,) = fn_aconv(win, *[_load(r) for r in refs[:n_par]])
        refs[n_par][...] = o.astype(refs[n_par].dtype)

    pars = _aconv_ops(kp, False)
    return pl.pallas_call(
        body, grid=(seq // tm,),
        in_specs=[pl.BlockSpec(aglu_pad.shape, lambda i: (0, 0))] + [pl.BlockSpec(o.block, o.imap) for o in pars],
        out_specs=pl.BlockSpec((tm, GROUP_W), lambda i: (i, 0)),
        out_shape=jax.ShapeDtypeStruct((seq, GROUP_W), MXU_DTYPE),
        compiler_params=pltpu.CompilerParams(dimension_semantics=("parallel",), vmem_limit_bytes=VMEM_LIMIT),
        name="aconv_fwd" + tag)(aglu_pad, *[o.arr for o in pars])


def aconv_bwd(aglu_pad, kp, d_oa, tag):
    seq = aglu_pad.shape[0] - 2 * CONV_A_HALO
    tm = ROW_TILE
    n_par = 4

    def body(x_ref, *refs):
        i = pl.program_id(0)
        rows = pl.ds(pl.multiple_of(i * tm, tm), tm + 2 * CONV_A_HALO)
        pars = [_load(r) for r in refs[:n_par]]
        ct = refs[n_par][...].astype(F32)
        outs = refs[n_par + 1:]
        _, vjp = jax.vjp(lambda *a: fn_aconv(*a), x_ref[rows, :], *pars)
        grads = vjp((ct,))

        @pl.when(i == 0)
        def _():
            for r in outs:
                r[...] = jnp.zeros_like(r)

        outs[0][rows, :] += grads[0]
        for r, g in zip(outs[1:], grads[1:]):
            r[...] += g

    pars = _aconv_ops(kp, True)
    whole = pl.BlockSpec(aglu_pad.shape, lambda i: (0, 0))
    par_specs = [pl.BlockSpec(o.block, o.imap) for o in pars]
    return pl.pallas_call(
        body, grid=(seq // tm,),
        in_specs=[whole] + par_specs + [pl.BlockSpec((tm, GROUP_W), lambda i: (i, 0))],
        out_specs=[whole] + par_specs,
        out_shape=[jax.ShapeDtypeStruct(aglu_pad.shape, F32)] + [jax.ShapeDtypeStruct(o.arr.shape, F32) for o in pars],
        compiler_params=pltpu.CompilerParams(dimension_semantics=("arbitrary",), vmem_limit_bytes=VMEM_LIMIT),
        name="aconv_bwd" + tag)(aglu_pad, *[o.arr for o in pars], d_oa)


ATTN_TQ = 256
ATTN_TK = 512


def attn_fwd(q3, k3, v3, scale, tag):
    heads, seq, dk = q3.shape
    group = heads // k3.shape[0]
    dv = v3.shape[2]
    tq, tk = min(ATTN_TQ, seq), min(ATTN_TK, seq)

    n_chunks = seq // tk
    log2e = math.log2(math.e)

    def body(q_ref, k_ref, v_ref, o_ref, lse_ref):
        q = q_ref[...]
        scores = lambda c: _rawdot(q, k_ref[pl.ds(c * tk, tk), :], 1, 1)
        m, l, acc = jnp.full((tq, 1), -jnp.inf, F32), jnp.zeros((tq, 1), F32), jnp.zeros((tq, dv), F32)
        s_next = scores(0)
        for c in range(n_chunks):
            s_cur, s_next = s_next, (scores(c + 1) if c + 1 < n_chunks else None)
            t = s_cur * (scale * log2e)
            m_new = jnp.maximum(m, jnp.max(t, axis=-1, keepdims=True))
            alpha = jnp.exp2(m - m_new)
            p = jnp.exp2(t - m_new)
            l = alpha * l + jnp.sum(p, axis=-1, keepdims=True)
            acc = alpha * acc + _rawdot(p, v_ref[pl.ds(c * tk, tk), :], 1, 0)
            m = m_new
        o_ref[...] = acc * (1.0 / l)
        lse_ref[...] = m * (1.0 / log2e) + jnp.log(l)

    return pl.pallas_call(
        body, grid=(heads, seq // tq),
        in_specs=[pl.BlockSpec((None, tq, dk), lambda h, i: (h, i, 0)),
                  pl.BlockSpec((None, seq, dk), lambda h, i: (h // group, 0, 0)),
                  pl.BlockSpec((None, seq, dv), lambda h, i: (h // group, 0, 0))],
        out_specs=[pl.BlockSpec((None, tq, dv), lambda h, i: (h, i, 0)),
                   pl.BlockSpec((None, tq, 1), lambda h, i: (h, i, 0))],
        out_shape=[jax.ShapeDtypeStruct((heads, seq, dv), F32), jax.ShapeDtypeStruct((heads, seq, 1), F32)],
        compiler_params=pltpu.CompilerParams(dimension_semantics=("parallel", "parallel"),
                                             vmem_limit_bytes=VMEM_LIMIT),
        name="attn_fwd" + tag)(q3, k3, v3)


def attn_bwd(q3, k3, v3, o3, lse3, do3, scale, tag):
    heads, seq, dk = q3.shape
    group = heads // k3.shape[0]
    dv = v3.shape[2]
    tq, tk = min(ATTN_TQ, seq), min(ATTN_TK, seq)
    n_chunks = seq // tk
    log2e = math.log2(math.e)

    def body(q_ref, k_ref, v_ref, o_ref, lse_ref, do_ref, dq_ref, dk_ref, dv_ref):
        @pl.when(jnp.logical_and(pl.program_id(0) % group == 0, pl.program_id(1) == 0))
        def _():
            dk_ref[...] = jnp.zeros_like(dk_ref)
            dv_ref[...] = jnp.zeros_like(dv_ref)

        q, do = q_ref[...], do_ref[...]
        dob = do.astype(MXU_DTYPE)
        delta = jnp.sum(do * o_ref[...], axis=-1, keepdims=True)
        lse2 = lse_ref[...] * log2e
        rows = lambda c: pl.ds(c * tk, tk)
        products = lambda c: (_rawdot(q, k_ref[rows(c), :], 1, 1), _rawdot(dob, v_ref[rows(c), :], 1, 1))
        dq = jnp.zeros((tq, dk), F32)
        nxt = products(0)
        for c in range(n_chunks):
            (s_cur, dp_cur), nxt = nxt, (products(c + 1) if c + 1 < n_chunks else None)
            p = jnp.exp2(s_cur * (scale * log2e) - lse2)
            ds = (p * ((dp_cur - delta) * scale)).astype(MXU_DTYPE)
            dv_ref[rows(c), :] += _rawdot(p, dob, 0, 0)
            dk_ref[rows(c), :] += _rawdot(ds, q, 0, 0)
            dq = dq + _rawdot(ds, k_ref[rows(c), :], 1, 0)
        dq_ref[...] = dq

    qspec = lambda d: pl.BlockSpec((None, tq, d), lambda h, i: (h, i, 0))
    kvspec = lambda d: pl.BlockSpec((None, seq, d), lambda h, i: (h // group, 0, 0))
    return pl.pallas_call(
        body, grid=(heads, seq // tq),
        in_specs=[qspec(dk), kvspec(dk), kvspec(dv), qspec(dv), qspec(1), qspec(dv)],
        out_specs=[qspec(dk), kvspec(dk), kvspec(dv)],
        out_shape=[jax.ShapeDtypeStruct(q3.shape, F32), jax.ShapeDtypeStruct(k3.shape, F32),
                   jax.ShapeDtypeStruct(v3.shape, F32)],
        compiler_params=pltpu.CompilerParams(dimension_semantics=("arbitrary", "arbitrary"),
                                             vmem_limit_bytes=VMEM_LIMIT),
        name="attn_bwd" + tag)(q3, k3, v3, o3, lse3, do3)


def resln_fwd(h, r, g, b, tag):
    seq, d = h.shape
    tm = ROW_TILE
    ops = [_row_op(h, tm), _row_op(r, tm), _par_op(g), _par_op(b)]
    outs = [((seq, d), dt, (tm, d), lambda i: (i, 0)) for dt in (F32, MXU_DTYPE)]
    return stage_fwd("resln_fwd" + tag, _twice(fn_resln), ops, outs, (seq // tm,))


def resln_bwd(h, r, g, b, dys, tag):
    seq, d = h.shape
    tm = ROW_TILE
    ops = [_row_op(h, tm, grad=True), _row_op(r, tm, grad=True, gdtype=MXU_DTYPE), _par_op(g, grad=True),
           _par_op(b, grad=True)]
    ct = [[(dy, (tm, d), lambda i: (i, 0)) for dy in dys]]
    return stage_bwd("resln_bwd" + tag, fn_resln, ops, ct, (seq // tm,))


def _ffnconv_ops(up, w, b, grad):
    seq = up.shape[0]
    nblk = D_FF // 128
    lo, hi = (lambda j: (0, j)), (lambda j: (0, j + nblk))
    half = lambda a: dict(gshape=(a.shape[0], D_FF), gimap=lo)
    return [Op(up, (seq, 128), lo, grad=grad, gdtype=MXU_DTYPE, **half(up)),
            Op(up, (seq, 128), hi, grad=grad, gdtype=MXU_DTYPE, **half(up)),
            Op(w, (3, 128), lo, grad=grad, **half(w)), Op(w, (3, 128), hi, grad=grad, **half(w)),
            Op(b, (1, 128), lo, grad=grad, **half(b)), Op(b, (1, 128), hi, grad=grad, **half(b))]


def ffnconv_fwd(up, w, b, tag):
    seq = up.shape[0]
    outs = [((seq, D_FF), MXU_DTYPE, (seq, 128), lambda j: (0, j))]
    return stage_fwd("ffnconv_fwd" + tag, fn_ffnconv, _ffnconv_ops(up, w, b, False), outs, (D_FF // 128,))[0]


def ffnconv_bwd(up, w, b, dact, tag):
    seq = up.shape[0]
    ct = [[(dact, (seq, 128), lambda j: (0, j))]]
    du1, du2, dw1, dw2, db1, db2 = stage_bwd("ffnconv_bwd" + tag, fn_ffnconv, _ffnconv_ops(up, w, b, True), ct,
                                             (D_FF // 128,))
    cat = lambda a, b_: jnp.concatenate([a, b_], axis=-1)
    return cat(du1, du2), cat(dw1, dw2), cat(db1, db2)


def final_bwd(h, r, t, g, b, tag):
    seq, d = h.shape
    tm = ROW_TILE
    ops = [_row_op(h, tm, grad=True), _row_op(r, tm, grad=True, gdtype=MXU_DTYPE), _row_op(t, tm),
           _par_op(g, grad=True), _par_op(b, grad=True)]
    return stage_bwd("final_bwd" + tag, fn_final, ops, [None], (seq // tm,), value_acc=True)


def _layer_params(wts, l):
    row = lambda a: a.reshape(1, -1)
    wuq = wts['mla_w_uq'][l].reshape(MLA_Q_LORA, MLA_HEADS, MLA_NOPE + MLA_ROPE)
    wuq = jnp.pad(wuq, ((0, CQ_PAD - MLA_Q_LORA), (0, 0), (0, MLA_DK_PAD - MLA_NOPE - MLA_ROPE)))
    wukv = wts['mla_w_ukv'][l].reshape(MLA_KV_LORA, MLA_HEADS, MLA_NOPE + MLA_V)
    wuk = jnp.pad(wukv[:, :, :MLA_NOPE], ((0, 0), (0, 0), (0, MLA_DK_PAD - MLA_NOPE)))
    return dict(
        qng=jnp.tile(row(wts['qk_norm_q'][l]), (1, GQA_HEADS)), kng=jnp.tile(row(wts['qk_norm_k'][l]), (1, GQA_KV_HEADS)),
        sg=row(wts['sgu_ln_g'][l]), sb=row(wts['sgu_ln_b'][l]),
        sw=wts['sgu_w'][l].reshape(SGU_GROUPS * CHUNK, CHUNK), sbt=wts['sgu_b'][l].T,
        mqn=jnp.pad(row(wts['mla_q_norm'][l]), ((0, 0), (0, CQ_PAD - MLA_Q_LORA))),
        wuq=wuq.reshape(CQ_PAD, MLA_HEADS * MLA_DK_PAD), mkvn=row(wts['mla_kv_norm'][l]),
        wukv=jnp.concatenate([wuk.reshape(MLA_KV_LORA, -1), wukv[:, :, MLA_NOPE:].reshape(MLA_KV_LORA, -1)], axis=1),
        caw=wts['conv_a_w'][l], cab=row(wts['conv_a_b'][l]), lag=row(wts['ln_a_g'][l]), lab=row(wts['ln_a_b'][l]),
        lmg=row(wts['ln_mix_g'][l]), lmb=row(wts['ln_mix_b'][l]),
        fcw=wts['ffn_conv_w'][l], fcb=row(wts['ffn_conv_b'][l]),
        lfg=row(wts['ln_ffn_g'][l]), lfb=row(wts['ln_ffn_b'][l]))


def _to_heads(a, heads):
    seq = a.shape[0]
    return a.reshape(seq, heads, -1).transpose(1, 0, 2)


def _from_heads(a3):
    return a3.transpose(1, 0, 2).reshape(a3.shape[1], -1)


def local_step(x, target, wts, mat, hook):
    seq = x.shape[0]
    tm = ROW_TILE
    tabs = _rope_tables(seq)
    scale_b = HEAD_DIM ** -0.5
    scale_d = (MLA_NOPE + MLA_ROPE) ** -0.5
    ln_in_g, ln_in_b = wts['ln_in_g'].reshape(1, -1), wts['ln_in_b'].reshape(1, -1)

    h, h_m = stage_fwd("ln_in_fwd", _twice(fn_ln), [_row_op(x, tm), _par_op(ln_in_g), _par_op(ln_in_b)],
                       [((seq, D_MODEL), dt, (tm, D_MODEL), lambda i: (i, 0)) for dt in (F32, MXU_DTYPE)],
                       (seq // tm,))
    saved = []
    for l in range(DEPTH):
        tag = f"_l{l}"
        kp, unprep = jax.vjp(lambda w: _layer_params(w, l), wts)
        m = {'w_in': mat(l, 'w_in', h_m)}
        proj = matmul(h_m, m['w_in'], 'nn', F32, "mm_proj" + tag)
        aglu, q, k, v, o_c, qf, kf, vd = pre_fwd(proj, tabs, kp, tag)
        aglu_pad = jnp.pad(aglu, ((CONV_A_HALO, CONV_A_HALO), (0, 0)))
        o_a = aconv_fwd(aglu_pad, kp, tag)
        q3, k3, v3 = _to_heads(q, GQA_HEADS), _to_heads(k, GQA_KV_HEADS), _to_heads(v, GQA_KV_HEADS)
        o_b3, lse_b3 = attn_fwd(q3, k3, v3, scale_b, "_b" + tag)
        qd3, kd3, vd3 = _to_heads(qf, MLA_HEADS), _to_heads(kf, MLA_HEADS), _to_heads(vd, MLA_HEADS)
        o_d3, lse_d3 = attn_fwd(qd3, kd3, vd3, scale_d, "_d" + tag)
        o_cat = jnp.concatenate([o_a, _from_heads(o_b3).astype(MXU_DTYPE), o_c,
                                 _from_heads(o_d3).astype(MXU_DTYPE)], axis=-1)
        m['w_out'] = mat(l, 'w_out', o_cat)
        mix = matmul(o_cat, m['w_out'], 'nn', F32, "mm_mix" + tag)
        h1, h1_m = resln_fwd(h, mix, kp['lmg'], kp['lmb'], "_mix" + tag)
        m['ffn_w_up'] = mat(l, 'ffn_w_up', h1_m)
        up = matmul(h1_m, m['ffn_w_up'], 'nn', F32, "mm_up" + tag)
        act = ffnconv_fwd(up, kp['fcw'], kp['fcb'], tag)
        m['ffn_w_down'] = mat(l, 'ffn_w_down', act)
        f = matmul(act, m['ffn_w_down'], 'nn', F32, "mm_down" + tag)
        saved.append(dict(kp=kp, unprep=unprep, m=m, h=h, h_m=h_m, h1_m=h1_m, proj=proj, o_b3=o_b3, lse_b3=lse_b3,
                          o_d3=o_d3, lse_d3=lse_d3, aglu_pad=aglu_pad, q3=q3, k3=k3, v3=v3, qd3=qd3,
                          kd3=kd3, vd3=vd3, o_cat=o_cat, mix=mix, h1=h1, up=up, act=act, f=f))
        if l + 1 < DEPTH:
            h, h_m = resln_fwd(h1, f, kp['lfg'], kp['lfb'], "_ffn" + tag)

    after = lambda a, tok: a if tok is None else a + tok
    small_acc = None
    dh_parts = None
    loss = None
    tok = None
    g_mix = None
    for l in reversed(range(DEPTH)):
        tag = f"_l{l}"
        s = saved[l]
        kp, m = s['kp'], s['m']
        dkp = {}
        lfg = after(kp['lfg'], tok)
        if l == DEPTH - 1:
            dh1_a, df, dkp['lfg'], dkp['lfb'], loss = final_bwd(s['h1'], s['f'], target, lfg, kp['lfb'], tag)
        else:
            dh1_a, df, dkp['lfg'], dkp['lfb'] = resln_bwd(s['h1'], s['f'], lfg, kp['lfb'], dh_parts, "_ffn" + tag)
        dact = matmul(df, m['ffn_w_down'], 'nt', F32, "mm_dact" + tag)
        g_down = matmul(s['act'], df, 'tn', F32, "mm_gdown" + tag)
        dup, dkp['fcw'], dkp['fcb'] = ffnconv_bwd(s['up'], kp['fcw'], kp['fcb'], dact, tag)
        dh1_b = matmul(dup, m['ffn_w_up'], 'nt', F32, "mm_dh1" + tag)
        g_up = matmul(s['h1_m'], dup, 'tn', F32, "mm_gup" + tag)
        tok = hook(f"ffn{l}", {('ffn_w_down', l): g_down, ('ffn_w_up', l): g_up})
        dh_a, dmix, dkp['lmg'], dkp['lmb'] = resln_bwd(s['h'], s['mix'], after(kp['lmg'], tok), kp['lmb'],
                                                       [dh1_a, dh1_b], "_mix" + tag)
        do_cat = matmul(dmix, m['w_out'], 'nt', F32, "mm_docat" + tag)
        g_out = matmul(s['o_cat'], dmix, 'tn', F32, "mm_gout" + tag)
        do_a, do_b, do_c, do_d = (do_cat[:, GROUP_W * j:GROUP_W * (j + 1)] for j in range(4))
        dq3, dk3, dv3 = attn_bwd(s['q3'], s['k3'], s['v3'], s['o_b3'], s['lse_b3'], _to_heads(do_b, GQA_HEADS),
                                 scale_b, "_b" + tag)
        dqd3, dkd3, dvd3 = attn_bwd(s['qd3'], s['kd3'], s['vd3'], s['o_d3'], s['lse_d3'],
                                    _to_heads(do_d, MLA_HEADS), scale_d, "_d" + tag)
        daglu_pad, dkp['caw'], dkp['cab'], dkp['lag'], dkp['lab'] = aconv_bwd(s['aglu_pad'], kp, do_a, tag)
        cts = [daglu_pad[CONV_A_HALO:CONV_A_HALO + seq], _from_heads(dq3), _from_heads(dk3), _from_heads(dv3), do_c,
               _from_heads(dqd3), _from_heads(dkd3), _from_heads(dvd3)]
        pre_g = pre_bwd(s['proj'], tabs, kp, cts, tag)
        dproj = pre_g[0]
        for n, g in zip(('qng', 'kng', 'sg', 'sb', 'sw', 'sbt', 'mqn', 'wuq', 'mkvn', 'wukv'), pre_g[1:]):
            dkp[n] = g
        dh_b = matmul(dproj, m['w_in'], 'nt', F32, "mm_dh" + tag)
        g_in = matmul(s['h_m'], dproj, 'tn', F32, "mm_gin" + tag)
        dh_parts = [dh_a, dh_b]
        (dw,) = s['unprep'](dkp)
        small_acc = dw if small_acc is None else jax.tree.map(jnp.add, small_acc, dw)
        g_mix = {('w_out', l): g_out, ('w_in', l): _unpad_w_in(g_in)}
        if l > 0:
            tok = hook(f"mix{l}", g_mix)

    g_mix.update({(n, None): small_acc[n] for n in SHARDED if n not in MATMUL_WEIGHTS})
    tok = hook("last", g_mix)
    dx, dg, db = stage_bwd("ln_in_bwd", fn_ln,
                           [_row_op(x, tm, grad=True), _par_op(after(ln_in_g, tok), grad=True),
                            _par_op(ln_in_b, grad=True)],
                           [[(p, (tm, D_MODEL), lambda i: (i, 0)) for p in dh_parts]], (seq // tm,))
    out = {n: small_acc[n] for n in REPLICATED}
    out['ln_in_g'], out['ln_in_b'] = dg.reshape(-1), db.reshape(-1)
    return loss, dx, out


def _peer(x, y, c, r):
    return ((1 - x) if r & 4 else x, (1 - y) if r & 2 else y, (1 - c) if r & 1 else c)


def _exchange_copy(src_ref, land_ref, send_sems, recv_sems, k, gather, x, y, c, r):
    px, py, pc = _peer(x, y, c, r)
    me, peer = 4 * x + 2 * y + c, 4 * px + 2 * py + pc
    src = src_ref if gather else src_ref.at[peer]
    mk = lambda dst: pltpu.make_async_remote_copy(
        src_ref=src, dst_ref=dst, send_sem=send_sems.at[k * (N_DEV - 1) + r - 1],
        recv_sem=recv_sems.at[k * (N_DEV - 1) + r - 1],
        device_id=(px, py, pc), device_id_type=pl.DeviceIdType.MESH)
    return mk(land_ref.at[me]), mk(land_ref.at[peer])


_HBM_SPEC = pl.BlockSpec(memory_space=pltpu.HBM)
_SEM_SPEC = pl.BlockSpec(memory_space=pltpu.SEMAPHORE)


def exchange_start(srcs, gather, groups, name):
    n_t = len(srcs)
    lands =[lax.empty(((N_DEV,) + s.shape) if gt else s.shape, s.dtype) for s, gt in zip(srcs, gather)]

    def body(*refs):
        src_refs, land_refs = refs[:n_t], refs[n_t:2 * n_t]
        sem_refs = refs[2 * n_t:2 * n_t + 2 * len(groups)]
        token = refs[-1]
        x, y, c = lax.axis_index("x"), lax.axis_index("y"), lax.axis_index("c")
        for gi, g in enumerate(groups):
            for k, t in enumerate(g):
                for r in range(1, N_DEV):
                    _exchange_copy(src_refs[t], land_refs[t], sem_refs[2 * gi], sem_refs[2 * gi + 1], k, gather[t],
                                   x, y, c, r)[0].start()
        token[...] = jnp.zeros_like(token)

    sem_shapes = []
    for g in groups:
        sem_shapes += [pltpu.SemaphoreType.DMA((len(g) * (N_DEV - 1),))] * 2
    hbm_shapes = [pltpu.HBM(a.shape, a.dtype) for a in list(srcs) + lands]
    n_sem = len(sem_shapes)
    res = pl.pallas_call(
        body, name=name,
        out_shape=tuple(sem_shapes + hbm_shapes + [jax.ShapeDtypeStruct((8, 128), F32)]),
        in_specs=[_HBM_SPEC] * (2 * n_t),
        out_specs=tuple([_SEM_SPEC] * n_sem + [_HBM_SPEC] * (2 * n_t) + [pl.BlockSpec(memory_space=pltpu.VMEM)]),
        input_output_aliases={i: n_sem + i for i in range(2 * n_t)},
        compiler_params=pltpu.CompilerParams(has_side_effects=pltpu.SideEffectType.DATAFLOW_SIDE_EFFECTING),
    )(*[pltpu.with_memory_space_constraint(a, pltpu.HBM) for a in list(srcs) + lands])
    sems = [(res[2 * gi], res[2 * gi + 1]) for gi in range(len(groups))]
    return sems, list(res[n_sem:n_sem + n_t]), list(res[n_sem + n_t:n_sem + 2 * n_t]), res[-1]


def exchange_wait(sems, srcs, lands, gather, after, name):
    n_t = len(srcs)

    def body(*refs):
        src_refs, land_refs = refs[:n_t], refs[n_t:2 * n_t]
        send_sems, recv_sems = refs[2 * n_t], refs[2 * n_t + 1]
        x, y, c = lax.axis_index("x"), lax.axis_index("y"), lax.axis_index("c")
        for k in range(n_t):
            for r in range(1, N_DEV):
                send, recv = _exchange_copy(src_refs[k], land_refs[k], send_sems, recv_sems, k, gather[k], x, y, c, r)
                send.wait_send()
                recv.wait_recv()

    res = pl.pallas_call(
        body, name=name,
        out_shape=tuple(pltpu.HBM(a.shape, a.dtype) for a in list(srcs) + list(lands)),
        in_specs=[_HBM_SPEC] * (2 * n_t) + [_SEM_SPEC, _SEM_SPEC, pl.BlockSpec(memory_space=pl.ANY)],
        out_specs=tuple([_HBM_SPEC] * (2 * n_t)),
        input_output_aliases={i: i for i in range(2 * n_t)},
        compiler_params=pltpu.CompilerParams(has_side_effects=pltpu.SideEffectType.DATAFLOW_SIDE_EFFECTING),
    )(*srcs, *lands, sems[0], sems[1], after)
    return list(res[:n_t]), list(res[n_t:])


def adamw(parts, w, m, v, name):
    n_l, n_r, n_c = w.shape
    tr = n_r
    if n_r % 8 == 0:
        for cand in (512, 256, 128, 64, 32, 16, 8):
            if n_r % cand == 0 and cand * n_c * 4 <= 512 * 1024:
                tr = cand
                break
    c1 = 1.0 - ADAM_B1 ** ADAM_STEP
    c2 = 1.0 - ADAM_B2 ** ADAM_STEP
    per_layer = isinstance(parts, (list, tuple))
    n_p = n_l if per_layer else 1
    n_rb = n_r // tr

    def update(g, w_ref, m_ref, v_ref, g_ref, d_ref, nm_ref, nv_ref):
        w_, m_, v_ = w_ref[0], m_ref[0], v_ref[0]
        nm = ADAM_B1 * m_ + (1.0 - ADAM_B1) * g
        nv = ADAM_B2 * v_ + (1.0 - ADAM_B2) * (g * g)
        g_ref[0] = g
        nm_ref[0] = nm
        nv_ref[0] = nv
        d_ref[0] = -ADAM_LR * ((nm / c1) / (jnp.sqrt(nv / c2) + ADAM_EPS) + ADAM_WD * w_)

    def body(*refs):
        p_refs, rest = refs[:n_p], refs[n_p:]
        if not per_layer:
            g = p_refs[0][0, 0].astype(F32)
            for s in range(1, N_DEV):
                g = g + p_refs[0][s, 0].astype(F32)
            update(g, *rest)
        else:
            for lay in range(n_l):
                @pl.when(pl.program_id(0) == lay)
                def _(lay=lay):
                    g = p_refs[lay][0].astype(F32)
                    for s in range(1, N_DEV):
                        g = g + p_refs[lay][s].astype(F32)
                    update(g, *rest)

    blk = pl.BlockSpec((1, tr, n_c), lambda l, r: (l, r, 0))
    if per_layer:
        def p_spec(lay):
            park = 0 if lay > 0 else n_rb - 1
            return pl.BlockSpec((N_DEV, tr, n_c), lambda l, r: (0, jnp.where(l == lay, r, park), 0))
        p_specs, p_args = [p_spec(lay) for lay in range(n_l)], list(parts)
    else:
        p_specs, p_args = [pl.BlockSpec((N_DEV, 1, tr, n_c), lambda l, r: (0, l, r, 0))], [parts]
    return pl.pallas_call(
        body, grid=(n_l, n_rb), in_specs=p_specs + [blk, blk, blk],
        out_specs=[blk] * 4, out_shape=[jax.ShapeDtypeStruct(w.shape, F32)] * 4,
        compiler_params=pltpu.CompilerParams(dimension_semantics=("arbitrary", "arbitrary"),
                                             vmem_limit_bytes=VMEM_LIMIT),
        name=name)(*p_args, w, m, v)


PACK_UNIT = 1024
PACK_ROWS = 256


def _pack(pieces):
    rows = []
    for p in pieces:
        flat = p.reshape(-1).astype(F32)
        pad = (-flat.shape[0]) % PACK_UNIT
        rows.append(jnp.pad(flat, (0, pad)).reshape(-1, 128))
    n = sum(r.shape[0] for r in rows)
    rows.append(jnp.zeros(((-n) % PACK_ROWS, 128), F32))
    return jnp.concatenate(rows, axis=0)


def _unpack(packed, shapes):
    out, r0 = [], 0
    for shp in shapes:
        n = math.prod(shp)
        nr = -(-n // PACK_UNIT) * (PACK_UNIT // 128)
        out.append(packed[r0:r0 + nr].reshape(-1)[:n].reshape(shp))
        r0 += nr
    return out


def _shard_slots(g, axis):
    if axis == 1:
        return g.reshape(g.shape[0], N_DEV, g.shape[1] // N_DEV, g.shape[2]).transpose(1, 0, 2, 3)
    return g.reshape(g.shape[0], g.shape[1], N_DEV, g.shape[2] // N_DEV).transpose(2, 0, 1, 3)


def _unshard(slots, axis):
    if axis == 1:
        return slots.transpose(1, 0, 2, 3).reshape(slots.shape[1], -1, slots.shape[3])
    return slots.transpose(1, 2, 0, 3).reshape(slots.shape[1], slots.shape[2], -1)


def kernel(x, ln_in_g, ln_in_b, w_in, conv_a_w, conv_a_b, ln_a_g, ln_a_b, qk_norm_q, qk_norm_k, sgu_ln_g, sgu_ln_b, sgu_w, sgu_b, mla_q_norm, mla_w_uq, mla_kv_norm, mla_w_ukv, w_out, ln_mix_g, ln_mix_b, ffn_w_up, ffn_conv_w, ffn_conv_b, ffn_w_down, ln_ffn_g, ln_ffn_b, loss_target, m_ln_in_g, m_ln_in_b, m_w_in, m_conv_a_w, m_conv_a_b, m_ln_a_g, m_ln_a_b, m_qk_norm_q, m_qk_norm_k, m_sgu_ln_g, m_sgu_ln_b, m_sgu_w, m_sgu_b, m_mla_q_norm, m_mla_w_uq, m_mla_kv_norm, m_mla_w_ukv, m_w_out, m_ln_mix_g, m_ln_mix_b, m_ffn_w_up, m_ffn_conv_w, m_ffn_conv_b, m_ffn_w_down, m_ln_ffn_g, m_ln_ffn_b, v_ln_in_g, v_ln_in_b, v_w_in, v_conv_a_w, v_conv_a_b, v_ln_a_g, v_ln_a_b, v_qk_norm_q, v_qk_norm_k, v_sgu_ln_g, v_sgu_ln_b, v_sgu_w, v_sgu_b, v_mla_q_norm, v_mla_w_uq, v_mla_kv_norm, v_mla_w_ukv, v_w_out, v_ln_mix_g, v_ln_mix_b, v_ffn_w_up, v_ffn_conv_w, v_ffn_conv_b, v_ffn_w_down, v_ln_ffn_g, v_ln_ffn_b):
    local = dict(ln_in_g=ln_in_g, ln_in_b=ln_in_b, w_in=w_in, conv_a_w=conv_a_w, conv_a_b=conv_a_b, ln_a_g=ln_a_g, ln_a_b=ln_a_b, qk_norm_q=qk_norm_q, qk_norm_k=qk_norm_k, sgu_ln_g=sgu_ln_g, sgu_ln_b=sgu_ln_b, sgu_w=sgu_w, sgu_b=sgu_b, mla_q_norm=mla_q_norm, mla_w_uq=mla_w_uq, mla_kv_norm=mla_kv_norm, mla_w_ukv=mla_w_ukv, w_out=w_out, ln_mix_g=ln_mix_g, ln_mix_b=ln_mix_b, ffn_w_up=ffn_w_up, ffn_conv_w=ffn_conv_w, ffn_conv_b=ffn_conv_b, ffn_w_down=ffn_w_down, ln_ffn_g=ln_ffn_g, ln_ffn_b=ln_ffn_b)
    mom = dict(ln_in_g=m_ln_in_g, ln_in_b=m_ln_in_b, w_in=m_w_in, conv_a_w=m_conv_a_w, conv_a_b=m_conv_a_b, ln_a_g=m_ln_a_g, ln_a_b=m_ln_a_b, qk_norm_q=m_qk_norm_q, qk_norm_k=m_qk_norm_k, sgu_ln_g=m_sgu_ln_g, sgu_ln_b=m_sgu_ln_b, sgu_w=m_sgu_w, sgu_b=m_sgu_b, mla_q_norm=m_mla_q_norm, mla_w_uq=m_mla_w_uq, mla_kv_norm=m_mla_kv_norm, mla_w_ukv=m_mla_w_ukv, w_out=m_w_out, ln_mix_g=m_ln_mix_g, ln_mix_b=m_ln_mix_b, ffn_w_up=m_ffn_w_up, ffn_conv_w=m_ffn_conv_w, ffn_conv_b=m_ffn_conv_b, ffn_w_down=m_ffn_w_down, ln_ffn_g=m_ln_ffn_g, ln_ffn_b=m_ln_ffn_b)
    var = dict(ln_in_g=v_ln_in_g, ln_in_b=v_ln_in_b, w_in=v_w_in, conv_a_w=v_conv_a_w, conv_a_b=v_conv_a_b, ln_a_g=v_ln_a_g, ln_a_b=v_ln_a_b, qk_norm_q=v_qk_norm_q, qk_norm_k=v_qk_norm_k, sgu_ln_g=v_sgu_ln_g, sgu_ln_b=v_sgu_ln_b, sgu_w=v_sgu_w, sgu_b=v_sgu_b, mla_q_norm=v_mla_q_norm, mla_w_uq=v_mla_w_uq, mla_kv_norm=v_mla_kv_norm, mla_w_ukv=v_mla_w_ukv, w_out=v_w_out, ln_mix_g=v_ln_mix_g, ln_mix_b=v_ln_mix_b, ffn_w_up=v_ffn_w_up, ffn_conv_w=v_ffn_conv_w, ffn_conv_b=v_ffn_conv_b, ffn_w_down=v_ffn_w_down, ln_ffn_g=v_ln_ffn_g, ln_ffn_b=v_ln_ffn_b)

    me = 4 * lax.axis_index("x") + 2 * lax.axis_index("y") + lax.axis_index("c")

    def own_slot(slots, block):
        return lax.dynamic_update_slice(slots, block[None], (me,) + (0,) * block.ndim)

    small_sharded = [n for n in SHARDED if n not in MATMUL_WEIGHTS]
    big_order = [(n, l) for l in range(DEPTH) for n in MATMUL_WEIGHTS]
    srcs = [local['w_in'][0].astype(COMM_DTYPE)] + [local[n] for n in small_sharded]
    srcs += [local[n][l].astype(COMM_DTYPE) for (n, l) in big_order[1:]]
    n_first = 1 + len(small_sharded)
    groups = [list(range(n_first))] + [[n_first + j] for j in range(len(big_order) - 1)]
    g_sems, g_srcs, g_lands, tok0 = exchange_start(srcs, [True] * len(srcs), groups, "gather_start")
    tok0 = tok0[0, 0]
    pending = {key: gi for gi, key in enumerate(big_order)}

    def finish(gi, after):
        idx = groups[gi]
        _, lands = exchange_wait(g_sems[gi], [g_srcs[t] for t in idx], [g_lands[t] for t in idx], [True] * len(idx),
                                 after, f"gather_wait{gi}")
        return [own_slot(ld, srcs[t]) for ld, t in zip(lands, idx)]

    first = finish(0, local['ln_in_g'] + tok0)
    wts = {n: local[n] for n in REPLICATED}
    wts['ln_in_g'] = local['ln_in_g'] + tok0
    for n, slots in zip(small_sharded, first[1:]):
        wts[n] = _unshard(slots, SHARDED[n])

    def unshard_layer(slots, n):
        if SHARDED[n] == 1:
            return slots.reshape(-1, slots.shape[2])
        return slots.transpose(1, 0, 2).reshape(slots.shape[1], -1)

    def mat(l, n, after):
        gi = pending[(n, l)]
        slots = first[0] if gi == 0 else finish(gi, after)[0]
        w = unshard_layer(slots, n).astype(MXU_DTYPE)
        return _pad_w_in(w) if n == 'w_in' else w

    started = []

    def hook(key, grads):
        tensors = []
        for (n, l), g in grads.items():
            if l is None:
                tensors.append(((n, l), _shard_slots(g, SHARDED[n])))
            else:
                tensors.append(((n, l), _shard_slots(g[None], SHARDED[n])[:, 0].astype(COMM_DTYPE)))
        sems, s_srcs, s_lands, tok = exchange_start([a for _, a in tensors], [False] * len(tensors),
                                                    [list(range(len(tensors)))], "scatter_start_" + key)
        started.append((key, [k for k, _ in tensors], sems[0], s_srcs, s_lands))
        return tok[0, 0]

    loss, dx, grads = local_step(x[0], loss_target[0], wts, mat, hook)

    pieces = [grads[n] for n in REPLICATED] + [loss]
    packed = _pack(pieces)
    p_sems, p_srcs, p_lands, p_tok = exchange_start([packed], [True], [[0]], "gather_small_start")

    parts = {}
    for key, keys, sems, s_srcs, s_lands in started:
        s_out, lands = exchange_wait(sems, s_srcs, s_lands, [False] * len(keys), p_tok, "scatter_wait_" + key)
        for k, so, ld in zip(keys, s_out, lands):
            parts[k] = own_slot(ld, lax.dynamic_index_in_dim(so, me, 0, keepdims=False))
    res = {}
    for n in SHARDED:
        p = [parts[(n, l)] for l in range(DEPTH)] if n in MATMUL_WEIGHTS else parts[(n, None)]
        res[n] = adamw(p, local[n], mom[n], var[n], "adamw_" + n)
    _, p_lands = exchange_wait(p_sems[0], p_srcs, p_lands, [True], res[list(SHARDED)[-1]][1], "gather_small_wait")
    gath = own_slot(p_lands[0], packed)
    zeros1 = jnp.zeros((1, 1), F32)
    pk = lambda d: _pack([d[n] for n in REPLICATED] + [zeros1])[None]
    g_s, d_s, m_s, v_s = adamw(gath[:, None], pk(local), pk(mom), pk(var), "adamw_small")
    shapes = [local[n].shape for n in REPLICATED] + [(1, 1)]
    unp = [_unpack(a[0], shapes) for a in (g_s, d_s, m_s, v_s)]
    for j, n in enumerate(REPLICATED):
        res[n] = tuple(u[j] for u in unp)
    loss_total = unp[0][-1].reshape(())

    return (loss_total, dx[None], *[res[n][0] for n in WEIGHTS], *[res[n][1] for n in WEIGHTS],
            *[res[n][2] for n in WEIGHTS], *[res[n][3] for n in WEIGHTS])
```

```python
import functools
import math

import jax
import jax.numpy as jnp
from jax import lax
from jax.experimental import pallas as pl
from jax.experimental.pallas import tpu as pltpu

F32 = jnp.float32
MXU_DTYPE = jnp.bfloat16
COMM_DTYPE = jnp.bfloat16

N_DEV = 8
D_MODEL = 1024
DEPTH = 2
GRID_W = 64
GROUP_W = 256
HEAD_DIM = 64
CONV_A_WIDTH = 31
CONV_A_HALO = 16
GQA_HEADS = 4
GQA_KV_HEADS = 2
CHUNK = 128
SGU_GROUPS = 4
MLA_HEADS = 4
MLA_Q_LORA = 192
MLA_KV_LORA = 128
MLA_NOPE = 64
MLA_ROPE = 32
MLA_V = 64
MLA_DK_PAD = 128
ROPE_THETA = 10000.0
D_FF = 2816
DEEPNORM_ALPHA = (2 * DEPTH) ** 0.25
LN_EPS = 1e-5
RMS_EPS = 1e-6
D_IN_PROJ = 1888

ADAM_LR = 0.001
ADAM_B1 = 0.9
ADAM_B2 = 0.999
ADAM_EPS = 1e-08
ADAM_WD = 0.01
ADAM_STEP = 10

WEIGHTS = ['ln_in_g', 'ln_in_b', 'w_in', 'conv_a_w', 'conv_a_b', 'ln_a_g', 'ln_a_b', 'qk_norm_q', 'qk_norm_k',
           'sgu_ln_g', 'sgu_ln_b', 'sgu_w', 'sgu_b', 'mla_q_norm', 'mla_w_uq', 'mla_kv_norm', 'mla_w_ukv', 'w_out',
           'ln_mix_g', 'ln_mix_b', 'ffn_w_up', 'ffn_conv_w', 'ffn_conv_b', 'ffn_w_down', 'ln_ffn_g', 'ln_ffn_b']
SHARDED = {'w_in': 2, 'conv_a_w': 2, 'mla_w_uq': 2, 'mla_w_ukv': 2, 'w_out': 1, 'ffn_w_up': 2, 'ffn_conv_w': 2,
           'ffn_w_down': 1}
MATMUL_WEIGHTS = ('w_in', 'w_out', 'ffn_w_up', 'ffn_w_down')
REPLICATED = [n for n in WEIGHTS if n not in SHARDED]

ROW_TILE = 256
VMEM_LIMIT = 56 * 1024 * 1024


def _rawdot(a, b, ca, cb):
    return lax.dot_general(a.astype(MXU_DTYPE), b.astype(MXU_DTYPE), (((ca,), (cb,)), ((), ())),
                           preferred_element_type=F32)


@jax.custom_vjp
def mm_nn(a, b):
    return _rawdot(a, b, 1, 0)


def _mm_nn_fwd(a, b):
    return _rawdot(a, b, 1, 0), (a, b)


def _mm_nn_bwd(res, dy):
    a, b = res
    return _rawdot(dy, b, 1, 1), _rawdot(a, dy, 0, 0)


mm_nn.defvjp(_mm_nn_fwd, _mm_nn_bwd)


@jax.custom_vjp
def mm_nt(a, b):
    return _rawdot(a, b, 1, 1)


def _mm_nt_fwd(a, b):
    return _rawdot(a, b, 1, 1), (a, b)


def _mm_nt_bwd(res, dy):
    a, b = res
    return _rawdot(dy, b, 1, 0), _rawdot(dy, a, 0, 0)


mm_nt.defvjp(_mm_nt_fwd, _mm_nt_bwd)


def _pick_tile(d, cands):
    for c in cands:
        if d % c == 0:
            return c
    return d


def matmul(a, b, mode, out_dtype, name):
    if mode == 'nn':
        (m, k), (k2, n) = a.shape, b.shape
    elif mode == 'nt':
        (m, k), (n, k2) = a.shape, b.shape
    else:
        (k, m), (k2, n) = a.shape, b.shape
    assert k == k2, (a.shape, b.shape, mode)
    tm = _pick_tile(m, (1024, 1408, 512, 256, 128))
    tn = _pick_tile(n, (512, 1408, 256, 128))
    tk = _pick_tile(k, (2816, 2048, 1024, 512, 256, 128))
    nk = k // tk
    ca = 0 if mode == 'tn' else 1
    cb = 1 if mode == 'nt' else 0
    a_spec = pl.BlockSpec((tk, tm), lambda i, j, kk: (kk, i)) if mode == 'tn' else pl.BlockSpec((tm, tk), lambda i, j, kk: (i, kk))
    b_spec = pl.BlockSpec((tn, tk), lambda i, j, kk: (j, kk)) if mode == 'nt' else pl.BlockSpec((tk, tn), lambda i, j, kk: (kk, j))

    def body(a_ref, b_ref, o_ref, acc_ref):
        kk = pl.program_id(2)

        @pl.when(kk == 0)
        def _():
            acc_ref[...] = jnp.zeros_like(acc_ref)

        acc_ref[...] += _rawdot(a_ref[...], b_ref[...], ca, cb)

        @pl.when(kk == nk - 1)
        def _():
            o_ref[...] = acc_ref[...].astype(o_ref.dtype)

    return pl.pallas_call(
        body, grid=(m // tm, n // tn, nk), in_specs=[a_spec, b_spec],
        out_specs=pl.BlockSpec((tm, tn), lambda i, j, kk: (i, j)),
        out_shape=jax.ShapeDtypeStruct((m, n), out_dtype),
        scratch_shapes=[pltpu.VMEM((tm, tn), F32)],
        compiler_params=pltpu.CompilerParams(dimension_semantics=("parallel", "parallel", "arbitrary"),
                                             vmem_limit_bytes=VMEM_LIMIT),
        name=name)(a, b)


class Op:
    def __init__(self, arr, block, imap, grad=False, acc=False, first=None, gdtype=F32, gshape=None, gimap=None):
        self.arr, self.block, self.imap = arr, block, imap
        self.grad, self.acc, self.first, self.gdtype = grad, acc, first, gdtype
        self.gshape = arr.shape if gshape is None else gshape
        self.gimap = imap if gimap is None else gimap


def _row_op(arr, tm, grad=False, gdtype=F32):
    return Op(arr, (tm, arr.shape[1]), lambda i: (i, 0), grad=grad, gdtype=gdtype)


def _par_op(arr, grad=False):
    nd = arr.ndim
    return Op(arr, arr.shape, lambda i: (0,) * nd, grad=grad, acc=True, first=lambda ids: ids[0] == 0)


def _load(ref):
    v = ref[...]
    return v.astype(F32) if jnp.issubdtype(v.dtype, jnp.floating) else v


def stage_fwd(name, fn, ops, outs, grid):
    n_in = len(ops)

    def body(*refs):
        res = fn(*[_load(r) for r in refs[:n_in]])
        for r, o in zip(refs[n_in:], res):
            r[...] = o.astype(r.dtype)

    return pl.pallas_call(
        body, grid=grid, in_specs=[pl.BlockSpec(o.block, o.imap) for o in ops],
        out_specs=[pl.BlockSpec(b, im) for (_, _, b, im) in outs],
        out_shape=[jax.ShapeDtypeStruct(s, d) for (s, d, _, _) in outs],
        compiler_params=pltpu.CompilerParams(dimension_semantics=("parallel",) * len(grid),
                                             vmem_limit_bytes=VMEM_LIMIT),
        name=name)(*[o.arr for o in ops])


def stage_bwd(name, fn, ops, cts, grid, value_acc=False):
    n_in = len(ops)
    ct_flat = [c for group in cts if group is not None for c in group]
    n_ct = len(ct_flat)
    diff = [i for i, o in enumerate(ops) if o.grad]
    any_acc = value_acc or any(ops[i].acc for i in diff)
    ngrid = len(grid)

    def body(*refs):
        ids = [pl.program_id(a) for a in range(ngrid)]
        vals = [_load(r) for r in refs[:n_in]]
        ct_refs = refs[n_in:n_in + n_ct]
        out_refs = refs[n_in + n_ct:]

        def f(*dv):
            full = list(vals)
            for i, v in zip(diff, dv):
                full[i] = v
            return tuple(fn(*full))

        res, vjp = jax.vjp(f, *[vals[i] for i in diff])
        ct, pos = [], 0
        for group, r in zip(cts, res):
            if group is None:
                ct.append(jnp.ones_like(r))
            else:
                tot = None
                for _ in group:
                    c = ct_refs[pos][...].astype(F32)
                    tot = c if tot is None else tot + c
                    pos += 1
                ct.append(tot)
        grads = vjp(tuple(ct))
        for i, g, r in zip(diff, grads, out_refs):
            if ops[i].acc:
                @pl.when(ops[i].first(ids))
                def _(r=r):
                    r[...] = jnp.zeros_like(r)

                r[...] += g.astype(r.dtype)
            else:
                r[...] = g.astype(r.dtype)
        if value_acc:
            r = out_refs[len(diff)]

            @pl.when(ids[0] == 0)
            def _():
                r[...] = jnp.zeros_like(r)

            r[...] += res[0]

    in_specs = [pl.BlockSpec(o.block, o.imap) for o in ops] + [pl.BlockSpec(b, im) for (_, b, im) in ct_flat]
    out_specs = [pl.BlockSpec(ops[i].block, ops[i].gimap) for i in diff]
    out_shape = [jax.ShapeDtypeStruct(ops[i].gshape, ops[i].gdtype) for i in diff]
    if value_acc:
        out_specs.append(pl.BlockSpec((1, 1), lambda *ids: (0, 0)))
        out_shape.append(jax.ShapeDtypeStruct((1, 1), F32))
    sem = ("arbitrary",) * ngrid if any_acc else ("parallel",) * ngrid
    return pl.pallas_call(
        body, grid=grid, in_specs=in_specs, out_specs=out_specs, out_shape=out_shape,
        compiler_params=pltpu.CompilerParams(dimension_semantics=sem, vmem_limit_bytes=VMEM_LIMIT),
        name=name)(*[o.arr for o in ops], *[a for (a, _, _) in ct_flat])


def _sigmoid(x):
    return 1.0 / (1.0 + jnp.exp(-x))


def _silu(x):
    return x * _sigmoid(x)


def _gelu_tanh(x):
    return 0.5 * x * (1.0 + jnp.tanh(math.sqrt(2.0 / math.pi) * (x + 0.044715 * (x * x * x))))


def _ln(x, g, b):
    mu = jnp.mean(x, axis=-1, keepdims=True)
    xc = x - mu
    var = jnp.mean(xc * xc, axis=-1, keepdims=True)
    return xc * lax.rsqrt(var + LN_EPS) * g + b


def _rms(x, g):
    ms = jnp.mean(x * x, axis=-1, keepdims=True)
    return x * lax.rsqrt(ms + RMS_EPS) * g


def _swap_halves(x, half):
    width = x.shape[-1]
    lane = lax.broadcasted_iota(jnp.int32, x.shape, 1)
    return jnp.where(lane % (2 * half) < half, pltpu.roll(x, width - half, 1), pltpu.roll(x, half, 1))


def _make_swap(half):
    @jax.custom_vjp
    def swap(x):
        return _swap_halves(x, half)

    swap.defvjp(lambda x: (_swap_halves(x, half), None), lambda _, dy: (_swap_halves(dy, half),))
    return swap


_swap16, _swap8 = _make_swap(16), _make_swap(8)


def _rope(x, cos, sin_signed, swap):
    return x * cos + swap(x) * sin_signed


def _dot_f32(a, b):
    return jnp.dot(a, b, preferred_element_type=F32, precision=lax.Precision.HIGHEST)


def fn_ln(x, g, b):
    return (_ln(x, g, b),)


def _twice(fn):
    def f(*a):
        (y,) = fn(*a)
        return y, y
    return f


PROJ_W = 2048
P_A, P_Q, P_K, P_V, P_C, P_CQ, P_CKV, P_KR = 0, 512, 768, 896, 1024, 1536, 1792, 1920
CQ_PAD = 256
_CQ_END = P_CQ + MLA_Q_LORA


def _pad_w_in(w):
    z = lambda n: jnp.zeros((w.shape[0], n), w.dtype)
    return jnp.concatenate([w[:, :_CQ_END], z(P_CKV - _CQ_END), w[:, _CQ_END:], z(PROJ_W - P_KR - MLA_ROPE)], axis=1)


def _unpad_w_in(g):
    return jnp.concatenate([g[:, :_CQ_END], g[:, P_CKV:P_KR + MLA_ROPE]], axis=1)


def fn_pre(proj, tab_q, tab_d, seg, place, qng, kng, sg, sb, sw, sbt, mqn, wuq, mkvn, wukv):
    tm = proj.shape[0]
    aglu = proj[:, P_A:P_A + GROUP_W] * _sigmoid(proj[:, P_A + GROUP_W:P_Q])
    b_q, b_k, b_v = proj[:, P_Q:P_K], proj[:, P_K:P_V], proj[:, P_V:P_C]
    cos_q, sin_q = tab_q[:, :GROUP_W], tab_q[:, GROUP_W:]
    q = b_q * lax.rsqrt(_dot_f32(b_q * b_q, seg) + RMS_EPS) * qng
    q = _rope(q, cos_q, sin_q, _swap16)
    k = b_k * lax.rsqrt(_dot_f32(b_k * b_k, seg[:128, :128]) + RMS_EPS) * kng
    k = _rope(k, cos_q[:, :128], sin_q[:, :128], _swap16)
    c = _gelu_tanh(proj[:, P_C:P_CQ])
    u, sv = c[:, :GROUP_W], _ln(c[:, GROUP_W:], sg, sb)
    group = lax.broadcasted_iota(jnp.int32, (CHUNK, GROUP_W), 1) // HEAD_DIM
    rows = []
    for n in range(tm // CHUNK):
        svn = sv[CHUNK * n:CHUNK * (n + 1)]
        acc = jnp.zeros((CHUNK, GROUP_W), F32)
        for g in range(SGU_GROUPS):
            acc = acc + jnp.where(group == g, mm_nn(sw[CHUNK * g:CHUNK * (g + 1)], svn) + sbt[:, g:g + 1], 0.0)
        rows.append(acc)
    o_c = u * jnp.concatenate(rows, axis=0)
    d_cq, d_ckv, d_kr = proj[:, P_CQ:P_CKV], proj[:, P_CKV:P_KR], proj[:, P_KR:PROJ_W]
    cqn = d_cq * lax.rsqrt(jnp.sum(d_cq * d_cq, axis=-1, keepdims=True) * (1.0 / MLA_Q_LORA) + RMS_EPS) * mqn
    cos_d = jnp.concatenate([tab_d[:, :MLA_DK_PAD]] * MLA_HEADS, axis=-1)
    sin_d = jnp.concatenate([tab_d[:, MLA_DK_PAD:]] * MLA_HEADS, axis=-1)
    qf = _rope(mm_nn(cqn, wuq), cos_d, sin_d, _swap8)
    kvd = mm_nn(_rms(d_ckv, mkvn), wukv)
    kf = _rope(kvd[:, :MLA_HEADS * MLA_DK_PAD] + _dot_f32(d_kr, place), cos_d, sin_d, _swap8)
    return aglu, q, k, b_v, o_c, qf, kf, kvd[:, MLA_HEADS * MLA_DK_PAD:]


def fn_aconv(win, w, b, g, beta):
    tm = win.shape[0] - 2 * CONV_A_HALO
    off = CONV_A_HALO - CONV_A_WIDTH // 2
    acc = None
    for kk in range(CONV_A_WIDTH):
        term = win[off + kk:off + kk + tm] * w[kk:kk + 1, :]
        acc = term if acc is None else acc + term
    return (_silu(_ln(acc + b, g, beta)),)


def fn_resln(h, r, g, b):
    return (_ln(DEEPNORM_ALPHA * h + r, g, b),)


def _shift_down(x):
    return jnp.concatenate([jnp.zeros((8, x.shape[1]), F32), x], axis=0)[7:7 + x.shape[0]]


def _shift_up(x):
    return jnp.concatenate([x, jnp.zeros((8, x.shape[1]), F32)], axis=0)[1:1 + x.shape[0]]


def fn_ffnconv(u1, u2, w1, w2, b1, b2):
    c1 = _shift_down(u1) * w1[0:1] + u1 * w1[1:2] + _shift_up(u1) * w1[2:3] + b1
    c2 = _shift_down(u2) * w2[0:1] + u2 * w2[1:2] + _shift_up(u2) * w2[2:3] + b2
    return (_silu(c1) * c2,)


def fn_final(h, r, t, g, b):
    y = _ln(DEEPNORM_ALPHA * h + r, g, b)
    err = (y - t) * (y - t)
    return (0.5 * jnp.sum(jnp.mean(err, axis=-1, keepdims=True), axis=0, keepdims=True),)


def _rope_tables(seq):
    pos = jnp.arange(seq, dtype=jnp.int32)
    row = (pos // GRID_W).astype(F32)
    col = (pos % GRID_W).astype(F32)

    def tab(half):
        inv = ROPE_THETA ** (-jnp.arange(half, dtype=F32) / half)
        ar, ac = row[:, None] * inv[None, :], col[:, None] * inv[None, :]
        cos = jnp.concatenate([jnp.cos(ar)] * 2 + [jnp.cos(ac)] * 2, axis=-1)
        sin = jnp.concatenate([-jnp.sin(ar), jnp.sin(ar), -jnp.sin(ac), jnp.sin(ac)], axis=-1)
        return cos, sin

    cos_b, sin_b = tab(HEAD_DIM // 4)
    tab_q = jnp.concatenate([cos_b] * GQA_HEADS + [sin_b] * GQA_HEADS, axis=-1)
    cos_r, sin_r = tab(MLA_ROPE // 4)
    ones = jnp.ones((seq, MLA_NOPE), F32)
    zpad = MLA_DK_PAD - MLA_NOPE - MLA_ROPE
    tab_d = jnp.concatenate([ones, cos_r, jnp.ones((seq, zpad), F32), 0.0 * ones, sin_r, jnp.zeros((seq, zpad), F32)],
                            axis=-1)
    lane = jnp.arange(GROUP_W)
    seg = jnp.where(lane[:, None] // HEAD_DIM == lane[None, :] // HEAD_DIM, 1.0 / HEAD_DIM, 0.0).astype(F32)
    src, dst = jnp.arange(128)[:, None], jnp.arange(MLA_HEADS * MLA_DK_PAD)[None, :]
    place = jnp.where((src < MLA_ROPE) & (dst % MLA_DK_PAD == MLA_NOPE + src), 1.0, 0.0).astype(F32)
    return tab_q, tab_d, seg, place


def _pre_ops(proj, tabs, kp, grad):
    tm = ROW_TILE
    ops = [_row_op(proj, tm, grad=grad, gdtype=MXU_DTYPE), _row_op(tabs[0], tm), _row_op(tabs[1], tm),
           _par_op(tabs[2]), _par_op(tabs[3])]
    ops += [_par_op(kp[n], grad=grad) for n in ('qng', 'kng', 'sg', 'sb', 'sw', 'sbt', 'mqn', 'wuq', 'mkvn', 'wukv')]
    return ops


PRE_OUT_WIDTHS = (GROUP_W, GROUP_W, 128, 128, GROUP_W, MLA_HEADS * MLA_DK_PAD, MLA_HEADS * MLA_DK_PAD, GROUP_W)


def pre_fwd(proj, tabs, kp, tag):
    seq = proj.shape[0]
    tm = ROW_TILE
    dts = (F32,) + (MXU_DTYPE,) * 7
    outs = [((seq, w), dt, (tm, w), lambda i: (i, 0)) for w, dt in zip(PRE_OUT_WIDTHS, dts)]
    return stage_fwd("pre_fwd" + tag, fn_pre, _pre_ops(proj, tabs, kp, False), outs, (seq // tm,))


def pre_bwd(proj, tabs, kp, cts, tag):
    seq = proj.shape[0]
    tm = ROW_TILE
    ct = [[(c, (tm, c.shape[1]), lambda i: (i, 0))] for c in cts]
    return stage_bwd("pre_bwd" + tag, fn_pre, _pre_ops(proj, tabs, kp, True), ct, (seq // tm,))


def _aconv_ops(kp, grad):
    return [_par_op(kp[n], grad=grad) for n in ('caw', 'cab', 'lag', 'lab')]


def aconv_fwd(aglu_pad, kp, tag):
    seq = aglu_pad.shape[0] - 2 * CONV_A_HALO
    tm = ROW_TILE
    n_par = 4

    def body(x_ref, *refs):
        i = pl.program_id(0)
        win = x_ref[pl.ds(pl.multiple_of(i * tm, tm), tm + 2 * CONV_A_HALO), :]
        (o,) = fn_aconv(win, *[_load(r) for r in refs[:n_par]])
        refs[n_par][...] = o.astype(refs[n_par].dtype)

    pars = _aconv_ops(kp, False)
    return pl.pallas_call(
        body, grid=(seq // tm,),
        in_specs=[pl.BlockSpec(aglu_pad.shape, lambda i: (0, 0))] + [pl.BlockSpec(o.block, o.imap) for o in pars],
        out_specs=pl.BlockSpec((tm, GROUP_W), lambda i: (i, 0)),
        out_shape=jax.ShapeDtypeStruct((seq, GROUP_W), MXU_DTYPE),
        compiler_params=pltpu.CompilerParams(dimension_semantics=("parallel",), vmem_limit_bytes=VMEM_LIMIT),
        name="aconv_fwd" + tag)(aglu_pad, *[o.arr for o in pars])


def aconv_bwd(aglu_pad, kp, d_oa, tag):
    seq = aglu_pad.shape[0] - 2 * CONV_A_HALO
    tm = ROW_TILE
    n_par = 4

    def body(x_ref, *refs):
        i = pl.program_id(0)
        rows = pl.ds(pl.multiple_of(i * tm, tm), tm + 2 * CONV_A_HALO)
        pars = [_load(r) for r in refs[:n_par]]
        ct = refs[n_par][...].astype(F32)
        outs = refs[n_par + 1:]
        _, vjp = jax.vjp(lambda *a: fn_aconv(*a), x_ref[rows, :], *pars)
        grads = vjp((ct,))

        @pl.when(i == 0)
        def _():
            for r in outs:
                r[...] = jnp.zeros_like(r)

        outs[0][rows, :] += grads[0]
        for r, g in zip(outs[1:], grads[1:]):
            r[...] += g

    pars = _aconv_ops(kp, True)
    whole = pl.BlockSpec(aglu_pad.shape, lambda i: (0, 0))
    par_specs = [pl.BlockSpec(o.block, o.imap) for o in pars]
    return pl.pallas_call(
        body, grid=(seq // tm,),
        in_specs=[whole] + par_specs + [pl.BlockSpec((tm, GROUP_W), lambda i: (i, 0))],
        out_specs=[whole] + par_specs,
        out_shape=[jax.ShapeDtypeStruct(aglu_pad.shape, F32)] + [jax.ShapeDtypeStruct(o.arr.shape, F32) for o in pars],
        compiler_params=pltpu.CompilerParams(dimension_semantics=("arbitrary",), vmem_limit_bytes=VMEM_LIMIT),
        name="aconv_bwd" + tag)(aglu_pad, *[o.arr for o in pars], d_oa)


ATTN_TQ = 256
ATTN_TK = 512


def attn_fwd(q3, k3, v3, scale, tag):
    heads, seq, dk = q3.shape
    group = heads // k3.shape[0]
    dv = v3.shape[2]
    tq, tk = min(ATTN_TQ, seq), min(ATTN_TK, seq)

    n_chunks = seq // tk
    log2e = math.log2(math.e)

    def body(q_ref, k_ref, v_ref, o_ref, lse_ref):
        q = q_ref[...]
        scores = lambda c: _rawdot(q, k_ref[pl.ds(c * tk, tk), :], 1, 1)
        m, l, acc = jnp.full((tq, 1), -jnp.inf, F32), jnp.zeros((tq, 1), F32), jnp.zeros((tq, dv), F32)
        s_next = scores(0)
        for c in range(n_chunks):
            s_cur, s_next = s_next, (scores(c + 1) if c + 1 < n_chunks else None)
            t = s_cur * (scale * log2e)
            m_new = jnp.maximum(m, jnp.max(t, axis=-1, keepdims=True))
            alpha = jnp.exp2(m - m_new)
            p = jnp.exp2(t - m_new)
            l = alpha * l + jnp.sum(p, axis=-1, keepdims=True)
            acc = alpha * acc + _rawdot(p, v_ref[pl.ds(c * tk, tk), :], 1, 0)
            m = m_new
        o_ref[...] = acc * (1.0 / l)
        lse_ref[...] = m * (1.0 / log2e) + jnp.log(l)

    return pl.pallas_call(
        body, grid=(heads, seq // tq),
        in_specs=[pl.BlockSpec((None, tq, dk), lambda h, i: (h, i, 0)),
                  pl.BlockSpec((None, seq, dk), lambda h, i: (h // group, 0, 0)),
                  pl.BlockSpec((None, seq, dv), lambda h, i: (h // group, 0, 0))],
        out_specs=[pl.BlockSpec((None, tq, dv), lambda h, i: (h, i, 0)),
                   pl.BlockSpec((None, tq, 1), lambda h, i: (h, i, 0))],
        out_shape=[jax.ShapeDtypeStruct((heads, seq, dv), F32), jax.ShapeDtypeStruct((heads, seq, 1), F32)],
        compiler_params=pltpu.CompilerParams(dimension_semantics=("parallel", "parallel"),
                                             vmem_limit_bytes=VMEM_LIMIT),
        name="attn_fwd" + tag)(q3, k3, v3)


def attn_bwd(q3, k3, v3, o3, lse3, do3, scale, tag):
    heads, seq, dk = q3.shape
    group = heads // k3.shape[0]
    dv = v3.shape[2]
    tq, tk = min(ATTN_TQ, seq), min(ATTN_TK, seq)
    n_chunks = seq // tk
    log2e = math.log2(math.e)

    def body(q_ref, k_ref, v_ref, o_ref, lse_ref, do_ref, dq_ref, dk_ref, dv_ref):
        @pl.when(jnp.logical_and(pl.program_id(0) % group == 0, pl.program_id(1) == 0))
        def _():
            dk_ref[...] = jnp.zeros_like(dk_ref)
            dv_ref[...] = jnp.zeros_like(dv_ref)

        q, do = q_ref[...], do_ref[...]
        dob = do.astype(MXU_DTYPE)
        delta = jnp.sum(do * o_ref[...], axis=-1, keepdims=True)
        lse2 = lse_ref[...] * log2e
        rows = lambda c: pl.ds(c * tk, tk)
        products = lambda c: (_rawdot(q, k_ref[rows(c), :], 1, 1), _rawdot(dob, v_ref[rows(c), :], 1, 1))
        dq = jnp.zeros((tq, dk), F32)
        nxt = products(0)
        for c in range(n_chunks):
            (s_cur, dp_cur), nxt = nxt, (products(c + 1) if c + 1 < n_chunks else None)
            p = jnp.exp2(s_cur * (scale * log2e) - lse2)
            ds = (p * ((dp_cur - delta) * scale)).astype(MXU_DTYPE)
            dv_ref[rows(c), :] += _rawdot(p, dob, 0, 0)
            dk_ref[rows(c), :] += _rawdot(ds, q, 0, 0)
            dq = dq + _rawdot(ds, k_ref[rows(c), :], 1, 0)
        dq_ref[...] = dq

    qspec = lambda d: pl.BlockSpec((None, tq, d), lambda h, i: (h, i, 0))
    kvspec = lambda d: pl.BlockSpec((None, seq, d), lambda h, i: (h // group, 0, 0))
    return pl.pallas_call(
        body, grid=(heads, seq // tq),
        in_specs=[qspec(dk), kvspec(dk), kvspec(dv), qspec(dv), qspec(1), qspec(dv)],
        out_specs=[qspec(dk), kvspec(dk), kvspec(dv)],
        out_shape=[jax.ShapeDtypeStruct(q3.shape, F32), jax.ShapeDtypeStruct(k3.shape, F32),
                   jax.ShapeDtypeStruct(v3.shape, F32)],
        compiler_params=pltpu.CompilerParams(dimension_semantics=("arbitrary", "arbitrary"),
                                             vmem_limit_bytes=VMEM_LIMIT),
        name="attn_bwd" + tag)(q3, k3, v3, o3, lse3, do3)


def resln_fwd(h, r, g, b, tag):
    seq, d = h.shape
    tm = ROW_TILE
    ops = [_row_op(h, tm), _row_op(r, tm), _par_op(g), _par_op(b)]
    outs = [((seq, d), dt, (tm, d), lambda i: (i, 0)) for dt in (F32, MXU_DTYPE)]
    return stage_fwd("resln_fwd" + tag, _twice(fn_resln), ops, outs, (seq // tm,))


def resln_bwd(h, r, g, b, dys, tag):
    seq, d = h.shape
    tm = ROW_TILE
    ops = [_row_op(h, tm, grad=True), _row_op(r, tm, grad=True, gdtype=MXU_DTYPE), _par_op(g, grad=True),
           _par_op(b, grad=True)]
    ct = [[(dy, (tm, d), lambda i: (i, 0)) for dy in dys]]
    return stage_bwd("resln_bwd" + tag, fn_resln, ops, ct, (seq // tm,))


def _ffnconv_ops(up1, up2, w, b, grad):
    seq = up1.shape[0]
    nblk = D_FF // 128
    lo, hi = (lambda j: (0, j)), (lambda j: (0, j + nblk))
    half = lambda a: dict(gshape=(a.shape[0], D_FF), gimap=lo)
    return [Op(up1, (seq, 128), lo, grad=grad, gdtype=MXU_DTYPE), Op(up2, (seq, 128), lo, grad=grad, gdtype=MXU_DTYPE),
            Op(w, (3, 128), lo, grad=grad, **half(w)), Op(w, (3, 128), hi, grad=grad, **half(w)),
            Op(b, (1, 128), lo, grad=grad, **half(b)), Op(b, (1, 128), hi, grad=grad, **half(b))]


def ffnconv_fwd(up1, up2, w, b, tag):
    seq = up1.shape[0]
    outs = [((seq, D_FF), MXU_DTYPE, (seq, 128), lambda j: (0, j))]
    return stage_fwd("ffnconv_fwd" + tag, fn_ffnconv, _ffnconv_ops(up1, up2, w, b, False), outs, (D_FF // 128,))[0]


def ffnconv_bwd(up1, up2, w, b, dact, tag):
    seq = up1.shape[0]
    ct = [[(dact, (seq, 128), lambda j: (0, j))]]
    du1, du2, dw1, dw2, db1, db2 = stage_bwd("ffnconv_bwd" + tag, fn_ffnconv, _ffnconv_ops(up1, up2, w, b, True),
                                             ct, (D_FF // 128,))
    cat = lambda a, b_: jnp.concatenate([a, b_], axis=-1)
    return du1, du2, cat(dw1, dw2), cat(db1, db2)


def final_bwd(h, r, t, g, b, tag):
    seq, d = h.shape
    tm = ROW_TILE
    ops = [_row_op(h, tm, grad=True), _row_op(r, tm, grad=True, gdtype=MXU_DTYPE), _row_op(t, tm),
           _par_op(g, grad=True), _par_op(b, grad=True)]
    return stage_bwd("final_bwd" + tag, fn_final, ops, [None], (seq // tm,), value_acc=True)


def _layer_params(wts, l):
    row = lambda a: a.reshape(1, -1)
    wuq = wts['mla_w_uq'][l].reshape(MLA_Q_LORA, MLA_HEADS, MLA_NOPE + MLA_ROPE)
    wuq = jnp.pad(wuq, ((0, CQ_PAD - MLA_Q_LORA), (0, 0), (0, MLA_DK_PAD - MLA_NOPE - MLA_ROPE)))
    wukv = wts['mla_w_ukv'][l].reshape(MLA_KV_LORA, MLA_HEADS, MLA_NOPE + MLA_V)
    wuk = jnp.pad(wukv[:, :, :MLA_NOPE], ((0, 0), (0, 0), (0, MLA_DK_PAD - MLA_NOPE)))
    return dict(
        qng=jnp.tile(row(wts['qk_norm_q'][l]), (1, GQA_HEADS)), kng=jnp.tile(row(wts['qk_norm_k'][l]), (1, GQA_KV_HEADS)),
        sg=row(wts['sgu_ln_g'][l]), sb=row(wts['sgu_ln_b'][l]),
        sw=wts['sgu_w'][l].reshape(SGU_GROUPS * CHUNK, CHUNK), sbt=wts['sgu_b'][l].T,
        mqn=jnp.pad(row(wts['mla_q_norm'][l]), ((0, 0), (0, CQ_PAD - MLA_Q_LORA))),
        wuq=wuq.reshape(CQ_PAD, MLA_HEADS * MLA_DK_PAD), mkvn=row(wts['mla_kv_norm'][l]),
        wukv=jnp.concatenate([wuk.reshape(MLA_KV_LORA, -1), wukv[:, :, MLA_NOPE:].reshape(MLA_KV_LORA, -1)], axis=1),
        caw=wts['conv_a_w'][l], cab=row(wts['conv_a_b'][l]), lag=row(wts['ln_a_g'][l]), lab=row(wts['ln_a_b'][l]),
        lmg=row(wts['ln_mix_g'][l]), lmb=row(wts['ln_mix_b'][l]),
        fcw=wts['ffn_conv_w'][l], fcb=row(wts['ffn_conv_b'][l]),
        lfg=row(wts['ln_ffn_g'][l]), lfb=row(wts['ln_ffn_b'][l]))


def _to_heads(a, heads):
    seq = a.shape[0]
    return a.reshape(seq, heads, -1).transpose(1, 0, 2)


def _from_heads(a3):
    return a3.transpose(1, 0, 2).reshape(a3.shape[1], -1)


def local_step(x, target, wts, mat, hook):
    seq = x.shape[0]
    tm = ROW_TILE
    tabs = _rope_tables(seq)
    scale_b = HEAD_DIM ** -0.5
    scale_d = (MLA_NOPE + MLA_ROPE) ** -0.5
    ln_in_g, ln_in_b = wts['ln_in_g'].reshape(1, -1), wts['ln_in_b'].reshape(1, -1)

    h, h_m = stage_fwd("ln_in_fwd", _twice(fn_ln), [_row_op(x, tm), _par_op(ln_in_g), _par_op(ln_in_b)],
                       [((seq, D_MODEL), dt, (tm, D_MODEL), lambda i: (i, 0)) for dt in (F32, MXU_DTYPE)],
                       (seq // tm,))
    saved = []
    for l in range(DEPTH):
        tag = f"_l{l}"
        kp, unprep = jax.vjp(lambda w: _layer_params(w, l), wts)
        m = {'w_in': mat(l, 'w_in', h_m)}
        proj = matmul(h_m, m['w_in'], 'nn', F32, "mm_proj" + tag)
        aglu, q, k, v, o_c, qf, kf, vd = pre_fwd(proj, tabs, kp, tag)
        aglu_pad = jnp.pad(aglu, ((CONV_A_HALO, CONV_A_HALO), (0, 0)))
        o_a = aconv_fwd(aglu_pad, kp, tag)
        q3, k3, v3 = _to_heads(q, GQA_HEADS), _to_heads(k, GQA_KV_HEADS), _to_heads(v, GQA_KV_HEADS)
        o_b3, lse_b3 = attn_fwd(q3, k3, v3, scale_b, "_b" + tag)
        qd3, kd3, vd3 = _to_heads(qf, MLA_HEADS), _to_heads(kf, MLA_HEADS), _to_heads(vd, MLA_HEADS)
        o_d3, lse_d3 = attn_fwd(qd3, kd3, vd3, scale_d, "_d" + tag)
        o_cat = jnp.concatenate([o_a, _from_heads(o_b3).astype(MXU_DTYPE), o_c,
                                 _from_heads(o_d3).astype(MXU_DTYPE)], axis=-1)
        m['w_out'] = mat(l, 'w_out', o_cat)
        mix = matmul(o_cat, m['w_out'], 'nn', F32, "mm_mix" + tag)
        h1, h1_m = resln_fwd(h, mix, kp['lmg'], kp['lmb'], "_mix" + tag)
        w_up = mat(l, 'ffn_w_up', h1_m)
        m['w_up1'], m['w_up2'] = w_up[:, :D_FF], w_up[:, D_FF:]
        up1 = matmul(h1_m, m['w_up1'], 'nn', F32, "mm_up1" + tag)
        up2 = matmul(h1_m, m['w_up2'], 'nn', F32, "mm_up2" + tag)
        act = ffnconv_fwd(up1, up2, kp['fcw'], kp['fcb'], tag)
        m['ffn_w_down'] = mat(l, 'ffn_w_down', act)
        f = matmul(act, m['ffn_w_down'], 'nn', F32, "mm_down" + tag)
        saved.append(dict(kp=kp, unprep=unprep, m=m, h=h, h_m=h_m, h1_m=h1_m, proj=proj, o_b3=o_b3, lse_b3=lse_b3,
                          o_d3=o_d3, lse_d3=lse_d3, aglu_pad=aglu_pad, q3=q3, k3=k3, v3=v3, qd3=qd3,
                          kd3=kd3, vd3=vd3, o_cat=o_cat, mix=mix, h1=h1, up1=up1, up2=up2, act=act, f=f))
        if l + 1 < DEPTH:
            h, h_m = resln_fwd(h1, f, kp['lfg'], kp['lfb'], "_ffn" + tag)

    after = lambda a, tok: a if tok is None else a + tok
    small_acc = None
    dh_parts = None
    loss = None
    tok = None
    g_mix = None
    for l in reversed(range(DEPTH)):
        tag = f"_l{l}"
        s = saved[l]
        kp, m = s['kp'], s['m']
        dkp = {}
        lfg = after(kp['lfg'], tok)
        if l == DEPTH - 1:
            dh1_a, df, dkp['lfg'], dkp['lfb'], loss = final_bwd(s['h1'], s['f'], target, lfg, kp['lfb'], tag)
        else:
            dh1_a, df, dkp['lfg'], dkp['lfb'] = resln_bwd(s['h1'], s['f'], lfg, kp['lfb'], dh_parts, "_ffn" + tag)
        dact = matmul(df, m['ffn_w_down'], 'nt', F32, "mm_dact" + tag)
        g_down = matmul(s['act'], df, 'tn', F32, "mm_gdown" + tag)
        dup1, dup2, dkp['fcw'], dkp['fcb'] = ffnconv_bwd(s['up1'], s['up2'], kp['fcw'], kp['fcb'], dact, tag)
        dh1_b1 = matmul(dup1, m['w_up1'], 'nt', F32, "mm_dh1a" + tag)
        dh1_b2 = matmul(dup2, m['w_up2'], 'nt', F32, "mm_dh1b" + tag)
        g_up = jnp.concatenate([matmul(s['h1_m'], dup1, 'tn', F32, "mm_gup1" + tag),
                                matmul(s['h1_m'], dup2, 'tn', F32, "mm_gup2" + tag)], axis=1)
        tok = hook(f"ffn{l}", {('ffn_w_down', l): g_down, ('ffn_w_up', l): g_up})
        dh_a, dmix, dkp['lmg'], dkp['lmb'] = resln_bwd(s['h'], s['mix'], after(kp['lmg'], tok), kp['lmb'],
                                                       [dh1_a, dh1_b1, dh1_b2], "_mix" + tag)
        do_cat = matmul(dmix, m['w_out'], 'nt', F32, "mm_docat" + tag)
        g_out = matmul(s['o_cat'], dmix, 'tn', F32, "mm_gout" + tag)
        do_a, do_b, do_c, do_d = (do_cat[:, GROUP_W * j:GROUP_W * (j + 1)] for j in range(4))
        dq3, dk3, dv3 = attn_bwd(s['q3'], s['k3'], s['v3'], s['o_b3'], s['lse_b3'], _to_heads(do_b, GQA_HEADS),
                                 scale_b, "_b" + tag)
        dqd3, dkd3, dvd3 = attn_bwd(s['qd3'], s['kd3'], s['vd3'], s['o_d3'], s['lse_d3'],
                                    _to_heads(do_d, MLA_HEADS), scale_d, "_d" + tag)
        daglu_pad, dkp['caw'], dkp['cab'], dkp['lag'], dkp['lab'] = aconv_bwd(s['aglu_pad'], kp, do_a, tag)
        cts = [daglu_pad[CONV_A_HALO:CONV_A_HALO + seq], _from_heads(dq3), _from_heads(dk3), _from_heads(dv3), do_c,
               _from_heads(dqd3), _from_heads(dkd3), _from_heads(dvd3)]
        pre_g = pre_bwd(s['proj'], tabs, kp, cts, tag)
        dproj = pre_g[0]
        for n, g in zip(('qng', 'kng', 'sg', 'sb', 'sw', 'sbt', 'mqn', 'wuq', 'mkvn', 'wukv'), pre_g[1:]):
            dkp[n] = g
        dh_b = matmul(dproj, m['w_in'], 'nt', F32, "mm_dh" + tag)
        g_in = matmul(s['h_m'], dproj, 'tn', F32, "mm_gin" + tag)
        dh_parts = [dh_a, dh_b]
        (dw,) = s['unprep'](dkp)
        small_acc = dw if small_acc is None else jax.tree.map(jnp.add, small_acc, dw)
        g_mix = {('w_out', l): g_out, ('w_in', l): _unpad_w_in(g_in)}
        if l > 0:
            tok = hook(f"mix{l}", g_mix)

    g_mix.update({(n, None): small_acc[n] for n in SHARDED if n not in MATMUL_WEIGHTS})
    tok = hook("last", g_mix)
    dx, dg, db = stage_bwd("ln_in_bwd", fn_ln,
                           [_row_op(x, tm, grad=True), _par_op(after(ln_in_g, tok), grad=True),
                            _par_op(ln_in_b, grad=True)],
                           [[(p, (tm, D_MODEL), lambda i: (i, 0)) for p in dh_parts]], (seq // tm,))
    out = {n: small_acc[n] for n in REPLICATED}
    out['ln_in_g'], out['ln_in_b'] = dg.reshape(-1), db.reshape(-1)
    return loss, dx, out


def _peer(x, y, c, r):
    return ((1 - x) if r & 4 else x, (1 - y) if r & 2 else y, (1 - c) if r & 1 else c)


def _exchange_copy(src_ref, land_ref, send_sems, recv_sems, k, gather, x, y, c, r):
    px, py, pc = _peer(x, y, c, r)
    me, peer = 4 * x + 2 * y + c, 4 * px + 2 * py + pc
    src = src_ref if gather else src_ref.at[peer]
    mk = lambda dst: pltpu.make_async_remote_copy(
        src_ref=src, dst_ref=dst, send_sem=send_sems.at[k * (N_DEV - 1) + r - 1],
        recv_sem=recv_sems.at[k * (N_DEV - 1) + r - 1],
        device_id=(px, py, pc), device_id_type=pl.DeviceIdType.MESH)
    return mk(land_ref.at[me]), mk(land_ref.at[peer])


_HBM_SPEC = pl.BlockSpec(memory_space=pltpu.HBM)
_SEM_SPEC = pl.BlockSpec(memory_space=pltpu.SEMAPHORE)


def exchange_start(srcs, gather, groups, name):
    n_t = len(srcs)
    lands =[lax.empty(((N_DEV,) + s.shape) if gt else s.shape, s.dtype) for s, gt in zip(srcs, gather)]

    def body(*refs):
        src_refs, land_refs = refs[:n_t], refs[n_t:2 * n_t]
        sem_refs = refs[2 * n_t:2 * n_t + 2 * len(groups)]
        token = refs[-1]
        x, y, c = lax.axis_index("x"), lax.axis_index("y"), lax.axis_index("c")
        for gi, g in enumerate(groups):
            for k, t in enumerate(g):
                for r in range(1, N_DEV):
                    _exchange_copy(src_refs[t], land_refs[t], sem_refs[2 * gi], sem_refs[2 * gi + 1], k, gather[t],
                                   x, y, c, r)[0].start()
        token[...] = jnp.zeros_like(token)

    sem_shapes = []
    for g in groups:
        sem_shapes += [pltpu.SemaphoreType.DMA((len(g) * (N_DEV - 1),))] * 2
    hbm_shapes = [pltpu.HBM(a.shape, a.dtype) for a in list(srcs) + lands]
    n_sem = len(sem_shapes)
    res = pl.pallas_call(
        body, name=name,
        out_shape=tuple(sem_shapes + hbm_shapes + [jax.ShapeDtypeStruct((8, 128), F32)]),
        in_specs=[_HBM_SPEC] * (2 * n_t),
        out_specs=tuple([_SEM_SPEC] * n_sem + [_HBM_SPEC] * (2 * n_t) + [pl.BlockSpec(memory_space=pltpu.VMEM)]),
        input_output_aliases={i: n_sem + i for i in range(2 * n_t)},
        compiler_params=pltpu.CompilerParams(has_side_effects=pltpu.SideEffectType.DATAFLOW_SIDE_EFFECTING),
    )(*[pltpu.with_memory_space_constraint(a, pltpu.HBM) for a in list(srcs) + lands])
    sems = [(res[2 * gi], res[2 * gi + 1]) for gi in range(len(groups))]
    return sems, list(res[n_sem:n_sem + n_t]), list(res[n_sem + n_t:n_sem + 2 * n_t]), res[-1]


def exchange_wait(sems, srcs, lands, gather, after, name):
    n_t = len(srcs)

    def body(*refs):
        src_refs, land_refs = refs[:n_t], refs[n_t:2 * n_t]
        send_sems, recv_sems = refs[2 * n_t], refs[2 * n_t + 1]
        x, y, c = lax.axis_index("x"), lax.axis_index("y"), lax.axis_index("c")
        for k in range(n_t):
            for r in range(1, N_DEV):
                send, recv = _exchange_copy(src_refs[k], land_refs[k], send_sems, recv_sems, k, gather[k], x, y, c, r)
                send.wait_send()
                recv.wait_recv()

    res = pl.pallas_call(
        body, name=name,
        out_shape=tuple(pltpu.HBM(a.shape, a.dtype) for a in list(srcs) + list(lands)),
        in_specs=[_HBM_SPEC] * (2 * n_t) + [_SEM_SPEC, _SEM_SPEC, pl.BlockSpec(memory_space=pl.ANY)],
        out_specs=tuple([_HBM_SPEC] * (2 * n_t)),
        input_output_aliases={i: i for i in range(2 * n_t)},
        compiler_params=pltpu.CompilerParams(has_side_effects=pltpu.SideEffectType.DATAFLOW_SIDE_EFFECTING),
    )(*srcs, *lands, sems[0], sems[1], after)
    return list(res[:n_t]), list(res[n_t:])


def adamw(parts, w, m, v, name):
    n_l, n_r, n_c = w.shape
    tr = n_r
    if n_r % 8 == 0:
        for cand in (512, 256, 128, 64, 32, 16, 8):
            if n_r % cand == 0 and cand * n_c * 4 <= 512 * 1024:
                tr = cand
                break
    c1 = 1.0 - ADAM_B1 ** ADAM_STEP
    c2 = 1.0 - ADAM_B2 ** ADAM_STEP
    per_layer = isinstance(parts, (list, tuple))
    n_p = n_l if per_layer else 1
    n_rb = n_r // tr

    def update(g, w_ref, m_ref, v_ref, g_ref, d_ref, nm_ref, nv_ref):
        w_, m_, v_ = w_ref[0], m_ref[0], v_ref[0]
        nm = ADAM_B1 * m_ + (1.0 - ADAM_B1) * g
        nv = ADAM_B2 * v_ + (1.0 - ADAM_B2) * (g * g)
        g_ref[0] = g
        nm_ref[0] = nm
        nv_ref[0] = nv
        d_ref[0] = -ADAM_LR * ((nm / c1) / (jnp.sqrt(nv / c2) + ADAM_EPS) + ADAM_WD * w_)

    def body(*refs):
        p_refs, rest = refs[:n_p], refs[n_p:]
        if not per_layer:
            g = p_refs[0][0, 0].astype(F32)
            for s in range(1, N_DEV):
                g = g + p_refs[0][s, 0].astype(F32)
            update(g, *rest)
        else:
            for lay in range(n_l):
                @pl.when(pl.program_id(0) == lay)
                def _(lay=lay):
                    g = p_refs[lay][0].astype(F32)
                    for s in range(1, N_DEV):
                        g = g + p_refs[lay][s].astype(F32)
                    update(g, *rest)

    blk = pl.BlockSpec((1, tr, n_c), lambda l, r: (l, r, 0))
    if per_layer:
        def p_spec(lay):
            park = 0 if lay > 0 else n_rb - 1
            return pl.BlockSpec((N_DEV, tr, n_c), lambda l, r: (0, jnp.where(l == lay, r, park), 0))
        p_specs, p_args = [p_spec(lay) for lay in range(n_l)], list(parts)
    else:
        p_specs, p_args = [pl.BlockSpec((N_DEV, 1, tr, n_c), lambda l, r: (0, l, r, 0))], [parts]
    return pl.pallas_call(
        body, grid=(n_l, n_rb), in_specs=p_specs + [blk, blk, blk],
        out_specs=[blk] * 4, out_shape=[jax.ShapeDtypeStruct(w.shape, F32)] * 4,
        compiler_params=pltpu.CompilerParams(dimension_semantics=("arbitrary", "arbitrary"),
                                             vmem_limit_bytes=VMEM_LIMIT),
        name=name)(*p_args, w, m, v)


def adamw_replicated(lands, own, ws, ms, vs, loss_land, loss_own):
    n_t = len(lands)
    c1 = 1.0 - ADAM_B1 ** ADAM_STEP
    c2 = 1.0 - ADAM_B2 ** ADAM_STEP

    def body(*refs):
        ins, outs = refs[:5 * n_t + 2], refs[5 * n_t + 2:]
        me = 4 * lax.axis_index("x") + 2 * lax.axis_index("y") + lax.axis_index("c")

        def total(land_ref, own_ref):
            g = None
            for s in range(N_DEV):
                term = jnp.where(me == s, own_ref[...], land_ref[s])
                g = term if g is None else g + term
            return g

        for t in range(n_t):
            land_ref, own_ref, w_ref, m_ref, v_ref = ins[5 * t:5 * t + 5]
            g = total(land_ref, own_ref)
            nm = ADAM_B1 * m_ref[...] + (1.0 - ADAM_B1) * g
            nv = ADAM_B2 * v_ref[...] + (1.0 - ADAM_B2) * (g * g)
            g_ref, d_ref, nm_ref, nv_ref = outs[4 * t:4 * t + 4]
            g_ref[...] = g
            nm_ref[...] = nm
            nv_ref[...] = nv
            d_ref[...] = -ADAM_LR * ((nm / c1) / (jnp.sqrt(nv / c2) + ADAM_EPS) + ADAM_WD * w_ref[...])
        outs[4 * n_t][...] = total(ins[5 * n_t], ins[5 * n_t + 1])

    args = []
    for t in range(n_t):
        args += [lands[t], own[t], ws[t], ms[t], vs[t]]
    out_shape = []
    for t in range(n_t):
        out_shape += [jax.ShapeDtypeStruct(ws[t].shape, F32)] * 4
    out_shape.append(jax.ShapeDtypeStruct(loss_own.shape, F32))
    res = pl.pallas_call(body, out_shape=out_shape,
                         compiler_params=pltpu.CompilerParams(vmem_limit_bytes=VMEM_LIMIT),
                         name="adamw_replicated")(*args, loss_land, loss_own)
    return [tuple(res[4 * t:4 * t + 4]) for t in range(n_t)], res[-1]


def _shard_slots(g, axis):
    if axis == 1:
        return g.reshape(g.shape[0], N_DEV, g.shape[1] // N_DEV, g.shape[2]).transpose(1, 0, 2, 3)
    return g.reshape(g.shape[0], g.shape[1], N_DEV, g.shape[2] // N_DEV).transpose(2, 0, 1, 3)


def _unshard(slots, axis):
    if axis == 1:
        return slots.transpose(1, 0, 2, 3).reshape(slots.shape[1], -1, slots.shape[3])
    return slots.transpose(1, 2, 0, 3).reshape(slots.shape[1], slots.shape[2], -1)


def kernel(x, ln_in_g, ln_in_b, w_in, conv_a_w, conv_a_b, ln_a_g, ln_a_b, qk_norm_q, qk_norm_k, sgu_ln_g, sgu_ln_b, sgu_w, sgu_b, mla_q_norm, mla_w_uq, mla_kv_norm, mla_w_ukv, w_out, ln_mix_g, ln_mix_b, ffn_w_up, ffn_conv_w, ffn_conv_b, ffn_w_down, ln_ffn_g, ln_ffn_b, loss_target, m_ln_in_g, m_ln_in_b, m_w_in, m_conv_a_w, m_conv_a_b, m_ln_a_g, m_ln_a_b, m_qk_norm_q, m_qk_norm_k, m_sgu_ln_g, m_sgu_ln_b, m_sgu_w, m_sgu_b, m_mla_q_norm, m_mla_w_uq, m_mla_kv_norm, m_mla_w_ukv, m_w_out, m_ln_mix_g, m_ln_mix_b, m_ffn_w_up, m_ffn_conv_w, m_ffn_conv_b, m_ffn_w_down, m_ln_ffn_g, m_ln_ffn_b, v_ln_in_g, v_ln_in_b, v_w_in, v_conv_a_w, v_conv_a_b, v_ln_a_g, v_ln_a_b, v_qk_norm_q, v_qk_norm_k, v_sgu_ln_g, v_sgu_ln_b, v_sgu_w, v_sgu_b, v_mla_q_norm, v_mla_w_uq, v_mla_kv_norm, v_mla_w_ukv, v_w_out, v_ln_mix_g, v_ln_mix_b, v_ffn_w_up, v_ffn_conv_w, v_ffn_conv_b, v_ffn_w_down, v_ln_ffn_g, v_ln_ffn_b):
    local = dict(ln_in_g=ln_in_g, ln_in_b=ln_in_b, w_in=w_in, conv_a_w=conv_a_w, conv_a_b=conv_a_b, ln_a_g=ln_a_g, ln_a_b=ln_a_b, qk_norm_q=qk_norm_q, qk_norm_k=qk_norm_k, sgu_ln_g=sgu_ln_g, sgu_ln_b=sgu_ln_b, sgu_w=sgu_w, sgu_b=sgu_b, mla_q_norm=mla_q_norm, mla_w_uq=mla_w_uq, mla_kv_norm=mla_kv_norm, mla_w_ukv=mla_w_ukv, w_out=w_out, ln_mix_g=ln_mix_g, ln_mix_b=ln_mix_b, ffn_w_up=ffn_w_up, ffn_conv_w=ffn_conv_w, ffn_conv_b=ffn_conv_b, ffn_w_down=ffn_w_down, ln_ffn_g=ln_ffn_g, ln_ffn_b=ln_ffn_b)
    mom = dict(ln_in_g=m_ln_in_g, ln_in_b=m_ln_in_b, w_in=m_w_in, conv_a_w=m_conv_a_w, conv_a_b=m_conv_a_b, ln_a_g=m_ln_a_g, ln_a_b=m_ln_a_b, qk_norm_q=m_qk_norm_q, qk_norm_k=m_qk_norm_k, sgu_ln_g=m_sgu_ln_g, sgu_ln_b=m_sgu_ln_b, sgu_w=m_sgu_w, sgu_b=m_sgu_b, mla_q_norm=m_mla_q_norm, mla_w_uq=m_mla_w_uq, mla_kv_norm=m_mla_kv_norm, mla_w_ukv=m_mla_w_ukv, w_out=m_w_out, ln_mix_g=m_ln_mix_g, ln_mix_b=m_ln_mix_b, ffn_w_up=m_ffn_w_up, ffn_conv_w=m_ffn_conv_w, ffn_conv_b=m_ffn_conv_b, ffn_w_down=m_ffn_w_down, ln_ffn_g=m_ln_ffn_g, ln_ffn_b=m_ln_ffn_b)
    var = dict(ln_in_g=v_ln_in_g, ln_in_b=v_ln_in_b, w_in=v_w_in, conv_a_w=v_conv_a_w, conv_a_b=v_conv_a_b, ln_a_g=v_ln_a_g, ln_a_b=v_ln_a_b, qk_norm_q=v_qk_norm_q, qk_norm_k=v_qk_norm_k, sgu_ln_g=v_sgu_ln_g, sgu_ln_b=v_sgu_ln_b, sgu_w=v_sgu_w, sgu_b=v_sgu_b, mla_q_norm=v_mla_q_norm, mla_w_uq=v_mla_w_uq, mla_kv_norm=v_mla_kv_norm, mla_w_ukv=v_mla_w_ukv, w_out=v_w_out, ln_mix_g=v_ln_mix_g, ln_mix_b=v_ln_mix_b, ffn_w_up=v_ffn_w_up, ffn_conv_w=v_ffn_conv_w, ffn_conv_b=v_ffn_conv_b, ffn_w_down=v_ffn_w_down, ln_ffn_g=v_ln_ffn_g, ln_ffn_b=v_ln_ffn_b)

    me = 4 * lax.axis_index("x") + 2 * lax.axis_index("y") + lax.axis_index("c")

    def own_slot(slots, block):
        return lax.dynamic_update_slice(slots, block[None], (me,) + (0,) * block.ndim)

    small_sharded = [n for n in SHARDED if n not in MATMUL_WEIGHTS]
    big_order = [(n, l) for l in range(DEPTH) for n in MATMUL_WEIGHTS]
    srcs = [local['w_in'][0].astype(COMM_DTYPE)] + [local[n] for n in small_sharded]
    srcs += [local[n][l].astype(COMM_DTYPE) for (n, l) in big_order[1:]]
    n_first = 1 + len(small_sharded)
    groups = [list(range(n_first))] + [[n_first + j] for j in range(len(big_order) - 1)]
    g_sems, g_srcs, g_lands, tok0 = exchange_start(srcs, [True] * len(srcs), groups, "gather_start")
    tok0 = tok0[0, 0]
    pending = {key: gi for gi, key in enumerate(big_order)}

    def finish(gi, after):
        idx = groups[gi]
        _, lands = exchange_wait(g_sems[gi], [g_srcs[t] for t in idx], [g_lands[t] for t in idx], [True] * len(idx),
                                 after, f"gather_wait{gi}")
        return [own_slot(ld, srcs[t]) for ld, t in zip(lands, idx)]

    first = finish(0, local['ln_in_g'] + tok0)
    wts = {n: local[n] for n in REPLICATED}
    wts['ln_in_g'] = local['ln_in_g'] + tok0
    for n, slots in zip(small_sharded, first[1:]):
        wts[n] = _unshard(slots, SHARDED[n])

    def unshard_layer(slots, n):
        if SHARDED[n] == 1:
            return slots.reshape(-1, slots.shape[2])
        return slots.transpose(1, 0, 2).reshape(slots.shape[1], -1)

    def mat(l, n, after):
        gi = pending[(n, l)]
        slots = first[0] if gi == 0 else finish(gi, after)[0]
        w = unshard_layer(slots, n).astype(MXU_DTYPE)
        return _pad_w_in(w) if n == 'w_in' else w

    started = []

    def hook(key, grads):
        tensors = []
        for (n, l), g in grads.items():
            if l is None:
                tensors.append(((n, l), _shard_slots(g, SHARDED[n])))
            else:
                tensors.append(((n, l), _shard_slots(g[None], SHARDED[n])[:, 0].astype(COMM_DTYPE)))
        sems, s_srcs, s_lands, tok = exchange_start([a for _, a in tensors], [False] * len(tensors),
                                                    [list(range(len(tensors)))], "scatter_start_" + key)
        started.append((key, [k for k, _ in tensors], sems[0], s_srcs, s_lands))
        return tok[0, 0]

    loss, dx, grads = local_step(x[0], loss_target[0], wts, mat, hook)

    as2d = lambda a: a.reshape(-1, a.shape[-1]) if a.ndim > 1 else a.reshape(1, -1)
    small_g = [as2d(grads[n]) for n in REPLICATED] + [jnp.broadcast_to(loss, (8, 128))]
    p_sems, p_srcs, p_lands, p_tok = exchange_start(small_g, [True] * len(small_g), [list(range(len(small_g)))],
                                                    "gather_small_start")

    parts = {}
    for key, keys, sems, s_srcs, s_lands in started:
        s_out, lands = exchange_wait(sems, s_srcs, s_lands, [False] * len(keys), p_tok, "scatter_wait_" + key)
        for k, so, ld in zip(keys, s_out, lands):
            parts[k] = own_slot(ld, lax.dynamic_index_in_dim(so, me, 0, keepdims=False))
    res = {}
    for n in SHARDED:
        p = [parts[(n, l)] for l in range(DEPTH)] if n in MATMUL_WEIGHTS else parts[(n, None)]
        res[n] = adamw(p, local[n], mom[n], var[n], "adamw_" + n)
    updated = jnp.zeros((8, 128), F32) + sum(res[n][1][0, 0, 0] for n in SHARDED)
    p_own, p_lands = exchange_wait(p_sems[0], p_srcs, p_lands, [True] * len(small_g), updated, "gather_small_wait")
    small, loss_sum = adamw_replicated(p_lands[:-1], p_own[:-1], [as2d(local[n]) for n in REPLICATED],
                                       [as2d(mom[n]) for n in REPLICATED], [as2d(var[n]) for n in REPLICATED],
                                       p_lands[-1], p_own[-1])
    for n, quad in zip(REPLICATED, small):
        res[n] = tuple(a.reshape(local[n].shape) for a in quad)
    loss_total = loss_sum[0, 0]

    return (loss_total, dx[None], *[res[n][0] for n in WEIGHTS], *[res[n][1] for n in WEIGHTS],
            *[res[n][2] for n in WEIGHTS], *[res[n][3] for n in WEIGHTS])
```

```python
import functools
import math

import jax
import jax.numpy as jnp
from jax import lax
from jax.experimental import pallas as pl
from jax.experimental.pallas import tpu as pltpu

F32 = jnp.float32
MXU_DTYPE = jnp.bfloat16
COMM_DTYPE = jnp.bfloat16

N_DEV = 8
D_MODEL = 1024
DEPTH = 2
GRID_W = 64
GROUP_W = 256
HEAD_DIM = 64
CONV_A_WIDTH = 31
CONV_A_HALO = 16
GQA_HEADS = 4
GQA_KV_HEADS = 2
CHUNK = 128
SGU_GROUPS = 4
MLA_HEADS = 4
MLA_Q_LORA = 192
MLA_KV_LORA = 128
MLA_NOPE = 64
MLA_ROPE = 32
MLA_V = 64
MLA_DK_PAD = 128
ROPE_THETA = 10000.0
D_FF = 2816
DEEPNORM_ALPHA = (2 * DEPTH) ** 0.25
LN_EPS = 1e-5
RMS_EPS = 1e-6
D_IN_PROJ = 1888

ADAM_LR = 0.001
ADAM_B1 = 0.9
ADAM_B2 = 0.999
ADAM_EPS = 1e-08
ADAM_WD = 0.01
ADAM_STEP = 10

WEIGHTS = ['ln_in_g', 'ln_in_b', 'w_in', 'conv_a_w', 'conv_a_b', 'ln_a_g', 'ln_a_b', 'qk_norm_q', 'qk_norm_k',
           'sgu_ln_g', 'sgu_ln_b', 'sgu_w', 'sgu_b', 'mla_q_norm', 'mla_w_uq', 'mla_kv_norm', 'mla_w_ukv', 'w_out',
           'ln_mix_g', 'ln_mix_b', 'ffn_w_up', 'ffn_conv_w', 'ffn_conv_b', 'ffn_w_down', 'ln_ffn_g', 'ln_ffn_b']
SHARDED = {'w_in': 2, 'conv_a_w': 2, 'mla_w_uq': 2, 'mla_w_ukv': 2, 'w_out': 1, 'ffn_w_up': 2, 'ffn_conv_w': 2,
           'ffn_w_down': 1}
MATMUL_WEIGHTS = ('w_in', 'w_out', 'ffn_w_up', 'ffn_w_down')
REPLICATED = [n for n in WEIGHTS if n not in SHARDED]

ROW_TILE = 256
VMEM_LIMIT = 56 * 1024 * 1024


def _rawdot(a, b, ca, cb):
    return lax.dot_general(a.astype(MXU_DTYPE), b.astype(MXU_DTYPE), (((ca,), (cb,)), ((), ())),
                           preferred_element_type=F32)


@jax.custom_vjp
def mm_nn(a, b):
    return _rawdot(a, b, 1, 0)


def _mm_nn_fwd(a, b):
    return _rawdot(a, b, 1, 0), (a, b)


def _mm_nn_bwd(res, dy):
    a, b = res
    return _rawdot(dy, b, 1, 1), _rawdot(a, dy, 0, 0)


mm_nn.defvjp(_mm_nn_fwd, _mm_nn_bwd)


@jax.custom_vjp
def mm_nt(a, b):
    return _rawdot(a, b, 1, 1)


def _mm_nt_fwd(a, b):
    return _rawdot(a, b, 1, 1), (a, b)


def _mm_nt_bwd(res, dy):
    a, b = res
    return _rawdot(dy, b, 1, 0), _rawdot(dy, a, 0, 0)


mm_nt.defvjp(_mm_nt_fwd, _mm_nt_bwd)


def _pick_tile(d, cands):
    for c in cands:
        if d % c == 0:
            return c
    return d


def matmul(a, b, mode, out_dtype, name):
    if mode == 'nn':
        (m, k), (k2, n) = a.shape, b.shape
    elif mode == 'nt':
        (m, k), (n, k2) = a.shape, b.shape
    else:
        (k, m), (k2, n) = a.shape, b.shape
    assert k == k2, (a.shape, b.shape, mode)
    tm = _pick_tile(m, (1024, 1408, 512, 256, 128))
    tn = _pick_tile(n, (512, 1408, 256, 128))
    tk = _pick_tile(k, (2816, 2048, 1024, 512, 256, 128))
    nk = k // tk
    ca = 0 if mode == 'tn' else 1
    cb = 1 if mode == 'nt' else 0
    a_spec = pl.BlockSpec((tk, tm), lambda i, j, kk: (kk, i)) if mode == 'tn' else pl.BlockSpec((tm, tk), lambda i, j, kk: (i, kk))
    b_spec = pl.BlockSpec((tn, tk), lambda i, j, kk: (j, kk)) if mode == 'nt' else pl.BlockSpec((tk, tn), lambda i, j, kk: (kk, j))

    def body(a_ref, b_ref, o_ref, acc_ref):
        kk = pl.program_id(2)

        @pl.when(kk == 0)
        def _():
            acc_ref[...] = jnp.zeros_like(acc_ref)

        acc_ref[...] += _rawdot(a_ref[...], b_ref[...], ca, cb)

        @pl.when(kk == nk - 1)
        def _():
            o_ref[...] = acc_ref[...].astype(o_ref.dtype)

    return pl.pallas_call(
        body, grid=(m // tm, n // tn, nk), in_specs=[a_spec, b_spec],
        out_specs=pl.BlockSpec((tm, tn), lambda i, j, kk: (i, j)),
        out_shape=jax.ShapeDtypeStruct((m, n), out_dtype),
        scratch_shapes=[pltpu.VMEM((tm, tn), F32)],
        compiler_params=pltpu.CompilerParams(dimension_semantics=("parallel", "parallel", "arbitrary"),
                                             vmem_limit_bytes=VMEM_LIMIT),
        name=name)(a, b)


class Op:
    def __init__(self, arr, block, imap, grad=False, acc=False, first=None, gdtype=F32, gshape=None, gimap=None):
        self.arr, self.block, self.imap = arr, block, imap
        self.grad, self.acc, self.first, self.gdtype = grad, acc, first, gdtype
        self.gshape = arr.shape if gshape is None else gshape
        self.gimap = imap if gimap is None else gimap


def _row_op(arr, tm, grad=False, gdtype=F32):
    return Op(arr, (tm, arr.shape[1]), lambda i: (i, 0), grad=grad, gdtype=gdtype)


def _par_op(arr, grad=False):
    nd = arr.ndim
    return Op(arr, arr.shape, lambda i: (0,) * nd, grad=grad, acc=True, first=lambda ids: ids[0] == 0)


def _load(ref):
    v = ref[...]
    return v.astype(F32) if jnp.issubdtype(v.dtype, jnp.floating) else v


def _store_heads(ref, val):
    if len(ref.shape) == 2:
        ref[...] = val.astype(ref.dtype)
    else:
        d = ref.shape[2]
        for h in range(ref.shape[0]):
            ref[h] = val[:, d * h:d * (h + 1)].astype(ref.dtype)


def _load_heads(ref):
    if len(ref.shape) == 2:
        return ref[...].astype(F32)
    return jnp.concatenate([ref[h].astype(F32) for h in range(ref.shape[0])], axis=-1)


def stage_fwd(name, fn, ops, outs, grid):
    n_in = len(ops)

    def body(*refs):
        res = fn(*[_load(r) for r in refs[:n_in]])
        for r, o in zip(refs[n_in:], res):
            _store_heads(r, o)

    return pl.pallas_call(
        body, grid=grid, in_specs=[pl.BlockSpec(o.block, o.imap) for o in ops],
        out_specs=[pl.BlockSpec(b, im) for (_, _, b, im) in outs],
        out_shape=[jax.ShapeDtypeStruct(s, d) for (s, d, _, _) in outs],
        compiler_params=pltpu.CompilerParams(dimension_semantics=("parallel",) * len(grid),
                                             vmem_limit_bytes=VMEM_LIMIT),
        name=name)(*[o.arr for o in ops])


def stage_bwd(name, fn, ops, cts, grid, value_acc=False):
    n_in = len(ops)
    ct_flat = [c for group in cts if group is not None for c in group]
    n_ct = len(ct_flat)
    diff = [i for i, o in enumerate(ops) if o.grad]
    any_acc = value_acc or any(ops[i].acc for i in diff)
    ngrid = len(grid)

    def body(*refs):
        ids = [pl.program_id(a) for a in range(ngrid)]
        vals = [_load(r) for r in refs[:n_in]]
        ct_refs = refs[n_in:n_in + n_ct]
        out_refs = refs[n_in + n_ct:]

        def f(*dv):
            full = list(vals)
            for i, v in zip(diff, dv):
                full[i] = v
            return tuple(fn(*full))

        res, vjp = jax.vjp(f, *[vals[i] for i in diff])
        ct, pos = [], 0
        for group, r in zip(cts, res):
            if group is None:
                ct.append(jnp.ones_like(r))
            else:
                tot = None
                for _ in group:
                    c = _load_heads(ct_refs[pos])
                    tot = c if tot is None else tot + c
                    pos += 1
                ct.append(tot)
        grads = vjp(tuple(ct))
        for i, g, r in zip(diff, grads, out_refs):
            if ops[i].acc:
                @pl.when(ops[i].first(ids))
                def _(r=r):
                    r[...] = jnp.zeros_like(r)

                r[...] += g.astype(r.dtype)
            else:
                r[...] = g.astype(r.dtype)
        if value_acc:
            r = out_refs[len(diff)]

            @pl.when(ids[0] == 0)
            def _():
                r[...] = jnp.zeros_like(r)

            r[...] += res[0]

    in_specs = [pl.BlockSpec(o.block, o.imap) for o in ops] + [pl.BlockSpec(b, im) for (_, b, im) in ct_flat]
    out_specs = [pl.BlockSpec(ops[i].block, ops[i].gimap) for i in diff]
    out_shape = [jax.ShapeDtypeStruct(ops[i].gshape, ops[i].gdtype) for i in diff]
    if value_acc:
        out_specs.append(pl.BlockSpec((1, 1), lambda *ids: (0, 0)))
        out_shape.append(jax.ShapeDtypeStruct((1, 1), F32))
    sem = ("arbitrary",) * ngrid if any_acc else ("parallel",) * ngrid
    return pl.pallas_call(
        body, grid=grid, in_specs=in_specs, out_specs=out_specs, out_shape=out_shape,
        compiler_params=pltpu.CompilerParams(dimension_semantics=sem, vmem_limit_bytes=VMEM_LIMIT),
        name=name)(*[o.arr for o in ops], *[a for (a, _, _) in ct_flat])


def _sigmoid(x):
    return 1.0 / (1.0 + jnp.exp(-x))


def _silu(x):
    return x * _sigmoid(x)


def _gelu_tanh(x):
    return 0.5 * x * (1.0 + jnp.tanh(math.sqrt(2.0 / math.pi) * (x + 0.044715 * (x * x * x))))


def _ln(x, g, b):
    mu = jnp.mean(x, axis=-1, keepdims=True)
    xc = x - mu
    var = jnp.mean(xc * xc, axis=-1, keepdims=True)
    return xc * lax.rsqrt(var + LN_EPS) * g + b


def _rms(x, g):
    ms = jnp.mean(x * x, axis=-1, keepdims=True)
    return x * lax.rsqrt(ms + RMS_EPS) * g


def _swap_halves(x, half):
    width = x.shape[-1]
    lane = lax.broadcasted_iota(jnp.int32, x.shape, 1)
    return jnp.where(lane % (2 * half) < half, pltpu.roll(x, width - half, 1), pltpu.roll(x, half, 1))


def _make_swap(half):
    @jax.custom_vjp
    def swap(x):
        return _swap_halves(x, half)

    swap.defvjp(lambda x: (_swap_halves(x, half), None), lambda _, dy: (_swap_halves(dy, half),))
    return swap


_swap16, _swap8 = _make_swap(16), _make_swap(8)


def _rope(x, cos, sin_signed, swap):
    return x * cos + swap(x) * sin_signed


def _dot_f32(a, b):
    return jnp.dot(a, b, preferred_element_type=F32, precision=lax.Precision.HIGHEST)


def fn_ln(x, g, b):
    return (_ln(x, g, b),)


def _twice(fn):
    def f(*a):
        (y,) = fn(*a)
        return y, y
    return f


PROJ_W = 2048
P_A, P_Q, P_K, P_V, P_C, P_CQ, P_CKV, P_KR = 0, 512, 768, 896, 1024, 1536, 1792, 1920
CQ_PAD = 256
_CQ_END = P_CQ + MLA_Q_LORA


def _pad_w_in(w):
    z = lambda n: jnp.zeros((w.shape[0], n), w.dtype)
    return jnp.concatenate([w[:, :_CQ_END], z(P_CKV - _CQ_END), w[:, _CQ_END:], z(PROJ_W - P_KR - MLA_ROPE)], axis=1)


def _unpad_w_in(g):
    return jnp.concatenate([g[:, :_CQ_END], g[:, P_CKV:P_KR + MLA_ROPE]], axis=1)


def fn_pre(proj, tab_q, tab_d, seg, place, qng, kng, sg, sb, sw, sbt, mqn, wuq, mkvn, wukv):
    tm = proj.shape[0]
    aglu = proj[:, P_A:P_A + GROUP_W] * _sigmoid(proj[:, P_A + GROUP_W:P_Q])
    b_q, b_k, b_v = proj[:, P_Q:P_K], proj[:, P_K:P_V], proj[:, P_V:P_C]
    cos_q, sin_q = tab_q[:, :GROUP_W], tab_q[:, GROUP_W:]
    q = b_q * lax.rsqrt(_dot_f32(b_q * b_q, seg) + RMS_EPS) * qng
    q = _rope(q, cos_q, sin_q, _swap16)
    k = b_k * lax.rsqrt(_dot_f32(b_k * b_k, seg[:128, :128]) + RMS_EPS) * kng
    k = _rope(k, cos_q[:, :128], sin_q[:, :128], _swap16)
    c = _gelu_tanh(proj[:, P_C:P_CQ])
    u, sv = c[:, :GROUP_W], _ln(c[:, GROUP_W:], sg, sb)
    group = lax.broadcasted_iota(jnp.int32, (CHUNK, GROUP_W), 1) // HEAD_DIM
    rows = []
    for n in range(tm // CHUNK):
        svn = sv[CHUNK * n:CHUNK * (n + 1)]
        acc = jnp.zeros((CHUNK, GROUP_W), F32)
        for g in range(SGU_GROUPS):
            acc = acc + jnp.where(group == g, mm_nn(sw[CHUNK * g:CHUNK * (g + 1)], svn) + sbt[:, g:g + 1], 0.0)
        rows.append(acc)
    o_c = u * jnp.concatenate(rows, axis=0)
    d_cq, d_ckv, d_kr = proj[:, P_CQ:P_CKV], proj[:, P_CKV:P_KR], proj[:, P_KR:PROJ_W]
    cqn = d_cq * lax.rsqrt(jnp.sum(d_cq * d_cq, axis=-1, keepdims=True) * (1.0 / MLA_Q_LORA) + RMS_EPS) * mqn
    cos_d = jnp.concatenate([tab_d[:, :MLA_DK_PAD]] * MLA_HEADS, axis=-1)
    sin_d = jnp.concatenate([tab_d[:, MLA_DK_PAD:]] * MLA_HEADS, axis=-1)
    qf = _rope(mm_nn(cqn, wuq), cos_d, sin_d, _swap8)
    kvd = mm_nn(_rms(d_ckv, mkvn), wukv)
    kf = _rope(kvd[:, :MLA_HEADS * MLA_DK_PAD] + _dot_f32(d_kr, place), cos_d, sin_d, _swap8)
    return aglu, q, k, b_v, o_c, qf, kf, kvd[:, MLA_HEADS * MLA_DK_PAD:]


def fn_aconv(win, w, b, g, beta):
    tm = win.shape[0] - 2 * CONV_A_HALO
    off = CONV_A_HALO - CONV_A_WIDTH // 2
    acc = None
    for kk in range(CONV_A_WIDTH):
        term = win[off + kk:off + kk + tm] * w[kk:kk + 1, :]
        acc = term if acc is None else acc + term
    return (_silu(_ln(acc + b, g, beta)),)


def fn_resln(h, r, g, b):
    return (_ln(DEEPNORM_ALPHA * h + r, g, b),)


def _shift_down(x):
    return jnp.concatenate([jnp.zeros((8, x.shape[1]), F32), x], axis=0)[7:7 + x.shape[0]]


def _shift_up(x):
    return jnp.concatenate([x, jnp.zeros((8, x.shape[1]), F32)], axis=0)[1:1 + x.shape[0]]


def fn_ffnconv(u1, u2, w1, w2, b1, b2):
    c1 = _shift_down(u1) * w1[0:1] + u1 * w1[1:2] + _shift_up(u1) * w1[2:3] + b1
    c2 = _shift_down(u2) * w2[0:1] + u2 * w2[1:2] + _shift_up(u2) * w2[2:3] + b2
    return (_silu(c1) * c2,)


def fn_final(h, r, t, g, b):
    y = _ln(DEEPNORM_ALPHA * h + r, g, b)
    err = (y - t) * (y - t)
    return (0.5 * jnp.sum(jnp.mean(err, axis=-1, keepdims=True), axis=0, keepdims=True),)


def _rope_tables(seq):
    n_rows = seq // GRID_W

    def tab(half):
        inv = ROPE_THETA ** (-jnp.arange(half, dtype=F32) / half)
        ar = jnp.arange(n_rows, dtype=F32)[:, None] * inv[None, :]
        ac = jnp.arange(GRID_W, dtype=F32)[:, None] * inv[None, :]
        by_row = lambda a: jnp.repeat(a, GRID_W, axis=0)
        by_col = lambda a: jnp.tile(a, (n_rows, 1))
        cos_r, sin_r, cos_c, sin_c = by_row(jnp.cos(ar)), by_row(jnp.sin(ar)), by_col(jnp.cos(ac)), by_col(jnp.sin(ac))
        return (jnp.concatenate([cos_r, cos_r, cos_c, cos_c], axis=-1),
                jnp.concatenate([-sin_r, sin_r, -sin_c, sin_c], axis=-1))

    cos_b, sin_b = tab(HEAD_DIM // 4)
    tab_q = jnp.concatenate([cos_b] * GQA_HEADS + [sin_b] * GQA_HEADS, axis=-1)
    cos_r, sin_r = tab(MLA_ROPE // 4)
    ones = jnp.ones((seq, MLA_NOPE), F32)
    zpad = MLA_DK_PAD - MLA_NOPE - MLA_ROPE
    tab_d = jnp.concatenate([ones, cos_r, jnp.ones((seq, zpad), F32), 0.0 * ones, sin_r, jnp.zeros((seq, zpad), F32)],
                            axis=-1)
    lane = jnp.arange(GROUP_W)
    seg = jnp.where(lane[:, None] // HEAD_DIM == lane[None, :] // HEAD_DIM, 1.0 / HEAD_DIM, 0.0).astype(F32)
    src, dst = jnp.arange(128)[:, None], jnp.arange(MLA_HEADS * MLA_DK_PAD)[None, :]
    place = jnp.where((src < MLA_ROPE) & (dst % MLA_DK_PAD == MLA_NOPE + src), 1.0, 0.0).astype(F32)
    return tab_q, tab_d, seg, place


def _pre_ops(proj, tabs, kp, grad):
    tm = ROW_TILE
    ops = [_row_op(proj, tm, grad=grad, gdtype=MXU_DTYPE), _row_op(tabs[0], tm), _row_op(tabs[1], tm),
           _par_op(tabs[2]), _par_op(tabs[3])]
    ops += [_par_op(kp[n], grad=grad) for n in ('qng', 'kng', 'sg', 'sb', 'sw', 'sbt', 'mqn', 'wuq', 'mkvn', 'wukv')]
    return ops


PRE_OUTS = ((0, GROUP_W), (GQA_HEADS, HEAD_DIM), (GQA_KV_HEADS, HEAD_DIM), (GQA_KV_HEADS, HEAD_DIM), (0, GROUP_W),
            (MLA_HEADS, MLA_DK_PAD), (MLA_HEADS, MLA_DK_PAD), (MLA_HEADS, MLA_V))


def _pre_out_specs(seq, tm):
    specs = []
    for heads, w in PRE_OUTS:
        if heads:
            specs.append(((heads, seq, w), (heads, tm, w), lambda i: (0, i, 0)))
        else:
            specs.append(((seq, w), (tm, w), lambda i: (i, 0)))
    return specs


def pre_fwd(proj, tabs, kp, tag):
    seq = proj.shape[0]
    tm = ROW_TILE
    dts = (F32,) + (MXU_DTYPE,) * 7
    outs = [(shape, dt, block, imap) for (shape, block, imap), dt in zip(_pre_out_specs(seq, tm), dts)]
    return stage_fwd("pre_fwd" + tag, fn_pre, _pre_ops(proj, tabs, kp, False), outs, (seq // tm,))


def pre_bwd(proj, tabs, kp, cts, tag):
    seq = proj.shape[0]
    tm = ROW_TILE
    ct = [[c if isinstance(c, tuple) else (c, block, imap)] for c, (_, block, imap) in zip(cts, _pre_out_specs(seq, tm))]
    return stage_bwd("pre_bwd" + tag, fn_pre, _pre_ops(proj, tabs, kp, True), ct, (seq // tm,))


def _aconv_ops(kp, grad):
    return [_par_op(kp[n], grad=grad) for n in ('caw', 'cab', 'lag', 'lab')]


def aconv_fwd(aglu_pad, kp, tag):
    seq = aglu_pad.shape[0] - 2 * CONV_A_HALO
    tm = ROW_TILE
    n_par = 4

    def body(x_ref, *refs):
        i = pl.program_id(0)
        win = x_ref[pl.ds(pl.multiple_of(i * tm, tm), tm + 2 * CONV_A_HALO), :]
        (o,) = fn_aconv(win, *[_load(r) for r in refs[:n_par]])
        refs[n_par][...] = o.astype(refs[n_par].dtype)

    pars = _aconv_ops(kp, False)
    return pl.pallas_call(
        body, grid=(seq // tm,),
        in_specs=[pl.BlockSpec(aglu_pad.shape, lambda i: (0, 0))] + [pl.BlockSpec(o.block, o.imap) for o in pars],
        out_specs=pl.BlockSpec((tm, GROUP_W), lambda i: (i, 0)),
        out_shape=jax.ShapeDtypeStruct((seq, GROUP_W), MXU_DTYPE),
        compiler_params=pltpu.CompilerParams(dimension_semantics=("parallel",), vmem_limit_bytes=VMEM_LIMIT),
        name="aconv_fwd" + tag)(aglu_pad, *[o.arr for o in pars])


def aconv_bwd(aglu_pad, kp, d_oa, tag):
    seq = aglu_pad.shape[0] - 2 * CONV_A_HALO
    tm = ROW_TILE
    n_par = 4

    def body(x_ref, *refs):
        i = pl.program_id(0)
        rows = pl.ds(pl.multiple_of(i * tm, tm), tm + 2 * CONV_A_HALO)
        pars = [_load(r) for r in refs[:n_par]]
        ct = refs[n_par][...].astype(F32)
        outs = refs[n_par + 1:]
        _, vjp = jax.vjp(lambda *a: fn_aconv(*a), x_ref[rows, :], *pars)
        grads = vjp((ct,))

        @pl.when(i == 0)
        def _():
            for r in outs:
                r[...] = jnp.zeros_like(r)

        outs[0][rows, :] += grads[0]
        for r, g in zip(outs[1:], grads[1:]):
            r[...] += g

    pars = _aconv_ops(kp, True)
    whole = pl.BlockSpec(aglu_pad.shape, lambda i: (0, 0))
    par_specs = [pl.BlockSpec(o.block, o.imap) for o in pars]
    return pl.pallas_call(
        body, grid=(seq // tm,),
        in_specs=[whole] + par_specs + [pl.BlockSpec((tm, GROUP_W), lambda i: (i, 0))],
        out_specs=[whole] + par_specs,
        out_shape=[jax.ShapeDtypeStruct(aglu_pad.shape, F32)] + [jax.ShapeDtypeStruct(o.arr.shape, F32) for o in pars],
        compiler_params=pltpu.CompilerParams(dimension_semantics=("arbitrary",), vmem_limit_bytes=VMEM_LIMIT),
        name="aconv_bwd" + tag)(aglu_pad, *[o.arr for o in pars], d_oa)


ATTN_TQ = 256
ATTN_TK = 512


def attn_fwd(q3, k3, v3, scale, tag):
    heads, seq, dk = q3.shape
    group = heads // k3.shape[0]
    kv_per_pair = 2 // group
    dv = v3.shape[2]
    tq, tk = min(ATTN_TQ, seq), min(ATTN_TK, seq)
    n_chunks = seq // tk
    log2e = math.log2(math.e)

    def one_head(q, k_ref, v_ref):
        scores = lambda c: _rawdot(q, k_ref[pl.ds(c * tk, tk), :], 1, 1)
        m, l, acc = jnp.full((tq, 1), -jnp.inf, F32), jnp.zeros((tq, 1), F32), jnp.zeros((tq, dv), F32)
        s_next = scores(0)
        for c in range(n_chunks):
            s_cur, s_next = s_next, (scores(c + 1) if c + 1 < n_chunks else None)
            t = s_cur * (scale * log2e)
            m_new = jnp.maximum(m, jnp.max(t, axis=-1, keepdims=True))
            alpha = jnp.exp2(m - m_new)
            p = jnp.exp2(t - m_new)
            l = alpha * l + jnp.sum(p, axis=-1, keepdims=True)
            acc = alpha * acc + _rawdot(p, v_ref[pl.ds(c * tk, tk), :], 1, 0)
            m = m_new
        return acc * (1.0 / l), m * (1.0 / log2e) + jnp.log(l)

    def body(q_ref, k_ref, v_ref, o_ref, lse_ref):
        outs = []
        for h in range(2):
            o, lse = one_head(q_ref[h], k_ref.at[h // group], v_ref.at[h // group])
            lse_ref[h] = lse
            outs.append(o)
        o_ref[...] = jnp.concatenate(outs, axis=-1)

    return pl.pallas_call(
        body, grid=(heads // 2, seq // tq),
        in_specs=[pl.BlockSpec((2, tq, dk), lambda j, i: (j, i, 0)),
                  pl.BlockSpec((kv_per_pair, seq, dk), lambda j, i: (j, 0, 0)),
                  pl.BlockSpec((kv_per_pair, seq, dv), lambda j, i: (j, 0, 0))],
        out_specs=[pl.BlockSpec((tq, 2 * dv), lambda j, i: (i, j)),
                   pl.BlockSpec((2, tq, 1), lambda j, i: (j, i, 0))],
        out_shape=[jax.ShapeDtypeStruct((seq, heads * dv), F32), jax.ShapeDtypeStruct((heads, seq, 1), F32)],
        compiler_params=pltpu.CompilerParams(dimension_semantics=("parallel", "parallel"),
                                             vmem_limit_bytes=VMEM_LIMIT),
        name="attn_fwd" + tag)(q3, k3, v3)


def attn_bwd(q3, k3, v3, o, lse3, do_all, do_col, scale, tag):
    heads, seq, dk = q3.shape
    group = heads // k3.shape[0]
    kv_per_pair = 2 // group
    dv = v3.shape[2]
    tq, tk = min(ATTN_TQ, seq), min(ATTN_TK, seq)
    n_chunks = seq // tk
    log2e = math.log2(math.e)

    def one_head(q, do, o_h, lse, k_ref, v_ref, dk_ref, dv_ref):
        dob = do.astype(MXU_DTYPE)
        delta = jnp.sum(do * o_h, axis=-1, keepdims=True)
        lse2 = lse * log2e
        rows = lambda c: pl.ds(c * tk, tk)
        products = lambda c: (_rawdot(q, k_ref[rows(c), :], 1, 1), _rawdot(dob, v_ref[rows(c), :], 1, 1))
        dq = jnp.zeros((tq, dk), F32)
        nxt = products(0)
        for c in range(n_chunks):
            (s_cur, dp_cur), nxt = nxt, (products(c + 1) if c + 1 < n_chunks else None)
            p = jnp.exp2(s_cur * (scale * log2e) - lse2)
            ds = (p * ((dp_cur - delta) * scale)).astype(MXU_DTYPE)
            dv_ref[rows(c), :] += _rawdot(p, dob, 0, 0)
            dk_ref[rows(c), :] += _rawdot(ds, q, 0, 0)
            dq = dq + _rawdot(ds, k_ref[rows(c), :], 1, 0)
        return dq

    def body(q_ref, k_ref, v_ref, o_ref, lse_ref, do_ref, dq_ref, dk_ref, dv_ref):
        @pl.when(pl.program_id(1) == 0)
        def _():
            dk_ref[...] = jnp.zeros_like(dk_ref)
            dv_ref[...] = jnp.zeros_like(dv_ref)

        do_pair, o_pair = do_ref[...], o_ref[...]
        for h in range(2):
            kv = h // group
            dq_ref[h] = one_head(q_ref[h], do_pair[:, dv * h:dv * (h + 1)], o_pair[:, dv * h:dv * (h + 1)],
                                 lse_ref[h], k_ref.at[kv], v_ref.at[kv], dk_ref.at[kv], dv_ref.at[kv])

    qspec = lambda d: pl.BlockSpec((2, tq, d), lambda j, i: (j, i, 0))
    kvspec = lambda d: pl.BlockSpec((kv_per_pair, seq, d), lambda j, i: (j, 0, 0))
    return pl.pallas_call(
        body, grid=(heads // 2, seq // tq),
        in_specs=[qspec(dk), kvspec(dk), kvspec(dv), pl.BlockSpec((tq, 2 * dv), lambda j, i: (i, j)), qspec(1),
                  pl.BlockSpec((tq, 2 * dv), lambda j, i: (i, do_col + j))],
        out_specs=[qspec(dk), kvspec(dk), kvspec(dv)],
        out_shape=[jax.ShapeDtypeStruct(q3.shape, F32), jax.ShapeDtypeStruct(k3.shape, F32),
                   jax.ShapeDtypeStruct(v3.shape, F32)],
        compiler_params=pltpu.CompilerParams(dimension_semantics=("parallel", "arbitrary"),
                                             vmem_limit_bytes=VMEM_LIMIT),
        name="attn_bwd" + tag)(q3, k3, v3, o, lse3, do_all)


def resln_fwd(h, r, g, b, tag):
    seq, d = h.shape
    tm = ROW_TILE
    ops = [_row_op(h, tm), _row_op(r, tm), _par_op(g), _par_op(b)]
    outs = [((seq, d), dt, (tm, d), lambda i: (i, 0)) for dt in (F32, MXU_DTYPE)]
    return stage_fwd("resln_fwd" + tag, _twice(fn_resln), ops, outs, (seq // tm,))


def resln_bwd(h, r, g, b, dys, tag):
    seq, d = h.shape
    tm = ROW_TILE
    ops = [_row_op(h, tm, grad=True), _row_op(r, tm, grad=True, gdtype=MXU_DTYPE), _par_op(g, grad=True),
           _par_op(b, grad=True)]
    ct = [[(dy, (tm, d), lambda i: (i, 0)) for dy in dys]]
    return stage_bwd("resln_bwd" + tag, fn_resln, ops, ct, (seq // tm,))


def _ffnconv_ops(up1, up2, w, b, grad):
    seq = up1.shape[0]
    nblk = D_FF // 128
    lo, hi = (lambda j: (0, j)), (lambda j: (0, j + nblk))
    half = lambda a: dict(gshape=(a.shape[0], D_FF), gimap=lo)
    return [Op(up1, (seq, 128), lo, grad=grad, gdtype=MXU_DTYPE), Op(up2, (seq, 128), lo, grad=grad, gdtype=MXU_DTYPE),
            Op(w, (3, 128), lo, grad=grad, **half(w)), Op(w, (3, 128), hi, grad=grad, **half(w)),
            Op(b, (1, 128), lo, grad=grad, **half(b)), Op(b, (1, 128), hi, grad=grad, **half(b))]


def ffnconv_fwd(up1, up2, w, b, tag):
    seq = up1.shape[0]
    outs = [((seq, D_FF), MXU_DTYPE, (seq, 128), lambda j: (0, j))]
    return stage_fwd("ffnconv_fwd" + tag, fn_ffnconv, _ffnconv_ops(up1, up2, w, b, False), outs, (D_FF // 128,))[0]


def ffnconv_bwd(up1, up2, w, b, dact, tag):
    seq = up1.shape[0]
    ct = [[(dact, (seq, 128), lambda j: (0, j))]]
    du1, du2, dw1, dw2, db1, db2 = stage_bwd("ffnconv_bwd" + tag, fn_ffnconv, _ffnconv_ops(up1, up2, w, b, True),
                                             ct, (D_FF // 128,))
    cat = lambda a, b_: jnp.concatenate([a, b_], axis=-1)
    return du1, du2, cat(dw1, dw2), cat(db1, db2)


def final_bwd(h, r, t, g, b, tag):
    seq, d = h.shape
    tm = ROW_TILE
    ops = [_row_op(h, tm, grad=True), _row_op(r, tm, grad=True, gdtype=MXU_DTYPE), _row_op(t, tm),
           _par_op(g, grad=True), _par_op(b, grad=True)]
    return stage_bwd("final_bwd" + tag, fn_final, ops, [None], (seq // tm,), value_acc=True)


def _layer_params(wts, l):
    row = lambda a: a.reshape(1, -1)
    wuq = wts['mla_w_uq'][l].reshape(MLA_Q_LORA, MLA_HEADS, MLA_NOPE + MLA_ROPE)
    wuq = jnp.pad(wuq, ((0, CQ_PAD - MLA_Q_LORA), (0, 0), (0, MLA_DK_PAD - MLA_NOPE - MLA_ROPE)))
    wukv = wts['mla_w_ukv'][l].reshape(MLA_KV_LORA, MLA_HEADS, MLA_NOPE + MLA_V)
    wuk = jnp.pad(wukv[:, :, :MLA_NOPE], ((0, 0), (0, 0), (0, MLA_DK_PAD - MLA_NOPE)))
    return dict(
        qng=jnp.tile(row(wts['qk_norm_q'][l]), (1, GQA_HEADS)), kng=jnp.tile(row(wts['qk_norm_k'][l]), (1, GQA_KV_HEADS)),
        sg=row(wts['sgu_ln_g'][l]), sb=row(wts['sgu_ln_b'][l]),
        sw=wts['sgu_w'][l].reshape(SGU_GROUPS * CHUNK, CHUNK), sbt=wts['sgu_b'][l].T,
        mqn=jnp.pad(row(wts['mla_q_norm'][l]), ((0, 0), (0, CQ_PAD - MLA_Q_LORA))),
        wuq=wuq.reshape(CQ_PAD, MLA_HEADS * MLA_DK_PAD), mkvn=row(wts['mla_kv_norm'][l]),
        wukv=jnp.concatenate([wuk.reshape(MLA_KV_LORA, -1), wukv[:, :, MLA_NOPE:].reshape(MLA_KV_LORA, -1)], axis=1),
        caw=wts['conv_a_w'][l], cab=row(wts['conv_a_b'][l]), lag=row(wts['ln_a_g'][l]), lab=row(wts['ln_a_b'][l]),
        lmg=row(wts['ln_mix_g'][l]), lmb=row(wts['ln_mix_b'][l]),
        fcw=wts['ffn_conv_w'][l], fcb=row(wts['ffn_conv_b'][l]),
        lfg=row(wts['ln_ffn_g'][l]), lfb=row(wts['ln_ffn_b'][l]))


def _to_heads(a, heads):
    seq = a.shape[0]
    return a.reshape(seq, heads, -1).transpose(1, 0, 2)


def _from_heads(a3):
    return a3.transpose(1, 0, 2).reshape(a3.shape[1], -1)


def local_step(x, target, wts, mat, hook):
    seq = x.shape[0]
    tm = ROW_TILE
    tabs = _rope_tables(seq)
    scale_b = HEAD_DIM ** -0.5
    scale_d = (MLA_NOPE + MLA_ROPE) ** -0.5
    ln_in_g, ln_in_b = wts['ln_in_g'].reshape(1, -1), wts['ln_in_b'].reshape(1, -1)

    h, h_m = stage_fwd("ln_in_fwd", _twice(fn_ln), [_row_op(x, tm), _par_op(ln_in_g), _par_op(ln_in_b)],
                       [((seq, D_MODEL), dt, (tm, D_MODEL), lambda i: (i, 0)) for dt in (F32, MXU_DTYPE)],
                       (seq // tm,))
    saved = []
    for l in range(DEPTH):
        tag = f"_l{l}"
        kp, unprep = jax.vjp(lambda w: _layer_params(w, l), wts)
        m = {'w_in': mat(l, 'w_in', h_m)}
        proj = matmul(h_m, m['w_in'], 'nn', F32, "mm_proj" + tag)
        aglu, q3, k3, v3, o_c, qd3, kd3, vd3 = pre_fwd(proj, tabs, kp, tag)
        aglu_pad = jnp.pad(aglu, ((CONV_A_HALO, CONV_A_HALO), (0, 0)))
        o_a = aconv_fwd(aglu_pad, kp, tag)
        o_b3, lse_b3 = attn_fwd(q3, k3, v3, scale_b, "_b" + tag)
        o_d3, lse_d3 = attn_fwd(qd3, kd3, vd3, scale_d, "_d" + tag)
        o_cat = jnp.concatenate([o_a, o_b3.astype(MXU_DTYPE), o_c, o_d3.astype(MXU_DTYPE)], axis=-1)
        m['w_out'] = mat(l, 'w_out', o_cat)
        mix = matmul(o_cat, m['w_out'], 'nn', F32, "mm_mix" + tag)
        h1, h1_m = resln_fwd(h, mix, kp['lmg'], kp['lmb'], "_mix" + tag)
        w_up = mat(l, 'ffn_w_up', h1_m)
        m['w_up1'], m['w_up2'] = w_up[:, :D_FF], w_up[:, D_FF:]
        up1 = matmul(h1_m, m['w_up1'], 'nn', F32, "mm_up1" + tag)
        up2 = matmul(h1_m, m['w_up2'], 'nn', F32, "mm_up2" + tag)
        act = ffnconv_fwd(up1, up2, kp['fcw'], kp['fcb'], tag)
        m['ffn_w_down'] = mat(l, 'ffn_w_down', act)
        f = matmul(act, m['ffn_w_down'], 'nn', F32, "mm_down" + tag)
        saved.append(dict(kp=kp, unprep=unprep, m=m, h=h, h_m=h_m, h1_m=h1_m, proj=proj, o_b3=o_b3, lse_b3=lse_b3,
                          o_d3=o_d3, lse_d3=lse_d3, aglu_pad=aglu_pad, q3=q3, k3=k3, v3=v3, qd3=qd3,
                          kd3=kd3, vd3=vd3, o_cat=o_cat, mix=mix, h1=h1, up1=up1, up2=up2, act=act, f=f))
        if l + 1 < DEPTH:
            h, h_m = resln_fwd(h1, f, kp['lfg'], kp['lfb'], "_ffn" + tag)

    after = lambda a, tok: a if tok is None else a + tok
    small_acc = None
    dh_parts = None
    loss = None
    tok = None
    g_mix = None
    for l in reversed(range(DEPTH)):
        tag = f"_l{l}"
        s = saved[l]
        kp, m = s['kp'], s['m']
        dkp = {}
        lfg = after(kp['lfg'], tok)
        if l == DEPTH - 1:
            dh1_a, df, dkp['lfg'], dkp['lfb'], loss = final_bwd(s['h1'], s['f'], target, lfg, kp['lfb'], tag)
        else:
            dh1_a, df, dkp['lfg'], dkp['lfb'] = resln_bwd(s['h1'], s['f'], lfg, kp['lfb'], dh_parts, "_ffn" + tag)
        dact = matmul(df, m['ffn_w_down'], 'nt', F32, "mm_dact" + tag)
        g_down = matmul(s['act'], df, 'tn', F32, "mm_gdown" + tag)
        dup1, dup2, dkp['fcw'], dkp['fcb'] = ffnconv_bwd(s['up1'], s['up2'], kp['fcw'], kp['fcb'], dact, tag)
        dh1_b1 = matmul(dup1, m['w_up1'], 'nt', F32, "mm_dh1a" + tag)
        dh1_b2 = matmul(dup2, m['w_up2'], 'nt', F32, "mm_dh1b" + tag)
        g_up = jnp.concatenate([matmul(s['h1_m'], dup1, 'tn', F32, "mm_gup1" + tag),
                                matmul(s['h1_m'], dup2, 'tn', F32, "mm_gup2" + tag)], axis=1)
        tok = hook(f"ffn{l}", {('ffn_w_down', l): g_down, ('ffn_w_up', l): g_up})
        dh_a, dmix, dkp['lmg'], dkp['lmb'] = resln_bwd(s['h'], s['mix'], after(kp['lmg'], tok), kp['lmb'],
                                                       [dh1_a, dh1_b1, dh1_b2], "_mix" + tag)
        do_cat = matmul(dmix, m['w_out'], 'nt', F32, "mm_docat" + tag)
        g_out = matmul(s['o_cat'], dmix, 'tn', F32, "mm_gout" + tag)
        lse_b3 = s['lse_b3']
        if l == 0:
            lse_b3 = after(lse_b3, hook("out0", {('w_out', l): g_out}))
        do_c = (do_cat, (ROW_TILE, GROUP_W), lambda i: (i, 2))
        pair_w = 2 * HEAD_DIM
        dq3, dk3, dv3 = attn_bwd(s['q3'], s['k3'], s['v3'], s['o_b3'], lse_b3, do_cat, GROUP_W // pair_w,
                                 scale_b, "_b" + tag)
        dqd3, dkd3, dvd3 = attn_bwd(s['qd3'], s['kd3'], s['vd3'], s['o_d3'], s['lse_d3'], do_cat,
                                    3 * GROUP_W // pair_w, scale_d, "_d" + tag)
        daglu_pad, dkp['caw'], dkp['cab'], dkp['lag'], dkp['lab'] = aconv_bwd(s['aglu_pad'], kp, do_cat, tag)
        cts = [daglu_pad[CONV_A_HALO:CONV_A_HALO + seq], dq3, dk3, dv3, do_c, dqd3, dkd3, dvd3]
        pre_g = pre_bwd(s['proj'], tabs, kp, cts, tag)
        dproj = pre_g[0]
        for n, g in zip(('qng', 'kng', 'sg', 'sb', 'sw', 'sbt', 'mqn', 'wuq', 'mkvn', 'wukv'), pre_g[1:]):
            dkp[n] = g
        dh_b = matmul(dproj, m['w_in'], 'nt', F32, "mm_dh" + tag)
        g_in = matmul(s['h_m'], dproj, 'tn', F32, "mm_gin" + tag)
        dh_parts = [dh_a, dh_b]
        (dw,) = s['unprep'](dkp)
        small_acc = dw if small_acc is None else jax.tree.map(jnp.add, small_acc, dw)
        g_mix = {('w_out', l): g_out, ('w_in', l): _unpad_w_in(g_in)}
        if l > 0:
            tok = hook(f"mix{l}", g_mix)
        else:
            g_mix.pop(('w_out', l))

    g_mix.update({(n, None): small_acc[n] for n in SHARDED if n not in MATMUL_WEIGHTS})
    tok = hook("last", g_mix)
    dx, dg, db = stage_bwd("ln_in_bwd", fn_ln,
                           [_row_op(x, tm, grad=True), _par_op(after(ln_in_g, tok), grad=True),
                            _par_op(ln_in_b, grad=True)],
                           [[(p, (tm, D_MODEL), lambda i: (i, 0)) for p in dh_parts]], (seq // tm,))
    out = {n: small_acc[n] for n in REPLICATED}
    out['ln_in_g'], out['ln_in_b'] = dg.reshape(-1), db.reshape(-1)
    return loss, dx, out


def _peer(x, y, c, r):
    return ((1 - x) if r & 4 else x, (1 - y) if r & 2 else y, (1 - c) if r & 1 else c)


def _exchange_copy(src_ref, land_ref, send_sems, recv_sems, k, gather, x, y, c, r):
    px, py, pc = _peer(x, y, c, r)
    me, peer = 4 * x + 2 * y + c, 4 * px + 2 * py + pc
    src = src_ref if gather else src_ref.at[peer]
    mk = lambda dst: pltpu.make_async_remote_copy(
        src_ref=src, dst_ref=dst, send_sem=send_sems.at[k * (N_DEV - 1) + r - 1],
        recv_sem=recv_sems.at[k * (N_DEV - 1) + r - 1],
        device_id=(px, py, pc), device_id_type=pl.DeviceIdType.MESH)
    return mk(land_ref.at[me]), mk(land_ref.at[peer])


_HBM_SPEC = pl.BlockSpec(memory_space=pltpu.HBM)
_SEM_SPEC = pl.BlockSpec(memory_space=pltpu.SEMAPHORE)


def exchange_start(srcs, gather, groups, name):
    n_t = len(srcs)
    lands =[lax.empty(((N_DEV,) + s.shape) if gt else s.shape, s.dtype) for s, gt in zip(srcs, gather)]

    def body(*refs):
        src_refs, land_refs = refs[:n_t], refs[n_t:2 * n_t]
        sem_refs = refs[2 * n_t:2 * n_t + 2 * len(groups)]
        token = refs[-1]
        x, y, c = lax.axis_index("x"), lax.axis_index("y"), lax.axis_index("c")
        for gi, g in enumerate(groups):
            for k, t in enumerate(g):
                for r in range(1, N_DEV):
                    _exchange_copy(src_refs[t], land_refs[t], sem_refs[2 * gi], sem_refs[2 * gi + 1], k, gather[t],
                                   x, y, c, r)[0].start()
        token[...] = jnp.zeros_like(token)

    sem_shapes = []
    for g in groups:
        sem_shapes += [pltpu.SemaphoreType.DMA((len(g) * (N_DEV - 1),))] * 2
    hbm_shapes = [pltpu.HBM(a.shape, a.dtype) for a in list(srcs) + lands]
    n_sem = len(sem_shapes)
    res = pl.pallas_call(
        body, name=name,
        out_shape=tuple(sem_shapes + hbm_shapes + [jax.ShapeDtypeStruct((8, 128), F32)]),
        in_specs=[_HBM_SPEC] * (2 * n_t),
        out_specs=tuple([_SEM_SPEC] * n_sem + [_HBM_SPEC] * (2 * n_t) + [pl.BlockSpec(memory_space=pltpu.VMEM)]),
        input_output_aliases={i: n_sem + i for i in range(2 * n_t)},
        compiler_params=pltpu.CompilerParams(has_side_effects=pltpu.SideEffectType.DATAFLOW_SIDE_EFFECTING),
    )(*[pltpu.with_memory_space_constraint(a, pltpu.HBM) for a in list(srcs) + lands])
    sems = [(res[2 * gi], res[2 * gi + 1]) for gi in range(len(groups))]
    return sems, list(res[n_sem:n_sem + n_t]), list(res[n_sem + n_t:n_sem + 2 * n_t]), res[-1]


def exchange_wait(sems, srcs, lands, gather, after, name):
    n_t = len(srcs)

    def body(*refs):
        src_refs, land_refs = refs[:n_t], refs[n_t:2 * n_t]
        send_sems, recv_sems = refs[2 * n_t], refs[2 * n_t + 1]
        x, y, c = lax.axis_index("x"), lax.axis_index("y"), lax.axis_index("c")
        for k in range(n_t):
            for r in range(1, N_DEV):
                send, recv = _exchange_copy(src_refs[k], land_refs[k], send_sems, recv_sems, k, gather[k], x, y, c, r)
                send.wait_send()
                recv.wait_recv()

    res = pl.pallas_call(
        body, name=name,
        out_shape=tuple(pltpu.HBM(a.shape, a.dtype) for a in list(srcs) + list(lands)),
        in_specs=[_HBM_SPEC] * (2 * n_t) + [_SEM_SPEC, _SEM_SPEC, pl.BlockSpec(memory_space=pl.ANY)],
        out_specs=tuple([_HBM_SPEC] * (2 * n_t)),
        input_output_aliases={i: i for i in range(2 * n_t)},
        compiler_params=pltpu.CompilerParams(has_side_effects=pltpu.SideEffectType.DATAFLOW_SIDE_EFFECTING),
    )(*srcs, *lands, sems[0], sems[1], after)
    return list(res[:n_t]), list(res[n_t:])


def adamw(parts, w, m, v, name):
    n_l, n_r, n_c = w.shape
    tr = n_r
    if n_r % 8 == 0:
        for cand in (512, 256, 128, 64, 32, 16, 8):
            if n_r % cand == 0 and cand * n_c * 4 <= 512 * 1024:
                tr = cand
                break
    c1 = 1.0 - ADAM_B1 ** ADAM_STEP
    c2 = 1.0 - ADAM_B2 ** ADAM_STEP
    per_layer = isinstance(parts, (list, tuple))
    n_p = n_l if per_layer else 1
    n_rb = n_r // tr

    def update(g, w_ref, m_ref, v_ref, g_ref, d_ref, nm_ref, nv_ref):
        w_, m_, v_ = w_ref[0], m_ref[0], v_ref[0]
        nm = ADAM_B1 * m_ + (1.0 - ADAM_B1) * g
        nv = ADAM_B2 * v_ + (1.0 - ADAM_B2) * (g * g)
        g_ref[0] = g
        nm_ref[0] = nm
        nv_ref[0] = nv
        d_ref[0] = -ADAM_LR * ((nm / c1) / (jnp.sqrt(nv / c2) + ADAM_EPS) + ADAM_WD * w_)

    def body(*refs):
        p_refs, rest = refs[:n_p], refs[n_p:]
        if not per_layer:
            g = p_refs[0][0, 0].astype(F32)
            for s in range(1, N_DEV):
                g = g + p_refs[0][s, 0].astype(F32)
            update(g, *rest)
        else:
            for lay in range(n_l):
                @pl.when(pl.program_id(0) == lay)
                def _(lay=lay):
                    g = p_refs[lay][0].astype(F32)
                    for s in range(1, N_DEV):
                        g = g + p_refs[lay][s].astype(F32)
                    update(g, *rest)

    blk = pl.BlockSpec((1, tr, n_c), lambda l, r: (l, r, 0))
    if per_layer:
        def p_spec(lay):
            park = 0 if lay > 0 else n_rb - 1
            return pl.BlockSpec((N_DEV, tr, n_c), lambda l, r: (0, jnp.where(l == lay, r, park), 0))
        p_specs, p_args = [p_spec(lay) for lay in range(n_l)], list(parts)
    else:
        p_specs, p_args = [pl.BlockSpec((N_DEV, 1, tr, n_c), lambda l, r: (0, l, r, 0))], [parts]
    return pl.pallas_call(
        body, grid=(n_l, n_rb), in_specs=p_specs + [blk, blk, blk],
        out_specs=[blk] * 4, out_shape=[jax.ShapeDtypeStruct(w.shape, F32)] * 4,
        compiler_params=pltpu.CompilerParams(dimension_semantics=("arbitrary", "arbitrary"),
                                             vmem_limit_bytes=VMEM_LIMIT),
        name=name)(*p_args, w, m, v)


def adamw_replicated(lands, own, ws, ms, vs, loss_land, loss_own):
    n_t = len(lands)
    c1 = 1.0 - ADAM_B1 ** ADAM_STEP
    c2 = 1.0 - ADAM_B2 ** ADAM_STEP

    def body(*refs):
        ins, outs = refs[:5 * n_t + 2], refs[5 * n_t + 2:]
        me = 4 * lax.axis_index("x") + 2 * lax.axis_index("y") + lax.axis_index("c")

        def total(land_ref, own_ref):
            g = None
            for s in range(N_DEV):
                term = jnp.where(me == s, own_ref[...], land_ref[s])
                g = term if g is None else g + term
            return g

        for t in range(n_t):
            land_ref, own_ref, w_ref, m_ref, v_ref = ins[5 * t:5 * t + 5]
            g = total(land_ref, own_ref)
            nm = ADAM_B1 * m_ref[...] + (1.0 - ADAM_B1) * g
            nv = ADAM_B2 * v_ref[...] + (1.0 - ADAM_B2) * (g * g)
            g_ref, d_ref, nm_ref, nv_ref = outs[4 * t:4 * t + 4]
            g_ref[...] = g
            nm_ref[...] = nm
            nv_ref[...] = nv
            d_ref[...] = -ADAM_LR * ((nm / c1) / (jnp.sqrt(nv / c2) + ADAM_EPS) + ADAM_WD * w_ref[...])
        outs[4 * n_t][...] = total(ins[5 * n_t], ins[5 * n_t + 1])

    args = []
    for t in range(n_t):
        args += [lands[t], own[t], ws[t], ms[t], vs[t]]
    out_shape = []
    for t in range(n_t):
        out_shape += [jax.ShapeDtypeStruct(ws[t].shape, F32)] * 4
    out_shape.append(jax.ShapeDtypeStruct(loss_own.shape, F32))
    res = pl.pallas_call(body, out_shape=out_shape,
                         compiler_params=pltpu.CompilerParams(vmem_limit_bytes=VMEM_LIMIT),
                         name="adamw_replicated")(*args, loss_land, loss_own)
    return [tuple(res[4 * t:4 * t + 4]) for t in range(n_t)], res[-1]


def _shard_slots(g, axis):
    if axis == 1:
        return g.reshape(g.shape[0], N_DEV, g.shape[1] // N_DEV, g.shape[2]).transpose(1, 0, 2, 3)
    return g.reshape(g.shape[0], g.shape[1], N_DEV, g.shape[2] // N_DEV).transpose(2, 0, 1, 3)


def _unshard(slots, axis):
    if axis == 1:
        return slots.transpose(1, 0, 2, 3).reshape(slots.shape[1], -1, slots.shape[3])
    return slots.transpose(1, 2, 0, 3).reshape(slots.shape[1], slots.shape[2], -1)


def kernel(x, ln_in_g, ln_in_b, w_in, conv_a_w, conv_a_b, ln_a_g, ln_a_b, qk_norm_q, qk_norm_k, sgu_ln_g, sgu_ln_b, sgu_w, sgu_b, mla_q_norm, mla_w_uq, mla_kv_norm, mla_w_ukv, w_out, ln_mix_g, ln_mix_b, ffn_w_up, ffn_conv_w, ffn_conv_b, ffn_w_down, ln_ffn_g, ln_ffn_b, loss_target, m_ln_in_g, m_ln_in_b, m_w_in, m_conv_a_w, m_conv_a_b, m_ln_a_g, m_ln_a_b, m_qk_norm_q, m_qk_norm_k, m_sgu_ln_g, m_sgu_ln_b, m_sgu_w, m_sgu_b, m_mla_q_norm, m_mla_w_uq, m_mla_kv_norm, m_mla_w_ukv, m_w_out, m_ln_mix_g, m_ln_mix_b, m_ffn_w_up, m_ffn_conv_w, m_ffn_conv_b, m_ffn_w_down, m_ln_ffn_g, m_ln_ffn_b, v_ln_in_g, v_ln_in_b, v_w_in, v_conv_a_w, v_conv_a_b, v_ln_a_g, v_ln_a_b, v_qk_norm_q, v_qk_norm_k, v_sgu_ln_g, v_sgu_ln_b, v_sgu_w, v_sgu_b, v_mla_q_norm, v_mla_w_uq, v_mla_kv_norm, v_mla_w_ukv, v_w_out, v_ln_mix_g, v_ln_mix_b, v_ffn_w_up, v_ffn_conv_w, v_ffn_conv_b, v_ffn_w_down, v_ln_ffn_g, v_ln_ffn_b):
    local = dict(ln_in_g=ln_in_g, ln_in_b=ln_in_b, w_in=w_in, conv_a_w=conv_a_w, conv_a_b=conv_a_b, ln_a_g=ln_a_g, ln_a_b=ln_a_b, qk_norm_q=qk_norm_q, qk_norm_k=qk_norm_k, sgu_ln_g=sgu_ln_g, sgu_ln_b=sgu_ln_b, sgu_w=sgu_w, sgu_b=sgu_b, mla_q_norm=mla_q_norm, mla_w_uq=mla_w_uq, mla_kv_norm=mla_kv_norm, mla_w_ukv=mla_w_ukv, w_out=w_out, ln_mix_g=ln_mix_g, ln_mix_b=ln_mix_b, ffn_w_up=ffn_w_up, ffn_conv_w=ffn_conv_w, ffn_conv_b=ffn_conv_b, ffn_w_down=ffn_w_down, ln_ffn_g=ln_ffn_g, ln_ffn_b=ln_ffn_b)
    mom = dict(ln_in_g=m_ln_in_g, ln_in_b=m_ln_in_b, w_in=m_w_in, conv_a_w=m_conv_a_w, conv_a_b=m_conv_a_b, ln_a_g=m_ln_a_g, ln_a_b=m_ln_a_b, qk_norm_q=m_qk_norm_q, qk_norm_k=m_qk_norm_k, sgu_ln_g=m_sgu_ln_g, sgu_ln_b=m_sgu_ln_b, sgu_w=m_sgu_w, sgu_b=m_sgu_b, mla_q_norm=m_mla_q_norm, mla_w_uq=m_mla_w_uq, mla_kv_norm=m_mla_kv_norm, mla_w_ukv=m_mla_w_ukv, w_out=m_w_out, ln_mix_g=m_ln_mix_g, ln_mix_b=m_ln_mix_b, ffn_w_up=m_ffn_w_up, ffn_conv_w=m_ffn_conv_w, ffn_conv_b=m_ffn_conv_b, ffn_w_down=m_ffn_w_down, ln_ffn_g=m_ln_ffn_g, ln_ffn_b=m_ln_ffn_b)
    var = dict(ln_in_g=v_ln_in_g, ln_in_b=v_ln_in_b, w_in=v_w_in, conv_a_w=v_conv_a_w, conv_a_b=v_conv_a_b, ln_a_g=v_ln_a_g, ln_a_b=v_ln_a_b, qk_norm_q=v_qk_norm_q, qk_norm_k=v_qk_norm_k, sgu_ln_g=v_sgu_ln_g, sgu_ln_b=v_sgu_ln_b, sgu_w=v_sgu_w, sgu_b=v_sgu_b, mla_q_norm=v_mla_q_norm, mla_w_uq=v_mla_w_uq, mla_kv_norm=v_mla_kv_norm, mla_w_ukv=v_mla_w_ukv, w_out=v_w_out, ln_mix_g=v_ln_mix_g, ln_mix_b=v_ln_mix_b, ffn_w_up=v_ffn_w_up, ffn_conv_w=v_ffn_conv_w, ffn_conv_b=v_ffn_conv_b, ffn_w_down=v_ffn_w_down, ln_ffn_g=v_ln_ffn_g, ln_ffn_b=v_ln_ffn_b)

    me = 4 * lax.axis_index("x") + 2 * lax.axis_index("y") + lax.axis_index("c")

    def own_slot(slots, block):
        return lax.dynamic_update_slice(slots, block[None], (me,) + (0,) * block.ndim)

    small_sharded = [n for n in SHARDED if n not in MATMUL_WEIGHTS]
    big_order = [(n, l) for l in range(DEPTH) for n in MATMUL_WEIGHTS]
    srcs = [local['w_in'][0].astype(COMM_DTYPE)] + [local[n] for n in small_sharded]
    srcs += [local[n][l].astype(COMM_DTYPE) for (n, l) in big_order[1:]]
    n_first = 1 + len(small_sharded)
    groups = [list(range(n_first))] + [[n_first + j] for j in range(len(big_order) - 1)]
    g_sems, g_srcs, g_lands, tok0 = exchange_start(srcs, [True] * len(srcs), groups, "gather_start")
    tok0 = tok0[0, 0]
    pending = {key: gi for gi, key in enumerate(big_order)}

    def finish(gi, after):
        idx = groups[gi]
        _, lands = exchange_wait(g_sems[gi], [g_srcs[t] for t in idx], [g_lands[t] for t in idx], [True] * len(idx),
                                 after, f"gather_wait{gi}")
        return [own_slot(ld, srcs[t]) for ld, t in zip(lands, idx)]

    first = finish(0, local['ln_in_g'] + tok0)
    wts = {n: local[n] for n in REPLICATED}
    wts['ln_in_g'] = local['ln_in_g'] + tok0
    for n, slots in zip(small_sharded, first[1:]):
        wts[n] = _unshard(slots, SHARDED[n])

    def unshard_layer(slots, n):
        if SHARDED[n] == 1:
            return slots.reshape(-1, slots.shape[2])
        return slots.transpose(1, 0, 2).reshape(slots.shape[1], -1)

    def mat(l, n, after):
        gi = pending[(n, l)]
        slots = first[0] if gi == 0 else finish(gi, after)[0]
        w = unshard_layer(slots, n).astype(MXU_DTYPE)
        return _pad_w_in(w) if n == 'w_in' else w

    started = []

    def hook(key, grads):
        tensors = []
        for (n, l), g in grads.items():
            if l is None:
                tensors.append(((n, l), _shard_slots(g, SHARDED[n])))
            else:
                tensors.append(((n, l), _shard_slots(g[None], SHARDED[n])[:, 0].astype(COMM_DTYPE)))
        sems, s_srcs, s_lands, tok = exchange_start([a for _, a in tensors], [False] * len(tensors),
                                                    [list(range(len(tensors)))], "scatter_start_" + key)
        started.append((key, [k for k, _ in tensors], sems[0], s_srcs, s_lands))
        return tok[0, 0]

    loss, dx, grads = local_step(x[0], loss_target[0], wts, mat, hook)

    as2d = lambda a: a.reshape(-1, a.shape[-1]) if a.ndim > 1 else a.reshape(1, -1)
    small_g = [as2d(grads[n]) for n in REPLICATED] + [jnp.broadcast_to(loss, (8, 128))]
    p_sems, p_srcs, p_lands, p_tok = exchange_start(small_g, [True] * len(small_g), [list(range(len(small_g)))],
                                                    "gather_small_start")

    parts, res = {}, {}

    def finish_scatter(entries, after):
        for key, keys, sems, s_srcs, s_lands in entries:
            s_out, lands = exchange_wait(sems, s_srcs, s_lands, [False] * len(keys), after, "scatter_wait_" + key)
            for k, so, ld in zip(keys, s_out, lands):
                parts[k] = own_slot(ld, lax.dynamic_index_in_dim(so, me, 0, keepdims=False))

    def update(names_):
        for n in names_:
            p = [parts[(n, l)] for l in range(DEPTH)] if n in MATMUL_WEIGHTS else parts[(n, None)]
            res[n] = adamw(p, local[n], mom[n], var[n], "adamw_" + n)

    early = ('ffn_w_up', 'ffn_w_down', 'w_out')
    finish_scatter([e for e in started if e[0] != "last"], p_tok)
    update(early)
    finish_scatter([e for e in started if e[0] == "last"], res[early[-1]][1])
    update([n for n in SHARDED if n not in early])
    updated = jnp.zeros((8, 128), F32) + sum(res[n][1][0, 0, 0] for n in SHARDED)
    p_own, p_lands = exchange_wait(p_sems[0], p_srcs, p_lands, [True] * len(small_g), updated, "gather_small_wait")
    small, loss_sum = adamw_replicated(p_lands[:-1], p_own[:-1], [as2d(local[n]) for n in REPLICATED],
                                       [as2d(mom[n]) for n in REPLICATED], [as2d(var[n]) for n in REPLICATED],
                                       p_lands[-1], p_own[-1])
    for n, quad in zip(REPLICATED, small):
        res[n] = tuple(a.reshape(local[n].shape) for a in quad)
    loss_total = loss_sum[0, 0]

    return (loss_total, dx[None], *[res[n][0] for n in WEIGHTS], *[res[n][1] for n in WEIGHTS],
            *[res[n][2] for n in WEIGHTS], *[res[n][3] for n in WEIGHTS])
```

```python
import functools
import math

import jax
import jax.numpy as jnp
from jax import lax
from jax.experimental import pallas as pl
from jax.experimental.pallas import tpu as pltpu

F32 = jnp.float32
MXU_DTYPE = jnp.bfloat16
COMM_DTYPE = jnp.bfloat16

N_DEV = 8
D_MODEL = 1024
DEPTH = 2
GRID_W = 64
GROUP_W = 256
HEAD_DIM = 64
CONV_A_WIDTH = 31
CONV_A_HALO = 16
GQA_HEADS = 4
GQA_KV_HEADS = 2
CHUNK = 128
SGU_GROUPS = 4
MLA_HEADS = 4
MLA_Q_LORA = 192
MLA_KV_LORA = 128
MLA_NOPE = 64
MLA_ROPE = 32
MLA_V = 64
MLA_DK_PAD = 128
ROPE_THETA = 10000.0
D_FF = 2816
DEEPNORM_ALPHA = (2 * DEPTH) ** 0.25
LN_EPS = 1e-5
RMS_EPS = 1e-6
D_IN_PROJ = 1888

ADAM_LR = 0.001
ADAM_B1 = 0.9
ADAM_B2 = 0.999
ADAM_EPS = 1e-08
ADAM_WD = 0.01
ADAM_STEP = 10

WEIGHTS = ['ln_in_g', 'ln_in_b', 'w_in', 'conv_a_w', 'conv_a_b', 'ln_a_g', 'ln_a_b', 'qk_norm_q', 'qk_norm_k',
           'sgu_ln_g', 'sgu_ln_b', 'sgu_w', 'sgu_b', 'mla_q_norm', 'mla_w_uq', 'mla_kv_norm', 'mla_w_ukv', 'w_out',
           'ln_mix_g', 'ln_mix_b', 'ffn_w_up', 'ffn_conv_w', 'ffn_conv_b', 'ffn_w_down', 'ln_ffn_g', 'ln_ffn_b']
SHARDED = {'w_in': 2, 'conv_a_w': 2, 'mla_w_uq': 2, 'mla_w_ukv': 2, 'w_out': 1, 'ffn_w_up': 2, 'ffn_conv_w': 2,
           'ffn_w_down': 1}
MATMUL_WEIGHTS = ('w_in', 'w_out', 'ffn_w_up', 'ffn_w_down')
REPLICATED = [n for n in WEIGHTS if n not in SHARDED]

ROW_TILE = 256
VMEM_LIMIT = 56 * 1024 * 1024


def _rawdot(a, b, ca, cb):
    return lax.dot_general(a.astype(MXU_DTYPE), b.astype(MXU_DTYPE), (((ca,), (cb,)), ((), ())),
                           preferred_element_type=F32)


@jax.custom_vjp
def mm_nn(a, b):
    return _rawdot(a, b, 1, 0)


def _mm_nn_fwd(a, b):
    return _rawdot(a, b, 1, 0), (a, b)


def _mm_nn_bwd(res, dy):
    a, b = res
    return _rawdot(dy, b, 1, 1), _rawdot(a, dy, 0, 0)


mm_nn.defvjp(_mm_nn_fwd, _mm_nn_bwd)


@jax.custom_vjp
def mm_nt(a, b):
    return _rawdot(a, b, 1, 1)


def _mm_nt_fwd(a, b):
    return _rawdot(a, b, 1, 1), (a, b)


def _mm_nt_bwd(res, dy):
    a, b = res
    return _rawdot(dy, b, 1, 0), _rawdot(dy, a, 0, 0)


mm_nt.defvjp(_mm_nt_fwd, _mm_nt_bwd)


def _pick_tile(d, cands):
    for c in cands:
        if d % c == 0:
            return c
    return d


def matmul(a, b, mode, out_dtype, name):
    if mode == 'nn':
        (m, k), (k2, n) = a.shape, b.shape
    elif mode == 'nt':
        (m, k), (n, k2) = a.shape, b.shape
    else:
        (k, m), (k2, n) = a.shape, b.shape
    assert k == k2, (a.shape, b.shape, mode)
    tm = _pick_tile(m, (1024, 1408, 512, 256, 128))
    tn = _pick_tile(n, (512, 1408, 256, 128))
    tk = _pick_tile(k, (2816, 2048, 1024, 512, 256, 128))
    nk = k // tk
    ca = 0 if mode == 'tn' else 1
    cb = 1 if mode == 'nt' else 0
    a_spec = pl.BlockSpec((tk, tm), lambda i, j, kk: (kk, i)) if mode == 'tn' else pl.BlockSpec((tm, tk), lambda i, j, kk: (i, kk))
    b_spec = pl.BlockSpec((tn, tk), lambda i, j, kk: (j, kk)) if mode == 'nt' else pl.BlockSpec((tk, tn), lambda i, j, kk: (kk, j))

    def body(a_ref, b_ref, o_ref, acc_ref):
        kk = pl.program_id(2)

        @pl.when(kk == 0)
        def _():
            acc_ref[...] = jnp.zeros_like(acc_ref)

        acc_ref[...] += _rawdot(a_ref[...], b_ref[...], ca, cb)

        @pl.when(kk == nk - 1)
        def _():
            o_ref[...] = acc_ref[...].astype(o_ref.dtype)

    return pl.pallas_call(
        body, grid=(m // tm, n // tn, nk), in_specs=[a_spec, b_spec],
        out_specs=pl.BlockSpec((tm, tn), lambda i, j, kk: (i, j)),
        out_shape=jax.ShapeDtypeStruct((m, n), out_dtype),
        scratch_shapes=[pltpu.VMEM((tm, tn), F32)],
        compiler_params=pltpu.CompilerParams(dimension_semantics=("parallel", "parallel", "arbitrary"),
                                             vmem_limit_bytes=VMEM_LIMIT),
        name=name)(a, b)


class Op:
    def __init__(self, arr, block, imap, grad=False, acc=False, first=None, gdtype=F32, gshape=None, gimap=None):
        self.arr, self.block, self.imap = arr, block, imap
        self.grad, self.acc, self.first, self.gdtype = grad, acc, first, gdtype
        self.gshape = arr.shape if gshape is None else gshape
        self.gimap = imap if gimap is None else gimap


def _row_op(arr, tm, grad=False, gdtype=F32):
    return Op(arr, (tm, arr.shape[1]), lambda i: (i, 0), grad=grad, gdtype=gdtype)


def _par_op(arr, grad=False):
    nd = arr.ndim
    return Op(arr, arr.shape, lambda i: (0,) * nd, grad=grad, acc=True, first=lambda ids: ids[0] == 0)


def _load(ref):
    v = ref[...]
    return v.astype(F32) if jnp.issubdtype(v.dtype, jnp.floating) else v


def _store_heads(ref, val):
    if len(ref.shape) == 2:
        ref[...] = val.astype(ref.dtype)
    else:
        d = ref.shape[2]
        for h in range(ref.shape[0]):
            ref[h] = val[:, d * h:d * (h + 1)].astype(ref.dtype)


def _load_heads(ref):
    if len(ref.shape) == 2:
        return ref[...].astype(F32)
    return jnp.concatenate([ref[h].astype(F32) for h in range(ref.shape[0])], axis=-1)


def stage_fwd(name, fn, ops, outs, grid):
    n_in = len(ops)

    def body(*refs):
        res = fn(*[_load(r) for r in refs[:n_in]])
        for r, o in zip(refs[n_in:], res):
            _store_heads(r, o)

    return pl.pallas_call(
        body, grid=grid, in_specs=[pl.BlockSpec(o.block, o.imap) for o in ops],
        out_specs=[pl.BlockSpec(b, im) for (_, _, b, im) in outs],
        out_shape=[jax.ShapeDtypeStruct(s, d) for (s, d, _, _) in outs],
        compiler_params=pltpu.CompilerParams(dimension_semantics=("parallel",) * len(grid),
                                             vmem_limit_bytes=VMEM_LIMIT),
        name=name)(*[o.arr for o in ops])


def stage_bwd(name, fn, ops, cts, grid, value_acc=False):
    n_in = len(ops)
    ct_flat = [c for group in cts if group is not None for c in group]
    n_ct = len(ct_flat)
    diff = [i for i, o in enumerate(ops) if o.grad]
    any_acc = value_acc or any(ops[i].acc for i in diff)
    ngrid = len(grid)

    def body(*refs):
        ids = [pl.program_id(a) for a in range(ngrid)]
        vals = [_load(r) for r in refs[:n_in]]
        ct_refs = refs[n_in:n_in + n_ct]
        out_refs = refs[n_in + n_ct:]

        def f(*dv):
            full = list(vals)
            for i, v in zip(diff, dv):
                full[i] = v
            return tuple(fn(*full))

        res, vjp = jax.vjp(f, *[vals[i] for i in diff])
        ct, pos = [], 0
        for group, r in zip(cts, res):
            if group is None:
                ct.append(jnp.ones_like(r))
            else:
                tot = None
                for _ in group:
                    c = _load_heads(ct_refs[pos])
                    tot = c if tot is None else tot + c
                    pos += 1
                ct.append(tot)
        grads = vjp(tuple(ct))
        for i, g, r in zip(diff, grads, out_refs):
            if ops[i].acc:
                @pl.when(ops[i].first(ids))
                def _(r=r):
                    r[...] = jnp.zeros_like(r)

                r[...] += g.astype(r.dtype)
            else:
                r[...] = g.astype(r.dtype)
        if value_acc:
            r = out_refs[len(diff)]

            @pl.when(ids[0] == 0)
            def _():
                r[...] = jnp.zeros_like(r)

            r[...] += res[0]

    in_specs = [pl.BlockSpec(o.block, o.imap) for o in ops] + [pl.BlockSpec(b, im) for (_, b, im) in ct_flat]
    out_specs = [pl.BlockSpec(ops[i].block, ops[i].gimap) for i in diff]
    out_shape = [jax.ShapeDtypeStruct(ops[i].gshape, ops[i].gdtype) for i in diff]
    if value_acc:
        out_specs.append(pl.BlockSpec((1, 1), lambda *ids: (0, 0)))
        out_shape.append(jax.ShapeDtypeStruct((1, 1), F32))
    sem = ("arbitrary",) * ngrid if any_acc else ("parallel",) * ngrid
    return pl.pallas_call(
        body, grid=grid, in_specs=in_specs, out_specs=out_specs, out_shape=out_shape,
        compiler_params=pltpu.CompilerParams(dimension_semantics=sem, vmem_limit_bytes=VMEM_LIMIT),
        name=name)(*[o.arr for o in ops], *[a for (a, _, _) in ct_flat])


def _sigmoid(x):
    return 1.0 / (1.0 + jnp.exp(-x))


def _silu(x):
    return x * _sigmoid(x)


def _gelu_tanh(x):
    return 0.5 * x * (1.0 + jnp.tanh(math.sqrt(2.0 / math.pi) * (x + 0.044715 * (x * x * x))))


def _ln(x, g, b):
    mu = jnp.mean(x, axis=-1, keepdims=True)
    xc = x - mu
    var = jnp.mean(xc * xc, axis=-1, keepdims=True)
    return xc * lax.rsqrt(var + LN_EPS) * g + b


def _rms(x, g):
    ms = jnp.mean(x * x, axis=-1, keepdims=True)
    return x * lax.rsqrt(ms + RMS_EPS) * g


def _swap_halves(x, half):
    width = x.shape[-1]
    lane = lax.broadcasted_iota(jnp.int32, x.shape, 1)
    return jnp.where(lane % (2 * half) < half, pltpu.roll(x, width - half, 1), pltpu.roll(x, half, 1))


def _make_swap(half):
    @jax.custom_vjp
    def swap(x):
        return _swap_halves(x, half)

    swap.defvjp(lambda x: (_swap_halves(x, half), None), lambda _, dy: (_swap_halves(dy, half),))
    return swap


_swap16, _swap8 = _make_swap(16), _make_swap(8)


def _rope(x, cos, sin_signed, swap):
    return x * cos + swap(x) * sin_signed


def _dot_f32(a, b):
    return jnp.dot(a, b, preferred_element_type=F32, precision=lax.Precision.HIGHEST)


def fn_ln(x, g, b):
    return (_ln(x, g, b),)


def _twice(fn):
    def f(*a):
        (y,) = fn(*a)
        return y, y
    return f


PROJ_W = 2048
P_A, P_Q, P_K, P_V, P_C, P_CQ, P_CKV, P_KR = 0, 512, 768, 896, 1024, 1536, 1792, 1920
CQ_PAD = 256
_CQ_END = P_CQ + MLA_Q_LORA


def _pad_w_in(w):
    z = lambda n: jnp.zeros((w.shape[0], n), w.dtype)
    return jnp.concatenate([w[:, :_CQ_END], z(P_CKV - _CQ_END), w[:, _CQ_END:], z(PROJ_W - P_KR - MLA_ROPE)], axis=1)


def _unpad_w_in(g):
    return jnp.concatenate([g[:, :_CQ_END], g[:, P_CKV:P_KR + MLA_ROPE]], axis=1)


def fn_pre(proj, tab_q, tab_d, seg, place, qng, kng, sg, sb, sw, sbt, mqn, wuq, mkvn, wukv):
    tm = proj.shape[0]
    aglu = proj[:, P_A:P_A + GROUP_W] * _sigmoid(proj[:, P_A + GROUP_W:P_Q])
    b_q, b_k, b_v = proj[:, P_Q:P_K], proj[:, P_K:P_V], proj[:, P_V:P_C]
    cos_q, sin_q = tab_q[:, :GROUP_W], tab_q[:, GROUP_W:]
    q = b_q * lax.rsqrt(_dot_f32(b_q * b_q, seg) + RMS_EPS) * qng
    q = _rope(q, cos_q, sin_q, _swap16)
    k = b_k * lax.rsqrt(_dot_f32(b_k * b_k, seg[:128, :128]) + RMS_EPS) * kng
    k = _rope(k, cos_q[:, :128], sin_q[:, :128], _swap16)
    c = _gelu_tanh(proj[:, P_C:P_CQ])
    u, sv = c[:, :GROUP_W], _ln(c[:, GROUP_W:], sg, sb)
    group = lax.broadcasted_iota(jnp.int32, (CHUNK, GROUP_W), 1) // HEAD_DIM
    rows = []
    for n in range(tm // CHUNK):
        svn = sv[CHUNK * n:CHUNK * (n + 1)]
        acc = jnp.zeros((CHUNK, GROUP_W), F32)
        for g in range(SGU_GROUPS):
            acc = acc + jnp.where(group == g, mm_nn(sw[CHUNK * g:CHUNK * (g + 1)], svn) + sbt[:, g:g + 1], 0.0)
        rows.append(acc)
    o_c = u * jnp.concatenate(rows, axis=0)
    d_cq, d_ckv, d_kr = proj[:, P_CQ:P_CKV], proj[:, P_CKV:P_KR], proj[:, P_KR:PROJ_W]
    cqn = d_cq * lax.rsqrt(jnp.sum(d_cq * d_cq, axis=-1, keepdims=True) * (1.0 / MLA_Q_LORA) + RMS_EPS) * mqn
    cos_d = jnp.concatenate([tab_d[:, :MLA_DK_PAD]] * MLA_HEADS, axis=-1)
    sin_d = jnp.concatenate([tab_d[:, MLA_DK_PAD:]] * MLA_HEADS, axis=-1)
    qf = _rope(mm_nn(cqn, wuq), cos_d, sin_d, _swap8)
    kvd = mm_nn(_rms(d_ckv, mkvn), wukv)
    kf = _rope(kvd[:, :MLA_HEADS * MLA_DK_PAD] + _dot_f32(d_kr, place), cos_d, sin_d, _swap8)
    return aglu, q, k, b_v, o_c, qf, kf, kvd[:, MLA_HEADS * MLA_DK_PAD:]


def fn_aconv(win, w, b, g, beta):
    tm = win.shape[0] - 2 * CONV_A_HALO
    off = CONV_A_HALO - CONV_A_WIDTH // 2
    rolled = [win] + [_roll_rows(win, -r) for r in range(1, 8)]
    acc = None
    for kk in range(CONV_A_WIDTH):
        r = (off + kk) % 8
        base = off + kk - r
        term = rolled[r][base:base + tm] * w[kk:kk + 1, :]
        acc = term if acc is None else acc + term
    return (_silu(_ln(acc + b, g, beta)),)


def fn_resln(h, r, g, b):
    return (_ln(DEEPNORM_ALPHA * h + r, g, b),)


@functools.partial(jax.custom_vjp, nondiff_argnums=(1,))
def _roll_rows(x, shift):
    return pltpu.roll(x, shift % x.shape[0], 0)


_roll_rows.defvjp(lambda x, shift: (pltpu.roll(x, shift % x.shape[0], 0), None),
                  lambda shift, _, dy: (pltpu.roll(dy, (-shift) % dy.shape[0], 0),))


def _shift_down(x):
    row = lax.broadcasted_iota(jnp.int32, x.shape, 0)
    return jnp.where(row == 0, 0.0, _roll_rows(x, 1))


def _shift_up(x):
    row = lax.broadcasted_iota(jnp.int32, x.shape, 0)
    return jnp.where(row == x.shape[0] - 1, 0.0, _roll_rows(x, -1))


def fn_ffnconv(u1, u2, w1, w2, b1, b2):
    c1 = _shift_down(u1) * w1[0:1] + u1 * w1[1:2] + _shift_up(u1) * w1[2:3] + b1
    c2 = _shift_down(u2) * w2[0:1] + u2 * w2[1:2] + _shift_up(u2) * w2[2:3] + b2
    return (_silu(c1) * c2,)


def fn_final(h, r, t, g, b):
    y = _ln(DEEPNORM_ALPHA * h + r, g, b)
    err = (y - t) * (y - t)
    return (0.5 * jnp.sum(jnp.mean(err, axis=-1, keepdims=True), axis=0, keepdims=True),)


def _rope_tables(seq):
    n_rows = seq // GRID_W

    def tab(half):
        inv = ROPE_THETA ** (-jnp.arange(half, dtype=F32) / half)
        ar = jnp.arange(n_rows, dtype=F32)[:, None] * inv[None, :]
        ac = jnp.arange(GRID_W, dtype=F32)[:, None] * inv[None, :]
        by_row = lambda a: jnp.repeat(a, GRID_W, axis=0)
        by_col = lambda a: jnp.tile(a, (n_rows, 1))
        cos_r, sin_r, cos_c, sin_c = by_row(jnp.cos(ar)), by_row(jnp.sin(ar)), by_col(jnp.cos(ac)), by_col(jnp.sin(ac))
        return (jnp.concatenate([cos_r, cos_r, cos_c, cos_c], axis=-1),
                jnp.concatenate([-sin_r, sin_r, -sin_c, sin_c], axis=-1))

    cos_b, sin_b = tab(HEAD_DIM // 4)
    tab_q = jnp.concatenate([cos_b] * GQA_HEADS + [sin_b] * GQA_HEADS, axis=-1)
    cos_r, sin_r = tab(MLA_ROPE // 4)
    ones = jnp.ones((seq, MLA_NOPE), F32)
    zpad = MLA_DK_PAD - MLA_NOPE - MLA_ROPE
    tab_d = jnp.concatenate([ones, cos_r, jnp.ones((seq, zpad), F32), 0.0 * ones, sin_r, jnp.zeros((seq, zpad), F32)],
                            axis=-1)
    lane = jnp.arange(GROUP_W)
    seg = jnp.where(lane[:, None] // HEAD_DIM == lane[None, :] // HEAD_DIM, 1.0 / HEAD_DIM, 0.0).astype(F32)
    src, dst = jnp.arange(128)[:, None], jnp.arange(MLA_HEADS * MLA_DK_PAD)[None, :]
    place = jnp.where((src < MLA_ROPE) & (dst % MLA_DK_PAD == MLA_NOPE + src), 1.0, 0.0).astype(F32)
    return tab_q, tab_d, seg, place


def _pre_ops(proj, tabs, kp, grad):
    tm = ROW_TILE
    ops = [_row_op(proj, tm, grad=grad, gdtype=MXU_DTYPE), _row_op(tabs[0], tm), _row_op(tabs[1], tm),
           _par_op(tabs[2]), _par_op(tabs[3])]
    ops += [_par_op(kp[n], grad=grad) for n in ('qng', 'kng', 'sg', 'sb', 'sw', 'sbt', 'mqn', 'wuq', 'mkvn', 'wukv')]
    return ops


PRE_OUTS = ((0, GROUP_W), (GQA_HEADS, HEAD_DIM), (GQA_KV_HEADS, HEAD_DIM), (GQA_KV_HEADS, HEAD_DIM), (0, GROUP_W),
            (MLA_HEADS, MLA_DK_PAD), (MLA_HEADS, MLA_DK_PAD), (MLA_HEADS, MLA_V))


def _pre_out_specs(seq, tm):
    specs = []
    for heads, w in PRE_OUTS:
        if heads:
            specs.append(((heads, seq, w), (heads, tm, w), lambda i: (0, i, 0)))
        else:
            specs.append(((seq, w), (tm, w), lambda i: (i, 0)))
    return specs


def pre_fwd(proj, tabs, kp, tag):
    seq = proj.shape[0]
    tm = ROW_TILE
    dts = (F32,) + (MXU_DTYPE,) * 7
    outs = [(shape, dt, block, imap) for (shape, block, imap), dt in zip(_pre_out_specs(seq, tm), dts)]
    return stage_fwd("pre_fwd" + tag, fn_pre, _pre_ops(proj, tabs, kp, False), outs, (seq // tm,))


def pre_bwd(proj, tabs, kp, cts, tag):
    seq = proj.shape[0]
    tm = ROW_TILE
    ct = [[c if isinstance(c, tuple) else (c, block, imap)] for c, (_, block, imap) in zip(cts, _pre_out_specs(seq, tm))]
    return stage_bwd("pre_bwd" + tag, fn_pre, _pre_ops(proj, tabs, kp, True), ct, (seq // tm,))


def _aconv_ops(kp, grad):
    return [_par_op(kp[n], grad=grad) for n in ('caw', 'cab', 'lag', 'lab')]


def aconv_fwd(aglu_pad, kp, tag):
    seq = aglu_pad.shape[0] - 2 * CONV_A_HALO
    tm = ROW_TILE
    n_par = 4

    def body(x_ref, *refs):
        i = pl.program_id(0)
        win = x_ref[pl.ds(pl.multiple_of(i * tm, tm), tm + 2 * CONV_A_HALO), :]
        (o,) = fn_aconv(win, *[_load(r) for r in refs[:n_par]])
        refs[n_par][...] = o.astype(refs[n_par].dtype)

    pars = _aconv_ops(kp, False)
    return pl.pallas_call(
        body, grid=(seq // tm,),
        in_specs=[pl.BlockSpec(aglu_pad.shape, lambda i: (0, 0))] + [pl.BlockSpec(o.block, o.imap) for o in pars],
        out_specs=pl.BlockSpec((tm, GROUP_W), lambda i: (i, 0)),
        out_shape=jax.ShapeDtypeStruct((seq, GROUP_W), MXU_DTYPE),
        compiler_params=pltpu.CompilerParams(dimension_semantics=("parallel",), vmem_limit_bytes=VMEM_LIMIT),
        name="aconv_fwd" + tag)(aglu_pad, *[o.arr for o in pars])


def aconv_bwd(aglu_pad, kp, d_oa, tag):
    seq = aglu_pad.shape[0] - 2 * CONV_A_HALO
    tm = ROW_TILE
    n_par = 4

    def body(x_ref, *refs):
        i = pl.program_id(0)
        rows = pl.ds(pl.multiple_of(i * tm, tm), tm + 2 * CONV_A_HALO)
        pars = [_load(r) for r in refs[:n_par]]
        ct = refs[n_par][...].astype(F32)
        outs = refs[n_par + 1:]
        _, vjp = jax.vjp(lambda *a: fn_aconv(*a), x_ref[rows, :], *pars)
        grads = vjp((ct,))

        @pl.when(i == 0)
        def _():
            for r in outs:
                r[...] = jnp.zeros_like(r)

        outs[0][rows, :] += grads[0]
        for r, g in zip(outs[1:], grads[1:]):
            r[...] += g

    pars = _aconv_ops(kp, True)
    whole = pl.BlockSpec(aglu_pad.shape, lambda i: (0, 0))
    par_specs = [pl.BlockSpec(o.block, o.imap) for o in pars]
    return pl.pallas_call(
        body, grid=(seq // tm,),
        in_specs=[whole] + par_specs + [pl.BlockSpec((tm, GROUP_W), lambda i: (i, 0))],
        out_specs=[whole] + par_specs,
        out_shape=[jax.ShapeDtypeStruct(aglu_pad.shape, F32)] + [jax.ShapeDtypeStruct(o.arr.shape, F32) for o in pars],
        compiler_params=pltpu.CompilerParams(dimension_semantics=("arbitrary",), vmem_limit_bytes=VMEM_LIMIT),
        name="aconv_bwd" + tag)(aglu_pad, *[o.arr for o in pars], d_oa)


ATTN_TQ = 256
ATTN_TK = 512


def attn_fwd(q3, k3, v3, scale, tag):
    heads, seq, dk = q3.shape
    group = heads // k3.shape[0]
    kv_per_pair = 2 // group
    dv = v3.shape[2]
    tq, tk = min(ATTN_TQ, seq), min(ATTN_TK, seq)
    n_chunks = seq // tk
    log2e = math.log2(math.e)

    def one_head(q, k_ref, v_ref):
        scores = lambda c: _rawdot(q, k_ref[pl.ds(c * tk, tk), :], 1, 1)
        m, l, acc = jnp.full((tq, 1), -jnp.inf, F32), jnp.zeros((tq, 1), F32), jnp.zeros((tq, dv), F32)
        s_next = scores(0)
        for c in range(n_chunks):
            s_cur, s_next = s_next, (scores(c + 1) if c + 1 < n_chunks else None)
            t = s_cur * (scale * log2e)
            m_new = jnp.maximum(m, jnp.max(t, axis=-1, keepdims=True))
            alpha = jnp.exp2(m - m_new)
            p = jnp.exp2(t - m_new)
            l = alpha * l + jnp.sum(p, axis=-1, keepdims=True)
            acc = alpha * acc + _rawdot(p, v_ref[pl.ds(c * tk, tk), :], 1, 0)
            m = m_new
        return acc * (1.0 / l), m * (1.0 / log2e) + jnp.log(l)

    def body(q_ref, k_ref, v_ref, o_ref, lse_ref):
        outs = []
        for h in range(2):
            o, lse = one_head(q_ref[h], k_ref.at[h // group], v_ref.at[h // group])
            lse_ref[h] = lse
            outs.append(o)
        o_ref[...] = jnp.concatenate(outs, axis=-1)

    return pl.pallas_call(
        body, grid=(heads // 2, seq // tq),
        in_specs=[pl.BlockSpec((2, tq, dk), lambda j, i: (j, i, 0)),
                  pl.BlockSpec((kv_per_pair, seq, dk), lambda j, i: (j, 0, 0)),
                  pl.BlockSpec((kv_per_pair, seq, dv), lambda j, i: (j, 0, 0))],
        out_specs=[pl.BlockSpec((tq, 2 * dv), lambda j, i: (i, j)),
                   pl.BlockSpec((2, tq, 1), lambda j, i: (j, i, 0))],
        out_shape=[jax.ShapeDtypeStruct((seq, heads * dv), F32), jax.ShapeDtypeStruct((heads, seq, 1), F32)],
        compiler_params=pltpu.CompilerParams(dimension_semantics=("parallel", "parallel"),
                                             vmem_limit_bytes=VMEM_LIMIT),
        name="attn_fwd" + tag)(q3, k3, v3)


def attn_bwd(q3, k3, v3, o, lse3, do_all, do_col, scale, tag):
    heads, seq, dk = q3.shape
    group = heads // k3.shape[0]
    kv_per_pair = 2 // group
    dv = v3.shape[2]
    tq, tk = min(ATTN_TQ, seq), min(ATTN_TK, seq)
    n_chunks = seq // tk
    log2e = math.log2(math.e)

    def one_head(q, do, o_h, lse, k_ref, v_ref, dk_ref, dv_ref):
        dob = do.astype(MXU_DTYPE)
        delta = jnp.sum(do * o_h, axis=-1, keepdims=True)
        lse2 = lse * log2e
        rows = lambda c: pl.ds(c * tk, tk)
        products = lambda c: (_rawdot(q, k_ref[rows(c), :], 1, 1), _rawdot(dob, v_ref[rows(c), :], 1, 1))
        dq = jnp.zeros((tq, dk), F32)
        nxt = products(0)
        for c in range(n_chunks):
            (s_cur, dp_cur), nxt = nxt, (products(c + 1) if c + 1 < n_chunks else None)
            p = jnp.exp2(s_cur * (scale * log2e) - lse2)
            ds = (p * ((dp_cur - delta) * scale)).astype(MXU_DTYPE)
            dv_ref[rows(c), :] += _rawdot(p, dob, 0, 0)
            dk_ref[rows(c), :] += _rawdot(ds, q, 0, 0)
            dq = dq + _rawdot(ds, k_ref[rows(c), :], 1, 0)
        return dq

    def body(q_ref, k_ref, v_ref, o_ref, lse_ref, do_ref, dq_ref, dk_ref, dv_ref):
        @pl.when(pl.program_id(1) == 0)
        def _():
            dk_ref[...] = jnp.zeros_like(dk_ref)
            dv_ref[...] = jnp.zeros_like(dv_ref)

        do_pair, o_pair = do_ref[...], o_ref[...]
        for h in range(2):
            kv = h // group
            dq_ref[h] = one_head(q_ref[h], do_pair[:, dv * h:dv * (h + 1)], o_pair[:, dv * h:dv * (h + 1)],
                                 lse_ref[h], k_ref.at[kv], v_ref.at[kv], dk_ref.at[kv], dv_ref.at[kv])

    qspec = lambda d: pl.BlockSpec((2, tq, d), lambda j, i: (j, i, 0))
    kvspec = lambda d: pl.BlockSpec((kv_per_pair, seq, d), lambda j, i: (j, 0, 0))
    return pl.pallas_call(
        body, grid=(heads // 2, seq // tq),
        in_specs=[qspec(dk), kvspec(dk), kvspec(dv), pl.BlockSpec((tq, 2 * dv), lambda j, i: (i, j)), qspec(1),
                  pl.BlockSpec((tq, 2 * dv), lambda j, i: (i, do_col + j))],
        out_specs=[qspec(dk), kvspec(dk), kvspec(dv)],
        out_shape=[jax.ShapeDtypeStruct(q3.shape, F32), jax.ShapeDtypeStruct(k3.shape, F32),
                   jax.ShapeDtypeStruct(v3.shape, F32)],
        compiler_params=pltpu.CompilerParams(dimension_semantics=("parallel", "arbitrary"),
                                             vmem_limit_bytes=VMEM_LIMIT),
        name="attn_bwd" + tag)(q3, k3, v3, o, lse3, do_all)


def resln_fwd(h, r, g, b, tag):
    seq, d = h.shape
    tm = ROW_TILE
    ops = [_row_op(h, tm), _row_op(r, tm), _par_op(g), _par_op(b)]
    outs = [((seq, d), dt, (tm, d), lambda i: (i, 0)) for dt in (F32, MXU_DTYPE)]
    return stage_fwd("resln_fwd" + tag, _twice(fn_resln), ops, outs, (seq // tm,))


def resln_bwd(h, r, g, b, dys, tag):
    seq, d = h.shape
    tm = ROW_TILE
    ops = [_row_op(h, tm, grad=True), _row_op(r, tm, grad=True, gdtype=MXU_DTYPE), _par_op(g, grad=True),
           _par_op(b, grad=True)]
    ct = [[(dy, (tm, d), lambda i: (i, 0)) for dy in dys]]
    return stage_bwd("resln_bwd" + tag, fn_resln, ops, ct, (seq // tm,))


def _ffnconv_ops(up1, up2, w, b, grad):
    seq = up1.shape[0]
    nblk = D_FF // 128
    lo, hi = (lambda j: (0, j)), (lambda j: (0, j + nblk))
    half = lambda a: dict(gshape=(a.shape[0], D_FF), gimap=lo)
    return [Op(up1, (seq, 128), lo, grad=grad, gdtype=MXU_DTYPE), Op(up2, (seq, 128), lo, grad=grad, gdtype=MXU_DTYPE),
            Op(w, (3, 128), lo, grad=grad, **half(w)), Op(w, (3, 128), hi, grad=grad, **half(w)),
            Op(b, (1, 128), lo, grad=grad, **half(b)), Op(b, (1, 128), hi, grad=grad, **half(b))]


def ffnconv_fwd(up1, up2, w, b, tag):
    seq = up1.shape[0]
    outs = [((seq, D_FF), MXU_DTYPE, (seq, 128), lambda j: (0, j))]
    return stage_fwd("ffnconv_fwd" + tag, fn_ffnconv, _ffnconv_ops(up1, up2, w, b, False), outs, (D_FF // 128,))[0]


def ffnconv_bwd(up1, up2, w, b, dact, tag):
    seq = up1.shape[0]
    ct = [[(dact, (seq, 128), lambda j: (0, j))]]
    du1, du2, dw1, dw2, db1, db2 = stage_bwd("ffnconv_bwd" + tag, fn_ffnconv, _ffnconv_ops(up1, up2, w, b, True),
                                             ct, (D_FF // 128,))
    cat = lambda a, b_: jnp.concatenate([a, b_], axis=-1)
    return du1, du2, cat(dw1, dw2), cat(db1, db2)


def final_bwd(h, r, t, g, b, tag):
    seq, d = h.shape
    tm = ROW_TILE
    ops = [_row_op(h, tm, grad=True), _row_op(r, tm, grad=True, gdtype=MXU_DTYPE), _row_op(t, tm),
           _par_op(g, grad=True), _par_op(b, grad=True)]
    return stage_bwd("final_bwd" + tag, fn_final, ops, [None], (seq // tm,), value_acc=True)


def _layer_params(wts, l):
    row = lambda a: a.reshape(1, -1)
    wuq = wts['mla_w_uq'][l].reshape(MLA_Q_LORA, MLA_HEADS, MLA_NOPE + MLA_ROPE)
    wuq = jnp.pad(wuq, ((0, CQ_PAD - MLA_Q_LORA), (0, 0), (0, MLA_DK_PAD - MLA_NOPE - MLA_ROPE)))
    wukv = wts['mla_w_ukv'][l].reshape(MLA_KV_LORA, MLA_HEADS, MLA_NOPE + MLA_V)
    wuk = jnp.pad(wukv[:, :, :MLA_NOPE], ((0, 0), (0, 0), (0, MLA_DK_PAD - MLA_NOPE)))
    return dict(
        qng=jnp.tile(row(wts['qk_norm_q'][l]), (1, GQA_HEADS)), kng=jnp.tile(row(wts['qk_norm_k'][l]), (1, GQA_KV_HEADS)),
        sg=row(wts['sgu_ln_g'][l]), sb=row(wts['sgu_ln_b'][l]),
        sw=wts['sgu_w'][l].reshape(SGU_GROUPS * CHUNK, CHUNK), sbt=wts['sgu_b'][l].T,
        mqn=jnp.pad(row(wts['mla_q_norm'][l]), ((0, 0), (0, CQ_PAD - MLA_Q_LORA))),
        wuq=wuq.reshape(CQ_PAD, MLA_HEADS * MLA_DK_PAD), mkvn=row(wts['mla_kv_norm'][l]),
        wukv=jnp.concatenate([wuk.reshape(MLA_KV_LORA, -1), wukv[:, :, MLA_NOPE:].reshape(MLA_KV_LORA, -1)], axis=1),
        caw=wts['conv_a_w'][l], cab=row(wts['conv_a_b'][l]), lag=row(wts['ln_a_g'][l]), lab=row(wts['ln_a_b'][l]),
        lmg=row(wts['ln_mix_g'][l]), lmb=row(wts['ln_mix_b'][l]),
        fcw=wts['ffn_conv_w'][l], fcb=row(wts['ffn_conv_b'][l]),
        lfg=row(wts['ln_ffn_g'][l]), lfb=row(wts['ln_ffn_b'][l]))


def _to_heads(a, heads):
    seq = a.shape[0]
    return a.reshape(seq, heads, -1).transpose(1, 0, 2)


def _from_heads(a3):
    return a3.transpose(1, 0, 2).reshape(a3.shape[1], -1)


def local_step(x, target, ln_in, get_wts, mat, hook):
    seq = x.shape[0]
    tm = ROW_TILE
    tabs = _rope_tables(seq)
    scale_b = HEAD_DIM ** -0.5
    scale_d = (MLA_NOPE + MLA_ROPE) ** -0.5
    ln_in_g, ln_in_b = ln_in[0].reshape(1, -1), ln_in[1].reshape(1, -1)

    h, h_m = stage_fwd("ln_in_fwd", _twice(fn_ln), [_row_op(x, tm), _par_op(ln_in_g), _par_op(ln_in_b)],
                       [((seq, D_MODEL), dt, (tm, D_MODEL), lambda i: (i, 0)) for dt in (F32, MXU_DTYPE)],
                       (seq // tm,))
    wts = get_wts(h_m)
    saved = []
    for l in range(DEPTH):
        tag = f"_l{l}"
        kp, unprep = jax.vjp(lambda w: _layer_params(w, l), wts)
        m = {'w_in': mat(l, 'w_in', h_m)}
        proj = matmul(h_m, m['w_in'], 'nn', F32, "mm_proj" + tag)
        aglu, q3, k3, v3, o_c, qd3, kd3, vd3 = pre_fwd(proj, tabs, kp, tag)
        aglu_pad = jnp.pad(aglu, ((CONV_A_HALO, CONV_A_HALO), (0, 0)))
        o_a = aconv_fwd(aglu_pad, kp, tag)
        o_b3, lse_b3 = attn_fwd(q3, k3, v3, scale_b, "_b" + tag)
        o_d3, lse_d3 = attn_fwd(qd3, kd3, vd3, scale_d, "_d" + tag)
        o_cat = jnp.concatenate([o_a, o_b3.astype(MXU_DTYPE), o_c, o_d3.astype(MXU_DTYPE)], axis=-1)
        m['w_out'] = mat(l, 'w_out', o_cat)
        mix = matmul(o_cat, m['w_out'], 'nn', F32, "mm_mix" + tag)
        h1, h1_m = resln_fwd(h, mix, kp['lmg'], kp['lmb'], "_mix" + tag)
        w_up = mat(l, 'ffn_w_up', h1_m)
        m['w_up1'], m['w_up2'] = w_up[:, :D_FF], w_up[:, D_FF:]
        up1 = matmul(h1_m, m['w_up1'], 'nn', F32, "mm_up1" + tag)
        up2 = matmul(h1_m, m['w_up2'], 'nn', F32, "mm_up2" + tag)
        act = ffnconv_fwd(up1, up2, kp['fcw'], kp['fcb'], tag)
        m['ffn_w_down'] = mat(l, 'ffn_w_down', act)
        f = matmul(act, m['ffn_w_down'], 'nn', F32, "mm_down" + tag)
        saved.append(dict(kp=kp, unprep=unprep, m=m, h=h, h_m=h_m, h1_m=h1_m, proj=proj, o_b3=o_b3, lse_b3=lse_b3,
                          o_d3=o_d3, lse_d3=lse_d3, aglu_pad=aglu_pad, q3=q3, k3=k3, v3=v3, qd3=qd3,
                          kd3=kd3, vd3=vd3, o_cat=o_cat, mix=mix, h1=h1, up1=up1, up2=up2, act=act, f=f))
        if l + 1 < DEPTH:
            h, h_m = resln_fwd(h1, f, kp['lfg'], kp['lfb'], "_ffn" + tag)

    after = lambda a, tok: a if tok is None else a + tok
    small_acc = None
    dh_parts = None
    loss = None
    tok = None
    g_mix = None
    for l in reversed(range(DEPTH)):
        tag = f"_l{l}"
        s = saved[l]
        kp, m = s['kp'], s['m']
        dkp = {}
        lfg = after(kp['lfg'], tok)
        if l == DEPTH - 1:
            dh1_a, df, dkp['lfg'], dkp['lfb'], loss = final_bwd(s['h1'], s['f'], target, lfg, kp['lfb'], tag)
        else:
            dh1_a, df, dkp['lfg'], dkp['lfb'] = resln_bwd(s['h1'], s['f'], lfg, kp['lfb'], dh_parts, "_ffn" + tag)
        g_down = matmul(s['act'], df, 'tn', COMM_DTYPE, "mm_gdown" + tag)
        dact = matmul(df, m['ffn_w_down'], 'nt', F32, "mm_dact" + tag)
        dup1, dup2, dkp['fcw'], dkp['fcb'] = ffnconv_bwd(s['up1'], s['up2'], kp['fcw'], kp['fcb'], dact, tag)
        dh1_b1 = matmul(dup1, m['w_up1'], 'nt', F32, "mm_dh1a" + tag)
        dh1_b2 = matmul(dup2, m['w_up2'], 'nt', F32, "mm_dh1b" + tag)
        g_up = jnp.concatenate([matmul(s['h1_m'], dup1, 'tn', COMM_DTYPE, "mm_gup1" + tag),
                                matmul(s['h1_m'], dup2, 'tn', COMM_DTYPE, "mm_gup2" + tag)], axis=1)
        tok = hook(f"ffn{l}", {('ffn_w_down', l): g_down, ('ffn_w_up', l): g_up})
        dh_a, dmix, dkp['lmg'], dkp['lmb'] = resln_bwd(s['h'], s['mix'], after(kp['lmg'], tok), kp['lmb'],
                                                       [dh1_a, dh1_b1, dh1_b2], "_mix" + tag)
        do_cat = matmul(dmix, m['w_out'], 'nt', F32, "mm_docat" + tag)
        g_out = matmul(s['o_cat'], dmix, 'tn', COMM_DTYPE, "mm_gout" + tag)
        lse_b3 = s['lse_b3']
        if l == 0:
            lse_b3 = after(lse_b3, hook("out0", {('w_out', l): g_out}))
        do_c = (do_cat, (ROW_TILE, GROUP_W), lambda i: (i, 2))
        pair_w = 2 * HEAD_DIM
        dq3, dk3, dv3 = attn_bwd(s['q3'], s['k3'], s['v3'], s['o_b3'], lse_b3, do_cat, GROUP_W // pair_w,
                                 scale_b, "_b" + tag)
        dqd3, dkd3, dvd3 = attn_bwd(s['qd3'], s['kd3'], s['vd3'], s['o_d3'], s['lse_d3'], do_cat,
                                    3 * GROUP_W // pair_w, scale_d, "_d" + tag)
        daglu_pad, dkp['caw'], dkp['cab'], dkp['lag'], dkp['lab'] = aconv_bwd(s['aglu_pad'], kp, do_cat, tag)
        cts = [daglu_pad[CONV_A_HALO:CONV_A_HALO + seq], dq3, dk3, dv3, do_c, dqd3, dkd3, dvd3]
        pre_g = pre_bwd(s['proj'], tabs, kp, cts, tag)
        dproj = pre_g[0]
        for n, g in zip(('qng', 'kng', 'sg', 'sb', 'sw', 'sbt', 'mqn', 'wuq', 'mkvn', 'wukv'), pre_g[1:]):
            dkp[n] = g
        dh_b = matmul(dproj, m['w_in'], 'nt', F32, "mm_dh" + tag)
        g_in = matmul(s['h_m'], dproj, 'tn', COMM_DTYPE, "mm_gin" + tag)
        dh_parts = [dh_a, dh_b]
        (dw,) = s['unprep'](dkp)
        small_acc = dw if small_acc is None else jax.tree.map(jnp.add, small_acc, dw)
        g_mix = {('w_out', l): g_out, ('w_in', l): _unpad_w_in(g_in)}
        if l > 0:
            tok = hook(f"mix{l}", g_mix)
        else:
            g_mix.pop(('w_out', l))

    g_mix.update({(n, None): small_acc[n] for n in SHARDED if n not in MATMUL_WEIGHTS})
    tok = hook("last", g_mix)
    dx, dg, db = stage_bwd("ln_in_bwd", fn_ln,
                           [_row_op(x, tm, grad=True), _par_op(after(ln_in_g, tok), grad=True),
                            _par_op(ln_in_b, grad=True)],
                           [[(p, (tm, D_MODEL), lambda i: (i, 0)) for p in dh_parts]], (seq // tm,))
    out = {n: small_acc[n] for n in REPLICATED}
    out['ln_in_g'], out['ln_in_b'] = dg.reshape(-1), db.reshape(-1)
    return loss, dx, out


def _peer(x, y, c, r):
    return ((1 - x) if r & 4 else x, (1 - y) if r & 2 else y, (1 - c) if r & 1 else c)


def _exchange_copy(src_ref, land_ref, send_sems, recv_sems, k, gather, x, y, c, r):
    px, py, pc = _peer(x, y, c, r)
    me, peer = 4 * x + 2 * y + c, 4 * px + 2 * py + pc
    src = src_ref if gather else src_ref.at[peer]
    mk = lambda dst: pltpu.make_async_remote_copy(
        src_ref=src, dst_ref=dst, send_sem=send_sems.at[k * (N_DEV - 1) + r - 1],
        recv_sem=recv_sems.at[k * (N_DEV - 1) + r - 1],
        device_id=(px, py, pc), device_id_type=pl.DeviceIdType.MESH)
    return mk(land_ref.at[me]), mk(land_ref.at[peer])


_HBM_SPEC = pl.BlockSpec(memory_space=pltpu.HBM)
_SEM_SPEC = pl.BlockSpec(memory_space=pltpu.SEMAPHORE)


def exchange_start(srcs, gather, groups, name):
    n_t = len(srcs)
    lands =[lax.empty(((N_DEV,) + s.shape) if gt else s.shape, s.dtype) for s, gt in zip(srcs, gather)]

    def body(*refs):
        src_refs, land_refs = refs[:n_t], refs[n_t:2 * n_t]
        sem_refs = refs[2 * n_t:2 * n_t + 2 * len(groups)]
        token = refs[-1]
        x, y, c = lax.axis_index("x"), lax.axis_index("y"), lax.axis_index("c")
        for gi, g in enumerate(groups):
            for k, t in enumerate(g):
                for r in range(1, N_DEV):
                    _exchange_copy(src_refs[t], land_refs[t], sem_refs[2 * gi], sem_refs[2 * gi + 1], k, gather[t],
                                   x, y, c, r)[0].start()
        token[...] = jnp.zeros_like(token)

    sem_shapes = []
    for g in groups:
        sem_shapes += [pltpu.SemaphoreType.DMA((len(g) * (N_DEV - 1),))] * 2
    hbm_shapes = [pltpu.HBM(a.shape, a.dtype) for a in list(srcs) + lands]
    n_sem = len(sem_shapes)
    res = pl.pallas_call(
        body, name=name,
        out_shape=tuple(sem_shapes + hbm_shapes + [jax.ShapeDtypeStruct((8, 128), F32)]),
        in_specs=[_HBM_SPEC] * (2 * n_t),
        out_specs=tuple([_SEM_SPEC] * n_sem + [_HBM_SPEC] * (2 * n_t) + [pl.BlockSpec(memory_space=pltpu.VMEM)]),
        input_output_aliases={i: n_sem + i for i in range(2 * n_t)},
        compiler_params=pltpu.CompilerParams(has_side_effects=pltpu.SideEffectType.DATAFLOW_SIDE_EFFECTING),
    )(*[pltpu.with_memory_space_constraint(a, pltpu.HBM) for a in list(srcs) + lands])
    sems = [(res[2 * gi], res[2 * gi + 1]) for gi in range(len(groups))]
    return sems, list(res[n_sem:n_sem + n_t]), list(res[n_sem + n_t:n_sem + 2 * n_t]), res[-1]


def exchange_wait(sems, srcs, lands, gather, after, name):
    n_t = len(srcs)

    def body(*refs):
        src_refs, land_refs = refs[:n_t], refs[n_t:2 * n_t]
        send_sems, recv_sems = refs[2 * n_t], refs[2 * n_t + 1]
        x, y, c = lax.axis_index("x"), lax.axis_index("y"), lax.axis_index("c")
        for k in range(n_t):
            for r in range(1, N_DEV):
                send, recv = _exchange_copy(src_refs[k], land_refs[k], send_sems, recv_sems, k, gather[k], x, y, c, r)
                send.wait_send()
                recv.wait_recv()

    res = pl.pallas_call(
        body, name=name,
        out_shape=tuple(pltpu.HBM(a.shape, a.dtype) for a in list(srcs) + list(lands)),
        in_specs=[_HBM_SPEC] * (2 * n_t) + [_SEM_SPEC, _SEM_SPEC, pl.BlockSpec(memory_space=pl.ANY)],
        out_specs=tuple([_HBM_SPEC] * (2 * n_t)),
        input_output_aliases={i: i for i in range(2 * n_t)},
        compiler_params=pltpu.CompilerParams(has_side_effects=pltpu.SideEffectType.DATAFLOW_SIDE_EFFECTING),
    )(*srcs, *lands, sems[0], sems[1], after)
    return list(res[:n_t]), list(res[n_t:])


def adamw(parts, w, m, v, name):
    n_l, n_r, n_c = w.shape
    tr = n_r
    if n_r % 8 == 0:
        for cand in (512, 256, 128, 64, 32, 16, 8):
            if n_r % cand == 0 and cand * n_c * 4 <= 512 * 1024:
                tr = cand
                break
    c1 = 1.0 - ADAM_B1 ** ADAM_STEP
    c2 = 1.0 - ADAM_B2 ** ADAM_STEP
    per_layer = isinstance(parts, (list, tuple))
    n_p = n_l if per_layer else 1
    n_rb = n_r // tr

    def update(g, w_ref, m_ref, v_ref, g_ref, d_ref, nm_ref, nv_ref):
        w_, m_, v_ = w_ref[0], m_ref[0], v_ref[0]
        nm = ADAM_B1 * m_ + (1.0 - ADAM_B1) * g
        nv = ADAM_B2 * v_ + (1.0 - ADAM_B2) * (g * g)
        g_ref[0] = g
        nm_ref[0] = nm
        nv_ref[0] = nv
        d_ref[0] = -ADAM_LR * ((nm / c1) / (jnp.sqrt(nv / c2) + ADAM_EPS) + ADAM_WD * w_)

    def body(*refs):
        p_refs, rest = refs[:n_p], refs[n_p:]
        if not per_layer:
            g = p_refs[0][0, 0].astype(F32)
            for s in range(1, N_DEV):
                g = g + p_refs[0][s, 0].astype(F32)
            update(g, *rest)
        else:
            for lay in range(n_l):
                @pl.when(pl.program_id(0) == lay)
                def _(lay=lay):
                    g = p_refs[lay][0].astype(F32)
                    for s in range(1, N_DEV):
                        g = g + p_refs[lay][s].astype(F32)
                    update(g, *rest)

    blk = pl.BlockSpec((1, tr, n_c), lambda l, r: (l, r, 0))
    if per_layer:
        def p_spec(lay):
            park = 0 if lay > 0 else n_rb - 1
            return pl.BlockSpec((N_DEV, tr, n_c), lambda l, r: (0, jnp.where(l == lay, r, park), 0))
        p_specs, p_args = [p_spec(lay) for lay in range(n_l)], list(parts)
    else:
        p_specs, p_args = [pl.BlockSpec((N_DEV, 1, tr, n_c), lambda l, r: (0, l, r, 0))], [parts]
    return pl.pallas_call(
        body, grid=(n_l, n_rb), in_specs=p_specs + [blk, blk, blk],
        out_specs=[blk] * 4, out_shape=[jax.ShapeDtypeStruct(w.shape, F32)] * 4,
        compiler_params=pltpu.CompilerParams(dimension_semantics=("arbitrary", "arbitrary"),
                                             vmem_limit_bytes=VMEM_LIMIT),
        name=name)(*p_args, w, m, v)


def adamw_replicated(lands, own, ws, ms, vs, loss_land, loss_own):
    n_t = len(lands)
    c1 = 1.0 - ADAM_B1 ** ADAM_STEP
    c2 = 1.0 - ADAM_B2 ** ADAM_STEP

    def body(*refs):
        ins, outs = refs[:5 * n_t + 2], refs[5 * n_t + 2:]
        me = 4 * lax.axis_index("x") + 2 * lax.axis_index("y") + lax.axis_index("c")

        def total(land_ref, own_ref):
            g = None
            for s in range(N_DEV):
                term = jnp.where(me == s, own_ref[...], land_ref[s])
                g = term if g is None else g + term
            return g

        for t in range(n_t):
            land_ref, own_ref, w_ref, m_ref, v_ref = ins[5 * t:5 * t + 5]
            g = total(land_ref, own_ref)
            nm = ADAM_B1 * m_ref[...] + (1.0 - ADAM_B1) * g
            nv = ADAM_B2 * v_ref[...] + (1.0 - ADAM_B2) * (g * g)
            g_ref, d_ref, nm_ref, nv_ref = outs[4 * t:4 * t + 4]
            g_ref[...] = g
            nm_ref[...] = nm
            nv_ref[...] = nv
            d_ref[...] = -ADAM_LR * ((nm / c1) / (jnp.sqrt(nv / c2) + ADAM_EPS) + ADAM_WD * w_ref[...])
        outs[4 * n_t][...] = total(ins[5 * n_t], ins[5 * n_t + 1])

    args = []
    for t in range(n_t):
        args += [lands[t], own[t], ws[t], ms[t], vs[t]]
    out_shape = []
    for t in range(n_t):
        out_shape += [jax.ShapeDtypeStruct(ws[t].shape, F32)] * 4
    out_shape.append(jax.ShapeDtypeStruct(loss_own.shape, F32))
    res = pl.pallas_call(body, out_shape=out_shape,
                         compiler_params=pltpu.CompilerParams(vmem_limit_bytes=VMEM_LIMIT),
                         name="adamw_replicated")(*args, loss_land, loss_own)
    return [tuple(res[4 * t:4 * t + 4]) for t in range(n_t)], res[-1]


def _shard_slots(g, axis):
    if axis == 1:
        return g.reshape(g.shape[0], N_DEV, g.shape[1] // N_DEV, g.shape[2]).transpose(1, 0, 2, 3)
    return g.reshape(g.shape[0], g.shape[1], N_DEV, g.shape[2] // N_DEV).transpose(2, 0, 1, 3)


def _unshard(slots, axis):
    if axis == 1:
        return slots.transpose(1, 0, 2, 3).reshape(slots.shape[1], -1, slots.shape[3])
    return slots.transpose(1, 2, 0, 3).reshape(slots.shape[1], slots.shape[2], -1)


def kernel(x, ln_in_g, ln_in_b, w_in, conv_a_w, conv_a_b, ln_a_g, ln_a_b, qk_norm_q, qk_norm_k, sgu_ln_g, sgu_ln_b, sgu_w, sgu_b, mla_q_norm, mla_w_uq, mla_kv_norm, mla_w_ukv, w_out, ln_mix_g, ln_mix_b, ffn_w_up, ffn_conv_w, ffn_conv_b, ffn_w_down, ln_ffn_g, ln_ffn_b, loss_target, m_ln_in_g, m_ln_in_b, m_w_in, m_conv_a_w, m_conv_a_b, m_ln_a_g, m_ln_a_b, m_qk_norm_q, m_qk_norm_k, m_sgu_ln_g, m_sgu_ln_b, m_sgu_w, m_sgu_b, m_mla_q_norm, m_mla_w_uq, m_mla_kv_norm, m_mla_w_ukv, m_w_out, m_ln_mix_g, m_ln_mix_b, m_ffn_w_up, m_ffn_conv_w, m_ffn_conv_b, m_ffn_w_down, m_ln_ffn_g, m_ln_ffn_b, v_ln_in_g, v_ln_in_b, v_w_in, v_conv_a_w, v_conv_a_b, v_ln_a_g, v_ln_a_b, v_qk_norm_q, v_qk_norm_k, v_sgu_ln_g, v_sgu_ln_b, v_sgu_w, v_sgu_b, v_mla_q_norm, v_mla_w_uq, v_mla_kv_norm, v_mla_w_ukv, v_w_out, v_ln_mix_g, v_ln_mix_b, v_ffn_w_up, v_ffn_conv_w, v_ffn_conv_b, v_ffn_w_down, v_ln_ffn_g, v_ln_ffn_b):
    local = dict(ln_in_g=ln_in_g, ln_in_b=ln_in_b, w_in=w_in, conv_a_w=conv_a_w, conv_a_b=conv_a_b, ln_a_g=ln_a_g, ln_a_b=ln_a_b, qk_norm_q=qk_norm_q, qk_norm_k=qk_norm_k, sgu_ln_g=sgu_ln_g, sgu_ln_b=sgu_ln_b, sgu_w=sgu_w, sgu_b=sgu_b, mla_q_norm=mla_q_norm, mla_w_uq=mla_w_uq, mla_kv_norm=mla_kv_norm, mla_w_ukv=mla_w_ukv, w_out=w_out, ln_mix_g=ln_mix_g, ln_mix_b=ln_mix_b, ffn_w_up=ffn_w_up, ffn_conv_w=ffn_conv_w, ffn_conv_b=ffn_conv_b, ffn_w_down=ffn_w_down, ln_ffn_g=ln_ffn_g, ln_ffn_b=ln_ffn_b)
    mom = dict(ln_in_g=m_ln_in_g, ln_in_b=m_ln_in_b, w_in=m_w_in, conv_a_w=m_conv_a_w, conv_a_b=m_conv_a_b, ln_a_g=m_ln_a_g, ln_a_b=m_ln_a_b, qk_norm_q=m_qk_norm_q, qk_norm_k=m_qk_norm_k, sgu_ln_g=m_sgu_ln_g, sgu_ln_b=m_sgu_ln_b, sgu_w=m_sgu_w, sgu_b=m_sgu_b, mla_q_norm=m_mla_q_norm, mla_w_uq=m_mla_w_uq, mla_kv_norm=m_mla_kv_norm, mla_w_ukv=m_mla_w_ukv, w_out=m_w_out, ln_mix_g=m_ln_mix_g, ln_mix_b=m_ln_mix_b, ffn_w_up=m_ffn_w_up, ffn_conv_w=m_ffn_conv_w, ffn_conv_b=m_ffn_conv_b, ffn_w_down=m_ffn_w_down, ln_ffn_g=m_ln_ffn_g, ln_ffn_b=m_ln_ffn_b)
    var = dict(ln_in_g=v_ln_in_g, ln_in_b=v_ln_in_b, w_in=v_w_in, conv_a_w=v_conv_a_w, conv_a_b=v_conv_a_b, ln_a_g=v_ln_a_g, ln_a_b=v_ln_a_b, qk_norm_q=v_qk_norm_q, qk_norm_k=v_qk_norm_k, sgu_ln_g=v_sgu_ln_g, sgu_ln_b=v_sgu_ln_b, sgu_w=v_sgu_w, sgu_b=v_sgu_b, mla_q_norm=v_mla_q_norm, mla_w_uq=v_mla_w_uq, mla_kv_norm=v_mla_kv_norm, mla_w_ukv=v_mla_w_ukv, w_out=v_w_out, ln_mix_g=v_ln_mix_g, ln_mix_b=v_ln_mix_b, ffn_w_up=v_ffn_w_up, ffn_conv_w=v_ffn_conv_w, ffn_conv_b=v_ffn_conv_b, ffn_w_down=v_ffn_w_down, ln_ffn_g=v_ln_ffn_g, ln_ffn_b=v_ln_ffn_b)

    me = 4 * lax.axis_index("x") + 2 * lax.axis_index("y") + lax.axis_index("c")

    def own_slot(slots, block):
        return lax.dynamic_update_slice(slots, block[None], (me,) + (0,) * block.ndim)

    small_sharded = [n for n in SHARDED if n not in MATMUL_WEIGHTS]
    big_order = [(n, l) for l in range(DEPTH) for n in MATMUL_WEIGHTS]
    srcs = [local['w_in'][0].astype(COMM_DTYPE)] + [local[n] for n in small_sharded]
    srcs += [local[n][l].astype(COMM_DTYPE) for (n, l) in big_order[1:]]
    n_first = 1 + len(small_sharded)
    groups = [list(range(n_first))] + [[n_first + j] for j in range(len(big_order) - 1)]
    g_sems, g_srcs, g_lands, tok0 = exchange_start(srcs, [True] * len(srcs), groups, "gather_start")
    tok0 = tok0[0, 0]
    pending = {key: gi for gi, key in enumerate(big_order)}

    def finish(gi, after):
        idx = groups[gi]
        _, lands = exchange_wait(g_sems[gi], [g_srcs[t] for t in idx], [g_lands[t] for t in idx], [True] * len(idx),
                                 after, f"gather_wait{gi}")
        return [own_slot(ld, srcs[t]) for ld, t in zip(lands, idx)]

    first = []

    def get_wts(after):
        first.extend(finish(0, after))
        wts = {n: local[n] for n in REPLICATED}
        for n, slots in zip(small_sharded, first[1:]):
            wts[n] = _unshard(slots, SHARDED[n])
        return wts

    def unshard_layer(slots, n):
        if SHARDED[n] == 1:
            return slots.reshape(-1, slots.shape[2])
        return slots.transpose(1, 0, 2).reshape(slots.shape[1], -1)

    def mat(l, n, after):
        gi = pending[(n, l)]
        slots = first[0] if gi == 0 else finish(gi, after)[0]
        w = unshard_layer(slots, n).astype(MXU_DTYPE)
        return _pad_w_in(w) if n == 'w_in' else w

    started = []

    def hook(key, grads):
        tensors = []
        for (n, l), g in grads.items():
            if l is None:
                tensors.append(((n, l), _shard_slots(g, SHARDED[n])))
            else:
                tensors.append(((n, l), _shard_slots(g[None], SHARDED[n])[:, 0].astype(COMM_DTYPE)))
        sems, s_srcs, s_lands, tok = exchange_start([a for _, a in tensors], [False] * len(tensors),
                                                    [list(range(len(tensors)))], "scatter_start_" + key)
        started.append((key, [k for k, _ in tensors], sems[0], s_srcs, s_lands))
        return tok[0, 0]

    loss, dx, grads = local_step(x[0], loss_target[0], (local['ln_in_g'] + tok0, local['ln_in_b']), get_wts, mat, hook)

    as2d = lambda a: a.reshape(-1, a.shape[-1]) if a.ndim > 1 else a.reshape(1, -1)
    small_g = [as2d(grads[n]) for n in REPLICATED] + [jnp.broadcast_to(loss, (8, 128))]
    p_sems, p_srcs, p_lands, p_tok = exchange_start(small_g, [True] * len(small_g), [list(range(len(small_g)))],
                                                    "gather_small_start")

    parts, res = {}, {}

    def finish_scatter(entries, after):
        for key, keys, sems, s_srcs, s_lands in entries:
            s_out, lands = exchange_wait(sems, s_srcs, s_lands, [False] * len(keys), after, "scatter_wait_" + key)
            for k, so, ld in zip(keys, s_out, lands):
                parts[k] = own_slot(ld, lax.dynamic_index_in_dim(so, me, 0, keepdims=False))

    def update(names_):
        for n in names_:
            p = [parts[(n, l)] for l in range(DEPTH)] if n in MATMUL_WEIGHTS else parts[(n, None)]
            res[n] = adamw(p, local[n], mom[n], var[n], "adamw_" + n)

    early = ('ffn_w_up', 'ffn_w_down', 'w_out')
    finish_scatter([e for e in started if e[0] != "last"], p_tok)
    update(early)
    finish_scatter([e for e in started if e[0] == "last"], res[early[-1]][1])
    update([n for n in SHARDED if n not in early])
    updated = jnp.zeros((8, 128), F32) + sum(res[n][1][0, 0, 0] for n in SHARDED)
    p_own, p_lands = exchange_wait(p_sems[0], p_srcs, p_lands, [True] * len(small_g), updated, "gather_small_wait")
    small, loss_sum = adamw_replicated(p_lands[:-1], p_own[:-1], [as2d(local[n]) for n in REPLICATED],
                                       [as2d(mom[n]) for n in REPLICATED], [as2d(var[n]) for n in REPLICATED],
                                       p_lands[-1], p_own[-1])
    for n, quad in zip(REPLICATED, small):
        res[n] = tuple(a.reshape(local[n].shape) for a in quad)
    loss_total = loss_sum[0, 0]

    return (loss_total, dx[None], *[res[n][0] for n in WEIGHTS], *[res[n][1] for n in WEIGHTS],
            *[res[n][2] for n in WEIGHTS], *[res[n][3] for n in WEIGHTS])
```

```python
import functools
import math

import jax
import jax.numpy as jnp
from jax import lax
from jax.experimental import pallas as pl
from jax.experimental.pallas import tpu as pltpu

F32 = jnp.float32
MXU_DTYPE = jnp.bfloat16
COMM_DTYPE = jnp.bfloat16

N_DEV = 8
D_MODEL = 1024
DEPTH = 2
GRID_W = 64
GROUP_W = 256
HEAD_DIM = 64
CONV_A_WIDTH = 31
CONV_A_HALO = 16
GQA_HEADS = 4
GQA_KV_HEADS = 2
CHUNK = 128
SGU_GROUPS = 4
MLA_HEADS = 4
MLA_Q_LORA = 192
MLA_KV_LORA = 128
MLA_NOPE = 64
MLA_ROPE = 32
MLA_V = 64
MLA_DK_PAD = 128
ROPE_THETA = 10000.0
D_FF = 2816
DEEPNORM_ALPHA = (2 * DEPTH) ** 0.25
LN_EPS = 1e-5
RMS_EPS = 1e-6
D_IN_PROJ = 1888

ADAM_LR = 0.001
ADAM_B1 = 0.9
ADAM_B2 = 0.999
ADAM_EPS = 1e-08
ADAM_WD = 0.01
ADAM_STEP = 10

WEIGHTS = ['ln_in_g', 'ln_in_b', 'w_in', 'conv_a_w', 'conv_a_b', 'ln_a_g', 'ln_a_b', 'qk_norm_q', 'qk_norm_k',
           'sgu_ln_g', 'sgu_ln_b', 'sgu_w', 'sgu_b', 'mla_q_norm', 'mla_w_uq', 'mla_kv_norm', 'mla_w_ukv', 'w_out',
           'ln_mix_g', 'ln_mix_b', 'ffn_w_up', 'ffn_conv_w', 'ffn_conv_b', 'ffn_w_down', 'ln_ffn_g', 'ln_ffn_b']
SHARDED = {'w_in': 2, 'conv_a_w': 2, 'mla_w_uq': 2, 'mla_w_ukv': 2, 'w_out': 1, 'ffn_w_up': 2, 'ffn_conv_w': 2,
           'ffn_w_down': 1}
MATMUL_WEIGHTS = ('w_in', 'w_out', 'ffn_w_up', 'ffn_w_down')
TRANSPOSED = ('ffn_w_up',)
REPLICATED = [n for n in WEIGHTS if n not in SHARDED]

ROW_TILE = 256
VMEM_LIMIT = 56 * 1024 * 1024


def _rawdot(a, b, ca, cb):
    return lax.dot_general(a.astype(MXU_DTYPE), b.astype(MXU_DTYPE), (((ca,), (cb,)), ((), ())),
                           preferred_element_type=F32)


@jax.custom_vjp
def mm_nn(a, b):
    return _rawdot(a, b, 1, 0)


def _mm_nn_fwd(a, b):
    return _rawdot(a, b, 1, 0), (a, b)


def _mm_nn_bwd(res, dy):
    a, b = res
    return _rawdot(dy, b, 1, 1), _rawdot(a, dy, 0, 0)


mm_nn.defvjp(_mm_nn_fwd, _mm_nn_bwd)


@jax.custom_vjp
def mm_nt(a, b):
    return _rawdot(a, b, 1, 1)


def _mm_nt_fwd(a, b):
    return _rawdot(a, b, 1, 1), (a, b)


def _mm_nt_bwd(res, dy):
    a, b = res
    return _rawdot(dy, b, 1, 0), _rawdot(dy, a, 0, 0)


mm_nt.defvjp(_mm_nt_fwd, _mm_nt_bwd)


def _pick_tile(d, cands):
    for c in cands:
        if d % c == 0:
            return c
    return d


def matmul(a, b, mode, out_dtype, name):
    if mode == 'nn':
        (m, k), (k2, n) = a.shape, b.shape
    elif mode == 'nt':
        (m, k), (n, k2) = a.shape, b.shape
    else:
        (k, m), (k2, n) = a.shape, b.shape
    assert k == k2, (a.shape, b.shape, mode)
    tm = _pick_tile(m, (1024, 1408, 512, 256, 128))
    tn = _pick_tile(n, (512, 1408, 256, 128))
    tk = _pick_tile(k, (2816, 2048, 1024, 512, 256, 128))
    nk = k // tk
    ca = 0 if mode == 'tn' else 1
    cb = 1 if mode == 'nt' else 0
    a_spec = pl.BlockSpec((tk, tm), lambda i, j, kk: (kk, i)) if mode == 'tn' else pl.BlockSpec((tm, tk), lambda i, j, kk: (i, kk))
    b_spec = pl.BlockSpec((tn, tk), lambda i, j, kk: (j, kk)) if mode == 'nt' else pl.BlockSpec((tk, tn), lambda i, j, kk: (kk, j))

    def body(a_ref, b_ref, o_ref, acc_ref):
        kk = pl.program_id(2)

        @pl.when(kk == 0)
        def _():
            acc_ref[...] = jnp.zeros_like(acc_ref)

        acc_ref[...] += _rawdot(a_ref[...], b_ref[...], ca, cb)

        @pl.when(kk == nk - 1)
        def _():
            o_ref[...] = acc_ref[...].astype(o_ref.dtype)

    return pl.pallas_call(
        body, grid=(m // tm, n // tn, nk), in_specs=[a_spec, b_spec],
        out_specs=pl.BlockSpec((tm, tn), lambda i, j, kk: (i, j)),
        out_shape=jax.ShapeDtypeStruct((m, n), out_dtype),
        scratch_shapes=[pltpu.VMEM((tm, tn), F32)],
        compiler_params=pltpu.CompilerParams(dimension_semantics=("parallel", "parallel", "arbitrary"),
                                             vmem_limit_bytes=VMEM_LIMIT),
        name=name)(a, b)


class Op:
    def __init__(self, arr, block, imap, grad=False, acc=False, first=None, gdtype=F32, gshape=None, gimap=None):
        self.arr, self.block, self.imap = arr, block, imap
        self.grad, self.acc, self.first, self.gdtype = grad, acc, first, gdtype
        self.gshape = arr.shape if gshape is None else gshape
        self.gimap = imap if gimap is None else gimap


def _row_op(arr, tm, grad=False, gdtype=F32):
    return Op(arr, (tm, arr.shape[1]), lambda i: (i, 0), grad=grad, gdtype=gdtype)


def _par_op(arr, grad=False):
    nd = arr.ndim
    return Op(arr, arr.shape, lambda i: (0,) * nd, grad=grad, acc=True, first=lambda ids: ids[0] == 0)


def _load(ref):
    v = ref[...]
    return v.astype(F32) if jnp.issubdtype(v.dtype, jnp.floating) else v


def _store_heads(ref, val):
    if len(ref.shape) == 2:
        ref[...] = val.astype(ref.dtype)
    else:
        d = ref.shape[2]
        for h in range(ref.shape[0]):
            ref[h] = val[:, d * h:d * (h + 1)].astype(ref.dtype)


def _load_heads(ref):
    if len(ref.shape) == 2:
        return ref[...].astype(F32)
    return jnp.concatenate([ref[h].astype(F32) for h in range(ref.shape[0])], axis=-1)


def stage_fwd(name, fn, ops, outs, grid):
    n_in = len(ops)

    def body(*refs):
        res = fn(*[_load(r) for r in refs[:n_in]])
        for r, o in zip(refs[n_in:], res):
            _store_heads(r, o)

    return pl.pallas_call(
        body, grid=grid, in_specs=[pl.BlockSpec(o.block, o.imap) for o in ops],
        out_specs=[pl.BlockSpec(b, im) for (_, _, b, im) in outs],
        out_shape=[jax.ShapeDtypeStruct(s, d) for (s, d, _, _) in outs],
        compiler_params=pltpu.CompilerParams(dimension_semantics=("parallel",) * len(grid),
                                             vmem_limit_bytes=VMEM_LIMIT),
        name=name)(*[o.arr for o in ops])


def stage_bwd(name, fn, ops, cts, grid, value_acc=False):
    n_in = len(ops)
    ct_flat = [c for group in cts if group is not None for c in group]
    n_ct = len(ct_flat)
    diff = [i for i, o in enumerate(ops) if o.grad]
    any_acc = value_acc or any(ops[i].acc for i in diff)
    ngrid = len(grid)

    def body(*refs):
        ids = [pl.program_id(a) for a in range(ngrid)]
        vals = [_load(r) for r in refs[:n_in]]
        ct_refs = refs[n_in:n_in + n_ct]
        out_refs = refs[n_in + n_ct:]

        def f(*dv):
            full = list(vals)
            for i, v in zip(diff, dv):
                full[i] = v
            return tuple(fn(*full))

        res, vjp = jax.vjp(f, *[vals[i] for i in diff])
        ct, pos = [], 0
        for group, r in zip(cts, res):
            if group is None:
                ct.append(jnp.ones_like(r))
            else:
                tot = None
                for _ in group:
                    c = _load_heads(ct_refs[pos])
                    tot = c if tot is None else tot + c
                    pos += 1
                ct.append(tot)
        grads = vjp(tuple(ct))
        for i, g, r in zip(diff, grads, out_refs):
            if ops[i].acc:
                @pl.when(ops[i].first(ids))
                def _(r=r):
                    r[...] = jnp.zeros_like(r)

                r[...] += g.astype(r.dtype)
            else:
                r[...] = g.astype(r.dtype)
        if value_acc:
            r = out_refs[len(diff)]

            @pl.when(ids[0] == 0)
            def _():
                r[...] = jnp.zeros_like(r)

            r[...] += res[0]

    in_specs = [pl.BlockSpec(o.block, o.imap) for o in ops] + [pl.BlockSpec(b, im) for (_, b, im) in ct_flat]
    out_specs = [pl.BlockSpec(ops[i].block, ops[i].gimap) for i in diff]
    out_shape = [jax.ShapeDtypeStruct(ops[i].gshape, ops[i].gdtype) for i in diff]
    if value_acc:
        out_specs.append(pl.BlockSpec((1, 1), lambda *ids: (0, 0)))
        out_shape.append(jax.ShapeDtypeStruct((1, 1), F32))
    sem = ("arbitrary",) * ngrid if any_acc else ("parallel",) * ngrid
    return pl.pallas_call(
        body, grid=grid, in_specs=in_specs, out_specs=out_specs, out_shape=out_shape,
        compiler_params=pltpu.CompilerParams(dimension_semantics=sem, vmem_limit_bytes=VMEM_LIMIT),
        name=name)(*[o.arr for o in ops], *[a for (a, _, _) in ct_flat])


def _sigmoid(x):
    return 1.0 / (1.0 + jnp.exp(-x))


def _silu(x):
    return x * _sigmoid(x)


def _gelu_tanh(x):
    return 0.5 * x * (1.0 + jnp.tanh(math.sqrt(2.0 / math.pi) * (x + 0.044715 * (x * x * x))))


def _ln(x, g, b):
    mu = jnp.mean(x, axis=-1, keepdims=True)
    xc = x - mu
    var = jnp.mean(xc * xc, axis=-1, keepdims=True)
    return xc * lax.rsqrt(var + LN_EPS) * g + b


def _rms(x, g):
    ms = jnp.mean(x * x, axis=-1, keepdims=True)
    return x * lax.rsqrt(ms + RMS_EPS) * g


def _swap_halves(x, half):
    width = x.shape[-1]
    lane = lax.broadcasted_iota(jnp.int32, x.shape, 1)
    return jnp.where(lane % (2 * half) < half, pltpu.roll(x, width - half, 1), pltpu.roll(x, half, 1))


def _make_swap(half):
    @jax.custom_vjp
    def swap(x):
        return _swap_halves(x, half)

    swap.defvjp(lambda x: (_swap_halves(x, half), None), lambda _, dy: (_swap_halves(dy, half),))
    return swap


_swap16, _swap8 = _make_swap(16), _make_swap(8)


def _rope(x, cos, sin_signed, swap):
    return x * cos + swap(x) * sin_signed


def _dot_f32(a, b):
    return jnp.dot(a, b, preferred_element_type=F32, precision=lax.Precision.HIGHEST)


def fn_ln(x, g, b):
    return (_ln(x, g, b),)


def _twice(fn):
    def f(*a):
        (y,) = fn(*a)
        return y, y
    return f


PROJ_W = 2048
P_A, P_Q, P_K, P_V, P_C, P_CQ, P_CKV, P_KR = 0, 512, 768, 896, 1024, 1536, 1792, 1920
CQ_PAD = 256
_CQ_END = P_CQ + MLA_Q_LORA


def _pad_w_in(w):
    z = lambda n: jnp.zeros((w.shape[0], n), w.dtype)
    return jnp.concatenate([w[:, :_CQ_END], z(P_CKV - _CQ_END), w[:, _CQ_END:], z(PROJ_W - P_KR - MLA_ROPE)], axis=1)


def _unpad_w_in(g):
    return jnp.concatenate([g[:, :_CQ_END], g[:, P_CKV:P_KR + MLA_ROPE]], axis=1)


def fn_pre(proj, tab_q, tab_d, seg, place, qng, kng, sg, sb, sw, sbt, mqn, wuq, mkvn, wukv):
    tm = proj.shape[0]
    aglu = proj[:, P_A:P_A + GROUP_W] * _sigmoid(proj[:, P_A + GROUP_W:P_Q])
    b_q, b_k, b_v = proj[:, P_Q:P_K], proj[:, P_K:P_V], proj[:, P_V:P_C]
    cos_q, sin_q = tab_q[:, :GROUP_W], tab_q[:, GROUP_W:]
    q = b_q * lax.rsqrt(_dot_f32(b_q * b_q, seg) + RMS_EPS) * qng
    q = _rope(q, cos_q, sin_q, _swap16)
    k = b_k * lax.rsqrt(_dot_f32(b_k * b_k, seg[:128, :128]) + RMS_EPS) * kng
    k = _rope(k, cos_q[:, :128], sin_q[:, :128], _swap16)
    c = _gelu_tanh(proj[:, P_C:P_CQ])
    u, sv = c[:, :GROUP_W], _ln(c[:, GROUP_W:], sg, sb)
    group = lax.broadcasted_iota(jnp.int32, (CHUNK, GROUP_W), 1) // HEAD_DIM
    rows = []
    for n in range(tm // CHUNK):
        svn = sv[CHUNK * n:CHUNK * (n + 1)]
        acc = jnp.zeros((CHUNK, GROUP_W), F32)
        for g in range(SGU_GROUPS):
            acc = acc + jnp.where(group == g, mm_nn(sw[CHUNK * g:CHUNK * (g + 1)], svn) + sbt[:, g:g + 1], 0.0)
        rows.append(acc)
    o_c = u * jnp.concatenate(rows, axis=0)
    d_cq, d_ckv, d_kr = proj[:, P_CQ:P_CKV], proj[:, P_CKV:P_KR], proj[:, P_KR:PROJ_W]
    cqn = d_cq * lax.rsqrt(jnp.sum(d_cq * d_cq, axis=-1, keepdims=True) * (1.0 / MLA_Q_LORA) + RMS_EPS) * mqn
    cos_d = jnp.concatenate([tab_d[:, :MLA_DK_PAD]] * MLA_HEADS, axis=-1)
    sin_d = jnp.concatenate([tab_d[:, MLA_DK_PAD:]] * MLA_HEADS, axis=-1)
    qf = _rope(mm_nn(cqn, wuq), cos_d, sin_d, _swap8)
    kvd = mm_nn(_rms(d_ckv, mkvn), wukv)
    kf = _rope(kvd[:, :MLA_HEADS * MLA_DK_PAD] + _dot_f32(d_kr, place), cos_d, sin_d, _swap8)
    return aglu, q, k, b_v, o_c, qf, kf, kvd[:, MLA_HEADS * MLA_DK_PAD:]


def fn_aconv(win, w, b, g, beta):
    tm = win.shape[0] - 2 * CONV_A_HALO
    off = CONV_A_HALO - CONV_A_WIDTH // 2
    rolled = [win] + [_roll_rows(win, -r) for r in range(1, 8)]
    acc = None
    for kk in range(CONV_A_WIDTH):
        r = (off + kk) % 8
        base = off + kk - r
        term = rolled[r][base:base + tm] * w[kk:kk + 1, :]
        acc = term if acc is None else acc + term
    return (_silu(_ln(acc + b, g, beta)),)


def fn_resln(h, r, g, b):
    return (_ln(DEEPNORM_ALPHA * h + r, g, b),)


@functools.partial(jax.custom_vjp, nondiff_argnums=(1,))
def _roll_rows(x, shift):
    return pltpu.roll(x, shift % x.shape[0], 0)


_roll_rows.defvjp(lambda x, shift: (pltpu.roll(x, shift % x.shape[0], 0), None),
                  lambda shift, _, dy: (pltpu.roll(dy, (-shift) % dy.shape[0], 0),))


def _shift_down(x):
    row = lax.broadcasted_iota(jnp.int32, x.shape, 0)
    return jnp.where(row == 0, 0.0, _roll_rows(x, 1))


def _shift_up(x):
    row = lax.broadcasted_iota(jnp.int32, x.shape, 0)
    return jnp.where(row == x.shape[0] - 1, 0.0, _roll_rows(x, -1))


def fn_ffnconv(u1, u2, w1, w2, b1, b2):
    c1 = _shift_down(u1) * w1[0:1] + u1 * w1[1:2] + _shift_up(u1) * w1[2:3] + b1
    c2 = _shift_down(u2) * w2[0:1] + u2 * w2[1:2] + _shift_up(u2) * w2[2:3] + b2
    return (_silu(c1) * c2,)


def fn_final(h, r, t, g, b):
    y = _ln(DEEPNORM_ALPHA * h + r, g, b)
    err = (y - t) * (y - t)
    return (0.5 * jnp.sum(jnp.mean(err, axis=-1, keepdims=True), axis=0, keepdims=True),)


def _rope_tables(seq):
    n_rows = seq // GRID_W
    lane128 = jnp.arange(128)

    def tile_tables(j, rotated, half):
        inv = ROPE_THETA ** (-(j % half).astype(F32) / half)
        by_row, by_col = rotated & (j < 2 * half), rotated & (j >= 2 * half)
        sign = jnp.where(j % (2 * half) < half, -1.0, 1.0)
        ar = jnp.arange(n_rows, dtype=F32)[:, None] * inv[None, :]
        ac = jnp.arange(GRID_W, dtype=F32)[:, None] * inv[None, :]
        grid = lambda r, c: (jnp.where(by_row, r, 0.0)[:, None, :] + jnp.where(by_col, c, 0.0)[None, :, :])
        cos = grid(jnp.cos(ar), jnp.cos(ac)) + jnp.where(rotated, 0.0, 1.0)
        sin = grid(sign * jnp.sin(ar), sign * jnp.sin(ac))
        return cos.reshape(seq, 128), sin.reshape(seq, 128)

    cos_b, sin_b = tile_tables(lane128 % HEAD_DIM, lane128 >= 0, HEAD_DIM // 4)
    tab_q = jnp.concatenate([cos_b] * (GROUP_W // 128) + [sin_b] * (GROUP_W // 128), axis=-1)
    tab_d = jnp.concatenate(tile_tables(lane128 - MLA_NOPE, (lane128 >= MLA_NOPE) & (lane128 < MLA_NOPE + MLA_ROPE),
                                        MLA_ROPE // 4), axis=-1)
    lane = jnp.arange(GROUP_W)
    seg = jnp.where(lane[:, None] // HEAD_DIM == lane[None, :] // HEAD_DIM, 1.0 / HEAD_DIM, 0.0).astype(F32)
    src, dst = jnp.arange(128)[:, None], jnp.arange(MLA_HEADS * MLA_DK_PAD)[None, :]
    place = jnp.where((src < MLA_ROPE) & (dst % MLA_DK_PAD == MLA_NOPE + src), 1.0, 0.0).astype(F32)
    return tab_q, tab_d, seg, place


def _pre_ops(proj, tabs, kp, grad):
    tm = ROW_TILE
    ops = [_row_op(proj, tm, grad=grad, gdtype=MXU_DTYPE), _row_op(tabs[0], tm), _row_op(tabs[1], tm),
           _par_op(tabs[2]), _par_op(tabs[3])]
    ops += [_par_op(kp[n], grad=grad) for n in ('qng', 'kng', 'sg', 'sb', 'sw', 'sbt', 'mqn', 'wuq', 'mkvn', 'wukv')]
    return ops


PRE_OUTS = ((0, GROUP_W), (GQA_HEADS, HEAD_DIM), (GQA_KV_HEADS, HEAD_DIM), (GQA_KV_HEADS, HEAD_DIM), (0, GROUP_W),
            (MLA_HEADS, MLA_DK_PAD), (MLA_HEADS, MLA_DK_PAD), (MLA_HEADS, MLA_V))


def _pre_out_specs(seq, tm):
    specs = []
    for heads, w in PRE_OUTS:
        if heads:
            specs.append(((heads, seq, w), (heads, tm, w), lambda i: (0, i, 0)))
        else:
            specs.append(((seq, w), (tm, w), lambda i: (i, 0)))
    return specs


def pre_fwd(proj, tabs, kp, tag):
    seq = proj.shape[0]
    tm = ROW_TILE
    dts = (F32,) + (MXU_DTYPE,) * 7
    outs = [(shape, dt, block, imap) for (shape, block, imap), dt in zip(_pre_out_specs(seq, tm), dts)]
    return stage_fwd("pre_fwd" + tag, fn_pre, _pre_ops(proj, tabs, kp, False), outs, (seq // tm,))


def pre_bwd(proj, tabs, kp, cts, tag):
    seq = proj.shape[0]
    tm = ROW_TILE
    ct = [[c if isinstance(c, tuple) else (c, block, imap)] for c, (_, block, imap) in zip(cts, _pre_out_specs(seq, tm))]
    return stage_bwd("pre_bwd" + tag, fn_pre, _pre_ops(proj, tabs, kp, True), ct, (seq // tm,))


def _aconv_ops(kp, grad):
    return [_par_op(kp[n], grad=grad) for n in ('caw', 'cab', 'lag', 'lab')]


def aconv_fwd(aglu_pad, kp, tag):
    seq = aglu_pad.shape[0] - 2 * CONV_A_HALO
    tm = ROW_TILE
    n_par = 4

    def body(x_ref, *refs):
        i = pl.program_id(0)
        win = x_ref[pl.ds(pl.multiple_of(i * tm, tm), tm + 2 * CONV_A_HALO), :]
        (o,) = fn_aconv(win, *[_load(r) for r in refs[:n_par]])
        refs[n_par][...] = o.astype(refs[n_par].dtype)

    pars = _aconv_ops(kp, False)
    return pl.pallas_call(
        body, grid=(seq // tm,),
        in_specs=[pl.BlockSpec(aglu_pad.shape, lambda i: (0, 0))] + [pl.BlockSpec(o.block, o.imap) for o in pars],
        out_specs=pl.BlockSpec((tm, GROUP_W), lambda i: (i, 0)),
        out_shape=jax.ShapeDtypeStruct((seq, GROUP_W), MXU_DTYPE),
        compiler_params=pltpu.CompilerParams(dimension_semantics=("parallel",), vmem_limit_bytes=VMEM_LIMIT),
        name="aconv_fwd" + tag)(aglu_pad, *[o.arr for o in pars])


def aconv_bwd(aglu_pad, kp, d_oa, tag):
    seq = aglu_pad.shape[0] - 2 * CONV_A_HALO
    tm = ROW_TILE
    n_par = 4

    def body(x_ref, *refs):
        i = pl.program_id(0)
        rows = pl.ds(pl.multiple_of(i * tm, tm), tm + 2 * CONV_A_HALO)
        pars = [_load(r) for r in refs[:n_par]]
        ct = refs[n_par][...].astype(F32)
        outs = refs[n_par + 1:]
        _, vjp = jax.vjp(lambda *a: fn_aconv(*a), x_ref[rows, :], *pars)
        grads = vjp((ct,))

        @pl.when(i == 0)
        def _():
            for r in outs:
                r[...] = jnp.zeros_like(r)

        outs[0][rows, :] += grads[0]
        for r, g in zip(outs[1:], grads[1:]):
            r[...] += g

    pars = _aconv_ops(kp, True)
    whole = pl.BlockSpec(aglu_pad.shape, lambda i: (0, 0))
    par_specs = [pl.BlockSpec(o.block, o.imap) for o in pars]
    return pl.pallas_call(
        body, grid=(seq // tm,),
        in_specs=[whole] + par_specs + [pl.BlockSpec((tm, GROUP_W), lambda i: (i, 0))],
        out_specs=[whole] + par_specs,
        out_shape=[jax.ShapeDtypeStruct(aglu_pad.shape, F32)] + [jax.ShapeDtypeStruct(o.arr.shape, F32) for o in pars],
        compiler_params=pltpu.CompilerParams(dimension_semantics=("arbitrary",), vmem_limit_bytes=VMEM_LIMIT),
        name="aconv_bwd" + tag)(aglu_pad, *[o.arr for o in pars], d_oa)


ATTN_TQ = 256
ATTN_TK = 512


def attn_fwd(q3, k3, v3, scale, tag):
    heads, seq, dk = q3.shape
    group = heads // k3.shape[0]
    kv_per_pair = 2 // group
    dv = v3.shape[2]
    tq, tk = min(ATTN_TQ, seq), min(ATTN_TK, seq)
    n_chunks = seq // tk
    log2e = math.log2(math.e)

    def one_head(q, k_ref, v_ref):
        scores = lambda c: _rawdot(q, k_ref[pl.ds(c * tk, tk), :], 1, 1)
        m, l, acc = jnp.full((tq, 1), -jnp.inf, F32), jnp.zeros((tq, 1), F32), jnp.zeros((tq, dv), F32)
        s_next = scores(0)
        for c in range(n_chunks):
            s_cur, s_next = s_next, (scores(c + 1) if c + 1 < n_chunks else None)
            t = s_cur * (scale * log2e)
            m_new = jnp.maximum(m, jnp.max(t, axis=-1, keepdims=True))
            alpha = jnp.exp2(m - m_new)
            p = jnp.exp2(t - m_new)
            l = alpha * l + jnp.sum(p, axis=-1, keepdims=True)
            acc = alpha * acc + _rawdot(p, v_ref[pl.ds(c * tk, tk), :], 1, 0)
            m = m_new
        return acc * (1.0 / l), m * (1.0 / log2e) + jnp.log(l)

    def body(q_ref, k_ref, v_ref, o_ref, lse_ref):
        outs = []
        for h in range(2):
            o, lse = one_head(q_ref[h], k_ref.at[h // group], v_ref.at[h // group])
            lse_ref[h] = lse
            outs.append(o)
        o_ref[...] = jnp.concatenate(outs, axis=-1)

    return pl.pallas_call(
        body, grid=(heads // 2, seq // tq),
        in_specs=[pl.BlockSpec((2, tq, dk), lambda j, i: (j, i, 0)),
                  pl.BlockSpec((kv_per_pair, seq, dk), lambda j, i: (j, 0, 0)),
                  pl.BlockSpec((kv_per_pair, seq, dv), lambda j, i: (j, 0, 0))],
        out_specs=[pl.BlockSpec((tq, 2 * dv), lambda j, i: (i, j)),
                   pl.BlockSpec((2, tq, 1), lambda j, i: (j, i, 0))],
        out_shape=[jax.ShapeDtypeStruct((seq, heads * dv), F32), jax.ShapeDtypeStruct((heads, seq, 1), F32)],
        compiler_params=pltpu.CompilerParams(dimension_semantics=("parallel", "parallel"),
                                             vmem_limit_bytes=VMEM_LIMIT),
        name="attn_fwd" + tag)(q3, k3, v3)


def attn_bwd(q3, k3, v3, o, lse3, do_all, do_col, scale, tag):
    heads, seq, dk = q3.shape
    group = heads // k3.shape[0]
    kv_per_pair = 2 // group
    dv = v3.shape[2]
    tq, tk = min(ATTN_TQ, seq), min(ATTN_TK, seq)
    n_chunks = seq // tk
    log2e = math.log2(math.e)

    def one_head(q, do, o_h, lse, k_ref, v_ref, dk_ref, dv_ref):
        dob = do.astype(MXU_DTYPE)
        delta = jnp.sum(do * o_h, axis=-1, keepdims=True)
        lse2 = lse * log2e
        rows = lambda c: pl.ds(c * tk, tk)
        products = lambda c: (_rawdot(q, k_ref[rows(c), :], 1, 1), _rawdot(dob, v_ref[rows(c), :], 1, 1))
        dq = jnp.zeros((tq, dk), F32)
        nxt = products(0)
        for c in range(n_chunks):
            (s_cur, dp_cur), nxt = nxt, (products(c + 1) if c + 1 < n_chunks else None)
            p = jnp.exp2(s_cur * (scale * log2e) - lse2)
            ds = (p * ((dp_cur - delta) * scale)).astype(MXU_DTYPE)
            dv_ref[rows(c), :] += _rawdot(p, dob, 0, 0)
            dk_ref[rows(c), :] += _rawdot(ds, q, 0, 0)
            dq = dq + _rawdot(ds, k_ref[rows(c), :], 1, 0)
        return dq

    def body(q_ref, k_ref, v_ref, o_ref, lse_ref, do_ref, dq_ref, dk_ref, dv_ref):
        @pl.when(pl.program_id(1) == 0)
        def _():
            dk_ref[...] = jnp.zeros_like(dk_ref)
            dv_ref[...] = jnp.zeros_like(dv_ref)

        do_pair, o_pair = do_ref[...], o_ref[...]
        for h in range(2):
            kv = h // group
            dq_ref[h] = one_head(q_ref[h], do_pair[:, dv * h:dv * (h + 1)], o_pair[:, dv * h:dv * (h + 1)],
                                 lse_ref[h], k_ref.at[kv], v_ref.at[kv], dk_ref.at[kv], dv_ref.at[kv])

    qspec = lambda d: pl.BlockSpec((2, tq, d), lambda j, i: (j, i, 0))
    kvspec = lambda d: pl.BlockSpec((kv_per_pair, seq, d), lambda j, i: (j, 0, 0))
    return pl.pallas_call(
        body, grid=(heads // 2, seq // tq),
        in_specs=[qspec(dk), kvspec(dk), kvspec(dv), pl.BlockSpec((tq, 2 * dv), lambda j, i: (i, j)), qspec(1),
                  pl.BlockSpec((tq, 2 * dv), lambda j, i: (i, do_col + j))],
        out_specs=[qspec(dk), kvspec(dk), kvspec(dv)],
        out_shape=[jax.ShapeDtypeStruct(q3.shape, F32), jax.ShapeDtypeStruct(k3.shape, F32),
                   jax.ShapeDtypeStruct(v3.shape, F32)],
        compiler_params=pltpu.CompilerParams(dimension_semantics=("parallel", "arbitrary"),
                                             vmem_limit_bytes=VMEM_LIMIT),
        name="attn_bwd" + tag)(q3, k3, v3, o, lse3, do_all)


def resln_fwd(h, r, g, b, tag):
    seq, d = h.shape
    tm = ROW_TILE
    ops = [_row_op(h, tm), _row_op(r, tm), _par_op(g), _par_op(b)]
    outs = [((seq, d), dt, (tm, d), lambda i: (i, 0)) for dt in (F32, MXU_DTYPE)]
    return stage_fwd("resln_fwd" + tag, _twice(fn_resln), ops, outs, (seq // tm,))


def resln_bwd(h, r, g, b, dys, tag):
    seq, d = h.shape
    tm = ROW_TILE
    ops = [_row_op(h, tm, grad=True), _row_op(r, tm, grad=True, gdtype=MXU_DTYPE), _par_op(g, grad=True),
           _par_op(b, grad=True)]
    ct = [[(dy, (tm, d), lambda i: (i, 0)) for dy in dys]]
    return stage_bwd("resln_bwd" + tag, fn_resln, ops, ct, (seq // tm,))


def _ffnconv_ops(up1, up2, w, b, grad):
    seq = up1.shape[0]
    nblk = D_FF // 128
    lo, hi = (lambda j: (0, j)), (lambda j: (0, j + nblk))
    half = lambda a: dict(gshape=(a.shape[0], D_FF), gimap=lo)
    return [Op(up1, (seq, 128), lo, grad=grad, gdtype=MXU_DTYPE), Op(up2, (seq, 128), lo, grad=grad, gdtype=MXU_DTYPE),
            Op(w, (3, 128), lo, grad=grad, **half(w)), Op(w, (3, 128), hi, grad=grad, **half(w)),
            Op(b, (1, 128), lo, grad=grad, **half(b)), Op(b, (1, 128), hi, grad=grad, **half(b))]


def ffnconv_fwd(up1, up2, w, b, tag):
    seq = up1.shape[0]
    outs = [((seq, D_FF), MXU_DTYPE, (seq, 128), lambda j: (0, j))]
    return stage_fwd("ffnconv_fwd" + tag, fn_ffnconv, _ffnconv_ops(up1, up2, w, b, False), outs, (D_FF // 128,))[0]


def ffnconv_bwd(up1, up2, w, b, dact, tag):
    seq = up1.shape[0]
    ct = [[(dact, (seq, 128), lambda j: (0, j))]]
    du1, du2, dw1, dw2, db1, db2 = stage_bwd("ffnconv_bwd" + tag, fn_ffnconv, _ffnconv_ops(up1, up2, w, b, True),
                                             ct, (D_FF // 128,))
    cat = lambda a, b_: jnp.concatenate([a, b_], axis=-1)
    return du1, du2, cat(dw1, dw2), cat(db1, db2)


def final_bwd(h, r, t, g, b, tag):
    seq, d = h.shape
    tm = ROW_TILE
    ops = [_row_op(h, tm, grad=True), _row_op(r, tm, grad=True, gdtype=MXU_DTYPE), _row_op(t, tm),
           _par_op(g, grad=True), _par_op(b, grad=True)]
    return stage_bwd("final_bwd" + tag, fn_final, ops, [None], (seq // tm,), value_acc=True)


def _layer_params(wts, l):
    row = lambda a: a.reshape(1, -1)
    wuq = wts['mla_w_uq'][l].reshape(MLA_Q_LORA, MLA_HEADS, MLA_NOPE + MLA_ROPE)
    wuq = jnp.pad(wuq, ((0, CQ_PAD - MLA_Q_LORA), (0, 0), (0, MLA_DK_PAD - MLA_NOPE - MLA_ROPE)))
    wukv = wts['mla_w_ukv'][l].reshape(MLA_KV_LORA, MLA_HEADS, MLA_NOPE + MLA_V)
    wuk = jnp.pad(wukv[:, :, :MLA_NOPE], ((0, 0), (0, 0), (0, MLA_DK_PAD - MLA_NOPE)))
    return dict(
        qng=jnp.tile(row(wts['qk_norm_q'][l]), (1, GQA_HEADS)), kng=jnp.tile(row(wts['qk_norm_k'][l]), (1, GQA_KV_HEADS)),
        sg=row(wts['sgu_ln_g'][l]), sb=row(wts['sgu_ln_b'][l]),
        sw=wts['sgu_w'][l].reshape(SGU_GROUPS * CHUNK, CHUNK), sbt=wts['sgu_b'][l].T,
        mqn=jnp.pad(row(wts['mla_q_norm'][l]), ((0, 0), (0, CQ_PAD - MLA_Q_LORA))),
        wuq=wuq.reshape(CQ_PAD, MLA_HEADS * MLA_DK_PAD), mkvn=row(wts['mla_kv_norm'][l]),
        wukv=jnp.concatenate([wuk.reshape(MLA_KV_LORA, -1), wukv[:, :, MLA_NOPE:].reshape(MLA_KV_LORA, -1)], axis=1),
        caw=wts['conv_a_w'][l], cab=row(wts['conv_a_b'][l]), lag=row(wts['ln_a_g'][l]), lab=row(wts['ln_a_b'][l]),
        lmg=row(wts['ln_mix_g'][l]), lmb=row(wts['ln_mix_b'][l]),
        fcw=wts['ffn_conv_w'][l], fcb=row(wts['ffn_conv_b'][l]),
        lfg=row(wts['ln_ffn_g'][l]), lfb=row(wts['ln_ffn_b'][l]))


def _to_heads(a, heads):
    seq = a.shape[0]
    return a.reshape(seq, heads, -1).transpose(1, 0, 2)


def _from_heads(a3):
    return a3.transpose(1, 0, 2).reshape(a3.shape[1], -1)


def local_step(x, target, ln_in, get_wts, mat, hook):
    seq = x.shape[0]
    tm = ROW_TILE
    tabs = _rope_tables(seq)
    scale_b = HEAD_DIM ** -0.5
    scale_d = (MLA_NOPE + MLA_ROPE) ** -0.5
    ln_in_g, ln_in_b = ln_in[0].reshape(1, -1), ln_in[1].reshape(1, -1)

    h, h_m = stage_fwd("ln_in_fwd", _twice(fn_ln), [_row_op(x, tm), _par_op(ln_in_g), _par_op(ln_in_b)],
                       [((seq, D_MODEL), dt, (tm, D_MODEL), lambda i: (i, 0)) for dt in (F32, MXU_DTYPE)],
                       (seq // tm,))
    wts = get_wts(h_m)
    saved = []
    for l in range(DEPTH):
        tag = f"_l{l}"
        kp, unprep = jax.vjp(lambda w: _layer_params(w, l), wts)
        m = {'w_in': mat(l, 'w_in', h_m)}
        proj = matmul(h_m, m['w_in'], 'nn', F32, "mm_proj" + tag)
        aglu, q3, k3, v3, o_c, qd3, kd3, vd3 = pre_fwd(proj, tabs, kp, tag)
        aglu_pad = jnp.pad(aglu, ((CONV_A_HALO, CONV_A_HALO), (0, 0)))
        o_a = aconv_fwd(aglu_pad, kp, tag)
        o_b3, lse_b3 = attn_fwd(q3, k3, v3, scale_b, "_b" + tag)
        o_d3, lse_d3 = attn_fwd(qd3, kd3, vd3, scale_d, "_d" + tag)
        o_cat = jnp.concatenate([o_a, o_b3.astype(MXU_DTYPE), o_c, o_d3.astype(MXU_DTYPE)], axis=-1)
        m['w_out'] = mat(l, 'w_out', o_cat)
        mix = matmul(o_cat, m['w_out'], 'nn', F32, "mm_mix" + tag)
        h1, h1_m = resln_fwd(h, mix, kp['lmg'], kp['lmb'], "_mix" + tag)
        w_up_t = mat(l, 'ffn_w_up', h1_m)
        m['w_up1'], m['w_up2'] = w_up_t[:D_FF], w_up_t[D_FF:]
        up1 = matmul(h1_m, m['w_up1'], 'nt', F32, "mm_up1" + tag)
        up2 = matmul(h1_m, m['w_up2'], 'nt', F32, "mm_up2" + tag)
        act = ffnconv_fwd(up1, up2, kp['fcw'], kp['fcb'], tag)
        m['ffn_w_down'] = mat(l, 'ffn_w_down', act)
        f = matmul(act, m['ffn_w_down'], 'nn', F32, "mm_down" + tag)
        saved.append(dict(kp=kp, unprep=unprep, m=m, h=h, h_m=h_m, h1_m=h1_m, proj=proj, o_b3=o_b3, lse_b3=lse_b3,
                          o_d3=o_d3, lse_d3=lse_d3, aglu_pad=aglu_pad, q3=q3, k3=k3, v3=v3, qd3=qd3,
                          kd3=kd3, vd3=vd3, o_cat=o_cat, mix=mix, h1=h1, up1=up1, up2=up2, act=act, f=f))
        if l + 1 < DEPTH:
            h, h_m = resln_fwd(h1, f, kp['lfg'], kp['lfb'], "_ffn" + tag)

    after = lambda a, tok: a if tok is None else a + tok
    small_acc = None
    dh_parts = None
    loss = None
    tok = None
    g_mix = None
    for l in reversed(range(DEPTH)):
        tag = f"_l{l}"
        s = saved[l]
        kp, m = s['kp'], s['m']
        dkp = {}
        lfg = after(kp['lfg'], tok)
        if l == DEPTH - 1:
            dh1_a, df, dkp['lfg'], dkp['lfb'], loss = final_bwd(s['h1'], s['f'], target, lfg, kp['lfb'], tag)
        else:
            dh1_a, df, dkp['lfg'], dkp['lfb'] = resln_bwd(s['h1'], s['f'], lfg, kp['lfb'], dh_parts, "_ffn" + tag)
        g_down = matmul(s['act'], df, 'tn', COMM_DTYPE, "mm_gdown" + tag)
        dact = matmul(df, m['ffn_w_down'], 'nt', F32, "mm_dact" + tag)
        dup1, dup2, dkp['fcw'], dkp['fcb'] = ffnconv_bwd(s['up1'], s['up2'], kp['fcw'], kp['fcb'], dact, tag)
        dh1_b1 = matmul(dup1, m['w_up1'], 'nn', F32, "mm_dh1a" + tag)
        dh1_b2 = matmul(dup2, m['w_up2'], 'nn', F32, "mm_dh1b" + tag)
        g_up = jnp.concatenate([matmul(dup1, s['h1_m'], 'tn', COMM_DTYPE, "mm_gup1" + tag),
                                matmul(dup2, s['h1_m'], 'tn', COMM_DTYPE, "mm_gup2" + tag)], axis=0)
        tok = hook(f"ffn{l}", {('ffn_w_down', l): g_down, ('ffn_w_up', l): g_up})
        dh_a, dmix, dkp['lmg'], dkp['lmb'] = resln_bwd(s['h'], s['mix'], after(kp['lmg'], tok), kp['lmb'],
                                                       [dh1_a, dh1_b1, dh1_b2], "_mix" + tag)
        g_out = matmul(s['o_cat'], dmix, 'tn', COMM_DTYPE, "mm_gout" + tag)
        w_out = m['w_out']
        if l == 0:
            w_out = w_out + hook("out0", {('w_out', l): g_out}).astype(w_out.dtype)
        do_cat = matmul(dmix, w_out, 'nt', F32, "mm_docat" + tag)
        lse_b3 = s['lse_b3']
        do_c = (do_cat, (ROW_TILE, GROUP_W), lambda i: (i, 2))
        pair_w = 2 * HEAD_DIM
        dq3, dk3, dv3 = attn_bwd(s['q3'], s['k3'], s['v3'], s['o_b3'], lse_b3, do_cat, GROUP_W // pair_w,
                                 scale_b, "_b" + tag)
        dqd3, dkd3, dvd3 = attn_bwd(s['qd3'], s['kd3'], s['vd3'], s['o_d3'], s['lse_d3'], do_cat,
                                    3 * GROUP_W // pair_w, scale_d, "_d" + tag)
        daglu_pad, dkp['caw'], dkp['cab'], dkp['lag'], dkp['lab'] = aconv_bwd(s['aglu_pad'], kp, do_cat, tag)
        cts = [daglu_pad[CONV_A_HALO:CONV_A_HALO + seq], dq3, dk3, dv3, do_c, dqd3, dkd3, dvd3]
        pre_g = pre_bwd(s['proj'], tabs, kp, cts, tag)
        dproj = pre_g[0]
        for n, g in zip(('qng', 'kng', 'sg', 'sb', 'sw', 'sbt', 'mqn', 'wuq', 'mkvn', 'wukv'), pre_g[1:]):
            dkp[n] = g
        dh_b = matmul(dproj, m['w_in'], 'nt', F32, "mm_dh" + tag)
        g_in = matmul(s['h_m'], dproj, 'tn', COMM_DTYPE, "mm_gin" + tag)
        dh_parts = [dh_a, dh_b]
        (dw,) = s['unprep'](dkp)
        small_acc = dw if small_acc is None else jax.tree.map(jnp.add, small_acc, dw)
        g_mix = {('w_out', l): g_out, ('w_in', l): _unpad_w_in(g_in)}
        if l > 0:
            tok = hook(f"mix{l}", g_mix)
        else:
            g_mix.pop(('w_out', l))

    g_mix.update({(n, None): small_acc[n] for n in SHARDED if n not in MATMUL_WEIGHTS})
    tok = hook("last", g_mix)
    dx, dg, db = stage_bwd("ln_in_bwd", fn_ln,
                           [_row_op(x, tm, grad=True), _par_op(after(ln_in_g, tok), grad=True),
                            _par_op(ln_in_b, grad=True)],
                           [[(p, (tm, D_MODEL), lambda i: (i, 0)) for p in dh_parts]], (seq // tm,))
    out = {n: small_acc[n] for n in REPLICATED}
    out['ln_in_g'], out['ln_in_b'] = dg.reshape(-1), db.reshape(-1)
    return loss, dx, out


def _peer(x, y, c, r):
    return ((1 - x) if r & 4 else x, (1 - y) if r & 2 else y, (1 - c) if r & 1 else c)


def _exchange_copy(src_ref, land_ref, send_sems, recv_sems, k, gather, x, y, c, r):
    px, py, pc = _peer(x, y, c, r)
    me, peer = 4 * x + 2 * y + c, 4 * px + 2 * py + pc
    src = src_ref if gather else src_ref.at[peer]
    mk = lambda dst: pltpu.make_async_remote_copy(
        src_ref=src, dst_ref=dst, send_sem=send_sems.at[k * (N_DEV - 1) + r - 1],
        recv_sem=recv_sems.at[k * (N_DEV - 1) + r - 1],
        device_id=(px, py, pc), device_id_type=pl.DeviceIdType.MESH)
    return mk(land_ref.at[me]), mk(land_ref.at[peer])


_HBM_SPEC = pl.BlockSpec(memory_space=pltpu.HBM)
_SEM_SPEC = pl.BlockSpec(memory_space=pltpu.SEMAPHORE)


def exchange_start(srcs, gather, groups, name):
    n_t = len(srcs)
    lands =[lax.empty(((N_DEV,) + s.shape) if gt else s.shape, s.dtype) for s, gt in zip(srcs, gather)]

    def body(*refs):
        src_refs, land_refs = refs[:n_t], refs[n_t:2 * n_t]
        sem_refs = refs[2 * n_t:2 * n_t + 2 * len(groups)]
        token = refs[-1]
        x, y, c = lax.axis_index("x"), lax.axis_index("y"), lax.axis_index("c")
        for gi, g in enumerate(groups):
            for k, t in enumerate(g):
                for r in range(1, N_DEV):
                    _exchange_copy(src_refs[t], land_refs[t], sem_refs[2 * gi], sem_refs[2 * gi + 1], k, gather[t],
                                   x, y, c, r)[0].start()
        token[...] = jnp.zeros_like(token)

    sem_shapes = []
    for g in groups:
        sem_shapes += [pltpu.SemaphoreType.DMA((len(g) * (N_DEV - 1),))] * 2
    hbm_shapes = [pltpu.HBM(a.shape, a.dtype) for a in list(srcs) + lands]
    n_sem = len(sem_shapes)
    res = pl.pallas_call(
        body, name=name,
        out_shape=tuple(sem_shapes + hbm_shapes + [jax.ShapeDtypeStruct((8, 128), F32)]),
        in_specs=[_HBM_SPEC] * (2 * n_t),
        out_specs=tuple([_SEM_SPEC] * n_sem + [_HBM_SPEC] * (2 * n_t) + [pl.BlockSpec(memory_space=pltpu.VMEM)]),
        input_output_aliases={i: n_sem + i for i in range(2 * n_t)},
        compiler_params=pltpu.CompilerParams(has_side_effects=pltpu.SideEffectType.DATAFLOW_SIDE_EFFECTING),
    )(*[pltpu.with_memory_space_constraint(a, pltpu.HBM) for a in list(srcs) + lands])
    sems = [(res[2 * gi], res[2 * gi + 1]) for gi in range(len(groups))]
    return sems, list(res[n_sem:n_sem + n_t]), list(res[n_sem + n_t:n_sem + 2 * n_t]), res[-1]


def exchange_wait(sems, srcs, lands, gather, after, name):
    n_t = len(srcs)

    def body(*refs):
        src_refs, land_refs = refs[:n_t], refs[n_t:2 * n_t]
        send_sems, recv_sems = refs[2 * n_t], refs[2 * n_t + 1]
        x, y, c = lax.axis_index("x"), lax.axis_index("y"), lax.axis_index("c")
        for k in range(n_t):
            for r in range(1, N_DEV):
                send, recv = _exchange_copy(src_refs[k], land_refs[k], send_sems, recv_sems, k, gather[k], x, y, c, r)
                send.wait_send()
                recv.wait_recv()

    res = pl.pallas_call(
        body, name=name,
        out_shape=tuple(pltpu.HBM(a.shape, a.dtype) for a in list(srcs) + list(lands)),
        in_specs=[_HBM_SPEC] * (2 * n_t) + [_SEM_SPEC, _SEM_SPEC, pl.BlockSpec(memory_space=pl.ANY)],
        out_specs=tuple([_HBM_SPEC] * (2 * n_t)),
        input_output_aliases={i: i for i in range(2 * n_t)},
        compiler_params=pltpu.CompilerParams(has_side_effects=pltpu.SideEffectType.DATAFLOW_SIDE_EFFECTING),
    )(*srcs, *lands, sems[0], sems[1], after)
    return list(res[:n_t]), list(res[n_t:])


def adamw(parts, w, m, v, name):
    n_l, n_r, n_c = w.shape
    tr = n_r
    if n_r % 8 == 0:
        for cand in (512, 256, 128, 64, 32, 16, 8):
            if n_r % cand == 0 and cand * n_c * 4 <= 512 * 1024:
                tr = cand
                break
    c1 = 1.0 - ADAM_B1 ** ADAM_STEP
    c2 = 1.0 - ADAM_B2 ** ADAM_STEP
    per_layer = isinstance(parts, (list, tuple))
    n_p = n_l if per_layer else 1
    n_rb = n_r // tr

    def update(g, w_ref, m_ref, v_ref, g_ref, d_ref, nm_ref, nv_ref):
        w_, m_, v_ = w_ref[0], m_ref[0], v_ref[0]
        nm = ADAM_B1 * m_ + (1.0 - ADAM_B1) * g
        nv = ADAM_B2 * v_ + (1.0 - ADAM_B2) * (g * g)
        g_ref[0] = g
        nm_ref[0] = nm
        nv_ref[0] = nv
        d_ref[0] = -ADAM_LR * ((nm / c1) / (jnp.sqrt(nv / c2) + ADAM_EPS) + ADAM_WD * w_)

    def body(*refs):
        p_refs, rest = refs[:n_p], refs[n_p:]
        if not per_layer:
            g = p_refs[0][0, 0].astype(F32)
            for s in range(1, N_DEV):
                g = g + p_refs[0][s, 0].astype(F32)
            update(g, *rest)
        else:
            for lay in range(n_l):
                @pl.when(pl.program_id(0) == lay)
                def _(lay=lay):
                    g = p_refs[lay][0].astype(F32)
                    for s in range(1, N_DEV):
                        g = g + p_refs[lay][s].astype(F32)
                    update(g, *rest)

    blk = pl.BlockSpec((1, tr, n_c), lambda l, r: (l, r, 0))
    if per_layer:
        def p_spec(lay):
            park = 0 if lay > 0 else n_rb - 1
            return pl.BlockSpec((N_DEV, tr, n_c), lambda l, r: (0, jnp.where(l == lay, r, park), 0))
        p_specs, p_args = [p_spec(lay) for lay in range(n_l)], list(parts)
    else:
        p_specs, p_args = [pl.BlockSpec((N_DEV, 1, tr, n_c), lambda l, r: (0, l, r, 0))], [parts]
    return pl.pallas_call(
        body, grid=(n_l, n_rb), in_specs=p_specs + [blk, blk, blk],
        out_specs=[blk] * 4, out_shape=[jax.ShapeDtypeStruct(w.shape, F32)] * 4,
        compiler_params=pltpu.CompilerParams(dimension_semantics=("arbitrary", "arbitrary"),
                                             vmem_limit_bytes=VMEM_LIMIT),
        name=name)(*p_args, w, m, v)


def adamw_replicated(lands, own, ws, ms, vs, loss_land, loss_own):
    n_t = len(lands)
    c1 = 1.0 - ADAM_B1 ** ADAM_STEP
    c2 = 1.0 - ADAM_B2 ** ADAM_STEP

    def body(*refs):
        ins, outs = refs[:5 * n_t + 2], refs[5 * n_t + 2:]
        me = 4 * lax.axis_index("x") + 2 * lax.axis_index("y") + lax.axis_index("c")

        def total(land_ref, own_ref):
            g = None
            for s in range(N_DEV):
                term = jnp.where(me == s, own_ref[...], land_ref[s])
                g = term if g is None else g + term
            return g

        for t in range(n_t):
            land_ref, own_ref, w_ref, m_ref, v_ref = ins[5 * t:5 * t + 5]
            g = total(land_ref, own_ref)
            nm = ADAM_B1 * m_ref[...] + (1.0 - ADAM_B1) * g
            nv = ADAM_B2 * v_ref[...] + (1.0 - ADAM_B2) * (g * g)
            g_ref, d_ref, nm_ref, nv_ref = outs[4 * t:4 * t + 4]
            g_ref[...] = g
            nm_ref[...] = nm
            nv_ref[...] = nv
            d_ref[...] = -ADAM_LR * ((nm / c1) / (jnp.sqrt(nv / c2) + ADAM_EPS) + ADAM_WD * w_ref[...])
        outs[4 * n_t][...] = total(ins[5 * n_t], ins[5 * n_t + 1])

    args = []
    for t in range(n_t):
        args += [lands[t], own[t], ws[t], ms[t], vs[t]]
    out_shape = []
    for t in range(n_t):
        out_shape += [jax.ShapeDtypeStruct(ws[t].shape, F32)] * 4
    out_shape.append(jax.ShapeDtypeStruct(loss_own.shape, F32))
    res = pl.pallas_call(body, out_shape=out_shape,
                         compiler_params=pltpu.CompilerParams(vmem_limit_bytes=VMEM_LIMIT),
                         name="adamw_replicated")(*args, loss_land, loss_own)
    return [tuple(res[4 * t:4 * t + 4]) for t in range(n_t)], res[-1]


def _shard_slots(g, axis):
    if axis == 1:
        return g.reshape(g.shape[0], N_DEV, g.shape[1] // N_DEV, g.shape[2]).transpose(1, 0, 2, 3)
    return g.reshape(g.shape[0], g.shape[1], N_DEV, g.shape[2] // N_DEV).transpose(2, 0, 1, 3)


def _unshard(slots, axis):
    if axis == 1:
        return slots.transpose(1, 0, 2, 3).reshape(slots.shape[1], -1, slots.shape[3])
    return slots.transpose(1, 2, 0, 3).reshape(slots.shape[1], slots.shape[2], -1)


def kernel(x, ln_in_g, ln_in_b, w_in, conv_a_w, conv_a_b, ln_a_g, ln_a_b, qk_norm_q, qk_norm_k, sgu_ln_g, sgu_ln_b, sgu_w, sgu_b, mla_q_norm, mla_w_uq, mla_kv_norm, mla_w_ukv, w_out, ln_mix_g, ln_mix_b, ffn_w_up, ffn_conv_w, ffn_conv_b, ffn_w_down, ln_ffn_g, ln_ffn_b, loss_target, m_ln_in_g, m_ln_in_b, m_w_in, m_conv_a_w, m_conv_a_b, m_ln_a_g, m_ln_a_b, m_qk_norm_q, m_qk_norm_k, m_sgu_ln_g, m_sgu_ln_b, m_sgu_w, m_sgu_b, m_mla_q_norm, m_mla_w_uq, m_mla_kv_norm, m_mla_w_ukv, m_w_out, m_ln_mix_g, m_ln_mix_b, m_ffn_w_up, m_ffn_conv_w, m_ffn_conv_b, m_ffn_w_down, m_ln_ffn_g, m_ln_ffn_b, v_ln_in_g, v_ln_in_b, v_w_in, v_conv_a_w, v_conv_a_b, v_ln_a_g, v_ln_a_b, v_qk_norm_q, v_qk_norm_k, v_sgu_ln_g, v_sgu_ln_b, v_sgu_w, v_sgu_b, v_mla_q_norm, v_mla_w_uq, v_mla_kv_norm, v_mla_w_ukv, v_w_out, v_ln_mix_g, v_ln_mix_b, v_ffn_w_up, v_ffn_conv_w, v_ffn_conv_b, v_ffn_w_down, v_ln_ffn_g, v_ln_ffn_b):
    local = dict(ln_in_g=ln_in_g, ln_in_b=ln_in_b, w_in=w_in, conv_a_w=conv_a_w, conv_a_b=conv_a_b, ln_a_g=ln_a_g, ln_a_b=ln_a_b, qk_norm_q=qk_norm_q, qk_norm_k=qk_norm_k, sgu_ln_g=sgu_ln_g, sgu_ln_b=sgu_ln_b, sgu_w=sgu_w, sgu_b=sgu_b, mla_q_norm=mla_q_norm, mla_w_uq=mla_w_uq, mla_kv_norm=mla_kv_norm, mla_w_ukv=mla_w_ukv, w_out=w_out, ln_mix_g=ln_mix_g, ln_mix_b=ln_mix_b, ffn_w_up=ffn_w_up, ffn_conv_w=ffn_conv_w, ffn_conv_b=ffn_conv_b, ffn_w_down=ffn_w_down, ln_ffn_g=ln_ffn_g, ln_ffn_b=ln_ffn_b)
    mom = dict(ln_in_g=m_ln_in_g, ln_in_b=m_ln_in_b, w_in=m_w_in, conv_a_w=m_conv_a_w, conv_a_b=m_conv_a_b, ln_a_g=m_ln_a_g, ln_a_b=m_ln_a_b, qk_norm_q=m_qk_norm_q, qk_norm_k=m_qk_norm_k, sgu_ln_g=m_sgu_ln_g, sgu_ln_b=m_sgu_ln_b, sgu_w=m_sgu_w, sgu_b=m_sgu_b, mla_q_norm=m_mla_q_norm, mla_w_uq=m_mla_w_uq, mla_kv_norm=m_mla_kv_norm, mla_w_ukv=m_mla_w_ukv, w_out=m_w_out, ln_mix_g=m_ln_mix_g, ln_mix_b=m_ln_mix_b, ffn_w_up=m_ffn_w_up, ffn_conv_w=m_ffn_conv_w, ffn_conv_b=m_ffn_conv_b, ffn_w_down=m_ffn_w_down, ln_ffn_g=m_ln_ffn_g, ln_ffn_b=m_ln_ffn_b)
    var = dict(ln_in_g=v_ln_in_g, ln_in_b=v_ln_in_b, w_in=v_w_in, conv_a_w=v_conv_a_w, conv_a_b=v_conv_a_b, ln_a_g=v_ln_a_g, ln_a_b=v_ln_a_b, qk_norm_q=v_qk_norm_q, qk_norm_k=v_qk_norm_k, sgu_ln_g=v_sgu_ln_g, sgu_ln_b=v_sgu_ln_b, sgu_w=v_sgu_w, sgu_b=v_sgu_b, mla_q_norm=v_mla_q_norm, mla_w_uq=v_mla_w_uq, mla_kv_norm=v_mla_kv_norm, mla_w_ukv=v_mla_w_ukv, w_out=v_w_out, ln_mix_g=v_ln_mix_g, ln_mix_b=v_ln_mix_b, ffn_w_up=v_ffn_w_up, ffn_conv_w=v_ffn_conv_w, ffn_conv_b=v_ffn_conv_b, ffn_w_down=v_ffn_w_down, ln_ffn_g=v_ln_ffn_g, ln_ffn_b=v_ln_ffn_b)

    me = 4 * lax.axis_index("x") + 2 * lax.axis_index("y") + lax.axis_index("c")

    def own_slot(slots, block):
        return lax.dynamic_update_slice(slots, block[None], (me,) + (0,) * block.ndim)

    small_sharded = [n for n in SHARDED if n not in MATMUL_WEIGHTS]
    big_order = [(n, l) for l in range(DEPTH) for n in MATMUL_WEIGHTS]
    srcs = [local['w_in'][0].astype(COMM_DTYPE)] + [local[n] for n in small_sharded]
    send_view = lambda n, l: local[n].transpose(0, 2, 1)[l] if n in TRANSPOSED else local[n][l]
    srcs += [send_view(n, l).astype(COMM_DTYPE) for (n, l) in big_order[1:]]
    n_first = 1 + len(small_sharded)
    groups = [list(range(n_first))] + [[n_first + j] for j in range(len(big_order) - 1)]
    g_sems, g_srcs, g_lands, tok0 = exchange_start(srcs, [True] * len(srcs), groups, "gather_start")
    tok0 = tok0[0, 0]
    pending = {key: gi for gi, key in enumerate(big_order)}

    def finish(gi, after):
        idx = groups[gi]
        _, lands = exchange_wait(g_sems[gi], [g_srcs[t] for t in idx], [g_lands[t] for t in idx], [True] * len(idx),
                                 after, f"gather_wait{gi}")
        return [own_slot(ld, srcs[t]) for ld, t in zip(lands, idx)]

    first = []

    def get_wts(after):
        first.extend(finish(0, after))
        wts = {n: local[n] for n in REPLICATED}
        for n, slots in zip(small_sharded, first[1:]):
            wts[n] = _unshard(slots, SHARDED[n])
        return wts

    def unshard_layer(slots, n):
        if SHARDED[n] == 1 or n in TRANSPOSED:
            return slots.reshape(-1, slots.shape[2])
        return slots.transpose(1, 0, 2).reshape(slots.shape[1], -1)

    def mat(l, n, after):
        gi = pending[(n, l)]
        slots = first[0] if gi == 0 else finish(gi, after)[0]
        w = unshard_layer(slots, n).astype(MXU_DTYPE)
        return _pad_w_in(w) if n == 'w_in' else w

    started = []

    def hook(key, grads):
        tensors = []
        for (n, l), g in grads.items():
            if l is None:
                tensors.append(((n, l), _shard_slots(g, SHARDED[n])))
            elif n in TRANSPOSED:
                tensors.append(((n, l), g.reshape(N_DEV, g.shape[0] // N_DEV, g.shape[1]).astype(COMM_DTYPE)))
            else:
                tensors.append(((n, l), _shard_slots(g[None], SHARDED[n])[:, 0].astype(COMM_DTYPE)))
        sems, s_srcs, s_lands, tok = exchange_start([a for _, a in tensors], [False] * len(tensors),
                                                    [list(range(len(tensors)))], "scatter_start_" + key)
        started.append((key, [k for k, _ in tensors], sems[0], s_srcs, s_lands))
        return tok[0, 0]

    loss, dx, grads = local_step(x[0], loss_target[0], (local['ln_in_g'] + tok0, local['ln_in_b']), get_wts, mat, hook)

    as2d = lambda a: a.reshape(-1, a.shape[-1]) if a.ndim > 1 else a.reshape(1, -1)
    small_g = [as2d(grads[n]) for n in REPLICATED] + [jnp.broadcast_to(loss, (8, 128))]
    p_sems, p_srcs, p_lands, p_tok = exchange_start(small_g, [True] * len(small_g), [list(range(len(small_g)))],
                                                    "gather_small_start")

    parts, res = {}, {}

    def finish_scatter(entries, after):
        for key, keys, sems, s_srcs, s_lands in entries:
            s_out, lands = exchange_wait(sems, s_srcs, s_lands, [False] * len(keys), after, "scatter_wait_" + key)
            for k, so, ld in zip(keys, s_out, lands):
                parts[k] = own_slot(ld, lax.dynamic_index_in_dim(so, me, 0, keepdims=False))

    def update(names_):
        for n in names_:
            p = [parts[(n, l)] for l in range(DEPTH)] if n in MATMUL_WEIGHTS else parts[(n, None)]
            view = (lambda a: a.transpose(0, 2, 1)) if n in TRANSPOSED else (lambda a: a)
            res[n] = tuple(view(a) for a in adamw(p, view(local[n]), view(mom[n]), view(var[n]), "adamw_" + n))

    early = ('ffn_w_up', 'ffn_w_down', 'w_out')
    finish_scatter([e for e in started if e[0] != "last"], p_tok)
    update(early)
    finish_scatter([e for e in started if e[0] == "last"], res[early[-1]][1])
    update([n for n in SHARDED if n not in early])
    updated = jnp.zeros((8, 128), F32) + sum(res[n][1][0, 0, 0] for n in SHARDED)
    p_own, p_lands = exchange_wait(p_sems[0], p_srcs, p_lands, [True] * len(small_g), updated, "gather_small_wait")
    small, loss_sum = adamw_replicated(p_lands[:-1], p_own[:-1], [as2d(local[n]) for n in REPLICATED],
                                       [as2d(mom[n]) for n in REPLICATED], [as2d(var[n]) for n in REPLICATED],
                                       p_lands[-1], p_own[-1])
    for n, quad in zip(REPLICATED, small):
        res[n] = tuple(a.reshape(local[n].shape) for a in quad)
    loss_total = loss_sum[0, 0]

    return (loss_total, dx[None], *[res[n][0] for n in WEIGHTS], *[res[n][1] for n in WEIGHTS],
            *[res[n][2] for n in WEIGHTS], *[res[n][3] for n in WEIGHTS])
```

```python
import functools
import math

import jax
import jax.numpy as jnp
from jax import lax
from jax.experimental import pallas as pl
from jax.experimental.pallas import tpu as pltpu

F32 = jnp.float32
MXU_DTYPE = jnp.bfloat16
COMM_DTYPE = jnp.bfloat16

N_DEV = 8
D_MODEL = 1024
DEPTH = 2
GRID_W = 64
GROUP_W = 256
HEAD_DIM = 64
CONV_A_WIDTH = 31
CONV_A_HALO = 16
GQA_HEADS = 4
GQA_KV_HEADS = 2
CHUNK = 128
SGU_GROUPS = 4
MLA_HEADS = 4
MLA_Q_LORA = 192
MLA_KV_LORA = 128
MLA_NOPE = 64
MLA_ROPE = 32
MLA_V = 64
MLA_DK_PAD = 128
ROPE_THETA = 10000.0
D_FF = 2816
DEEPNORM_ALPHA = (2 * DEPTH) ** 0.25
LN_EPS = 1e-5
RMS_EPS = 1e-6
D_IN_PROJ = 1888

ADAM_LR = 0.001
ADAM_B1 = 0.9
ADAM_B2 = 0.999
ADAM_EPS = 1e-08
ADAM_WD = 0.01
ADAM_STEP = 10

WEIGHTS = ['ln_in_g', 'ln_in_b', 'w_in', 'conv_a_w', 'conv_a_b', 'ln_a_g', 'ln_a_b', 'qk_norm_q', 'qk_norm_k',
           'sgu_ln_g', 'sgu_ln_b', 'sgu_w', 'sgu_b', 'mla_q_norm', 'mla_w_uq', 'mla_kv_norm', 'mla_w_ukv', 'w_out',
           'ln_mix_g', 'ln_mix_b', 'ffn_w_up', 'ffn_conv_w', 'ffn_conv_b', 'ffn_w_down', 'ln_ffn_g', 'ln_ffn_b']
SHARDED = {'w_in': 2, 'conv_a_w': 2, 'mla_w_uq': 2, 'mla_w_ukv': 2, 'w_out': 1, 'ffn_w_up': 2, 'ffn_conv_w': 2,
           'ffn_w_down': 1}
MATMUL_WEIGHTS = ('w_in', 'w_out', 'ffn_w_up', 'ffn_w_down')
TRANSPOSED = ('w_in', 'ffn_w_up')
REPLICATED = [n for n in WEIGHTS if n not in SHARDED]

ROW_TILE = 256
VMEM_LIMIT = 56 * 1024 * 1024


def _rawdot(a, b, ca, cb):
    return lax.dot_general(a.astype(MXU_DTYPE), b.astype(MXU_DTYPE), (((ca,), (cb,)), ((), ())),
                           preferred_element_type=F32)


@jax.custom_vjp
def mm_nn(a, b):
    return _rawdot(a, b, 1, 0)


def _mm_nn_fwd(a, b):
    return _rawdot(a, b, 1, 0), (a, b)


def _mm_nn_bwd(res, dy):
    a, b = res
    return _rawdot(dy, b, 1, 1), _rawdot(a, dy, 0, 0)


mm_nn.defvjp(_mm_nn_fwd, _mm_nn_bwd)


@jax.custom_vjp
def mm_nt(a, b):
    return _rawdot(a, b, 1, 1)


def _mm_nt_fwd(a, b):
    return _rawdot(a, b, 1, 1), (a, b)


def _mm_nt_bwd(res, dy):
    a, b = res
    return _rawdot(dy, b, 1, 0), _rawdot(dy, a, 0, 0)


mm_nt.defvjp(_mm_nt_fwd, _mm_nt_bwd)


def _pick_tile(d, cands):
    for c in cands:
        if d % c == 0:
            return c
    return d


def matmul(a, b, mode, out_dtype, name, b_rows=None, into=None):
    b_start, b_size = (0, b.shape[0]) if b_rows is None else b_rows
    if mode == 'nn':
        (m, k), (k2, n) = a.shape, (b_size, b.shape[1])
    elif mode == 'nt':
        (m, k), (n, k2) = a.shape, (b_size, b.shape[1])
    else:
        (k, m), (k2, n) = a.shape, (b_size, b.shape[1])
    assert k == k2, (a.shape, b.shape, mode)
    tm = _pick_tile(m, (1024, 1408, 512, 256, 128))
    tn = _pick_tile(n, (512, 1408, 256, 128))
    tk = _pick_tile(k, (2816, 2048, 1024, 512, 256, 128))
    nk = k // tk
    ca = 0 if mode == 'tn' else 1
    cb = 1 if mode == 'nt' else 0
    b_blk = tn if mode == 'nt' else tk
    assert b_start % b_blk == 0, (b_rows, b_blk)
    b_off = b_start // b_blk
    a_spec = pl.BlockSpec((tk, tm), lambda i, j, kk: (kk, i)) if mode == 'tn' else pl.BlockSpec((tm, tk), lambda i, j, kk: (i, kk))
    b_spec = (pl.BlockSpec((tn, tk), lambda i, j, kk: (j + b_off, kk)) if mode == 'nt'
              else pl.BlockSpec((tk, tn), lambda i, j, kk: (kk + b_off, j)))
    in_specs, args, aliases = [a_spec, b_spec], [a, b], {}
    out_off, out_shape = 0, jax.ShapeDtypeStruct((m, n), out_dtype)
    if into is not None:
        buf, row = into
        assert row % tm == 0 and buf.shape[1] == n and buf.dtype == out_dtype, (buf.shape, row, tm)
        out_off, out_shape = row // tm, jax.ShapeDtypeStruct(buf.shape, buf.dtype)
        in_specs, args, aliases = in_specs + [pl.BlockSpec(memory_space=pl.ANY)], args + [buf], {2: 0}

    def body(a_ref, b_ref, *rest):
        o_ref, acc_ref = rest[-2:]
        kk = pl.program_id(2)

        @pl.when(kk == 0)
        def _():
            acc_ref[...] = jnp.zeros_like(acc_ref)

        acc_ref[...] += _rawdot(a_ref[...], b_ref[...], ca, cb)

        @pl.when(kk == nk - 1)
        def _():
            o_ref[...] = acc_ref[...].astype(o_ref.dtype)

    return pl.pallas_call(
        body, grid=(m // tm, n // tn, nk), in_specs=in_specs,
        out_specs=pl.BlockSpec((tm, tn), lambda i, j, kk: (i + out_off, j)),
        out_shape=out_shape, input_output_aliases=aliases,
        scratch_shapes=[pltpu.VMEM((tm, tn), F32)],
        compiler_params=pltpu.CompilerParams(dimension_semantics=("parallel", "parallel", "arbitrary"),
                                             vmem_limit_bytes=VMEM_LIMIT),
        name=name)(*args)


class Op:
    def __init__(self, arr, block, imap, grad=False, acc=False, first=None, gdtype=F32, gshape=None, gimap=None):
        self.arr, self.block, self.imap = arr, block, imap
        self.grad, self.acc, self.first, self.gdtype = grad, acc, first, gdtype
        self.gshape = arr.shape if gshape is None else gshape
        self.gimap = imap if gimap is None else gimap


def _row_op(arr, tm, grad=False, gdtype=F32):
    return Op(arr, (tm, arr.shape[1]), lambda i: (i, 0), grad=grad, gdtype=gdtype)


def _par_op(arr, grad=False):
    nd = arr.ndim
    return Op(arr, arr.shape, lambda i: (0,) * nd, grad=grad, acc=True, first=lambda ids: ids[0] == 0)


def _load(ref):
    v = ref[...]
    return v.astype(F32) if jnp.issubdtype(v.dtype, jnp.floating) else v


def _store_heads(ref, val):
    if len(ref.shape) == 2:
        ref[...] = val.astype(ref.dtype)
    else:
        d = ref.shape[2]
        for h in range(ref.shape[0]):
            ref[h] = val[:, d * h:d * (h + 1)].astype(ref.dtype)


def _load_heads(ref):
    if len(ref.shape) == 2:
        return ref[...].astype(F32)
    return jnp.concatenate([ref[h].astype(F32) for h in range(ref.shape[0])], axis=-1)


def stage_fwd(name, fn, ops, outs, grid):
    n_in = len(ops)

    def body(*refs):
        res = fn(*[_load(r) for r in refs[:n_in]])
        for r, o in zip(refs[n_in:], res):
            _store_heads(r, o)

    return pl.pallas_call(
        body, grid=grid, in_specs=[pl.BlockSpec(o.block, o.imap) for o in ops],
        out_specs=[pl.BlockSpec(b, im) for (_, _, b, im) in outs],
        out_shape=[jax.ShapeDtypeStruct(s, d) for (s, d, _, _) in outs],
        compiler_params=pltpu.CompilerParams(dimension_semantics=("parallel",) * len(grid),
                                             vmem_limit_bytes=VMEM_LIMIT),
        name=name)(*[o.arr for o in ops])


def stage_bwd(name, fn, ops, cts, grid, value_acc=False):
    n_in = len(ops)
    ct_flat = [c for group in cts if group is not None for c in group]
    n_ct = len(ct_flat)
    diff = [i for i, o in enumerate(ops) if o.grad]
    any_acc = value_acc or any(ops[i].acc for i in diff)
    ngrid = len(grid)

    def body(*refs):
        ids = [pl.program_id(a) for a in range(ngrid)]
        vals = [_load(r) for r in refs[:n_in]]
        ct_refs = refs[n_in:n_in + n_ct]
        out_refs = refs[n_in + n_ct:]

        def f(*dv):
            full = list(vals)
            for i, v in zip(diff, dv):
                full[i] = v
            return tuple(fn(*full))

        res, vjp = jax.vjp(f, *[vals[i] for i in diff])
        ct, pos = [], 0
        for group, r in zip(cts, res):
            if group is None:
                ct.append(jnp.ones_like(r))
            else:
                tot = None
                for _ in group:
                    c = _load_heads(ct_refs[pos])
                    tot = c if tot is None else tot + c
                    pos += 1
                ct.append(tot)
        grads = vjp(tuple(ct))
        for i, g, r in zip(diff, grads, out_refs):
            if ops[i].acc:
                @pl.when(ops[i].first(ids))
                def _(r=r):
                    r[...] = jnp.zeros_like(r)

                r[...] += g.astype(r.dtype)
            else:
                r[...] = g.astype(r.dtype)
        if value_acc:
            r = out_refs[len(diff)]

            @pl.when(ids[0] == 0)
            def _():
                r[...] = jnp.zeros_like(r)

            r[...] += res[0]

    in_specs = [pl.BlockSpec(o.block, o.imap) for o in ops] + [pl.BlockSpec(b, im) for (_, b, im) in ct_flat]
    out_specs = [pl.BlockSpec(ops[i].block, ops[i].gimap) for i in diff]
    out_shape = [jax.ShapeDtypeStruct(ops[i].gshape, ops[i].gdtype) for i in diff]
    if value_acc:
        out_specs.append(pl.BlockSpec((1, 1), lambda *ids: (0, 0)))
        out_shape.append(jax.ShapeDtypeStruct((1, 1), F32))
    sem = ("arbitrary",) * ngrid if any_acc else ("parallel",) * ngrid
    return pl.pallas_call(
        body, grid=grid, in_specs=in_specs, out_specs=out_specs, out_shape=out_shape,
        compiler_params=pltpu.CompilerParams(dimension_semantics=sem, vmem_limit_bytes=VMEM_LIMIT),
        name=name)(*[o.arr for o in ops], *[a for (a, _, _) in ct_flat])


def _sigmoid(x):
    return 1.0 / (1.0 + jnp.exp(-x))


def _silu(x):
    return x * _sigmoid(x)


def _gelu_tanh(x):
    return 0.5 * x * (1.0 + jnp.tanh(math.sqrt(2.0 / math.pi) * (x + 0.044715 * (x * x * x))))


def _ln(x, g, b):
    mu = jnp.mean(x, axis=-1, keepdims=True)
    xc = x - mu
    var = jnp.mean(xc * xc, axis=-1, keepdims=True)
    return xc * lax.rsqrt(var + LN_EPS) * g + b


def _rms(x, g):
    ms = jnp.mean(x * x, axis=-1, keepdims=True)
    return x * lax.rsqrt(ms + RMS_EPS) * g


def _swap_halves(x, half):
    width = x.shape[-1]
    lane = lax.broadcasted_iota(jnp.int32, x.shape, 1)
    return jnp.where(lane % (2 * half) < half, pltpu.roll(x, width - half, 1), pltpu.roll(x, half, 1))


def _make_swap(half):
    @jax.custom_vjp
    def swap(x):
        return _swap_halves(x, half)

    swap.defvjp(lambda x: (_swap_halves(x, half), None), lambda _, dy: (_swap_halves(dy, half),))
    return swap


_swap16, _swap8 = _make_swap(16), _make_swap(8)


def _rope(x, cos, sin_signed, swap):
    return x * cos + swap(x) * sin_signed


def _dot_f32(a, b):
    return jnp.dot(a, b, preferred_element_type=F32, precision=lax.Precision.HIGHEST)


def fn_ln(x, g, b):
    return (_ln(x, g, b),)


def _twice(fn):
    def f(*a):
        (y,) = fn(*a)
        return y, y
    return f


PROJ_W = 2048
P_A, P_Q, P_K, P_V, P_C, P_CQ, P_CKV, P_KR = 0, 512, 768, 896, 1024, 1536, 1792, 1920
CQ_PAD = 256
_CQ_END = P_CQ + MLA_Q_LORA


def _pad_w_in(wt):
    z = lambda n: jnp.zeros((n, wt.shape[1]), wt.dtype)
    return jnp.concatenate([wt[:_CQ_END], z(P_CKV - _CQ_END), wt[_CQ_END:], z(PROJ_W - P_KR - MLA_ROPE)], axis=0)


def _unpad_w_in(gt):
    return jnp.concatenate([gt[:_CQ_END], gt[P_CKV:P_KR + MLA_ROPE]], axis=0)


def fn_pre(proj, tab_q, tab_d, seg, place, qng, kng, sg, sb, sw, sbt, mqn, wuq, mkvn, wukv):
    tm = proj.shape[0]
    aglu = proj[:, P_A:P_A + GROUP_W] * _sigmoid(proj[:, P_A + GROUP_W:P_Q])
    b_q, b_k, b_v = proj[:, P_Q:P_K], proj[:, P_K:P_V], proj[:, P_V:P_C]
    cos_q, sin_q = tab_q[:, :GROUP_W], tab_q[:, GROUP_W:]
    q = b_q * lax.rsqrt(_dot_f32(b_q * b_q, seg) + RMS_EPS) * qng
    q = _rope(q, cos_q, sin_q, _swap16)
    k = b_k * lax.rsqrt(_dot_f32(b_k * b_k, seg[:128, :128]) + RMS_EPS) * kng
    k = _rope(k, cos_q[:, :128], sin_q[:, :128], _swap16)
    c = _gelu_tanh(proj[:, P_C:P_CQ])
    u, sv = c[:, :GROUP_W], _ln(c[:, GROUP_W:], sg, sb)
    group = lax.broadcasted_iota(jnp.int32, (CHUNK, GROUP_W), 1) // HEAD_DIM
    rows = []
    for n in range(tm // CHUNK):
        svn = sv[CHUNK * n:CHUNK * (n + 1)]
        acc = jnp.zeros((CHUNK, GROUP_W), F32)
        for g in range(SGU_GROUPS):
            acc = acc + jnp.where(group == g, mm_nn(sw[CHUNK * g:CHUNK * (g + 1)], svn) + sbt[:, g:g + 1], 0.0)
        rows.append(acc)
    o_c = u * jnp.concatenate(rows, axis=0)
    d_cq, d_ckv, d_kr = proj[:, P_CQ:P_CKV], proj[:, P_CKV:P_KR], proj[:, P_KR:PROJ_W]
    cqn = d_cq * lax.rsqrt(jnp.sum(d_cq * d_cq, axis=-1, keepdims=True) * (1.0 / MLA_Q_LORA) + RMS_EPS) * mqn
    cos_d = jnp.concatenate([tab_d[:, :MLA_DK_PAD]] * MLA_HEADS, axis=-1)
    sin_d = jnp.concatenate([tab_d[:, MLA_DK_PAD:]] * MLA_HEADS, axis=-1)
    qf = _rope(mm_nn(cqn, wuq), cos_d, sin_d, _swap8)
    kvd = mm_nn(_rms(d_ckv, mkvn), wukv)
    kf = _rope(kvd[:, :MLA_HEADS * MLA_DK_PAD] + _dot_f32(d_kr, place), cos_d, sin_d, _swap8)
    return aglu, q, k, b_v, o_c, qf, kf, kvd[:, MLA_HEADS * MLA_DK_PAD:]


def fn_aconv(win, w, b, g, beta):
    tm = win.shape[0] - 2 * CONV_A_HALO
    off = CONV_A_HALO - CONV_A_WIDTH // 2
    rolled = [win] + [_roll_rows(win, -r) for r in range(1, 8)]
    acc = None
    for kk in range(CONV_A_WIDTH):
        r = (off + kk) % 8
        base = off + kk - r
        term = rolled[r][base:base + tm] * w[kk:kk + 1, :]
        acc = term if acc is None else acc + term
    return (_silu(_ln(acc + b, g, beta)),)


def fn_resln(h, r, g, b):
    return (_ln(DEEPNORM_ALPHA * h + r, g, b),)


@functools.partial(jax.custom_vjp, nondiff_argnums=(1,))
def _roll_rows(x, shift):
    return pltpu.roll(x, shift % x.shape[0], 0)


_roll_rows.defvjp(lambda x, shift: (pltpu.roll(x, shift % x.shape[0], 0), None),
                  lambda shift, _, dy: (pltpu.roll(dy, (-shift) % dy.shape[0], 0),))


def _shift_down(x):
    row = lax.broadcasted_iota(jnp.int32, x.shape, 0)
    return jnp.where(row == 0, 0.0, _roll_rows(x, 1))


def _shift_up(x):
    row = lax.broadcasted_iota(jnp.int32, x.shape, 0)
    return jnp.where(row == x.shape[0] - 1, 0.0, _roll_rows(x, -1))


def fn_ffnconv(u1, u2, w1, w2, b1, b2):
    c1 = _shift_down(u1) * w1[0:1] + u1 * w1[1:2] + _shift_up(u1) * w1[2:3] + b1
    c2 = _shift_down(u2) * w2[0:1] + u2 * w2[1:2] + _shift_up(u2) * w2[2:3] + b2
    return (_silu(c1) * c2,)


def fn_final(h, r, t, g, b):
    y = _ln(DEEPNORM_ALPHA * h + r, g, b)
    err = (y - t) * (y - t)
    return (0.5 * jnp.sum(jnp.mean(err, axis=-1, keepdims=True), axis=0, keepdims=True),)


def _rope_tables(seq):
    n_rows = seq // GRID_W
    lane128 = jnp.arange(128)

    def tile_tables(j, rotated, half):
        inv = ROPE_THETA ** (-(j % half).astype(F32) / half)
        by_row, by_col = rotated & (j < 2 * half), rotated & (j >= 2 * half)
        sign = jnp.where(j % (2 * half) < half, -1.0, 1.0)
        ar = jnp.arange(n_rows, dtype=F32)[:, None] * inv[None, :]
        ac = jnp.arange(GRID_W, dtype=F32)[:, None] * inv[None, :]
        grid = lambda r, c: (jnp.where(by_row, r, 0.0)[:, None, :] + jnp.where(by_col, c, 0.0)[None, :, :])
        cos = grid(jnp.cos(ar), jnp.cos(ac)) + jnp.where(rotated, 0.0, 1.0)
        sin = grid(sign * jnp.sin(ar), sign * jnp.sin(ac))
        return cos.reshape(seq, 128), sin.reshape(seq, 128)

    cos_b, sin_b = tile_tables(lane128 % HEAD_DIM, lane128 >= 0, HEAD_DIM // 4)
    tab_q = jnp.concatenate([cos_b] * (GROUP_W // 128) + [sin_b] * (GROUP_W // 128), axis=-1)
    tab_d = jnp.concatenate(tile_tables(lane128 - MLA_NOPE, (lane128 >= MLA_NOPE) & (lane128 < MLA_NOPE + MLA_ROPE),
                                        MLA_ROPE // 4), axis=-1)
    lane = jnp.arange(GROUP_W)
    seg = jnp.where(lane[:, None] // HEAD_DIM == lane[None, :] // HEAD_DIM, 1.0 / HEAD_DIM, 0.0).astype(F32)
    src, dst = jnp.arange(128)[:, None], jnp.arange(MLA_HEADS * MLA_DK_PAD)[None, :]
    place = jnp.where((src < MLA_ROPE) & (dst % MLA_DK_PAD == MLA_NOPE + src), 1.0, 0.0).astype(F32)
    return tab_q, tab_d, seg, place


def _pre_ops(proj, tabs, kp, grad):
    tm = ROW_TILE
    ops = [_row_op(proj, tm, grad=grad, gdtype=MXU_DTYPE), _row_op(tabs[0], tm), _row_op(tabs[1], tm),
           _par_op(tabs[2]), _par_op(tabs[3])]
    ops += [_par_op(kp[n], grad=grad) for n in ('qng', 'kng', 'sg', 'sb', 'sw', 'sbt', 'mqn', 'wuq', 'mkvn', 'wukv')]
    return ops


PRE_OUTS = ((0, GROUP_W), (GQA_HEADS, HEAD_DIM), (GQA_KV_HEADS, HEAD_DIM), (GQA_KV_HEADS, HEAD_DIM), (0, GROUP_W),
            (MLA_HEADS, MLA_DK_PAD), (MLA_HEADS, MLA_DK_PAD), (MLA_HEADS, MLA_V))


def _pre_out_specs(seq, tm):
    specs = []
    for heads, w in PRE_OUTS:
        if heads:
            specs.append(((heads, seq, w), (heads, tm, w), lambda i: (0, i, 0)))
        else:
            specs.append(((seq, w), (tm, w), lambda i: (i, 0)))
    return specs


def pre_fwd(proj, tabs, kp, tag):
    seq = proj.shape[0]
    tm = ROW_TILE
    dts = (F32,) + (MXU_DTYPE,) * 7
    outs = [(shape, dt, block, imap) for (shape, block, imap), dt in zip(_pre_out_specs(seq, tm), dts)]
    return stage_fwd("pre_fwd" + tag, fn_pre, _pre_ops(proj, tabs, kp, False), outs, (seq // tm,))


def pre_bwd(proj, tabs, kp, cts, tag):
    seq = proj.shape[0]
    tm = ROW_TILE
    ct = [[c if isinstance(c, tuple) else (c, block, imap)] for c, (_, block, imap) in zip(cts, _pre_out_specs(seq, tm))]
    return stage_bwd("pre_bwd" + tag, fn_pre, _pre_ops(proj, tabs, kp, True), ct, (seq // tm,))


def _aconv_ops(kp, grad):
    return [_par_op(kp[n], grad=grad) for n in ('caw', 'cab', 'lag', 'lab')]


def aconv_fwd(aglu_pad, kp, tag):
    seq = aglu_pad.shape[0] - 2 * CONV_A_HALO
    tm = ROW_TILE
    n_par = 4

    def body(x_ref, *refs):
        i = pl.program_id(0)
        win = x_ref[pl.ds(pl.multiple_of(i * tm, tm), tm + 2 * CONV_A_HALO), :]
        (o,) = fn_aconv(win, *[_load(r) for r in refs[:n_par]])
        refs[n_par][...] = o.astype(refs[n_par].dtype)

    pars = _aconv_ops(kp, False)
    return pl.pallas_call(
        body, grid=(seq // tm,),
        in_specs=[pl.BlockSpec(aglu_pad.shape, lambda i: (0, 0))] + [pl.BlockSpec(o.block, o.imap) for o in pars],
        out_specs=pl.BlockSpec((tm, GROUP_W), lambda i: (i, 0)),
        out_shape=jax.ShapeDtypeStruct((seq, GROUP_W), MXU_DTYPE),
        compiler_params=pltpu.CompilerParams(dimension_semantics=("parallel",), vmem_limit_bytes=VMEM_LIMIT),
        name="aconv_fwd" + tag)(aglu_pad, *[o.arr for o in pars])


def aconv_bwd(aglu_pad, kp, d_oa, tag):
    seq = aglu_pad.shape[0] - 2 * CONV_A_HALO
    tm = ROW_TILE
    n_par = 4

    def body(x_ref, *refs):
        i = pl.program_id(0)
        rows = pl.ds(pl.multiple_of(i * tm, tm), tm + 2 * CONV_A_HALO)
        pars = [_load(r) for r in refs[:n_par]]
        ct = refs[n_par][...].astype(F32)
        outs = refs[n_par + 1:]
        _, vjp = jax.vjp(lambda *a: fn_aconv(*a), x_ref[rows, :], *pars)
        grads = vjp((ct,))

        @pl.when(i == 0)
        def _():
            for r in outs:
                r[...] = jnp.zeros_like(r)

        outs[0][rows, :] += grads[0]
        for r, g in zip(outs[1:], grads[1:]):
            r[...] += g

    pars = _aconv_ops(kp, True)
    whole = pl.BlockSpec(aglu_pad.shape, lambda i: (0, 0))
    par_specs = [pl.BlockSpec(o.block, o.imap) for o in pars]
    return pl.pallas_call(
        body, grid=(seq // tm,),
        in_specs=[whole] + par_specs + [pl.BlockSpec((tm, GROUP_W), lambda i: (i, 0))],
        out_specs=[whole] + par_specs,
        out_shape=[jax.ShapeDtypeStruct(aglu_pad.shape, F32)] + [jax.ShapeDtypeStruct(o.arr.shape, F32) for o in pars],
        compiler_params=pltpu.CompilerParams(dimension_semantics=("arbitrary",), vmem_limit_bytes=VMEM_LIMIT),
        name="aconv_bwd" + tag)(aglu_pad, *[o.arr for o in pars], d_oa)


ATTN_TQ = 256
ATTN_TK = 512


def attn_fwd(q3, k3, v3, scale, tag):
    heads, seq, dk = q3.shape
    group = heads // k3.shape[0]
    kv_per_pair = 2 // group
    dv = v3.shape[2]
    tq, tk = min(ATTN_TQ, seq), min(ATTN_TK, seq)
    n_chunks = seq // tk
    log2e = math.log2(math.e)

    def one_head(q, k_ref, v_ref):
        scores = lambda c: _rawdot(q, k_ref[pl.ds(c * tk, tk), :], 1, 1)
        m, l, acc = jnp.full((tq, 1), -jnp.inf, F32), jnp.zeros((tq, 1), F32), jnp.zeros((tq, dv), F32)
        s_next = scores(0)
        for c in range(n_chunks):
            s_cur, s_next = s_next, (scores(c + 1) if c + 1 < n_chunks else None)
            t = s_cur * (scale * log2e)
            m_new = jnp.maximum(m, jnp.max(t, axis=-1, keepdims=True))
            alpha = jnp.exp2(m - m_new)
            p = jnp.exp2(t - m_new)
            l = alpha * l + jnp.sum(p, axis=-1, keepdims=True)
            acc = alpha * acc + _rawdot(p, v_ref[pl.ds(c * tk, tk), :], 1, 0)
            m = m_new
        return acc * (1.0 / l), m * (1.0 / log2e) + jnp.log(l)

    def body(q_ref, k_ref, v_ref, o_ref, lse_ref):
        outs = []
        for h in range(2):
            o, lse = one_head(q_ref[h], k_ref.at[h // group], v_ref.at[h // group])
            lse_ref[h] = lse
            outs.append(o)
        o_ref[...] = jnp.concatenate(outs, axis=-1)

    return pl.pallas_call(
        body, grid=(heads // 2, seq // tq),
        in_specs=[pl.BlockSpec((2, tq, dk), lambda j, i: (j, i, 0)),
                  pl.BlockSpec((kv_per_pair, seq, dk), lambda j, i: (j, 0, 0)),
                  pl.BlockSpec((kv_per_pair, seq, dv), lambda j, i: (j, 0, 0))],
        out_specs=[pl.BlockSpec((tq, 2 * dv), lambda j, i: (i, j)),
                   pl.BlockSpec((2, tq, 1), lambda j, i: (j, i, 0))],
        out_shape=[jax.ShapeDtypeStruct((seq, heads * dv), F32), jax.ShapeDtypeStruct((heads, seq, 1), F32)],
        compiler_params=pltpu.CompilerParams(dimension_semantics=("parallel", "parallel"),
                                             vmem_limit_bytes=VMEM_LIMIT),
        name="attn_fwd" + tag)(q3, k3, v3)


def attn_bwd(q3, k3, v3, o, lse3, do_all, do_col, scale, tag):
    heads, seq, dk = q3.shape
    group = heads // k3.shape[0]
    kv_per_pair = 2 // group
    dv = v3.shape[2]
    tq, tk = min(ATTN_TQ, seq), min(ATTN_TK, seq)
    n_chunks = seq // tk
    log2e = math.log2(math.e)

    def one_head(q, do, o_h, lse, k_ref, v_ref, dk_ref, dv_ref):
        dob = do.astype(MXU_DTYPE)
        delta = jnp.sum(do * o_h, axis=-1, keepdims=True)
        lse2 = lse * log2e
        rows = lambda c: pl.ds(c * tk, tk)
        products = lambda c: (_rawdot(q, k_ref[rows(c), :], 1, 1), _rawdot(dob, v_ref[rows(c), :], 1, 1))
        dq = jnp.zeros((tq, dk), F32)
        nxt = products(0)
        for c in range(n_chunks):
            (s_cur, dp_cur), nxt = nxt, (products(c + 1) if c + 1 < n_chunks else None)
            p = jnp.exp2(s_cur * (scale * log2e) - lse2)
            ds = (p * ((dp_cur - delta) * scale)).astype(MXU_DTYPE)
            dv_ref[rows(c), :] += _rawdot(p, dob, 0, 0)
            dk_ref[rows(c), :] += _rawdot(ds, q, 0, 0)
            dq = dq + _rawdot(ds, k_ref[rows(c), :], 1, 0)
        return dq

    def body(q_ref, k_ref, v_ref, o_ref, lse_ref, do_ref, dq_ref, dk_ref, dv_ref):
        @pl.when(pl.program_id(1) == 0)
        def _():
            dk_ref[...] = jnp.zeros_like(dk_ref)
            dv_ref[...] = jnp.zeros_like(dv_ref)

        do_pair, o_pair = do_ref[...], o_ref[...]
        for h in range(2):
            kv = h // group
            dq_ref[h] = one_head(q_ref[h], do_pair[:, dv * h:dv * (h + 1)], o_pair[:, dv * h:dv * (h + 1)],
                                 lse_ref[h], k_ref.at[kv], v_ref.at[kv], dk_ref.at[kv], dv_ref.at[kv])

    qspec = lambda d: pl.BlockSpec((2, tq, d), lambda j, i: (j, i, 0))
    kvspec = lambda d: pl.BlockSpec((kv_per_pair, seq, d), lambda j, i: (j, 0, 0))
    return pl.pallas_call(
        body, grid=(heads // 2, seq // tq),
        in_specs=[qspec(dk), kvspec(dk), kvspec(dv), pl.BlockSpec((tq, 2 * dv), lambda j, i: (i, j)), qspec(1),
                  pl.BlockSpec((tq, 2 * dv), lambda j, i: (i, do_col + j))],
        out_specs=[qspec(dk), kvspec(dk), kvspec(dv)],
        out_shape=[jax.ShapeDtypeStruct(q3.shape, F32), jax.ShapeDtypeStruct(k3.shape, F32),
                   jax.ShapeDtypeStruct(v3.shape, F32)],
        compiler_params=pltpu.CompilerParams(dimension_semantics=("parallel", "arbitrary"),
                                             vmem_limit_bytes=VMEM_LIMIT),
        name="attn_bwd" + tag)(q3, k3, v3, o, lse3, do_all)


def resln_fwd(h, r, g, b, tag):
    seq, d = h.shape
    tm = ROW_TILE
    ops = [_row_op(h, tm), _row_op(r, tm), _par_op(g), _par_op(b)]
    outs = [((seq, d), dt, (tm, d), lambda i: (i, 0)) for dt in (F32, MXU_DTYPE)]
    return stage_fwd("resln_fwd" + tag, _twice(fn_resln), ops, outs, (seq // tm,))


def resln_bwd(h, r, g, b, dys, tag):
    seq, d = h.shape
    tm = ROW_TILE
    ops = [_row_op(h, tm, grad=True), _row_op(r, tm, grad=True, gdtype=MXU_DTYPE), _par_op(g, grad=True),
           _par_op(b, grad=True)]
    ct = [[(dy, (tm, d), lambda i: (i, 0)) for dy in dys]]
    return stage_bwd("resln_bwd" + tag, fn_resln, ops, ct, (seq // tm,))


def _ffnconv_ops(up1, up2, w, b, grad):
    seq = up1.shape[0]
    nblk = D_FF // 128
    lo, hi = (lambda j: (0, j)), (lambda j: (0, j + nblk))
    half = lambda a: dict(gshape=(a.shape[0], D_FF), gimap=lo)
    return [Op(up1, (seq, 128), lo, grad=grad, gdtype=MXU_DTYPE), Op(up2, (seq, 128), lo, grad=grad, gdtype=MXU_DTYPE),
            Op(w, (3, 128), lo, grad=grad, **half(w)), Op(w, (3, 128), hi, grad=grad, **half(w)),
            Op(b, (1, 128), lo, grad=grad, **half(b)), Op(b, (1, 128), hi, grad=grad, **half(b))]


def ffnconv_fwd(up1, up2, w, b, tag):
    seq = up1.shape[0]
    outs = [((seq, D_FF), MXU_DTYPE, (seq, 128), lambda j: (0, j))]
    return stage_fwd("ffnconv_fwd" + tag, fn_ffnconv, _ffnconv_ops(up1, up2, w, b, False), outs, (D_FF // 128,))[0]


def ffnconv_bwd(up1, up2, w, b, dact, tag):
    seq = up1.shape[0]
    ct = [[(dact, (seq, 128), lambda j: (0, j))]]
    du1, du2, dw1, dw2, db1, db2 = stage_bwd("ffnconv_bwd" + tag, fn_ffnconv, _ffnconv_ops(up1, up2, w, b, True),
                                             ct, (D_FF // 128,))
    cat = lambda a, b_: jnp.concatenate([a, b_], axis=-1)
    return du1, du2, cat(dw1, dw2), cat(db1, db2)


def final_bwd(h, r, t, g, b, tag):
    seq, d = h.shape
    tm = ROW_TILE
    ops = [_row_op(h, tm, grad=True), _row_op(r, tm, grad=True, gdtype=MXU_DTYPE), _row_op(t, tm),
           _par_op(g, grad=True), _par_op(b, grad=True)]
    return stage_bwd("final_bwd" + tag, fn_final, ops, [None], (seq // tm,), value_acc=True)


def _layer_params(wts, l):
    row = lambda a: a.reshape(1, -1)
    wuq = wts['mla_w_uq'][l].reshape(MLA_Q_LORA, MLA_HEADS, MLA_NOPE + MLA_ROPE)
    wuq = jnp.pad(wuq, ((0, CQ_PAD - MLA_Q_LORA), (0, 0), (0, MLA_DK_PAD - MLA_NOPE - MLA_ROPE)))
    wukv = wts['mla_w_ukv'][l].reshape(MLA_KV_LORA, MLA_HEADS, MLA_NOPE + MLA_V)
    wuk = jnp.pad(wukv[:, :, :MLA_NOPE], ((0, 0), (0, 0), (0, MLA_DK_PAD - MLA_NOPE)))
    return dict(
        qng=jnp.tile(row(wts['qk_norm_q'][l]), (1, GQA_HEADS)), kng=jnp.tile(row(wts['qk_norm_k'][l]), (1, GQA_KV_HEADS)),
        sg=row(wts['sgu_ln_g'][l]), sb=row(wts['sgu_ln_b'][l]),
        sw=wts['sgu_w'][l].reshape(SGU_GROUPS * CHUNK, CHUNK), sbt=wts['sgu_b'][l].T,
        mqn=jnp.pad(row(wts['mla_q_norm'][l]), ((0, 0), (0, CQ_PAD - MLA_Q_LORA))),
        wuq=wuq.reshape(CQ_PAD, MLA_HEADS * MLA_DK_PAD), mkvn=row(wts['mla_kv_norm'][l]),
        wukv=jnp.concatenate([wuk.reshape(MLA_KV_LORA, -1), wukv[:, :, MLA_NOPE:].reshape(MLA_KV_LORA, -1)], axis=1),
        caw=wts['conv_a_w'][l], cab=row(wts['conv_a_b'][l]), lag=row(wts['ln_a_g'][l]), lab=row(wts['ln_a_b'][l]),
        lmg=row(wts['ln_mix_g'][l]), lmb=row(wts['ln_mix_b'][l]),
        fcw=wts['ffn_conv_w'][l], fcb=row(wts['ffn_conv_b'][l]),
        lfg=row(wts['ln_ffn_g'][l]), lfb=row(wts['ln_ffn_b'][l]))


def _to_heads(a, heads):
    seq = a.shape[0]
    return a.reshape(seq, heads, -1).transpose(1, 0, 2)


def _from_heads(a3):
    return a3.transpose(1, 0, 2).reshape(a3.shape[1], -1)


def local_step(x, target, ln_in, get_wts, mat, hook):
    seq = x.shape[0]
    tm = ROW_TILE
    tabs = _rope_tables(seq)
    scale_b = HEAD_DIM ** -0.5
    scale_d = (MLA_NOPE + MLA_ROPE) ** -0.5
    ln_in_g, ln_in_b = ln_in[0].reshape(1, -1), ln_in[1].reshape(1, -1)

    h, h_m = stage_fwd("ln_in_fwd", _twice(fn_ln), [_row_op(x, tm), _par_op(ln_in_g), _par_op(ln_in_b)],
                       [((seq, D_MODEL), dt, (tm, D_MODEL), lambda i: (i, 0)) for dt in (F32, MXU_DTYPE)],
                       (seq // tm,))
    wts = get_wts(h_m)
    saved = []
    for l in range(DEPTH):
        tag = f"_l{l}"
        kp, unprep = jax.vjp(lambda w: _layer_params(w, l), wts)
        m = {'w_in': mat(l, 'w_in', h_m)}
        proj = matmul(h_m, m['w_in'], 'nt', F32, "mm_proj" + tag)
        aglu, q3, k3, v3, o_c, qd3, kd3, vd3 = pre_fwd(proj, tabs, kp, tag)
        aglu_pad = jnp.pad(aglu, ((CONV_A_HALO, CONV_A_HALO), (0, 0)))
        o_a = aconv_fwd(aglu_pad, kp, tag)
        o_b3, lse_b3 = attn_fwd(q3, k3, v3, scale_b, "_b" + tag)
        o_d3, lse_d3 = attn_fwd(qd3, kd3, vd3, scale_d, "_d" + tag)
        o_cat = jnp.concatenate([o_a, o_b3.astype(MXU_DTYPE), o_c, o_d3.astype(MXU_DTYPE)], axis=-1)
        m['w_out'] = mat(l, 'w_out', o_cat)
        mix = matmul(o_cat, m['w_out'], 'nn', F32, "mm_mix" + tag)
        h1, h1_m = resln_fwd(h, mix, kp['lmg'], kp['lmb'], "_mix" + tag)
        m['ffn_w_up'] = mat(l, 'ffn_w_up', h1_m)
        up1 = matmul(h1_m, m['ffn_w_up'], 'nt', F32, "mm_up1" + tag, b_rows=(0, D_FF))
        up2 = matmul(h1_m, m['ffn_w_up'], 'nt', F32, "mm_up2" + tag, b_rows=(D_FF, D_FF))
        act = ffnconv_fwd(up1, up2, kp['fcw'], kp['fcb'], tag)
        m['ffn_w_down'] = mat(l, 'ffn_w_down', act)
        f = matmul(act, m['ffn_w_down'], 'nn', F32, "mm_down" + tag)
        saved.append(dict(kp=kp, unprep=unprep, m=m, h=h, h_m=h_m, h1_m=h1_m, proj=proj, o_b3=o_b3, lse_b3=lse_b3,
                          o_d3=o_d3, lse_d3=lse_d3, aglu_pad=aglu_pad, q3=q3, k3=k3, v3=v3, qd3=qd3,
                          kd3=kd3, vd3=vd3, o_cat=o_cat, mix=mix, h1=h1, up1=up1, up2=up2, act=act, f=f))
        if l + 1 < DEPTH:
            h, h_m = resln_fwd(h1, f, kp['lfg'], kp['lfb'], "_ffn" + tag)

    after = lambda a, tok: a if tok is None else a + tok
    small_acc = None
    dh_parts = None
    loss = None
    tok = None
    g_mix = None
    for l in reversed(range(DEPTH)):
        tag = f"_l{l}"
        s = saved[l]
        kp, m = s['kp'], s['m']
        dkp = {}
        lfg = after(kp['lfg'], tok)
        if l == DEPTH - 1:
            dh1_a, df, dkp['lfg'], dkp['lfb'], loss = final_bwd(s['h1'], s['f'], target, lfg, kp['lfb'], tag)
        else:
            dh1_a, df, dkp['lfg'], dkp['lfb'] = resln_bwd(s['h1'], s['f'], lfg, kp['lfb'], dh_parts, "_ffn" + tag)
        g_down = matmul(s['act'], df, 'tn', COMM_DTYPE, "mm_gdown" + tag)
        dact = matmul(df, m['ffn_w_down'], 'nt', F32, "mm_dact" + tag)
        dup1, dup2, dkp['fcw'], dkp['fcb'] = ffnconv_bwd(s['up1'], s['up2'], kp['fcw'], kp['fcb'], dact, tag)
        dh1_b1 = matmul(dup1, m['ffn_w_up'], 'nn', F32, "mm_dh1a" + tag, b_rows=(0, D_FF))
        dh1_b2 = matmul(dup2, m['ffn_w_up'], 'nn', F32, "mm_dh1b" + tag, b_rows=(D_FF, D_FF))
        g_up = matmul(dup1, s['h1_m'], 'tn', COMM_DTYPE, "mm_gup1" + tag,
                      into=(lax.empty((2 * D_FF, D_MODEL), COMM_DTYPE), 0))
        g_up = matmul(dup2, s['h1_m'], 'tn', COMM_DTYPE, "mm_gup2" + tag, into=(g_up, D_FF))
        tok = hook(f"ffn{l}", {('ffn_w_down', l): g_down, ('ffn_w_up', l): g_up})
        dh_a, dmix, dkp['lmg'], dkp['lmb'] = resln_bwd(s['h'], s['mix'], after(kp['lmg'], tok), kp['lmb'],
                                                       [dh1_a, dh1_b1, dh1_b2], "_mix" + tag)
        g_out = matmul(s['o_cat'], dmix, 'tn', COMM_DTYPE, "mm_gout" + tag)
        w_out = m['w_out']
        if l == 0:
            w_out = w_out + hook("out0", {('w_out', l): g_out}).astype(w_out.dtype)
        do_cat = matmul(dmix, w_out, 'nt', F32, "mm_docat" + tag)
        lse_b3 = s['lse_b3']
        do_c = (do_cat, (ROW_TILE, GROUP_W), lambda i: (i, 2))
        pair_w = 2 * HEAD_DIM
        dq3, dk3, dv3 = attn_bwd(s['q3'], s['k3'], s['v3'], s['o_b3'], lse_b3, do_cat, GROUP_W // pair_w,
                                 scale_b, "_b" + tag)
        dqd3, dkd3, dvd3 = attn_bwd(s['qd3'], s['kd3'], s['vd3'], s['o_d3'], s['lse_d3'], do_cat,
                                    3 * GROUP_W // pair_w, scale_d, "_d" + tag)
        daglu_pad, dkp['caw'], dkp['cab'], dkp['lag'], dkp['lab'] = aconv_bwd(s['aglu_pad'], kp, do_cat, tag)
        cts = [daglu_pad[CONV_A_HALO:CONV_A_HALO + seq], dq3, dk3, dv3, do_c, dqd3, dkd3, dvd3]
        pre_g = pre_bwd(s['proj'], tabs, kp, cts, tag)
        dproj = pre_g[0]
        for n, g in zip(('qng', 'kng', 'sg', 'sb', 'sw', 'sbt', 'mqn', 'wuq', 'mkvn', 'wukv'), pre_g[1:]):
            dkp[n] = g
        dh_b = matmul(dproj, m['w_in'], 'nn', F32, "mm_dh" + tag)
        g_in = matmul(dproj, s['h_m'], 'tn', COMM_DTYPE, "mm_gin" + tag)
        dh_parts = [dh_a, dh_b]
        (dw,) = s['unprep'](dkp)
        small_acc = dw if small_acc is None else jax.tree.map(jnp.add, small_acc, dw)
        g_mix = {('w_out', l): g_out, ('w_in', l): _unpad_w_in(g_in)}
        if l > 0:
            tok = hook(f"mix{l}", g_mix)
        else:
            g_mix.pop(('w_out', l))

    g_mix.update({(n, None): small_acc[n] for n in SHARDED if n not in MATMUL_WEIGHTS})
    tok = hook("last", g_mix)
    dx, dg, db = stage_bwd("ln_in_bwd", fn_ln,
                           [_row_op(x, tm, grad=True), _par_op(after(ln_in_g, tok), grad=True),
                            _par_op(ln_in_b, grad=True)],
                           [[(p, (tm, D_MODEL), lambda i: (i, 0)) for p in dh_parts]], (seq // tm,))
    out = {n: small_acc[n] for n in REPLICATED}
    out['ln_in_g'], out['ln_in_b'] = dg.reshape(-1), db.reshape(-1)
    return loss, dx, out


def _peer(x, y, c, r):
    return ((1 - x) if r & 4 else x, (1 - y) if r & 2 else y, (1 - c) if r & 1 else c)


def _exchange_copy(src_ref, land_ref, send_sems, recv_sems, k, gather, x, y, c, r):
    px, py, pc = _peer(x, y, c, r)
    me, peer = 4 * x + 2 * y + c, 4 * px + 2 * py + pc
    src = src_ref if gather else src_ref.at[peer]
    mk = lambda dst: pltpu.make_async_remote_copy(
        src_ref=src, dst_ref=dst, send_sem=send_sems.at[k * (N_DEV - 1) + r - 1],
        recv_sem=recv_sems.at[k * (N_DEV - 1) + r - 1],
        device_id=(px, py, pc), device_id_type=pl.DeviceIdType.MESH)
    return mk(land_ref.at[me]), mk(land_ref.at[peer])


_HBM_SPEC = pl.BlockSpec(memory_space=pltpu.HBM)
_SEM_SPEC = pl.BlockSpec(memory_space=pltpu.SEMAPHORE)


def exchange_start(srcs, gather, groups, name):
    n_t = len(srcs)
    lands =[lax.empty(((N_DEV,) + s.shape) if gt else s.shape, s.dtype) for s, gt in zip(srcs, gather)]

    def body(*refs):
        src_refs, land_refs = refs[:n_t], refs[n_t:2 * n_t]
        sem_refs = refs[2 * n_t:2 * n_t + 2 * len(groups)]
        token = refs[-1]
        x, y, c = lax.axis_index("x"), lax.axis_index("y"), lax.axis_index("c")
        for gi, g in enumerate(groups):
            for k, t in enumerate(g):
                for r in range(1, N_DEV):
                    _exchange_copy(src_refs[t], land_refs[t], sem_refs[2 * gi], sem_refs[2 * gi + 1], k, gather[t],
                                   x, y, c, r)[0].start()
        token[...] = jnp.zeros_like(token)

    sem_shapes = []
    for g in groups:
        sem_shapes += [pltpu.SemaphoreType.DMA((len(g) * (N_DEV - 1),))] * 2
    hbm_shapes = [pltpu.HBM(a.shape, a.dtype) for a in list(srcs) + lands]
    n_sem = len(sem_shapes)
    res = pl.pallas_call(
        body, name=name,
        out_shape=tuple(sem_shapes + hbm_shapes + [jax.ShapeDtypeStruct((8, 128), F32)]),
        in_specs=[_HBM_SPEC] * (2 * n_t),
        out_specs=tuple([_SEM_SPEC] * n_sem + [_HBM_SPEC] * (2 * n_t) + [pl.BlockSpec(memory_space=pltpu.VMEM)]),
        input_output_aliases={i: n_sem + i for i in range(2 * n_t)},
        compiler_params=pltpu.CompilerParams(has_side_effects=pltpu.SideEffectType.DATAFLOW_SIDE_EFFECTING),
    )(*[pltpu.with_memory_space_constraint(a, pltpu.HBM) for a in list(srcs) + lands])
    sems = [(res[2 * gi], res[2 * gi + 1]) for gi in range(len(groups))]
    return sems, list(res[n_sem:n_sem + n_t]), list(res[n_sem + n_t:n_sem + 2 * n_t]), res[-1]


def exchange_wait(sems, srcs, lands, gather, after, name):
    n_t = len(srcs)

    def body(*refs):
        src_refs, land_refs = refs[:n_t], refs[n_t:2 * n_t]
        send_sems, recv_sems = refs[2 * n_t], refs[2 * n_t + 1]
        x, y, c = lax.axis_index("x"), lax.axis_index("y"), lax.axis_index("c")
        for k in range(n_t):
            for r in range(1, N_DEV):
                send, recv = _exchange_copy(src_refs[k], land_refs[k], send_sems, recv_sems, k, gather[k], x, y, c, r)
                send.wait_send()
                recv.wait_recv()

    res = pl.pallas_call(
        body, name=name,
        out_shape=tuple(pltpu.HBM(a.shape, a.dtype) for a in list(srcs) + list(lands)),
        in_specs=[_HBM_SPEC] * (2 * n_t) + [_SEM_SPEC, _SEM_SPEC, pl.BlockSpec(memory_space=pl.ANY)],
        out_specs=tuple([_HBM_SPEC] * (2 * n_t)),
        input_output_aliases={i: i for i in range(2 * n_t)},
        compiler_params=pltpu.CompilerParams(has_side_effects=pltpu.SideEffectType.DATAFLOW_SIDE_EFFECTING),
    )(*srcs, *lands, sems[0], sems[1], after)
    return list(res[:n_t]), list(res[n_t:])


def adamw(parts, w, m, v, name):
    n_l, n_r, n_c = w.shape
    tr = n_r
    if n_r % 8 == 0:
        for cand in (512, 256, 128, 64, 32, 16, 8):
            if n_r % cand == 0 and cand * n_c * 4 <= 512 * 1024:
                tr = cand
                break
    c1 = 1.0 - ADAM_B1 ** ADAM_STEP
    c2 = 1.0 - ADAM_B2 ** ADAM_STEP
    per_layer = isinstance(parts, (list, tuple))
    n_p = n_l if per_layer else 1
    n_rb = n_r // tr

    def update(g, w_ref, m_ref, v_ref, g_ref, d_ref, nm_ref, nv_ref):
        w_, m_, v_ = w_ref[0], m_ref[0], v_ref[0]
        nm = ADAM_B1 * m_ + (1.0 - ADAM_B1) * g
        nv = ADAM_B2 * v_ + (1.0 - ADAM_B2) * (g * g)
        g_ref[0] = g
        nm_ref[0] = nm
        nv_ref[0] = nv
        d_ref[0] = -ADAM_LR * ((nm / c1) / (jnp.sqrt(nv / c2) + ADAM_EPS) + ADAM_WD * w_)

    def body(*refs):
        p_refs, rest = refs[:n_p], refs[n_p:]
        if not per_layer:
            g = p_refs[0][0, 0].astype(F32)
            for s in range(1, N_DEV):
                g = g + p_refs[0][s, 0].astype(F32)
            update(g, *rest)
        else:
            for lay in range(n_l):
                @pl.when(pl.program_id(0) == lay)
                def _(lay=lay):
                    g = p_refs[lay][0].astype(F32)
                    for s in range(1, N_DEV):
                        g = g + p_refs[lay][s].astype(F32)
                    update(g, *rest)

    blk = pl.BlockSpec((1, tr, n_c), lambda l, r: (l, r, 0))
    if per_layer:
        def p_spec(lay):
            park = 0 if lay > 0 else n_rb - 1
            return pl.BlockSpec((N_DEV, tr, n_c), lambda l, r: (0, jnp.where(l == lay, r, park), 0))
        p_specs, p_args = [p_spec(lay) for lay in range(n_l)], list(parts)
    else:
        p_specs, p_args = [pl.BlockSpec((N_DEV, 1, tr, n_c), lambda l, r: (0, l, r, 0))], [parts]
    return pl.pallas_call(
        body, grid=(n_l, n_rb), in_specs=p_specs + [blk, blk, blk],
        out_specs=[blk] * 4, out_shape=[jax.ShapeDtypeStruct(w.shape, F32)] * 4,
        compiler_params=pltpu.CompilerParams(dimension_semantics=("arbitrary", "arbitrary"),
                                             vmem_limit_bytes=VMEM_LIMIT),
        name=name)(*p_args, w, m, v)


def adamw_replicated(lands, own, ws, ms, vs, loss_land, loss_own):
    n_t = len(lands)
    c1 = 1.0 - ADAM_B1 ** ADAM_STEP
    c2 = 1.0 - ADAM_B2 ** ADAM_STEP

    def body(*refs):
        ins, outs = refs[:5 * n_t + 2], refs[5 * n_t + 2:]
        me = 4 * lax.axis_index("x") + 2 * lax.axis_index("y") + lax.axis_index("c")

        def total(land_ref, own_ref):
            g = None
            for s in range(N_DEV):
                term = jnp.where(me == s, own_ref[...], land_ref[s])
                g = term if g is None else g + term
            return g

        for t in range(n_t):
            land_ref, own_ref, w_ref, m_ref, v_ref = ins[5 * t:5 * t + 5]
            g = total(land_ref, own_ref)
            nm = ADAM_B1 * m_ref[...] + (1.0 - ADAM_B1) * g
            nv = ADAM_B2 * v_ref[...] + (1.0 - ADAM_B2) * (g * g)
            g_ref, d_ref, nm_ref, nv_ref = outs[4 * t:4 * t + 4]
            g_ref[...] = g
            nm_ref[...] = nm
            nv_ref[...] = nv
            d_ref[...] = -ADAM_LR * ((nm / c1) / (jnp.sqrt(nv / c2) + ADAM_EPS) + ADAM_WD * w_ref[...])
        outs[4 * n_t][...] = total(ins[5 * n_t], ins[5 * n_t + 1])

    args = []
    for t in range(n_t):
        args += [lands[t], own[t], ws[t], ms[t], vs[t]]
    out_shape = []
    for t in range(n_t):
        out_shape += [jax.ShapeDtypeStruct(ws[t].shape, F32)] * 4
    out_shape.append(jax.ShapeDtypeStruct(loss_own.shape, F32))
    res = pl.pallas_call(body, out_shape=out_shape,
                         compiler_params=pltpu.CompilerParams(vmem_limit_bytes=VMEM_LIMIT),
                         name="adamw_replicated")(*args, loss_land, loss_own)
    return [tuple(res[4 * t:4 * t + 4]) for t in range(n_t)], res[-1]


def _shard_slots(g, axis):
    if axis == 1:
        return g.reshape(g.shape[0], N_DEV, g.shape[1] // N_DEV, g.shape[2]).transpose(1, 0, 2, 3)
    return g.reshape(g.shape[0], g.shape[1], N_DEV, g.shape[2] // N_DEV).transpose(2, 0, 1, 3)


def _unshard(slots, axis):
    if axis == 1:
        return slots.transpose(1, 0, 2, 3).reshape(slots.shape[1], -1, slots.shape[3])
    return slots.transpose(1, 2, 0, 3).reshape(slots.shape[1], slots.shape[2], -1)


def kernel(x, ln_in_g, ln_in_b, w_in, conv_a_w, conv_a_b, ln_a_g, ln_a_b, qk_norm_q, qk_norm_k, sgu_ln_g, sgu_ln_b, sgu_w, sgu_b, mla_q_norm, mla_w_uq, mla_kv_norm, mla_w_ukv, w_out, ln_mix_g, ln_mix_b, ffn_w_up, ffn_conv_w, ffn_conv_b, ffn_w_down, ln_ffn_g, ln_ffn_b, loss_target, m_ln_in_g, m_ln_in_b, m_w_in, m_conv_a_w, m_conv_a_b, m_ln_a_g, m_ln_a_b, m_qk_norm_q, m_qk_norm_k, m_sgu_ln_g, m_sgu_ln_b, m_sgu_w, m_sgu_b, m_mla_q_norm, m_mla_w_uq, m_mla_kv_norm, m_mla_w_ukv, m_w_out, m_ln_mix_g, m_ln_mix_b, m_ffn_w_up, m_ffn_conv_w, m_ffn_conv_b, m_ffn_w_down, m_ln_ffn_g, m_ln_ffn_b, v_ln_in_g, v_ln_in_b, v_w_in, v_conv_a_w, v_conv_a_b, v_ln_a_g, v_ln_a_b, v_qk_norm_q, v_qk_norm_k, v_sgu_ln_g, v_sgu_ln_b, v_sgu_w, v_sgu_b, v_mla_q_norm, v_mla_w_uq, v_mla_kv_norm, v_mla_w_ukv, v_w_out, v_ln_mix_g, v_ln_mix_b, v_ffn_w_up, v_ffn_conv_w, v_ffn_conv_b, v_ffn_w_down, v_ln_ffn_g, v_ln_ffn_b):
    local = dict(ln_in_g=ln_in_g, ln_in_b=ln_in_b, w_in=w_in, conv_a_w=conv_a_w, conv_a_b=conv_a_b, ln_a_g=ln_a_g, ln_a_b=ln_a_b, qk_norm_q=qk_norm_q, qk_norm_k=qk_norm_k, sgu_ln_g=sgu_ln_g, sgu_ln_b=sgu_ln_b, sgu_w=sgu_w, sgu_b=sgu_b, mla_q_norm=mla_q_norm, mla_w_uq=mla_w_uq, mla_kv_norm=mla_kv_norm, mla_w_ukv=mla_w_ukv, w_out=w_out, ln_mix_g=ln_mix_g, ln_mix_b=ln_mix_b, ffn_w_up=ffn_w_up, ffn_conv_w=ffn_conv_w, ffn_conv_b=ffn_conv_b, ffn_w_down=ffn_w_down, ln_ffn_g=ln_ffn_g, ln_ffn_b=ln_ffn_b)
    mom = dict(ln_in_g=m_ln_in_g, ln_in_b=m_ln_in_b, w_in=m_w_in, conv_a_w=m_conv_a_w, conv_a_b=m_conv_a_b, ln_a_g=m_ln_a_g, ln_a_b=m_ln_a_b, qk_norm_q=m_qk_norm_q, qk_norm_k=m_qk_norm_k, sgu_ln_g=m_sgu_ln_g, sgu_ln_b=m_sgu_ln_b, sgu_w=m_sgu_w, sgu_b=m_sgu_b, mla_q_norm=m_mla_q_norm, mla_w_uq=m_mla_w_uq, mla_kv_norm=m_mla_kv_norm, mla_w_ukv=m_mla_w_ukv, w_out=m_w_out, ln_mix_g=m_ln_mix_g, ln_mix_b=m_ln_mix_b, ffn_w_up=m_ffn_w_up, ffn_conv_w=m_ffn_conv_w, ffn_conv_b=m_ffn_conv_b, ffn_w_down=m_ffn_w_down, ln_ffn_g=m_ln_ffn_g, ln_ffn_b=m_ln_ffn_b)
    var = dict(ln_in_g=v_ln_in_g, ln_in_b=v_ln_in_b, w_in=v_w_in, conv_a_w=v_conv_a_w, conv_a_b=v_conv_a_b, ln_a_g=v_ln_a_g, ln_a_b=v_ln_a_b, qk_norm_q=v_qk_norm_q, qk_norm_k=v_qk_norm_k, sgu_ln_g=v_sgu_ln_g, sgu_ln_b=v_sgu_ln_b, sgu_w=v_sgu_w, sgu_b=v_sgu_b, mla_q_norm=v_mla_q_norm, mla_w_uq=v_mla_w_uq, mla_kv_norm=v_mla_kv_norm, mla_w_ukv=v_mla_w_ukv, w_out=v_w_out, ln_mix_g=v_ln_mix_g, ln_mix_b=v_ln_mix_b, ffn_w_up=v_ffn_w_up, ffn_conv_w=v_ffn_conv_w, ffn_conv_b=v_ffn_conv_b, ffn_w_down=v_ffn_w_down, ln_ffn_g=v_ln_ffn_g, ln_ffn_b=v_ln_ffn_b)

    me = 4 * lax.axis_index("x") + 2 * lax.axis_index("y") + lax.axis_index("c")

    def own_slot(slots, block):
        return lax.dynamic_update_slice(slots, block[None], (me,) + (0,) * block.ndim)

    small_sharded = [n for n in SHARDED if n not in MATMUL_WEIGHTS]
    big_order = [(n, l) for l in range(DEPTH) for n in MATMUL_WEIGHTS]
    send_view = lambda n, l: (local[n].transpose(0, 2, 1)[l] if n in TRANSPOSED else local[n][l]).astype(COMM_DTYPE)
    srcs = [send_view(*big_order[0])] + [local[n] for n in small_sharded]
    srcs += [send_view(n, l) for (n, l) in big_order[1:]]
    n_first = 1 + len(small_sharded)
    groups = [list(range(n_first))] + [[n_first + j] for j in range(len(big_order) - 1)]
    g_sems, g_srcs, g_lands, tok0 = exchange_start(srcs, [True] * len(srcs), groups, "gather_start")
    tok0 = tok0[0, 0]
    pending = {key: gi for gi, key in enumerate(big_order)}

    def finish(gi, after):
        idx = groups[gi]
        _, lands = exchange_wait(g_sems[gi], [g_srcs[t] for t in idx], [g_lands[t] for t in idx], [True] * len(idx),
                                 after, f"gather_wait{gi}")
        return [own_slot(ld, srcs[t]) for ld, t in zip(lands, idx)]

    first = []

    def get_wts(after):
        first.extend(finish(0, after))
        wts = {n: local[n] for n in REPLICATED}
        for n, slots in zip(small_sharded, first[1:]):
            wts[n] = _unshard(slots, SHARDED[n])
        return wts

    def unshard_layer(slots, n):
        if SHARDED[n] == 1 or n in TRANSPOSED:
            return slots.reshape(-1, slots.shape[2])
        return slots.transpose(1, 0, 2).reshape(slots.shape[1], -1)

    def mat(l, n, after):
        gi = pending[(n, l)]
        slots = first[0] if gi == 0 else finish(gi, after)[0]
        w = unshard_layer(slots, n).astype(MXU_DTYPE)
        return _pad_w_in(w) if n == 'w_in' else w

    started = []

    def hook(key, grads):
        tensors = []
        for (n, l), g in grads.items():
            if l is None:
                tensors.append(((n, l), _shard_slots(g, SHARDED[n])))
            elif n in TRANSPOSED:
                tensors.append(((n, l), g.reshape(N_DEV, g.shape[0] // N_DEV, g.shape[1]).astype(COMM_DTYPE)))
            else:
                tensors.append(((n, l), _shard_slots(g[None], SHARDED[n])[:, 0].astype(COMM_DTYPE)))
        sems, s_srcs, s_lands, tok = exchange_start([a for _, a in tensors], [False] * len(tensors),
                                                    [list(range(len(tensors)))], "scatter_start_" + key)
        started.append((key, [k for k, _ in tensors], sems[0], s_srcs, s_lands))
        return tok[0, 0]

    loss, dx, grads = local_step(x[0], loss_target[0], (local['ln_in_g'] + tok0, local['ln_in_b']), get_wts, mat, hook)

    as2d = lambda a: a.reshape(-1, a.shape[-1]) if a.ndim > 1 else a.reshape(1, -1)
    small_g = [as2d(grads[n]) for n in REPLICATED] + [jnp.broadcast_to(loss, (8, 128))]
    p_sems, p_srcs, p_lands, p_tok = exchange_start(small_g, [True] * len(small_g), [list(range(len(small_g)))],
                                                    "gather_small_start")

    parts, res = {}, {}

    def finish_scatter(entries, after):
        for key, keys, sems, s_srcs, s_lands in entries:
            s_out, lands = exchange_wait(sems, s_srcs, s_lands, [False] * len(keys), after, "scatter_wait_" + key)
            for k, so, ld in zip(keys, s_out, lands):
                parts[k] = own_slot(ld, lax.dynamic_index_in_dim(so, me, 0, keepdims=False))

    def update(names_):
        for n in names_:
            p = [parts[(n, l)] for l in range(DEPTH)] if n in MATMUL_WEIGHTS else parts[(n, None)]
            view = (lambda a: a.transpose(0, 2, 1)) if n in TRANSPOSED else (lambda a: a)
            res[n] = tuple(view(a) for a in adamw(p, view(local[n]), view(mom[n]), view(var[n]), "adamw_" + n))

    early = ('ffn_w_up', 'ffn_w_down', 'w_out')
    finish_scatter([e for e in started if e[0] != "last"], p_tok)
    update(early)
    finish_scatter([e for e in started if e[0] == "last"], res[early[-1]][1])
    update([n for n in SHARDED if n not in early])
    updated = jnp.zeros((8, 128), F32) + sum(res[n][1][0, 0, 0] for n in SHARDED)
    p_own, p_lands = exchange_wait(p_sems[0], p_srcs, p_lands, [True] * len(small_g), updated, "gather_small_wait")
    small, loss_sum = adamw_replicated(p_lands[:-1], p_own[:-1], [as2d(local[n]) for n in REPLICATED],
                                       [as2d(mom[n]) for n in REPLICATED], [as2d(var[n]) for n in REPLICATED],
                                       p_lands[-1], p_own[-1])
    for n, quad in zip(REPLICATED, small):
        res[n] = tuple(a.reshape(local[n].shape) for a in quad)
    loss_total = loss_sum[0, 0]

    return (loss_total, dx[None], *[res[n][0] for n in WEIGHTS], *[res[n][1] for n in WEIGHTS],
            *[res[n][2] for n in WEIGHTS], *[res[n][3] for n in WEIGHTS])
```

```python
import functools
import math

import jax
import jax.numpy as jnp
from jax import lax
from jax.experimental import pallas as pl
from jax.experimental.pallas import tpu as pltpu

F32 = jnp.float32
MXU_DTYPE = jnp.bfloat16
COMM_DTYPE = jnp.bfloat16

N_DEV = 8
D_MODEL = 1024
DEPTH = 2
GRID_W = 64
GROUP_W = 256
HEAD_DIM = 64
CONV_A_WIDTH = 31
CONV_A_HALO = 16
GQA_HEADS = 4
GQA_KV_HEADS = 2
CHUNK = 128
SGU_GROUPS = 4
MLA_HEADS = 4
MLA_Q_LORA = 192
MLA_KV_LORA = 128
MLA_NOPE = 64
MLA_ROPE = 32
MLA_V = 64
MLA_DK_PAD = 128
ROPE_THETA = 10000.0
D_FF = 2816
DEEPNORM_ALPHA = (2 * DEPTH) ** 0.25
LN_EPS = 1e-5
RMS_EPS = 1e-6
D_IN_PROJ = 1888

ADAM_LR = 0.001
ADAM_B1 = 0.9
ADAM_B2 = 0.999
ADAM_EPS = 1e-08
ADAM_WD = 0.01
ADAM_STEP = 10

WEIGHTS = ['ln_in_g', 'ln_in_b', 'w_in', 'conv_a_w', 'conv_a_b', 'ln_a_g', 'ln_a_b', 'qk_norm_q', 'qk_norm_k',
           'sgu_ln_g', 'sgu_ln_b', 'sgu_w', 'sgu_b', 'mla_q_norm', 'mla_w_uq', 'mla_kv_norm', 'mla_w_ukv', 'w_out',
           'ln_mix_g', 'ln_mix_b', 'ffn_w_up', 'ffn_conv_w', 'ffn_conv_b', 'ffn_w_down', 'ln_ffn_g', 'ln_ffn_b']
SHARDED = {'w_in': 2, 'conv_a_w': 2, 'mla_w_uq': 2, 'mla_w_ukv': 2, 'w_out': 1, 'ffn_w_up': 2, 'ffn_conv_w': 2,
           'ffn_w_down': 1}
MATMUL_WEIGHTS = ('w_in', 'w_out', 'ffn_w_up', 'ffn_w_down')
TRANSPOSED = ('w_in', 'ffn_w_up')
REPLICATED = [n for n in WEIGHTS if n not in SHARDED]

ROW_TILE = 256
VMEM_LIMIT = 56 * 1024 * 1024


def _rawdot(a, b, ca, cb):
    return lax.dot_general(a.astype(MXU_DTYPE), b.astype(MXU_DTYPE), (((ca,), (cb,)), ((), ())),
                           preferred_element_type=F32)


@jax.custom_vjp
def mm_nn(a, b):
    return _rawdot(a, b, 1, 0)


def _mm_nn_fwd(a, b):
    return _rawdot(a, b, 1, 0), (a, b)


def _mm_nn_bwd(res, dy):
    a, b = res
    return _rawdot(dy, b, 1, 1), _rawdot(a, dy, 0, 0)


mm_nn.defvjp(_mm_nn_fwd, _mm_nn_bwd)


@jax.custom_vjp
def mm_nt(a, b):
    return _rawdot(a, b, 1, 1)


def _mm_nt_fwd(a, b):
    return _rawdot(a, b, 1, 1), (a, b)


def _mm_nt_bwd(res, dy):
    a, b = res
    return _rawdot(dy, b, 1, 0), _rawdot(dy, a, 0, 0)


mm_nt.defvjp(_mm_nt_fwd, _mm_nt_bwd)


def _pick_tile(d, cands):
    for c in cands:
        if d % c == 0:
            return c
    return d


def matmul(a, b, mode, out_dtype, name, b_rows=None, into=None):
    b_start, b_size = (0, b.shape[0]) if b_rows is None else b_rows
    if mode == 'nn':
        (m, k), (k2, n) = a.shape, (b_size, b.shape[1])
    elif mode == 'nt':
        (m, k), (n, k2) = a.shape, (b_size, b.shape[1])
    else:
        (k, m), (k2, n) = a.shape, (b_size, b.shape[1])
    assert k == k2, (a.shape, b.shape, mode)
    tm = _pick_tile(m, (1024, 1408, 512, 256, 128))
    tn = _pick_tile(n, (512, 1408, 256, 128))
    tk = _pick_tile(k, (2816, 2048, 1024, 512, 256, 128))
    nk = k // tk
    ca = 0 if mode == 'tn' else 1
    cb = 1 if mode == 'nt' else 0
    b_blk = tn if mode == 'nt' else tk
    assert b_start % b_blk == 0, (b_rows, b_blk)
    b_off = b_start // b_blk
    a_spec = pl.BlockSpec((tk, tm), lambda i, j, kk: (kk, i)) if mode == 'tn' else pl.BlockSpec((tm, tk), lambda i, j, kk: (i, kk))
    b_spec = (pl.BlockSpec((tn, tk), lambda i, j, kk: (j + b_off, kk)) if mode == 'nt'
              else pl.BlockSpec((tk, tn), lambda i, j, kk: (kk + b_off, j)))
    in_specs, args, aliases = [a_spec, b_spec], [a, b], {}
    out_off, out_shape = 0, jax.ShapeDtypeStruct((m, n), out_dtype)
    if into is not None:
        buf, row = into
        assert row % tm == 0 and buf.shape[1] == n and buf.dtype == out_dtype, (buf.shape, row, tm)
        out_off, out_shape = row // tm, jax.ShapeDtypeStruct(buf.shape, buf.dtype)
        in_specs, args, aliases = in_specs + [pl.BlockSpec(memory_space=pl.ANY)], args + [buf], {2: 0}

    def body(a_ref, b_ref, *rest):
        o_ref, acc_ref = rest[-2:]
        kk = pl.program_id(2)

        @pl.when(kk == 0)
        def _():
            acc_ref[...] = jnp.zeros_like(acc_ref)

        acc_ref[...] += _rawdot(a_ref[...], b_ref[...], ca, cb)

        @pl.when(kk == nk - 1)
        def _():
            o_ref[...] = acc_ref[...].astype(o_ref.dtype)

    return pl.pallas_call(
        body, grid=(m // tm, n // tn, nk), in_specs=in_specs,
        out_specs=pl.BlockSpec((tm, tn), lambda i, j, kk: (i + out_off, j)),
        out_shape=out_shape, input_output_aliases=aliases,
        scratch_shapes=[pltpu.VMEM((tm, tn), F32)],
        compiler_params=pltpu.CompilerParams(dimension_semantics=("parallel", "parallel", "arbitrary"),
                                             vmem_limit_bytes=VMEM_LIMIT),
        name=name)(*args)


class Op:
    def __init__(self, arr, block, imap, grad=False, acc=False, first=None, gdtype=F32, gshape=None, gimap=None):
        self.arr, self.block, self.imap = arr, block, imap
        self.grad, self.acc, self.first, self.gdtype = grad, acc, first, gdtype
        self.gshape = arr.shape if gshape is None else gshape
        self.gimap = imap if gimap is None else gimap


def _row_op(arr, tm, grad=False, gdtype=F32):
    return Op(arr, (tm, arr.shape[1]), lambda i: (i, 0), grad=grad, gdtype=gdtype)


def _par_op(arr, grad=False):
    nd = arr.ndim
    return Op(arr, arr.shape, lambda i: (0,) * nd, grad=grad, acc=True, first=lambda ids: ids[0] == 0)


def _load(ref):
    v = ref[...]
    return v.astype(F32) if jnp.issubdtype(v.dtype, jnp.floating) else v


def _store_heads(ref, val):
    if len(ref.shape) == 2:
        ref[...] = val.astype(ref.dtype)
    else:
        d = ref.shape[2]
        for h in range(ref.shape[0]):
            ref[h] = val[:, d * h:d * (h + 1)].astype(ref.dtype)


def _load_heads(ref):
    if len(ref.shape) == 2:
        return ref[...].astype(F32)
    return jnp.concatenate([ref[h].astype(F32) for h in range(ref.shape[0])], axis=-1)


def stage_fwd(name, fn, ops, outs, grid):
    n_in = len(ops)

    def body(*refs):
        res = fn(*[_load(r) for r in refs[:n_in]])
        for r, o in zip(refs[n_in:], res):
            _store_heads(r, o)

    return pl.pallas_call(
        body, grid=grid, in_specs=[pl.BlockSpec(o.block, o.imap) for o in ops],
        out_specs=[pl.BlockSpec(b, im) for (_, _, b, im) in outs],
        out_shape=[jax.ShapeDtypeStruct(s, d) for (s, d, _, _) in outs],
        compiler_params=pltpu.CompilerParams(dimension_semantics=("parallel",) * len(grid),
                                             vmem_limit_bytes=VMEM_LIMIT),
        name=name)(*[o.arr for o in ops])


def stage_bwd(name, fn, ops, cts, grid, value_acc=False):
    n_in = len(ops)
    ct_flat = [c for group in cts if group is not None for c in group]
    n_ct = len(ct_flat)
    diff = [i for i, o in enumerate(ops) if o.grad]
    any_acc = value_acc or any(ops[i].acc for i in diff)
    ngrid = len(grid)

    def body(*refs):
        ids = [pl.program_id(a) for a in range(ngrid)]
        vals = [_load(r) for r in refs[:n_in]]
        ct_refs = refs[n_in:n_in + n_ct]
        out_refs = refs[n_in + n_ct:]

        def f(*dv):
            full = list(vals)
            for i, v in zip(diff, dv):
                full[i] = v
            return tuple(fn(*full))

        res, vjp = jax.vjp(f, *[vals[i] for i in diff])
        ct, pos = [], 0
        for group, r in zip(cts, res):
            if group is None:
                ct.append(jnp.ones_like(r))
            else:
                tot = None
                for _ in group:
                    c = _load_heads(ct_refs[pos])
                    tot = c if tot is None else tot + c
                    pos += 1
                ct.append(tot)
        grads = vjp(tuple(ct))
        for i, g, r in zip(diff, grads, out_refs):
            if ops[i].acc:
                @pl.when(ops[i].first(ids))
                def _(r=r):
                    r[...] = jnp.zeros_like(r)

                r[...] += g.astype(r.dtype)
            else:
                r[...] = g.astype(r.dtype)
        if value_acc:
            r = out_refs[len(diff)]

            @pl.when(ids[0] == 0)
            def _():
                r[...] = jnp.zeros_like(r)

            r[...] += res[0]

    in_specs = [pl.BlockSpec(o.block, o.imap) for o in ops] + [pl.BlockSpec(b, im) for (_, b, im) in ct_flat]
    out_specs = [pl.BlockSpec(ops[i].block, ops[i].gimap) for i in diff]
    out_shape = [jax.ShapeDtypeStruct(ops[i].gshape, ops[i].gdtype) for i in diff]
    if value_acc:
        out_specs.append(pl.BlockSpec((1, 1), lambda *ids: (0, 0)))
        out_shape.append(jax.ShapeDtypeStruct((1, 1), F32))
    sem = ("arbitrary",) * ngrid if any_acc else ("parallel",) * ngrid
    return pl.pallas_call(
        body, grid=grid, in_specs=in_specs, out_specs=out_specs, out_shape=out_shape,
        compiler_params=pltpu.CompilerParams(dimension_semantics=sem, vmem_limit_bytes=VMEM_LIMIT),
        name=name)(*[o.arr for o in ops], *[a for (a, _, _) in ct_flat])


def _sigmoid(x):
    return 1.0 / (1.0 + jnp.exp(-x))


def _silu(x):
    return x * _sigmoid(x)


def _gelu_tanh(x):
    return 0.5 * x * (1.0 + jnp.tanh(math.sqrt(2.0 / math.pi) * (x + 0.044715 * (x * x * x))))


def _ln(x, g, b):
    mu = jnp.mean(x, axis=-1, keepdims=True)
    xc = x - mu
    var = jnp.mean(xc * xc, axis=-1, keepdims=True)
    return xc * lax.rsqrt(var + LN_EPS) * g + b


def _rms(x, g):
    ms = jnp.mean(x * x, axis=-1, keepdims=True)
    return x * lax.rsqrt(ms + RMS_EPS) * g


def _swap_halves(x, half):
    width = x.shape[-1]
    lane = lax.broadcasted_iota(jnp.int32, x.shape, 1)
    return jnp.where(lane % (2 * half) < half, pltpu.roll(x, width - half, 1), pltpu.roll(x, half, 1))


def _make_swap(half):
    @jax.custom_vjp
    def swap(x):
        return _swap_halves(x, half)

    swap.defvjp(lambda x: (_swap_halves(x, half), None), lambda _, dy: (_swap_halves(dy, half),))
    return swap


_swap16, _swap8 = _make_swap(16), _make_swap(8)


def _rope(x, cos, sin_signed, swap):
    return x * cos + swap(x) * sin_signed


def _dot_f32(a, b):
    return jnp.dot(a, b, preferred_element_type=F32, precision=lax.Precision.HIGHEST)


def fn_ln(x, g, b):
    return (_ln(x, g, b),)


def _twice(fn):
    def f(*a):
        (y,) = fn(*a)
        return y, y
    return f


PROJ_W = 2048
P_A, P_Q, P_K, P_V, P_C, P_CQ, P_CKV, P_KR = 0, 512, 768, 896, 1024, 1536, 1792, 1920
CQ_PAD = 256
_CQ_END = P_CQ + MLA_Q_LORA


def _pad_w_in(wt):
    z = lambda n: jnp.zeros((n, wt.shape[1]), wt.dtype)
    return jnp.concatenate([wt[:_CQ_END], z(P_CKV - _CQ_END), wt[_CQ_END:], z(PROJ_W - P_KR - MLA_ROPE)], axis=0)


def _unpad_w_in(gt):
    return jnp.concatenate([gt[:_CQ_END], gt[P_CKV:P_KR + MLA_ROPE]], axis=0)


def fn_pre(proj, tab_q, tab_d, seg, place, qng, kng, sg, sb, sw, sbt, mqn, wuq, mkvn, wukv):
    tm = proj.shape[0]
    aglu = proj[:, P_A:P_A + GROUP_W] * _sigmoid(proj[:, P_A + GROUP_W:P_Q])
    b_q, b_k, b_v = proj[:, P_Q:P_K], proj[:, P_K:P_V], proj[:, P_V:P_C]
    cos_q, sin_q = tab_q[:, :GROUP_W], tab_q[:, GROUP_W:]
    q = b_q * lax.rsqrt(_dot_f32(b_q * b_q, seg) + RMS_EPS) * qng
    q = _rope(q, cos_q, sin_q, _swap16)
    k = b_k * lax.rsqrt(_dot_f32(b_k * b_k, seg[:128, :128]) + RMS_EPS) * kng
    k = _rope(k, cos_q[:, :128], sin_q[:, :128], _swap16)
    c = _gelu_tanh(proj[:, P_C:P_CQ])
    u, sv = c[:, :GROUP_W], _ln(c[:, GROUP_W:], sg, sb)
    group = lax.broadcasted_iota(jnp.int32, (CHUNK, GROUP_W), 1) // HEAD_DIM
    rows = []
    for n in range(tm // CHUNK):
        svn = sv[CHUNK * n:CHUNK * (n + 1)]
        acc = jnp.zeros((CHUNK, GROUP_W), F32)
        for g in range(SGU_GROUPS):
            acc = acc + jnp.where(group == g, mm_nn(sw[CHUNK * g:CHUNK * (g + 1)], svn) + sbt[:, g:g + 1], 0.0)
        rows.append(acc)
    o_c = u * jnp.concatenate(rows, axis=0)
    d_cq, d_ckv, d_kr = proj[:, P_CQ:P_CKV], proj[:, P_CKV:P_KR], proj[:, P_KR:PROJ_W]
    cqn = d_cq * lax.rsqrt(jnp.sum(d_cq * d_cq, axis=-1, keepdims=True) * (1.0 / MLA_Q_LORA) + RMS_EPS) * mqn
    cos_d = jnp.concatenate([tab_d[:, :MLA_DK_PAD]] * MLA_HEADS, axis=-1)
    sin_d = jnp.concatenate([tab_d[:, MLA_DK_PAD:]] * MLA_HEADS, axis=-1)
    qf = _rope(mm_nn(cqn, wuq), cos_d, sin_d, _swap8)
    kvd = mm_nn(_rms(d_ckv, mkvn), wukv)
    kf = _rope(kvd[:, :MLA_HEADS * MLA_DK_PAD] + _dot_f32(d_kr, place), cos_d, sin_d, _swap8)
    return aglu, q, k, b_v, o_c, qf, kf, kvd[:, MLA_HEADS * MLA_DK_PAD:]


def fn_aconv(win, w, b, g, beta):
    tm = win.shape[0] - 2 * CONV_A_HALO
    off = CONV_A_HALO - CONV_A_WIDTH // 2
    rolled = [win] + [_roll_rows(win, -r) for r in range(1, 8)]
    acc = None
    for kk in range(CONV_A_WIDTH):
        r = (off + kk) % 8
        base = off + kk - r
        term = rolled[r][base:base + tm] * w[kk:kk + 1, :]
        acc = term if acc is None else acc + term
    return (_silu(_ln(acc + b, g, beta)),)


def fn_resln(h, r, g, b):
    return (_ln(DEEPNORM_ALPHA * h + r, g, b),)


@functools.partial(jax.custom_vjp, nondiff_argnums=(1,))
def _roll_rows(x, shift):
    return pltpu.roll(x, shift % x.shape[0], 0)


_roll_rows.defvjp(lambda x, shift: (pltpu.roll(x, shift % x.shape[0], 0), None),
                  lambda shift, _, dy: (pltpu.roll(dy, (-shift) % dy.shape[0], 0),))


def _shift_down(x):
    row = lax.broadcasted_iota(jnp.int32, x.shape, 0)
    return jnp.where(row == 0, 0.0, _roll_rows(x, 1))


def _shift_up(x):
    row = lax.broadcasted_iota(jnp.int32, x.shape, 0)
    return jnp.where(row == x.shape[0] - 1, 0.0, _roll_rows(x, -1))


def fn_ffnconv(u1, u2, w1, w2, b1, b2):
    c1 = _shift_down(u1) * w1[0:1] + u1 * w1[1:2] + _shift_up(u1) * w1[2:3] + b1
    c2 = _shift_down(u2) * w2[0:1] + u2 * w2[1:2] + _shift_up(u2) * w2[2:3] + b2
    return (_silu(c1) * c2,)


def fn_final(h, r, t, g, b):
    y = _ln(DEEPNORM_ALPHA * h + r, g, b)
    err = (y - t) * (y - t)
    return (0.5 * jnp.sum(jnp.mean(err, axis=-1, keepdims=True), axis=0, keepdims=True),)


def _rope_tables(seq):
    n_rows = seq // GRID_W
    lane128 = jnp.arange(128)

    def tile_tables(j, rotated, half):
        inv = ROPE_THETA ** (-(j % half).astype(F32) / half)
        by_row, by_col = rotated & (j < 2 * half), rotated & (j >= 2 * half)
        sign = jnp.where(j % (2 * half) < half, -1.0, 1.0)
        ar = jnp.arange(n_rows, dtype=F32)[:, None] * inv[None, :]
        ac = jnp.arange(GRID_W, dtype=F32)[:, None] * inv[None, :]
        grid = lambda r, c: (jnp.where(by_row, r, 0.0)[:, None, :] + jnp.where(by_col, c, 0.0)[None, :, :])
        cos = grid(jnp.cos(ar), jnp.cos(ac)) + jnp.where(rotated, 0.0, 1.0)
        sin = grid(sign * jnp.sin(ar), sign * jnp.sin(ac))
        return cos.reshape(seq, 128), sin.reshape(seq, 128)

    cos_b, sin_b = tile_tables(lane128 % HEAD_DIM, lane128 >= 0, HEAD_DIM // 4)
    tab_q = jnp.concatenate([cos_b] * (GROUP_W // 128) + [sin_b] * (GROUP_W // 128), axis=-1)
    tab_d = jnp.concatenate(tile_tables(lane128 - MLA_NOPE, (lane128 >= MLA_NOPE) & (lane128 < MLA_NOPE + MLA_ROPE),
                                        MLA_ROPE // 4), axis=-1)
    lane = jnp.arange(GROUP_W)
    seg = jnp.where(lane[:, None] // HEAD_DIM == lane[None, :] // HEAD_DIM, 1.0 / HEAD_DIM, 0.0).astype(F32)
    src, dst = jnp.arange(128)[:, None], jnp.arange(MLA_HEADS * MLA_DK_PAD)[None, :]
    place = jnp.where((src < MLA_ROPE) & (dst % MLA_DK_PAD == MLA_NOPE + src), 1.0, 0.0).astype(F32)
    return tab_q, tab_d, seg, place


def _pre_ops(proj, tabs, kp, grad):
    tm = ROW_TILE
    ops = [_row_op(proj, tm, grad=grad, gdtype=MXU_DTYPE), _row_op(tabs[0], tm), _row_op(tabs[1], tm),
           _par_op(tabs[2]), _par_op(tabs[3])]
    ops += [_par_op(kp[n], grad=grad) for n in ('qng', 'kng', 'sg', 'sb', 'sw', 'sbt', 'mqn', 'wuq', 'mkvn', 'wukv')]
    return ops


PRE_OUTS = ((0, GROUP_W), (GQA_HEADS, HEAD_DIM), (GQA_KV_HEADS, HEAD_DIM), (GQA_KV_HEADS, HEAD_DIM), (0, GROUP_W),
            (MLA_HEADS, MLA_DK_PAD), (MLA_HEADS, MLA_DK_PAD), (MLA_HEADS, MLA_V))


def _pre_out_specs(seq, tm):
    specs = []
    for heads, w in PRE_OUTS:
        if heads:
            specs.append(((heads, seq, w), (heads, tm, w), lambda i: (0, i, 0)))
        else:
            specs.append(((seq, w), (tm, w), lambda i: (i, 0)))
    return specs


def pre_fwd(proj, tabs, kp, tag):
    seq = proj.shape[0]
    tm = ROW_TILE
    dts = (F32,) + (MXU_DTYPE,) * 7
    outs = [(shape, dt, block, imap) for (shape, block, imap), dt in zip(_pre_out_specs(seq, tm), dts)]
    return stage_fwd("pre_fwd" + tag, fn_pre, _pre_ops(proj, tabs, kp, False), outs, (seq // tm,))


def pre_bwd(proj, tabs, kp, cts, tag):
    seq = proj.shape[0]
    tm = ROW_TILE
    ct = [[c if isinstance(c, tuple) else (c, block, imap)] for c, (_, block, imap) in zip(cts, _pre_out_specs(seq, tm))]
    return stage_bwd("pre_bwd" + tag, fn_pre, _pre_ops(proj, tabs, kp, True), ct, (seq // tm,))


def _aconv_ops(kp, grad):
    return [_par_op(kp[n], grad=grad) for n in ('caw', 'cab', 'lag', 'lab')]


def aconv_fwd(aglu_pad, kp, tag):
    seq = aglu_pad.shape[0] - 2 * CONV_A_HALO
    tm = ROW_TILE
    n_par = 4

    def body(x_ref, *refs):
        i = pl.program_id(0)
        win = x_ref[pl.ds(pl.multiple_of(i * tm, tm), tm + 2 * CONV_A_HALO), :]
        (o,) = fn_aconv(win, *[_load(r) for r in refs[:n_par]])
        refs[n_par][...] = o.astype(refs[n_par].dtype)

    pars = _aconv_ops(kp, False)
    return pl.pallas_call(
        body, grid=(seq // tm,),
        in_specs=[pl.BlockSpec(aglu_pad.shape, lambda i: (0, 0))] + [pl.BlockSpec(o.block, o.imap) for o in pars],
        out_specs=pl.BlockSpec((tm, GROUP_W), lambda i: (i, 0)),
        out_shape=jax.ShapeDtypeStruct((seq, GROUP_W), MXU_DTYPE),
        compiler_params=pltpu.CompilerParams(dimension_semantics=("parallel",), vmem_limit_bytes=VMEM_LIMIT),
        name="aconv_fwd" + tag)(aglu_pad, *[o.arr for o in pars])


def aconv_bwd(aglu_pad, kp, d_oa, tag):
    seq = aglu_pad.shape[0] - 2 * CONV_A_HALO
    tm = ROW_TILE
    n_par = 4

    def body(x_ref, *refs):
        i = pl.program_id(0)
        rows = pl.ds(pl.multiple_of(i * tm, tm), tm + 2 * CONV_A_HALO)
        pars = [_load(r) for r in refs[:n_par]]
        ct = refs[n_par][...].astype(F32)
        outs = refs[n_par + 1:]
        _, vjp = jax.vjp(lambda *a: fn_aconv(*a), x_ref[rows, :], *pars)
        grads = vjp((ct,))

        @pl.when(i == 0)
        def _():
            for r in outs:
                r[...] = jnp.zeros_like(r)

        outs[0][rows, :] += grads[0]
        for r, g in zip(outs[1:], grads[1:]):
            r[...] += g

    pars = _aconv_ops(kp, True)
    whole = pl.BlockSpec(aglu_pad.shape, lambda i: (0, 0))
    par_specs = [pl.BlockSpec(o.block, o.imap) for o in pars]
    return pl.pallas_call(
        body, grid=(seq // tm,),
        in_specs=[whole] + par_specs + [pl.BlockSpec((tm, GROUP_W), lambda i: (i, 0))],
        out_specs=[whole] + par_specs,
        out_shape=[jax.ShapeDtypeStruct(aglu_pad.shape, F32)] + [jax.ShapeDtypeStruct(o.arr.shape, F32) for o in pars],
        compiler_params=pltpu.CompilerParams(dimension_semantics=("arbitrary",), vmem_limit_bytes=VMEM_LIMIT),
        name="aconv_bwd" + tag)(aglu_pad, *[o.arr for o in pars], d_oa)


ATTN_TQ_FWD = 512
ATTN_TQ = 256
ATTN_TK = 512


def attn_fwd(q3, k3, v3t, scale, tag):
    heads, seq, dk = q3.shape
    group = heads // k3.shape[0]
    kv_per_pair = 2 // group
    dv = v3t.shape[1]
    tq, tk = min(ATTN_TQ_FWD, seq), min(ATTN_TK, seq)
    n_chunks = seq // tk
    log2e = math.log2(math.e)

    def one_head(q, k_ref, vt_ref):
        scores = lambda c: _rawdot(k_ref[pl.ds(c * tk, tk), :], q, 1, 1)
        m, l, acc = jnp.full((1, tq), -jnp.inf, F32), jnp.zeros((1, tq), F32), jnp.zeros((dv, tq), F32)
        s_next = scores(0)
        for c in range(n_chunks):
            s_cur, s_next = s_next, (scores(c + 1) if c + 1 < n_chunks else None)
            t = s_cur * (scale * log2e)
            m_new = jnp.maximum(m, jnp.max(t, axis=0, keepdims=True))
            alpha = jnp.exp2(m - m_new)
            p = jnp.exp2(t - m_new)
            l = alpha * l + jnp.sum(p, axis=0, keepdims=True)
            acc = alpha * acc + _rawdot(vt_ref[:, c * tk:(c + 1) * tk], p, 1, 0)
            m = m_new
        return (acc * (1.0 / l)).T, (m * (1.0 / log2e) + jnp.log(l)).T

    def body(q_ref, k_ref, v_ref, o_ref, lse_ref):
        outs = []
        for h in range(2):
            o, lse = one_head(q_ref[h], k_ref.at[h // group], v_ref.at[h // group])
            lse_ref[h] = lse
            outs.append(o)
        o_ref[...] = jnp.concatenate(outs, axis=-1)

    return pl.pallas_call(
        body, grid=(heads // 2, seq // tq),
        in_specs=[pl.BlockSpec((2, tq, dk), lambda j, i: (j, i, 0)),
                  pl.BlockSpec((kv_per_pair, seq, dk), lambda j, i: (j, 0, 0)),
                  pl.BlockSpec((kv_per_pair, dv, seq), lambda j, i: (j, 0, 0))],
        out_specs=[pl.BlockSpec((tq, 2 * dv), lambda j, i: (i, j)),
                   pl.BlockSpec((2, tq, 1), lambda j, i: (j, i, 0))],
        out_shape=[jax.ShapeDtypeStruct((seq, heads * dv), F32), jax.ShapeDtypeStruct((heads, seq, 1), F32)],
        compiler_params=pltpu.CompilerParams(dimension_semantics=("parallel", "parallel"),
                                             vmem_limit_bytes=VMEM_LIMIT),
        name="attn_fwd" + tag)(q3, k3, v3t)


def attn_bwd(q3, k3, v3, o, lse3, do_all, do_col, scale, tag):
    heads, seq, dk = q3.shape
    group = heads // k3.shape[0]
    kv_per_pair = 2 // group
    dv = v3.shape[2]
    tq, tk = min(ATTN_TQ, seq), min(ATTN_TK, seq)
    n_chunks = seq // tk
    log2e = math.log2(math.e)

    def one_head(q, do, o_h, lse, k_ref, v_ref, dk_ref, dv_ref):
        dob = do.astype(MXU_DTYPE)
        do_t, q_t = do.T.astype(MXU_DTYPE), q.astype(F32).T.astype(MXU_DTYPE)
        delta = jnp.sum(do * o_h, axis=-1, keepdims=True)
        lse2 = lse * log2e
        rows = lambda c: pl.ds(c * tk, tk)
        products = lambda c: (_rawdot(q, k_ref[rows(c), :], 1, 1), _rawdot(dob, v_ref[rows(c), :], 1, 1))
        dq = jnp.zeros((tq, dk), F32)
        nxt = products(0)
        for c in range(n_chunks):
            (s_cur, dp_cur), nxt = nxt, (products(c + 1) if c + 1 < n_chunks else None)
            p = jnp.exp2(s_cur * (scale * log2e) - lse2)
            ds = (p * ((dp_cur - delta) * scale)).astype(MXU_DTYPE)
            dv_ref[:, c * tk:(c + 1) * tk] += _rawdot(do_t, p, 1, 0)
            dk_ref[:, c * tk:(c + 1) * tk] += _rawdot(q_t, ds, 1, 0)
            dq = dq + _rawdot(ds, k_ref[rows(c), :], 1, 0)
        return dq

    def body(q_ref, k_ref, v_ref, o_ref, lse_ref, do_ref, dq_ref, dk_ref, dv_ref):
        @pl.when(pl.program_id(1) == 0)
        def _():
            dk_ref[...] = jnp.zeros_like(dk_ref)
            dv_ref[...] = jnp.zeros_like(dv_ref)

        do_pair, o_pair = do_ref[...], o_ref[...]
        for h in range(2):
            kv = h // group
            dq_ref[h] = one_head(q_ref[h], do_pair[:, dv * h:dv * (h + 1)], o_pair[:, dv * h:dv * (h + 1)],
                                 lse_ref[h], k_ref.at[kv], v_ref.at[kv], dk_ref.at[kv], dv_ref.at[kv])

    qspec = lambda d: pl.BlockSpec((2, tq, d), lambda j, i: (j, i, 0))
    kvspec = lambda d: pl.BlockSpec((kv_per_pair, seq, d), lambda j, i: (j, 0, 0))
    kvt_spec = lambda d: pl.BlockSpec((kv_per_pair, d, seq), lambda j, i: (j, 0, 0))
    kvt_shape = lambda a: jax.ShapeDtypeStruct((a.shape[0], a.shape[2], a.shape[1]), F32)
    return pl.pallas_call(
        body, grid=(heads // 2, seq // tq),
        in_specs=[qspec(dk), kvspec(dk), kvspec(dv), pl.BlockSpec((tq, 2 * dv), lambda j, i: (i, j)), qspec(1),
                  pl.BlockSpec((tq, 2 * dv), lambda j, i: (i, do_col + j))],
        out_specs=[qspec(dk), kvt_spec(dk), kvt_spec(dv)],
        out_shape=[jax.ShapeDtypeStruct(q3.shape, F32), kvt_shape(k3), kvt_shape(v3)],
        compiler_params=pltpu.CompilerParams(dimension_semantics=("parallel", "arbitrary"),
                                             vmem_limit_bytes=VMEM_LIMIT),
        name="attn_bwd" + tag)(q3, k3, v3, o, lse3, do_all)


def resln_fwd(h, r, g, b, tag):
    seq, d = h.shape
    tm = ROW_TILE
    ops = [_row_op(h, tm), _row_op(r, tm), _par_op(g), _par_op(b)]
    outs = [((seq, d), dt, (tm, d), lambda i: (i, 0)) for dt in (F32, MXU_DTYPE)]
    return stage_fwd("resln_fwd" + tag, _twice(fn_resln), ops, outs, (seq // tm,))


def resln_bwd(h, r, g, b, dys, tag):
    seq, d = h.shape
    tm = ROW_TILE
    ops = [_row_op(h, tm, grad=True), _row_op(r, tm, grad=True, gdtype=MXU_DTYPE), _par_op(g, grad=True),
           _par_op(b, grad=True)]
    ct = [[(dy, (tm, d), lambda i: (i, 0)) for dy in dys]]
    return stage_bwd("resln_bwd" + tag, fn_resln, ops, ct, (seq // tm,))


def _ffnconv_ops(up1, up2, w, b, grad):
    seq = up1.shape[0]
    nblk = D_FF // 128
    lo, hi = (lambda j: (0, j)), (lambda j: (0, j + nblk))
    half = lambda a: dict(gshape=(a.shape[0], D_FF), gimap=lo)
    return [Op(up1, (seq, 128), lo, grad=grad, gdtype=MXU_DTYPE), Op(up2, (seq, 128), lo, grad=grad, gdtype=MXU_DTYPE),
            Op(w, (3, 128), lo, grad=grad, **half(w)), Op(w, (3, 128), hi, grad=grad, **half(w)),
            Op(b, (1, 128), lo, grad=grad, **half(b)), Op(b, (1, 128), hi, grad=grad, **half(b))]


def ffnconv_fwd(up1, up2, w, b, tag):
    seq = up1.shape[0]
    outs = [((seq, D_FF), MXU_DTYPE, (seq, 128), lambda j: (0, j))]
    return stage_fwd("ffnconv_fwd" + tag, fn_ffnconv, _ffnconv_ops(up1, up2, w, b, False), outs, (D_FF // 128,))[0]


def ffnconv_bwd(up1, up2, w, b, dact, tag):
    seq = up1.shape[0]
    ct = [[(dact, (seq, 128), lambda j: (0, j))]]
    du1, du2, dw1, dw2, db1, db2 = stage_bwd("ffnconv_bwd" + tag, fn_ffnconv, _ffnconv_ops(up1, up2, w, b, True),
                                             ct, (D_FF // 128,))
    cat = lambda a, b_: jnp.concatenate([a, b_], axis=-1)
    return du1, du2, cat(dw1, dw2), cat(db1, db2)


def final_bwd(h, r, t, g, b, tag):
    seq, d = h.shape
    tm = ROW_TILE
    ops = [_row_op(h, tm, grad=True), _row_op(r, tm, grad=True, gdtype=MXU_DTYPE), _row_op(t, tm),
           _par_op(g, grad=True), _par_op(b, grad=True)]
    return stage_bwd("final_bwd" + tag, fn_final, ops, [None], (seq // tm,), value_acc=True)


def _layer_params(wts, l):
    row = lambda a: a.reshape(1, -1)
    wuq = wts['mla_w_uq'][l].reshape(MLA_Q_LORA, MLA_HEADS, MLA_NOPE + MLA_ROPE)
    wuq = jnp.pad(wuq, ((0, CQ_PAD - MLA_Q_LORA), (0, 0), (0, MLA_DK_PAD - MLA_NOPE - MLA_ROPE)))
    wukv = wts['mla_w_ukv'][l].reshape(MLA_KV_LORA, MLA_HEADS, MLA_NOPE + MLA_V)
    wuk = jnp.pad(wukv[:, :, :MLA_NOPE], ((0, 0), (0, 0), (0, MLA_DK_PAD - MLA_NOPE)))
    return dict(
        qng=jnp.tile(row(wts['qk_norm_q'][l]), (1, GQA_HEADS)), kng=jnp.tile(row(wts['qk_norm_k'][l]), (1, GQA_KV_HEADS)),
        sg=row(wts['sgu_ln_g'][l]), sb=row(wts['sgu_ln_b'][l]),
        sw=wts['sgu_w'][l].reshape(SGU_GROUPS * CHUNK, CHUNK), sbt=wts['sgu_b'][l].T,
        mqn=jnp.pad(row(wts['mla_q_norm'][l]), ((0, 0), (0, CQ_PAD - MLA_Q_LORA))),
        wuq=wuq.reshape(CQ_PAD, MLA_HEADS * MLA_DK_PAD), mkvn=row(wts['mla_kv_norm'][l]),
        wukv=jnp.concatenate([wuk.reshape(MLA_KV_LORA, -1), wukv[:, :, MLA_NOPE:].reshape(MLA_KV_LORA, -1)], axis=1),
        caw=wts['conv_a_w'][l], cab=row(wts['conv_a_b'][l]), lag=row(wts['ln_a_g'][l]), lab=row(wts['ln_a_b'][l]),
        lmg=row(wts['ln_mix_g'][l]), lmb=row(wts['ln_mix_b'][l]),
        fcw=wts['ffn_conv_w'][l], fcb=row(wts['ffn_conv_b'][l]),
        lfg=row(wts['ln_ffn_g'][l]), lfb=row(wts['ln_ffn_b'][l]))


def _to_heads(a, heads):
    seq = a.shape[0]
    return a.reshape(seq, heads, -1).transpose(1, 0, 2)


def _from_heads(a3):
    return a3.transpose(1, 0, 2).reshape(a3.shape[1], -1)


def local_step(x, target, ln_in, get_wts, mat, hook):
    seq = x.shape[0]
    tm = ROW_TILE
    tabs = _rope_tables(seq)
    scale_b = HEAD_DIM ** -0.5
    scale_d = (MLA_NOPE + MLA_ROPE) ** -0.5
    ln_in_g, ln_in_b = ln_in[0].reshape(1, -1), ln_in[1].reshape(1, -1)

    h, h_m = stage_fwd("ln_in_fwd", _twice(fn_ln), [_row_op(x, tm), _par_op(ln_in_g), _par_op(ln_in_b)],
                       [((seq, D_MODEL), dt, (tm, D_MODEL), lambda i: (i, 0)) for dt in (F32, MXU_DTYPE)],
                       (seq // tm,))
    wts = get_wts(h_m)
    saved = []
    for l in range(DEPTH):
        tag = f"_l{l}"
        kp, unprep = jax.vjp(lambda w: _layer_params(w, l), wts)
        m = {'w_in': mat(l, 'w_in', h_m)}
        proj = matmul(h_m, m['w_in'], 'nt', F32, "mm_proj" + tag)
        aglu, q3, k3, v3, o_c, qd3, kd3, vd3 = pre_fwd(proj, tabs, kp, tag)
        aglu_pad = jnp.pad(aglu, ((CONV_A_HALO, CONV_A_HALO), (0, 0)))
        o_a = aconv_fwd(aglu_pad, kp, tag)
        swap = lambda a: a.transpose(0, 2, 1)
        o_b3, lse_b3 = attn_fwd(q3, k3, swap(v3), scale_b, "_b" + tag)
        o_d3, lse_d3 = attn_fwd(qd3, kd3, swap(vd3), scale_d, "_d" + tag)
        o_cat = jnp.concatenate([o_a, o_b3.astype(MXU_DTYPE), o_c, o_d3.astype(MXU_DTYPE)], axis=-1)
        m['w_out'] = mat(l, 'w_out', o_cat)
        mix = matmul(o_cat, m['w_out'], 'nn', F32, "mm_mix" + tag)
        h1, h1_m = resln_fwd(h, mix, kp['lmg'], kp['lmb'], "_mix" + tag)
        m['ffn_w_up'] = mat(l, 'ffn_w_up', h1_m)
        up1 = matmul(h1_m, m['ffn_w_up'], 'nt', F32, "mm_up1" + tag, b_rows=(0, D_FF))
        up2 = matmul(h1_m, m['ffn_w_up'], 'nt', F32, "mm_up2" + tag, b_rows=(D_FF, D_FF))
        act = ffnconv_fwd(up1, up2, kp['fcw'], kp['fcb'], tag)
        m['ffn_w_down'] = mat(l, 'ffn_w_down', act)
        f = matmul(act, m['ffn_w_down'], 'nn', F32, "mm_down" + tag)
        saved.append(dict(kp=kp, unprep=unprep, m=m, h=h, h_m=h_m, h1_m=h1_m, proj=proj, o_b3=o_b3, lse_b3=lse_b3,
                          o_d3=o_d3, lse_d3=lse_d3, aglu_pad=aglu_pad, q3=q3, k3=k3, v3=v3, qd3=qd3,
                          kd3=kd3, vd3=vd3, o_cat=o_cat, mix=mix, h1=h1, up1=up1, up2=up2, act=act, f=f))
        if l + 1 < DEPTH:
            h, h_m = resln_fwd(h1, f, kp['lfg'], kp['lfb'], "_ffn" + tag)

    after = lambda a, tok: a if tok is None else a + tok
    small_acc = None
    dh_parts = None
    loss = None
    tok = None
    g_mix = None
    for l in reversed(range(DEPTH)):
        tag = f"_l{l}"
        s = saved[l]
        kp, m = s['kp'], s['m']
        dkp = {}
        lfg = after(kp['lfg'], tok)
        if l == DEPTH - 1:
            dh1_a, df, dkp['lfg'], dkp['lfb'], loss = final_bwd(s['h1'], s['f'], target, lfg, kp['lfb'], tag)
        else:
            dh1_a, df, dkp['lfg'], dkp['lfb'] = resln_bwd(s['h1'], s['f'], lfg, kp['lfb'], dh_parts, "_ffn" + tag)
        g_down = matmul(s['act'], df, 'tn', COMM_DTYPE, "mm_gdown" + tag)
        dact = matmul(df, m['ffn_w_down'], 'nt', F32, "mm_dact" + tag)
        dup1, dup2, dkp['fcw'], dkp['fcb'] = ffnconv_bwd(s['up1'], s['up2'], kp['fcw'], kp['fcb'], dact, tag)
        dh1_b1 = matmul(dup1, m['ffn_w_up'], 'nn', F32, "mm_dh1a" + tag, b_rows=(0, D_FF))
        dh1_b2 = matmul(dup2, m['ffn_w_up'], 'nn', F32, "mm_dh1b" + tag, b_rows=(D_FF, D_FF))
        g_up = matmul(dup1, s['h1_m'], 'tn', COMM_DTYPE, "mm_gup1" + tag,
                      into=(lax.empty((2 * D_FF, D_MODEL), COMM_DTYPE), 0))
        g_up = matmul(dup2, s['h1_m'], 'tn', COMM_DTYPE, "mm_gup2" + tag, into=(g_up, D_FF))
        tok = hook(f"ffn{l}", {('ffn_w_down', l): g_down, ('ffn_w_up', l): g_up})
        dh_a, dmix, dkp['lmg'], dkp['lmb'] = resln_bwd(s['h'], s['mix'], after(kp['lmg'], tok), kp['lmb'],
                                                       [dh1_a, dh1_b1, dh1_b2], "_mix" + tag)
        g_out = matmul(s['o_cat'], dmix, 'tn', COMM_DTYPE, "mm_gout" + tag)
        w_out = m['w_out']
        if l == 0:
            w_out = w_out + hook("out0", {('w_out', l): g_out}).astype(w_out.dtype)
        do_cat = matmul(dmix, w_out, 'nt', F32, "mm_docat" + tag)
        lse_b3 = s['lse_b3']
        do_c = (do_cat, (ROW_TILE, GROUP_W), lambda i: (i, 2))
        pair_w = 2 * HEAD_DIM
        dq3, dk3, dv3 = attn_bwd(s['q3'], s['k3'], s['v3'], s['o_b3'], lse_b3, do_cat, GROUP_W // pair_w,
                                 scale_b, "_b" + tag)
        dqd3, dkd3, dvd3 = attn_bwd(s['qd3'], s['kd3'], s['vd3'], s['o_d3'], s['lse_d3'], do_cat,
                                    3 * GROUP_W // pair_w, scale_d, "_d" + tag)
        daglu_pad, dkp['caw'], dkp['cab'], dkp['lag'], dkp['lab'] = aconv_bwd(s['aglu_pad'], kp, do_cat, tag)
        swap = lambda a: a.transpose(0, 2, 1)
        cts = [daglu_pad[CONV_A_HALO:CONV_A_HALO + seq], dq3, swap(dk3), swap(dv3), do_c, dqd3, swap(dkd3),
               swap(dvd3)]
        pre_g = pre_bwd(s['proj'], tabs, kp, cts, tag)
        dproj = pre_g[0]
        for n, g in zip(('qng', 'kng', 'sg', 'sb', 'sw', 'sbt', 'mqn', 'wuq', 'mkvn', 'wukv'), pre_g[1:]):
            dkp[n] = g
        dh_b = matmul(dproj, m['w_in'], 'nn', F32, "mm_dh" + tag)
        g_in = matmul(dproj, s['h_m'], 'tn', COMM_DTYPE, "mm_gin" + tag)
        dh_parts = [dh_a, dh_b]
        (dw,) = s['unprep'](dkp)
        small_acc = dw if small_acc is None else jax.tree.map(jnp.add, small_acc, dw)
        g_mix = {('w_out', l): g_out, ('w_in', l): _unpad_w_in(g_in)}
        if l > 0:
            tok = hook(f"mix{l}", g_mix)
        else:
            g_mix.pop(('w_out', l))

    g_mix.update({(n, None): small_acc[n] for n in SHARDED if n not in MATMUL_WEIGHTS})
    tok = hook("last", g_mix)
    dx, dg, db = stage_bwd("ln_in_bwd", fn_ln,
                           [_row_op(x, tm, grad=True), _par_op(after(ln_in_g, tok), grad=True),
                            _par_op(ln_in_b, grad=True)],
                           [[(p, (tm, D_MODEL), lambda i: (i, 0)) for p in dh_parts]], (seq // tm,))
    out = {n: small_acc[n] for n in REPLICATED}
    out['ln_in_g'], out['ln_in_b'] = dg.reshape(-1), db.reshape(-1)
    return loss, dx, out


def _peer(x, y, c, r):
    return ((1 - x) if r & 4 else x, (1 - y) if r & 2 else y, (1 - c) if r & 1 else c)


def _exchange_copy(src_ref, land_ref, send_sems, recv_sems, k, gather, x, y, c, r):
    px, py, pc = _peer(x, y, c, r)
    me, peer = 4 * x + 2 * y + c, 4 * px + 2 * py + pc
    src = src_ref if gather else src_ref.at[peer]
    mk = lambda dst: pltpu.make_async_remote_copy(
        src_ref=src, dst_ref=dst, send_sem=send_sems.at[k * (N_DEV - 1) + r - 1],
        recv_sem=recv_sems.at[k * (N_DEV - 1) + r - 1],
        device_id=(px, py, pc), device_id_type=pl.DeviceIdType.MESH)
    return mk(land_ref.at[me]), mk(land_ref.at[peer])


_HBM_SPEC = pl.BlockSpec(memory_space=pltpu.HBM)
_SEM_SPEC = pl.BlockSpec(memory_space=pltpu.SEMAPHORE)


def exchange_start(srcs, gather, groups, name):
    n_t = len(srcs)
    lands =[lax.empty(((N_DEV,) + s.shape) if gt else s.shape, s.dtype) for s, gt in zip(srcs, gather)]

    def body(*refs):
        src_refs, land_refs = refs[:n_t], refs[n_t:2 * n_t]
        sem_refs = refs[2 * n_t:2 * n_t + 2 * len(groups)]
        token = refs[-1]
        x, y, c = lax.axis_index("x"), lax.axis_index("y"), lax.axis_index("c")
        for gi, g in enumerate(groups):
            for k, t in enumerate(g):
                for r in range(1, N_DEV):
                    _exchange_copy(src_refs[t], land_refs[t], sem_refs[2 * gi], sem_refs[2 * gi + 1], k, gather[t],
                                   x, y, c, r)[0].start()
        token[...] = jnp.zeros_like(token)

    sem_shapes = []
    for g in groups:
        sem_shapes += [pltpu.SemaphoreType.DMA((len(g) * (N_DEV - 1),))] * 2
    hbm_shapes = [pltpu.HBM(a.shape, a.dtype) for a in list(srcs) + lands]
    n_sem = len(sem_shapes)
    res = pl.pallas_call(
        body, name=name,
        out_shape=tuple(sem_shapes + hbm_shapes + [jax.ShapeDtypeStruct((8, 128), F32)]),
        in_specs=[_HBM_SPEC] * (2 * n_t),
        out_specs=tuple([_SEM_SPEC] * n_sem + [_HBM_SPEC] * (2 * n_t) + [pl.BlockSpec(memory_space=pltpu.VMEM)]),
        input_output_aliases={i: n_sem + i for i in range(2 * n_t)},
        compiler_params=pltpu.CompilerParams(has_side_effects=pltpu.SideEffectType.DATAFLOW_SIDE_EFFECTING),
    )(*[pltpu.with_memory_space_constraint(a, pltpu.HBM) for a in list(srcs) + lands])
    sems = [(res[2 * gi], res[2 * gi + 1]) for gi in range(len(groups))]
    return sems, list(res[n_sem:n_sem + n_t]), list(res[n_sem + n_t:n_sem + 2 * n_t]), res[-1]


def exchange_wait(sems, srcs, lands, gather, after, name):
    n_t = len(srcs)

    def body(*refs):
        src_refs, land_refs = refs[:n_t], refs[n_t:2 * n_t]
        send_sems, recv_sems = refs[2 * n_t], refs[2 * n_t + 1]
        x, y, c = lax.axis_index("x"), lax.axis_index("y"), lax.axis_index("c")
        for k in range(n_t):
            for r in range(1, N_DEV):
                send, recv = _exchange_copy(src_refs[k], land_refs[k], send_sems, recv_sems, k, gather[k], x, y, c, r)
                send.wait_send()
                recv.wait_recv()

    res = pl.pallas_call(
        body, name=name,
        out_shape=tuple(pltpu.HBM(a.shape, a.dtype) for a in list(srcs) + list(lands)),
        in_specs=[_HBM_SPEC] * (2 * n_t) + [_SEM_SPEC, _SEM_SPEC, pl.BlockSpec(memory_space=pl.ANY)],
        out_specs=tuple([_HBM_SPEC] * (2 * n_t)),
        input_output_aliases={i: i for i in range(2 * n_t)},
        compiler_params=pltpu.CompilerParams(has_side_effects=pltpu.SideEffectType.DATAFLOW_SIDE_EFFECTING),
    )(*srcs, *lands, sems[0], sems[1], after)
    return list(res[:n_t]), list(res[n_t:])


def adamw(parts, w, m, v, name):
    n_l, n_r, n_c = w.shape
    tr = n_r
    if n_r % 8 == 0:
        for cand in (512, 256, 128, 64, 32, 16, 8):
            if n_r % cand == 0 and cand * n_c * 4 <= 512 * 1024:
                tr = cand
                break
    c1 = 1.0 - ADAM_B1 ** ADAM_STEP
    c2 = 1.0 - ADAM_B2 ** ADAM_STEP
    per_layer = isinstance(parts, (list, tuple))
    n_p = n_l if per_layer else 1
    n_rb = n_r // tr

    def update(g, w_ref, m_ref, v_ref, g_ref, d_ref, nm_ref, nv_ref):
        w_, m_, v_ = w_ref[0], m_ref[0], v_ref[0]
        nm = ADAM_B1 * m_ + (1.0 - ADAM_B1) * g
        nv = ADAM_B2 * v_ + (1.0 - ADAM_B2) * (g * g)
        g_ref[0] = g
        nm_ref[0] = nm
        nv_ref[0] = nv
        d_ref[0] = -ADAM_LR * ((nm / c1) / (jnp.sqrt(nv / c2) + ADAM_EPS) + ADAM_WD * w_)

    def body(*refs):
        p_refs, rest = refs[:n_p], refs[n_p:]
        if not per_layer:
            g = p_refs[0][0, 0].astype(F32)
            for s in range(1, N_DEV):
                g = g + p_refs[0][s, 0].astype(F32)
            update(g, *rest)
        else:
            for lay in range(n_l):
                @pl.when(pl.program_id(0) == lay)
                def _(lay=lay):
                    g = p_refs[lay][0].astype(F32)
                    for s in range(1, N_DEV):
                        g = g + p_refs[lay][s].astype(F32)
                    update(g, *rest)

    blk = pl.BlockSpec((1, tr, n_c), lambda l, r: (l, r, 0))
    if per_layer:
        def p_spec(lay):
            park = 0 if lay > 0 else n_rb - 1
            return pl.BlockSpec((N_DEV, tr, n_c), lambda l, r: (0, jnp.where(l == lay, r, park), 0))
        p_specs, p_args = [p_spec(lay) for lay in range(n_l)], list(parts)
    else:
        p_specs, p_args = [pl.BlockSpec((N_DEV, 1, tr, n_c), lambda l, r: (0, l, r, 0))], [parts]
    return pl.pallas_call(
        body, grid=(n_l, n_rb), in_specs=p_specs + [blk, blk, blk],
        out_specs=[blk] * 4, out_shape=[jax.ShapeDtypeStruct(w.shape, F32)] * 4,
        compiler_params=pltpu.CompilerParams(dimension_semantics=("arbitrary", "arbitrary"),
                                             vmem_limit_bytes=VMEM_LIMIT),
        name=name)(*p_args, w, m, v)


def adamw_replicated(lands, own, ws, ms, vs, loss_land, loss_own):
    n_t = len(lands)
    c1 = 1.0 - ADAM_B1 ** ADAM_STEP
    c2 = 1.0 - ADAM_B2 ** ADAM_STEP

    def body(*refs):
        ins, outs = refs[:5 * n_t + 2], refs[5 * n_t + 2:]
        me = 4 * lax.axis_index("x") + 2 * lax.axis_index("y") + lax.axis_index("c")

        def total(land_ref, own_ref):
            g = None
            for s in range(N_DEV):
                term = jnp.where(me == s, own_ref[...], land_ref[s])
                g = term if g is None else g + term
            return g

        for t in range(n_t):
            land_ref, own_ref, w_ref, m_ref, v_ref = ins[5 * t:5 * t + 5]
            g = total(land_ref, own_ref)
            nm = ADAM_B1 * m_ref[...] + (1.0 - ADAM_B1) * g
            nv = ADAM_B2 * v_ref[...] + (1.0 - ADAM_B2) * (g * g)
            g_ref, d_ref, nm_ref, nv_ref = outs[4 * t:4 * t + 4]
            g_ref[...] = g
            nm_ref[...] = nm
            nv_ref[...] = nv
            d_ref[...] = -ADAM_LR * ((nm / c1) / (jnp.sqrt(nv / c2) + ADAM_EPS) + ADAM_WD * w_ref[...])
        outs[4 * n_t][...] = total(ins[5 * n_t], ins[5 * n_t + 1])

    args = []
    for t in range(n_t):
        args += [lands[t], own[t], ws[t], ms[t], vs[t]]
    out_shape = []
    for t in range(n_t):
        out_shape += [jax.ShapeDtypeStruct(ws[t].shape, F32)] * 4
    out_shape.append(jax.ShapeDtypeStruct(loss_own.shape, F32))
    res = pl.pallas_call(body, out_shape=out_shape,
                         compiler_params=pltpu.CompilerParams(vmem_limit_bytes=VMEM_LIMIT),
                         name="adamw_replicated")(*args, loss_land, loss_own)
    return [tuple(res[4 * t:4 * t + 4]) for t in range(n_t)], res[-1]


def _shard_slots(g, axis):
    if axis == 1:
        return g.reshape(g.shape[0], N_DEV, g.shape[1] // N_DEV, g.shape[2]).transpose(1, 0, 2, 3)
    return g.reshape(g.shape[0], g.shape[1], N_DEV, g.shape[2] // N_DEV).transpose(2, 0, 1, 3)


def _unshard(slots, axis):
    if axis == 1:
        return slots.transpose(1, 0, 2, 3).reshape(slots.shape[1], -1, slots.shape[3])
    return slots.transpose(1, 2, 0, 3).reshape(slots.shape[1], slots.shape[2], -1)


def kernel(x, ln_in_g, ln_in_b, w_in, conv_a_w, conv_a_b, ln_a_g, ln_a_b, qk_norm_q, qk_norm_k, sgu_ln_g, sgu_ln_b, sgu_w, sgu_b, mla_q_norm, mla_w_uq, mla_kv_norm, mla_w_ukv, w_out, ln_mix_g, ln_mix_b, ffn_w_up, ffn_conv_w, ffn_conv_b, ffn_w_down, ln_ffn_g, ln_ffn_b, loss_target, m_ln_in_g, m_ln_in_b, m_w_in, m_conv_a_w, m_conv_a_b, m_ln_a_g, m_ln_a_b, m_qk_norm_q, m_qk_norm_k, m_sgu_ln_g, m_sgu_ln_b, m_sgu_w, m_sgu_b, m_mla_q_norm, m_mla_w_uq, m_mla_kv_norm, m_mla_w_ukv, m_w_out, m_ln_mix_g, m_ln_mix_b, m_ffn_w_up, m_ffn_conv_w, m_ffn_conv_b, m_ffn_w_down, m_ln_ffn_g, m_ln_ffn_b, v_ln_in_g, v_ln_in_b, v_w_in, v_conv_a_w, v_conv_a_b, v_ln_a_g, v_ln_a_b, v_qk_norm_q, v_qk_norm_k, v_sgu_ln_g, v_sgu_ln_b, v_sgu_w, v_sgu_b, v_mla_q_norm, v_mla_w_uq, v_mla_kv_norm, v_mla_w_ukv, v_w_out, v_ln_mix_g, v_ln_mix_b, v_ffn_w_up, v_ffn_conv_w, v_ffn_conv_b, v_ffn_w_down, v_ln_ffn_g, v_ln_ffn_b):
    local = dict(ln_in_g=ln_in_g, ln_in_b=ln_in_b, w_in=w_in, conv_a_w=conv_a_w, conv_a_b=conv_a_b, ln_a_g=ln_a_g, ln_a_b=ln_a_b, qk_norm_q=qk_norm_q, qk_norm_k=qk_norm_k, sgu_ln_g=sgu_ln_g, sgu_ln_b=sgu_ln_b, sgu_w=sgu_w, sgu_b=sgu_b, mla_q_norm=mla_q_norm, mla_w_uq=mla_w_uq, mla_kv_norm=mla_kv_norm, mla_w_ukv=mla_w_ukv, w_out=w_out, ln_mix_g=ln_mix_g, ln_mix_b=ln_mix_b, ffn_w_up=ffn_w_up, ffn_conv_w=ffn_conv_w, ffn_conv_b=ffn_conv_b, ffn_w_down=ffn_w_down, ln_ffn_g=ln_ffn_g, ln_ffn_b=ln_ffn_b)
    mom = dict(ln_in_g=m_ln_in_g, ln_in_b=m_ln_in_b, w_in=m_w_in, conv_a_w=m_conv_a_w, conv_a_b=m_conv_a_b, ln_a_g=m_ln_a_g, ln_a_b=m_ln_a_b, qk_norm_q=m_qk_norm_q, qk_norm_k=m_qk_norm_k, sgu_ln_g=m_sgu_ln_g, sgu_ln_b=m_sgu_ln_b, sgu_w=m_sgu_w, sgu_b=m_sgu_b, mla_q_norm=m_mla_q_norm, mla_w_uq=m_mla_w_uq, mla_kv_norm=m_mla_kv_norm, mla_w_ukv=m_mla_w_ukv, w_out=m_w_out, ln_mix_g=m_ln_mix_g, ln_mix_b=m_ln_mix_b, ffn_w_up=m_ffn_w_up, ffn_conv_w=m_ffn_conv_w, ffn_conv_b=m_ffn_conv_b, ffn_w_down=m_ffn_w_down, ln_ffn_g=m_ln_ffn_g, ln_ffn_b=m_ln_ffn_b)
    var = dict(ln_in_g=v_ln_in_g, ln_in_b=v_ln_in_b, w_in=v_w_in, conv_a_w=v_conv_a_w, conv_a_b=v_conv_a_b, ln_a_g=v_ln_a_g, ln_a_b=v_ln_a_b, qk_norm_q=v_qk_norm_q, qk_norm_k=v_qk_norm_k, sgu_ln_g=v_sgu_ln_g, sgu_ln_b=v_sgu_ln_b, sgu_w=v_sgu_w, sgu_b=v_sgu_b, mla_q_norm=v_mla_q_norm, mla_w_uq=v_mla_w_uq, mla_kv_norm=v_mla_kv_norm, mla_w_ukv=v_mla_w_ukv, w_out=v_w_out, ln_mix_g=v_ln_mix_g, ln_mix_b=v_ln_mix_b, ffn_w_up=v_ffn_w_up, ffn_conv_w=v_ffn_conv_w, ffn_conv_b=v_ffn_conv_b, ffn_w_down=v_ffn_w_down, ln_ffn_g=v_ln_ffn_g, ln_ffn_b=v_ln_ffn_b)

    me = 4 * lax.axis_index("x") + 2 * lax.axis_index("y") + lax.axis_index("c")

    def own_slot(slots, block):
        return lax.dynamic_update_slice(slots, block[None], (me,) + (0,) * block.ndim)

    small_sharded = [n for n in SHARDED if n not in MATMUL_WEIGHTS]
    big_order = [(n, l) for l in range(DEPTH) for n in MATMUL_WEIGHTS]
    send_view = lambda n, l: (local[n].transpose(0, 2, 1)[l] if n in TRANSPOSED else local[n][l]).astype(COMM_DTYPE)
    srcs = [send_view(*big_order[0])] + [local[n] for n in small_sharded]
    srcs += [send_view(n, l) for (n, l) in big_order[1:]]
    n_first = 1 + len(small_sharded)
    groups = [list(range(n_first))] + [[n_first + j] for j in range(len(big_order) - 1)]
    g_sems, g_srcs, g_lands, tok0 = exchange_start(srcs, [True] * len(srcs), groups, "gather_start")
    tok0 = tok0[0, 0]
    pending = {key: gi for gi, key in enumerate(big_order)}

    def finish(gi, after):
        idx = groups[gi]
        _, lands = exchange_wait(g_sems[gi], [g_srcs[t] for t in idx], [g_lands[t] for t in idx], [True] * len(idx),
                                 after, f"gather_wait{gi}")
        return [own_slot(ld, srcs[t]) for ld, t in zip(lands, idx)]

    first = []

    def get_wts(after):
        first.extend(finish(0, after))
        wts = {n: local[n] for n in REPLICATED}
        for n, slots in zip(small_sharded, first[1:]):
            wts[n] = _unshard(slots, SHARDED[n])
        return wts

    def unshard_layer(slots, n):
        if SHARDED[n] == 1 or n in TRANSPOSED:
            return slots.reshape(-1, slots.shape[2])
        return slots.transpose(1, 0, 2).reshape(slots.shape[1], -1)

    def mat(l, n, after):
        gi = pending[(n, l)]
        slots = first[0] if gi == 0 else finish(gi, after)[0]
        w = unshard_layer(slots, n).astype(MXU_DTYPE)
        return _pad_w_in(w) if n == 'w_in' else w

    started = []

    def hook(key, grads):
        tensors = []
        for (n, l), g in grads.items():
            if l is None:
                tensors.append(((n, l), _shard_slots(g, SHARDED[n])))
            elif n in TRANSPOSED:
                tensors.append(((n, l), g.reshape(N_DEV, g.shape[0] // N_DEV, g.shape[1]).astype(COMM_DTYPE)))
            else:
                tensors.append(((n, l), _shard_slots(g[None], SHARDED[n])[:, 0].astype(COMM_DTYPE)))
        sems, s_srcs, s_lands, tok = exchange_start([a for _, a in tensors], [False] * len(tensors),
                                                    [list(range(len(tensors)))], "scatter_start_" + key)
        started.append((key, [k for k, _ in tensors], sems[0], s_srcs, s_lands))
        return tok[0, 0]

    loss, dx, grads = local_step(x[0], loss_target[0], (local['ln_in_g'] + tok0, local['ln_in_b']), get_wts, mat, hook)

    as2d = lambda a: a.reshape(-1, a.shape[-1]) if a.ndim > 1 else a.reshape(1, -1)
    small_g = [as2d(grads[n]) for n in REPLICATED] + [jnp.broadcast_to(loss, (8, 128))]
    p_sems, p_srcs, p_lands, p_tok = exchange_start(small_g, [True] * len(small_g), [list(range(len(small_g)))],
                                                    "gather_small_start")

    parts, res = {}, {}

    def finish_scatter(entries, after):
        for key, keys, sems, s_srcs, s_lands in entries:
            s_out, lands = exchange_wait(sems, s_srcs, s_lands, [False] * len(keys), after, "scatter_wait_" + key)
            for k, so, ld in zip(keys, s_out, lands):
                parts[k] = own_slot(ld, lax.dynamic_index_in_dim(so, me, 0, keepdims=False))

    def update(names_):
        for n in names_:
            p = [parts[(n, l)] for l in range(DEPTH)] if n in MATMUL_WEIGHTS else parts[(n, None)]
            view = (lambda a: a.transpose(0, 2, 1)) if n in TRANSPOSED else (lambda a: a)
            res[n] = tuple(view(a) for a in adamw(p, view(local[n]), view(mom[n]), view(var[n]), "adamw_" + n))

    early = ('ffn_w_up', 'ffn_w_down', 'w_out')
    finish_scatter([e for e in started if e[0] != "last"], p_tok)
    update(early)
    finish_scatter([e for e in started if e[0] == "last"], res[early[-1]][1])
    update([n for n in SHARDED if n not in early])
    updated = jnp.zeros((8, 128), F32) + sum(res[n][1][0, 0, 0] for n in SHARDED)
    p_own, p_lands = exchange_wait(p_sems[0], p_srcs, p_lands, [True] * len(small_g), updated, "gather_small_wait")
    small, loss_sum = adamw_replicated(p_lands[:-1], p_own[:-1], [as2d(local[n]) for n in REPLICATED],
                                       [as2d(mom[n]) for n in REPLICATED], [as2d(var[n]) for n in REPLICATED],
                                       p_lands[-1], p_own[-1])
    for n, quad in zip(REPLICATED, small):
        res[n] = tuple(a.reshape(local[n].shape) for a in quad)
    loss_total = loss_sum[0, 0]

    return (loss_total, dx[None], *[res[n][0] for n in WEIGHTS], *[res[n][1] for n in WEIGHTS],
            *[res[n][2] for n in WEIGHTS], *[res[n][3] for n in WEIGHTS])
```

```python
import functools
import math

import jax
import jax.numpy as jnp
from jax import lax
from jax.experimental import pallas as pl
from jax.experimental.pallas import tpu as pltpu

F32 = jnp.float32
MXU_DTYPE = jnp.bfloat16
COMM_DTYPE = jnp.bfloat16

N_DEV = 8
D_MODEL = 1024
DEPTH = 2
GRID_W = 64
GROUP_W = 256
HEAD_DIM = 64
CONV_A_WIDTH = 31
CONV_A_HALO = 16
GQA_HEADS = 4
GQA_KV_HEADS = 2
CHUNK = 128
SGU_GROUPS = 4
MLA_HEADS = 4
MLA_Q_LORA = 192
MLA_KV_LORA = 128
MLA_NOPE = 64
MLA_ROPE = 32
MLA_V = 64
MLA_DK_PAD = 128
ROPE_THETA = 10000.0
D_FF = 2816
DEEPNORM_ALPHA = (2 * DEPTH) ** 0.25
LN_EPS = 1e-5
RMS_EPS = 1e-6
D_IN_PROJ = 1888

ADAM_LR = 0.001
ADAM_B1 = 0.9
ADAM_B2 = 0.999
ADAM_EPS = 1e-08
ADAM_WD = 0.01
ADAM_STEP = 10

WEIGHTS = ['ln_in_g', 'ln_in_b', 'w_in', 'conv_a_w', 'conv_a_b', 'ln_a_g', 'ln_a_b', 'qk_norm_q', 'qk_norm_k',
           'sgu_ln_g', 'sgu_ln_b', 'sgu_w', 'sgu_b', 'mla_q_norm', 'mla_w_uq', 'mla_kv_norm', 'mla_w_ukv', 'w_out',
           'ln_mix_g', 'ln_mix_b', 'ffn_w_up', 'ffn_conv_w', 'ffn_conv_b', 'ffn_w_down', 'ln_ffn_g', 'ln_ffn_b']
SHARDED = {'w_in': 2, 'conv_a_w': 2, 'mla_w_uq': 2, 'mla_w_ukv': 2, 'w_out': 1, 'ffn_w_up': 2, 'ffn_conv_w': 2,
           'ffn_w_down': 1}
MATMUL_WEIGHTS = ('w_in', 'w_out', 'ffn_w_up', 'ffn_w_down')
TRANSPOSED = ('w_in', 'ffn_w_up')
REPLICATED = [n for n in WEIGHTS if n not in SHARDED]

ROW_TILE = 256
VMEM_LIMIT = 56 * 1024 * 1024


def _rawdot(a, b, ca, cb):
    return lax.dot_general(a.astype(MXU_DTYPE), b.astype(MXU_DTYPE), (((ca,), (cb,)), ((), ())),
                           preferred_element_type=F32)


@jax.custom_vjp
def mm_nn(a, b):
    return _rawdot(a, b, 1, 0)


def _mm_nn_fwd(a, b):
    return _rawdot(a, b, 1, 0), (a, b)


def _mm_nn_bwd(res, dy):
    a, b = res
    return _rawdot(dy, b, 1, 1), _rawdot(a, dy, 0, 0)


mm_nn.defvjp(_mm_nn_fwd, _mm_nn_bwd)


@jax.custom_vjp
def mm_nt(a, b):
    return _rawdot(a, b, 1, 1)


def _mm_nt_fwd(a, b):
    return _rawdot(a, b, 1, 1), (a, b)


def _mm_nt_bwd(res, dy):
    a, b = res
    return _rawdot(dy, b, 1, 0), _rawdot(dy, a, 0, 0)


mm_nt.defvjp(_mm_nt_fwd, _mm_nt_bwd)


def _pick_tile(d, cands):
    for c in cands:
        if d % c == 0:
            return c
    return d


def matmul(a, b, mode, out_dtype, name, b_rows=None, into=None, add=(), tm_max=1408):
    a_list = list(a) if isinstance(a, (list, tuple)) else [a]
    b_list = list(b) if isinstance(b, (list, tuple)) else [b]
    n_p = len(a_list)
    rows_list = [b_rows] if not isinstance(a, (list, tuple)) else (list(b_rows) if b_rows is not None else [None] * n_p)
    b_start, b_size = zip(*[(0, bb.shape[0]) if r is None else r for bb, r in zip(b_list, rows_list)])
    a0, b0 = a_list[0], b_list[0]
    if mode == 'nn':
        (m, k), (k2, n) = a0.shape, (b_size[0], b0.shape[1])
    elif mode == 'nt':
        (m, k), (n, k2) = a0.shape, (b_size[0], b0.shape[1])
    else:
        (k, m), (k2, n) = a0.shape, (b_size[0], b0.shape[1])
    assert k == k2 and all(x.shape == a0.shape for x in a_list) and len(set(b_size)) == 1, (a0.shape, b0.shape, mode)
    tm = _pick_tile(m, tuple(c for c in (1024, 1408, 512, 256, 128) if c <= tm_max))
    tn = _pick_tile(n, (512, 1408, 256, 128))
    tk = _pick_tile(k, (2816, 2048, 1024, 512, 256, 128))
    nk = k // tk
    ca = 0 if mode == 'tn' else 1
    cb = 1 if mode == 'nt' else 0
    b_blk = tn if mode == 'nt' else tk
    assert all(s % b_blk == 0 for s in b_start), (b_rows, b_blk)
    a_spec = pl.BlockSpec((tk, tm), lambda i, j, kk: (kk, i)) if mode == 'tn' else pl.BlockSpec((tm, tk), lambda i, j, kk: (i, kk))

    def b_spec(off):
        if mode == 'nt':
            return pl.BlockSpec((tn, tk), lambda i, j, kk: (j + off, kk))
        return pl.BlockSpec((tk, tn), lambda i, j, kk: (kk + off, j))

    in_specs, args, aliases = [], [], {}
    for x, y, s in zip(a_list, b_list, b_start):
        in_specs += [a_spec, b_spec(s // b_blk)]
        args += [x, y]
    in_specs += [pl.BlockSpec((tm, tn), lambda i, j, kk: (i, j))] * len(add)
    args += list(add)
    out_off, out_shape = 0, jax.ShapeDtypeStruct((m, n), out_dtype)
    if into is not None:
        buf, row = into
        assert row % tm == 0 and buf.shape[1] == n and buf.dtype == out_dtype, (buf.shape, row, tm)
        out_off, out_shape = row // tm, jax.ShapeDtypeStruct(buf.shape, buf.dtype)
        aliases = {len(args): 0}
        in_specs, args = in_specs + [pl.BlockSpec(memory_space=pl.ANY)], args + [buf]
    n_add = len(add)

    def body(*refs):
        o_ref, acc_ref = refs[-2:]
        kk = pl.program_id(2)

        @pl.when(kk == 0)
        def _():
            acc_ref[...] = jnp.zeros_like(acc_ref)

        for p in range(n_p):
            acc_ref[...] += _rawdot(refs[2 * p][...], refs[2 * p + 1][...], ca, cb)

        @pl.when(kk == nk - 1)
        def _():
            total = acc_ref[...]
            for r in refs[2 * n_p:2 * n_p + n_add]:
                total = total + r[...]
            o_ref[...] = total.astype(o_ref.dtype)

    return pl.pallas_call(
        body, grid=(m // tm, n // tn, nk), in_specs=in_specs,
        out_specs=pl.BlockSpec((tm, tn), lambda i, j, kk: (i + out_off, j)),
        out_shape=out_shape, input_output_aliases=aliases,
        scratch_shapes=[pltpu.VMEM((tm, tn), F32)],
        compiler_params=pltpu.CompilerParams(dimension_semantics=("parallel", "parallel", "arbitrary"),
                                             vmem_limit_bytes=VMEM_LIMIT),
        name=name)(*args)


class Op:
    def __init__(self, arr, block, imap, grad=False, acc=False, first=None, gdtype=F32, gshape=None, gimap=None):
        self.arr, self.block, self.imap = arr, block, imap
        self.grad, self.acc, self.first, self.gdtype = grad, acc, first, gdtype
        self.gshape = arr.shape if gshape is None else gshape
        self.gimap = imap if gimap is None else gimap


def _row_op(arr, tm, grad=False, gdtype=F32):
    return Op(arr, (tm, arr.shape[1]), lambda i: (i, 0), grad=grad, gdtype=gdtype)


def _par_op(arr, grad=False):
    nd = arr.ndim
    return Op(arr, arr.shape, lambda i: (0,) * nd, grad=grad, acc=True, first=lambda ids: ids[0] == 0)


def _load(ref):
    v = ref[...]
    return v.astype(F32) if jnp.issubdtype(v.dtype, jnp.floating) else v


def _store_heads(ref, val):
    rows = val.shape[0]
    if len(ref.shape) == 2:
        ref[...] = val.astype(ref.dtype)
    elif ref.shape[1] == rows:
        d = ref.shape[2]
        for h in range(ref.shape[0]):
            ref[h] = val[:, d * h:d * (h + 1)].astype(ref.dtype)
    else:
        d = ref.shape[1]
        assert ref.shape[2] == rows and d != rows, (ref.shape, val.shape)
        for h in range(ref.shape[0]):
            ref[h] = val[:, d * h:d * (h + 1)].T.astype(ref.dtype)


def _load_heads(ref, transposed=False):
    if len(ref.shape) == 2:
        return ref[...].astype(F32)
    parts = [ref[h].astype(F32) for h in range(ref.shape[0])]
    return jnp.concatenate([p.T for p in parts] if transposed else parts, axis=-1)


def stage_fwd(name, fn, ops, outs, grid):
    n_in = len(ops)

    def body(*refs):
        res = fn(*[_load(r) for r in refs[:n_in]])
        for r, o in zip(refs[n_in:], res):
            _store_heads(r, o)

    return pl.pallas_call(
        body, grid=grid, in_specs=[pl.BlockSpec(o.block, o.imap) for o in ops],
        out_specs=[pl.BlockSpec(b, im) for (_, _, b, im) in outs],
        out_shape=[jax.ShapeDtypeStruct(s, d) for (s, d, _, _) in outs],
        compiler_params=pltpu.CompilerParams(dimension_semantics=("parallel",) * len(grid),
                                             vmem_limit_bytes=VMEM_LIMIT),
        name=name)(*[o.arr for o in ops])


def stage_bwd(name, fn, ops, cts, grid, value_acc=False):
    n_in = len(ops)
    ct_flat = [(c + (False,))[:4] for group in cts if group is not None for c in group]
    n_ct = len(ct_flat)
    diff = [i for i, o in enumerate(ops) if o.grad]
    any_acc = value_acc or any(ops[i].acc for i in diff)
    ngrid = len(grid)

    def body(*refs):
        ids = [pl.program_id(a) for a in range(ngrid)]
        vals = [_load(r) for r in refs[:n_in]]
        ct_refs = refs[n_in:n_in + n_ct]
        out_refs = refs[n_in + n_ct:]

        def f(*dv):
            full = list(vals)
            for i, v in zip(diff, dv):
                full[i] = v
            return tuple(fn(*full))

        res, vjp = jax.vjp(f, *[vals[i] for i in diff])
        ct, pos = [], 0
        for group, r in zip(cts, res):
            if group is None:
                ct.append(jnp.ones_like(r))
            else:
                tot = None
                for _ in group:
                    c = _load_heads(ct_refs[pos], ct_flat[pos][3])
                    tot = c if tot is None else tot + c
                    pos += 1
                ct.append(tot)
        grads = vjp(tuple(ct))
        for i, g, r in zip(diff, grads, out_refs):
            if ops[i].acc:
                @pl.when(ops[i].first(ids))
                def _(r=r):
                    r[...] = jnp.zeros_like(r)

                r[...] += g.astype(r.dtype)
            else:
                r[...] = g.astype(r.dtype)
        if value_acc:
            r = out_refs[len(diff)]

            @pl.when(ids[0] == 0)
            def _():
                r[...] = jnp.zeros_like(r)

            r[...] += res[0]

    in_specs = [pl.BlockSpec(o.block, o.imap) for o in ops] + [pl.BlockSpec(b, im) for (_, b, im, _) in ct_flat]
    out_specs = [pl.BlockSpec(ops[i].block, ops[i].gimap) for i in diff]
    out_shape = [jax.ShapeDtypeStruct(ops[i].gshape, ops[i].gdtype) for i in diff]
    if value_acc:
        out_specs.append(pl.BlockSpec((1, 1), lambda *ids: (0, 0)))
        out_shape.append(jax.ShapeDtypeStruct((1, 1), F32))
    sem = ("arbitrary",) * ngrid if any_acc else ("parallel",) * ngrid
    return pl.pallas_call(
        body, grid=grid, in_specs=in_specs, out_specs=out_specs, out_shape=out_shape,
        compiler_params=pltpu.CompilerParams(dimension_semantics=sem, vmem_limit_bytes=VMEM_LIMIT),
        name=name)(*[o.arr for o in ops], *[a for (a, _, _, _) in ct_flat])


def _sigmoid(x):
    return 1.0 / (1.0 + jnp.exp(-x))


def _silu(x):
    return x * _sigmoid(x)


def _gelu_tanh(x):
    return 0.5 * x * (1.0 + jnp.tanh(math.sqrt(2.0 / math.pi) * (x + 0.044715 * (x * x * x))))


def _ln(x, g, b):
    mu = jnp.mean(x, axis=-1, keepdims=True)
    xc = x - mu
    var = jnp.mean(xc * xc, axis=-1, keepdims=True)
    return xc * lax.rsqrt(var + LN_EPS) * g + b


def _rms(x, g):
    ms = jnp.mean(x * x, axis=-1, keepdims=True)
    return x * lax.rsqrt(ms + RMS_EPS) * g


def _swap_halves(x, half):
    width = x.shape[-1]
    lane = lax.broadcasted_iota(jnp.int32, x.shape, 1)
    return jnp.where(lane % (2 * half) < half, pltpu.roll(x, width - half, 1), pltpu.roll(x, half, 1))


def _make_swap(half):
    @jax.custom_vjp
    def swap(x):
        return _swap_halves(x, half)

    swap.defvjp(lambda x: (_swap_halves(x, half), None), lambda _, dy: (_swap_halves(dy, half),))
    return swap


_swap16, _swap8 = _make_swap(16), _make_swap(8)


def _rope(x, cos, sin_signed, swap):
    return x * cos + swap(x) * sin_signed


def _dot_f32(a, b):
    return jnp.dot(a, b, preferred_element_type=F32, precision=lax.Precision.HIGHEST)


def fn_ln(x, g, b):
    return (_ln(x, g, b),)


def _twice(fn):
    def f(*a):
        (y,) = fn(*a)
        return y, y
    return f


PROJ_W = 2048
P_A, P_Q, P_K, P_V, P_C, P_CQ, P_CKV, P_KR = 0, 512, 768, 896, 1024, 1536, 1792, 1920
CQ_PAD = 256
_CQ_END = P_CQ + MLA_Q_LORA


def _pad_w_in(wt):
    z = lambda n: jnp.zeros((n, wt.shape[1]), wt.dtype)
    return jnp.concatenate([wt[:_CQ_END], z(P_CKV - _CQ_END), wt[_CQ_END:], z(PROJ_W - P_KR - MLA_ROPE)], axis=0)


def _unpad_w_in(gt):
    return jnp.concatenate([gt[:_CQ_END], gt[P_CKV:P_KR + MLA_ROPE]], axis=0)


def fn_pre(proj, tab_q, tab_d, seg, place, qng, kng, sg, sb, sw, sbt, mqn, wuq, mkvn, wukv):
    tm = proj.shape[0]
    aglu = proj[:, P_A:P_A + GROUP_W] * _sigmoid(proj[:, P_A + GROUP_W:P_Q])
    b_q, b_k, b_v = proj[:, P_Q:P_K], proj[:, P_K:P_V], proj[:, P_V:P_C]
    cos_q, sin_q = tab_q[:, :GROUP_W], tab_q[:, GROUP_W:]
    q = b_q * lax.rsqrt(_dot_f32(b_q * b_q, seg) + RMS_EPS) * qng
    q = _rope(q, cos_q, sin_q, _swap16)
    k = b_k * lax.rsqrt(_dot_f32(b_k * b_k, seg[:128, :128]) + RMS_EPS) * kng
    k = _rope(k, cos_q[:, :128], sin_q[:, :128], _swap16)
    c = _gelu_tanh(proj[:, P_C:P_CQ])
    u, sv = c[:, :GROUP_W], _ln(c[:, GROUP_W:], sg, sb)
    group = lax.broadcasted_iota(jnp.int32, (CHUNK, GROUP_W), 1) // HEAD_DIM
    rows = []
    for n in range(tm // CHUNK):
        svn = sv[CHUNK * n:CHUNK * (n + 1)]
        acc = jnp.zeros((CHUNK, GROUP_W), F32)
        for g in range(SGU_GROUPS):
            acc = acc + jnp.where(group == g, mm_nn(sw[CHUNK * g:CHUNK * (g + 1)], svn) + sbt[:, g:g + 1], 0.0)
        rows.append(acc)
    o_c = u * jnp.concatenate(rows, axis=0)
    d_cq, d_ckv, d_kr = proj[:, P_CQ:P_CKV], proj[:, P_CKV:P_KR], proj[:, P_KR:PROJ_W]
    cqn = d_cq * lax.rsqrt(jnp.sum(d_cq * d_cq, axis=-1, keepdims=True) * (1.0 / MLA_Q_LORA) + RMS_EPS) * mqn
    cos_d = jnp.concatenate([tab_d[:, :MLA_DK_PAD]] * MLA_HEADS, axis=-1)
    sin_d = jnp.concatenate([tab_d[:, MLA_DK_PAD:]] * MLA_HEADS, axis=-1)
    qf = _rope(mm_nn(cqn, wuq), cos_d, sin_d, _swap8)
    kvd = mm_nn(_rms(d_ckv, mkvn), wukv)
    kf = _rope(kvd[:, :MLA_HEADS * MLA_DK_PAD] + _dot_f32(d_kr, place), cos_d, sin_d, _swap8)
    return aglu, q, k, b_v, o_c, qf, kf, kvd[:, MLA_HEADS * MLA_DK_PAD:]


def fn_aconv(win, w, b, g, beta):
    tm = win.shape[0] - 2 * CONV_A_HALO
    off = CONV_A_HALO - CONV_A_WIDTH // 2
    rolled = [win] + [_roll_rows(win, -r) for r in range(1, 8)]
    acc = None
    for kk in range(CONV_A_WIDTH):
        r = (off + kk) % 8
        base = off + kk - r
        term = rolled[r][base:base + tm] * w[kk:kk + 1, :]
        acc = term if acc is None else acc + term
    return (_silu(_ln(acc + b, g, beta)),)


def fn_resln(h, r, g, b):
    return (_ln(DEEPNORM_ALPHA * h + r, g, b),)


@functools.partial(jax.custom_vjp, nondiff_argnums=(1,))
def _roll_rows(x, shift):
    return pltpu.roll(x, shift % x.shape[0], 0)


_roll_rows.defvjp(lambda x, shift: (pltpu.roll(x, shift % x.shape[0], 0), None),
                  lambda shift, _, dy: (pltpu.roll(dy, (-shift) % dy.shape[0], 0),))


def _shift_down(x):
    row = lax.broadcasted_iota(jnp.int32, x.shape, 0)
    return jnp.where(row == 0, 0.0, _roll_rows(x, 1))


def _shift_up(x):
    row = lax.broadcasted_iota(jnp.int32, x.shape, 0)
    return jnp.where(row == x.shape[0] - 1, 0.0, _roll_rows(x, -1))


def fn_ffnconv(u1, u2, w1, w2, b1, b2):
    c1 = _shift_down(u1) * w1[0:1] + u1 * w1[1:2] + _shift_up(u1) * w1[2:3] + b1
    c2 = _shift_down(u2) * w2[0:1] + u2 * w2[1:2] + _shift_up(u2) * w2[2:3] + b2
    return (_silu(c1) * c2,)


def fn_final(h, r, t, g, b):
    y = _ln(DEEPNORM_ALPHA * h + r, g, b)
    err = (y - t) * (y - t)
    return (0.5 * jnp.sum(jnp.mean(err, axis=-1, keepdims=True), axis=0, keepdims=True),)


def _rope_tables(seq):
    n_rows = seq // GRID_W
    lane128 = jnp.arange(128)

    def tile_tables(j, rotated, half):
        inv = ROPE_THETA ** (-(j % half).astype(F32) / half)
        by_row, by_col = rotated & (j < 2 * half), rotated & (j >= 2 * half)
        sign = jnp.where(j % (2 * half) < half, -1.0, 1.0)
        ar = jnp.arange(n_rows, dtype=F32)[:, None] * inv[None, :]
        ac = jnp.arange(GRID_W, dtype=F32)[:, None] * inv[None, :]
        grid = lambda r, c: (jnp.where(by_row, r, 0.0)[:, None, :] + jnp.where(by_col, c, 0.0)[None, :, :])
        cos = grid(jnp.cos(ar), jnp.cos(ac)) + jnp.where(rotated, 0.0, 1.0)
        sin = grid(sign * jnp.sin(ar), sign * jnp.sin(ac))
        return cos.reshape(seq, 128), sin.reshape(seq, 128)

    cos_b, sin_b = tile_tables(lane128 % HEAD_DIM, lane128 >= 0, HEAD_DIM // 4)
    tab_q = jnp.concatenate([cos_b] * (GROUP_W // 128) + [sin_b] * (GROUP_W // 128), axis=-1)
    tab_d = jnp.concatenate(tile_tables(lane128 - MLA_NOPE, (lane128 >= MLA_NOPE) & (lane128 < MLA_NOPE + MLA_ROPE),
                                        MLA_ROPE // 4), axis=-1)
    lane = jnp.arange(GROUP_W)
    seg = jnp.where(lane[:, None] // HEAD_DIM == lane[None, :] // HEAD_DIM, 1.0 / HEAD_DIM, 0.0).astype(F32)
    src, dst = jnp.arange(128)[:, None], jnp.arange(MLA_HEADS * MLA_DK_PAD)[None, :]
    place = jnp.where((src < MLA_ROPE) & (dst % MLA_DK_PAD == MLA_NOPE + src), 1.0, 0.0).astype(F32)
    return tab_q, tab_d, seg, place


def _pre_ops(proj, tabs, kp, grad):
    tm = ROW_TILE
    ops = [_row_op(proj, tm, grad=grad, gdtype=MXU_DTYPE), _row_op(tabs[0], tm), _row_op(tabs[1], tm),
           _par_op(tabs[2]), _par_op(tabs[3])]
    ops += [_par_op(kp[n], grad=grad) for n in ('qng', 'kng', 'sg', 'sb', 'sw', 'sbt', 'mqn', 'wuq', 'mkvn', 'wukv')]
    return ops


PRE_OUTS = ((0, GROUP_W), (GQA_HEADS, HEAD_DIM), (GQA_KV_HEADS, HEAD_DIM), (GQA_KV_HEADS, HEAD_DIM), (0, GROUP_W),
            (MLA_HEADS, MLA_DK_PAD), (MLA_HEADS, MLA_DK_PAD), (MLA_HEADS, MLA_V))


def _pre_out_specs(seq, tm):
    specs = []
    for heads, w in PRE_OUTS:
        if heads:
            specs.append(((heads, seq, w), (heads, tm, w), lambda i: (0, i, 0)))
        else:
            specs.append(((seq, w), (tm, w), lambda i: (i, 0)))
    return specs


PRE_KV = (2, 3, 6, 7)


def _transposed_spec(heads, w, seq, tm):
    return (heads, w, seq), (heads, w, tm), lambda i: (0, 0, i)


def pre_fwd(proj, tabs, kp, tag):
    seq = proj.shape[0]
    tm = ROW_TILE
    dts = (F32,) + (MXU_DTYPE,) * 7
    outs = [(shape, dt, block, imap) for (shape, block, imap), dt in zip(_pre_out_specs(seq, tm), dts)]
    return stage_fwd("pre_fwd" + tag, fn_pre, _pre_ops(proj, tabs, kp, False), outs, (seq // tm,))


def pre_bwd(proj, tabs, kp, cts, tag):
    seq = proj.shape[0]
    tm = ROW_TILE
    ct = []
    for j, (c, (_, block, imap)) in enumerate(zip(cts, _pre_out_specs(seq, tm))):
        if isinstance(c, tuple):
            ct.append([c])
        elif j in PRE_KV:
            ct.append([(c,) + _transposed_spec(*PRE_OUTS[j], seq, tm)[1:] + (True,)])
        else:
            ct.append([(c, block, imap)])
    return stage_bwd("pre_bwd" + tag, fn_pre, _pre_ops(proj, tabs, kp, True), ct, (seq // tm,))


def _aconv_ops(kp, grad):
    return [_par_op(kp[n], grad=grad) for n in ('caw', 'cab', 'lag', 'lab')]


def aconv_fwd(aglu_pad, kp, tag):
    seq = aglu_pad.shape[0] - 2 * CONV_A_HALO
    tm = ROW_TILE
    n_par = 4

    def body(x_ref, *refs):
        i = pl.program_id(0)
        win = x_ref[pl.ds(pl.multiple_of(i * tm, tm), tm + 2 * CONV_A_HALO), :]
        (o,) = fn_aconv(win, *[_load(r) for r in refs[:n_par]])
        refs[n_par][...] = o.astype(refs[n_par].dtype)

    pars = _aconv_ops(kp, False)
    return pl.pallas_call(
        body, grid=(seq // tm,),
        in_specs=[pl.BlockSpec(aglu_pad.shape, lambda i: (0, 0))] + [pl.BlockSpec(o.block, o.imap) for o in pars],
        out_specs=pl.BlockSpec((tm, GROUP_W), lambda i: (i, 0)),
        out_shape=jax.ShapeDtypeStruct((seq, GROUP_W), MXU_DTYPE),
        compiler_params=pltpu.CompilerParams(dimension_semantics=("parallel",), vmem_limit_bytes=VMEM_LIMIT),
        name="aconv_fwd" + tag)(aglu_pad, *[o.arr for o in pars])


def aconv_bwd(aglu_pad, kp, d_oa, tag):
    seq = aglu_pad.shape[0] - 2 * CONV_A_HALO
    tm = ROW_TILE
    n_par = 4

    def body(x_ref, *refs):
        i = pl.program_id(0)
        rows = pl.ds(pl.multiple_of(i * tm, tm), tm + 2 * CONV_A_HALO)
        pars = [_load(r) for r in refs[:n_par]]
        ct = refs[n_par][...].astype(F32)
        outs = refs[n_par + 1:]
        _, vjp = jax.vjp(lambda *a: fn_aconv(*a), x_ref[rows, :], *pars)
        grads = vjp((ct,))

        @pl.when(i == 0)
        def _():
            for r in outs:
                r[...] = jnp.zeros_like(r)

        outs[0][rows, :] += grads[0]
        for r, g in zip(outs[1:], grads[1:]):
            r[...] += g

    pars = _aconv_ops(kp, True)
    whole = pl.BlockSpec(aglu_pad.shape, lambda i: (0, 0))
    par_specs = [pl.BlockSpec(o.block, o.imap) for o in pars]
    return pl.pallas_call(
        body, grid=(seq // tm,),
        in_specs=[whole] + par_specs + [pl.BlockSpec((tm, GROUP_W), lambda i: (i, 0))],
        out_specs=[whole] + par_specs,
        out_shape=[jax.ShapeDtypeStruct(aglu_pad.shape, F32)] + [jax.ShapeDtypeStruct(o.arr.shape, F32) for o in pars],
        compiler_params=pltpu.CompilerParams(dimension_semantics=("arbitrary",), vmem_limit_bytes=VMEM_LIMIT),
        name="aconv_bwd" + tag)(aglu_pad, *[o.arr for o in pars], d_oa)


ATTN_TQ_FWD = 512
ATTN_TQ = 256
ATTN_TK = 512


def attn_fwd(q3, k3, v3t, scale, tag):
    heads, seq, dk = q3.shape
    group = heads // k3.shape[0]
    kv_per_pair = 2 // group
    dv = v3t.shape[1]
    tq, tk = min(ATTN_TQ_FWD, seq), min(ATTN_TK, seq)
    n_chunks = seq // tk
    log2e = math.log2(math.e)

    def one_head(q, k_ref, vt_ref):
        scores = lambda c: _rawdot(k_ref[pl.ds(c * tk, tk), :], q, 1, 1)
        m, l, acc = jnp.full((1, tq), -jnp.inf, F32), jnp.zeros((1, tq), F32), jnp.zeros((dv, tq), F32)
        s_next = scores(0)
        for c in range(n_chunks):
            s_cur, s_next = s_next, (scores(c + 1) if c + 1 < n_chunks else None)
            t = s_cur * (scale * log2e)
            m_new = jnp.maximum(m, jnp.max(t, axis=0, keepdims=True))
            alpha = jnp.exp2(m - m_new)
            p = jnp.exp2(t - m_new)
            l = alpha * l + jnp.sum(p, axis=0, keepdims=True)
            acc = alpha * acc + _rawdot(vt_ref[:, c * tk:(c + 1) * tk], p, 1, 0)
            m = m_new
        return (acc * (1.0 / l)).T, (m * (1.0 / log2e) + jnp.log(l)).T

    def body(q_ref, k_ref, v_ref, o_ref, lse_ref):
        outs = []
        for h in range(2):
            o, lse = one_head(q_ref[h], k_ref.at[h // group], v_ref.at[h // group])
            lse_ref[h] = lse
            outs.append(o)
        o_ref[...] = jnp.concatenate(outs, axis=-1)

    return pl.pallas_call(
        body, grid=(heads // 2, seq // tq),
        in_specs=[pl.BlockSpec((2, tq, dk), lambda j, i: (j, i, 0)),
                  pl.BlockSpec((kv_per_pair, seq, dk), lambda j, i: (j, 0, 0)),
                  pl.BlockSpec((kv_per_pair, dv, seq), lambda j, i: (j, 0, 0))],
        out_specs=[pl.BlockSpec((tq, 2 * dv), lambda j, i: (i, j)),
                   pl.BlockSpec((2, tq, 1), lambda j, i: (j, i, 0))],
        out_shape=[jax.ShapeDtypeStruct((seq, heads * dv), F32), jax.ShapeDtypeStruct((heads, seq, 1), F32)],
        compiler_params=pltpu.CompilerParams(dimension_semantics=("parallel", "parallel"),
                                             vmem_limit_bytes=VMEM_LIMIT),
        name="attn_fwd" + tag)(q3, k3, v3t)


def attn_bwd(q3, k3, v3, o, lse3, do_all, do_col, scale, tag):
    heads, seq, dk = q3.shape
    group = heads // k3.shape[0]
    kv_per_pair = 2 // group
    dv = v3.shape[2]
    tq, tk = min(ATTN_TQ, seq), min(ATTN_TK, seq)
    n_chunks = seq // tk
    log2e = math.log2(math.e)

    def one_head(q, do, o_h, lse, k_ref, v_ref, dk_ref, dv_ref):
        dob = do.astype(MXU_DTYPE)
        do_t, q_t = do.T.astype(MXU_DTYPE), q.astype(F32).T.astype(MXU_DTYPE)
        delta = jnp.sum(do * o_h, axis=-1, keepdims=True)
        lse2 = lse * log2e
        rows = lambda c: pl.ds(c * tk, tk)
        products = lambda c: (_rawdot(q, k_ref[rows(c), :], 1, 1), _rawdot(dob, v_ref[rows(c), :], 1, 1))
        dq = jnp.zeros((tq, dk), F32)
        nxt = products(0)
        for c in range(n_chunks):
            (s_cur, dp_cur), nxt = nxt, (products(c + 1) if c + 1 < n_chunks else None)
            p = jnp.exp2(s_cur * (scale * log2e) - lse2)
            ds = (p * ((dp_cur - delta) * scale)).astype(MXU_DTYPE)
            dv_ref[:, c * tk:(c + 1) * tk] += _rawdot(do_t, p, 1, 0)
            dk_ref[:, c * tk:(c + 1) * tk] += _rawdot(q_t, ds, 1, 0)
            dq = dq + _rawdot(ds, k_ref[rows(c), :], 1, 0)
        return dq

    def body(q_ref, k_ref, v_ref, o_ref, lse_ref, do_ref, dq_ref, dk_ref, dv_ref):
        @pl.when(pl.program_id(1) == 0)
        def _():
            dk_ref[...] = jnp.zeros_like(dk_ref)
            dv_ref[...] = jnp.zeros_like(dv_ref)

        do_pair, o_pair = do_ref[...], o_ref[...]
        for h in range(2):
            kv = h // group
            dq_ref[h] = one_head(q_ref[h], do_pair[:, dv * h:dv * (h + 1)], o_pair[:, dv * h:dv * (h + 1)],
                                 lse_ref[h], k_ref.at[kv], v_ref.at[kv], dk_ref.at[kv], dv_ref.at[kv])

    qspec = lambda d: pl.BlockSpec((2, tq, d), lambda j, i: (j, i, 0))
    kvspec = lambda d: pl.BlockSpec((kv_per_pair, seq, d), lambda j, i: (j, 0, 0))
    kvt_spec = lambda d: pl.BlockSpec((kv_per_pair, d, seq), lambda j, i: (j, 0, 0))
    kvt_shape = lambda a: jax.ShapeDtypeStruct((a.shape[0], a.shape[2], a.shape[1]), F32)
    return pl.pallas_call(
        body, grid=(heads // 2, seq // tq),
        in_specs=[qspec(dk), kvspec(dk), kvspec(dv), pl.BlockSpec((tq, 2 * dv), lambda j, i: (i, j)), qspec(1),
                  pl.BlockSpec((tq, 2 * dv), lambda j, i: (i, do_col + j))],
        out_specs=[qspec(dk), kvt_spec(dk), kvt_spec(dv)],
        out_shape=[jax.ShapeDtypeStruct(q3.shape, F32), kvt_shape(k3), kvt_shape(v3)],
        compiler_params=pltpu.CompilerParams(dimension_semantics=("parallel", "arbitrary"),
                                             vmem_limit_bytes=VMEM_LIMIT),
        name="attn_bwd" + tag)(q3, k3, v3, o, lse3, do_all)


def resln_fwd(h, r, g, b, tag):
    seq, d = h.shape
    tm = ROW_TILE
    ops = [_row_op(h, tm), _row_op(r, tm), _par_op(g), _par_op(b)]
    outs = [((seq, d), dt, (tm, d), lambda i: (i, 0)) for dt in (F32, MXU_DTYPE)]
    return stage_fwd("resln_fwd" + tag, _twice(fn_resln), ops, outs, (seq // tm,))


def resln_bwd(h, r, g, b, dys, tag):
    seq, d = h.shape
    tm = ROW_TILE
    ops = [_row_op(h, tm, grad=True), _row_op(r, tm, grad=True, gdtype=MXU_DTYPE), _par_op(g, grad=True),
           _par_op(b, grad=True)]
    ct = [[(dy, (tm, d), lambda i: (i, 0)) for dy in dys]]
    return stage_bwd("resln_bwd" + tag, fn_resln, ops, ct, (seq // tm,))


def _ffnconv_ops(up1, up2, w, b, grad):
    seq = up1.shape[0]
    nblk = D_FF // 128
    lo, hi = (lambda j: (0, j)), (lambda j: (0, j + nblk))
    half = lambda a: dict(gshape=(a.shape[0], D_FF), gimap=lo)
    return [Op(up1, (seq, 128), lo, grad=grad, gdtype=MXU_DTYPE), Op(up2, (seq, 128), lo, grad=grad, gdtype=MXU_DTYPE),
            Op(w, (3, 128), lo, grad=grad, **half(w)), Op(w, (3, 128), hi, grad=grad, **half(w)),
            Op(b, (1, 128), lo, grad=grad, **half(b)), Op(b, (1, 128), hi, grad=grad, **half(b))]


def ffnconv_fwd(up1, up2, w, b, tag):
    seq = up1.shape[0]
    outs = [((seq, D_FF), MXU_DTYPE, (seq, 128), lambda j: (0, j))]
    return stage_fwd("ffnconv_fwd" + tag, fn_ffnconv, _ffnconv_ops(up1, up2, w, b, False), outs, (D_FF // 128,))[0]


def ffnconv_bwd(up1, up2, w, b, dact, tag):
    seq = up1.shape[0]
    ct = [[(dact, (seq, 128), lambda j: (0, j))]]
    du1, du2, dw1, dw2, db1, db2 = stage_bwd("ffnconv_bwd" + tag, fn_ffnconv, _ffnconv_ops(up1, up2, w, b, True),
                                             ct, (D_FF // 128,))
    cat = lambda a, b_: jnp.concatenate([a, b_], axis=-1)
    return du1, du2, cat(dw1, dw2), cat(db1, db2)


def final_bwd(h, r, t, g, b, tag):
    seq, d = h.shape
    tm = ROW_TILE
    ops = [_row_op(h, tm, grad=True), _row_op(r, tm, grad=True, gdtype=MXU_DTYPE), _row_op(t, tm),
           _par_op(g, grad=True), _par_op(b, grad=True)]
    return stage_bwd("final_bwd" + tag, fn_final, ops, [None], (seq // tm,), value_acc=True)


def _layer_params(wts, l):
    row = lambda a: a.reshape(1, -1)
    wuq = wts['mla_w_uq'][l].reshape(MLA_Q_LORA, MLA_HEADS, MLA_NOPE + MLA_ROPE)
    wuq = jnp.pad(wuq, ((0, CQ_PAD - MLA_Q_LORA), (0, 0), (0, MLA_DK_PAD - MLA_NOPE - MLA_ROPE)))
    wukv = wts['mla_w_ukv'][l].reshape(MLA_KV_LORA, MLA_HEADS, MLA_NOPE + MLA_V)
    wuk = jnp.pad(wukv[:, :, :MLA_NOPE], ((0, 0), (0, 0), (0, MLA_DK_PAD - MLA_NOPE)))
    return dict(
        qng=jnp.tile(row(wts['qk_norm_q'][l]), (1, GQA_HEADS)), kng=jnp.tile(row(wts['qk_norm_k'][l]), (1, GQA_KV_HEADS)),
        sg=row(wts['sgu_ln_g'][l]), sb=row(wts['sgu_ln_b'][l]),
        sw=wts['sgu_w'][l].reshape(SGU_GROUPS * CHUNK, CHUNK), sbt=wts['sgu_b'][l].T,
        mqn=jnp.pad(row(wts['mla_q_norm'][l]), ((0, 0), (0, CQ_PAD - MLA_Q_LORA))),
        wuq=wuq.reshape(CQ_PAD, MLA_HEADS * MLA_DK_PAD), mkvn=row(wts['mla_kv_norm'][l]),
        wukv=jnp.concatenate([wuk.reshape(MLA_KV_LORA, -1), wukv[:, :, MLA_NOPE:].reshape(MLA_KV_LORA, -1)], axis=1),
        caw=wts['conv_a_w'][l], cab=row(wts['conv_a_b'][l]), lag=row(wts['ln_a_g'][l]), lab=row(wts['ln_a_b'][l]),
        lmg=row(wts['ln_mix_g'][l]), lmb=row(wts['ln_mix_b'][l]),
        fcw=wts['ffn_conv_w'][l], fcb=row(wts['ffn_conv_b'][l]),
        lfg=row(wts['ln_ffn_g'][l]), lfb=row(wts['ln_ffn_b'][l]))


def _to_heads(a, heads):
    seq = a.shape[0]
    return a.reshape(seq, heads, -1).transpose(1, 0, 2)


def _from_heads(a3):
    return a3.transpose(1, 0, 2).reshape(a3.shape[1], -1)


def local_step(x, target, ln_in, get_wts, mat, hook):
    seq = x.shape[0]
    tm = ROW_TILE
    tabs = _rope_tables(seq)
    scale_b = HEAD_DIM ** -0.5
    scale_d = (MLA_NOPE + MLA_ROPE) ** -0.5
    ln_in_g, ln_in_b = ln_in[0].reshape(1, -1), ln_in[1].reshape(1, -1)

    h, h_m = stage_fwd("ln_in_fwd", _twice(fn_ln), [_row_op(x, tm), _par_op(ln_in_g), _par_op(ln_in_b)],
                       [((seq, D_MODEL), dt, (tm, D_MODEL), lambda i: (i, 0)) for dt in (F32, MXU_DTYPE)],
                       (seq // tm,))
    wts = get_wts(h_m)
    saved = []
    for l in range(DEPTH):
        tag = f"_l{l}"
        kp, unprep = jax.vjp(lambda w: _layer_params(w, l), wts)
        m = {'w_in': mat(l, 'w_in', h_m)}
        proj = matmul(h_m, m['w_in'], 'nt', F32, "mm_proj" + tag)
        aglu, q3, k3, v3, o_c, qd3, kd3, vd3 = pre_fwd(proj, tabs, kp, tag)
        aglu_pad = jnp.pad(aglu, ((CONV_A_HALO, CONV_A_HALO), (0, 0)))
        o_a = aconv_fwd(aglu_pad, kp, tag)
        swap = lambda a: a.transpose(0, 2, 1)
        o_b3, lse_b3 = attn_fwd(q3, k3, swap(v3), scale_b, "_b" + tag)
        o_d3, lse_d3 = attn_fwd(qd3, kd3, swap(vd3), scale_d, "_d" + tag)
        o_cat = jnp.concatenate([o_a, o_b3.astype(MXU_DTYPE), o_c, o_d3.astype(MXU_DTYPE)], axis=-1)
        m['w_out'] = mat(l, 'w_out', o_cat)
        mix = matmul(o_cat, m['w_out'], 'nn', F32, "mm_mix" + tag)
        h1, h1_m = resln_fwd(h, mix, kp['lmg'], kp['lmb'], "_mix" + tag)
        m['ffn_w_up'] = mat(l, 'ffn_w_up', h1_m)
        up1 = matmul(h1_m, m['ffn_w_up'], 'nt', F32, "mm_up1" + tag, b_rows=(0, D_FF))
        up2 = matmul(h1_m, m['ffn_w_up'], 'nt', F32, "mm_up2" + tag, b_rows=(D_FF, D_FF))
        act = ffnconv_fwd(up1, up2, kp['fcw'], kp['fcb'], tag)
        m['ffn_w_down'] = mat(l, 'ffn_w_down', act)
        f = matmul(act, m['ffn_w_down'], 'nn', F32, "mm_down" + tag)
        saved.append(dict(kp=kp, unprep=unprep, m=m, h=h, h_m=h_m, h1_m=h1_m, proj=proj, o_b3=o_b3, lse_b3=lse_b3,
                          o_d3=o_d3, lse_d3=lse_d3, aglu_pad=aglu_pad, q3=q3, k3=k3, v3=v3, qd3=qd3,
                          kd3=kd3, vd3=vd3, o_cat=o_cat, mix=mix, h1=h1, up1=up1, up2=up2, act=act, f=f))
        if l + 1 < DEPTH:
            h, h_m = resln_fwd(h1, f, kp['lfg'], kp['lfb'], "_ffn" + tag)

    after = lambda a, tok: a if tok is None else a + tok
    small_acc = None
    dh_parts = None
    loss = None
    tok = None
    g_mix = None
    for l in reversed(range(DEPTH)):
        tag = f"_l{l}"
        s = saved[l]
        kp, m = s['kp'], s['m']
        dkp = {}
        lfg = after(kp['lfg'], tok)
        if l == DEPTH - 1:
            dh1_a, df, dkp['lfg'], dkp['lfb'], loss = final_bwd(s['h1'], s['f'], target, lfg, kp['lfb'], tag)
        else:
            dh1_a, df, dkp['lfg'], dkp['lfb'] = resln_bwd(s['h1'], s['f'], lfg, kp['lfb'], dh_parts, "_ffn" + tag)
        g_down = matmul(s['act'], df, 'tn', COMM_DTYPE, "mm_gdown" + tag)
        dact = matmul(df, m['ffn_w_down'], 'nt', F32, "mm_dact" + tag)
        dup1, dup2, dkp['fcw'], dkp['fcb'] = ffnconv_bwd(s['up1'], s['up2'], kp['fcw'], kp['fcb'], dact, tag)
        dh1 = matmul([dup1, dup2], [m['ffn_w_up']] * 2, 'nn', F32, "mm_dh1" + tag,
                     b_rows=[(0, D_FF), (D_FF, D_FF)], add=[dh1_a], tm_max=512)
        g_up = matmul(dup1, s['h1_m'], 'tn', COMM_DTYPE, "mm_gup1" + tag,
                      into=(lax.empty((2 * D_FF, D_MODEL), COMM_DTYPE), 0))
        g_up = matmul(dup2, s['h1_m'], 'tn', COMM_DTYPE, "mm_gup2" + tag, into=(g_up, D_FF))
        tok = hook(f"ffn{l}", {('ffn_w_down', l): g_down, ('ffn_w_up', l): g_up})
        dh_a, dmix, dkp['lmg'], dkp['lmb'] = resln_bwd(s['h'], s['mix'], after(kp['lmg'], tok), kp['lmb'],
                                                       [dh1], "_mix" + tag)
        g_out = matmul(s['o_cat'], dmix, 'tn', COMM_DTYPE, "mm_gout" + tag)
        w_out = m['w_out']
        if l == 0:
            w_out = w_out + hook("out0", {('w_out', l): g_out}).astype(w_out.dtype)
        do_cat = matmul(dmix, w_out, 'nt', F32, "mm_docat" + tag)
        lse_b3 = s['lse_b3']
        do_c = (do_cat, (ROW_TILE, GROUP_W), lambda i: (i, 2))
        pair_w = 2 * HEAD_DIM
        dq3, dk3, dv3 = attn_bwd(s['q3'], s['k3'], s['v3'], s['o_b3'], lse_b3, do_cat, GROUP_W // pair_w,
                                 scale_b, "_b" + tag)
        dqd3, dkd3, dvd3 = attn_bwd(s['qd3'], s['kd3'], s['vd3'], s['o_d3'], s['lse_d3'], do_cat,
                                    3 * GROUP_W // pair_w, scale_d, "_d" + tag)
        daglu_pad, dkp['caw'], dkp['cab'], dkp['lag'], dkp['lab'] = aconv_bwd(s['aglu_pad'], kp, do_cat, tag)
        cts = [daglu_pad[CONV_A_HALO:CONV_A_HALO + seq], dq3, dk3, dv3, do_c, dqd3, dkd3, dvd3]
        pre_g = pre_bwd(s['proj'], tabs, kp, cts, tag)
        dproj = pre_g[0]
        for n, g in zip(('qng', 'kng', 'sg', 'sb', 'sw', 'sbt', 'mqn', 'wuq', 'mkvn', 'wukv'), pre_g[1:]):
            dkp[n] = g
        dh_parts = [matmul(dproj, m['w_in'], 'nn', F32, "mm_dh" + tag, add=[dh_a])]
        g_in = matmul(dproj, s['h_m'], 'tn', COMM_DTYPE, "mm_gin" + tag)
        (dw,) = s['unprep'](dkp)
        small_acc = dw if small_acc is None else jax.tree.map(jnp.add, small_acc, dw)
        g_mix = {('w_out', l): g_out, ('w_in', l): _unpad_w_in(g_in)}
        if l > 0:
            tok = hook(f"mix{l}", g_mix)
        else:
            g_mix.pop(('w_out', l))

    g_mix.update({(n, None): small_acc[n] for n in SHARDED if n not in MATMUL_WEIGHTS})
    tok = hook("last", g_mix)
    dx, dg, db = stage_bwd("ln_in_bwd", fn_ln,
                           [_row_op(x, tm, grad=True), _par_op(after(ln_in_g, tok), grad=True),
                            _par_op(ln_in_b, grad=True)],
                           [[(p, (tm, D_MODEL), lambda i: (i, 0)) for p in dh_parts]], (seq // tm,))
    out = {n: small_acc[n] for n in REPLICATED}
    out['ln_in_g'], out['ln_in_b'] = dg.reshape(-1), db.reshape(-1)
    return loss, dx, out


def _peer(x, y, c, r):
    return ((1 - x) if r & 4 else x, (1 - y) if r & 2 else y, (1 - c) if r & 1 else c)


def _exchange_copy(src_ref, land_ref, send_sems, recv_sems, k, gather, x, y, c, r):
    px, py, pc = _peer(x, y, c, r)
    me, peer = 4 * x + 2 * y + c, 4 * px + 2 * py + pc
    src = src_ref if gather else src_ref.at[peer]
    mk = lambda dst: pltpu.make_async_remote_copy(
        src_ref=src, dst_ref=dst, send_sem=send_sems.at[k * (N_DEV - 1) + r - 1],
        recv_sem=recv_sems.at[k * (N_DEV - 1) + r - 1],
        device_id=(px, py, pc), device_id_type=pl.DeviceIdType.MESH)
    return mk(land_ref.at[me]), mk(land_ref.at[peer])


_HBM_SPEC = pl.BlockSpec(memory_space=pltpu.HBM)
_SEM_SPEC = pl.BlockSpec(memory_space=pltpu.SEMAPHORE)


def exchange_start(srcs, gather, groups, name):
    n_t = len(srcs)
    lands =[lax.empty(((N_DEV,) + s.shape) if gt else s.shape, s.dtype) for s, gt in zip(srcs, gather)]

    def body(*refs):
        src_refs, land_refs = refs[:n_t], refs[n_t:2 * n_t]
        sem_refs = refs[2 * n_t:2 * n_t + 2 * len(groups)]
        token = refs[-1]
        x, y, c = lax.axis_index("x"), lax.axis_index("y"), lax.axis_index("c")
        for gi, g in enumerate(groups):
            for k, t in enumerate(g):
                for r in range(1, N_DEV):
                    _exchange_copy(src_refs[t], land_refs[t], sem_refs[2 * gi], sem_refs[2 * gi + 1], k, gather[t],
                                   x, y, c, r)[0].start()
        token[...] = jnp.zeros_like(token)

    sem_shapes = []
    for g in groups:
        sem_shapes += [pltpu.SemaphoreType.DMA((len(g) * (N_DEV - 1),))] * 2
    hbm_shapes = [pltpu.HBM(a.shape, a.dtype) for a in list(srcs) + lands]
    n_sem = len(sem_shapes)
    res = pl.pallas_call(
        body, name=name,
        out_shape=tuple(sem_shapes + hbm_shapes + [jax.ShapeDtypeStruct((8, 128), F32)]),
        in_specs=[_HBM_SPEC] * (2 * n_t),
        out_specs=tuple([_SEM_SPEC] * n_sem + [_HBM_SPEC] * (2 * n_t) + [pl.BlockSpec(memory_space=pltpu.VMEM)]),
        input_output_aliases={i: n_sem + i for i in range(2 * n_t)},
        compiler_params=pltpu.CompilerParams(has_side_effects=pltpu.SideEffectType.DATAFLOW_SIDE_EFFECTING),
    )(*[pltpu.with_memory_space_constraint(a, pltpu.HBM) for a in list(srcs) + lands])
    sems = [(res[2 * gi], res[2 * gi + 1]) for gi in range(len(groups))]
    return sems, list(res[n_sem:n_sem + n_t]), list(res[n_sem + n_t:n_sem + 2 * n_t]), res[-1]


def exchange_wait(sems, srcs, lands, gather, after, name):
    n_t = len(srcs)

    def body(*refs):
        src_refs, land_refs = refs[:n_t], refs[n_t:2 * n_t]
        send_sems, recv_sems = refs[2 * n_t], refs[2 * n_t + 1]
        x, y, c = lax.axis_index("x"), lax.axis_index("y"), lax.axis_index("c")
        for k in range(n_t):
            for r in range(1, N_DEV):
                send, recv = _exchange_copy(src_refs[k], land_refs[k], send_sems, recv_sems, k, gather[k], x, y, c, r)
                send.wait_send()
                recv.wait_recv()

    res = pl.pallas_call(
        body, name=name,
        out_shape=tuple(pltpu.HBM(a.shape, a.dtype) for a in list(srcs) + list(lands)),
        in_specs=[_HBM_SPEC] * (2 * n_t) + [_SEM_SPEC, _SEM_SPEC, pl.BlockSpec(memory_space=pl.ANY)],
        out_specs=tuple([_HBM_SPEC] * (2 * n_t)),
        input_output_aliases={i: i for i in range(2 * n_t)},
        compiler_params=pltpu.CompilerParams(has_side_effects=pltpu.SideEffectType.DATAFLOW_SIDE_EFFECTING),
    )(*srcs, *lands, sems[0], sems[1], after)
    return list(res[:n_t]), list(res[n_t:])


def adamw(parts, w, m, v, name):
    n_l, n_r, n_c = w.shape
    tr = n_r
    if n_r % 8 == 0:
        for cand in (512, 256, 128, 64, 32, 16, 8):
            if n_r % cand == 0 and cand * n_c * 4 <= 512 * 1024:
                tr = cand
                break
    c1 = 1.0 - ADAM_B1 ** ADAM_STEP
    c2 = 1.0 - ADAM_B2 ** ADAM_STEP
    per_layer = isinstance(parts, (list, tuple))
    n_p = n_l if per_layer else 1
    n_rb = n_r // tr

    def update(g, w_ref, m_ref, v_ref, g_ref, d_ref, nm_ref, nv_ref):
        w_, m_, v_ = w_ref[0], m_ref[0], v_ref[0]
        nm = ADAM_B1 * m_ + (1.0 - ADAM_B1) * g
        nv = ADAM_B2 * v_ + (1.0 - ADAM_B2) * (g * g)
        g_ref[0] = g
        nm_ref[0] = nm
        nv_ref[0] = nv
        d_ref[0] = -ADAM_LR * ((nm / c1) / (jnp.sqrt(nv / c2) + ADAM_EPS) + ADAM_WD * w_)

    def body(*refs):
        p_refs, rest = refs[:n_p], refs[n_p:]
        if not per_layer:
            g = p_refs[0][0, 0].astype(F32)
            for s in range(1, N_DEV):
                g = g + p_refs[0][s, 0].astype(F32)
            update(g, *rest)
        else:
            for lay in range(n_l):
                @pl.when(pl.program_id(0) == lay)
                def _(lay=lay):
                    g = p_refs[lay][0].astype(F32)
                    for s in range(1, N_DEV):
                        g = g + p_refs[lay][s].astype(F32)
                    update(g, *rest)

    blk = pl.BlockSpec((1, tr, n_c), lambda l, r: (l, r, 0))
    if per_layer:
        def p_spec(lay):
            park = 0 if lay > 0 else n_rb - 1
            return pl.BlockSpec((N_DEV, tr, n_c), lambda l, r: (0, jnp.where(l == lay, r, park), 0))
        p_specs, p_args = [p_spec(lay) for lay in range(n_l)], list(parts)
    else:
        p_specs, p_args = [pl.BlockSpec((N_DEV, 1, tr, n_c), lambda l, r: (0, l, r, 0))], [parts]
    return pl.pallas_call(
        body, grid=(n_l, n_rb), in_specs=p_specs + [blk, blk, blk],
        out_specs=[blk] * 4, out_shape=[jax.ShapeDtypeStruct(w.shape, F32)] * 4,
        compiler_params=pltpu.CompilerParams(dimension_semantics=("arbitrary", "arbitrary"),
                                             vmem_limit_bytes=VMEM_LIMIT),
        name=name)(*p_args, w, m, v)


def adamw_replicated(lands, own, ws, ms, vs, loss_land, loss_own):
    n_t = len(lands)
    c1 = 1.0 - ADAM_B1 ** ADAM_STEP
    c2 = 1.0 - ADAM_B2 ** ADAM_STEP

    def body(*refs):
        ins, outs = refs[:5 * n_t + 2], refs[5 * n_t + 2:]
        me = 4 * lax.axis_index("x") + 2 * lax.axis_index("y") + lax.axis_index("c")

        def total(land_ref, own_ref):
            g = None
            for s in range(N_DEV):
                term = jnp.where(me == s, own_ref[...], land_ref[s])
                g = term if g is None else g + term
            return g

        for t in range(n_t):
            land_ref, own_ref, w_ref, m_ref, v_ref = ins[5 * t:5 * t + 5]
            g = total(land_ref, own_ref)
            nm = ADAM_B1 * m_ref[...] + (1.0 - ADAM_B1) * g
            nv = ADAM_B2 * v_ref[...] + (1.0 - ADAM_B2) * (g * g)
            g_ref, d_ref, nm_ref, nv_ref = outs[4 * t:4 * t + 4]
            g_ref[...] = g
            nm_ref[...] = nm
            nv_ref[...] = nv
            d_ref[...] = -ADAM_LR * ((nm / c1) / (jnp.sqrt(nv / c2) + ADAM_EPS) + ADAM_WD * w_ref[...])
        outs[4 * n_t][...] = total(ins[5 * n_t], ins[5 * n_t + 1])

    args = []
    for t in range(n_t):
        args += [lands[t], own[t], ws[t], ms[t], vs[t]]
    out_shape = []
    for t in range(n_t):
        out_shape += [jax.ShapeDtypeStruct(ws[t].shape, F32)] * 4
    out_shape.append(jax.ShapeDtypeStruct(loss_own.shape, F32))
    res = pl.pallas_call(body, out_shape=out_shape,
                         compiler_params=pltpu.CompilerParams(vmem_limit_bytes=VMEM_LIMIT),
                         name="adamw_replicated")(*args, loss_land, loss_own)
    return [tuple(res[4 * t:4 * t + 4]) for t in range(n_t)], res[-1]


def _shard_slots(g, axis):
    if axis == 1:
        return g.reshape(g.shape[0], N_DEV, g.shape[1] // N_DEV, g.shape[2]).transpose(1, 0, 2, 3)
    return g.reshape(g.shape[0], g.shape[1], N_DEV, g.shape[2] // N_DEV).transpose(2, 0, 1, 3)


def _unshard(slots, axis):
    if axis == 1:
        return slots.transpose(1, 0, 2, 3).reshape(slots.shape[1], -1, slots.shape[3])
    return slots.transpose(1, 2, 0, 3).reshape(slots.shape[1], slots.shape[2], -1)


def kernel(x, ln_in_g, ln_in_b, w_in, conv_a_w, conv_a_b, ln_a_g, ln_a_b, qk_norm_q, qk_norm_k, sgu_ln_g, sgu_ln_b, sgu_w, sgu_b, mla_q_norm, mla_w_uq, mla_kv_norm, mla_w_ukv, w_out, ln_mix_g, ln_mix_b, ffn_w_up, ffn_conv_w, ffn_conv_b, ffn_w_down, ln_ffn_g, ln_ffn_b, loss_target, m_ln_in_g, m_ln_in_b, m_w_in, m_conv_a_w, m_conv_a_b, m_ln_a_g, m_ln_a_b, m_qk_norm_q, m_qk_norm_k, m_sgu_ln_g, m_sgu_ln_b, m_sgu_w, m_sgu_b, m_mla_q_norm, m_mla_w_uq, m_mla_kv_norm, m_mla_w_ukv, m_w_out, m_ln_mix_g, m_ln_mix_b, m_ffn_w_up, m_ffn_conv_w, m_ffn_conv_b, m_ffn_w_down, m_ln_ffn_g, m_ln_ffn_b, v_ln_in_g, v_ln_in_b, v_w_in, v_conv_a_w, v_conv_a_b, v_ln_a_g, v_ln_a_b, v_qk_norm_q, v_qk_norm_k, v_sgu_ln_g, v_sgu_ln_b, v_sgu_w, v_sgu_b, v_mla_q_norm, v_mla_w_uq, v_mla_kv_norm, v_mla_w_ukv, v_w_out, v_ln_mix_g, v_ln_mix_b, v_ffn_w_up, v_ffn_conv_w, v_ffn_conv_b, v_ffn_w_down, v_ln_ffn_g, v_ln_ffn_b):
    local = dict(ln_in_g=ln_in_g, ln_in_b=ln_in_b, w_in=w_in, conv_a_w=conv_a_w, conv_a_b=conv_a_b, ln_a_g=ln_a_g, ln_a_b=ln_a_b, qk_norm_q=qk_norm_q, qk_norm_k=qk_norm_k, sgu_ln_g=sgu_ln_g, sgu_ln_b=sgu_ln_b, sgu_w=sgu_w, sgu_b=sgu_b, mla_q_norm=mla_q_norm, mla_w_uq=mla_w_uq, mla_kv_norm=mla_kv_norm, mla_w_ukv=mla_w_ukv, w_out=w_out, ln_mix_g=ln_mix_g, ln_mix_b=ln_mix_b, ffn_w_up=ffn_w_up, ffn_conv_w=ffn_conv_w, ffn_conv_b=ffn_conv_b, ffn_w_down=ffn_w_down, ln_ffn_g=ln_ffn_g, ln_ffn_b=ln_ffn_b)
    mom = dict(ln_in_g=m_ln_in_g, ln_in_b=m_ln_in_b, w_in=m_w_in, conv_a_w=m_conv_a_w, conv_a_b=m_conv_a_b, ln_a_g=m_ln_a_g, ln_a_b=m_ln_a_b, qk_norm_q=m_qk_norm_q, qk_norm_k=m_qk_norm_k, sgu_ln_g=m_sgu_ln_g, sgu_ln_b=m_sgu_ln_b, sgu_w=m_sgu_w, sgu_b=m_sgu_b, mla_q_norm=m_mla_q_norm, mla_w_uq=m_mla_w_uq, mla_kv_norm=m_mla_kv_norm, mla_w_ukv=m_mla_w_ukv, w_out=m_w_out, ln_mix_g=m_ln_mix_g, ln_mix_b=m_ln_mix_b, ffn_w_up=m_ffn_w_up, ffn_conv_w=m_ffn_conv_w, ffn_conv_b=m_ffn_conv_b, ffn_w_down=m_ffn_w_down, ln_ffn_g=m_ln_ffn_g, ln_ffn_b=m_ln_ffn_b)
    var = dict(ln_in_g=v_ln_in_g, ln_in_b=v_ln_in_b, w_in=v_w_in, conv_a_w=v_conv_a_w, conv_a_b=v_conv_a_b, ln_a_g=v_ln_a_g, ln_a_b=v_ln_a_b, qk_norm_q=v_qk_norm_q, qk_norm_k=v_qk_norm_k, sgu_ln_g=v_sgu_ln_g, sgu_ln_b=v_sgu_ln_b, sgu_w=v_sgu_w, sgu_b=v_sgu_b, mla_q_norm=v_mla_q_norm, mla_w_uq=v_mla_w_uq, mla_kv_norm=v_mla_kv_norm, mla_w_ukv=v_mla_w_ukv, w_out=v_w_out, ln_mix_g=v_ln_mix_g, ln_mix_b=v_ln_mix_b, ffn_w_up=v_ffn_w_up, ffn_conv_w=v_ffn_conv_w, ffn_conv_b=v_ffn_conv_b, ffn_w_down=v_ffn_w_down, ln_ffn_g=v_ln_ffn_g, ln_ffn_b=v_ln_ffn_b)

    me = 4 * lax.axis_index("x") + 2 * lax.axis_index("y") + lax.axis_index("c")

    def own_slot(slots, block):
        return lax.dynamic_update_slice(slots, block[None], (me,) + (0,) * block.ndim)

    small_sharded = [n for n in SHARDED if n not in MATMUL_WEIGHTS]
    big_order = [(n, l) for l in range(DEPTH) for n in MATMUL_WEIGHTS]
    send_view = lambda n, l: (local[n].transpose(0, 2, 1)[l] if n in TRANSPOSED else local[n][l]).astype(COMM_DTYPE)
    srcs = [send_view(*big_order[0])] + [local[n] for n in small_sharded]
    srcs += [send_view(n, l) for (n, l) in big_order[1:]]
    n_first = 1 + len(small_sharded)
    groups = [list(range(n_first))] + [[n_first + j] for j in range(len(big_order) - 1)]
    g_sems, g_srcs, g_lands, tok0 = exchange_start(srcs, [True] * len(srcs), groups, "gather_start")
    tok0 = tok0[0, 0]
    pending = {key: gi for gi, key in enumerate(big_order)}

    def finish(gi, after):
        idx = groups[gi]
        _, lands = exchange_wait(g_sems[gi], [g_srcs[t] for t in idx], [g_lands[t] for t in idx], [True] * len(idx),
                                 after, f"gather_wait{gi}")
        return [own_slot(ld, srcs[t]) for ld, t in zip(lands, idx)]

    first = []

    def get_wts(after):
        first.extend(finish(0, after))
        wts = {n: local[n] for n in REPLICATED}
        for n, slots in zip(small_sharded, first[1:]):
            wts[n] = _unshard(slots, SHARDED[n])
        return wts

    def unshard_layer(slots, n):
        if SHARDED[n] == 1 or n in TRANSPOSED:
            return slots.reshape(-1, slots.shape[2])
        return slots.transpose(1, 0, 2).reshape(slots.shape[1], -1)

    def mat(l, n, after):
        gi = pending[(n, l)]
        slots = first[0] if gi == 0 else finish(gi, after)[0]
        w = unshard_layer(slots, n).astype(MXU_DTYPE)
        return _pad_w_in(w) if n == 'w_in' else w

    started = []

    def hook(key, grads):
        tensors = []
        for (n, l), g in grads.items():
            if l is None:
                tensors.append(((n, l), _shard_slots(g, SHARDED[n])))
            elif n in TRANSPOSED:
                tensors.append(((n, l), g.reshape(N_DEV, g.shape[0] // N_DEV, g.shape[1]).astype(COMM_DTYPE)))
            else:
                tensors.append(((n, l), _shard_slots(g[None], SHARDED[n])[:, 0].astype(COMM_DTYPE)))
        sems, s_srcs, s_lands, tok = exchange_start([a for _, a in tensors], [False] * len(tensors),
                                                    [list(range(len(tensors)))], "scatter_start_" + key)
        started.append((key, [k for k, _ in tensors], sems[0], s_srcs, s_lands))
        return tok[0, 0]

    loss, dx, grads = local_step(x[0], loss_target[0], (local['ln_in_g'] + tok0, local['ln_in_b']), get_wts, mat, hook)

    as2d = lambda a: a.reshape(-1, a.shape[-1]) if a.ndim > 1 else a.reshape(1, -1)
    small_g = [as2d(grads[n]) for n in REPLICATED] + [jnp.broadcast_to(loss, (8, 128))]
    p_sems, p_srcs, p_lands, p_tok = exchange_start(small_g, [True] * len(small_g), [list(range(len(small_g)))],
                                                    "gather_small_start")

    parts, res = {}, {}

    def finish_scatter(entries, after):
        for key, keys, sems, s_srcs, s_lands in entries:
            s_out, lands = exchange_wait(sems, s_srcs, s_lands, [False] * len(keys), after, "scatter_wait_" + key)
            for k, so, ld in zip(keys, s_out, lands):
                parts[k] = own_slot(ld, lax.dynamic_index_in_dim(so, me, 0, keepdims=False))

    def update(names_):
        for n in names_:
            p = [parts[(n, l)] for l in range(DEPTH)] if n in MATMUL_WEIGHTS else parts[(n, None)]
            view = (lambda a: a.transpose(0, 2, 1)) if n in TRANSPOSED else (lambda a: a)
            res[n] = tuple(view(a) for a in adamw(p, view(local[n]), view(mom[n]), view(var[n]), "adamw_" + n))

    early = ('ffn_w_up', 'ffn_w_down', 'w_out')
    finish_scatter([e for e in started if e[0] != "last"], p_tok)
    update(early)
    finish_scatter([e for e in started if e[0] == "last"], res[early[-1]][1])
    update([n for n in SHARDED if n not in early])
    updated = jnp.zeros((8, 128), F32) + sum(res[n][1][0, 0, 0] for n in SHARDED)
    p_own, p_lands = exchange_wait(p_sems[0], p_srcs, p_lands, [True] * len(small_g), updated, "gather_small_wait")
    small, loss_sum = adamw_replicated(p_lands[:-1], p_own[:-1], [as2d(local[n]) for n in REPLICATED],
                                       [as2d(mom[n]) for n in REPLICATED], [as2d(var[n]) for n in REPLICATED],
                                       p_lands[-1], p_own[-1])
    for n, quad in zip(REPLICATED, small):
        res[n] = tuple(a.reshape(local[n].shape) for a in quad)
    loss_total = loss_sum[0, 0]

    return (loss_total, dx[None], *[res[n][0] for n in WEIGHTS], *[res[n][1] for n in WEIGHTS],
            *[res[n][2] for n in WEIGHTS], *[res[n][3] for n in WEIGHTS])
```

```python
import functools
import math

import jax
import jax.numpy as jnp
from jax import lax
from jax.experimental import pallas as pl
from jax.experimental.pallas import tpu as pltpu

F32 = jnp.float32
MXU_DTYPE = jnp.bfloat16
COMM_DTYPE = jnp.bfloat16

N_DEV = 8
D_MODEL = 1024
DEPTH = 2
GRID_W = 64
GROUP_W = 256
HEAD_DIM = 64
CONV_A_WIDTH = 31
CONV_A_HALO = 16
GQA_HEADS = 4
GQA_KV_HEADS = 2
CHUNK = 128
SGU_GROUPS = 4
MLA_HEADS = 4
MLA_Q_LORA = 192
MLA_KV_LORA = 128
MLA_NOPE = 64
MLA_ROPE = 32
MLA_V = 64
MLA_DK_PAD = 128
ROPE_THETA = 10000.0
D_FF = 2816
DEEPNORM_ALPHA = (2 * DEPTH) ** 0.25
LN_EPS = 1e-5
RMS_EPS = 1e-6
D_IN_PROJ = 1888

ADAM_LR = 0.001
ADAM_B1 = 0.9
ADAM_B2 = 0.999
ADAM_EPS = 1e-08
ADAM_WD = 0.01
ADAM_STEP = 10

WEIGHTS = ['ln_in_g', 'ln_in_b', 'w_in', 'conv_a_w', 'conv_a_b', 'ln_a_g', 'ln_a_b', 'qk_norm_q', 'qk_norm_k',
           'sgu_ln_g', 'sgu_ln_b', 'sgu_w', 'sgu_b', 'mla_q_norm', 'mla_w_uq', 'mla_kv_norm', 'mla_w_ukv', 'w_out',
           'ln_mix_g', 'ln_mix_b', 'ffn_w_up', 'ffn_conv_w', 'ffn_conv_b', 'ffn_w_down', 'ln_ffn_g', 'ln_ffn_b']
SHARDED = {'w_in': 2, 'conv_a_w': 2, 'mla_w_uq': 2, 'mla_w_ukv': 2, 'w_out': 1, 'ffn_w_up': 2, 'ffn_conv_w': 2,
           'ffn_w_down': 1}
MATMUL_WEIGHTS = ('w_in', 'w_out', 'ffn_w_up', 'ffn_w_down')
TRANSPOSED = ('w_in', 'ffn_w_up')
REPLICATED = [n for n in WEIGHTS if n not in SHARDED]

ROW_TILE = 256
LN_TILE = 512
VMEM_LIMIT = 56 * 1024 * 1024


def _rawdot(a, b, ca, cb):
    return lax.dot_general(a.astype(MXU_DTYPE), b.astype(MXU_DTYPE), (((ca,), (cb,)), ((), ())),
                           preferred_element_type=F32)


@jax.custom_vjp
def mm_nn(a, b):
    return _rawdot(a, b, 1, 0)


def _mm_nn_fwd(a, b):
    return _rawdot(a, b, 1, 0), (a, b)


def _mm_nn_bwd(res, dy):
    a, b = res
    return _rawdot(dy, b, 1, 1), _rawdot(a, dy, 0, 0)


mm_nn.defvjp(_mm_nn_fwd, _mm_nn_bwd)


@jax.custom_vjp
def mm_nt(a, b):
    return _rawdot(a, b, 1, 1)


def _mm_nt_fwd(a, b):
    return _rawdot(a, b, 1, 1), (a, b)


def _mm_nt_bwd(res, dy):
    a, b = res
    return _rawdot(dy, b, 1, 0), _rawdot(dy, a, 0, 0)


mm_nt.defvjp(_mm_nt_fwd, _mm_nt_bwd)


def _pick_tile(d, cands):
    for c in cands:
        if d % c == 0:
            return c
    return d


def matmul(a, b, mode, out_dtype, name, b_rows=None, into=None, add=(), tm_max=1408):
    a_list = list(a) if isinstance(a, (list, tuple)) else [a]
    b_list = list(b) if isinstance(b, (list, tuple)) else [b]
    n_p = len(a_list)
    rows_list = [b_rows] if not isinstance(a, (list, tuple)) else (list(b_rows) if b_rows is not None else [None] * n_p)
    b_start, b_size = zip(*[(0, bb.shape[0]) if r is None else r for bb, r in zip(b_list, rows_list)])
    a0, b0 = a_list[0], b_list[0]
    if mode == 'nn':
        (m, k), (k2, n) = a0.shape, (b_size[0], b0.shape[1])
    elif mode == 'nt':
        (m, k), (n, k2) = a0.shape, (b_size[0], b0.shape[1])
    else:
        (k, m), (k2, n) = a0.shape, (b_size[0], b0.shape[1])
    assert k == k2 and all(x.shape == a0.shape for x in a_list) and len(set(b_size)) == 1, (a0.shape, b0.shape, mode)
    tm = _pick_tile(m, tuple(c for c in (1024, 1408, 512, 256, 128) if c <= tm_max))
    tn = _pick_tile(n, (512, 1408, 256, 128))
    tk = _pick_tile(k, (2816, 2048, 1024, 512, 256, 128))
    nk = k // tk
    ca = 0 if mode == 'tn' else 1
    cb = 1 if mode == 'nt' else 0
    b_blk = tn if mode == 'nt' else tk
    assert all(s % b_blk == 0 for s in b_start), (b_rows, b_blk)
    a_spec = pl.BlockSpec((tk, tm), lambda i, j, kk: (kk, i)) if mode == 'tn' else pl.BlockSpec((tm, tk), lambda i, j, kk: (i, kk))

    def b_spec(off):
        if mode == 'nt':
            return pl.BlockSpec((tn, tk), lambda i, j, kk: (j + off, kk))
        return pl.BlockSpec((tk, tn), lambda i, j, kk: (kk + off, j))

    in_specs, args, aliases = [], [], {}
    for x, y, s in zip(a_list, b_list, b_start):
        in_specs += [a_spec, b_spec(s // b_blk)]
        args += [x, y]
    in_specs += [pl.BlockSpec((tm, tn), lambda i, j, kk: (i, j))] * len(add)
    args += list(add)
    out_off, out_shape = 0, jax.ShapeDtypeStruct((m, n), out_dtype)
    if into is not None:
        buf, row = into
        assert row % tm == 0 and buf.shape[1] == n and buf.dtype == out_dtype, (buf.shape, row, tm)
        out_off, out_shape = row // tm, jax.ShapeDtypeStruct(buf.shape, buf.dtype)
        aliases = {len(args): 0}
        in_specs, args = in_specs + [pl.BlockSpec(memory_space=pl.ANY)], args + [buf]
    n_add = len(add)

    def body(*refs):
        o_ref, acc_ref = refs[-2:]
        kk = pl.program_id(2)

        @pl.when(kk == 0)
        def _():
            acc_ref[...] = jnp.zeros_like(acc_ref)

        for p in range(n_p):
            acc_ref[...] += _rawdot(refs[2 * p][...], refs[2 * p + 1][...], ca, cb)

        @pl.when(kk == nk - 1)
        def _():
            total = acc_ref[...]
            for r in refs[2 * n_p:2 * n_p + n_add]:
                total = total + r[...]
            o_ref[...] = total.astype(o_ref.dtype)

    return pl.pallas_call(
        body, grid=(m // tm, n // tn, nk), in_specs=in_specs,
        out_specs=pl.BlockSpec((tm, tn), lambda i, j, kk: (i + out_off, j)),
        out_shape=out_shape, input_output_aliases=aliases,
        scratch_shapes=[pltpu.VMEM((tm, tn), F32)],
        compiler_params=pltpu.CompilerParams(dimension_semantics=("parallel", "parallel", "arbitrary"),
                                             vmem_limit_bytes=VMEM_LIMIT),
        name=name)(*args)


class Op:
    def __init__(self, arr, block, imap, grad=False, acc=False, first=None, gdtype=F32, gshape=None, gimap=None):
        self.arr, self.block, self.imap = arr, block, imap
        self.grad, self.acc, self.first, self.gdtype = grad, acc, first, gdtype
        self.gshape = arr.shape if gshape is None else gshape
        self.gimap = imap if gimap is None else gimap


def _row_op(arr, tm, grad=False, gdtype=F32):
    return Op(arr, (tm, arr.shape[1]), lambda i: (i, 0), grad=grad, gdtype=gdtype)


def _par_op(arr, grad=False):
    nd = arr.ndim
    return Op(arr, arr.shape, lambda i: (0,) * nd, grad=grad, acc=True, first=lambda ids: ids[0] == 0)


def _load(ref):
    v = ref[...]
    return v.astype(F32) if jnp.issubdtype(v.dtype, jnp.floating) else v


def _store_heads(ref, val):
    rows = val.shape[0]
    if len(ref.shape) == 2:
        ref[...] = val.astype(ref.dtype)
    elif ref.shape[1] == rows:
        d = ref.shape[2]
        for h in range(ref.shape[0]):
            ref[h] = val[:, d * h:d * (h + 1)].astype(ref.dtype)
    else:
        d = ref.shape[1]
        assert ref.shape[2] == rows and d != rows, (ref.shape, val.shape)
        for h in range(ref.shape[0]):
            ref[h] = val[:, d * h:d * (h + 1)].T.astype(ref.dtype)


def _load_heads(ref, transposed=False):
    if len(ref.shape) == 2:
        return ref[...].astype(F32)
    parts = [ref[h].astype(F32) for h in range(ref.shape[0])]
    return jnp.concatenate([p.T for p in parts] if transposed else parts, axis=-1)


def stage_fwd(name, fn, ops, outs, grid):
    n_in = len(ops)

    def body(*refs):
        res = fn(*[_load(r) for r in refs[:n_in]])
        for r, o in zip(refs[n_in:], res):
            _store_heads(r, o)

    return pl.pallas_call(
        body, grid=grid, in_specs=[pl.BlockSpec(o.block, o.imap) for o in ops],
        out_specs=[pl.BlockSpec(b, im) for (_, _, b, im) in outs],
        out_shape=[jax.ShapeDtypeStruct(s, d) for (s, d, _, _) in outs],
        compiler_params=pltpu.CompilerParams(dimension_semantics=("parallel",) * len(grid),
                                             vmem_limit_bytes=VMEM_LIMIT),
        name=name)(*[o.arr for o in ops])


def stage_bwd(name, fn, ops, cts, grid, value_acc=False):
    n_in = len(ops)
    ct_flat = [(c + (False,))[:4] for group in cts if group is not None for c in group]
    n_ct = len(ct_flat)
    diff = [i for i, o in enumerate(ops) if o.grad]
    any_acc = value_acc or any(ops[i].acc for i in diff)
    ngrid = len(grid)

    def body(*refs):
        ids = [pl.program_id(a) for a in range(ngrid)]
        vals = [_load(r) for r in refs[:n_in]]
        ct_refs = refs[n_in:n_in + n_ct]
        out_refs = refs[n_in + n_ct:]

        def f(*dv):
            full = list(vals)
            for i, v in zip(diff, dv):
                full[i] = v
            return tuple(fn(*full))

        res, vjp = jax.vjp(f, *[vals[i] for i in diff])
        ct, pos = [], 0
        for group, r in zip(cts, res):
            if group is None:
                ct.append(jnp.ones_like(r))
            else:
                tot = None
                for _ in group:
                    c = _load_heads(ct_refs[pos], ct_flat[pos][3])
                    tot = c if tot is None else tot + c
                    pos += 1
                ct.append(tot)
        grads = vjp(tuple(ct))
        for i, g, r in zip(diff, grads, out_refs):
            if ops[i].acc:
                @pl.when(ops[i].first(ids))
                def _(r=r):
                    r[...] = jnp.zeros_like(r)

                r[...] += g.astype(r.dtype)
            else:
                r[...] = g.astype(r.dtype)
        if value_acc:
            r = out_refs[len(diff)]

            @pl.when(ids[0] == 0)
            def _():
                r[...] = jnp.zeros_like(r)

            r[...] += res[0]

    in_specs = [pl.BlockSpec(o.block, o.imap) for o in ops] + [pl.BlockSpec(b, im) for (_, b, im, _) in ct_flat]
    out_specs = [pl.BlockSpec(ops[i].block, ops[i].gimap) for i in diff]
    out_shape = [jax.ShapeDtypeStruct(ops[i].gshape, ops[i].gdtype) for i in diff]
    if value_acc:
        out_specs.append(pl.BlockSpec((1, 1), lambda *ids: (0, 0)))
        out_shape.append(jax.ShapeDtypeStruct((1, 1), F32))
    sem = ("arbitrary",) * ngrid if any_acc else ("parallel",) * ngrid
    return pl.pallas_call(
        body, grid=grid, in_specs=in_specs, out_specs=out_specs, out_shape=out_shape,
        compiler_params=pltpu.CompilerParams(dimension_semantics=sem, vmem_limit_bytes=VMEM_LIMIT),
        name=name)(*[o.arr for o in ops], *[a for (a, _, _, _) in ct_flat])


def _sigmoid(x):
    return 1.0 / (1.0 + jnp.exp(-x))


def _silu(x):
    return x * _sigmoid(x)


def _gelu_tanh(x):
    return 0.5 * x * (1.0 + jnp.tanh(math.sqrt(2.0 / math.pi) * (x + 0.044715 * (x * x * x))))


def _ln(x, g, b):
    mu = jnp.mean(x, axis=-1, keepdims=True)
    xc = x - mu
    var = jnp.mean(xc * xc, axis=-1, keepdims=True)
    return xc * lax.rsqrt(var + LN_EPS) * g + b


def _rms(x, g):
    ms = jnp.mean(x * x, axis=-1, keepdims=True)
    return x * lax.rsqrt(ms + RMS_EPS) * g


def _swap_halves(x, half):
    width = x.shape[-1]
    lane = lax.broadcasted_iota(jnp.int32, x.shape, 1)
    return jnp.where(lane % (2 * half) < half, pltpu.roll(x, width - half, 1), pltpu.roll(x, half, 1))


def _make_swap(half):
    @jax.custom_vjp
    def swap(x):
        return _swap_halves(x, half)

    swap.defvjp(lambda x: (_swap_halves(x, half), None), lambda _, dy: (_swap_halves(dy, half),))
    return swap


_swap16, _swap8 = _make_swap(16), _make_swap(8)


def _rope(x, cos, sin_signed, swap):
    return x * cos + swap(x) * sin_signed


def _dot_f32(a, b):
    return jnp.dot(a, b, preferred_element_type=F32, precision=lax.Precision.HIGHEST)


def fn_ln(x, g, b):
    return (_ln(x, g, b),)


def _twice(fn):
    def f(*a):
        (y,) = fn(*a)
        return y, y
    return f


PROJ_W = 2048
P_A, P_Q, P_K, P_V, P_C, P_CQ, P_CKV, P_KR = 0, 512, 768, 896, 1024, 1536, 1792, 1920
CQ_PAD = 256
_CQ_END = P_CQ + MLA_Q_LORA


def _pad_w_in(wt):
    z = lambda n: jnp.zeros((n, wt.shape[1]), wt.dtype)
    return jnp.concatenate([wt[:_CQ_END], z(P_CKV - _CQ_END), wt[_CQ_END:], z(PROJ_W - P_KR - MLA_ROPE)], axis=0)


def _unpad_w_in(gt):
    return jnp.concatenate([gt[:_CQ_END], gt[P_CKV:P_KR + MLA_ROPE]], axis=0)


def fn_pre(proj, tab_q, tab_d, seg, place, qng, kng, sg, sb, sw, sbt, mqn, wuq, mkvn, wukv):
    tm = proj.shape[0]
    aglu = proj[:, P_A:P_A + GROUP_W] * _sigmoid(proj[:, P_A + GROUP_W:P_Q])
    b_q, b_k, b_v = proj[:, P_Q:P_K], proj[:, P_K:P_V], proj[:, P_V:P_C]
    cos_q, sin_q = tab_q[:, :GROUP_W], tab_q[:, GROUP_W:]
    q = b_q * lax.rsqrt(_dot_f32(b_q * b_q, seg) + RMS_EPS) * qng
    q = _rope(q, cos_q, sin_q, _swap16)
    k = b_k * lax.rsqrt(_dot_f32(b_k * b_k, seg[:128, :128]) + RMS_EPS) * kng
    k = _rope(k, cos_q[:, :128], sin_q[:, :128], _swap16)
    c = _gelu_tanh(proj[:, P_C:P_CQ])
    u, sv = c[:, :GROUP_W], _ln(c[:, GROUP_W:], sg, sb)
    group = lax.broadcasted_iota(jnp.int32, (CHUNK, GROUP_W), 1) // HEAD_DIM
    rows = []
    for n in range(tm // CHUNK):
        svn = sv[CHUNK * n:CHUNK * (n + 1)]
        acc = jnp.zeros((CHUNK, GROUP_W), F32)
        for g in range(SGU_GROUPS):
            acc = acc + jnp.where(group == g, mm_nn(sw[CHUNK * g:CHUNK * (g + 1)], svn) + sbt[:, g:g + 1], 0.0)
        rows.append(acc)
    o_c = u * jnp.concatenate(rows, axis=0)
    d_cq, d_ckv, d_kr = proj[:, P_CQ:P_CKV], proj[:, P_CKV:P_KR], proj[:, P_KR:PROJ_W]
    cqn = d_cq * lax.rsqrt(jnp.sum(d_cq * d_cq, axis=-1, keepdims=True) * (1.0 / MLA_Q_LORA) + RMS_EPS) * mqn
    cos_d = jnp.concatenate([tab_d[:, :MLA_DK_PAD]] * MLA_HEADS, axis=-1)
    sin_d = jnp.concatenate([tab_d[:, MLA_DK_PAD:]] * MLA_HEADS, axis=-1)
    qf = _rope(mm_nn(cqn, wuq), cos_d, sin_d, _swap8)
    kvd = mm_nn(_rms(d_ckv, mkvn), wukv)
    kf = _rope(kvd[:, :MLA_HEADS * MLA_DK_PAD] + _dot_f32(d_kr, place), cos_d, sin_d, _swap8)
    return aglu, q, k, b_v, o_c, qf, kf, kvd[:, MLA_HEADS * MLA_DK_PAD:]


def fn_aconv(win, w, b, g, beta):
    tm = win.shape[0] - 2 * CONV_A_HALO
    off = CONV_A_HALO - CONV_A_WIDTH // 2
    acc = None
    for r in range(8):
        rolled = win if r == 0 else _roll_rows(win, -r)
        for kk in range(CONV_A_WIDTH):
            if (off + kk) % 8 == r:
                base = off + kk - r
                term = rolled[base:base + tm] * w[kk:kk + 1, :]
                acc = term if acc is None else acc + term
    return (_silu(_ln(acc + b, g, beta)),)


def fn_resln(h, r, g, b):
    return (_ln(DEEPNORM_ALPHA * h + r, g, b),)


@functools.partial(jax.custom_vjp, nondiff_argnums=(1,))
def _roll_rows(x, shift):
    return pltpu.roll(x, shift % x.shape[0], 0)


_roll_rows.defvjp(lambda x, shift: (pltpu.roll(x, shift % x.shape[0], 0), None),
                  lambda shift, _, dy: (pltpu.roll(dy, (-shift) % dy.shape[0], 0),))


def _shift_down(x):
    row = lax.broadcasted_iota(jnp.int32, x.shape, 0)
    return jnp.where(row == 0, 0.0, _roll_rows(x, 1))


def _shift_up(x):
    row = lax.broadcasted_iota(jnp.int32, x.shape, 0)
    return jnp.where(row == x.shape[0] - 1, 0.0, _roll_rows(x, -1))


def fn_ffnconv(u1, u2, w1, w2, b1, b2):
    c1 = _shift_down(u1) * w1[0:1] + u1 * w1[1:2] + _shift_up(u1) * w1[2:3] + b1
    c2 = _shift_down(u2) * w2[0:1] + u2 * w2[1:2] + _shift_up(u2) * w2[2:3] + b2
    return (_silu(c1) * c2,)


def fn_final(h, r, t, g, b):
    y = _ln(DEEPNORM_ALPHA * h + r, g, b)
    err = (y - t) * (y - t)
    return (0.5 * jnp.sum(jnp.mean(err, axis=-1, keepdims=True), axis=0, keepdims=True),)


def _rope_tables(seq):
    n_rows = seq // GRID_W
    lane128 = jnp.arange(128)

    def tile_tables(j, rotated, half):
        inv = ROPE_THETA ** (-(j % half).astype(F32) / half)
        by_row, by_col = rotated & (j < 2 * half), rotated & (j >= 2 * half)
        sign = jnp.where(j % (2 * half) < half, -1.0, 1.0)
        ar = jnp.arange(n_rows, dtype=F32)[:, None] * inv[None, :]
        ac = jnp.arange(GRID_W, dtype=F32)[:, None] * inv[None, :]
        grid = lambda r, c: (jnp.where(by_row, r, 0.0)[:, None, :] + jnp.where(by_col, c, 0.0)[None, :, :])
        cos = grid(jnp.cos(ar), jnp.cos(ac)) + jnp.where(rotated, 0.0, 1.0)
        sin = grid(sign * jnp.sin(ar), sign * jnp.sin(ac))
        return cos.reshape(seq, 128), sin.reshape(seq, 128)

    cos_b, sin_b = tile_tables(lane128 % HEAD_DIM, lane128 >= 0, HEAD_DIM // 4)
    tab_q = jnp.concatenate([cos_b] * (GROUP_W // 128) + [sin_b] * (GROUP_W // 128), axis=-1)
    tab_d = jnp.concatenate(tile_tables(lane128 - MLA_NOPE, (lane128 >= MLA_NOPE) & (lane128 < MLA_NOPE + MLA_ROPE),
                                        MLA_ROPE // 4), axis=-1)
    lane = jnp.arange(GROUP_W)
    seg = jnp.where(lane[:, None] // HEAD_DIM == lane[None, :] // HEAD_DIM, 1.0 / HEAD_DIM, 0.0).astype(F32)
    src, dst = jnp.arange(128)[:, None], jnp.arange(MLA_HEADS * MLA_DK_PAD)[None, :]
    place = jnp.where((src < MLA_ROPE) & (dst % MLA_DK_PAD == MLA_NOPE + src), 1.0, 0.0).astype(F32)
    return tab_q, tab_d, seg, place


def _pre_ops(proj, tabs, kp, grad):
    tm = ROW_TILE
    ops = [_row_op(proj, tm, grad=grad, gdtype=MXU_DTYPE), _row_op(tabs[0], tm), _row_op(tabs[1], tm),
           _par_op(tabs[2]), _par_op(tabs[3])]
    ops += [_par_op(kp[n], grad=grad) for n in ('qng', 'kng', 'sg', 'sb', 'sw', 'sbt', 'mqn', 'wuq', 'mkvn', 'wukv')]
    return ops


PRE_OUTS = ((0, GROUP_W), (GQA_HEADS, HEAD_DIM), (GQA_KV_HEADS, HEAD_DIM), (GQA_KV_HEADS, HEAD_DIM), (0, GROUP_W),
            (MLA_HEADS, MLA_DK_PAD), (MLA_HEADS, MLA_DK_PAD), (MLA_HEADS, MLA_V))


def _pre_out_specs(seq, tm):
    specs = []
    for heads, w in PRE_OUTS:
        if heads:
            specs.append(((heads, seq, w), (heads, tm, w), lambda i: (0, i, 0)))
        else:
            specs.append(((seq, w), (tm, w), lambda i: (i, 0)))
    return specs


PRE_KV = (2, 3, 6, 7)


def _transposed_spec(heads, w, seq, tm):
    return (heads, w, seq), (heads, w, tm), lambda i: (0, 0, i)


def pre_fwd(proj, tabs, kp, tag):
    seq = proj.shape[0]
    tm = ROW_TILE
    dts = (F32,) + (MXU_DTYPE,) * 7
    outs = [(shape, dt, block, imap) for (shape, block, imap), dt in zip(_pre_out_specs(seq, tm), dts)]
    return stage_fwd("pre_fwd" + tag, fn_pre, _pre_ops(proj, tabs, kp, False), outs, (seq // tm,))


def pre_bwd(proj, tabs, kp, cts, tag):
    seq = proj.shape[0]
    tm = ROW_TILE
    ct = []
    for j, (c, (_, block, imap)) in enumerate(zip(cts, _pre_out_specs(seq, tm))):
        if isinstance(c, tuple):
            ct.append([c])
        elif j in PRE_KV:
            ct.append([(c,) + _transposed_spec(*PRE_OUTS[j], seq, tm)[1:] + (True,)])
        else:
            ct.append([(c, block, imap)])
    return stage_bwd("pre_bwd" + tag, fn_pre, _pre_ops(proj, tabs, kp, True), ct, (seq // tm,))


def _aconv_ops(kp, grad):
    return [_par_op(kp[n], grad=grad) for n in ('caw', 'cab', 'lag', 'lab')]


def aconv_fwd(aglu_pad, kp, tag):
    seq = aglu_pad.shape[0] - 2 * CONV_A_HALO
    tm = ROW_TILE
    n_par = 4

    def body(x_ref, *refs):
        i = pl.program_id(0)
        win = x_ref[pl.ds(pl.multiple_of(i * tm, tm), tm + 2 * CONV_A_HALO), :]
        (o,) = fn_aconv(win, *[_load(r) for r in refs[:n_par]])
        refs[n_par][...] = o.astype(refs[n_par].dtype)

    pars = _aconv_ops(kp, False)
    return pl.pallas_call(
        body, grid=(seq // tm,),
        in_specs=[pl.BlockSpec(aglu_pad.shape, lambda i: (0, 0))] + [pl.BlockSpec(o.block, o.imap) for o in pars],
        out_specs=pl.BlockSpec((tm, GROUP_W), lambda i: (i, 0)),
        out_shape=jax.ShapeDtypeStruct((seq, GROUP_W), MXU_DTYPE),
        compiler_params=pltpu.CompilerParams(dimension_semantics=("parallel",), vmem_limit_bytes=VMEM_LIMIT),
        name="aconv_fwd" + tag)(aglu_pad, *[o.arr for o in pars])


def aconv_bwd(aglu_pad, kp, d_oa, tag):
    seq = aglu_pad.shape[0] - 2 * CONV_A_HALO
    tm = ROW_TILE
    n_par = 4

    def body(x_ref, *refs):
        i = pl.program_id(0)
        rows = pl.ds(pl.multiple_of(i * tm, tm), tm + 2 * CONV_A_HALO)
        pars = [_load(r) for r in refs[:n_par]]
        ct = refs[n_par][...].astype(F32)
        outs = refs[n_par + 1:]
        _, vjp = jax.vjp(lambda *a: fn_aconv(*a), x_ref[rows, :], *pars)
        grads = vjp((ct,))

        @pl.when(i == 0)
        def _():
            for r in outs:
                r[...] = jnp.zeros_like(r)

        outs[0][rows, :] += grads[0]
        for r, g in zip(outs[1:], grads[1:]):
            r[...] += g

    pars = _aconv_ops(kp, True)
    whole = pl.BlockSpec(aglu_pad.shape, lambda i: (0, 0))
    par_specs = [pl.BlockSpec(o.block, o.imap) for o in pars]
    return pl.pallas_call(
        body, grid=(seq // tm,),
        in_specs=[whole] + par_specs + [pl.BlockSpec((tm, GROUP_W), lambda i: (i, 0))],
        out_specs=[whole] + par_specs,
        out_shape=[jax.ShapeDtypeStruct(aglu_pad.shape, F32)] + [jax.ShapeDtypeStruct(o.arr.shape, F32) for o in pars],
        compiler_params=pltpu.CompilerParams(dimension_semantics=("arbitrary",), vmem_limit_bytes=VMEM_LIMIT),
        name="aconv_bwd" + tag)(aglu_pad, *[o.arr for o in pars], d_oa)


ATTN_TQ_FWD = 512
ATTN_TQ = 256
ATTN_TK = 512


def attn_fwd(q3, k3, v3t, scale, tag):
    heads, seq, dk = q3.shape
    group = heads // k3.shape[0]
    kv_per_pair = 2 // group
    dv = v3t.shape[1]
    tq, tk = min(ATTN_TQ_FWD, seq), min(ATTN_TK, seq)
    n_chunks = seq // tk
    log2e = math.log2(math.e)

    def one_head(q, k_ref, vt_ref):
        scores = lambda c: _rawdot(k_ref[pl.ds(c * tk, tk), :], q, 1, 1)
        m, l, acc = jnp.full((1, tq), -jnp.inf, F32), jnp.zeros((1, tq), F32), jnp.zeros((dv, tq), F32)
        s_next = scores(0)
        for c in range(n_chunks):
            s_cur, s_next = s_next, (scores(c + 1) if c + 1 < n_chunks else None)
            t = s_cur * (scale * log2e)
            m_new = jnp.maximum(m, jnp.max(t, axis=0, keepdims=True))
            alpha = jnp.exp2(m - m_new)
            p = jnp.exp2(t - m_new)
            l = alpha * l + jnp.sum(p, axis=0, keepdims=True)
            acc = alpha * acc + _rawdot(vt_ref[:, c * tk:(c + 1) * tk], p, 1, 0)
            m = m_new
        return (acc * (1.0 / l)).T, (m * (1.0 / log2e) + jnp.log(l)).T

    def body(q_ref, k_ref, v_ref, o_ref, lse_ref):
        outs = []
        for h in range(2):
            o, lse = one_head(q_ref[h], k_ref.at[h // group], v_ref.at[h // group])
            lse_ref[h] = lse
            outs.append(o)
        o_ref[...] = jnp.concatenate(outs, axis=-1)

    return pl.pallas_call(
        body, grid=(heads // 2, seq // tq),
        in_specs=[pl.BlockSpec((2, tq, dk), lambda j, i: (j, i, 0)),
                  pl.BlockSpec((kv_per_pair, seq, dk), lambda j, i: (j, 0, 0)),
                  pl.BlockSpec((kv_per_pair, dv, seq), lambda j, i: (j, 0, 0))],
        out_specs=[pl.BlockSpec((tq, 2 * dv), lambda j, i: (i, j)),
                   pl.BlockSpec((2, tq, 1), lambda j, i: (j, i, 0))],
        out_shape=[jax.ShapeDtypeStruct((seq, heads * dv), F32), jax.ShapeDtypeStruct((heads, seq, 1), F32)],
        compiler_params=pltpu.CompilerParams(dimension_semantics=("parallel", "parallel"),
                                             vmem_limit_bytes=VMEM_LIMIT),
        name="attn_fwd" + tag)(q3, k3, v3t)


def attn_bwd(q3, k3, v3, o, lse3, do_all, do_col, scale, tag):
    heads, seq, dk = q3.shape
    group = heads // k3.shape[0]
    kv_per_pair = 2 // group
    dv = v3.shape[2]
    tq, tk = min(ATTN_TQ, seq), min(ATTN_TK, seq)
    n_chunks = seq // tk
    log2e = math.log2(math.e)

    def one_head(q, do, o_h, lse, k_ref, v_ref, dk_ref, dv_ref):
        dob = do.astype(MXU_DTYPE)
        do_t, q_t = do.T.astype(MXU_DTYPE), q.astype(F32).T.astype(MXU_DTYPE)
        delta = jnp.sum(do * o_h, axis=-1, keepdims=True)
        lse2 = lse * log2e
        rows = lambda c: pl.ds(c * tk, tk)
        products = lambda c: (_rawdot(q, k_ref[rows(c), :], 1, 1), _rawdot(dob, v_ref[rows(c), :], 1, 1))
        dq = jnp.zeros((tq, dk), F32)
        nxt = products(0)
        for c in range(n_chunks):
            (s_cur, dp_cur), nxt = nxt, (products(c + 1) if c + 1 < n_chunks else None)
            p = jnp.exp2(s_cur * (scale * log2e) - lse2)
            ds = (p * ((dp_cur - delta) * scale)).astype(MXU_DTYPE)
            dv_ref[:, c * tk:(c + 1) * tk] += _rawdot(do_t, p, 1, 0)
            dk_ref[:, c * tk:(c + 1) * tk] += _rawdot(q_t, ds, 1, 0)
            dq = dq + _rawdot(ds, k_ref[rows(c), :], 1, 0)
        return dq

    def body(q_ref, k_ref, v_ref, o_ref, lse_ref, do_ref, dq_ref, dk_ref, dv_ref):
        @pl.when(pl.program_id(1) == 0)
        def _():
            dk_ref[...] = jnp.zeros_like(dk_ref)
            dv_ref[...] = jnp.zeros_like(dv_ref)

        do_pair, o_pair = do_ref[...], o_ref[...]
        for h in range(2):
            kv = h // group
            dq_ref[h] = one_head(q_ref[h], do_pair[:, dv * h:dv * (h + 1)], o_pair[:, dv * h:dv * (h + 1)],
                                 lse_ref[h], k_ref.at[kv], v_ref.at[kv], dk_ref.at[kv], dv_ref.at[kv])

    qspec = lambda d: pl.BlockSpec((2, tq, d), lambda j, i: (j, i, 0))
    kvspec = lambda d: pl.BlockSpec((kv_per_pair, seq, d), lambda j, i: (j, 0, 0))
    kvt_spec = lambda d: pl.BlockSpec((kv_per_pair, d, seq), lambda j, i: (j, 0, 0))
    kvt_shape = lambda a: jax.ShapeDtypeStruct((a.shape[0], a.shape[2], a.shape[1]), F32)
    return pl.pallas_call(
        body, grid=(heads // 2, seq // tq),
        in_specs=[qspec(dk), kvspec(dk), kvspec(dv), pl.BlockSpec((tq, 2 * dv), lambda j, i: (i, j)), qspec(1),
                  pl.BlockSpec((tq, 2 * dv), lambda j, i: (i, do_col + j))],
        out_specs=[qspec(dk), kvt_spec(dk), kvt_spec(dv)],
        out_shape=[jax.ShapeDtypeStruct(q3.shape, F32), kvt_shape(k3), kvt_shape(v3)],
        compiler_params=pltpu.CompilerParams(dimension_semantics=("parallel", "arbitrary"),
                                             vmem_limit_bytes=VMEM_LIMIT),
        name="attn_bwd" + tag)(q3, k3, v3, o, lse3, do_all)


def resln_fwd(h, r, g, b, tag):
    seq, d = h.shape
    tm = min(LN_TILE, seq)
    ops = [_row_op(h, tm), _row_op(r, tm), _par_op(g), _par_op(b)]
    outs = [((seq, d), dt, (tm, d), lambda i: (i, 0)) for dt in (F32, MXU_DTYPE)]
    return stage_fwd("resln_fwd" + tag, _twice(fn_resln), ops, outs, (seq // tm,))


def resln_bwd(h, r, g, b, dys, tag):
    seq, d = h.shape
    tm = min(LN_TILE, seq)
    ops = [_row_op(h, tm, grad=True), _row_op(r, tm, grad=True, gdtype=MXU_DTYPE), _par_op(g, grad=True),
           _par_op(b, grad=True)]
    ct = [[(dy, (tm, d), lambda i: (i, 0)) for dy in dys]]
    return stage_bwd("resln_bwd" + tag, fn_resln, ops, ct, (seq // tm,))


def _ffnconv_ops(up1, up2, w, b, grad):
    seq = up1.shape[0]
    nblk = D_FF // 128
    lo, hi = (lambda j: (0, j)), (lambda j: (0, j + nblk))
    half = lambda a: dict(gshape=(a.shape[0], D_FF), gimap=lo)
    return [Op(up1, (seq, 128), lo, grad=grad, gdtype=MXU_DTYPE), Op(up2, (seq, 128), lo, grad=grad, gdtype=MXU_DTYPE),
            Op(w, (3, 128), lo, grad=grad, **half(w)), Op(w, (3, 128), hi, grad=grad, **half(w)),
            Op(b, (1, 128), lo, grad=grad, **half(b)), Op(b, (1, 128), hi, grad=grad, **half(b))]


def ffnconv_fwd(up1, up2, w, b, tag):
    seq = up1.shape[0]
    outs = [((seq, D_FF), MXU_DTYPE, (seq, 128), lambda j: (0, j))]
    return stage_fwd("ffnconv_fwd" + tag, fn_ffnconv, _ffnconv_ops(up1, up2, w, b, False), outs, (D_FF // 128,))[0]


def ffnconv_bwd(up1, up2, w, b, dact, tag):
    seq = up1.shape[0]
    ct = [[(dact, (seq, 128), lambda j: (0, j))]]
    du1, du2, dw1, dw2, db1, db2 = stage_bwd("ffnconv_bwd" + tag, fn_ffnconv, _ffnconv_ops(up1, up2, w, b, True),
                                             ct, (D_FF // 128,))
    cat = lambda a, b_: jnp.concatenate([a, b_], axis=-1)
    return du1, du2, cat(dw1, dw2), cat(db1, db2)


def final_bwd(h, r, t, g, b, tag):
    seq, d = h.shape
    tm = min(LN_TILE, seq)
    ops = [_row_op(h, tm, grad=True), _row_op(r, tm, grad=True, gdtype=MXU_DTYPE), _row_op(t, tm),
           _par_op(g, grad=True), _par_op(b, grad=True)]
    return stage_bwd("final_bwd" + tag, fn_final, ops, [None], (seq // tm,), value_acc=True)


def _layer_params(wts, l):
    row = lambda a: a.reshape(1, -1)
    wuq = wts['mla_w_uq'][l].reshape(MLA_Q_LORA, MLA_HEADS, MLA_NOPE + MLA_ROPE)
    wuq = jnp.pad(wuq, ((0, CQ_PAD - MLA_Q_LORA), (0, 0), (0, MLA_DK_PAD - MLA_NOPE - MLA_ROPE)))
    wukv = wts['mla_w_ukv'][l].reshape(MLA_KV_LORA, MLA_HEADS, MLA_NOPE + MLA_V)
    wuk = jnp.pad(wukv[:, :, :MLA_NOPE], ((0, 0), (0, 0), (0, MLA_DK_PAD - MLA_NOPE)))
    return dict(
        qng=jnp.tile(row(wts['qk_norm_q'][l]), (1, GQA_HEADS)), kng=jnp.tile(row(wts['qk_norm_k'][l]), (1, GQA_KV_HEADS)),
        sg=row(wts['sgu_ln_g'][l]), sb=row(wts['sgu_ln_b'][l]),
        sw=wts['sgu_w'][l].reshape(SGU_GROUPS * CHUNK, CHUNK), sbt=wts['sgu_b'][l].T,
        mqn=jnp.pad(row(wts['mla_q_norm'][l]), ((0, 0), (0, CQ_PAD - MLA_Q_LORA))),
        wuq=wuq.reshape(CQ_PAD, MLA_HEADS * MLA_DK_PAD), mkvn=row(wts['mla_kv_norm'][l]),
        wukv=jnp.concatenate([wuk.reshape(MLA_KV_LORA, -1), wukv[:, :, MLA_NOPE:].reshape(MLA_KV_LORA, -1)], axis=1),
        caw=wts['conv_a_w'][l], cab=row(wts['conv_a_b'][l]), lag=row(wts['ln_a_g'][l]), lab=row(wts['ln_a_b'][l]),
        lmg=row(wts['ln_mix_g'][l]), lmb=row(wts['ln_mix_b'][l]),
        fcw=wts['ffn_conv_w'][l], fcb=row(wts['ffn_conv_b'][l]),
        lfg=row(wts['ln_ffn_g'][l]), lfb=row(wts['ln_ffn_b'][l]))


def _to_heads(a, heads):
    seq = a.shape[0]
    return a.reshape(seq, heads, -1).transpose(1, 0, 2)


def _from_heads(a3):
    return a3.transpose(1, 0, 2).reshape(a3.shape[1], -1)


def local_step(x, target, ln_in, get_wts, mat, hook):
    seq = x.shape[0]
    tm = min(LN_TILE, seq)
    tabs = _rope_tables(seq)
    scale_b = HEAD_DIM ** -0.5
    scale_d = (MLA_NOPE + MLA_ROPE) ** -0.5
    ln_in_g, ln_in_b = ln_in[0].reshape(1, -1), ln_in[1].reshape(1, -1)

    h, h_m = stage_fwd("ln_in_fwd", _twice(fn_ln), [_row_op(x, tm), _par_op(ln_in_g), _par_op(ln_in_b)],
                       [((seq, D_MODEL), dt, (tm, D_MODEL), lambda i: (i, 0)) for dt in (F32, MXU_DTYPE)],
                       (seq // tm,))
    wts = get_wts(h_m[:8, :128].astype(F32) + tabs[0][:8, :128] + tabs[1][:8, :128])
    saved = []
    for l in range(DEPTH):
        tag = f"_l{l}"
        kp, unprep = jax.vjp(lambda w: _layer_params(w, l), wts)
        m = {'w_in': mat(l, 'w_in', h_m)}
        proj = matmul(h_m, m['w_in'], 'nt', F32, "mm_proj" + tag)
        aglu, q3, k3, v3, o_c, qd3, kd3, vd3 = pre_fwd(proj, tabs, kp, tag)
        aglu_pad = jnp.pad(aglu, ((CONV_A_HALO, CONV_A_HALO), (0, 0)))
        o_a = aconv_fwd(aglu_pad, kp, tag)
        swap = lambda a: a.transpose(0, 2, 1)
        o_b3, lse_b3 = attn_fwd(q3, k3, swap(v3), scale_b, "_b" + tag)
        o_d3, lse_d3 = attn_fwd(qd3, kd3, swap(vd3), scale_d, "_d" + tag)
        o_cat = jnp.concatenate([o_a, o_b3.astype(MXU_DTYPE), o_c, o_d3.astype(MXU_DTYPE)], axis=-1)
        m['w_out'] = mat(l, 'w_out', o_cat)
        mix = matmul(o_cat, m['w_out'], 'nn', F32, "mm_mix" + tag)
        h1, h1_m = resln_fwd(h, mix, kp['lmg'], kp['lmb'], "_mix" + tag)
        m['ffn_w_up'] = mat(l, 'ffn_w_up', h1_m)
        up1 = matmul(h1_m, m['ffn_w_up'], 'nt', F32, "mm_up1" + tag, b_rows=(0, D_FF))
        up2 = matmul(h1_m, m['ffn_w_up'], 'nt', F32, "mm_up2" + tag, b_rows=(D_FF, D_FF))
        act = ffnconv_fwd(up1, up2, kp['fcw'], kp['fcb'], tag)
        m['ffn_w_down'] = mat(l, 'ffn_w_down', act)
        f = matmul(act, m['ffn_w_down'], 'nn', F32, "mm_down" + tag)
        saved.append(dict(kp=kp, unprep=unprep, m=m, h=h, h_m=h_m, h1_m=h1_m, proj=proj, o_b3=o_b3, lse_b3=lse_b3,
                          o_d3=o_d3, lse_d3=lse_d3, aglu_pad=aglu_pad, q3=q3, k3=k3, v3=v3, qd3=qd3,
                          kd3=kd3, vd3=vd3, o_cat=o_cat, mix=mix, h1=h1, up1=up1, up2=up2, act=act, f=f))
        if l + 1 < DEPTH:
            h, h_m = resln_fwd(h1, f, kp['lfg'], kp['lfb'], "_ffn" + tag)

    after = lambda a, tok: a if tok is None else a + tok
    small_acc = None
    dh_parts = None
    loss = None
    tok = None
    g_mix = None
    for l in reversed(range(DEPTH)):
        tag = f"_l{l}"
        s = saved[l]
        kp, m = s['kp'], s['m']
        dkp = {}
        lfg = after(kp['lfg'], tok)
        if l == DEPTH - 1:
            dh1_a, df, dkp['lfg'], dkp['lfb'], loss = final_bwd(s['h1'], s['f'], target, lfg, kp['lfb'], tag)
        else:
            dh1_a, df, dkp['lfg'], dkp['lfb'] = resln_bwd(s['h1'], s['f'], lfg, kp['lfb'], dh_parts, "_ffn" + tag)
        g_down = matmul(s['act'], df, 'tn', COMM_DTYPE, "mm_gdown" + tag)
        dact = matmul(df, m['ffn_w_down'], 'nt', F32, "mm_dact" + tag)
        dup1, dup2, dkp['fcw'], dkp['fcb'] = ffnconv_bwd(s['up1'], s['up2'], kp['fcw'], kp['fcb'], dact, tag)
        dh1 = matmul([dup1, dup2], [m['ffn_w_up']] * 2, 'nn', F32, "mm_dh1" + tag,
                     b_rows=[(0, D_FF), (D_FF, D_FF)], add=[dh1_a], tm_max=512)
        g_up = matmul(dup1, s['h1_m'], 'tn', COMM_DTYPE, "mm_gup1" + tag,
                      into=(lax.empty((2 * D_FF, D_MODEL), COMM_DTYPE), 0))
        g_up = matmul(dup2, s['h1_m'], 'tn', COMM_DTYPE, "mm_gup2" + tag, into=(g_up, D_FF))
        tok = hook(f"ffn{l}", {('ffn_w_down', l): g_down, ('ffn_w_up', l): g_up})
        dh_a, dmix, dkp['lmg'], dkp['lmb'] = resln_bwd(s['h'], s['mix'], after(kp['lmg'], tok), kp['lmb'],
                                                       [dh1], "_mix" + tag)
        g_out = matmul(s['o_cat'], dmix, 'tn', COMM_DTYPE, "mm_gout" + tag)
        w_out = m['w_out']
        if l == 0:
            w_out = w_out + hook("out0", {('w_out', l): g_out}).astype(w_out.dtype)
        do_cat = matmul(dmix, w_out, 'nt', F32, "mm_docat" + tag)
        lse_b3 = s['lse_b3']
        do_c = (do_cat, (ROW_TILE, GROUP_W), lambda i: (i, 2))
        pair_w = 2 * HEAD_DIM
        dq3, dk3, dv3 = attn_bwd(s['q3'], s['k3'], s['v3'], s['o_b3'], lse_b3, do_cat, GROUP_W // pair_w,
                                 scale_b, "_b" + tag)
        dqd3, dkd3, dvd3 = attn_bwd(s['qd3'], s['kd3'], s['vd3'], s['o_d3'], s['lse_d3'], do_cat,
                                    3 * GROUP_W // pair_w, scale_d, "_d" + tag)
        daglu_pad, dkp['caw'], dkp['cab'], dkp['lag'], dkp['lab'] = aconv_bwd(s['aglu_pad'], kp, do_cat, tag)
        cts = [daglu_pad[CONV_A_HALO:CONV_A_HALO + seq], dq3, dk3, dv3, do_c, dqd3, dkd3, dvd3]
        pre_g = pre_bwd(s['proj'], tabs, kp, cts, tag)
        dproj = pre_g[0]
        for n, g in zip(('qng', 'kng', 'sg', 'sb', 'sw', 'sbt', 'mqn', 'wuq', 'mkvn', 'wukv'), pre_g[1:]):
            dkp[n] = g
        dh_parts = [matmul(dproj, m['w_in'], 'nn', F32, "mm_dh" + tag, add=[dh_a])]
        g_in = matmul(dproj, s['h_m'], 'tn', COMM_DTYPE, "mm_gin" + tag)
        (dw,) = s['unprep'](dkp)
        small_acc = dw if small_acc is None else jax.tree.map(jnp.add, small_acc, dw)
        g_mix = {('w_out', l): g_out, ('w_in', l): _unpad_w_in(g_in)}
        if l > 0:
            tok = hook(f"mix{l}", g_mix)
        else:
            g_mix.pop(('w_out', l))

    g_mix.update({(n, None): small_acc[n] for n in SHARDED if n not in MATMUL_WEIGHTS})
    tok = hook("last", g_mix)
    dx, dg, db = stage_bwd("ln_in_bwd", fn_ln,
                           [_row_op(x, tm, grad=True), _par_op(after(ln_in_g, tok), grad=True),
                            _par_op(ln_in_b, grad=True)],
                           [[(p, (tm, D_MODEL), lambda i: (i, 0)) for p in dh_parts]], (seq // tm,))
    out = {n: small_acc[n] for n in REPLICATED}
    out['ln_in_g'], out['ln_in_b'] = dg.reshape(-1), db.reshape(-1)
    return loss, dx, out


def _peer(x, y, c, r):
    return ((1 - x) if r & 4 else x, (1 - y) if r & 2 else y, (1 - c) if r & 1 else c)


def _exchange_copy(src_ref, land_ref, send_sems, recv_sems, k, gather, x, y, c, r):
    px, py, pc = _peer(x, y, c, r)
    me, peer = 4 * x + 2 * y + c, 4 * px + 2 * py + pc
    src = src_ref if gather else src_ref.at[peer]
    mk = lambda dst: pltpu.make_async_remote_copy(
        src_ref=src, dst_ref=dst, send_sem=send_sems.at[k * (N_DEV - 1) + r - 1],
        recv_sem=recv_sems.at[k * (N_DEV - 1) + r - 1],
        device_id=(px, py, pc), device_id_type=pl.DeviceIdType.MESH)
    return mk(land_ref.at[me]), mk(land_ref.at[peer])


_HBM_SPEC = pl.BlockSpec(memory_space=pltpu.HBM)
_SEM_SPEC = pl.BlockSpec(memory_space=pltpu.SEMAPHORE)


def exchange_start(srcs, gather, groups, name):
    n_t = len(srcs)
    lands =[lax.empty(((N_DEV,) + s.shape) if gt else s.shape, s.dtype) for s, gt in zip(srcs, gather)]

    def body(*refs):
        src_refs, land_refs = refs[:n_t], refs[n_t:2 * n_t]
        sem_refs = refs[2 * n_t:2 * n_t + 2 * len(groups)]
        token = refs[-1]
        x, y, c = lax.axis_index("x"), lax.axis_index("y"), lax.axis_index("c")
        for gi, g in enumerate(groups):
            for k, t in enumerate(g):
                for r in range(1, N_DEV):
                    _exchange_copy(src_refs[t], land_refs[t], sem_refs[2 * gi], sem_refs[2 * gi + 1], k, gather[t],
                                   x, y, c, r)[0].start()
        token[...] = jnp.zeros_like(token)

    sem_shapes = []
    for g in groups:
        sem_shapes += [pltpu.SemaphoreType.DMA((len(g) * (N_DEV - 1),))] * 2
    hbm_shapes = [pltpu.HBM(a.shape, a.dtype) for a in list(srcs) + lands]
    n_sem = len(sem_shapes)
    res = pl.pallas_call(
        body, name=name,
        out_shape=tuple(sem_shapes + hbm_shapes + [jax.ShapeDtypeStruct((8, 128), F32)]),
        in_specs=[_HBM_SPEC] * (2 * n_t),
        out_specs=tuple([_SEM_SPEC] * n_sem + [_HBM_SPEC] * (2 * n_t) + [pl.BlockSpec(memory_space=pltpu.VMEM)]),
        input_output_aliases={i: n_sem + i for i in range(2 * n_t)},
        compiler_params=pltpu.CompilerParams(has_side_effects=pltpu.SideEffectType.DATAFLOW_SIDE_EFFECTING),
    )(*[pltpu.with_memory_space_constraint(a, pltpu.HBM) for a in list(srcs) + lands])
    sems = [(res[2 * gi], res[2 * gi + 1]) for gi in range(len(groups))]
    return sems, list(res[n_sem:n_sem + n_t]), list(res[n_sem + n_t:n_sem + 2 * n_t]), res[-1]


def exchange_wait(sems, srcs, lands, gather, after, name):
    n_t = len(srcs)

    def body(*refs):
        src_refs, land_refs = refs[:n_t], refs[n_t:2 * n_t]
        send_sems, recv_sems = refs[2 * n_t], refs[2 * n_t + 1]
        x, y, c = lax.axis_index("x"), lax.axis_index("y"), lax.axis_index("c")
        for k in range(n_t):
            for r in range(1, N_DEV):
                send, recv = _exchange_copy(src_refs[k], land_refs[k], send_sems, recv_sems, k, gather[k], x, y, c, r)
                send.wait_send()
                recv.wait_recv()

    res = pl.pallas_call(
        body, name=name,
        out_shape=tuple(pltpu.HBM(a.shape, a.dtype) for a in list(srcs) + list(lands)),
        in_specs=[_HBM_SPEC] * (2 * n_t) + [_SEM_SPEC, _SEM_SPEC, pl.BlockSpec(memory_space=pl.ANY)],
        out_specs=tuple([_HBM_SPEC] * (2 * n_t)),
        input_output_aliases={i: i for i in range(2 * n_t)},
        compiler_params=pltpu.CompilerParams(has_side_effects=pltpu.SideEffectType.DATAFLOW_SIDE_EFFECTING),
    )(*srcs, *lands, sems[0], sems[1], after)
    return list(res[:n_t]), list(res[n_t:])


def adamw(parts, w, m, v, name):
    n_l, n_r, n_c = w.shape
    tr = n_r
    if n_r % 8 == 0:
        for cand in (512, 256, 128, 64, 32, 16, 8):
            if n_r % cand == 0 and cand * n_c * 4 <= 512 * 1024:
                tr = cand
                break
    c1 = 1.0 - ADAM_B1 ** ADAM_STEP
    c2 = 1.0 - ADAM_B2 ** ADAM_STEP
    per_layer = isinstance(parts, (list, tuple))
    n_p = n_l if per_layer else 1
    n_rb = n_r // tr

    def update(g, w_ref, m_ref, v_ref, g_ref, d_ref, nm_ref, nv_ref):
        w_, m_, v_ = w_ref[0], m_ref[0], v_ref[0]
        nm = ADAM_B1 * m_ + (1.0 - ADAM_B1) * g
        nv = ADAM_B2 * v_ + (1.0 - ADAM_B2) * (g * g)
        g_ref[0] = g
        nm_ref[0] = nm
        nv_ref[0] = nv
        d_ref[0] = -ADAM_LR * ((nm / c1) / (jnp.sqrt(nv / c2) + ADAM_EPS) + ADAM_WD * w_)

    def body(*refs):
        p_refs, rest = refs[:n_p], refs[n_p:]
        if not per_layer:
            g = p_refs[0][0, 0].astype(F32)
            for s in range(1, N_DEV):
                g = g + p_refs[0][s, 0].astype(F32)
            update(g, *rest)
        else:
            for lay in range(n_l):
                @pl.when(pl.program_id(0) == lay)
                def _(lay=lay):
                    g = p_refs[lay][0].astype(F32)
                    for s in range(1, N_DEV):
                        g = g + p_refs[lay][s].astype(F32)
                    update(g, *rest)

    blk = pl.BlockSpec((1, tr, n_c), lambda l, r: (l, r, 0))
    if per_layer:
        def p_spec(lay):
            park = 0 if lay > 0 else n_rb - 1
            return pl.BlockSpec((N_DEV, tr, n_c), lambda l, r: (0, jnp.where(l == lay, r, park), 0))
        p_specs, p_args = [p_spec(lay) for lay in range(n_l)], list(parts)
    else:
        p_specs, p_args = [pl.BlockSpec((N_DEV, 1, tr, n_c), lambda l, r: (0, l, r, 0))], [parts]
    return pl.pallas_call(
        body, grid=(n_l, n_rb), in_specs=p_specs + [blk, blk, blk],
        out_specs=[blk] * 4, out_shape=[jax.ShapeDtypeStruct(w.shape, F32)] * 4,
        compiler_params=pltpu.CompilerParams(dimension_semantics=("arbitrary", "arbitrary"),
                                             vmem_limit_bytes=VMEM_LIMIT),
        name=name)(*p_args, w, m, v)


def adamw_replicated(lands, own, ws, ms, vs, loss_land, loss_own):
    n_t = len(lands)
    c1 = 1.0 - ADAM_B1 ** ADAM_STEP
    c2 = 1.0 - ADAM_B2 ** ADAM_STEP

    def body(*refs):
        ins, outs = refs[:5 * n_t + 2], refs[5 * n_t + 2:]
        me = 4 * lax.axis_index("x") + 2 * lax.axis_index("y") + lax.axis_index("c")

        def total(land_ref, own_ref):
            g = None
            for s in range(N_DEV):
                term = jnp.where(me == s, own_ref[...], land_ref[s])
                g = term if g is None else g + term
            return g

        for t in range(n_t):
            land_ref, own_ref, w_ref, m_ref, v_ref = ins[5 * t:5 * t + 5]
            g = total(land_ref, own_ref)
            nm = ADAM_B1 * m_ref[...] + (1.0 - ADAM_B1) * g
            nv = ADAM_B2 * v_ref[...] + (1.0 - ADAM_B2) * (g * g)
            g_ref, d_ref, nm_ref, nv_ref = outs[4 * t:4 * t + 4]
            g_ref[...] = g
            nm_ref[...] = nm
            nv_ref[...] = nv
            d_ref[...] = -ADAM_LR * ((nm / c1) / (jnp.sqrt(nv / c2) + ADAM_EPS) + ADAM_WD * w_ref[...])
        outs[4 * n_t][...] = total(ins[5 * n_t], ins[5 * n_t + 1])

    args = []
    for t in range(n_t):
        args += [lands[t], own[t], ws[t], ms[t], vs[t]]
    out_shape = []
    for t in range(n_t):
        out_shape += [jax.ShapeDtypeStruct(ws[t].shape, F32)] * 4
    out_shape.append(jax.ShapeDtypeStruct(loss_own.shape, F32))
    res = pl.pallas_call(body, out_shape=out_shape,
                         compiler_params=pltpu.CompilerParams(vmem_limit_bytes=VMEM_LIMIT),
                         name="adamw_replicated")(*args, loss_land, loss_own)
    return [tuple(res[4 * t:4 * t + 4]) for t in range(n_t)], res[-1]


def _shard_slots(g, axis):
    if axis == 1:
        return g.reshape(g.shape[0], N_DEV, g.shape[1] // N_DEV, g.shape[2]).transpose(1, 0, 2, 3)
    return g.reshape(g.shape[0], g.shape[1], N_DEV, g.shape[2] // N_DEV).transpose(2, 0, 1, 3)


def _unshard(slots, axis):
    if axis == 1:
        return slots.transpose(1, 0, 2, 3).reshape(slots.shape[1], -1, slots.shape[3])
    return slots.transpose(1, 2, 0, 3).reshape(slots.shape[1], slots.shape[2], -1)


def kernel(x, ln_in_g, ln_in_b, w_in, conv_a_w, conv_a_b, ln_a_g, ln_a_b, qk_norm_q, qk_norm_k, sgu_ln_g, sgu_ln_b, sgu_w, sgu_b, mla_q_norm, mla_w_uq, mla_kv_norm, mla_w_ukv, w_out, ln_mix_g, ln_mix_b, ffn_w_up, ffn_conv_w, ffn_conv_b, ffn_w_down, ln_ffn_g, ln_ffn_b, loss_target, m_ln_in_g, m_ln_in_b, m_w_in, m_conv_a_w, m_conv_a_b, m_ln_a_g, m_ln_a_b, m_qk_norm_q, m_qk_norm_k, m_sgu_ln_g, m_sgu_ln_b, m_sgu_w, m_sgu_b, m_mla_q_norm, m_mla_w_uq, m_mla_kv_norm, m_mla_w_ukv, m_w_out, m_ln_mix_g, m_ln_mix_b, m_ffn_w_up, m_ffn_conv_w, m_ffn_conv_b, m_ffn_w_down, m_ln_ffn_g, m_ln_ffn_b, v_ln_in_g, v_ln_in_b, v_w_in, v_conv_a_w, v_conv_a_b, v_ln_a_g, v_ln_a_b, v_qk_norm_q, v_qk_norm_k, v_sgu_ln_g, v_sgu_ln_b, v_sgu_w, v_sgu_b, v_mla_q_norm, v_mla_w_uq, v_mla_kv_norm, v_mla_w_ukv, v_w_out, v_ln_mix_g, v_ln_mix_b, v_ffn_w_up, v_ffn_conv_w, v_ffn_conv_b, v_ffn_w_down, v_ln_ffn_g, v_ln_ffn_b):
    local = dict(ln_in_g=ln_in_g, ln_in_b=ln_in_b, w_in=w_in, conv_a_w=conv_a_w, conv_a_b=conv_a_b, ln_a_g=ln_a_g, ln_a_b=ln_a_b, qk_norm_q=qk_norm_q, qk_norm_k=qk_norm_k, sgu_ln_g=sgu_ln_g, sgu_ln_b=sgu_ln_b, sgu_w=sgu_w, sgu_b=sgu_b, mla_q_norm=mla_q_norm, mla_w_uq=mla_w_uq, mla_kv_norm=mla_kv_norm, mla_w_ukv=mla_w_ukv, w_out=w_out, ln_mix_g=ln_mix_g, ln_mix_b=ln_mix_b, ffn_w_up=ffn_w_up, ffn_conv_w=ffn_conv_w, ffn_conv_b=ffn_conv_b, ffn_w_down=ffn_w_down, ln_ffn_g=ln_ffn_g, ln_ffn_b=ln_ffn_b)
    mom = dict(ln_in_g=m_ln_in_g, ln_in_b=m_ln_in_b, w_in=m_w_in, conv_a_w=m_conv_a_w, conv_a_b=m_conv_a_b, ln_a_g=m_ln_a_g, ln_a_b=m_ln_a_b, qk_norm_q=m_qk_norm_q, qk_norm_k=m_qk_norm_k, sgu_ln_g=m_sgu_ln_g, sgu_ln_b=m_sgu_ln_b, sgu_w=m_sgu_w, sgu_b=m_sgu_b, mla_q_norm=m_mla_q_norm, mla_w_uq=m_mla_w_uq, mla_kv_norm=m_mla_kv_norm, mla_w_ukv=m_mla_w_ukv, w_out=m_w_out, ln_mix_g=m_ln_mix_g, ln_mix_b=m_ln_mix_b, ffn_w_up=m_ffn_w_up, ffn_conv_w=m_ffn_conv_w, ffn_conv_b=m_ffn_conv_b, ffn_w_down=m_ffn_w_down, ln_ffn_g=m_ln_ffn_g, ln_ffn_b=m_ln_ffn_b)
    var = dict(ln_in_g=v_ln_in_g, ln_in_b=v_ln_in_b, w_in=v_w_in, conv_a_w=v_conv_a_w, conv_a_b=v_conv_a_b, ln_a_g=v_ln_a_g, ln_a_b=v_ln_a_b, qk_norm_q=v_qk_norm_q, qk_norm_k=v_qk_norm_k, sgu_ln_g=v_sgu_ln_g, sgu_ln_b=v_sgu_ln_b, sgu_w=v_sgu_w, sgu_b=v_sgu_b, mla_q_norm=v_mla_q_norm, mla_w_uq=v_mla_w_uq, mla_kv_norm=v_mla_kv_norm, mla_w_ukv=v_mla_w_ukv, w_out=v_w_out, ln_mix_g=v_ln_mix_g, ln_mix_b=v_ln_mix_b, ffn_w_up=v_ffn_w_up, ffn_conv_w=v_ffn_conv_w, ffn_conv_b=v_ffn_conv_b, ffn_w_down=v_ffn_w_down, ln_ffn_g=v_ln_ffn_g, ln_ffn_b=v_ln_ffn_b)

    me = 4 * lax.axis_index("x") + 2 * lax.axis_index("y") + lax.axis_index("c")

    def own_slot(slots, block):
        return lax.dynamic_update_slice(slots, block[None], (me,) + (0,) * block.ndim)

    small_sharded = [n for n in SHARDED if n not in MATMUL_WEIGHTS]
    big_order = [(n, l) for l in range(DEPTH) for n in MATMUL_WEIGHTS]
    send_view = lambda n, l: (local[n].transpose(0, 2, 1)[l] if n in TRANSPOSED else local[n][l]).astype(COMM_DTYPE)
    srcs = [send_view(*big_order[0])] + [local[n] for n in small_sharded]
    srcs += [send_view(n, l) for (n, l) in big_order[1:]]
    n_first = 1 + len(small_sharded)
    groups = [list(range(n_first))] + [[n_first + j] for j in range(len(big_order) - 1)]
    g_sems, g_srcs, g_lands, tok0 = exchange_start(srcs, [True] * len(srcs), groups, "gather_start")
    tok0 = tok0[0, 0]
    pending = {key: gi for gi, key in enumerate(big_order)}

    def finish(gi, after):
        idx = groups[gi]
        _, lands = exchange_wait(g_sems[gi], [g_srcs[t] for t in idx], [g_lands[t] for t in idx], [True] * len(idx),
                                 after, f"gather_wait{gi}")
        return [own_slot(ld, srcs[t]) for ld, t in zip(lands, idx)]

    first = []

    opt_view = {n: tuple(a.transpose(0, 2, 1) for a in (local[n], mom[n], var[n])) for n in TRANSPOSED}

    def get_wts(after):
        for views in opt_view.values():
            after = after + sum(a[0, :8, :128] for a in views)
        first.extend(finish(0, after))
        wts = {n: local[n] for n in REPLICATED}
        for n, slots in zip(small_sharded, first[1:]):
            wts[n] = _unshard(slots, SHARDED[n])
        return wts

    def unshard_layer(slots, n):
        if SHARDED[n] == 1 or n in TRANSPOSED:
            return slots.reshape(-1, slots.shape[2])
        return slots.transpose(1, 0, 2).reshape(slots.shape[1], -1)

    def mat(l, n, after):
        gi = pending[(n, l)]
        slots = first[0] if gi == 0 else finish(gi, after)[0]
        w = unshard_layer(slots, n).astype(MXU_DTYPE)
        return _pad_w_in(w) if n == 'w_in' else w

    started = []

    def hook(key, grads):
        tensors = []
        for (n, l), g in grads.items():
            if l is None:
                tensors.append(((n, l), _shard_slots(g, SHARDED[n])))
            elif n in TRANSPOSED:
                tensors.append(((n, l), g.reshape(N_DEV, g.shape[0] // N_DEV, g.shape[1]).astype(COMM_DTYPE)))
            else:
                tensors.append(((n, l), _shard_slots(g[None], SHARDED[n])[:, 0].astype(COMM_DTYPE)))
        sems, s_srcs, s_lands, tok = exchange_start([a for _, a in tensors], [False] * len(tensors),
                                                    [list(range(len(tensors)))], "scatter_start_" + key)
        started.append((key, [k for k, _ in tensors], sems[0], s_srcs, s_lands))
        return tok[0, 0]

    loss, dx, grads = local_step(x[0], loss_target[0], (local['ln_in_g'] + tok0, local['ln_in_b']), get_wts, mat, hook)

    as2d = lambda a: a.reshape(-1, a.shape[-1]) if a.ndim > 1 else a.reshape(1, -1)
    small_g = [as2d(grads[n]) for n in REPLICATED] + [jnp.broadcast_to(loss, (8, 128))]
    p_sems, p_srcs, p_lands, p_tok = exchange_start(small_g, [True] * len(small_g), [list(range(len(small_g)))],
                                                    "gather_small_start")

    parts, res = {}, {}

    def finish_scatter(entries, after):
        for key, keys, sems, s_srcs, s_lands in entries:
            s_out, lands = exchange_wait(sems, s_srcs, s_lands, [False] * len(keys), after, "scatter_wait_" + key)
            for k, so, ld in zip(keys, s_out, lands):
                parts[k] = own_slot(ld, lax.dynamic_index_in_dim(so, me, 0, keepdims=False))

    def update(names_):
        for n in names_:
            p = [parts[(n, l)] for l in range(DEPTH)] if n in MATMUL_WEIGHTS else parts[(n, None)]
            if n in TRANSPOSED:
                res[n] = tuple(a.transpose(0, 2, 1) for a in adamw(p, *opt_view[n], "adamw_" + n))
            else:
                res[n] = adamw(p, local[n], mom[n], var[n], "adamw_" + n)

    early = ('ffn_w_up', 'ffn_w_down', 'w_out')
    finish_scatter([e for e in started if e[0] != "last"], p_tok)
    update(early)
    finish_scatter([e for e in started if e[0] == "last"], res[early[-1]][1])
    update([n for n in SHARDED if n not in early])
    updated = jnp.zeros((8, 128), F32) + sum(res[n][1][0, 0, 0] for n in SHARDED)
    p_own, p_lands = exchange_wait(p_sems[0], p_srcs, p_lands, [True] * len(small_g), updated, "gather_small_wait")
    small, loss_sum = adamw_replicated(p_lands[:-1], p_own[:-1], [as2d(local[n]) for n in REPLICATED],
                                       [as2d(mom[n]) for n in REPLICATED], [as2d(var[n]) for n in REPLICATED],
                                       p_lands[-1], p_own[-1])
    for n, quad in zip(REPLICATED, small):
        res[n] = tuple(a.reshape(local[n].shape) for a in quad)
    loss_total = loss_sum[0, 0]

    return (loss_total, dx[None], *[res[n][0] for n in WEIGHTS], *[res[n][1] for n in WEIGHTS],
            *[res[n][2] for n in WEIGHTS], *[res[n][3] for n in WEIGHTS])
```

```python
import functools
import math

import jax
import jax.numpy as jnp
from jax import lax
from jax.experimental import pallas as pl
from jax.experimental.pallas import tpu as pltpu

F32 = jnp.float32
MXU_DTYPE = jnp.bfloat16
COMM_DTYPE = jnp.bfloat16

N_DEV = 8
D_MODEL = 1024
DEPTH = 2
GRID_W = 64
GROUP_W = 256
HEAD_DIM = 64
CONV_A_WIDTH = 31
CONV_A_HALO = 16
GQA_HEADS = 4
GQA_KV_HEADS = 2
CHUNK = 128
SGU_GROUPS = 4
MLA_HEADS = 4
MLA_Q_LORA = 192
MLA_KV_LORA = 128
MLA_NOPE = 64
MLA_ROPE = 32
MLA_V = 64
MLA_DK_PAD = 128
ROPE_THETA = 10000.0
D_FF = 2816
DEEPNORM_ALPHA = (2 * DEPTH) ** 0.25
LN_EPS = 1e-5
RMS_EPS = 1e-6
D_IN_PROJ = 1888

ADAM_LR = 0.001
ADAM_B1 = 0.9
ADAM_B2 = 0.999
ADAM_EPS = 1e-08
ADAM_WD = 0.01
ADAM_STEP = 10

WEIGHTS = ['ln_in_g', 'ln_in_b', 'w_in', 'conv_a_w', 'conv_a_b', 'ln_a_g', 'ln_a_b', 'qk_norm_q', 'qk_norm_k',
           'sgu_ln_g', 'sgu_ln_b', 'sgu_w', 'sgu_b', 'mla_q_norm', 'mla_w_uq', 'mla_kv_norm', 'mla_w_ukv', 'w_out',
           'ln_mix_g', 'ln_mix_b', 'ffn_w_up', 'ffn_conv_w', 'ffn_conv_b', 'ffn_w_down', 'ln_ffn_g', 'ln_ffn_b']
SHARDED = {'w_in': 2, 'conv_a_w': 2, 'mla_w_uq': 2, 'mla_w_ukv': 2, 'w_out': 1, 'ffn_w_up': 2, 'ffn_conv_w': 2,
           'ffn_w_down': 1}
MATMUL_WEIGHTS = ('w_in', 'w_out', 'ffn_w_up', 'ffn_w_down')
TRANSPOSED = ('w_in', 'ffn_w_up')
REPLICATED = [n for n in WEIGHTS if n not in SHARDED]

ROW_TILE = 256
LN_TILE = 512
VMEM_LIMIT = 56 * 1024 * 1024


def _rawdot(a, b, ca, cb):
    return lax.dot_general(a.astype(MXU_DTYPE), b.astype(MXU_DTYPE), (((ca,), (cb,)), ((), ())),
                           preferred_element_type=F32)


@jax.custom_vjp
def mm_nn(a, b):
    return _rawdot(a, b, 1, 0)


def _mm_nn_fwd(a, b):
    return _rawdot(a, b, 1, 0), (a, b)


def _mm_nn_bwd(res, dy):
    a, b = res
    return _rawdot(dy, b, 1, 1), _rawdot(a, dy, 0, 0)


mm_nn.defvjp(_mm_nn_fwd, _mm_nn_bwd)


@jax.custom_vjp
def mm_nt(a, b):
    return _rawdot(a, b, 1, 1)


def _mm_nt_fwd(a, b):
    return _rawdot(a, b, 1, 1), (a, b)


def _mm_nt_bwd(res, dy):
    a, b = res
    return _rawdot(dy, b, 1, 0), _rawdot(dy, a, 0, 0)


mm_nt.defvjp(_mm_nt_fwd, _mm_nt_bwd)


def _pick_tile(d, cands):
    for c in cands:
        if d % c == 0:
            return c
    return d


def matmul(a, b, mode, out_dtype, name, b_rows=None, into=None, add=(), tm_max=1408, tn=None):
    a_list = list(a) if isinstance(a, (list, tuple)) else [a]
    b_list = list(b) if isinstance(b, (list, tuple)) else [b]
    n_p = len(a_list)
    rows_list = [b_rows] if not isinstance(a, (list, tuple)) else (list(b_rows) if b_rows is not None else [None] * n_p)
    b_start, b_size = zip(*[(0, bb.shape[0]) if r is None else r for bb, r in zip(b_list, rows_list)])
    a0, b0 = a_list[0], b_list[0]
    if mode == 'nn':
        (m, k), (k2, n) = a0.shape, (b_size[0], b0.shape[1])
    elif mode == 'nt':
        (m, k), (n, k2) = a0.shape, (b_size[0], b0.shape[1])
    else:
        (k, m), (k2, n) = a0.shape, (b_size[0], b0.shape[1])
    assert k == k2 and all(x.shape == a0.shape for x in a_list) and len(set(b_size)) == 1, (a0.shape, b0.shape, mode)
    tm = _pick_tile(m, tuple(c for c in (1024, 1408, 512, 256, 128) if c <= tm_max))
    tn = _pick_tile(n, (512, 1408, 256, 128)) if tn is None else tn
    assert n % tn == 0, (n, tn)
    tk = _pick_tile(k, (2816, 2048, 1024, 512, 256, 128))
    nk = k // tk
    ca = 0 if mode == 'tn' else 1
    cb = 1 if mode == 'nt' else 0
    b_blk = tn if mode == 'nt' else tk
    assert all(s % b_blk == 0 for s in b_start), (b_rows, b_blk)
    a_spec = pl.BlockSpec((tk, tm), lambda i, j, kk: (kk, i)) if mode == 'tn' else pl.BlockSpec((tm, tk), lambda i, j, kk: (i, kk))

    def b_spec(off):
        if mode == 'nt':
            return pl.BlockSpec((tn, tk), lambda i, j, kk: (j + off, kk))
        return pl.BlockSpec((tk, tn), lambda i, j, kk: (kk + off, j))

    in_specs, args, aliases = [], [], {}
    for x, y, s in zip(a_list, b_list, b_start):
        in_specs += [a_spec, b_spec(s // b_blk)]
        args += [x, y]
    in_specs += [pl.BlockSpec((tm, tn), lambda i, j, kk: (i, j))] * len(add)
    args += list(add)
    out_off, out_shape = 0, jax.ShapeDtypeStruct((m, n), out_dtype)
    if into is not None:
        buf, row = into
        assert row % tm == 0 and buf.shape[1] == n and buf.dtype == out_dtype, (buf.shape, row, tm)
        out_off, out_shape = row // tm, jax.ShapeDtypeStruct(buf.shape, buf.dtype)
        aliases = {len(args): 0}
        in_specs, args = in_specs + [pl.BlockSpec(memory_space=pl.ANY)], args + [buf]
    n_add = len(add)

    def body(*refs):
        o_ref, acc_ref = refs[-2:]
        kk = pl.program_id(2)

        @pl.when(kk == 0)
        def _():
            acc_ref[...] = jnp.zeros_like(acc_ref)

        for p in range(n_p):
            acc_ref[...] += _rawdot(refs[2 * p][...], refs[2 * p + 1][...], ca, cb)

        @pl.when(kk == nk - 1)
        def _():
            total = acc_ref[...]
            for r in refs[2 * n_p:2 * n_p + n_add]:
                total = total + r[...]
            o_ref[...] = total.astype(o_ref.dtype)

    return pl.pallas_call(
        body, grid=(m // tm, n // tn, nk), in_specs=in_specs,
        out_specs=pl.BlockSpec((tm, tn), lambda i, j, kk: (i + out_off, j)),
        out_shape=out_shape, input_output_aliases=aliases,
        scratch_shapes=[pltpu.VMEM((tm, tn), F32)],
        compiler_params=pltpu.CompilerParams(dimension_semantics=("parallel", "parallel", "arbitrary"),
                                             vmem_limit_bytes=VMEM_LIMIT),
        name=name)(*args)


class Op:
    def __init__(self, arr, block, imap, grad=False, acc=False, first=None, gdtype=F32, gshape=None, gimap=None):
        self.arr, self.block, self.imap = arr, block, imap
        self.grad, self.acc, self.first, self.gdtype = grad, acc, first, gdtype
        self.gshape = arr.shape if gshape is None else gshape
        self.gimap = imap if gimap is None else gimap


def _row_op(arr, tm, grad=False, gdtype=F32):
    return Op(arr, (tm, arr.shape[1]), lambda i: (i, 0), grad=grad, gdtype=gdtype)


def _par_op(arr, grad=False):
    nd = arr.ndim
    return Op(arr, arr.shape, lambda i: (0,) * nd, grad=grad, acc=True, first=lambda ids: ids[0] == 0)


def _load(ref):
    v = ref[...]
    return v.astype(F32) if jnp.issubdtype(v.dtype, jnp.floating) else v


def _store_heads(ref, val):
    rows = val.shape[0]
    if len(ref.shape) == 2:
        ref[...] = val.astype(ref.dtype)
    elif ref.shape[1] == rows:
        d = ref.shape[2]
        for h in range(ref.shape[0]):
            ref[h] = val[:, d * h:d * (h + 1)].astype(ref.dtype)
    else:
        d = ref.shape[1]
        assert ref.shape[2] == rows and d != rows, (ref.shape, val.shape)
        for h in range(ref.shape[0]):
            ref[h] = val[:, d * h:d * (h + 1)].T.astype(ref.dtype)


def _load_heads(ref, transposed=False):
    if len(ref.shape) == 2:
        return ref[...].astype(F32)
    parts = [ref[h].astype(F32) for h in range(ref.shape[0])]
    return jnp.concatenate([p.T for p in parts] if transposed else parts, axis=-1)


def stage_fwd(name, fn, ops, outs, grid):
    n_in = len(ops)

    def body(*refs):
        res = fn(*[_load(r) for r in refs[:n_in]])
        for r, o in zip(refs[n_in:], res):
            _store_heads(r, o)

    return pl.pallas_call(
        body, grid=grid, in_specs=[pl.BlockSpec(o.block, o.imap) for o in ops],
        out_specs=[pl.BlockSpec(b, im) for (_, _, b, im) in outs],
        out_shape=[jax.ShapeDtypeStruct(s, d) for (s, d, _, _) in outs],
        compiler_params=pltpu.CompilerParams(dimension_semantics=("parallel",) * len(grid),
                                             vmem_limit_bytes=VMEM_LIMIT),
        name=name)(*[o.arr for o in ops])


def stage_bwd(name, fn, ops, cts, grid, value_acc=False):
    n_in = len(ops)
    ct_flat = [(c + (False,))[:4] for group in cts if group is not None for c in group]
    n_ct = len(ct_flat)
    diff = [i for i, o in enumerate(ops) if o.grad]
    any_acc = value_acc or any(ops[i].acc for i in diff)
    ngrid = len(grid)

    def body(*refs):
        ids = [pl.program_id(a) for a in range(ngrid)]
        vals = [_load(r) for r in refs[:n_in]]
        ct_refs = refs[n_in:n_in + n_ct]
        out_refs = refs[n_in + n_ct:]

        def f(*dv):
            full = list(vals)
            for i, v in zip(diff, dv):
                full[i] = v
            return tuple(fn(*full))

        res, vjp = jax.vjp(f, *[vals[i] for i in diff])
        ct, pos = [], 0
        for group, r in zip(cts, res):
            if group is None:
                ct.append(jnp.ones_like(r))
            else:
                tot = None
                for _ in group:
                    c = _load_heads(ct_refs[pos], ct_flat[pos][3])
                    tot = c if tot is None else tot + c
                    pos += 1
                ct.append(tot)
        grads = vjp(tuple(ct))
        for i, g, r in zip(diff, grads, out_refs):
            if ops[i].acc:
                @pl.when(ops[i].first(ids))
                def _(r=r):
                    r[...] = jnp.zeros_like(r)

                r[...] += g.astype(r.dtype)
            else:
                r[...] = g.astype(r.dtype)
        if value_acc:
            r = out_refs[len(diff)]

            @pl.when(ids[0] == 0)
            def _():
                r[...] = jnp.zeros_like(r)

            r[...] += res[0]

    in_specs = [pl.BlockSpec(o.block, o.imap) for o in ops] + [pl.BlockSpec(b, im) for (_, b, im, _) in ct_flat]
    out_specs = [pl.BlockSpec(ops[i].block, ops[i].gimap) for i in diff]
    out_shape = [jax.ShapeDtypeStruct(ops[i].gshape, ops[i].gdtype) for i in diff]
    if value_acc:
        out_specs.append(pl.BlockSpec((1, 1), lambda *ids: (0, 0)))
        out_shape.append(jax.ShapeDtypeStruct((1, 1), F32))
    sem = ("arbitrary",) * ngrid if any_acc else ("parallel",) * ngrid
    return pl.pallas_call(
        body, grid=grid, in_specs=in_specs, out_specs=out_specs, out_shape=out_shape,
        compiler_params=pltpu.CompilerParams(dimension_semantics=sem, vmem_limit_bytes=VMEM_LIMIT),
        name=name)(*[o.arr for o in ops], *[a for (a, _, _, _) in ct_flat])


def _sigmoid(x):
    return 1.0 / (1.0 + jnp.exp(-x))


def _silu(x):
    return x * _sigmoid(x)


def _gelu_tanh(x):
    return 0.5 * x * (1.0 + jnp.tanh(math.sqrt(2.0 / math.pi) * (x + 0.044715 * (x * x * x))))


def _ln(x, g, b):
    mu = jnp.mean(x, axis=-1, keepdims=True)
    xc = x - mu
    var = jnp.mean(xc * xc, axis=-1, keepdims=True)
    return xc * lax.rsqrt(var + LN_EPS) * g + b


def _rms(x, g):
    ms = jnp.mean(x * x, axis=-1, keepdims=True)
    return x * lax.rsqrt(ms + RMS_EPS) * g


def _swap_halves(x, half):
    width = x.shape[-1]
    lane = lax.broadcasted_iota(jnp.int32, x.shape, 1)
    return jnp.where(lane % (2 * half) < half, pltpu.roll(x, width - half, 1), pltpu.roll(x, half, 1))


def _make_swap(half):
    @jax.custom_vjp
    def swap(x):
        return _swap_halves(x, half)

    swap.defvjp(lambda x: (_swap_halves(x, half), None), lambda _, dy: (_swap_halves(dy, half),))
    return swap


_swap16, _swap8 = _make_swap(16), _make_swap(8)


def _rope(x, cos, sin_signed, swap):
    return x * cos + swap(x) * sin_signed


def _dot_f32(a, b):
    return jnp.dot(a, b, preferred_element_type=F32, precision=lax.Precision.HIGHEST)


def fn_ln(x, g, b):
    return (_ln(x, g, b),)


def _twice(fn):
    def f(*a):
        (y,) = fn(*a)
        return y, y
    return f


PROJ_W = 2048
P_A, P_Q, P_K, P_V, P_C, P_CQ, P_CKV, P_KR = 0, 512, 768, 896, 1024, 1536, 1792, 1920
CQ_PAD = 256
_CQ_END = P_CQ + MLA_Q_LORA


def _pad_w_in(wt):
    z = lambda n: jnp.zeros((n, wt.shape[1]), wt.dtype)
    return jnp.concatenate([wt[:_CQ_END], z(P_CKV - _CQ_END), wt[_CQ_END:], z(PROJ_W - P_KR - MLA_ROPE)], axis=0)


def _unpad_w_in(gt):
    return jnp.concatenate([gt[:_CQ_END], gt[P_CKV:P_KR + MLA_ROPE]], axis=0)


def fn_pre(proj, tab_q, tab_d, seg, place, qng, kng, sg, sb, sw, sbt, mqn, wuq, mkvn, wukv):
    tm = proj.shape[0]
    aglu = proj[:, P_A:P_A + GROUP_W] * _sigmoid(proj[:, P_A + GROUP_W:P_Q])
    b_q, b_k, b_v = proj[:, P_Q:P_K], proj[:, P_K:P_V], proj[:, P_V:P_C]
    cos_q, sin_q = tab_q[:, :GROUP_W], tab_q[:, GROUP_W:]
    q = b_q * lax.rsqrt(_dot_f32(b_q * b_q, seg) + RMS_EPS) * qng
    q = _rope(q, cos_q, sin_q, _swap16)
    k = b_k * lax.rsqrt(_dot_f32(b_k * b_k, seg[:128, :128]) + RMS_EPS) * kng
    k = _rope(k, cos_q[:, :128], sin_q[:, :128], _swap16)
    c = _gelu_tanh(proj[:, P_C:P_CQ])
    u, sv = c[:, :GROUP_W], _ln(c[:, GROUP_W:], sg, sb)
    group = lax.broadcasted_iota(jnp.int32, (CHUNK, GROUP_W), 1) // HEAD_DIM
    rows = []
    for n in range(tm // CHUNK):
        svn = sv[CHUNK * n:CHUNK * (n + 1)]
        acc = jnp.zeros((CHUNK, GROUP_W), F32)
        for g in range(SGU_GROUPS):
            acc = acc + jnp.where(group == g, mm_nn(sw[CHUNK * g:CHUNK * (g + 1)], svn) + sbt[:, g:g + 1], 0.0)
        rows.append(acc)
    o_c = u * jnp.concatenate(rows, axis=0)
    d_cq, d_ckv, d_kr = proj[:, P_CQ:P_CKV], proj[:, P_CKV:P_KR], proj[:, P_KR:PROJ_W]
    cqn = d_cq * lax.rsqrt(jnp.sum(d_cq * d_cq, axis=-1, keepdims=True) * (1.0 / MLA_Q_LORA) + RMS_EPS) * mqn
    cos_d = jnp.concatenate([tab_d[:, :MLA_DK_PAD]] * MLA_HEADS, axis=-1)
    sin_d = jnp.concatenate([tab_d[:, MLA_DK_PAD:]] * MLA_HEADS, axis=-1)
    qf = _rope(mm_nn(cqn, wuq), cos_d, sin_d, _swap8)
    kvd = mm_nn(_rms(d_ckv, mkvn), wukv)
    kf = _rope(kvd[:, :MLA_HEADS * MLA_DK_PAD] + _dot_f32(d_kr, place), cos_d, sin_d, _swap8)
    return aglu, q, k, b_v, o_c, qf, kf, kvd[:, MLA_HEADS * MLA_DK_PAD:]


def fn_aconv(win, w, b, g, beta):
    tm = win.shape[0] - 2 * CONV_A_HALO
    off = CONV_A_HALO - CONV_A_WIDTH // 2
    acc = None
    for r in range(8):
        rolled = win if r == 0 else _roll_rows(win, -r)
        for kk in range(CONV_A_WIDTH):
            if (off + kk) % 8 == r:
                base = off + kk - r
                term = rolled[base:base + tm] * w[kk:kk + 1, :]
                acc = term if acc is None else acc + term
    return (_silu(_ln(acc + b, g, beta)),)


def fn_resln(h, r, g, b):
    return (_ln(DEEPNORM_ALPHA * h + r, g, b),)


@functools.partial(jax.custom_vjp, nondiff_argnums=(1,))
def _roll_rows(x, shift):
    return pltpu.roll(x, shift % x.shape[0], 0)


_roll_rows.defvjp(lambda x, shift: (pltpu.roll(x, shift % x.shape[0], 0), None),
                  lambda shift, _, dy: (pltpu.roll(dy, (-shift) % dy.shape[0], 0),))


def _shift_down(x):
    row = lax.broadcasted_iota(jnp.int32, x.shape, 0)
    return jnp.where(row == 0, 0.0, _roll_rows(x, 1))


def _shift_up(x):
    row = lax.broadcasted_iota(jnp.int32, x.shape, 0)
    return jnp.where(row == x.shape[0] - 1, 0.0, _roll_rows(x, -1))


def fn_ffnconv(u1, u2, w1, w2, b1, b2):
    c1 = _shift_down(u1) * w1[0:1] + u1 * w1[1:2] + _shift_up(u1) * w1[2:3] + b1
    c2 = _shift_down(u2) * w2[0:1] + u2 * w2[1:2] + _shift_up(u2) * w2[2:3] + b2
    return (_silu(c1) * c2,)


def fn_final(h, r, t, g, b):
    y = _ln(DEEPNORM_ALPHA * h + r, g, b)
    err = (y - t) * (y - t)
    return (0.5 * jnp.sum(jnp.mean(err, axis=-1, keepdims=True), axis=0, keepdims=True),)


def _rope_tables(seq):
    n_rows = seq // GRID_W
    lane128 = jnp.arange(128)

    def tile_tables(j, rotated, half):
        inv = ROPE_THETA ** (-(j % half).astype(F32) / half)
        by_row, by_col = rotated & (j < 2 * half), rotated & (j >= 2 * half)
        sign = jnp.where(j % (2 * half) < half, -1.0, 1.0)
        ar = jnp.arange(n_rows, dtype=F32)[:, None] * inv[None, :]
        ac = jnp.arange(GRID_W, dtype=F32)[:, None] * inv[None, :]
        grid = lambda r, c: (jnp.where(by_row, r, 0.0)[:, None, :] + jnp.where(by_col, c, 0.0)[None, :, :])
        cos = grid(jnp.cos(ar), jnp.cos(ac)) + jnp.where(rotated, 0.0, 1.0)
        sin = grid(sign * jnp.sin(ar), sign * jnp.sin(ac))
        return cos.reshape(seq, 128), sin.reshape(seq, 128)

    cos_b, sin_b = tile_tables(lane128 % HEAD_DIM, lane128 >= 0, HEAD_DIM // 4)
    tab_q = jnp.concatenate([cos_b] * (GROUP_W // 128) + [sin_b] * (GROUP_W // 128), axis=-1)
    tab_d = jnp.concatenate(tile_tables(lane128 - MLA_NOPE, (lane128 >= MLA_NOPE) & (lane128 < MLA_NOPE + MLA_ROPE),
                                        MLA_ROPE // 4), axis=-1)
    lane = jnp.arange(GROUP_W)
    seg = jnp.where(lane[:, None] // HEAD_DIM == lane[None, :] // HEAD_DIM, 1.0 / HEAD_DIM, 0.0).astype(F32)
    src, dst = jnp.arange(128)[:, None], jnp.arange(MLA_HEADS * MLA_DK_PAD)[None, :]
    place = jnp.where((src < MLA_ROPE) & (dst % MLA_DK_PAD == MLA_NOPE + src), 1.0, 0.0).astype(F32)
    return tab_q, tab_d, seg, place


def _pre_ops(proj, tabs, kp, grad):
    tm = ROW_TILE
    ops = [_row_op(proj, tm, grad=grad, gdtype=MXU_DTYPE), _row_op(tabs[0], tm), _row_op(tabs[1], tm),
           _par_op(tabs[2]), _par_op(tabs[3])]
    ops += [_par_op(kp[n], grad=grad) for n in ('qng', 'kng', 'sg', 'sb', 'sw', 'sbt', 'mqn', 'wuq', 'mkvn', 'wukv')]
    return ops


PRE_OUTS = ((0, GROUP_W), (GQA_HEADS, HEAD_DIM), (GQA_KV_HEADS, HEAD_DIM), (GQA_KV_HEADS, HEAD_DIM), (0, GROUP_W),
            (MLA_HEADS, MLA_DK_PAD), (MLA_HEADS, MLA_DK_PAD), (MLA_HEADS, MLA_V))


def _pre_out_specs(seq, tm):
    specs = []
    for heads, w in PRE_OUTS:
        if heads:
            specs.append(((heads, seq, w), (heads, tm, w), lambda i: (0, i, 0)))
        else:
            specs.append(((seq, w), (tm, w), lambda i: (i, 0)))
    return specs


PRE_KV = (2, 3, 6, 7)


def _transposed_spec(heads, w, seq, tm):
    return (heads, w, seq), (heads, w, tm), lambda i: (0, 0, i)


def pre_fwd(proj, tabs, kp, tag):
    seq = proj.shape[0]
    tm = ROW_TILE
    dts = (F32,) + (MXU_DTYPE,) * 7
    outs = [(shape, dt, block, imap) for (shape, block, imap), dt in zip(_pre_out_specs(seq, tm), dts)]
    return stage_fwd("pre_fwd" + tag, fn_pre, _pre_ops(proj, tabs, kp, False), outs, (seq // tm,))


def pre_bwd(proj, tabs, kp, cts, tag):
    seq = proj.shape[0]
    tm = ROW_TILE
    ct = []
    for j, (c, (_, block, imap)) in enumerate(zip(cts, _pre_out_specs(seq, tm))):
        if isinstance(c, tuple):
            ct.append([c])
        elif j in PRE_KV:
            ct.append([(c,) + _transposed_spec(*PRE_OUTS[j], seq, tm)[1:] + (True,)])
        else:
            ct.append([(c, block, imap)])
    return stage_bwd("pre_bwd" + tag, fn_pre, _pre_ops(proj, tabs, kp, True), ct, (seq // tm,))


def _aconv_ops(kp, grad):
    return [_par_op(kp[n], grad=grad) for n in ('caw', 'cab', 'lag', 'lab')]


def aconv_fwd(aglu_pad, kp, tag):
    seq = aglu_pad.shape[0] - 2 * CONV_A_HALO
    tm = ROW_TILE
    n_par = 4

    def body(x_ref, *refs):
        i = pl.program_id(0)
        win = x_ref[pl.ds(pl.multiple_of(i * tm, tm), tm + 2 * CONV_A_HALO), :]
        (o,) = fn_aconv(win, *[_load(r) for r in refs[:n_par]])
        refs[n_par][...] = o.astype(refs[n_par].dtype)

    pars = _aconv_ops(kp, False)
    return pl.pallas_call(
        body, grid=(seq // tm,),
        in_specs=[pl.BlockSpec(aglu_pad.shape, lambda i: (0, 0))] + [pl.BlockSpec(o.block, o.imap) for o in pars],
        out_specs=pl.BlockSpec((tm, GROUP_W), lambda i: (i, 0)),
        out_shape=jax.ShapeDtypeStruct((seq, GROUP_W), MXU_DTYPE),
        compiler_params=pltpu.CompilerParams(dimension_semantics=("parallel",), vmem_limit_bytes=VMEM_LIMIT),
        name="aconv_fwd" + tag)(aglu_pad, *[o.arr for o in pars])


def aconv_bwd(aglu_pad, kp, d_oa, tag):
    seq = aglu_pad.shape[0] - 2 * CONV_A_HALO
    tm = ROW_TILE
    n_par = 4

    def body(x_ref, *refs):
        i = pl.program_id(0)
        rows = pl.ds(pl.multiple_of(i * tm, tm), tm + 2 * CONV_A_HALO)
        pars = [_load(r) for r in refs[:n_par]]
        ct = refs[n_par][...].astype(F32)
        outs = refs[n_par + 1:]
        _, vjp = jax.vjp(lambda *a: fn_aconv(*a), x_ref[rows, :], *pars)
        grads = vjp((ct,))

        @pl.when(i == 0)
        def _():
            for r in outs:
                r[...] = jnp.zeros_like(r)

        outs[0][rows, :] += grads[0]
        for r, g in zip(outs[1:], grads[1:]):
            r[...] += g

    pars = _aconv_ops(kp, True)
    whole = pl.BlockSpec(aglu_pad.shape, lambda i: (0, 0))
    par_specs = [pl.BlockSpec(o.block, o.imap) for o in pars]
    return pl.pallas_call(
        body, grid=(seq // tm,),
        in_specs=[whole] + par_specs + [pl.BlockSpec((tm, GROUP_W), lambda i: (i, 0))],
        out_specs=[whole] + par_specs,
        out_shape=[jax.ShapeDtypeStruct(aglu_pad.shape, F32)] + [jax.ShapeDtypeStruct(o.arr.shape, F32) for o in pars],
        compiler_params=pltpu.CompilerParams(dimension_semantics=("arbitrary",), vmem_limit_bytes=VMEM_LIMIT),
        name="aconv_bwd" + tag)(aglu_pad, *[o.arr for o in pars], d_oa)


ATTN_TQ_FWD = 512
ATTN_TQ = 256
ATTN_TK = 512


def attn_fwd(q3, k3, v3t, scale, tag):
    heads, seq, dk = q3.shape
    group = heads // k3.shape[0]
    kv_per_pair = 2 // group
    dv = v3t.shape[1]
    tq, tk = min(ATTN_TQ_FWD, seq), min(ATTN_TK, seq)
    n_chunks = seq // tk
    log2e = math.log2(math.e)

    def one_head(q, k_ref, vt_ref):
        scores = lambda c: _rawdot(k_ref[pl.ds(c * tk, tk), :], q, 1, 1)
        m, l, acc = jnp.full((1, tq), -jnp.inf, F32), jnp.zeros((1, tq), F32), jnp.zeros((dv, tq), F32)
        s_next = scores(0)
        for c in range(n_chunks):
            s_cur, s_next = s_next, (scores(c + 1) if c + 1 < n_chunks else None)
            t = s_cur * (scale * log2e)
            m_new = jnp.maximum(m, jnp.max(t, axis=0, keepdims=True))
            alpha = jnp.exp2(m - m_new)
            p = jnp.exp2(t - m_new)
            l = alpha * l + jnp.sum(p, axis=0, keepdims=True)
            acc = alpha * acc + _rawdot(vt_ref[:, c * tk:(c + 1) * tk], p, 1, 0)
            m = m_new
        return (acc * (1.0 / l)).T, (m * (1.0 / log2e) + jnp.log(l)).T

    def body(q_ref, k_ref, v_ref, o_ref, lse_ref):
        outs = []
        for h in range(2):
            o, lse = one_head(q_ref[h], k_ref.at[h // group], v_ref.at[h // group])
            lse_ref[h] = lse
            outs.append(o)
        o_ref[...] = jnp.concatenate(outs, axis=-1)

    return pl.pallas_call(
        body, grid=(heads // 2, seq // tq),
        in_specs=[pl.BlockSpec((2, tq, dk), lambda j, i: (j, i, 0)),
                  pl.BlockSpec((kv_per_pair, seq, dk), lambda j, i: (j, 0, 0)),
                  pl.BlockSpec((kv_per_pair, dv, seq), lambda j, i: (j, 0, 0))],
        out_specs=[pl.BlockSpec((tq, 2 * dv), lambda j, i: (i, j)),
                   pl.BlockSpec((2, tq, 1), lambda j, i: (j, i, 0))],
        out_shape=[jax.ShapeDtypeStruct((seq, heads * dv), F32), jax.ShapeDtypeStruct((heads, seq, 1), F32)],
        compiler_params=pltpu.CompilerParams(dimension_semantics=("parallel", "parallel"),
                                             vmem_limit_bytes=VMEM_LIMIT),
        name="attn_fwd" + tag)(q3, k3, v3t)


def attn_bwd(q3, k3, v3, o, lse3, do_all, do_col, scale, tag):
    heads, seq, dk = q3.shape
    group = heads // k3.shape[0]
    kv_per_pair = 2 // group
    dv = v3.shape[2]
    tq, tk = min(ATTN_TQ, seq), min(ATTN_TK, seq)
    n_chunks = seq // tk
    log2e = math.log2(math.e)

    def one_head(q, do, o_h, lse, k_ref, v_ref, dk_ref, dv_ref):
        dob = do.astype(MXU_DTYPE)
        do_t, q_t = do.T.astype(MXU_DTYPE), q.astype(F32).T.astype(MXU_DTYPE)
        delta = jnp.sum(do * o_h, axis=-1, keepdims=True)
        lse2 = lse * log2e
        rows = lambda c: pl.ds(c * tk, tk)
        products = lambda c: (_rawdot(q, k_ref[rows(c), :], 1, 1), _rawdot(dob, v_ref[rows(c), :], 1, 1))
        dq = jnp.zeros((tq, dk), F32)
        nxt = products(0)
        for c in range(n_chunks):
            (s_cur, dp_cur), nxt = nxt, (products(c + 1) if c + 1 < n_chunks else None)
            p = jnp.exp2(s_cur * (scale * log2e) - lse2)
            ds = (p * ((dp_cur - delta) * scale)).astype(MXU_DTYPE)
            dv_ref[:, c * tk:(c + 1) * tk] += _rawdot(do_t, p, 1, 0)
            dk_ref[:, c * tk:(c + 1) * tk] += _rawdot(q_t, ds, 1, 0)
            dq = dq + _rawdot(ds, k_ref[rows(c), :], 1, 0)
        return dq

    def body(q_ref, k_ref, v_ref, o_ref, lse_ref, do_ref, dq_ref, dk_ref, dv_ref):
        @pl.when(pl.program_id(1) == 0)
        def _():
            dk_ref[...] = jnp.zeros_like(dk_ref)
            dv_ref[...] = jnp.zeros_like(dv_ref)

        do_pair, o_pair = do_ref[...], o_ref[...]
        for h in range(2):
            kv = h // group
            dq_ref[h] = one_head(q_ref[h], do_pair[:, dv * h:dv * (h + 1)], o_pair[:, dv * h:dv * (h + 1)],
                                 lse_ref[h], k_ref.at[kv], v_ref.at[kv], dk_ref.at[kv], dv_ref.at[kv])

    qspec = lambda d: pl.BlockSpec((2, tq, d), lambda j, i: (j, i, 0))
    kvspec = lambda d: pl.BlockSpec((kv_per_pair, seq, d), lambda j, i: (j, 0, 0))
    kvt_spec = lambda d: pl.BlockSpec((kv_per_pair, d, seq), lambda j, i: (j, 0, 0))
    kvt_shape = lambda a: jax.ShapeDtypeStruct((a.shape[0], a.shape[2], a.shape[1]), F32)
    return pl.pallas_call(
        body, grid=(heads // 2, seq // tq),
        in_specs=[qspec(dk), kvspec(dk), kvspec(dv), pl.BlockSpec((tq, 2 * dv), lambda j, i: (i, j)), qspec(1),
                  pl.BlockSpec((tq, 2 * dv), lambda j, i: (i, do_col + j))],
        out_specs=[qspec(dk), kvt_spec(dk), kvt_spec(dv)],
        out_shape=[jax.ShapeDtypeStruct(q3.shape, F32), kvt_shape(k3), kvt_shape(v3)],
        compiler_params=pltpu.CompilerParams(dimension_semantics=("parallel", "arbitrary"),
                                             vmem_limit_bytes=VMEM_LIMIT),
        name="attn_bwd" + tag)(q3, k3, v3, o, lse3, do_all)


def resln_fwd(h, r, g, b, tag):
    seq, d = h.shape
    tm = min(LN_TILE, seq)
    ops = [_row_op(h, tm), _row_op(r, tm), _par_op(g), _par_op(b)]
    outs = [((seq, d), dt, (tm, d), lambda i: (i, 0)) for dt in (F32, MXU_DTYPE)]
    return stage_fwd("resln_fwd" + tag, _twice(fn_resln), ops, outs, (seq // tm,))


def resln_bwd(h, r, g, b, dys, tag):
    seq, d = h.shape
    tm = min(LN_TILE, seq)
    ops = [_row_op(h, tm, grad=True), _row_op(r, tm, grad=True, gdtype=MXU_DTYPE), _par_op(g, grad=True),
           _par_op(b, grad=True)]
    ct = [[(dy, (tm, d), lambda i: (i, 0)) for dy in dys]]
    return stage_bwd("resln_bwd" + tag, fn_resln, ops, ct, (seq // tm,))


def _ffnconv_ops(up1, up2, w, b, grad):
    seq = up1.shape[0]
    nblk = D_FF // 128
    lo, hi = (lambda j: (0, j)), (lambda j: (0, j + nblk))
    half = lambda a: dict(gshape=(a.shape[0], D_FF), gimap=lo)
    return [Op(up1, (seq, 128), lo, grad=grad, gdtype=MXU_DTYPE), Op(up2, (seq, 128), lo, grad=grad, gdtype=MXU_DTYPE),
            Op(w, (3, 128), lo, grad=grad, **half(w)), Op(w, (3, 128), hi, grad=grad, **half(w)),
            Op(b, (1, 128), lo, grad=grad, **half(b)), Op(b, (1, 128), hi, grad=grad, **half(b))]


def ffnconv_fwd(up1, up2, w, b, tag):
    seq = up1.shape[0]
    outs = [((seq, D_FF), MXU_DTYPE, (seq, 128), lambda j: (0, j))]
    return stage_fwd("ffnconv_fwd" + tag, fn_ffnconv, _ffnconv_ops(up1, up2, w, b, False), outs, (D_FF // 128,))[0]


def ffnconv_bwd(up1, up2, w, b, dact, tag):
    seq = up1.shape[0]
    ct = [[(dact, (seq, 128), lambda j: (0, j))]]
    du1, du2, dw1, dw2, db1, db2 = stage_bwd("ffnconv_bwd" + tag, fn_ffnconv, _ffnconv_ops(up1, up2, w, b, True),
                                             ct, (D_FF // 128,))
    cat = lambda a, b_: jnp.concatenate([a, b_], axis=-1)
    return du1, du2, cat(dw1, dw2), cat(db1, db2)


def final_bwd(h, r, t, g, b, tag):
    seq, d = h.shape
    tm = min(LN_TILE, seq)
    ops = [_row_op(h, tm, grad=True), _row_op(r, tm, grad=True, gdtype=MXU_DTYPE), _row_op(t, tm),
           _par_op(g, grad=True), _par_op(b, grad=True)]
    return stage_bwd("final_bwd" + tag, fn_final, ops, [None], (seq // tm,), value_acc=True)


def _layer_params(wts, l):
    row = lambda a: a.reshape(1, -1)
    wuq = wts['mla_w_uq'][l].reshape(MLA_Q_LORA, MLA_HEADS, MLA_NOPE + MLA_ROPE)
    wuq = jnp.pad(wuq, ((0, CQ_PAD - MLA_Q_LORA), (0, 0), (0, MLA_DK_PAD - MLA_NOPE - MLA_ROPE)))
    wukv = wts['mla_w_ukv'][l].reshape(MLA_KV_LORA, MLA_HEADS, MLA_NOPE + MLA_V)
    wuk = jnp.pad(wukv[:, :, :MLA_NOPE], ((0, 0), (0, 0), (0, MLA_DK_PAD - MLA_NOPE)))
    return dict(
        qng=jnp.tile(row(wts['qk_norm_q'][l]), (1, GQA_HEADS)), kng=jnp.tile(row(wts['qk_norm_k'][l]), (1, GQA_KV_HEADS)),
        sg=row(wts['sgu_ln_g'][l]), sb=row(wts['sgu_ln_b'][l]),
        sw=wts['sgu_w'][l].reshape(SGU_GROUPS * CHUNK, CHUNK), sbt=wts['sgu_b'][l].T,
        mqn=jnp.pad(row(wts['mla_q_norm'][l]), ((0, 0), (0, CQ_PAD - MLA_Q_LORA))),
        wuq=wuq.reshape(CQ_PAD, MLA_HEADS * MLA_DK_PAD), mkvn=row(wts['mla_kv_norm'][l]),
        wukv=jnp.concatenate([wuk.reshape(MLA_KV_LORA, -1), wukv[:, :, MLA_NOPE:].reshape(MLA_KV_LORA, -1)], axis=1),
        caw=wts['conv_a_w'][l], cab=row(wts['conv_a_b'][l]), lag=row(wts['ln_a_g'][l]), lab=row(wts['ln_a_b'][l]),
        lmg=row(wts['ln_mix_g'][l]), lmb=row(wts['ln_mix_b'][l]),
        fcw=wts['ffn_conv_w'][l], fcb=row(wts['ffn_conv_b'][l]),
        lfg=row(wts['ln_ffn_g'][l]), lfb=row(wts['ln_ffn_b'][l]))


def _to_heads(a, heads):
    seq = a.shape[0]
    return a.reshape(seq, heads, -1).transpose(1, 0, 2)


def _from_heads(a3):
    return a3.transpose(1, 0, 2).reshape(a3.shape[1], -1)


def local_step(x, target, ln_in, get_wts, mat, hook):
    seq = x.shape[0]
    tm = min(LN_TILE, seq)
    tabs = _rope_tables(seq)
    scale_b = HEAD_DIM ** -0.5
    scale_d = (MLA_NOPE + MLA_ROPE) ** -0.5
    ln_in_g, ln_in_b = ln_in[0].reshape(1, -1), ln_in[1].reshape(1, -1)

    h, h_m = stage_fwd("ln_in_fwd", _twice(fn_ln), [_row_op(x, tm), _par_op(ln_in_g), _par_op(ln_in_b)],
                       [((seq, D_MODEL), dt, (tm, D_MODEL), lambda i: (i, 0)) for dt in (F32, MXU_DTYPE)],
                       (seq // tm,))
    wts = get_wts(h_m[:8, :128].astype(F32) + tabs[0][:8, :128] + tabs[1][:8, :128])
    saved = []
    for l in range(DEPTH):
        tag = f"_l{l}"
        kp, unprep = jax.vjp(lambda w: _layer_params(w, l), wts)
        m = {'w_in': mat(l, 'w_in', h_m)}
        proj = matmul(h_m, m['w_in'], 'nt', F32, "mm_proj" + tag, tn=PROJ_W, tm_max=512)
        aglu, q3, k3, v3, o_c, qd3, kd3, vd3 = pre_fwd(proj, tabs, kp, tag)
        aglu_pad = jnp.pad(aglu, ((CONV_A_HALO, CONV_A_HALO), (0, 0)))
        o_a = aconv_fwd(aglu_pad, kp, tag)
        swap = lambda a: a.transpose(0, 2, 1)
        o_b3, lse_b3 = attn_fwd(q3, k3, swap(v3), scale_b, "_b" + tag)
        o_d3, lse_d3 = attn_fwd(qd3, kd3, swap(vd3), scale_d, "_d" + tag)
        o_cat = jnp.concatenate([o_a, o_b3.astype(MXU_DTYPE), o_c, o_d3.astype(MXU_DTYPE)], axis=-1)
        m['w_out'] = mat(l, 'w_out', o_cat)
        mix = matmul(o_cat, m['w_out'], 'nn', F32, "mm_mix" + tag, tn=D_MODEL, tm_max=512)
        h1, h1_m = resln_fwd(h, mix, kp['lmg'], kp['lmb'], "_mix" + tag)
        m['ffn_w_up'] = mat(l, 'ffn_w_up', h1_m)
        wide = dict(tn=D_FF, tm_max=512)
        up1 = matmul(h1_m, m['ffn_w_up'], 'nt', F32, "mm_up1" + tag, b_rows=(0, D_FF), **wide)
        up2 = matmul(h1_m, m['ffn_w_up'], 'nt', F32, "mm_up2" + tag, b_rows=(D_FF, D_FF), **wide)
        act = ffnconv_fwd(up1, up2, kp['fcw'], kp['fcb'], tag)
        m['ffn_w_down'] = mat(l, 'ffn_w_down', act)
        f = matmul(act, m['ffn_w_down'], 'nn', F32, "mm_down" + tag, tn=D_MODEL, tm_max=512)
        saved.append(dict(kp=kp, unprep=unprep, m=m, h=h, h_m=h_m, h1_m=h1_m, proj=proj, o_b3=o_b3, lse_b3=lse_b3,
                          o_d3=o_d3, lse_d3=lse_d3, aglu_pad=aglu_pad, q3=q3, k3=k3, v3=v3, qd3=qd3,
                          kd3=kd3, vd3=vd3, o_cat=o_cat, mix=mix, h1=h1, up1=up1, up2=up2, act=act, f=f))
        if l + 1 < DEPTH:
            h, h_m = resln_fwd(h1, f, kp['lfg'], kp['lfb'], "_ffn" + tag)

    after = lambda a, tok: a if tok is None else a + tok
    small_acc = None
    dh_parts = None
    loss = None
    tok = None
    g_mix = None
    for l in reversed(range(DEPTH)):
        tag = f"_l{l}"
        s = saved[l]
        kp, m = s['kp'], s['m']
        dkp = {}
        lfg = after(kp['lfg'], tok)
        if l == DEPTH - 1:
            dh1_a, df, dkp['lfg'], dkp['lfb'], loss = final_bwd(s['h1'], s['f'], target, lfg, kp['lfb'], tag)
        else:
            dh1_a, df, dkp['lfg'], dkp['lfb'] = resln_bwd(s['h1'], s['f'], lfg, kp['lfb'], dh_parts, "_ffn" + tag)
        g_down = matmul(s['act'], df, 'tn', COMM_DTYPE, "mm_gdown" + tag)
        dact = matmul(df, m['ffn_w_down'], 'nt', F32, "mm_dact" + tag, tn=D_FF, tm_max=512)
        dup1, dup2, dkp['fcw'], dkp['fcb'] = ffnconv_bwd(s['up1'], s['up2'], kp['fcw'], kp['fcb'], dact, tag)
        dh1 = matmul([dup1, dup2], [m['ffn_w_up']] * 2, 'nn', F32, "mm_dh1" + tag,
                     b_rows=[(0, D_FF), (D_FF, D_FF)], add=[dh1_a], tm_max=512)
        g_up = matmul(dup1, s['h1_m'], 'tn', COMM_DTYPE, "mm_gup1" + tag,
                      into=(lax.empty((2 * D_FF, D_MODEL), COMM_DTYPE), 0))
        g_up = matmul(dup2, s['h1_m'], 'tn', COMM_DTYPE, "mm_gup2" + tag, into=(g_up, D_FF))
        tok = hook(f"ffn{l}", {('ffn_w_down', l): g_down, ('ffn_w_up', l): g_up})
        dh_a, dmix, dkp['lmg'], dkp['lmb'] = resln_bwd(s['h'], s['mix'], after(kp['lmg'], tok), kp['lmb'],
                                                       [dh1], "_mix" + tag)
        g_out = matmul(s['o_cat'], dmix, 'tn', COMM_DTYPE, "mm_gout" + tag)
        w_out = m['w_out']
        if l == 0:
            w_out = w_out + hook("out0", {('w_out', l): g_out}).astype(w_out.dtype)
        do_cat = matmul(dmix, w_out, 'nt', F32, "mm_docat" + tag, tn=D_MODEL, tm_max=512)
        lse_b3 = s['lse_b3']
        do_c = (do_cat, (ROW_TILE, GROUP_W), lambda i: (i, 2))
        pair_w = 2 * HEAD_DIM
        dq3, dk3, dv3 = attn_bwd(s['q3'], s['k3'], s['v3'], s['o_b3'], lse_b3, do_cat, GROUP_W // pair_w,
                                 scale_b, "_b" + tag)
        dqd3, dkd3, dvd3 = attn_bwd(s['qd3'], s['kd3'], s['vd3'], s['o_d3'], s['lse_d3'], do_cat,
                                    3 * GROUP_W // pair_w, scale_d, "_d" + tag)
        daglu_pad, dkp['caw'], dkp['cab'], dkp['lag'], dkp['lab'] = aconv_bwd(s['aglu_pad'], kp, do_cat, tag)
        cts = [daglu_pad[CONV_A_HALO:CONV_A_HALO + seq], dq3, dk3, dv3, do_c, dqd3, dkd3, dvd3]
        pre_g = pre_bwd(s['proj'], tabs, kp, cts, tag)
        dproj = pre_g[0]
        for n, g in zip(('qng', 'kng', 'sg', 'sb', 'sw', 'sbt', 'mqn', 'wuq', 'mkvn', 'wukv'), pre_g[1:]):
            dkp[n] = g
        dh_parts = [matmul(dproj, m['w_in'], 'nn', F32, "mm_dh" + tag, add=[dh_a], tn=D_MODEL,
                           tm_max=512)]
        g_in = matmul(dproj, s['h_m'], 'tn', COMM_DTYPE, "mm_gin" + tag)
        (dw,) = s['unprep'](dkp)
        small_acc = dw if small_acc is None else jax.tree.map(jnp.add, small_acc, dw)
        g_mix = {('w_out', l): g_out, ('w_in', l): _unpad_w_in(g_in)}
        if l > 0:
            tok = hook(f"mix{l}", g_mix)
        else:
            g_mix.pop(('w_out', l))

    g_mix.update({(n, None): small_acc[n] for n in SHARDED if n not in MATMUL_WEIGHTS})
    tok = hook("last", g_mix)
    dx, dg, db = stage_bwd("ln_in_bwd", fn_ln,
                           [_row_op(x, tm, grad=True), _par_op(after(ln_in_g, tok), grad=True),
                            _par_op(ln_in_b, grad=True)],
                           [[(p, (tm, D_MODEL), lambda i: (i, 0)) for p in dh_parts]], (seq // tm,))
    out = {n: small_acc[n] for n in REPLICATED}
    out['ln_in_g'], out['ln_in_b'] = dg.reshape(-1), db.reshape(-1)
    return loss, dx, out


def _peer(x, y, c, r):
    return ((1 - x) if r & 4 else x, (1 - y) if r & 2 else y, (1 - c) if r & 1 else c)


def _exchange_copy(src_ref, land_ref, send_sems, recv_sems, k, gather, x, y, c, r):
    px, py, pc = _peer(x, y, c, r)
    me, peer = 4 * x + 2 * y + c, 4 * px + 2 * py + pc
    src = src_ref if gather else src_ref.at[peer]
    mk = lambda dst: pltpu.make_async_remote_copy(
        src_ref=src, dst_ref=dst, send_sem=send_sems.at[k * (N_DEV - 1) + r - 1],
        recv_sem=recv_sems.at[k * (N_DEV - 1) + r - 1],
        device_id=(px, py, pc), device_id_type=pl.DeviceIdType.MESH)
    return mk(land_ref.at[me]), mk(land_ref.at[peer])


_HBM_SPEC = pl.BlockSpec(memory_space=pltpu.HBM)
_SEM_SPEC = pl.BlockSpec(memory_space=pltpu.SEMAPHORE)


def exchange_start(srcs, gather, groups, name):
    n_t = len(srcs)
    lands =[lax.empty(((N_DEV,) + s.shape) if gt else s.shape, s.dtype) for s, gt in zip(srcs, gather)]

    def body(*refs):
        src_refs, land_refs = refs[:n_t], refs[n_t:2 * n_t]
        sem_refs = refs[2 * n_t:2 * n_t + 2 * len(groups)]
        token = refs[-1]
        x, y, c = lax.axis_index("x"), lax.axis_index("y"), lax.axis_index("c")
        for gi, g in enumerate(groups):
            for k, t in enumerate(g):
                for r in range(1, N_DEV):
                    _exchange_copy(src_refs[t], land_refs[t], sem_refs[2 * gi], sem_refs[2 * gi + 1], k, gather[t],
                                   x, y, c, r)[0].start()
        token[...] = jnp.zeros_like(token)

    sem_shapes = []
    for g in groups:
        sem_shapes += [pltpu.SemaphoreType.DMA((len(g) * (N_DEV - 1),))] * 2
    hbm_shapes = [pltpu.HBM(a.shape, a.dtype) for a in list(srcs) + lands]
    n_sem = len(sem_shapes)
    res = pl.pallas_call(
        body, name=name,
        out_shape=tuple(sem_shapes + hbm_shapes + [jax.ShapeDtypeStruct((8, 128), F32)]),
        in_specs=[_HBM_SPEC] * (2 * n_t),
        out_specs=tuple([_SEM_SPEC] * n_sem + [_HBM_SPEC] * (2 * n_t) + [pl.BlockSpec(memory_space=pltpu.VMEM)]),
        input_output_aliases={i: n_sem + i for i in range(2 * n_t)},
        compiler_params=pltpu.CompilerParams(has_side_effects=pltpu.SideEffectType.DATAFLOW_SIDE_EFFECTING),
    )(*[pltpu.with_memory_space_constraint(a, pltpu.HBM) for a in list(srcs) + lands])
    sems = [(res[2 * gi], res[2 * gi + 1]) for gi in range(len(groups))]
    return sems, list(res[n_sem:n_sem + n_t]), list(res[n_sem + n_t:n_sem + 2 * n_t]), res[-1]


def exchange_wait(sems, srcs, lands, gather, after, name):
    n_t = len(srcs)

    def body(*refs):
        src_refs, land_refs = refs[:n_t], refs[n_t:2 * n_t]
        send_sems, recv_sems = refs[2 * n_t], refs[2 * n_t + 1]
        x, y, c = lax.axis_index("x"), lax.axis_index("y"), lax.axis_index("c")
        for k in range(n_t):
            for r in range(1, N_DEV):
                send, recv = _exchange_copy(src_refs[k], land_refs[k], send_sems, recv_sems, k, gather[k], x, y, c, r)
                send.wait_send()
                recv.wait_recv()

    res = pl.pallas_call(
        body, name=name,
        out_shape=tuple(pltpu.HBM(a.shape, a.dtype) for a in list(srcs) + list(lands)),
        in_specs=[_HBM_SPEC] * (2 * n_t) + [_SEM_SPEC, _SEM_SPEC, pl.BlockSpec(memory_space=pl.ANY)],
        out_specs=tuple([_HBM_SPEC] * (2 * n_t)),
        input_output_aliases={i: i for i in range(2 * n_t)},
        compiler_params=pltpu.CompilerParams(has_side_effects=pltpu.SideEffectType.DATAFLOW_SIDE_EFFECTING),
    )(*srcs, *lands, sems[0], sems[1], after)
    return list(res[:n_t]), list(res[n_t:])


def adamw(parts, w, m, v, name):
    n_l, n_r, n_c = w.shape
    tr = n_r
    if n_r % 8 == 0:
        for cand in (512, 256, 128, 64, 32, 16, 8):
            if n_r % cand == 0 and cand * n_c * 4 <= 512 * 1024:
                tr = cand
                break
    c1 = 1.0 - ADAM_B1 ** ADAM_STEP
    c2 = 1.0 - ADAM_B2 ** ADAM_STEP
    per_layer = isinstance(parts, (list, tuple))
    n_p = n_l if per_layer else 1
    n_rb = n_r // tr

    def update(g, w_ref, m_ref, v_ref, g_ref, d_ref, nm_ref, nv_ref):
        w_, m_, v_ = w_ref[0], m_ref[0], v_ref[0]
        nm = ADAM_B1 * m_ + (1.0 - ADAM_B1) * g
        nv = ADAM_B2 * v_ + (1.0 - ADAM_B2) * (g * g)
        g_ref[0] = g
        nm_ref[0] = nm
        nv_ref[0] = nv
        d_ref[0] = -ADAM_LR * ((nm / c1) / (jnp.sqrt(nv / c2) + ADAM_EPS) + ADAM_WD * w_)

    def body(*refs):
        p_refs, rest = refs[:n_p], refs[n_p:]
        if not per_layer:
            g = p_refs[0][0, 0].astype(F32)
            for s in range(1, N_DEV):
                g = g + p_refs[0][s, 0].astype(F32)
            update(g, *rest)
        else:
            for lay in range(n_l):
                @pl.when(pl.program_id(0) == lay)
                def _(lay=lay):
                    g = p_refs[lay][0].astype(F32)
                    for s in range(1, N_DEV):
                        g = g + p_refs[lay][s].astype(F32)
                    update(g, *rest)

    blk = pl.BlockSpec((1, tr, n_c), lambda l, r: (l, r, 0))
    if per_layer:
        def p_spec(lay):
            park = 0 if lay > 0 else n_rb - 1
            return pl.BlockSpec((N_DEV, tr, n_c), lambda l, r: (0, jnp.where(l == lay, r, park), 0))
        p_specs, p_args = [p_spec(lay) for lay in range(n_l)], list(parts)
    else:
        p_specs, p_args = [pl.BlockSpec((N_DEV, 1, tr, n_c), lambda l, r: (0, l, r, 0))], [parts]
    return pl.pallas_call(
        body, grid=(n_l, n_rb), in_specs=p_specs + [blk, blk, blk],
        out_specs=[blk] * 4, out_shape=[jax.ShapeDtypeStruct(w.shape, F32)] * 4,
        compiler_params=pltpu.CompilerParams(dimension_semantics=("arbitrary", "arbitrary"),
                                             vmem_limit_bytes=VMEM_LIMIT),
        name=name)(*p_args, w, m, v)


def adamw_replicated(lands, own, ws, ms, vs, loss_land, loss_own):
    n_t = len(lands)
    c1 = 1.0 - ADAM_B1 ** ADAM_STEP
    c2 = 1.0 - ADAM_B2 ** ADAM_STEP

    def body(*refs):
        ins, outs = refs[:5 * n_t + 2], refs[5 * n_t + 2:]
        me = 4 * lax.axis_index("x") + 2 * lax.axis_index("y") + lax.axis_index("c")

        def total(land_ref, own_ref):
            g = None
            for s in range(N_DEV):
                term = jnp.where(me == s, own_ref[...], land_ref[s])
                g = term if g is None else g + term
            return g

        for t in range(n_t):
            land_ref, own_ref, w_ref, m_ref, v_ref = ins[5 * t:5 * t + 5]
            g = total(land_ref, own_ref)
            nm = ADAM_B1 * m_ref[...] + (1.0 - ADAM_B1) * g
            nv = ADAM_B2 * v_ref[...] + (1.0 - ADAM_B2) * (g * g)
            g_ref, d_ref, nm_ref, nv_ref = outs[4 * t:4 * t + 4]
            g_ref[...] = g
            nm_ref[...] = nm
            nv_ref[...] = nv
            d_ref[...] = -ADAM_LR * ((nm / c1) / (jnp.sqrt(nv / c2) + ADAM_EPS) + ADAM_WD * w_ref[...])
        outs[4 * n_t][...] = total(ins[5 * n_t], ins[5 * n_t + 1])

    args = []
    for t in range(n_t):
        args += [lands[t], own[t], ws[t], ms[t], vs[t]]
    out_shape = []
    for t in range(n_t):
        out_shape += [jax.ShapeDtypeStruct(ws[t].shape, F32)] * 4
    out_shape.append(jax.ShapeDtypeStruct(loss_own.shape, F32))
    res = pl.pallas_call(body, out_shape=out_shape,
                         compiler_params=pltpu.CompilerParams(vmem_limit_bytes=VMEM_LIMIT),
                         name="adamw_replicated")(*args, loss_land, loss_own)
    return [tuple(res[4 * t:4 * t + 4]) for t in range(n_t)], res[-1]


def _shard_slots(g, axis):
    if axis == 1:
        return g.reshape(g.shape[0], N_DEV, g.shape[1] // N_DEV, g.shape[2]).transpose(1, 0, 2, 3)
    return g.reshape(g.shape[0], g.shape[1], N_DEV, g.shape[2] // N_DEV).transpose(2, 0, 1, 3)


def _unshard(slots, axis):
    if axis == 1:
        return slots.transpose(1, 0, 2, 3).reshape(slots.shape[1], -1, slots.shape[3])
    return slots.transpose(1, 2, 0, 3).reshape(slots.shape[1], slots.shape[2], -1)


def kernel(x, ln_in_g, ln_in_b, w_in, conv_a_w, conv_a_b, ln_a_g, ln_a_b, qk_norm_q, qk_norm_k, sgu_ln_g, sgu_ln_b, sgu_w, sgu_b, mla_q_norm, mla_w_uq, mla_kv_norm, mla_w_ukv, w_out, ln_mix_g, ln_mix_b, ffn_w_up, ffn_conv_w, ffn_conv_b, ffn_w_down, ln_ffn_g, ln_ffn_b, loss_target, m_ln_in_g, m_ln_in_b, m_w_in, m_conv_a_w, m_conv_a_b, m_ln_a_g, m_ln_a_b, m_qk_norm_q, m_qk_norm_k, m_sgu_ln_g, m_sgu_ln_b, m_sgu_w, m_sgu_b, m_mla_q_norm, m_mla_w_uq, m_mla_kv_norm, m_mla_w_ukv, m_w_out, m_ln_mix_g, m_ln_mix_b, m_ffn_w_up, m_ffn_conv_w, m_ffn_conv_b, m_ffn_w_down, m_ln_ffn_g, m_ln_ffn_b, v_ln_in_g, v_ln_in_b, v_w_in, v_conv_a_w, v_conv_a_b, v_ln_a_g, v_ln_a_b, v_qk_norm_q, v_qk_norm_k, v_sgu_ln_g, v_sgu_ln_b, v_sgu_w, v_sgu_b, v_mla_q_norm, v_mla_w_uq, v_mla_kv_norm, v_mla_w_ukv, v_w_out, v_ln_mix_g, v_ln_mix_b, v_ffn_w_up, v_ffn_conv_w, v_ffn_conv_b, v_ffn_w_down, v_ln_ffn_g, v_ln_ffn_b):
    local = dict(ln_in_g=ln_in_g, ln_in_b=ln_in_b, w_in=w_in, conv_a_w=conv_a_w, conv_a_b=conv_a_b, ln_a_g=ln_a_g, ln_a_b=ln_a_b, qk_norm_q=qk_norm_q, qk_norm_k=qk_norm_k, sgu_ln_g=sgu_ln_g, sgu_ln_b=sgu_ln_b, sgu_w=sgu_w, sgu_b=sgu_b, mla_q_norm=mla_q_norm, mla_w_uq=mla_w_uq, mla_kv_norm=mla_kv_norm, mla_w_ukv=mla_w_ukv, w_out=w_out, ln_mix_g=ln_mix_g, ln_mix_b=ln_mix_b, ffn_w_up=ffn_w_up, ffn_conv_w=ffn_conv_w, ffn_conv_b=ffn_conv_b, ffn_w_down=ffn_w_down, ln_ffn_g=ln_ffn_g, ln_ffn_b=ln_ffn_b)
    mom = dict(ln_in_g=m_ln_in_g, ln_in_b=m_ln_in_b, w_in=m_w_in, conv_a_w=m_conv_a_w, conv_a_b=m_conv_a_b, ln_a_g=m_ln_a_g, ln_a_b=m_ln_a_b, qk_norm_q=m_qk_norm_q, qk_norm_k=m_qk_norm_k, sgu_ln_g=m_sgu_ln_g, sgu_ln_b=m_sgu_ln_b, sgu_w=m_sgu_w, sgu_b=m_sgu_b, mla_q_norm=m_mla_q_norm, mla_w_uq=m_mla_w_uq, mla_kv_norm=m_mla_kv_norm, mla_w_ukv=m_mla_w_ukv, w_out=m_w_out, ln_mix_g=m_ln_mix_g, ln_mix_b=m_ln_mix_b, ffn_w_up=m_ffn_w_up, ffn_conv_w=m_ffn_conv_w, ffn_conv_b=m_ffn_conv_b, ffn_w_down=m_ffn_w_down, ln_ffn_g=m_ln_ffn_g, ln_ffn_b=m_ln_ffn_b)
    var = dict(ln_in_g=v_ln_in_g, ln_in_b=v_ln_in_b, w_in=v_w_in, conv_a_w=v_conv_a_w, conv_a_b=v_conv_a_b, ln_a_g=v_ln_a_g, ln_a_b=v_ln_a_b, qk_norm_q=v_qk_norm_q, qk_norm_k=v_qk_norm_k, sgu_ln_g=v_sgu_ln_g, sgu_ln_b=v_sgu_ln_b, sgu_w=v_sgu_w, sgu_b=v_sgu_b, mla_q_norm=v_mla_q_norm, mla_w_uq=v_mla_w_uq, mla_kv_norm=v_mla_kv_norm, mla_w_ukv=v_mla_w_ukv, w_out=v_w_out, ln_mix_g=v_ln_mix_g, ln_mix_b=v_ln_mix_b, ffn_w_up=v_ffn_w_up, ffn_conv_w=v_ffn_conv_w, ffn_conv_b=v_ffn_conv_b, ffn_w_down=v_ffn_w_down, ln_ffn_g=v_ln_ffn_g, ln_ffn_b=v_ln_ffn_b)

    me = 4 * lax.axis_index("x") + 2 * lax.axis_index("y") + lax.axis_index("c")

    def own_slot(slots, block):
        return lax.dynamic_update_slice(slots, block[None], (me,) + (0,) * block.ndim)

    small_sharded = [n for n in SHARDED if n not in MATMUL_WEIGHTS]
    big_order = [(n, l) for l in range(DEPTH) for n in MATMUL_WEIGHTS]
    send_view = lambda n, l: (local[n].transpose(0, 2, 1)[l] if n in TRANSPOSED else local[n][l]).astype(COMM_DTYPE)
    srcs = [send_view(*big_order[0])] + [local[n] for n in small_sharded]
    srcs += [send_view(n, l) for (n, l) in big_order[1:]]
    n_first = 1 + len(small_sharded)
    groups = [list(range(n_first))] + [[n_first + j] for j in range(len(big_order) - 1)]
    g_sems, g_srcs, g_lands, tok0 = exchange_start(srcs, [True] * len(srcs), groups, "gather_start")
    tok0 = tok0[0, 0]
    pending = {key: gi for gi, key in enumerate(big_order)}

    def finish(gi, after):
        idx = groups[gi]
        _, lands = exchange_wait(g_sems[gi], [g_srcs[t] for t in idx], [g_lands[t] for t in idx], [True] * len(idx),
                                 after, f"gather_wait{gi}")
        return [own_slot(ld, srcs[t]) for ld, t in zip(lands, idx)]

    first = []

    opt_view = {n: tuple(a.transpose(0, 2, 1) for a in (local[n], mom[n], var[n])) for n in TRANSPOSED}

    def get_wts(after):
        for views in opt_view.values():
            after = after + sum(a[0, :8, :128] for a in views)
        first.extend(finish(0, after))
        wts = {n: local[n] for n in REPLICATED}
        for n, slots in zip(small_sharded, first[1:]):
            wts[n] = _unshard(slots, SHARDED[n])
        return wts

    def unshard_layer(slots, n):
        if SHARDED[n] == 1 or n in TRANSPOSED:
            return slots.reshape(-1, slots.shape[2])
        return slots.transpose(1, 0, 2).reshape(slots.shape[1], -1)

    def mat(l, n, after):
        gi = pending[(n, l)]
        slots = first[0] if gi == 0 else finish(gi, after)[0]
        w = unshard_layer(slots, n).astype(MXU_DTYPE)
        return _pad_w_in(w) if n == 'w_in' else w

    started = []

    def hook(key, grads):
        tensors = []
        for (n, l), g in grads.items():
            if l is None:
                tensors.append(((n, l), _shard_slots(g, SHARDED[n])))
            elif n in TRANSPOSED:
                tensors.append(((n, l), g.reshape(N_DEV, g.shape[0] // N_DEV, g.shape[1]).astype(COMM_DTYPE)))
            else:
                tensors.append(((n, l), _shard_slots(g[None], SHARDED[n])[:, 0].astype(COMM_DTYPE)))
        sems, s_srcs, s_lands, tok = exchange_start([a for _, a in tensors], [False] * len(tensors),
                                                    [list(range(len(tensors)))], "scatter_start_" + key)
        started.append((key, [k for k, _ in tensors], sems[0], s_srcs, s_lands))
        return tok[0, 0]

    loss, dx, grads = local_step(x[0], loss_target[0], (local['ln_in_g'] + tok0, local['ln_in_b']), get_wts, mat, hook)

    as2d = lambda a: a.reshape(-1, a.shape[-1]) if a.ndim > 1 else a.reshape(1, -1)
    small_g = [as2d(grads[n]) for n in REPLICATED] + [jnp.broadcast_to(loss, (8, 128))]
    p_sems, p_srcs, p_lands, p_tok = exchange_start(small_g, [True] * len(small_g), [list(range(len(small_g)))],
                                                    "gather_small_start")

    parts, res = {}, {}

    def finish_scatter(entries, after):
        for key, keys, sems, s_srcs, s_lands in entries:
            s_out, lands = exchange_wait(sems, s_srcs, s_lands, [False] * len(keys), after, "scatter_wait_" + key)
            for k, so, ld in zip(keys, s_out, lands):
                parts[k] = own_slot(ld, lax.dynamic_index_in_dim(so, me, 0, keepdims=False))

    def update(names_):
        for n in names_:
            p = [parts[(n, l)] for l in range(DEPTH)] if n in MATMUL_WEIGHTS else parts[(n, None)]
            if n in TRANSPOSED:
                res[n] = tuple(a.transpose(0, 2, 1) for a in adamw(p, *opt_view[n], "adamw_" + n))
            else:
                res[n] = adamw(p, local[n], mom[n], var[n], "adamw_" + n)

    early = ('ffn_w_up', 'ffn_w_down', 'w_out')
    finish_scatter([e for e in started if e[0] != "last"], p_tok)
    update(early)
    finish_scatter([e for e in started if e[0] == "last"], res[early[-1]][1])
    update([n for n in SHARDED if n not in early])
    updated = jnp.zeros((8, 128), F32) + sum(res[n][1][0, 0, 0] for n in SHARDED)
    p_own, p_lands = exchange_wait(p_sems[0], p_srcs, p_lands, [True] * len(small_g), updated, "gather_small_wait")
    small, loss_sum = adamw_replicated(p_lands[:-1], p_own[:-1], [as2d(local[n]) for n in REPLICATED],
                                       [as2d(mom[n]) for n in REPLICATED], [as2d(var[n]) for n in REPLICATED],
                                       p_lands[-1], p_own[-1])
    for n, quad in zip(REPLICATED, small):
        res[n] = tuple(a.reshape(local[n].shape) for a in quad)
    loss_total = loss_sum[0, 0]

    return (loss_total, dx[None], *[res[n][0] for n in WEIGHTS], *[res[n][1] for n in WEIGHTS],
            *[res[n][2] for n in WEIGHTS], *[res[n][3] for n in WEIGHTS])
```

```python
import functools
import math

import jax
import jax.numpy as jnp
from jax import lax
from jax.experimental import pallas as pl
from jax.experimental.pallas import tpu as pltpu

F32 = jnp.float32
MXU_DTYPE = jnp.bfloat16
COMM_DTYPE = jnp.bfloat16

N_DEV = 8
D_MODEL = 1024
DEPTH = 2
GRID_W = 64
GROUP_W = 256
HEAD_DIM = 64
CONV_A_WIDTH = 31
CONV_A_HALO = 16
GQA_HEADS = 4
GQA_KV_HEADS = 2
CHUNK = 128
SGU_GROUPS = 4
MLA_HEADS = 4
MLA_Q_LORA = 192
MLA_KV_LORA = 128
MLA_NOPE = 64
MLA_ROPE = 32
MLA_V = 64
MLA_DK_PAD = 128
ROPE_THETA = 10000.0
D_FF = 2816
DEEPNORM_ALPHA = (2 * DEPTH) ** 0.25
LN_EPS = 1e-5
RMS_EPS = 1e-6
D_IN_PROJ = 1888

ADAM_LR = 0.001
ADAM_B1 = 0.9
ADAM_B2 = 0.999
ADAM_EPS = 1e-08
ADAM_WD = 0.01
ADAM_STEP = 10

WEIGHTS = ['ln_in_g', 'ln_in_b', 'w_in', 'conv_a_w', 'conv_a_b', 'ln_a_g', 'ln_a_b', 'qk_norm_q', 'qk_norm_k',
           'sgu_ln_g', 'sgu_ln_b', 'sgu_w', 'sgu_b', 'mla_q_norm', 'mla_w_uq', 'mla_kv_norm', 'mla_w_ukv', 'w_out',
           'ln_mix_g', 'ln_mix_b', 'ffn_w_up', 'ffn_conv_w', 'ffn_conv_b', 'ffn_w_down', 'ln_ffn_g', 'ln_ffn_b']
SHARDED = {'w_in': 2, 'conv_a_w': 2, 'mla_w_uq': 2, 'mla_w_ukv': 2, 'w_out': 1, 'ffn_w_up': 2, 'ffn_conv_w': 2,
           'ffn_w_down': 1}
MATMUL_WEIGHTS = ('w_in', 'w_out', 'ffn_w_up', 'ffn_w_down')
TRANSPOSED = ('w_in', 'ffn_w_up')
REPLICATED = [n for n in WEIGHTS if n not in SHARDED]

ROW_TILE = 256
LN_TILE = 512
VMEM_LIMIT = 56 * 1024 * 1024


def _rawdot(a, b, ca, cb):
    return lax.dot_general(a.astype(MXU_DTYPE), b.astype(MXU_DTYPE), (((ca,), (cb,)), ((), ())),
                           preferred_element_type=F32)


@jax.custom_vjp
def mm_nn(a, b):
    return _rawdot(a, b, 1, 0)


def _mm_nn_fwd(a, b):
    return _rawdot(a, b, 1, 0), (a, b)


def _mm_nn_bwd(res, dy):
    a, b = res
    return _rawdot(dy, b, 1, 1), _rawdot(a, dy, 0, 0)


mm_nn.defvjp(_mm_nn_fwd, _mm_nn_bwd)


@jax.custom_vjp
def mm_nt(a, b):
    return _rawdot(a, b, 1, 1)


def _mm_nt_fwd(a, b):
    return _rawdot(a, b, 1, 1), (a, b)


def _mm_nt_bwd(res, dy):
    a, b = res
    return _rawdot(dy, b, 1, 0), _rawdot(dy, a, 0, 0)


mm_nt.defvjp(_mm_nt_fwd, _mm_nt_bwd)


def _pick_tile(d, cands):
    for c in cands:
        if d % c == 0:
            return c
    return d


def matmul(a, b, mode, out_dtype, name, b_rows=None, into=None, add=(), tm_max=1408, tn=None):
    a_list = list(a) if isinstance(a, (list, tuple)) else [a]
    b_list = list(b) if isinstance(b, (list, tuple)) else [b]
    n_p = len(a_list)
    rows_list = [b_rows] if not isinstance(a, (list, tuple)) else (list(b_rows) if b_rows is not None else [None] * n_p)
    b_start, b_size = zip(*[(0, bb.shape[0]) if r is None else r for bb, r in zip(b_list, rows_list)])
    a0, b0 = a_list[0], b_list[0]
    if mode == 'nn':
        (m, k), (k2, n) = a0.shape, (b_size[0], b0.shape[1])
    elif mode == 'nt':
        (m, k), (n, k2) = a0.shape, (b_size[0], b0.shape[1])
    else:
        (k, m), (k2, n) = a0.shape, (b_size[0], b0.shape[1])
    assert k == k2 and all(x.shape == a0.shape for x in a_list) and len(set(b_size)) == 1, (a0.shape, b0.shape, mode)
    tm = _pick_tile(m, tuple(c for c in (1024, 1408, 512, 256, 128) if c <= tm_max))
    tn = _pick_tile(n, (512, 1408, 256, 128)) if tn is None else tn
    assert n % tn == 0, (n, tn)
    tk = _pick_tile(k, (2816, 2048, 1024, 512, 256, 128))
    nk = k // tk
    ca = 0 if mode == 'tn' else 1
    cb = 1 if mode == 'nt' else 0
    b_blk = tn if mode == 'nt' else tk
    assert all(s % b_blk == 0 for s in b_start), (b_rows, b_blk)
    a_spec = pl.BlockSpec((tk, tm), lambda i, j, kk: (kk, i)) if mode == 'tn' else pl.BlockSpec((tm, tk), lambda i, j, kk: (i, kk))

    def b_spec(off):
        if mode == 'nt':
            return pl.BlockSpec((tn, tk), lambda i, j, kk: (j + off, kk))
        return pl.BlockSpec((tk, tn), lambda i, j, kk: (kk + off, j))

    in_specs, args, aliases = [], [], {}
    for x, y, s in zip(a_list, b_list, b_start):
        in_specs += [a_spec, b_spec(s // b_blk)]
        args += [x, y]
    in_specs += [pl.BlockSpec((tm, tn), lambda i, j, kk: (i, j))] * len(add)
    args += list(add)
    out_off, out_shape = 0, jax.ShapeDtypeStruct((m, n), out_dtype)
    if into is not None:
        buf, row = into
        assert row % tm == 0 and buf.shape[1] == n and buf.dtype == out_dtype, (buf.shape, row, tm)
        out_off, out_shape = row // tm, jax.ShapeDtypeStruct(buf.shape, buf.dtype)
        aliases = {len(args): 0}
        in_specs, args = in_specs + [pl.BlockSpec(memory_space=pl.ANY)], args + [buf]
    n_add = len(add)

    def body(*refs):
        o_ref, acc_ref = refs[-2:]
        kk = pl.program_id(2)

        @pl.when(kk == 0)
        def _():
            acc_ref[...] = jnp.zeros_like(acc_ref)

        for p in range(n_p):
            acc_ref[...] += _rawdot(refs[2 * p][...], refs[2 * p + 1][...], ca, cb)

        @pl.when(kk == nk - 1)
        def _():
            total = acc_ref[...]
            for r in refs[2 * n_p:2 * n_p + n_add]:
                total = total + r[...]
            o_ref[...] = total.astype(o_ref.dtype)

    return pl.pallas_call(
        body, grid=(m // tm, n // tn, nk), in_specs=in_specs,
        out_specs=pl.BlockSpec((tm, tn), lambda i, j, kk: (i + out_off, j)),
        out_shape=out_shape, input_output_aliases=aliases,
        scratch_shapes=[pltpu.VMEM((tm, tn), F32)],
        compiler_params=pltpu.CompilerParams(dimension_semantics=("parallel", "parallel", "arbitrary"),
                                             vmem_limit_bytes=VMEM_LIMIT),
        name=name)(*args)


class Op:
    def __init__(self, arr, block, imap, grad=False, acc=False, first=None, gdtype=F32, gshape=None, gimap=None):
        self.arr, self.block, self.imap = arr, block, imap
        self.grad, self.acc, self.first, self.gdtype = grad, acc, first, gdtype
        self.gshape = arr.shape if gshape is None else gshape
        self.gimap = imap if gimap is None else gimap


def _row_op(arr, tm, grad=False, gdtype=F32):
    return Op(arr, (tm, arr.shape[1]), lambda i: (i, 0), grad=grad, gdtype=gdtype)


def _par_op(arr, grad=False):
    nd = arr.ndim
    return Op(arr, arr.shape, lambda i: (0,) * nd, grad=grad, acc=True, first=lambda ids: ids[0] == 0)


def _load(ref):
    v = ref[...]
    return v.astype(F32) if jnp.issubdtype(v.dtype, jnp.floating) else v


def _store_heads(ref, val):
    rows = val.shape[0]
    if len(ref.shape) == 2:
        ref[...] = val.astype(ref.dtype)
    elif ref.shape[1] == rows:
        d = ref.shape[2]
        for h in range(ref.shape[0]):
            ref[h] = val[:, d * h:d * (h + 1)].astype(ref.dtype)
    else:
        d = ref.shape[1]
        assert ref.shape[2] == rows and d != rows, (ref.shape, val.shape)
        for h in range(ref.shape[0]):
            ref[h] = val[:, d * h:d * (h + 1)].T.astype(ref.dtype)


def _load_heads(ref, transposed=False):
    if len(ref.shape) == 2:
        return ref[...].astype(F32)
    parts = [ref[h].astype(F32) for h in range(ref.shape[0])]
    return jnp.concatenate([p.T for p in parts] if transposed else parts, axis=-1)


def stage_fwd(name, fn, ops, outs, grid):
    n_in = len(ops)

    def body(*refs):
        res = fn(*[_load(r) for r in refs[:n_in]])
        for r, o in zip(refs[n_in:], res):
            _store_heads(r, o)

    return pl.pallas_call(
        body, grid=grid, in_specs=[pl.BlockSpec(o.block, o.imap) for o in ops],
        out_specs=[pl.BlockSpec(b, im) for (_, _, b, im) in outs],
        out_shape=[jax.ShapeDtypeStruct(s, d) for (s, d, _, _) in outs],
        compiler_params=pltpu.CompilerParams(dimension_semantics=("parallel",) * len(grid),
                                             vmem_limit_bytes=VMEM_LIMIT),
        name=name)(*[o.arr for o in ops])


def stage_bwd(name, fn, ops, cts, grid, value_acc=False):
    n_in = len(ops)
    ct_flat = [(c + (False,))[:4] for group in cts if group is not None for c in group]
    n_ct = len(ct_flat)
    diff = [i for i, o in enumerate(ops) if o.grad]
    any_acc = value_acc or any(ops[i].acc for i in diff)
    ngrid = len(grid)

    def body(*refs):
        ids = [pl.program_id(a) for a in range(ngrid)]
        vals = [_load(r) for r in refs[:n_in]]
        ct_refs = refs[n_in:n_in + n_ct]
        out_refs = refs[n_in + n_ct:]

        def f(*dv):
            full = list(vals)
            for i, v in zip(diff, dv):
                full[i] = v
            return tuple(fn(*full))

        res, vjp = jax.vjp(f, *[vals[i] for i in diff])
        ct, pos = [], 0
        for group, r in zip(cts, res):
            if group is None:
                ct.append(jnp.ones_like(r))
            else:
                tot = None
                for _ in group:
                    c = _load_heads(ct_refs[pos], ct_flat[pos][3])
                    tot = c if tot is None else tot + c
                    pos += 1
                ct.append(tot)
        grads = vjp(tuple(ct))
        for i, g, r in zip(diff, grads, out_refs):
            if ops[i].acc:
                @pl.when(ops[i].first(ids))
                def _(r=r):
                    r[...] = jnp.zeros_like(r)

                r[...] += g.astype(r.dtype)
            else:
                r[...] = g.astype(r.dtype)
        if value_acc:
            r = out_refs[len(diff)]

            @pl.when(ids[0] == 0)
            def _():
                r[...] = jnp.zeros_like(r)

            r[...] += res[0]

    in_specs = [pl.BlockSpec(o.block, o.imap) for o in ops] + [pl.BlockSpec(b, im) for (_, b, im, _) in ct_flat]
    out_specs = [pl.BlockSpec(ops[i].block, ops[i].gimap) for i in diff]
    out_shape = [jax.ShapeDtypeStruct(ops[i].gshape, ops[i].gdtype) for i in diff]
    if value_acc:
        out_specs.append(pl.BlockSpec((1, 1), lambda *ids: (0, 0)))
        out_shape.append(jax.ShapeDtypeStruct((1, 1), F32))
    sem = ("arbitrary",) * ngrid if any_acc else ("parallel",) * ngrid
    return pl.pallas_call(
        body, grid=grid, in_specs=in_specs, out_specs=out_specs, out_shape=out_shape,
        compiler_params=pltpu.CompilerParams(dimension_semantics=sem, vmem_limit_bytes=VMEM_LIMIT),
        name=name)(*[o.arr for o in ops], *[a for (a, _, _, _) in ct_flat])


def _sigmoid(x):
    return 1.0 / (1.0 + jnp.exp(-x))


def _silu(x):
    return x * _sigmoid(x)


def _gelu_tanh(x):
    return 0.5 * x * (1.0 + jnp.tanh(math.sqrt(2.0 / math.pi) * (x + 0.044715 * (x * x * x))))


def _ln(x, g, b):
    mu = jnp.mean(x, axis=-1, keepdims=True)
    xc = x - mu
    var = jnp.mean(xc * xc, axis=-1, keepdims=True)
    return xc * lax.rsqrt(var + LN_EPS) * g + b


def _rms(x, g):
    ms = jnp.mean(x * x, axis=-1, keepdims=True)
    return x * lax.rsqrt(ms + RMS_EPS) * g


def _swap_halves(x, half):
    width = x.shape[-1]
    lane = lax.broadcasted_iota(jnp.int32, x.shape, 1)
    return jnp.where(lane % (2 * half) < half, pltpu.roll(x, width - half, 1), pltpu.roll(x, half, 1))


def _make_swap(half):
    @jax.custom_vjp
    def swap(x):
        return _swap_halves(x, half)

    swap.defvjp(lambda x: (_swap_halves(x, half), None), lambda _, dy: (_swap_halves(dy, half),))
    return swap


_swap16, _swap8 = _make_swap(16), _make_swap(8)


def _rope(x, cos, sin_signed, swap):
    return x * cos + swap(x) * sin_signed


def _dot_f32(a, b):
    return jnp.dot(a, b, preferred_element_type=F32, precision=lax.Precision.HIGHEST)


def fn_ln(x, g, b):
    return (_ln(x, g, b),)


def _twice(fn):
    def f(*a):
        (y,) = fn(*a)
        return y, y
    return f


PROJ_W = 2048
P_A, P_Q, P_K, P_V, P_C, P_CQ, P_CKV, P_KR = 0, 512, 768, 896, 1024, 1536, 1792, 1920
CQ_PAD = 256
_CQ_END = P_CQ + MLA_Q_LORA


def _pad_w_in(wt):
    z = lambda n: jnp.zeros((n, wt.shape[1]), wt.dtype)
    return jnp.concatenate([wt[:_CQ_END], z(P_CKV - _CQ_END), wt[_CQ_END:], z(PROJ_W - P_KR - MLA_ROPE)], axis=0)


def _unpad_w_in(gt):
    return jnp.concatenate([gt[:_CQ_END], gt[P_CKV:P_KR + MLA_ROPE]], axis=0)


def fn_pre(proj, tab_q, tab_d, seg, place, qng, kng, sg, sb, sw, sbt, mqn, wuq, mkvn, wukv):
    tm = proj.shape[0]
    aglu = proj[:, P_A:P_A + GROUP_W] * _sigmoid(proj[:, P_A + GROUP_W:P_Q])
    b_q, b_k, b_v = proj[:, P_Q:P_K], proj[:, P_K:P_V], proj[:, P_V:P_C]
    cos_q, sin_q = tab_q[:, :GROUP_W], tab_q[:, GROUP_W:]
    q = b_q * lax.rsqrt(_dot_f32(b_q * b_q, seg) + RMS_EPS) * qng
    q = _rope(q, cos_q, sin_q, _swap16)
    k = b_k * lax.rsqrt(_dot_f32(b_k * b_k, seg[:128, :128]) + RMS_EPS) * kng
    k = _rope(k, cos_q[:, :128], sin_q[:, :128], _swap16)
    c = _gelu_tanh(proj[:, P_C:P_CQ])
    u, sv = c[:, :GROUP_W], _ln(c[:, GROUP_W:], sg, sb)
    group = lax.broadcasted_iota(jnp.int32, (CHUNK, GROUP_W), 1) // HEAD_DIM
    rows = []
    for n in range(tm // CHUNK):
        svn = sv[CHUNK * n:CHUNK * (n + 1)]
        acc = jnp.zeros((CHUNK, GROUP_W), F32)
        for g in range(SGU_GROUPS):
            acc = acc + jnp.where(group == g, mm_nn(sw[CHUNK * g:CHUNK * (g + 1)], svn) + sbt[:, g:g + 1], 0.0)
        rows.append(acc)
    o_c = u * jnp.concatenate(rows, axis=0)
    d_cq, d_ckv, d_kr = proj[:, P_CQ:P_CKV], proj[:, P_CKV:P_KR], proj[:, P_KR:PROJ_W]
    cqn = d_cq * lax.rsqrt(jnp.sum(d_cq * d_cq, axis=-1, keepdims=True) * (1.0 / MLA_Q_LORA) + RMS_EPS) * mqn
    cos_d = jnp.concatenate([tab_d[:, :MLA_DK_PAD]] * MLA_HEADS, axis=-1)
    sin_d = jnp.concatenate([tab_d[:, MLA_DK_PAD:]] * MLA_HEADS, axis=-1)
    qf = _rope(mm_nn(cqn, wuq), cos_d, sin_d, _swap8)
    kvd = mm_nn(_rms(d_ckv, mkvn), wukv)
    kf = _rope(kvd[:, :MLA_HEADS * MLA_DK_PAD] + _dot_f32(d_kr, place), cos_d, sin_d, _swap8)
    return aglu, q, k, b_v, o_c, qf, kf, kvd[:, MLA_HEADS * MLA_DK_PAD:]


def fn_aconv(win, w, b, g, beta):
    tm = win.shape[0] - 2 * CONV_A_HALO
    off = CONV_A_HALO - CONV_A_WIDTH // 2
    acc = None
    for r in range(8):
        rolled = win if r == 0 else _roll_rows(win, -r)
        for kk in range(CONV_A_WIDTH):
            if (off + kk) % 8 == r:
                base = off + kk - r
                term = rolled[base:base + tm] * w[kk:kk + 1, :]
                acc = term if acc is None else acc + term
    return (_silu(_ln(acc + b, g, beta)),)


def fn_resln(h, r, g, b):
    return (_ln(DEEPNORM_ALPHA * h + r, g, b),)


@functools.partial(jax.custom_vjp, nondiff_argnums=(1,))
def _roll_rows(x, shift):
    return pltpu.roll(x, shift % x.shape[0], 0)


_roll_rows.defvjp(lambda x, shift: (pltpu.roll(x, shift % x.shape[0], 0), None),
                  lambda shift, _, dy: (pltpu.roll(dy, (-shift) % dy.shape[0], 0),))


def _shift_down(x):
    row = lax.broadcasted_iota(jnp.int32, x.shape, 0)
    return jnp.where(row == 0, 0.0, _roll_rows(x, 1))


def _shift_up(x):
    row = lax.broadcasted_iota(jnp.int32, x.shape, 0)
    return jnp.where(row == x.shape[0] - 1, 0.0, _roll_rows(x, -1))


def fn_ffnconv(u1, u2, w1, w2, b1, b2):
    c1 = _shift_down(u1) * w1[0:1] + u1 * w1[1:2] + _shift_up(u1) * w1[2:3] + b1
    c2 = _shift_down(u2) * w2[0:1] + u2 * w2[1:2] + _shift_up(u2) * w2[2:3] + b2
    return (_silu(c1) * c2,)


def fn_final(h, r, t, g, b):
    y = _ln(DEEPNORM_ALPHA * h + r, g, b)
    err = (y - t) * (y - t)
    return (0.5 * jnp.sum(jnp.mean(err, axis=-1, keepdims=True), axis=0, keepdims=True),)


def _rope_tables(seq):
    n_rows = seq // GRID_W
    lane128 = jnp.arange(128)

    def tile_tables(j, rotated, half):
        inv = ROPE_THETA ** (-(j % half).astype(F32) / half)
        by_row, by_col = rotated & (j < 2 * half), rotated & (j >= 2 * half)
        sign = jnp.where(j % (2 * half) < half, -1.0, 1.0)
        ar = jnp.arange(n_rows, dtype=F32)[:, None] * inv[None, :]
        ac = jnp.arange(GRID_W, dtype=F32)[:, None] * inv[None, :]
        grid = lambda r, c: (jnp.where(by_row, r, 0.0)[:, None, :] + jnp.where(by_col, c, 0.0)[None, :, :])
        cos = grid(jnp.cos(ar), jnp.cos(ac)) + jnp.where(rotated, 0.0, 1.0)
        sin = grid(sign * jnp.sin(ar), sign * jnp.sin(ac))
        return cos.reshape(seq, 128), sin.reshape(seq, 128)

    cos_b, sin_b = tile_tables(lane128 % HEAD_DIM, lane128 >= 0, HEAD_DIM // 4)
    tab_q = jnp.concatenate([cos_b] * (GROUP_W // 128) + [sin_b] * (GROUP_W // 128), axis=-1)
    tab_d = jnp.concatenate(tile_tables(lane128 - MLA_NOPE, (lane128 >= MLA_NOPE) & (lane128 < MLA_NOPE + MLA_ROPE),
                                        MLA_ROPE // 4), axis=-1)
    lane = jnp.arange(GROUP_W)
    seg = jnp.where(lane[:, None] // HEAD_DIM == lane[None, :] // HEAD_DIM, 1.0 / HEAD_DIM, 0.0).astype(F32)
    src, dst = jnp.arange(128)[:, None], jnp.arange(MLA_HEADS * MLA_DK_PAD)[None, :]
    place = jnp.where((src < MLA_ROPE) & (dst % MLA_DK_PAD == MLA_NOPE + src), 1.0, 0.0).astype(F32)
    return tab_q, tab_d, seg, place


def _pre_ops(proj, tabs, kp, grad):
    tm = ROW_TILE
    ops = [_row_op(proj, tm, grad=grad, gdtype=MXU_DTYPE), _row_op(tabs[0], tm), _row_op(tabs[1], tm),
           _par_op(tabs[2]), _par_op(tabs[3])]
    ops += [_par_op(kp[n], grad=grad) for n in ('qng', 'kng', 'sg', 'sb', 'sw', 'sbt', 'mqn', 'wuq', 'mkvn', 'wukv')]
    return ops


PRE_OUTS = ((0, GROUP_W), (GQA_HEADS, HEAD_DIM), (GQA_KV_HEADS, HEAD_DIM), (GQA_KV_HEADS, HEAD_DIM), (0, GROUP_W),
            (MLA_HEADS, MLA_DK_PAD), (MLA_HEADS, MLA_DK_PAD), (MLA_HEADS, MLA_V))


def _pre_out_specs(seq, tm):
    specs = []
    for heads, w in PRE_OUTS:
        if heads:
            specs.append(((heads, seq, w), (heads, tm, w), lambda i: (0, i, 0)))
        else:
            specs.append(((seq, w), (tm, w), lambda i: (i, 0)))
    return specs


PRE_KV = (2, 3, 6, 7)


def _transposed_spec(heads, w, seq, tm):
    return (heads, w, seq), (heads, w, tm), lambda i: (0, 0, i)


def pre_fwd(proj, tabs, kp, tag):
    seq = proj.shape[0]
    tm = ROW_TILE
    dts = (F32,) + (MXU_DTYPE,) * 7
    outs = [(shape, dt, block, imap) for (shape, block, imap), dt in zip(_pre_out_specs(seq, tm), dts)]
    return stage_fwd("pre_fwd" + tag, fn_pre, _pre_ops(proj, tabs, kp, False), outs, (seq // tm,))


def pre_bwd(proj, tabs, kp, cts, tag):
    seq = proj.shape[0]
    tm = ROW_TILE
    ct = []
    for j, (c, (_, block, imap)) in enumerate(zip(cts, _pre_out_specs(seq, tm))):
        if isinstance(c, tuple):
            ct.append([c])
        elif j in PRE_KV:
            ct.append([(c,) + _transposed_spec(*PRE_OUTS[j], seq, tm)[1:] + (True,)])
        else:
            ct.append([(c, block, imap)])
    return stage_bwd("pre_bwd" + tag, fn_pre, _pre_ops(proj, tabs, kp, True), ct, (seq // tm,))


def _aconv_ops(kp, grad):
    return [_par_op(kp[n], grad=grad) for n in ('caw', 'cab', 'lag', 'lab')]


def aconv_fwd(aglu_pad, kp, tag):
    seq = aglu_pad.shape[0] - 2 * CONV_A_HALO
    tm = ROW_TILE
    n_par = 4

    def body(x_ref, *refs):
        i = pl.program_id(0)
        win = x_ref[pl.ds(pl.multiple_of(i * tm, tm), tm + 2 * CONV_A_HALO), :]
        (o,) = fn_aconv(win, *[_load(r) for r in refs[:n_par]])
        refs[n_par][...] = o.astype(refs[n_par].dtype)

    pars = _aconv_ops(kp, False)
    return pl.pallas_call(
        body, grid=(seq // tm,),
        in_specs=[pl.BlockSpec(aglu_pad.shape, lambda i: (0, 0))] + [pl.BlockSpec(o.block, o.imap) for o in pars],
        out_specs=pl.BlockSpec((tm, GROUP_W), lambda i: (i, 0)),
        out_shape=jax.ShapeDtypeStruct((seq, GROUP_W), MXU_DTYPE),
        compiler_params=pltpu.CompilerParams(dimension_semantics=("parallel",), vmem_limit_bytes=VMEM_LIMIT),
        name="aconv_fwd" + tag)(aglu_pad, *[o.arr for o in pars])


def aconv_bwd(aglu_pad, kp, d_oa, tag):
    seq = aglu_pad.shape[0] - 2 * CONV_A_HALO
    tm = ROW_TILE
    n_par = 4

    def body(x_ref, *refs):
        i = pl.program_id(0)
        rows = pl.ds(pl.multiple_of(i * tm, tm), tm + 2 * CONV_A_HALO)
        pars = [_load(r) for r in refs[:n_par]]
        ct = refs[n_par][...].astype(F32)
        outs = refs[n_par + 1:]
        _, vjp = jax.vjp(lambda *a: fn_aconv(*a), x_ref[rows, :], *pars)
        grads = vjp((ct,))

        @pl.when(i == 0)
        def _():
            for r in outs:
                r[...] = jnp.zeros_like(r)

        outs[0][rows, :] += grads[0]
        for r, g in zip(outs[1:], grads[1:]):
            r[...] += g

    pars = _aconv_ops(kp, True)
    whole = pl.BlockSpec(aglu_pad.shape, lambda i: (0, 0))
    par_specs = [pl.BlockSpec(o.block, o.imap) for o in pars]
    return pl.pallas_call(
        body, grid=(seq // tm,),
        in_specs=[whole] + par_specs + [pl.BlockSpec((tm, GROUP_W), lambda i: (i, 0))],
        out_specs=[whole] + par_specs,
        out_shape=[jax.ShapeDtypeStruct(aglu_pad.shape, F32)] + [jax.ShapeDtypeStruct(o.arr.shape, F32) for o in pars],
        compiler_params=pltpu.CompilerParams(dimension_semantics=("arbitrary",), vmem_limit_bytes=VMEM_LIMIT),
        name="aconv_bwd" + tag)(aglu_pad, *[o.arr for o in pars], d_oa)


ATTN_TQ_FWD = 512
ATTN_TQ = 256
ATTN_TK = 512


def attn_fwd(q3, k3, v3t, scale, tag):
    heads, seq, dk = q3.shape
    group = heads // k3.shape[0]
    kv_per_pair = 2 // group
    dv = v3t.shape[1]
    tq, tk = min(ATTN_TQ_FWD, seq), min(ATTN_TK, seq)
    n_chunks = seq // tk
    log2e = math.log2(math.e)

    def one_head(q, k_ref, vt_ref):
        scores = lambda c: _rawdot(k_ref[pl.ds(c * tk, tk), :], q, 1, 1)
        m, l, acc = jnp.full((1, tq), -jnp.inf, F32), jnp.zeros((1, tq), F32), jnp.zeros((dv, tq), F32)
        s_next = scores(0)
        for c in range(n_chunks):
            s_cur, s_next = s_next, (scores(c + 1) if c + 1 < n_chunks else None)
            t = s_cur * (scale * log2e)
            m_new = jnp.maximum(m, jnp.max(t, axis=0, keepdims=True))
            alpha = jnp.exp2(m - m_new)
            p = jnp.exp2(t - m_new)
            l = alpha * l + jnp.sum(p, axis=0, keepdims=True)
            acc = alpha * acc + _rawdot(vt_ref[:, c * tk:(c + 1) * tk], p, 1, 0)
            m = m_new
        return (acc * (1.0 / l)).T, (m * (1.0 / log2e) + jnp.log(l)).T

    def body(q_ref, k_ref, v_ref, o_ref, lse_ref):
        outs = []
        for h in range(2):
            o, lse = one_head(q_ref[h], k_ref.at[h // group], v_ref.at[h // group])
            lse_ref[h] = lse
            outs.append(o)
        o_ref[...] = jnp.concatenate(outs, axis=-1)

    return pl.pallas_call(
        body, grid=(heads // 2, seq // tq),
        in_specs=[pl.BlockSpec((2, tq, dk), lambda j, i: (j, i, 0)),
                  pl.BlockSpec((kv_per_pair, seq, dk), lambda j, i: (j, 0, 0)),
                  pl.BlockSpec((kv_per_pair, dv, seq), lambda j, i: (j, 0, 0))],
        out_specs=[pl.BlockSpec((tq, 2 * dv), lambda j, i: (i, j)),
                   pl.BlockSpec((2, tq, 1), lambda j, i: (j, i, 0))],
        out_shape=[jax.ShapeDtypeStruct((seq, heads * dv), F32), jax.ShapeDtypeStruct((heads, seq, 1), F32)],
        compiler_params=pltpu.CompilerParams(dimension_semantics=("parallel", "parallel"),
                                             vmem_limit_bytes=VMEM_LIMIT),
        name="attn_fwd" + tag)(q3, k3, v3t)


def attn_bwd(q3, k3, v3, o, lse3, do_all, do_col, scale, tag):
    heads, seq, dk = q3.shape
    group = heads // k3.shape[0]
    kv_per_pair = 2 // group
    dv = v3.shape[2]
    tq, tk = min(ATTN_TQ, seq), min(ATTN_TK, seq)
    n_chunks = seq // tk
    log2e = math.log2(math.e)

    def one_head(q, do, o_h, lse, k_ref, v_ref, dk_ref, dv_ref):
        dob = do.astype(MXU_DTYPE)
        do_t, q_t = do.T.astype(MXU_DTYPE), q.astype(F32).T.astype(MXU_DTYPE)
        delta = jnp.sum(do * o_h, axis=-1, keepdims=True)
        lse2 = lse * log2e
        rows = lambda c: pl.ds(c * tk, tk)
        products = lambda c: (_rawdot(q, k_ref[rows(c), :], 1, 1), _rawdot(dob, v_ref[rows(c), :], 1, 1))
        dq = jnp.zeros((tq, dk), F32)
        nxt = products(0)
        for c in range(n_chunks):
            (s_cur, dp_cur), nxt = nxt, (products(c + 1) if c + 1 < n_chunks else None)
            p = jnp.exp2(s_cur * (scale * log2e) - lse2)
            ds = (p * ((dp_cur - delta) * scale)).astype(MXU_DTYPE)
            dv_ref[:, c * tk:(c + 1) * tk] += _rawdot(do_t, p, 1, 0)
            dk_ref[:, c * tk:(c + 1) * tk] += _rawdot(q_t, ds, 1, 0)
            dq = dq + _rawdot(ds, k_ref[rows(c), :], 1, 0)
        return dq

    def body(q_ref, k_ref, v_ref, o_ref, lse_ref, do_ref, dq_ref, dk_ref, dv_ref):
        @pl.when(pl.program_id(1) == 0)
        def _():
            dk_ref[...] = jnp.zeros_like(dk_ref)
            dv_ref[...] = jnp.zeros_like(dv_ref)

        do_pair, o_pair = do_ref[...], o_ref[...]
        for h in range(2):
            kv = h // group
            dq_ref[h] = one_head(q_ref[h], do_pair[:, dv * h:dv * (h + 1)], o_pair[:, dv * h:dv * (h + 1)],
                                 lse_ref[h], k_ref.at[kv], v_ref.at[kv], dk_ref.at[kv], dv_ref.at[kv])

    qspec = lambda d: pl.BlockSpec((2, tq, d), lambda j, i: (j, i, 0))
    kvspec = lambda d: pl.BlockSpec((kv_per_pair, seq, d), lambda j, i: (j, 0, 0))
    kvt_spec = lambda d: pl.BlockSpec((kv_per_pair, d, seq), lambda j, i: (j, 0, 0))
    kvt_shape = lambda a: jax.ShapeDtypeStruct((a.shape[0], a.shape[2], a.shape[1]), F32)
    return pl.pallas_call(
        body, grid=(heads // 2, seq // tq),
        in_specs=[qspec(dk), kvspec(dk), kvspec(dv), pl.BlockSpec((tq, 2 * dv), lambda j, i: (i, j)), qspec(1),
                  pl.BlockSpec((tq, 2 * dv), lambda j, i: (i, do_col + j))],
        out_specs=[qspec(dk), kvt_spec(dk), kvt_spec(dv)],
        out_shape=[jax.ShapeDtypeStruct(q3.shape, F32), kvt_shape(k3), kvt_shape(v3)],
        compiler_params=pltpu.CompilerParams(dimension_semantics=("parallel", "arbitrary"),
                                             vmem_limit_bytes=VMEM_LIMIT),
        name="attn_bwd" + tag)(q3, k3, v3, o, lse3, do_all)


def resln_fwd(h, r, g, b, tag):
    seq, d = h.shape
    tm = min(LN_TILE, seq)
    ops = [_row_op(h, tm), _row_op(r, tm), _par_op(g), _par_op(b)]
    outs = [((seq, d), dt, (tm, d), lambda i: (i, 0)) for dt in (F32, MXU_DTYPE)]
    return stage_fwd("resln_fwd" + tag, _twice(fn_resln), ops, outs, (seq // tm,))


def resln_bwd(h, r, g, b, dys, tag):
    seq, d = h.shape
    tm = min(LN_TILE, seq)
    ops = [_row_op(h, tm, grad=True), _row_op(r, tm, grad=True, gdtype=MXU_DTYPE), _par_op(g, grad=True),
           _par_op(b, grad=True)]
    ct = [[(dy, (tm, d), lambda i: (i, 0)) for dy in dys]]
    return stage_bwd("resln_bwd" + tag, fn_resln, ops, ct, (seq // tm,))


def _ffnconv_ops(up1, up2, w, b, grad):
    seq = up1.shape[0]
    nblk = D_FF // 128
    lo, hi = (lambda j: (0, j)), (lambda j: (0, j + nblk))
    half = lambda a: dict(gshape=(a.shape[0], D_FF), gimap=lo)
    return [Op(up1, (seq, 128), lo, grad=grad, gdtype=MXU_DTYPE), Op(up2, (seq, 128), lo, grad=grad, gdtype=MXU_DTYPE),
            Op(w, (3, 128), lo, grad=grad, **half(w)), Op(w, (3, 128), hi, grad=grad, **half(w)),
            Op(b, (1, 128), lo, grad=grad, **half(b)), Op(b, (1, 128), hi, grad=grad, **half(b))]


def ffnconv_fwd(up1, up2, w, b, tag):
    seq = up1.shape[0]
    outs = [((seq, D_FF), MXU_DTYPE, (seq, 128), lambda j: (0, j))]
    return stage_fwd("ffnconv_fwd" + tag, fn_ffnconv, _ffnconv_ops(up1, up2, w, b, False), outs, (D_FF // 128,))[0]


def ffnconv_bwd(up1, up2, w, b, dact, tag):
    seq = up1.shape[0]
    ct = [[(dact, (seq, 128), lambda j: (0, j))]]
    du1, du2, dw1, dw2, db1, db2 = stage_bwd("ffnconv_bwd" + tag, fn_ffnconv, _ffnconv_ops(up1, up2, w, b, True),
                                             ct, (D_FF // 128,))
    cat = lambda a, b_: jnp.concatenate([a, b_], axis=-1)
    return du1, du2, cat(dw1, dw2), cat(db1, db2)


def final_bwd(h, r, t, g, b, tag):
    seq, d = h.shape
    tm = min(LN_TILE, seq)
    ops = [_row_op(h, tm, grad=True), _row_op(r, tm, grad=True, gdtype=MXU_DTYPE), _row_op(t, tm),
           _par_op(g, grad=True), _par_op(b, grad=True)]
    return stage_bwd("final_bwd" + tag, fn_final, ops, [None], (seq // tm,), value_acc=True)


def _layer_params(wts, l):
    row = lambda a: a.reshape(1, -1)
    wuq = wts['mla_w_uq'][l].reshape(MLA_Q_LORA, MLA_HEADS, MLA_NOPE + MLA_ROPE)
    wuq = jnp.pad(wuq, ((0, CQ_PAD - MLA_Q_LORA), (0, 0), (0, MLA_DK_PAD - MLA_NOPE - MLA_ROPE)))
    wukv = wts['mla_w_ukv'][l].reshape(MLA_KV_LORA, MLA_HEADS, MLA_NOPE + MLA_V)
    wuk = jnp.pad(wukv[:, :, :MLA_NOPE], ((0, 0), (0, 0), (0, MLA_DK_PAD - MLA_NOPE)))
    return dict(
        qng=jnp.tile(row(wts['qk_norm_q'][l]), (1, GQA_HEADS)), kng=jnp.tile(row(wts['qk_norm_k'][l]), (1, GQA_KV_HEADS)),
        sg=row(wts['sgu_ln_g'][l]), sb=row(wts['sgu_ln_b'][l]),
        sw=wts['sgu_w'][l].reshape(SGU_GROUPS * CHUNK, CHUNK), sbt=wts['sgu_b'][l].T,
        mqn=jnp.pad(row(wts['mla_q_norm'][l]), ((0, 0), (0, CQ_PAD - MLA_Q_LORA))),
        wuq=wuq.reshape(CQ_PAD, MLA_HEADS * MLA_DK_PAD), mkvn=row(wts['mla_kv_norm'][l]),
        wukv=jnp.concatenate([wuk.reshape(MLA_KV_LORA, -1), wukv[:, :, MLA_NOPE:].reshape(MLA_KV_LORA, -1)], axis=1),
        caw=wts['conv_a_w'][l], cab=row(wts['conv_a_b'][l]), lag=row(wts['ln_a_g'][l]), lab=row(wts['ln_a_b'][l]),
        lmg=row(wts['ln_mix_g'][l]), lmb=row(wts['ln_mix_b'][l]),
        fcw=wts['ffn_conv_w'][l], fcb=row(wts['ffn_conv_b'][l]),
        lfg=row(wts['ln_ffn_g'][l]), lfb=row(wts['ln_ffn_b'][l]))


def _to_heads(a, heads):
    seq = a.shape[0]
    return a.reshape(seq, heads, -1).transpose(1, 0, 2)


def _from_heads(a3):
    return a3.transpose(1, 0, 2).reshape(a3.shape[1], -1)


def local_step(x, target, ln_in, get_wts, mat, hook):
    seq = x.shape[0]
    tm = min(LN_TILE, seq)
    tabs = _rope_tables(seq)
    scale_b = HEAD_DIM ** -0.5
    scale_d = (MLA_NOPE + MLA_ROPE) ** -0.5
    ln_in_g, ln_in_b = ln_in[0].reshape(1, -1), ln_in[1].reshape(1, -1)

    h, h_m = stage_fwd("ln_in_fwd", _twice(fn_ln), [_row_op(x, tm), _par_op(ln_in_g), _par_op(ln_in_b)],
                       [((seq, D_MODEL), dt, (tm, D_MODEL), lambda i: (i, 0)) for dt in (F32, MXU_DTYPE)],
                       (seq // tm,))
    wts = get_wts(h_m[:8, :128].astype(F32) + tabs[0][:8, :128] + tabs[1][:8, :128])
    saved = []
    for l in range(DEPTH):
        tag = f"_l{l}"
        kp, unprep = jax.vjp(lambda w: _layer_params(w, l), wts)
        m = {'w_in': mat(l, 'w_in', h_m)}
        proj = matmul(h_m, m['w_in'], 'nt', F32, "mm_proj" + tag, tn=PROJ_W, tm_max=512)
        aglu, q3, k3, v3, o_c, qd3, kd3, vd3 = pre_fwd(proj, tabs, kp, tag)
        aglu_pad = jnp.pad(aglu, ((CONV_A_HALO, CONV_A_HALO), (0, 0)))
        o_a = aconv_fwd(aglu_pad, kp, tag)
        swap = lambda a: a.transpose(0, 2, 1)
        o_b3, lse_b3 = attn_fwd(q3, k3, swap(v3), scale_b, "_b" + tag)
        o_d3, lse_d3 = attn_fwd(qd3, kd3, swap(vd3), scale_d, "_d" + tag)
        o_cat = jnp.concatenate([o_a, o_b3.astype(MXU_DTYPE), o_c, o_d3.astype(MXU_DTYPE)], axis=-1)
        m['w_out'] = mat(l, 'w_out', o_cat)
        mix = matmul(o_cat, m['w_out'], 'nn', F32, "mm_mix" + tag, tn=D_MODEL, tm_max=512)
        h1, h1_m = resln_fwd(h, mix, kp['lmg'], kp['lmb'], "_mix" + tag)
        m['ffn_w_up'] = mat(l, 'ffn_w_up', h1_m)
        wide = dict(tn=D_FF, tm_max=512)
        up1 = matmul(h1_m, m['ffn_w_up'], 'nt', F32, "mm_up1" + tag, b_rows=(0, D_FF), **wide)
        up2 = matmul(h1_m, m['ffn_w_up'], 'nt', F32, "mm_up2" + tag, b_rows=(D_FF, D_FF), **wide)
        act = ffnconv_fwd(up1, up2, kp['fcw'], kp['fcb'], tag)
        m['ffn_w_down'] = mat(l, 'ffn_w_down', act)
        f = matmul(act, m['ffn_w_down'], 'nn', F32, "mm_down" + tag, tn=D_MODEL, tm_max=512)
        saved.append(dict(kp=kp, unprep=unprep, m=m, h=h, h_m=h_m, h1_m=h1_m, proj=proj, o_b3=o_b3, lse_b3=lse_b3,
                          o_d3=o_d3, lse_d3=lse_d3, aglu_pad=aglu_pad, q3=q3, k3=k3, v3=v3, qd3=qd3,
                          kd3=kd3, vd3=vd3, o_cat=o_cat, mix=mix, h1=h1, up1=up1, up2=up2, act=act, f=f))
        if l + 1 < DEPTH:
            h, h_m = resln_fwd(h1, f, kp['lfg'], kp['lfb'], "_ffn" + tag)

    after = lambda a, tok: a if tok is None else a + tok
    small_acc = None
    dh_parts = None
    loss = None
    tok = None
    g_mix = None
    for l in reversed(range(DEPTH)):
        tag = f"_l{l}"
        s = saved[l]
        kp, m = s['kp'], s['m']
        dkp = {}
        lfg = after(kp['lfg'], tok)
        if l == DEPTH - 1:
            dh1_a, df, dkp['lfg'], dkp['lfb'], loss = final_bwd(s['h1'], s['f'], target, lfg, kp['lfb'], tag)
        else:
            dh1_a, df, dkp['lfg'], dkp['lfb'] = resln_bwd(s['h1'], s['f'], lfg, kp['lfb'], dh_parts, "_ffn" + tag)
        g_down = matmul(s['act'], df, 'tn', COMM_DTYPE, "mm_gdown" + tag, tn=D_MODEL)
        dact = matmul(df, m['ffn_w_down'], 'nt', F32, "mm_dact" + tag, tn=D_FF, tm_max=512)
        dup1, dup2, dkp['fcw'], dkp['fcb'] = ffnconv_bwd(s['up1'], s['up2'], kp['fcw'], kp['fcb'], dact, tag)
        dh1 = matmul([dup1, dup2], [m['ffn_w_up']] * 2, 'nn', F32, "mm_dh1" + tag,
                     b_rows=[(0, D_FF), (D_FF, D_FF)], add=[dh1_a], tn=D_MODEL, tm_max=512)
        g_up = matmul(dup1, s['h1_m'], 'tn', COMM_DTYPE, "mm_gup1" + tag, tn=D_MODEL,
                      into=(lax.empty((2 * D_FF, D_MODEL), COMM_DTYPE), 0))
        g_up = matmul(dup2, s['h1_m'], 'tn', COMM_DTYPE, "mm_gup2" + tag, tn=D_MODEL, into=(g_up, D_FF))
        tok = hook(f"ffn{l}", {('ffn_w_down', l): g_down, ('ffn_w_up', l): g_up})
        dh_a, dmix, dkp['lmg'], dkp['lmb'] = resln_bwd(s['h'], s['mix'], after(kp['lmg'], tok), kp['lmb'],
                                                       [dh1], "_mix" + tag)
        g_out = matmul(s['o_cat'], dmix, 'tn', COMM_DTYPE, "mm_gout" + tag, tn=D_MODEL)
        w_out = m['w_out']
        if l == 0:
            w_out = w_out + hook("out0", {('w_out', l): g_out}).astype(w_out.dtype)
        do_cat = matmul(dmix, w_out, 'nt', F32, "mm_docat" + tag, tn=D_MODEL, tm_max=512)
        lse_b3 = s['lse_b3']
        do_c = (do_cat, (ROW_TILE, GROUP_W), lambda i: (i, 2))
        pair_w = 2 * HEAD_DIM
        dq3, dk3, dv3 = attn_bwd(s['q3'], s['k3'], s['v3'], s['o_b3'], lse_b3, do_cat, GROUP_W // pair_w,
                                 scale_b, "_b" + tag)
        dqd3, dkd3, dvd3 = attn_bwd(s['qd3'], s['kd3'], s['vd3'], s['o_d3'], s['lse_d3'], do_cat,
                                    3 * GROUP_W // pair_w, scale_d, "_d" + tag)
        daglu_pad, dkp['caw'], dkp['cab'], dkp['lag'], dkp['lab'] = aconv_bwd(s['aglu_pad'], kp, do_cat, tag)
        cts = [daglu_pad[CONV_A_HALO:CONV_A_HALO + seq], dq3, dk3, dv3, do_c, dqd3, dkd3, dvd3]
        pre_g = pre_bwd(s['proj'], tabs, kp, cts, tag)
        dproj = pre_g[0]
        for n, g in zip(('qng', 'kng', 'sg', 'sb', 'sw', 'sbt', 'mqn', 'wuq', 'mkvn', 'wukv'), pre_g[1:]):
            dkp[n] = g
        dh_parts = [matmul(dproj, m['w_in'], 'nn', F32, "mm_dh" + tag, add=[dh_a], tn=D_MODEL,
                           tm_max=512)]
        g_in = matmul(dproj, s['h_m'], 'tn', COMM_DTYPE, "mm_gin" + tag, tn=D_MODEL)
        (dw,) = s['unprep'](dkp)
        small_acc = dw if small_acc is None else jax.tree.map(jnp.add, small_acc, dw)
        g_mix = {('w_out', l): g_out, ('w_in', l): _unpad_w_in(g_in)}
        if l > 0:
            tok = hook(f"mix{l}", g_mix)
        else:
            g_mix.pop(('w_out', l))

    g_mix.update({(n, None): small_acc[n] for n in SHARDED if n not in MATMUL_WEIGHTS})
    tok = hook("last", g_mix)
    dx, dg, db = stage_bwd("ln_in_bwd", fn_ln,
                           [_row_op(x, tm, grad=True), _par_op(after(ln_in_g, tok), grad=True),
                            _par_op(ln_in_b, grad=True)],
                           [[(p, (tm, D_MODEL), lambda i: (i, 0)) for p in dh_parts]], (seq // tm,))
    out = {n: small_acc[n] for n in REPLICATED}
    out['ln_in_g'], out['ln_in_b'] = dg.reshape(-1), db.reshape(-1)
    return loss, dx, out


def _peer(x, y, c, r):
    return ((1 - x) if r & 4 else x, (1 - y) if r & 2 else y, (1 - c) if r & 1 else c)


def _exchange_copy(src_ref, land_ref, send_sems, recv_sems, k, gather, x, y, c, r):
    px, py, pc = _peer(x, y, c, r)
    me, peer = 4 * x + 2 * y + c, 4 * px + 2 * py + pc
    src = src_ref if gather else src_ref.at[peer]
    mk = lambda dst: pltpu.make_async_remote_copy(
        src_ref=src, dst_ref=dst, send_sem=send_sems.at[k * (N_DEV - 1) + r - 1],
        recv_sem=recv_sems.at[k * (N_DEV - 1) + r - 1],
        device_id=(px, py, pc), device_id_type=pl.DeviceIdType.MESH)
    return mk(land_ref.at[me]), mk(land_ref.at[peer])


_HBM_SPEC = pl.BlockSpec(memory_space=pltpu.HBM)
_SEM_SPEC = pl.BlockSpec(memory_space=pltpu.SEMAPHORE)


def exchange_start(srcs, gather, groups, name):
    n_t = len(srcs)
    lands =[lax.empty(((N_DEV,) + s.shape) if gt else s.shape, s.dtype) for s, gt in zip(srcs, gather)]

    def body(*refs):
        src_refs, land_refs = refs[:n_t], refs[n_t:2 * n_t]
        sem_refs = refs[2 * n_t:2 * n_t + 2 * len(groups)]
        token = refs[-1]
        x, y, c = lax.axis_index("x"), lax.axis_index("y"), lax.axis_index("c")
        for gi, g in enumerate(groups):
            for k, t in enumerate(g):
                for r in range(1, N_DEV):
                    _exchange_copy(src_refs[t], land_refs[t], sem_refs[2 * gi], sem_refs[2 * gi + 1], k, gather[t],
                                   x, y, c, r)[0].start()
        token[...] = jnp.zeros_like(token)

    sem_shapes = []
    for g in groups:
        sem_shapes += [pltpu.SemaphoreType.DMA((len(g) * (N_DEV - 1),))] * 2
    hbm_shapes = [pltpu.HBM(a.shape, a.dtype) for a in list(srcs) + lands]
    n_sem = len(sem_shapes)
    res = pl.pallas_call(
        body, name=name,
        out_shape=tuple(sem_shapes + hbm_shapes + [jax.ShapeDtypeStruct((8, 128), F32)]),
        in_specs=[_HBM_SPEC] * (2 * n_t),
        out_specs=tuple([_SEM_SPEC] * n_sem + [_HBM_SPEC] * (2 * n_t) + [pl.BlockSpec(memory_space=pltpu.VMEM)]),
        input_output_aliases={i: n_sem + i for i in range(2 * n_t)},
        compiler_params=pltpu.CompilerParams(has_side_effects=pltpu.SideEffectType.DATAFLOW_SIDE_EFFECTING),
    )(*[pltpu.with_memory_space_constraint(a, pltpu.HBM) for a in list(srcs) + lands])
    sems = [(res[2 * gi], res[2 * gi + 1]) for gi in range(len(groups))]
    return sems, list(res[n_sem:n_sem + n_t]), list(res[n_sem + n_t:n_sem + 2 * n_t]), res[-1]


def exchange_wait(sems, srcs, lands, gather, after, name):
    n_t = len(srcs)

    def body(*refs):
        src_refs, land_refs = refs[:n_t], refs[n_t:2 * n_t]
        send_sems, recv_sems = refs[2 * n_t], refs[2 * n_t + 1]
        x, y, c = lax.axis_index("x"), lax.axis_index("y"), lax.axis_index("c")
        for k in range(n_t):
            for r in range(1, N_DEV):
                send, recv = _exchange_copy(src_refs[k], land_refs[k], send_sems, recv_sems, k, gather[k], x, y, c, r)
                send.wait_send()
                recv.wait_recv()

    res = pl.pallas_call(
        body, name=name,
        out_shape=tuple(pltpu.HBM(a.shape, a.dtype) for a in list(srcs) + list(lands)),
        in_specs=[_HBM_SPEC] * (2 * n_t) + [_SEM_SPEC, _SEM_SPEC, pl.BlockSpec(memory_space=pl.ANY)],
        out_specs=tuple([_HBM_SPEC] * (2 * n_t)),
        input_output_aliases={i: i for i in range(2 * n_t)},
        compiler_params=pltpu.CompilerParams(has_side_effects=pltpu.SideEffectType.DATAFLOW_SIDE_EFFECTING),
    )(*srcs, *lands, sems[0], sems[1], after)
    return list(res[:n_t]), list(res[n_t:])


def adamw(parts, w, m, v, name):
    n_l, n_r, n_c = w.shape
    tr = n_r
    if n_r % 8 == 0:
        for cand in (512, 256, 128, 64, 32, 16, 8):
            if n_r % cand == 0 and cand * n_c * 4 <= 512 * 1024:
                tr = cand
                break
    c1 = 1.0 - ADAM_B1 ** ADAM_STEP
    c2 = 1.0 - ADAM_B2 ** ADAM_STEP
    per_layer = isinstance(parts, (list, tuple))
    n_p = n_l if per_layer else 1
    n_rb = n_r // tr

    def update(g, w_ref, m_ref, v_ref, g_ref, d_ref, nm_ref, nv_ref):
        w_, m_, v_ = w_ref[0], m_ref[0], v_ref[0]
        nm = ADAM_B1 * m_ + (1.0 - ADAM_B1) * g
        nv = ADAM_B2 * v_ + (1.0 - ADAM_B2) * (g * g)
        g_ref[0] = g
        nm_ref[0] = nm
        nv_ref[0] = nv
        d_ref[0] = -ADAM_LR * ((nm / c1) / (jnp.sqrt(nv / c2) + ADAM_EPS) + ADAM_WD * w_)

    def body(*refs):
        p_refs, rest = refs[:n_p], refs[n_p:]
        if not per_layer:
            g = p_refs[0][0, 0].astype(F32)
            for s in range(1, N_DEV):
                g = g + p_refs[0][s, 0].astype(F32)
            update(g, *rest)
        else:
            for lay in range(n_l):
                @pl.when(pl.program_id(0) == lay)
                def _(lay=lay):
                    g = p_refs[lay][0].astype(F32)
                    for s in range(1, N_DEV):
                        g = g + p_refs[lay][s].astype(F32)
                    update(g, *rest)

    blk = pl.BlockSpec((1, tr, n_c), lambda l, r: (l, r, 0))
    if per_layer:
        def p_spec(lay):
            park = 0 if lay > 0 else n_rb - 1
            return pl.BlockSpec((N_DEV, tr, n_c), lambda l, r: (0, jnp.where(l == lay, r, park), 0))
        p_specs, p_args = [p_spec(lay) for lay in range(n_l)], list(parts)
    else:
        p_specs, p_args = [pl.BlockSpec((N_DEV, 1, tr, n_c), lambda l, r: (0, l, r, 0))], [parts]
    return pl.pallas_call(
        body, grid=(n_l, n_rb), in_specs=p_specs + [blk, blk, blk],
        out_specs=[blk] * 4, out_shape=[jax.ShapeDtypeStruct(w.shape, F32)] * 4,
        compiler_params=pltpu.CompilerParams(dimension_semantics=("arbitrary", "arbitrary"),
                                             vmem_limit_bytes=VMEM_LIMIT),
        name=name)(*p_args, w, m, v)


def adamw_replicated(lands, own, ws, ms, vs, loss_land, loss_own):
    n_t = len(lands)
    c1 = 1.0 - ADAM_B1 ** ADAM_STEP
    c2 = 1.0 - ADAM_B2 ** ADAM_STEP

    def body(*refs):
        ins, outs = refs[:5 * n_t + 2], refs[5 * n_t + 2:]
        me = 4 * lax.axis_index("x") + 2 * lax.axis_index("y") + lax.axis_index("c")

        def total(land_ref, own_ref):
            g = None
            for s in range(N_DEV):
                term = jnp.where(me == s, own_ref[...], land_ref[s])
                g = term if g is None else g + term
            return g

        for t in range(n_t):
            land_ref, own_ref, w_ref, m_ref, v_ref = ins[5 * t:5 * t + 5]
            g = total(land_ref, own_ref)
            nm = ADAM_B1 * m_ref[...] + (1.0 - ADAM_B1) * g
            nv = ADAM_B2 * v_ref[...] + (1.0 - ADAM_B2) * (g * g)
            g_ref, d_ref, nm_ref, nv_ref = outs[4 * t:4 * t + 4]
            g_ref[...] = g
            nm_ref[...] = nm
            nv_ref[...] = nv
            d_ref[...] = -ADAM_LR * ((nm / c1) / (jnp.sqrt(nv / c2) + ADAM_EPS) + ADAM_WD * w_ref[...])
        outs[4 * n_t][...] = total(ins[5 * n_t], ins[5 * n_t + 1])

    args = []
    for t in range(n_t):
        args += [lands[t], own[t], ws[t], ms[t], vs[t]]
    out_shape = []
    for t in range(n_t):
        out_shape += [jax.ShapeDtypeStruct(ws[t].shape, F32)] * 4
    out_shape.append(jax.ShapeDtypeStruct(loss_own.shape, F32))
    res = pl.pallas_call(body, out_shape=out_shape,
                         compiler_params=pltpu.CompilerParams(vmem_limit_bytes=VMEM_LIMIT),
                         name="adamw_replicated")(*args, loss_land, loss_own)
    return [tuple(res[4 * t:4 * t + 4]) for t in range(n_t)], res[-1]


def _shard_slots(g, axis):
    if axis == 1:
        return g.reshape(g.shape[0], N_DEV, g.shape[1] // N_DEV, g.shape[2]).transpose(1, 0, 2, 3)
    return g.reshape(g.shape[0], g.shape[1], N_DEV, g.shape[2] // N_DEV).transpose(2, 0, 1, 3)


def _unshard(slots, axis):
    if axis == 1:
        return slots.transpose(1, 0, 2, 3).reshape(slots.shape[1], -1, slots.shape[3])
    return slots.transpose(1, 2, 0, 3).reshape(slots.shape[1], slots.shape[2], -1)


def kernel(x, ln_in_g, ln_in_b, w_in, conv_a_w, conv_a_b, ln_a_g, ln_a_b, qk_norm_q, qk_norm_k, sgu_ln_g, sgu_ln_b, sgu_w, sgu_b, mla_q_norm, mla_w_uq, mla_kv_norm, mla_w_ukv, w_out, ln_mix_g, ln_mix_b, ffn_w_up, ffn_conv_w, ffn_conv_b, ffn_w_down, ln_ffn_g, ln_ffn_b, loss_target, m_ln_in_g, m_ln_in_b, m_w_in, m_conv_a_w, m_conv_a_b, m_ln_a_g, m_ln_a_b, m_qk_norm_q, m_qk_norm_k, m_sgu_ln_g, m_sgu_ln_b, m_sgu_w, m_sgu_b, m_mla_q_norm, m_mla_w_uq, m_mla_kv_norm, m_mla_w_ukv, m_w_out, m_ln_mix_g, m_ln_mix_b, m_ffn_w_up, m_ffn_conv_w, m_ffn_conv_b, m_ffn_w_down, m_ln_ffn_g, m_ln_ffn_b, v_ln_in_g, v_ln_in_b, v_w_in, v_conv_a_w, v_conv_a_b, v_ln_a_g, v_ln_a_b, v_qk_norm_q, v_qk_norm_k, v_sgu_ln_g, v_sgu_ln_b, v_sgu_w, v_sgu_b, v_mla_q_norm, v_mla_w_uq, v_mla_kv_norm, v_mla_w_ukv, v_w_out, v_ln_mix_g, v_ln_mix_b, v_ffn_w_up, v_ffn_conv_w, v_ffn_conv_b, v_ffn_w_down, v_ln_ffn_g, v_ln_ffn_b):
    local = dict(ln_in_g=ln_in_g, ln_in_b=ln_in_b, w_in=w_in, conv_a_w=conv_a_w, conv_a_b=conv_a_b, ln_a_g=ln_a_g, ln_a_b=ln_a_b, qk_norm_q=qk_norm_q, qk_norm_k=qk_norm_k, sgu_ln_g=sgu_ln_g, sgu_ln_b=sgu_ln_b, sgu_w=sgu_w, sgu_b=sgu_b, mla_q_norm=mla_q_norm, mla_w_uq=mla_w_uq, mla_kv_norm=mla_kv_norm, mla_w_ukv=mla_w_ukv, w_out=w_out, ln_mix_g=ln_mix_g, ln_mix_b=ln_mix_b, ffn_w_up=ffn_w_up, ffn_conv_w=ffn_conv_w, ffn_conv_b=ffn_conv_b, ffn_w_down=ffn_w_down, ln_ffn_g=ln_ffn_g, ln_ffn_b=ln_ffn_b)
    mom = dict(ln_in_g=m_ln_in_g, ln_in_b=m_ln_in_b, w_in=m_w_in, conv_a_w=m_conv_a_w, conv_a_b=m_conv_a_b, ln_a_g=m_ln_a_g, ln_a_b=m_ln_a_b, qk_norm_q=m_qk_norm_q, qk_norm_k=m_qk_norm_k, sgu_ln_g=m_sgu_ln_g, sgu_ln_b=m_sgu_ln_b, sgu_w=m_sgu_w, sgu_b=m_sgu_b, mla_q_norm=m_mla_q_norm, mla_w_uq=m_mla_w_uq, mla_kv_norm=m_mla_kv_norm, mla_w_ukv=m_mla_w_ukv, w_out=m_w_out, ln_mix_g=m_ln_mix_g, ln_mix_b=m_ln_mix_b, ffn_w_up=m_ffn_w_up, ffn_conv_w=m_ffn_conv_w, ffn_conv_b=m_ffn_conv_b, ffn_w_down=m_ffn_w_down, ln_ffn_g=m_ln_ffn_g, ln_ffn_b=m_ln_ffn_b)
    var = dict(ln_in_g=v_ln_in_g, ln_in_b=v_ln_in_b, w_in=v_w_in, conv_a_w=v_conv_a_w, conv_a_b=v_conv_a_b, ln_a_g=v_ln_a_g, ln_a_b=v_ln_a_b, qk_norm_q=v_qk_norm_q, qk_norm_k=v_qk_norm_k, sgu_ln_g=v_sgu_ln_g, sgu_ln_b=v_sgu_ln_b, sgu_w=v_sgu_w, sgu_b=v_sgu_b, mla_q_norm=v_mla_q_norm, mla_w_uq=v_mla_w_uq, mla_kv_norm=v_mla_kv_norm, mla_w_ukv=v_mla_w_ukv, w_out=v_w_out, ln_mix_g=v_ln_mix_g, ln_mix_b=v_ln_mix_b, ffn_w_up=v_ffn_w_up, ffn_conv_w=v_ffn_conv_w, ffn_conv_b=v_ffn_conv_b, ffn_w_down=v_ffn_w_down, ln_ffn_g=v_ln_ffn_g, ln_ffn_b=v_ln_ffn_b)

    me = 4 * lax.axis_index("x") + 2 * lax.axis_index("y") + lax.axis_index("c")

    def own_slot(slots, block):
        return lax.dynamic_update_slice(slots, block[None], (me,) + (0,) * block.ndim)

    small_sharded = [n for n in SHARDED if n not in MATMUL_WEIGHTS]
    big_order = [(n, l) for l in range(DEPTH) for n in MATMUL_WEIGHTS]
    send_view = lambda n, l: (local[n].transpose(0, 2, 1)[l] if n in TRANSPOSED else local[n][l]).astype(COMM_DTYPE)
    srcs = [send_view(*big_order[0])] + [local[n] for n in small_sharded]
    srcs += [send_view(n, l) for (n, l) in big_order[1:]]
    n_first = 1 + len(small_sharded)
    groups = [list(range(n_first))] + [[n_first + j] for j in range(len(big_order) - 1)]
    g_sems, g_srcs, g_lands, tok0 = exchange_start(srcs, [True] * len(srcs), groups, "gather_start")
    tok0 = tok0[0, 0]
    pending = {key: gi for gi, key in enumerate(big_order)}

    def finish(gi, after):
        idx = groups[gi]
        _, lands = exchange_wait(g_sems[gi], [g_srcs[t] for t in idx], [g_lands[t] for t in idx], [True] * len(idx),
                                 after, f"gather_wait{gi}")
        return [own_slot(ld, srcs[t]) for ld, t in zip(lands, idx)]

    first = []

    opt_view = {n: tuple(a.transpose(0, 2, 1) for a in (local[n], mom[n], var[n])) for n in TRANSPOSED}

    def get_wts(after):
        for views in opt_view.values():
            after = after + sum(a[0, :8, :128] for a in views)
        first.extend(finish(0, after))
        wts = {n: local[n] for n in REPLICATED}
        for n, slots in zip(small_sharded, first[1:]):
            wts[n] = _unshard(slots, SHARDED[n])
        return wts

    def unshard_layer(slots, n):
        if SHARDED[n] == 1 or n in TRANSPOSED:
            return slots.reshape(-1, slots.shape[2])
        return slots.transpose(1, 0, 2).reshape(slots.shape[1], -1)

    def mat(l, n, after):
        gi = pending[(n, l)]
        slots = first[0] if gi == 0 else finish(gi, after)[0]
        w = unshard_layer(slots, n).astype(MXU_DTYPE)
        return _pad_w_in(w) if n == 'w_in' else w

    started = []

    def hook(key, grads):
        tensors = []
        for (n, l), g in grads.items():
            if l is None:
                tensors.append(((n, l), _shard_slots(g, SHARDED[n])))
            elif n in TRANSPOSED:
                tensors.append(((n, l), g.reshape(N_DEV, g.shape[0] // N_DEV, g.shape[1]).astype(COMM_DTYPE)))
            else:
                tensors.append(((n, l), _shard_slots(g[None], SHARDED[n])[:, 0].astype(COMM_DTYPE)))
        sems, s_srcs, s_lands, tok = exchange_start([a for _, a in tensors], [False] * len(tensors),
                                                    [list(range(len(tensors)))], "scatter_start_" + key)
        started.append((key, [k for k, _ in tensors], sems[0], s_srcs, s_lands))
        return tok[0, 0]

    loss, dx, grads = local_step(x[0], loss_target[0], (local['ln_in_g'] + tok0, local['ln_in_b']), get_wts, mat, hook)

    as2d = lambda a: a.reshape(-1, a.shape[-1]) if a.ndim > 1 else a.reshape(1, -1)
    small_g = [as2d(grads[n]) for n in REPLICATED] + [jnp.broadcast_to(loss, (8, 128))]
    p_sems, p_srcs, p_lands, p_tok = exchange_start(small_g, [True] * len(small_g), [list(range(len(small_g)))],
                                                    "gather_small_start")

    parts, res = {}, {}

    def finish_scatter(entries, after):
        for key, keys, sems, s_srcs, s_lands in entries:
            s_out, lands = exchange_wait(sems, s_srcs, s_lands, [False] * len(keys), after, "scatter_wait_" + key)
            for k, so, ld in zip(keys, s_out, lands):
                parts[k] = own_slot(ld, lax.dynamic_index_in_dim(so, me, 0, keepdims=False))

    def update(names_):
        for n in names_:
            p = [parts[(n, l)] for l in range(DEPTH)] if n in MATMUL_WEIGHTS else parts[(n, None)]
            if n in TRANSPOSED:
                res[n] = tuple(a.transpose(0, 2, 1) for a in adamw(p, *opt_view[n], "adamw_" + n))
            else:
                res[n] = adamw(p, local[n], mom[n], var[n], "adamw_" + n)

    early = ('ffn_w_up', 'ffn_w_down', 'w_out')
    finish_scatter([e for e in started if e[0] != "last"], p_tok)
    update(early)
    finish_scatter([e for e in started if e[0] == "last"], res[early[-1]][1])
    update([n for n in SHARDED if n not in early])
    updated = jnp.zeros((8, 128), F32) + sum(res[n][1][0, 0, 0] for n in SHARDED)
    p_own, p_lands = exchange_wait(p_sems[0], p_srcs, p_lands, [True] * len(small_g), updated, "gather_small_wait")
    small, loss_sum = adamw_replicated(p_lands[:-1], p_own[:-1], [as2d(local[n]) for n in REPLICATED],
                                       [as2d(mom[n]) for n in REPLICATED], [as2d(var[n]) for n in REPLICATED],
                                       p_lands[-1], p_own[-1])
    for n, quad in zip(REPLICATED, small):
        res[n] = tuple(a.reshape(local[n].shape) for a in quad)
    loss_total = loss_sum[0, 0]

    return (loss_total, dx[None], *[res[n][0] for n in WEIGHTS], *[res[n][1] for n in WEIGHTS],
            *[res[n][2] for n in WEIGHTS], *[res[n][3] for n in WEIGHTS])
```

```python
import functools
import math

import jax
import jax.numpy as jnp
from jax import lax
from jax.experimental import pallas as pl
from jax.experimental.pallas import tpu as pltpu

F32 = jnp.float32
MXU_DTYPE = jnp.bfloat16
COMM_DTYPE = jnp.bfloat16

N_DEV = 8
D_MODEL = 1024
DEPTH = 2
GRID_W = 64
GROUP_W = 256
HEAD_DIM = 64
CONV_A_WIDTH = 31
CONV_A_HALO = 16
GQA_HEADS = 4
GQA_KV_HEADS = 2
CHUNK = 128
SGU_GROUPS = 4
MLA_HEADS = 4
MLA_Q_LORA = 192
MLA_KV_LORA = 128
MLA_NOPE = 64
MLA_ROPE = 32
MLA_V = 64
MLA_DK_PAD = 128
ROPE_THETA = 10000.0
D_FF = 2816
DEEPNORM_ALPHA = (2 * DEPTH) ** 0.25
LN_EPS = 1e-5
RMS_EPS = 1e-6
D_IN_PROJ = 1888

ADAM_LR = 0.001
ADAM_B1 = 0.9
ADAM_B2 = 0.999
ADAM_EPS = 1e-08
ADAM_WD = 0.01
ADAM_STEP = 10

WEIGHTS = ['ln_in_g', 'ln_in_b', 'w_in', 'conv_a_w', 'conv_a_b', 'ln_a_g', 'ln_a_b', 'qk_norm_q', 'qk_norm_k',
           'sgu_ln_g', 'sgu_ln_b', 'sgu_w', 'sgu_b', 'mla_q_norm', 'mla_w_uq', 'mla_kv_norm', 'mla_w_ukv', 'w_out',
           'ln_mix_g', 'ln_mix_b', 'ffn_w_up', 'ffn_conv_w', 'ffn_conv_b', 'ffn_w_down', 'ln_ffn_g', 'ln_ffn_b']
SHARDED = {'w_in': 2, 'conv_a_w': 2, 'mla_w_uq': 2, 'mla_w_ukv': 2, 'w_out': 1, 'ffn_w_up': 2, 'ffn_conv_w': 2,
           'ffn_w_down': 1}
MATMUL_WEIGHTS = ('w_in', 'w_out', 'ffn_w_up', 'ffn_w_down')
TRANSPOSED = ('w_in', 'ffn_w_up')
REPLICATED = [n for n in WEIGHTS if n not in SHARDED]

ROW_TILE = 256
LN_TILE = 512
VMEM_LIMIT = 56 * 1024 * 1024


def _rawdot(a, b, ca, cb):
    return lax.dot_general(a.astype(MXU_DTYPE), b.astype(MXU_DTYPE), (((ca,), (cb,)), ((), ())),
                           preferred_element_type=F32)


@jax.custom_vjp
def mm_nn(a, b):
    return _rawdot(a, b, 1, 0)


def _mm_nn_fwd(a, b):
    return _rawdot(a, b, 1, 0), (a, b)


def _mm_nn_bwd(res, dy):
    a, b = res
    return _rawdot(dy, b, 1, 1), _rawdot(a, dy, 0, 0)


mm_nn.defvjp(_mm_nn_fwd, _mm_nn_bwd)


@jax.custom_vjp
def mm_nt(a, b):
    return _rawdot(a, b, 1, 1)


def _mm_nt_fwd(a, b):
    return _rawdot(a, b, 1, 1), (a, b)


def _mm_nt_bwd(res, dy):
    a, b = res
    return _rawdot(dy, b, 1, 0), _rawdot(dy, a, 0, 0)


mm_nt.defvjp(_mm_nt_fwd, _mm_nt_bwd)


def _pick_tile(d, cands):
    for c in cands:
        if d % c == 0:
            return c
    return d


def matmul(a, b, mode, out_dtype, name, b_rows=None, into=None, add=(), tm_max=1408, tn=None):
    a_list = list(a) if isinstance(a, (list, tuple)) else [a]
    b_list = list(b) if isinstance(b, (list, tuple)) else [b]
    n_p = len(a_list)
    rows_list = [b_rows] if not isinstance(a, (list, tuple)) else (list(b_rows) if b_rows is not None else [None] * n_p)
    b_start, b_size = zip(*[(0, bb.shape[0]) if r is None else r for bb, r in zip(b_list, rows_list)])
    a0, b0 = a_list[0], b_list[0]
    if mode == 'nn':
        (m, k), (k2, n) = a0.shape, (b_size[0], b0.shape[1])
    elif mode == 'nt':
        (m, k), (n, k2) = a0.shape, (b_size[0], b0.shape[1])
    else:
        (k, m), (k2, n) = a0.shape, (b_size[0], b0.shape[1])
    assert k == k2 and all(x.shape == a0.shape for x in a_list) and len(set(b_size)) == 1, (a0.shape, b0.shape, mode)
    tm = _pick_tile(m, tuple(c for c in (1024, 1408, 512, 256, 128) if c <= tm_max))
    tn = _pick_tile(n, (512, 1408, 256, 128)) if tn is None else tn
    assert n % tn == 0, (n, tn)
    tk = _pick_tile(k, (2816, 2048, 1024, 512, 256, 128))
    nk = k // tk
    ca = 0 if mode == 'tn' else 1
    cb = 1 if mode == 'nt' else 0
    b_blk = tn if mode == 'nt' else tk
    assert all(s % b_blk == 0 for s in b_start), (b_rows, b_blk)
    a_spec = pl.BlockSpec((tk, tm), lambda i, j, kk: (kk, i)) if mode == 'tn' else pl.BlockSpec((tm, tk), lambda i, j, kk: (i, kk))

    def b_spec(off):
        if mode == 'nt':
            return pl.BlockSpec((tn, tk), lambda i, j, kk: (j + off, kk))
        return pl.BlockSpec((tk, tn), lambda i, j, kk: (kk + off, j))

    in_specs, args, aliases = [], [], {}
    for x, y, s in zip(a_list, b_list, b_start):
        in_specs += [a_spec, b_spec(s // b_blk)]
        args += [x, y]
    in_specs += [pl.BlockSpec((tm, tn), lambda i, j, kk: (i, j))] * len(add)
    args += list(add)
    out_off, out_shape = 0, jax.ShapeDtypeStruct((m, n), out_dtype)
    if into is not None:
        buf, row = into
        assert row % tm == 0 and buf.shape[1] == n and buf.dtype == out_dtype, (buf.shape, row, tm)
        out_off, out_shape = row // tm, jax.ShapeDtypeStruct(buf.shape, buf.dtype)
        aliases = {len(args): 0}
        in_specs, args = in_specs + [pl.BlockSpec(memory_space=pl.ANY)], args + [buf]
    n_add = len(add)

    def body(*refs):
        o_ref, acc_ref = refs[-2:]
        kk = pl.program_id(2)

        @pl.when(kk == 0)
        def _():
            acc_ref[...] = jnp.zeros_like(acc_ref)

        for p in range(n_p):
            acc_ref[...] += _rawdot(refs[2 * p][...], refs[2 * p + 1][...], ca, cb)

        @pl.when(kk == nk - 1)
        def _():
            total = acc_ref[...]
            for r in refs[2 * n_p:2 * n_p + n_add]:
                total = total + r[...]
            o_ref[...] = total.astype(o_ref.dtype)

    return pl.pallas_call(
        body, grid=(m // tm, n // tn, nk), in_specs=in_specs,
        out_specs=pl.BlockSpec((tm, tn), lambda i, j, kk: (i + out_off, j)),
        out_shape=out_shape, input_output_aliases=aliases,
        scratch_shapes=[pltpu.VMEM((tm, tn), F32)],
        compiler_params=pltpu.CompilerParams(dimension_semantics=("parallel", "parallel", "arbitrary"),
                                             vmem_limit_bytes=VMEM_LIMIT),
        name=name)(*args)


class Op:
    def __init__(self, arr, block, imap, grad=False, acc=False, first=None, gdtype=F32, gshape=None, gimap=None):
        self.arr, self.block, self.imap = arr, block, imap
        self.grad, self.acc, self.first, self.gdtype = grad, acc, first, gdtype
        self.gshape = arr.shape if gshape is None else gshape
        self.gimap = imap if gimap is None else gimap


def _row_op(arr, tm, grad=False, gdtype=F32):
    return Op(arr, (tm, arr.shape[1]), lambda i: (i, 0), grad=grad, gdtype=gdtype)


def _par_op(arr, grad=False):
    nd = arr.ndim
    return Op(arr, arr.shape, lambda i: (0,) * nd, grad=grad, acc=True, first=lambda ids: ids[0] == 0)


def _load(ref):
    v = ref[...]
    return v.astype(F32) if jnp.issubdtype(v.dtype, jnp.floating) else v


def _store_heads(ref, val):
    rows = val.shape[0]
    if len(ref.shape) == 2:
        ref[...] = val.astype(ref.dtype)
    elif ref.shape[1] == rows:
        d = ref.shape[2]
        for h in range(ref.shape[0]):
            ref[h] = val[:, d * h:d * (h + 1)].astype(ref.dtype)
    else:
        d = ref.shape[1]
        assert ref.shape[2] == rows and d != rows, (ref.shape, val.shape)
        for h in range(ref.shape[0]):
            ref[h] = val[:, d * h:d * (h + 1)].T.astype(ref.dtype)


def _load_heads(ref, transposed=False):
    if len(ref.shape) == 2:
        return ref[...].astype(F32)
    parts = [ref[h].astype(F32) for h in range(ref.shape[0])]
    return jnp.concatenate([p.T for p in parts] if transposed else parts, axis=-1)


def stage_fwd(name, fn, ops, outs, grid):
    n_in = len(ops)

    def body(*refs):
        res = fn(*[_load(r) for r in refs[:n_in]])
        for r, o in zip(refs[n_in:], res):
            _store_heads(r, o)

    return pl.pallas_call(
        body, grid=grid, in_specs=[pl.BlockSpec(o.block, o.imap) for o in ops],
        out_specs=[pl.BlockSpec(b, im) for (_, _, b, im) in outs],
        out_shape=[jax.ShapeDtypeStruct(s, d) for (s, d, _, _) in outs],
        compiler_params=pltpu.CompilerParams(dimension_semantics=("parallel",) * len(grid),
                                             vmem_limit_bytes=VMEM_LIMIT),
        name=name)(*[o.arr for o in ops])


def stage_bwd(name, fn, ops, cts, grid, value_acc=False):
    n_in = len(ops)
    ct_flat = [(c + (False,))[:4] for group in cts if group is not None for c in group]
    n_ct = len(ct_flat)
    diff = [i for i, o in enumerate(ops) if o.grad]
    any_acc = value_acc or any(ops[i].acc for i in diff)
    ngrid = len(grid)

    def body(*refs):
        ids = [pl.program_id(a) for a in range(ngrid)]
        vals = [_load(r) for r in refs[:n_in]]
        ct_refs = refs[n_in:n_in + n_ct]
        out_refs = refs[n_in + n_ct:]

        def f(*dv):
            full = list(vals)
            for i, v in zip(diff, dv):
                full[i] = v
            return tuple(fn(*full))

        res, vjp = jax.vjp(f, *[vals[i] for i in diff])
        ct, pos = [], 0
        for group, r in zip(cts, res):
            if group is None:
                ct.append(jnp.ones_like(r))
            else:
                tot = None
                for _ in group:
                    c = _load_heads(ct_refs[pos], ct_flat[pos][3])
                    tot = c if tot is None else tot + c
                    pos += 1
                ct.append(tot)
        grads = vjp(tuple(ct))
        for i, g, r in zip(diff, grads, out_refs):
            if ops[i].acc:
                @pl.when(ops[i].first(ids))
                def _(r=r):
                    r[...] = jnp.zeros_like(r)

                r[...] += g.astype(r.dtype)
            else:
                r[...] = g.astype(r.dtype)
        if value_acc:
            r = out_refs[len(diff)]

            @pl.when(ids[0] == 0)
            def _():
                r[...] = jnp.zeros_like(r)

            r[...] += res[0]

    in_specs = [pl.BlockSpec(o.block, o.imap) for o in ops] + [pl.BlockSpec(b, im) for (_, b, im, _) in ct_flat]
    out_specs = [pl.BlockSpec(ops[i].block, ops[i].gimap) for i in diff]
    out_shape = [jax.ShapeDtypeStruct(ops[i].gshape, ops[i].gdtype) for i in diff]
    if value_acc:
        out_specs.append(pl.BlockSpec((1, 1), lambda *ids: (0, 0)))
        out_shape.append(jax.ShapeDtypeStruct((1, 1), F32))
    sem = ("arbitrary",) * ngrid if any_acc else ("parallel",) * ngrid
    return pl.pallas_call(
        body, grid=grid, in_specs=in_specs, out_specs=out_specs, out_shape=out_shape,
        compiler_params=pltpu.CompilerParams(dimension_semantics=sem, vmem_limit_bytes=VMEM_LIMIT),
        name=name)(*[o.arr for o in ops], *[a for (a, _, _, _) in ct_flat])


def _sigmoid(x):
    return 1.0 / (1.0 + jnp.exp(-x))


def _silu(x):
    return x * _sigmoid(x)


def _gelu_tanh(x):
    return 0.5 * x * (1.0 + jnp.tanh(math.sqrt(2.0 / math.pi) * (x + 0.044715 * (x * x * x))))


def _ln(x, g, b):
    mu = jnp.mean(x, axis=-1, keepdims=True)
    xc = x - mu
    var = jnp.mean(xc * xc, axis=-1, keepdims=True)
    return xc * lax.rsqrt(var + LN_EPS) * g + b


def _rms(x, g):
    ms = jnp.mean(x * x, axis=-1, keepdims=True)
    return x * lax.rsqrt(ms + RMS_EPS) * g


def _swap_halves(x, half):
    width = x.shape[-1]
    lane = lax.broadcasted_iota(jnp.int32, x.shape, 1)
    return jnp.where(lane % (2 * half) < half, pltpu.roll(x, width - half, 1), pltpu.roll(x, half, 1))


def _make_swap(half):
    @jax.custom_vjp
    def swap(x):
        return _swap_halves(x, half)

    swap.defvjp(lambda x: (_swap_halves(x, half), None), lambda _, dy: (_swap_halves(dy, half),))
    return swap


_swap16, _swap8 = _make_swap(16), _make_swap(8)


def _rope(x, cos, sin_signed, swap):
    return x * cos + swap(x) * sin_signed


def _dot_f32(a, b):
    return jnp.dot(a, b, preferred_element_type=F32, precision=lax.Precision.HIGHEST)


def fn_ln(x, g, b):
    return (_ln(x, g, b),)


def _twice(fn):
    def f(*a):
        (y,) = fn(*a)
        return y, y
    return f


PROJ_W = 2048
P_A, P_Q, P_K, P_V, P_C, P_CQ, P_CKV, P_KR = 0, 512, 768, 896, 1024, 1536, 1792, 1920
CQ_PAD = 256
_CQ_END = P_CQ + MLA_Q_LORA


def _pad_w_in(wt):
    z = lambda n: jnp.zeros((n, wt.shape[1]), wt.dtype)
    return jnp.concatenate([wt[:_CQ_END], z(P_CKV - _CQ_END), wt[_CQ_END:], z(PROJ_W - P_KR - MLA_ROPE)], axis=0)


def _unpad_w_in(gt):
    return jnp.concatenate([gt[:_CQ_END], gt[P_CKV:P_KR + MLA_ROPE]], axis=0)


def fn_pre(proj, tab_q, tab_d, seg, place, qng, kng, sg, sb, sw, sbt, mqn, wuq, mkvn, wukv):
    tm = proj.shape[0]
    aglu = proj[:, P_A:P_A + GROUP_W] * _sigmoid(proj[:, P_A + GROUP_W:P_Q])
    b_q, b_k, b_v = proj[:, P_Q:P_K], proj[:, P_K:P_V], proj[:, P_V:P_C]
    cos_q, sin_q = tab_q[:, :GROUP_W], tab_q[:, GROUP_W:]
    q = b_q * lax.rsqrt(_dot_f32(b_q * b_q, seg) + RMS_EPS) * qng
    q = _rope(q, cos_q, sin_q, _swap16)
    k = b_k * lax.rsqrt(_dot_f32(b_k * b_k, seg[:128, :128]) + RMS_EPS) * kng
    k = _rope(k, cos_q[:, :128], sin_q[:, :128], _swap16)
    c = _gelu_tanh(proj[:, P_C:P_CQ])
    u, sv = c[:, :GROUP_W], _ln(c[:, GROUP_W:], sg, sb)
    group = lax.broadcasted_iota(jnp.int32, (CHUNK, GROUP_W), 1) // HEAD_DIM
    rows = []
    for n in range(tm // CHUNK):
        svn = sv[CHUNK * n:CHUNK * (n + 1)]
        acc = jnp.zeros((CHUNK, GROUP_W), F32)
        for g in range(SGU_GROUPS):
            acc = acc + jnp.where(group == g, mm_nn(sw[CHUNK * g:CHUNK * (g + 1)], svn) + sbt[:, g:g + 1], 0.0)
        rows.append(acc)
    o_c = u * jnp.concatenate(rows, axis=0)
    d_cq, d_ckv, d_kr = proj[:, P_CQ:P_CKV], proj[:, P_CKV:P_KR], proj[:, P_KR:PROJ_W]
    cqn = d_cq * lax.rsqrt(jnp.sum(d_cq * d_cq, axis=-1, keepdims=True) * (1.0 / MLA_Q_LORA) + RMS_EPS) * mqn
    cos_d = jnp.concatenate([tab_d[:, :MLA_DK_PAD]] * MLA_HEADS, axis=-1)
    sin_d = jnp.concatenate([tab_d[:, MLA_DK_PAD:]] * MLA_HEADS, axis=-1)
    qf = _rope(mm_nn(cqn, wuq), cos_d, sin_d, _swap8)
    kvd = mm_nn(_rms(d_ckv, mkvn), wukv)
    kf = _rope(kvd[:, :MLA_HEADS * MLA_DK_PAD] + _dot_f32(d_kr, place), cos_d, sin_d, _swap8)
    return aglu, q, k, b_v, o_c, qf, kf, kvd[:, MLA_HEADS * MLA_DK_PAD:]


def fn_aconv(win, w, b, g, beta):
    tm = win.shape[0] - 2 * CONV_A_HALO
    off = CONV_A_HALO - CONV_A_WIDTH // 2
    acc = None
    for r in range(8):
        rolled = win if r == 0 else _roll_rows(win, -r)
        for kk in range(CONV_A_WIDTH):
            if (off + kk) % 8 == r:
                base = off + kk - r
                term = rolled[base:base + tm] * w[kk:kk + 1, :]
                acc = term if acc is None else acc + term
    return (_silu(_ln(acc + b, g, beta)),)


def fn_resln(h, r, g, b):
    return (_ln(DEEPNORM_ALPHA * h + r, g, b),)


@functools.partial(jax.custom_vjp, nondiff_argnums=(1,))
def _roll_rows(x, shift):
    return pltpu.roll(x, shift % x.shape[0], 0)


_roll_rows.defvjp(lambda x, shift: (pltpu.roll(x, shift % x.shape[0], 0), None),
                  lambda shift, _, dy: (pltpu.roll(dy, (-shift) % dy.shape[0], 0),))


def _shift_down(x):
    row = lax.broadcasted_iota(jnp.int32, x.shape, 0)
    return jnp.where(row == 0, 0.0, _roll_rows(x, 1))


def _shift_up(x):
    row = lax.broadcasted_iota(jnp.int32, x.shape, 0)
    return jnp.where(row == x.shape[0] - 1, 0.0, _roll_rows(x, -1))


def fn_ffnconv(u1, u2, w1, w2, b1, b2):
    c1 = _shift_down(u1) * w1[0:1] + u1 * w1[1:2] + _shift_up(u1) * w1[2:3] + b1
    c2 = _shift_down(u2) * w2[0:1] + u2 * w2[1:2] + _shift_up(u2) * w2[2:3] + b2
    return (_silu(c1) * c2,)


def fn_final(h, r, t, g, b):
    y = _ln(DEEPNORM_ALPHA * h + r, g, b)
    err = (y - t) * (y - t)
    return (0.5 * jnp.sum(jnp.mean(err, axis=-1, keepdims=True), axis=0, keepdims=True),)


def _rope_tables(seq):
    n_rows = seq // GRID_W
    lane128 = jnp.arange(128)

    def tile_tables(j, rotated, half):
        inv = ROPE_THETA ** (-(j % half).astype(F32) / half)
        by_row, by_col = rotated & (j < 2 * half), rotated & (j >= 2 * half)
        sign = jnp.where(j % (2 * half) < half, -1.0, 1.0)
        ar = jnp.arange(n_rows, dtype=F32)[:, None] * inv[None, :]
        ac = jnp.arange(GRID_W, dtype=F32)[:, None] * inv[None, :]
        grid = lambda r, c: (jnp.where(by_row, r, 0.0)[:, None, :] + jnp.where(by_col, c, 0.0)[None, :, :])
        cos = grid(jnp.cos(ar), jnp.cos(ac)) + jnp.where(rotated, 0.0, 1.0)
        sin = grid(sign * jnp.sin(ar), sign * jnp.sin(ac))
        return cos.reshape(seq, 128), sin.reshape(seq, 128)

    cos_b, sin_b = tile_tables(lane128 % HEAD_DIM, lane128 >= 0, HEAD_DIM // 4)
    tab_q = jnp.concatenate([cos_b] * (GROUP_W // 128) + [sin_b] * (GROUP_W // 128), axis=-1)
    tab_d = jnp.concatenate(tile_tables(lane128 - MLA_NOPE, (lane128 >= MLA_NOPE) & (lane128 < MLA_NOPE + MLA_ROPE),
                                        MLA_ROPE // 4), axis=-1)
    lane = jnp.arange(GROUP_W)
    seg = jnp.where(lane[:, None] // HEAD_DIM == lane[None, :] // HEAD_DIM, 1.0 / HEAD_DIM, 0.0).astype(F32)
    src, dst = jnp.arange(128)[:, None], jnp.arange(MLA_HEADS * MLA_DK_PAD)[None, :]
    place = jnp.where((src < MLA_ROPE) & (dst % MLA_DK_PAD == MLA_NOPE + src), 1.0, 0.0).astype(F32)
    return tab_q, tab_d, seg, place


def _pre_ops(proj, tabs, kp, grad):
    tm = ROW_TILE
    ops = [_row_op(proj, tm, grad=grad, gdtype=MXU_DTYPE), _row_op(tabs[0], tm), _row_op(tabs[1], tm),
           _par_op(tabs[2]), _par_op(tabs[3])]
    ops += [_par_op(kp[n], grad=grad) for n in ('qng', 'kng', 'sg', 'sb', 'sw', 'sbt', 'mqn', 'wuq', 'mkvn', 'wukv')]
    return ops


PRE_OUTS = ((0, GROUP_W), (GQA_HEADS, HEAD_DIM), (GQA_KV_HEADS, HEAD_DIM), (GQA_KV_HEADS, HEAD_DIM), (0, GROUP_W),
            (MLA_HEADS, MLA_DK_PAD), (MLA_HEADS, MLA_DK_PAD), (MLA_HEADS, MLA_V))


def _pre_out_specs(seq, tm):
    specs = []
    for j, (heads, w) in enumerate(PRE_OUTS):
        if heads:
            specs.append(((heads, seq, w), (heads, tm, w), lambda i: (0, i, 0)))
        elif j == 0:
            specs.append(((seq + 2 * tm, w), (tm, w), lambda i: (i + 1, 0)))
        else:
            specs.append(((seq, w), (tm, w), lambda i: (i, 0)))
    return specs


def _conv_window(i, tm, seq):
    rows = pl.ds(pl.multiple_of((i + 1) * tm - CONV_A_HALO, 8), tm + 2 * CONV_A_HALO)
    tok = i * tm - CONV_A_HALO + lax.broadcasted_iota(jnp.int32, (tm + 2 * CONV_A_HALO, 1), 0)
    return rows, jnp.logical_and(tok >= 0, tok < seq)


PRE_KV = (2, 3, 6, 7)


def _transposed_spec(heads, w, seq, tm):
    return (heads, w, seq), (heads, w, tm), lambda i: (0, 0, i)


def pre_fwd(proj, tabs, kp, tag):
    seq = proj.shape[0]
    tm = ROW_TILE
    dts = (F32,) + (MXU_DTYPE,) * 7
    outs = [(shape, dt, block, imap) for (shape, block, imap), dt in zip(_pre_out_specs(seq, tm), dts)]
    return stage_fwd("pre_fwd" + tag, fn_pre, _pre_ops(proj, tabs, kp, False), outs, (seq // tm,))


def pre_bwd(proj, tabs, kp, cts, tag):
    seq = proj.shape[0]
    tm = ROW_TILE
    ct = []
    for j, (c, (_, block, imap)) in enumerate(zip(cts, _pre_out_specs(seq, tm))):
        if isinstance(c, tuple):
            ct.append([c])
        elif j in PRE_KV:
            ct.append([(c,) + _transposed_spec(*PRE_OUTS[j], seq, tm)[1:] + (True,)])
        else:
            ct.append([(c, block, imap)])
    return stage_bwd("pre_bwd" + tag, fn_pre, _pre_ops(proj, tabs, kp, True), ct, (seq // tm,))


def _aconv_ops(kp, grad):
    return [_par_op(kp[n], grad=grad) for n in ('caw', 'cab', 'lag', 'lab')]


def aconv_fwd(aglu_pad, kp, tag):
    tm = ROW_TILE
    seq = aglu_pad.shape[0] - 2 * tm
    n_par = 4

    def body(x_ref, *refs):
        rows, is_token = _conv_window(pl.program_id(0), tm, seq)
        win = jnp.where(is_token, x_ref[rows, :], 0.0)
        (o,) = fn_aconv(win, *[_load(r) for r in refs[:n_par]])
        refs[n_par][...] = o.astype(refs[n_par].dtype)

    pars = _aconv_ops(kp, False)
    return pl.pallas_call(
        body, grid=(seq // tm,),
        in_specs=[pl.BlockSpec(aglu_pad.shape, lambda i: (0, 0))] + [pl.BlockSpec(o.block, o.imap) for o in pars],
        out_specs=pl.BlockSpec((tm, GROUP_W), lambda i: (i, 0)),
        out_shape=jax.ShapeDtypeStruct((seq, GROUP_W), MXU_DTYPE),
        compiler_params=pltpu.CompilerParams(dimension_semantics=("parallel",), vmem_limit_bytes=VMEM_LIMIT),
        name="aconv_fwd" + tag)(aglu_pad, *[o.arr for o in pars])


def aconv_bwd(aglu_pad, kp, d_oa, tag):
    tm = ROW_TILE
    seq = aglu_pad.shape[0] - 2 * tm
    n_par = 4

    def body(x_ref, *refs):
        i = pl.program_id(0)
        rows, is_token = _conv_window(i, tm, seq)
        pars = [_load(r) for r in refs[:n_par]]
        ct = refs[n_par][...].astype(F32)
        outs = refs[n_par + 1:]
        _, vjp = jax.vjp(lambda *a: fn_aconv(*a), jnp.where(is_token, x_ref[rows, :], 0.0), *pars)
        grads = vjp((ct,))

        @pl.when(i == 0)
        def _():
            for r in outs:
                r[...] = jnp.zeros_like(r)

        outs[0][rows, :] += grads[0]
        for r, g in zip(outs[1:], grads[1:]):
            r[...] += g

    pars = _aconv_ops(kp, True)
    whole = pl.BlockSpec(aglu_pad.shape, lambda i: (0, 0))
    par_specs = [pl.BlockSpec(o.block, o.imap) for o in pars]
    return pl.pallas_call(
        body, grid=(seq // tm,),
        in_specs=[whole] + par_specs + [pl.BlockSpec((tm, GROUP_W), lambda i: (i, 0))],
        out_specs=[whole] + par_specs,
        out_shape=[jax.ShapeDtypeStruct(aglu_pad.shape, F32)] + [jax.ShapeDtypeStruct(o.arr.shape, F32) for o in pars],
        compiler_params=pltpu.CompilerParams(dimension_semantics=("arbitrary",), vmem_limit_bytes=VMEM_LIMIT),
        name="aconv_bwd" + tag)(aglu_pad, *[o.arr for o in pars], d_oa)


ATTN_TQ_FWD = 512
ATTN_TQ = 256
ATTN_TK = 512


def attn_fwd(q3, k3, v3t, scale, tag):
    heads, seq, dk = q3.shape
    group = heads // k3.shape[0]
    kv_per_pair = 2 // group
    dv = v3t.shape[1]
    tq, tk = min(ATTN_TQ_FWD, seq), min(ATTN_TK, seq)
    n_chunks = seq // tk
    log2e = math.log2(math.e)

    def one_head(q, k_ref, vt_ref):
        scores = lambda c: _rawdot(k_ref[pl.ds(c * tk, tk), :], q, 1, 1)
        m, l, acc = jnp.full((1, tq), -jnp.inf, F32), jnp.zeros((1, tq), F32), jnp.zeros((dv, tq), F32)
        s_next = scores(0)
        for c in range(n_chunks):
            s_cur, s_next = s_next, (scores(c + 1) if c + 1 < n_chunks else None)
            t = s_cur * (scale * log2e)
            m_new = jnp.maximum(m, jnp.max(t, axis=0, keepdims=True))
            alpha = jnp.exp2(m - m_new)
            p = jnp.exp2(t - m_new)
            l = alpha * l + jnp.sum(p, axis=0, keepdims=True)
            acc = alpha * acc + _rawdot(vt_ref[:, c * tk:(c + 1) * tk], p, 1, 0)
            m = m_new
        return (acc * (1.0 / l)).T, (m * (1.0 / log2e) + jnp.log(l)).T

    def body(q_ref, k_ref, v_ref, o_ref, lse_ref):
        outs = []
        for h in range(2):
            o, lse = one_head(q_ref[h], k_ref.at[h // group], v_ref.at[h // group])
            lse_ref[h] = lse
            outs.append(o)
        o_ref[...] = jnp.concatenate(outs, axis=-1)

    return pl.pallas_call(
        body, grid=(heads // 2, seq // tq),
        in_specs=[pl.BlockSpec((2, tq, dk), lambda j, i: (j, i, 0)),
                  pl.BlockSpec((kv_per_pair, seq, dk), lambda j, i: (j, 0, 0)),
                  pl.BlockSpec((kv_per_pair, dv, seq), lambda j, i: (j, 0, 0))],
        out_specs=[pl.BlockSpec((tq, 2 * dv), lambda j, i: (i, j)),
                   pl.BlockSpec((2, tq, 1), lambda j, i: (j, i, 0))],
        out_shape=[jax.ShapeDtypeStruct((seq, heads * dv), F32), jax.ShapeDtypeStruct((heads, seq, 1), F32)],
        compiler_params=pltpu.CompilerParams(dimension_semantics=("parallel", "parallel"),
                                             vmem_limit_bytes=VMEM_LIMIT),
        name="attn_fwd" + tag)(q3, k3, v3t)


def attn_bwd(q3, k3, v3, o, lse3, do_all, do_col, scale, tag):
    heads, seq, dk = q3.shape
    group = heads // k3.shape[0]
    kv_per_pair = 2 // group
    dv = v3.shape[2]
    tq, tk = min(ATTN_TQ, seq), min(ATTN_TK, seq)
    n_chunks = seq // tk
    log2e = math.log2(math.e)

    def one_head(q, do, o_h, lse, k_ref, v_ref, dk_ref, dv_ref):
        dob = do.astype(MXU_DTYPE)
        do_t, q_t = do.T.astype(MXU_DTYPE), q.astype(F32).T.astype(MXU_DTYPE)
        delta = jnp.sum(do * o_h, axis=-1, keepdims=True)
        lse2 = lse * log2e
        rows = lambda c: pl.ds(c * tk, tk)
        products = lambda c: (_rawdot(q, k_ref[rows(c), :], 1, 1), _rawdot(dob, v_ref[rows(c), :], 1, 1))
        dq = jnp.zeros((tq, dk), F32)
        nxt = products(0)
        for c in range(n_chunks):
            (s_cur, dp_cur), nxt = nxt, (products(c + 1) if c + 1 < n_chunks else None)
            p = jnp.exp2(s_cur * (scale * log2e) - lse2)
            ds = (p * ((dp_cur - delta) * scale)).astype(MXU_DTYPE)
            dv_ref[:, c * tk:(c + 1) * tk] += _rawdot(do_t, p, 1, 0)
            dk_ref[:, c * tk:(c + 1) * tk] += _rawdot(q_t, ds, 1, 0)
            dq = dq + _rawdot(ds, k_ref[rows(c), :], 1, 0)
        return dq

    def body(q_ref, k_ref, v_ref, o_ref, lse_ref, do_ref, dq_ref, dk_ref, dv_ref):
        @pl.when(pl.program_id(1) == 0)
        def _():
            dk_ref[...] = jnp.zeros_like(dk_ref)
            dv_ref[...] = jnp.zeros_like(dv_ref)

        do_pair, o_pair = do_ref[...], o_ref[...]
        for h in range(2):
            kv = h // group
            dq_ref[h] = one_head(q_ref[h], do_pair[:, dv * h:dv * (h + 1)], o_pair[:, dv * h:dv * (h + 1)],
                                 lse_ref[h], k_ref.at[kv], v_ref.at[kv], dk_ref.at[kv], dv_ref.at[kv])

    qspec = lambda d: pl.BlockSpec((2, tq, d), lambda j, i: (j, i, 0))
    kvspec = lambda d: pl.BlockSpec((kv_per_pair, seq, d), lambda j, i: (j, 0, 0))
    kvt_spec = lambda d: pl.BlockSpec((kv_per_pair, d, seq), lambda j, i: (j, 0, 0))
    kvt_shape = lambda a: jax.ShapeDtypeStruct((a.shape[0], a.shape[2], a.shape[1]), F32)
    return pl.pallas_call(
        body, grid=(heads // 2, seq // tq),
        in_specs=[qspec(dk), kvspec(dk), kvspec(dv), pl.BlockSpec((tq, 2 * dv), lambda j, i: (i, j)), qspec(1),
                  pl.BlockSpec((tq, 2 * dv), lambda j, i: (i, do_col + j))],
        out_specs=[qspec(dk), kvt_spec(dk), kvt_spec(dv)],
        out_shape=[jax.ShapeDtypeStruct(q3.shape, F32), kvt_shape(k3), kvt_shape(v3)],
        compiler_params=pltpu.CompilerParams(dimension_semantics=("parallel", "arbitrary"),
                                             vmem_limit_bytes=VMEM_LIMIT),
        name="attn_bwd" + tag)(q3, k3, v3, o, lse3, do_all)


def resln_fwd(h, r, g, b, tag):
    seq, d = h.shape
    tm = min(LN_TILE, seq)
    ops = [_row_op(h, tm), _row_op(r, tm), _par_op(g), _par_op(b)]
    outs = [((seq, d), dt, (tm, d), lambda i: (i, 0)) for dt in (F32, MXU_DTYPE)]
    return stage_fwd("resln_fwd" + tag, _twice(fn_resln), ops, outs, (seq // tm,))


def resln_bwd(h, r, g, b, dys, tag):
    seq, d = h.shape
    tm = min(LN_TILE, seq)
    ops = [_row_op(h, tm, grad=True), _row_op(r, tm, grad=True, gdtype=MXU_DTYPE), _par_op(g, grad=True),
           _par_op(b, grad=True)]
    ct = [[(dy, (tm, d), lambda i: (i, 0)) for dy in dys]]
    return stage_bwd("resln_bwd" + tag, fn_resln, ops, ct, (seq // tm,))


def _ffnconv_ops(up1, up2, w, b, grad):
    seq = up1.shape[0]
    nblk = D_FF // 128
    lo, hi = (lambda j: (0, j)), (lambda j: (0, j + nblk))
    half = lambda a: dict(gshape=(a.shape[0], D_FF), gimap=lo)
    return [Op(up1, (seq, 128), lo, grad=grad, gdtype=MXU_DTYPE), Op(up2, (seq, 128), lo, grad=grad, gdtype=MXU_DTYPE),
            Op(w, (3, 128), lo, grad=grad, **half(w)), Op(w, (3, 128), hi, grad=grad, **half(w)),
            Op(b, (1, 128), lo, grad=grad, **half(b)), Op(b, (1, 128), hi, grad=grad, **half(b))]


def ffnconv_fwd(up1, up2, w, b, tag):
    seq = up1.shape[0]
    outs = [((seq, D_FF), MXU_DTYPE, (seq, 128), lambda j: (0, j))]
    return stage_fwd("ffnconv_fwd" + tag, fn_ffnconv, _ffnconv_ops(up1, up2, w, b, False), outs, (D_FF // 128,))[0]


def ffnconv_bwd(up1, up2, w, b, dact, tag):
    seq = up1.shape[0]
    ct = [[(dact, (seq, 128), lambda j: (0, j))]]
    du1, du2, dw1, dw2, db1, db2 = stage_bwd("ffnconv_bwd" + tag, fn_ffnconv, _ffnconv_ops(up1, up2, w, b, True),
                                             ct, (D_FF // 128,))
    cat = lambda a, b_: jnp.concatenate([a, b_], axis=-1)
    return du1, du2, cat(dw1, dw2), cat(db1, db2)


def final_bwd(h, r, t, g, b, tag):
    seq, d = h.shape
    tm = min(LN_TILE, seq)
    ops = [_row_op(h, tm, grad=True), _row_op(r, tm, grad=True, gdtype=MXU_DTYPE), _row_op(t, tm),
           _par_op(g, grad=True), _par_op(b, grad=True)]
    return stage_bwd("final_bwd" + tag, fn_final, ops, [None], (seq // tm,), value_acc=True)


def _layer_params(wts, l):
    row = lambda a: a.reshape(1, -1)
    wuq = wts['mla_w_uq'][l].reshape(MLA_Q_LORA, MLA_HEADS, MLA_NOPE + MLA_ROPE)
    wuq = jnp.pad(wuq, ((0, CQ_PAD - MLA_Q_LORA), (0, 0), (0, MLA_DK_PAD - MLA_NOPE - MLA_ROPE)))
    wukv = wts['mla_w_ukv'][l].reshape(MLA_KV_LORA, MLA_HEADS, MLA_NOPE + MLA_V)
    wuk = jnp.pad(wukv[:, :, :MLA_NOPE], ((0, 0), (0, 0), (0, MLA_DK_PAD - MLA_NOPE)))
    return dict(
        qng=jnp.tile(row(wts['qk_norm_q'][l]), (1, GQA_HEADS)), kng=jnp.tile(row(wts['qk_norm_k'][l]), (1, GQA_KV_HEADS)),
        sg=row(wts['sgu_ln_g'][l]), sb=row(wts['sgu_ln_b'][l]),
        sw=wts['sgu_w'][l].reshape(SGU_GROUPS * CHUNK, CHUNK), sbt=wts['sgu_b'][l].T,
        mqn=jnp.pad(row(wts['mla_q_norm'][l]), ((0, 0), (0, CQ_PAD - MLA_Q_LORA))),
        wuq=wuq.reshape(CQ_PAD, MLA_HEADS * MLA_DK_PAD), mkvn=row(wts['mla_kv_norm'][l]),
        wukv=jnp.concatenate([wuk.reshape(MLA_KV_LORA, -1), wukv[:, :, MLA_NOPE:].reshape(MLA_KV_LORA, -1)], axis=1),
        caw=wts['conv_a_w'][l], cab=row(wts['conv_a_b'][l]), lag=row(wts['ln_a_g'][l]), lab=row(wts['ln_a_b'][l]),
        lmg=row(wts['ln_mix_g'][l]), lmb=row(wts['ln_mix_b'][l]),
        fcw=wts['ffn_conv_w'][l], fcb=row(wts['ffn_conv_b'][l]),
        lfg=row(wts['ln_ffn_g'][l]), lfb=row(wts['ln_ffn_b'][l]))


def local_step(x, target, ln_in, get_wts, mat, hook):
    seq = x.shape[0]
    tm = min(LN_TILE, seq)
    tabs = _rope_tables(seq)
    scale_b = HEAD_DIM ** -0.5
    scale_d = (MLA_NOPE + MLA_ROPE) ** -0.5
    ln_in_g, ln_in_b = ln_in[0].reshape(1, -1), ln_in[1].reshape(1, -1)

    h, h_m = stage_fwd("ln_in_fwd", _twice(fn_ln), [_row_op(x, tm), _par_op(ln_in_g), _par_op(ln_in_b)],
                       [((seq, D_MODEL), dt, (tm, D_MODEL), lambda i: (i, 0)) for dt in (F32, MXU_DTYPE)],
                       (seq // tm,))
    wts = get_wts(h_m[:8, :128].astype(F32) + tabs[0][:8, :128] + tabs[1][:8, :128])
    saved = []
    for l in range(DEPTH):
        tag = f"_l{l}"
        kp, unprep = jax.vjp(lambda w: _layer_params(w, l), wts)
        m = {'w_in': mat(l, 'w_in', h_m)}
        proj = matmul(h_m, m['w_in'], 'nt', F32, "mm_proj" + tag, tn=PROJ_W, tm_max=512)
        aglu, q3, k3, v3, o_c, qd3, kd3, vd3 = pre_fwd(proj, tabs, kp, tag)
        aglu_pad = aglu
        o_a = aconv_fwd(aglu_pad, kp, tag)
        swap = lambda a: a.transpose(0, 2, 1)
        o_b3, lse_b3 = attn_fwd(q3, k3, swap(v3), scale_b, "_b" + tag)
        o_d3, lse_d3 = attn_fwd(qd3, kd3, swap(vd3), scale_d, "_d" + tag)
        o_cat = jnp.concatenate([o_a, o_b3.astype(MXU_DTYPE), o_c, o_d3.astype(MXU_DTYPE)], axis=-1)
        m['w_out'] = mat(l, 'w_out', o_cat)
        mix = matmul(o_cat, m['w_out'], 'nn', F32, "mm_mix" + tag, tn=D_MODEL, tm_max=512)
        h1, h1_m = resln_fwd(h, mix, kp['lmg'], kp['lmb'], "_mix" + tag)
        m['ffn_w_up'] = mat(l, 'ffn_w_up', h1_m)
        wide = dict(tn=D_FF, tm_max=512)
        up1 = matmul(h1_m, m['ffn_w_up'], 'nt', F32, "mm_up1" + tag, b_rows=(0, D_FF), **wide)
        up2 = matmul(h1_m, m['ffn_w_up'], 'nt', F32, "mm_up2" + tag, b_rows=(D_FF, D_FF), **wide)
        act = ffnconv_fwd(up1, up2, kp['fcw'], kp['fcb'], tag)
        m['ffn_w_down'] = mat(l, 'ffn_w_down', act)
        f = matmul(act, m['ffn_w_down'], 'nn', F32, "mm_down" + tag, tn=D_MODEL, tm_max=512)
        saved.append(dict(kp=kp, unprep=unprep, m=m, h=h, h_m=h_m, h1_m=h1_m, proj=proj, o_b3=o_b3, lse_b3=lse_b3,
                          o_d3=o_d3, lse_d3=lse_d3, aglu_pad=aglu_pad, q3=q3, k3=k3, v3=v3, qd3=qd3,
                          kd3=kd3, vd3=vd3, o_cat=o_cat, mix=mix, h1=h1, up1=up1, up2=up2, act=act, f=f))
        if l + 1 < DEPTH:
            h, h_m = resln_fwd(h1, f, kp['lfg'], kp['lfb'], "_ffn" + tag)

    after = lambda a, tok: a if tok is None else a + tok
    small_acc = None
    dh_parts = None
    loss = None
    tok = None
    g_mix = None
    for l in reversed(range(DEPTH)):
        tag = f"_l{l}"
        s = saved[l]
        kp, m = s['kp'], s['m']
        dkp = {}
        lfg = after(kp['lfg'], tok)
        if l == DEPTH - 1:
            dh1_a, df, dkp['lfg'], dkp['lfb'], loss = final_bwd(s['h1'], s['f'], target, lfg, kp['lfb'], tag)
        else:
            dh1_a, df, dkp['lfg'], dkp['lfb'] = resln_bwd(s['h1'], s['f'], lfg, kp['lfb'], dh_parts, "_ffn" + tag)
        g_down = matmul(s['act'], df, 'tn', COMM_DTYPE, "mm_gdown" + tag, tn=D_MODEL)
        dact = matmul(df, m['ffn_w_down'], 'nt', F32, "mm_dact" + tag, tn=D_FF, tm_max=512)
        dup1, dup2, dkp['fcw'], dkp['fcb'] = ffnconv_bwd(s['up1'], s['up2'], kp['fcw'], kp['fcb'], dact, tag)
        dh1 = matmul([dup1, dup2], [m['ffn_w_up']] * 2, 'nn', F32, "mm_dh1" + tag,
                     b_rows=[(0, D_FF), (D_FF, D_FF)], add=[dh1_a], tn=D_MODEL, tm_max=512)
        g_up = matmul(dup1, s['h1_m'], 'tn', COMM_DTYPE, "mm_gup1" + tag, tn=D_MODEL,
                      into=(lax.empty((2 * D_FF, D_MODEL), COMM_DTYPE), 0))
        g_up = matmul(dup2, s['h1_m'], 'tn', COMM_DTYPE, "mm_gup2" + tag, tn=D_MODEL, into=(g_up, D_FF))
        tok = hook(f"ffn{l}", {('ffn_w_down', l): g_down, ('ffn_w_up', l): g_up})
        dh_a, dmix, dkp['lmg'], dkp['lmb'] = resln_bwd(s['h'], s['mix'], after(kp['lmg'], tok), kp['lmb'],
                                                       [dh1], "_mix" + tag)
        g_out = matmul(s['o_cat'], dmix, 'tn', COMM_DTYPE, "mm_gout" + tag, tn=D_MODEL)
        w_out = m['w_out']
        if l == 0:
            w_out = w_out + hook("out0", {('w_out', l): g_out}).astype(w_out.dtype)
        do_cat = matmul(dmix, w_out, 'nt', F32, "mm_docat" + tag, tn=D_MODEL, tm_max=512)
        lse_b3 = s['lse_b3']
        do_c = (do_cat, (ROW_TILE, GROUP_W), lambda i: (i, 2))
        pair_w = 2 * HEAD_DIM
        dq3, dk3, dv3 = attn_bwd(s['q3'], s['k3'], s['v3'], s['o_b3'], lse_b3, do_cat, GROUP_W // pair_w,
                                 scale_b, "_b" + tag)
        dqd3, dkd3, dvd3 = attn_bwd(s['qd3'], s['kd3'], s['vd3'], s['o_d3'], s['lse_d3'], do_cat,
                                    3 * GROUP_W // pair_w, scale_d, "_d" + tag)
        daglu_pad, dkp['caw'], dkp['cab'], dkp['lag'], dkp['lab'] = aconv_bwd(s['aglu_pad'], kp, do_cat, tag)
        cts = [daglu_pad, dq3, dk3, dv3, do_c, dqd3, dkd3, dvd3]
        pre_g = pre_bwd(s['proj'], tabs, kp, cts, tag)
        dproj = pre_g[0]
        for n, g in zip(('qng', 'kng', 'sg', 'sb', 'sw', 'sbt', 'mqn', 'wuq', 'mkvn', 'wukv'), pre_g[1:]):
            dkp[n] = g
        dh_parts = [matmul(dproj, m['w_in'], 'nn', F32, "mm_dh" + tag, add=[dh_a], tn=D_MODEL,
                           tm_max=512)]
        g_in = matmul(dproj, s['h_m'], 'tn', COMM_DTYPE, "mm_gin" + tag, tn=D_MODEL)
        (dw,) = s['unprep'](dkp)
        small_acc = dw if small_acc is None else jax.tree.map(jnp.add, small_acc, dw)
        g_mix = {('w_out', l): g_out, ('w_in', l): _unpad_w_in(g_in)}
        if l > 0:
            tok = hook(f"mix{l}", g_mix)
        else:
            g_mix.pop(('w_out', l))

    g_mix.update({(n, None): small_acc[n] for n in SHARDED if n not in MATMUL_WEIGHTS})
    tok = hook("last", g_mix)
    dx, dg, db = stage_bwd("ln_in_bwd", fn_ln,
                           [_row_op(x, tm, grad=True), _par_op(after(ln_in_g, tok), grad=True),
                            _par_op(ln_in_b, grad=True)],
                           [[(p, (tm, D_MODEL), lambda i: (i, 0)) for p in dh_parts]], (seq // tm,))
    out = {n: small_acc[n] for n in REPLICATED}
    out['ln_in_g'], out['ln_in_b'] = dg.reshape(-1), db.reshape(-1)
    return loss, dx, out


def _peer(x, y, c, r):
    return ((1 - x) if r & 4 else x, (1 - y) if r & 2 else y, (1 - c) if r & 1 else c)


def _exchange_copy(src_ref, land_ref, send_sems, recv_sems, k, gather, x, y, c, r):
    px, py, pc = _peer(x, y, c, r)
    me, peer = 4 * x + 2 * y + c, 4 * px + 2 * py + pc
    src = src_ref if gather else src_ref.at[peer]
    mk = lambda dst: pltpu.make_async_remote_copy(
        src_ref=src, dst_ref=dst, send_sem=send_sems.at[k * (N_DEV - 1) + r - 1],
        recv_sem=recv_sems.at[k * (N_DEV - 1) + r - 1],
        device_id=(px, py, pc), device_id_type=pl.DeviceIdType.MESH)
    return mk(land_ref.at[me]), mk(land_ref.at[peer])


_HBM_SPEC = pl.BlockSpec(memory_space=pltpu.HBM)
_SEM_SPEC = pl.BlockSpec(memory_space=pltpu.SEMAPHORE)


def exchange_start(srcs, gather, groups, name):
    n_t = len(srcs)
    lands =[lax.empty(((N_DEV,) + s.shape) if gt else s.shape, s.dtype) for s, gt in zip(srcs, gather)]

    def body(*refs):
        src_refs, land_refs = refs[:n_t], refs[n_t:2 * n_t]
        sem_refs = refs[2 * n_t:2 * n_t + 2 * len(groups)]
        token = refs[-1]
        x, y, c = lax.axis_index("x"), lax.axis_index("y"), lax.axis_index("c")
        for gi, g in enumerate(groups):
            for k, t in enumerate(g):
                for r in range(1, N_DEV):
                    _exchange_copy(src_refs[t], land_refs[t], sem_refs[2 * gi], sem_refs[2 * gi + 1], k, gather[t],
                                   x, y, c, r)[0].start()
        token[...] = jnp.zeros_like(token)

    sem_shapes = []
    for g in groups:
        sem_shapes += [pltpu.SemaphoreType.DMA((len(g) * (N_DEV - 1),))] * 2
    hbm_shapes = [pltpu.HBM(a.shape, a.dtype) for a in list(srcs) + lands]
    n_sem = len(sem_shapes)
    res = pl.pallas_call(
        body, name=name,
        out_shape=tuple(sem_shapes + hbm_shapes + [jax.ShapeDtypeStruct((8, 128), F32)]),
        in_specs=[_HBM_SPEC] * (2 * n_t),
        out_specs=tuple([_SEM_SPEC] * n_sem + [_HBM_SPEC] * (2 * n_t) + [pl.BlockSpec(memory_space=pltpu.VMEM)]),
        input_output_aliases={i: n_sem + i for i in range(2 * n_t)},
        compiler_params=pltpu.CompilerParams(has_side_effects=pltpu.SideEffectType.DATAFLOW_SIDE_EFFECTING),
    )(*[pltpu.with_memory_space_constraint(a, pltpu.HBM) for a in list(srcs) + lands])
    sems = [(res[2 * gi], res[2 * gi + 1]) for gi in range(len(groups))]
    return sems, list(res[n_sem:n_sem + n_t]), list(res[n_sem + n_t:n_sem + 2 * n_t]), res[-1]


def exchange_wait(sems, srcs, lands, gather, after, name):
    n_t = len(srcs)

    def body(*refs):
        src_refs, land_refs = refs[:n_t], refs[n_t:2 * n_t]
        send_sems, recv_sems = refs[2 * n_t], refs[2 * n_t + 1]
        x, y, c = lax.axis_index("x"), lax.axis_index("y"), lax.axis_index("c")
        for k in range(n_t):
            for r in range(1, N_DEV):
                send, recv = _exchange_copy(src_refs[k], land_refs[k], send_sems, recv_sems, k, gather[k], x, y, c, r)
                send.wait_send()
                recv.wait_recv()

    res = pl.pallas_call(
        body, name=name,
        out_shape=tuple(pltpu.HBM(a.shape, a.dtype) for a in list(srcs) + list(lands)),
        in_specs=[_HBM_SPEC] * (2 * n_t) + [_SEM_SPEC, _SEM_SPEC, pl.BlockSpec(memory_space=pl.ANY)],
        out_specs=tuple([_HBM_SPEC] * (2 * n_t)),
        input_output_aliases={i: i for i in range(2 * n_t)},
        compiler_params=pltpu.CompilerParams(has_side_effects=pltpu.SideEffectType.DATAFLOW_SIDE_EFFECTING),
    )(*srcs, *lands, sems[0], sems[1], after)
    return list(res[:n_t]), list(res[n_t:])


def adamw(parts, w, m, v, name):
    n_l, n_r, n_c = w.shape
    tr = n_r
    if n_r % 8 == 0:
        for cand in (512, 256, 128, 64, 32, 16, 8):
            if n_r % cand == 0 and cand * n_c * 4 <= 512 * 1024:
                tr = cand
                break
    c1 = 1.0 - ADAM_B1 ** ADAM_STEP
    c2 = 1.0 - ADAM_B2 ** ADAM_STEP
    per_layer = isinstance(parts, (list, tuple))
    n_p = n_l if per_layer else 1
    n_rb = n_r // tr

    def update(g, w_ref, m_ref, v_ref, g_ref, d_ref, nm_ref, nv_ref):
        w_, m_, v_ = w_ref[0], m_ref[0], v_ref[0]
        nm = ADAM_B1 * m_ + (1.0 - ADAM_B1) * g
        nv = ADAM_B2 * v_ + (1.0 - ADAM_B2) * (g * g)
        g_ref[0] = g
        nm_ref[0] = nm
        nv_ref[0] = nv
        d_ref[0] = -ADAM_LR * ((nm / c1) / (jnp.sqrt(nv / c2) + ADAM_EPS) + ADAM_WD * w_)

    def body(*refs):
        p_refs, rest = refs[:n_p], refs[n_p:]
        if not per_layer:
            g = p_refs[0][0, 0].astype(F32)
            for s in range(1, N_DEV):
                g = g + p_refs[0][s, 0].astype(F32)
            update(g, *rest)
        else:
            for lay in range(n_l):
                @pl.when(pl.program_id(0) == lay)
                def _(lay=lay):
                    g = p_refs[lay][0].astype(F32)
                    for s in range(1, N_DEV):
                        g = g + p_refs[lay][s].astype(F32)
                    update(g, *rest)

    blk = pl.BlockSpec((1, tr, n_c), lambda l, r: (l, r, 0))
    if per_layer:
        def p_spec(lay):
            park = 0 if lay > 0 else n_rb - 1
            return pl.BlockSpec((N_DEV, tr, n_c), lambda l, r: (0, jnp.where(l == lay, r, park), 0))
        p_specs, p_args = [p_spec(lay) for lay in range(n_l)], list(parts)
    else:
        p_specs, p_args = [pl.BlockSpec((N_DEV, 1, tr, n_c), lambda l, r: (0, l, r, 0))], [parts]
    return pl.pallas_call(
        body, grid=(n_l, n_rb), in_specs=p_specs + [blk, blk, blk],
        out_specs=[blk] * 4, out_shape=[jax.ShapeDtypeStruct(w.shape, F32)] * 4,
        compiler_params=pltpu.CompilerParams(dimension_semantics=("arbitrary", "arbitrary"),
                                             vmem_limit_bytes=VMEM_LIMIT),
        name=name)(*p_args, w, m, v)


def adamw_replicated(lands, own, ws, ms, vs, loss_land, loss_own):
    n_t = len(lands)
    c1 = 1.0 - ADAM_B1 ** ADAM_STEP
    c2 = 1.0 - ADAM_B2 ** ADAM_STEP

    def body(*refs):
        ins, outs = refs[:5 * n_t + 2], refs[5 * n_t + 2:]
        me = 4 * lax.axis_index("x") + 2 * lax.axis_index("y") + lax.axis_index("c")

        def total(land_ref, own_ref):
            g = None
            for s in range(N_DEV):
                term = jnp.where(me == s, own_ref[...], land_ref[s])
                g = term if g is None else g + term
            return g

        for t in range(n_t):
            land_ref, own_ref, w_ref, m_ref, v_ref = ins[5 * t:5 * t + 5]
            g = total(land_ref, own_ref)
            nm = ADAM_B1 * m_ref[...] + (1.0 - ADAM_B1) * g
            nv = ADAM_B2 * v_ref[...] + (1.0 - ADAM_B2) * (g * g)
            g_ref, d_ref, nm_ref, nv_ref = outs[4 * t:4 * t + 4]
            g_ref[...] = g
            nm_ref[...] = nm
            nv_ref[...] = nv
            d_ref[...] = -ADAM_LR * ((nm / c1) / (jnp.sqrt(nv / c2) + ADAM_EPS) + ADAM_WD * w_ref[...])
        outs[4 * n_t][...] = total(ins[5 * n_t], ins[5 * n_t + 1])

    args = []
    for t in range(n_t):
        args += [lands[t], own[t], ws[t], ms[t], vs[t]]
    out_shape = []
    for t in range(n_t):
        out_shape += [jax.ShapeDtypeStruct(ws[t].shape, F32)] * 4
    out_shape.append(jax.ShapeDtypeStruct(loss_own.shape, F32))
    res = pl.pallas_call(body, out_shape=out_shape,
                         compiler_params=pltpu.CompilerParams(vmem_limit_bytes=VMEM_LIMIT),
                         name="adamw_replicated")(*args, loss_land, loss_own)
    return [tuple(res[4 * t:4 * t + 4]) for t in range(n_t)], res[-1]


def _shard_slots(g, axis):
    if axis == 1:
        return g.reshape(g.shape[0], N_DEV, g.shape[1] // N_DEV, g.shape[2]).transpose(1, 0, 2, 3)
    return g.reshape(g.shape[0], g.shape[1], N_DEV, g.shape[2] // N_DEV).transpose(2, 0, 1, 3)


def _unshard(slots, axis):
    if axis == 1:
        return slots.transpose(1, 0, 2, 3).reshape(slots.shape[1], -1, slots.shape[3])
    return slots.transpose(1, 2, 0, 3).reshape(slots.shape[1], slots.shape[2], -1)


def kernel(x, ln_in_g, ln_in_b, w_in, conv_a_w, conv_a_b, ln_a_g, ln_a_b, qk_norm_q, qk_norm_k, sgu_ln_g, sgu_ln_b, sgu_w, sgu_b, mla_q_norm, mla_w_uq, mla_kv_norm, mla_w_ukv, w_out, ln_mix_g, ln_mix_b, ffn_w_up, ffn_conv_w, ffn_conv_b, ffn_w_down, ln_ffn_g, ln_ffn_b, loss_target, m_ln_in_g, m_ln_in_b, m_w_in, m_conv_a_w, m_conv_a_b, m_ln_a_g, m_ln_a_b, m_qk_norm_q, m_qk_norm_k, m_sgu_ln_g, m_sgu_ln_b, m_sgu_w, m_sgu_b, m_mla_q_norm, m_mla_w_uq, m_mla_kv_norm, m_mla_w_ukv, m_w_out, m_ln_mix_g, m_ln_mix_b, m_ffn_w_up, m_ffn_conv_w, m_ffn_conv_b, m_ffn_w_down, m_ln_ffn_g, m_ln_ffn_b, v_ln_in_g, v_ln_in_b, v_w_in, v_conv_a_w, v_conv_a_b, v_ln_a_g, v_ln_a_b, v_qk_norm_q, v_qk_norm_k, v_sgu_ln_g, v_sgu_ln_b, v_sgu_w, v_sgu_b, v_mla_q_norm, v_mla_w_uq, v_mla_kv_norm, v_mla_w_ukv, v_w_out, v_ln_mix_g, v_ln_mix_b, v_ffn_w_up, v_ffn_conv_w, v_ffn_conv_b, v_ffn_w_down, v_ln_ffn_g, v_ln_ffn_b):
    local = dict(ln_in_g=ln_in_g, ln_in_b=ln_in_b, w_in=w_in, conv_a_w=conv_a_w, conv_a_b=conv_a_b, ln_a_g=ln_a_g, ln_a_b=ln_a_b, qk_norm_q=qk_norm_q, qk_norm_k=qk_norm_k, sgu_ln_g=sgu_ln_g, sgu_ln_b=sgu_ln_b, sgu_w=sgu_w, sgu_b=sgu_b, mla_q_norm=mla_q_norm, mla_w_uq=mla_w_uq, mla_kv_norm=mla_kv_norm, mla_w_ukv=mla_w_ukv, w_out=w_out, ln_mix_g=ln_mix_g, ln_mix_b=ln_mix_b, ffn_w_up=ffn_w_up, ffn_conv_w=ffn_conv_w, ffn_conv_b=ffn_conv_b, ffn_w_down=ffn_w_down, ln_ffn_g=ln_ffn_g, ln_ffn_b=ln_ffn_b)
    mom = dict(ln_in_g=m_ln_in_g, ln_in_b=m_ln_in_b, w_in=m_w_in, conv_a_w=m_conv_a_w, conv_a_b=m_conv_a_b, ln_a_g=m_ln_a_g, ln_a_b=m_ln_a_b, qk_norm_q=m_qk_norm_q, qk_norm_k=m_qk_norm_k, sgu_ln_g=m_sgu_ln_g, sgu_ln_b=m_sgu_ln_b, sgu_w=m_sgu_w, sgu_b=m_sgu_b, mla_q_norm=m_mla_q_norm, mla_w_uq=m_mla_w_uq, mla_kv_norm=m_mla_kv_norm, mla_w_ukv=m_mla_w_ukv, w_out=m_w_out, ln_mix_g=m_ln_mix_g, ln_mix_b=m_ln_mix_b, ffn_w_up=m_ffn_w_up, ffn_conv_w=m_ffn_conv_w, ffn_conv_b=m_ffn_conv_b, ffn_w_down=m_ffn_w_down, ln_ffn_g=m_ln_ffn_g, ln_ffn_b=m_ln_ffn_b)
    var = dict(ln_in_g=v_ln_in_g, ln_in_b=v_ln_in_b, w_in=v_w_in, conv_a_w=v_conv_a_w, conv_a_b=v_conv_a_b, ln_a_g=v_ln_a_g, ln_a_b=v_ln_a_b, qk_norm_q=v_qk_norm_q, qk_norm_k=v_qk_norm_k, sgu_ln_g=v_sgu_ln_g, sgu_ln_b=v_sgu_ln_b, sgu_w=v_sgu_w, sgu_b=v_sgu_b, mla_q_norm=v_mla_q_norm, mla_w_uq=v_mla_w_uq, mla_kv_norm=v_mla_kv_norm, mla_w_ukv=v_mla_w_ukv, w_out=v_w_out, ln_mix_g=v_ln_mix_g, ln_mix_b=v_ln_mix_b, ffn_w_up=v_ffn_w_up, ffn_conv_w=v_ffn_conv_w, ffn_conv_b=v_ffn_conv_b, ffn_w_down=v_ffn_w_down, ln_ffn_g=v_ln_ffn_g, ln_ffn_b=v_ln_ffn_b)

    me = 4 * lax.axis_index("x") + 2 * lax.axis_index("y") + lax.axis_index("c")

    def own_slot(slots, block):
        return lax.dynamic_update_slice(slots, block[None], (me,) + (0,) * block.ndim)

    small_sharded = [n for n in SHARDED if n not in MATMUL_WEIGHTS]
    big_order = [(n, l) for l in range(DEPTH) for n in MATMUL_WEIGHTS]
    send_view = lambda n, l: (local[n].transpose(0, 2, 1)[l] if n in TRANSPOSED else local[n][l]).astype(COMM_DTYPE)
    srcs = [send_view(*big_order[0])] + [local[n] for n in small_sharded]
    srcs += [send_view(n, l) for (n, l) in big_order[1:]]
    n_first = 1 + len(small_sharded)
    groups = [list(range(n_first))] + [[n_first + j] for j in range(len(big_order) - 1)]
    g_sems, g_srcs, g_lands, tok0 = exchange_start(srcs, [True] * len(srcs), groups, "gather_start")
    tok0 = tok0[0, 0]
    pending = {key: gi for gi, key in enumerate(big_order)}

    def finish(gi, after):
        idx = groups[gi]
        _, lands = exchange_wait(g_sems[gi], [g_srcs[t] for t in idx], [g_lands[t] for t in idx], [True] * len(idx),
                                 after, f"gather_wait{gi}")
        return [own_slot(ld, srcs[t]) for ld, t in zip(lands, idx)]

    first = []

    opt_view = {n: tuple(a.transpose(0, 2, 1) for a in (local[n], mom[n], var[n])) for n in TRANSPOSED}

    def get_wts(after):
        for views in opt_view.values():
            after = after + sum(a[0, :8, :128] for a in views)
        first.extend(finish(0, after))
        wts = {n: local[n] for n in REPLICATED}
        for n, slots in zip(small_sharded, first[1:]):
            wts[n] = _unshard(slots, SHARDED[n])
        return wts

    def unshard_layer(slots, n):
        if SHARDED[n] == 1 or n in TRANSPOSED:
            return slots.reshape(-1, slots.shape[2])
        return slots.transpose(1, 0, 2).reshape(slots.shape[1], -1)

    def mat(l, n, after):
        gi = pending[(n, l)]
        slots = first[0] if gi == 0 else finish(gi, after)[0]
        w = unshard_layer(slots, n).astype(MXU_DTYPE)
        return _pad_w_in(w) if n == 'w_in' else w

    started = []

    def hook(key, grads):
        tensors = []
        for (n, l), g in grads.items():
            if l is None:
                tensors.append(((n, l), _shard_slots(g, SHARDED[n])))
            elif n in TRANSPOSED:
                tensors.append(((n, l), g.reshape(N_DEV, g.shape[0] // N_DEV, g.shape[1]).astype(COMM_DTYPE)))
            else:
                tensors.append(((n, l), _shard_slots(g[None], SHARDED[n])[:, 0].astype(COMM_DTYPE)))
        sems, s_srcs, s_lands, tok = exchange_start([a for _, a in tensors], [False] * len(tensors),
                                                    [list(range(len(tensors)))], "scatter_start_" + key)
        started.append((key, [k for k, _ in tensors], sems[0], s_srcs, s_lands))
        return tok[0, 0]

    loss, dx, grads = local_step(x[0], loss_target[0], (local['ln_in_g'] + tok0, local['ln_in_b']), get_wts, mat, hook)

    as2d = lambda a: a.reshape(-1, a.shape[-1]) if a.ndim > 1 else a.reshape(1, -1)
    small_g = [as2d(grads[n]) for n in REPLICATED] + [jnp.broadcast_to(loss, (8, 128))]
    p_sems, p_srcs, p_lands, p_tok = exchange_start(small_g, [True] * len(small_g), [list(range(len(small_g)))],
                                                    "gather_small_start")

    parts, res = {}, {}

    def finish_scatter(entries, after):
        for key, keys, sems, s_srcs, s_lands in entries:
            s_out, lands = exchange_wait(sems, s_srcs, s_lands, [False] * len(keys), after, "scatter_wait_" + key)
            for k, so, ld in zip(keys, s_out, lands):
                parts[k] = own_slot(ld, lax.dynamic_index_in_dim(so, me, 0, keepdims=False))

    def update(names_):
        for n in names_:
            p = [parts[(n, l)] for l in range(DEPTH)] if n in MATMUL_WEIGHTS else parts[(n, None)]
            if n in TRANSPOSED:
                res[n] = tuple(a.transpose(0, 2, 1) for a in adamw(p, *opt_view[n], "adamw_" + n))
            else:
                res[n] = adamw(p, local[n], mom[n], var[n], "adamw_" + n)

    early = ('ffn_w_up', 'ffn_w_down', 'w_out')
    finish_scatter([e for e in started if e[0] != "last"], p_tok)
    update(early)
    finish_scatter([e for e in started if e[0] == "last"], res[early[-1]][1])
    update([n for n in SHARDED if n not in early])
    updated = jnp.zeros((8, 128), F32) + sum(res[n][1][0, 0, 0] for n in SHARDED)
    p_own, p_lands = exchange_wait(p_sems[0], p_srcs, p_lands, [True] * len(small_g), updated, "gather_small_wait")
    small, loss_sum = adamw_replicated(p_lands[:-1], p_own[:-1], [as2d(local[n]) for n in REPLICATED],
                                       [as2d(mom[n]) for n in REPLICATED], [as2d(var[n]) for n in REPLICATED],
                                       p_lands[-1], p_own[-1])
    for n, quad in zip(REPLICATED, small):
        res[n] = tuple(a.reshape(local[n].shape) for a in quad)
    loss_total = loss_sum[0, 0]

    return (loss_total, dx[None], *[res[n][0] for n in WEIGHTS], *[res[n][1] for n in WEIGHTS],
            *[res[n][2] for n in WEIGHTS], *[res[n][3] for n in WEIGHTS])
```

```python
import functools
import math

import jax
import jax.numpy as jnp
from jax import lax
from jax.experimental import pallas as pl
from jax.experimental.pallas import tpu as pltpu

F32 = jnp.float32
MXU_DTYPE = jnp.bfloat16
COMM_DTYPE = jnp.bfloat16

N_DEV = 8
D_MODEL = 1024
DEPTH = 2
GRID_W = 64
GROUP_W = 256
HEAD_DIM = 64
CONV_A_WIDTH = 31
CONV_A_HALO = 16
GQA_HEADS = 4
GQA_KV_HEADS = 2
CHUNK = 128
SGU_GROUPS = 4
MLA_HEADS = 4
MLA_Q_LORA = 192
MLA_KV_LORA = 128
MLA_NOPE = 64
MLA_ROPE = 32
MLA_V = 64
MLA_DK_PAD = 128
ROPE_THETA = 10000.0
D_FF = 2816
DEEPNORM_ALPHA = (2 * DEPTH) ** 0.25
LN_EPS = 1e-5
RMS_EPS = 1e-6
D_IN_PROJ = 1888

ADAM_LR = 0.001
ADAM_B1 = 0.9
ADAM_B2 = 0.999
ADAM_EPS = 1e-08
ADAM_WD = 0.01
ADAM_STEP = 10

WEIGHTS = ['ln_in_g', 'ln_in_b', 'w_in', 'conv_a_w', 'conv_a_b', 'ln_a_g', 'ln_a_b', 'qk_norm_q', 'qk_norm_k',
           'sgu_ln_g', 'sgu_ln_b', 'sgu_w', 'sgu_b', 'mla_q_norm', 'mla_w_uq', 'mla_kv_norm', 'mla_w_ukv', 'w_out',
           'ln_mix_g', 'ln_mix_b', 'ffn_w_up', 'ffn_conv_w', 'ffn_conv_b', 'ffn_w_down', 'ln_ffn_g', 'ln_ffn_b']
SHARDED = {'w_in': 2, 'conv_a_w': 2, 'mla_w_uq': 2, 'mla_w_ukv': 2, 'w_out': 1, 'ffn_w_up': 2, 'ffn_conv_w': 2,
           'ffn_w_down': 1}
MATMUL_WEIGHTS = ('w_in', 'w_out', 'ffn_w_up', 'ffn_w_down')
TRANSPOSED = ('w_in', 'ffn_w_up')
REPLICATED = [n for n in WEIGHTS if n not in SHARDED]

ROW_TILE = 256
LN_TILE = 512
VMEM_LIMIT = 56 * 1024 * 1024


def _rawdot(a, b, ca, cb):
    return lax.dot_general(a.astype(MXU_DTYPE), b.astype(MXU_DTYPE), (((ca,), (cb,)), ((), ())),
                           preferred_element_type=F32)


@jax.custom_vjp
def mm_nn(a, b):
    return _rawdot(a, b, 1, 0)


def _mm_nn_fwd(a, b):
    return _rawdot(a, b, 1, 0), (a, b)


def _mm_nn_bwd(res, dy):
    a, b = res
    return _rawdot(dy, b, 1, 1), _rawdot(a, dy, 0, 0)


mm_nn.defvjp(_mm_nn_fwd, _mm_nn_bwd)


@jax.custom_vjp
def mm_nt(a, b):
    return _rawdot(a, b, 1, 1)


def _mm_nt_fwd(a, b):
    return _rawdot(a, b, 1, 1), (a, b)


def _mm_nt_bwd(res, dy):
    a, b = res
    return _rawdot(dy, b, 1, 0), _rawdot(dy, a, 0, 0)


mm_nt.defvjp(_mm_nt_fwd, _mm_nt_bwd)


def _pick_tile(d, cands):
    for c in cands:
        if d % c == 0:
            return c
    return d


def matmul(a, b, mode, out_dtype, name, b_rows=None, into=None, add=(), tm_max=1408, tn=None):
    a_list = list(a) if isinstance(a, (list, tuple)) else [a]
    b_list = list(b) if isinstance(b, (list, tuple)) else [b]
    n_p = len(a_list)
    rows_list = [b_rows] if not isinstance(a, (list, tuple)) else (list(b_rows) if b_rows is not None else [None] * n_p)
    b_start, b_size = zip(*[(0, bb.shape[0]) if r is None else r for bb, r in zip(b_list, rows_list)])
    a0, b0 = a_list[0], b_list[0]
    if mode == 'nn':
        (m, k), (k2, n) = a0.shape, (b_size[0], b0.shape[1])
    elif mode == 'nt':
        (m, k), (n, k2) = a0.shape, (b_size[0], b0.shape[1])
    else:
        (k, m), (k2, n) = a0.shape, (b_size[0], b0.shape[1])
    assert k == k2 and all(x.shape == a0.shape for x in a_list) and len(set(b_size)) == 1, (a0.shape, b0.shape, mode)
    tm = _pick_tile(m, tuple(c for c in (1024, 1408, 512, 256, 128) if c <= tm_max))
    tn = _pick_tile(n, (512, 1408, 256, 128)) if tn is None else tn
    assert n % tn == 0, (n, tn)
    tk = _pick_tile(k, (2816, 2048, 1024, 512, 256, 128))
    nk = k // tk
    ca = 0 if mode == 'tn' else 1
    cb = 1 if mode == 'nt' else 0
    b_blk = tn if mode == 'nt' else tk
    assert all(s % b_blk == 0 for s in b_start), (b_rows, b_blk)
    a_spec = pl.BlockSpec((tk, tm), lambda i, j, kk: (kk, i)) if mode == 'tn' else pl.BlockSpec((tm, tk), lambda i, j, kk: (i, kk))

    def b_spec(off):
        if mode == 'nt':
            return pl.BlockSpec((tn, tk), lambda i, j, kk: (j + off, kk))
        return pl.BlockSpec((tk, tn), lambda i, j, kk: (kk + off, j))

    in_specs, args, aliases = [], [], {}
    for x, y, s in zip(a_list, b_list, b_start):
        in_specs += [a_spec, b_spec(s // b_blk)]
        args += [x, y]
    in_specs += [pl.BlockSpec((tm, tn), lambda i, j, kk: (i, j))] * len(add)
    args += list(add)
    out_off, out_shape = 0, jax.ShapeDtypeStruct((m, n), out_dtype)
    if into is not None:
        buf, row = into
        assert row % tm == 0 and buf.shape[1] == n and buf.dtype == out_dtype, (buf.shape, row, tm)
        out_off, out_shape = row // tm, jax.ShapeDtypeStruct(buf.shape, buf.dtype)
        aliases = {len(args): 0}
        in_specs, args = in_specs + [pl.BlockSpec(memory_space=pl.ANY)], args + [buf]
    n_add = len(add)

    def body(*refs):
        o_ref, acc_ref = refs[-2:]
        kk = pl.program_id(2)

        @pl.when(kk == 0)
        def _():
            acc_ref[...] = jnp.zeros_like(acc_ref)

        for p in range(n_p):
            acc_ref[...] += _rawdot(refs[2 * p][...], refs[2 * p + 1][...], ca, cb)

        @pl.when(kk == nk - 1)
        def _():
            total = acc_ref[...]
            for r in refs[2 * n_p:2 * n_p + n_add]:
                total = total + r[...]
            o_ref[...] = total.astype(o_ref.dtype)

    return pl.pallas_call(
        body, grid=(m // tm, n // tn, nk), in_specs=in_specs,
        out_specs=pl.BlockSpec((tm, tn), lambda i, j, kk: (i + out_off, j)),
        out_shape=out_shape, input_output_aliases=aliases,
        scratch_shapes=[pltpu.VMEM((tm, tn), F32)],
        compiler_params=pltpu.CompilerParams(dimension_semantics=("parallel", "parallel", "arbitrary"),
                                             vmem_limit_bytes=VMEM_LIMIT),
        name=name)(*args)


class Op:
    def __init__(self, arr, block, imap, grad=False, acc=False, first=None, gdtype=F32, gshape=None, gimap=None):
        self.arr, self.block, self.imap = arr, block, imap
        self.grad, self.acc, self.first, self.gdtype = grad, acc, first, gdtype
        self.gshape = arr.shape if gshape is None else gshape
        self.gimap = imap if gimap is None else gimap


def _row_op(arr, tm, grad=False, gdtype=F32):
    return Op(arr, (tm, arr.shape[1]), lambda i: (i, 0), grad=grad, gdtype=gdtype)


def _par_op(arr, grad=False):
    nd = arr.ndim
    return Op(arr, arr.shape, lambda i: (0,) * nd, grad=grad, acc=True, first=lambda ids: ids[0] == 0)


def _load(ref):
    v = ref[...]
    return v.astype(F32) if jnp.issubdtype(v.dtype, jnp.floating) else v


def _store_heads(ref, val):
    rows = val.shape[0]
    if len(ref.shape) == 2:
        ref[...] = val.astype(ref.dtype)
    elif ref.shape[1] == rows:
        d = ref.shape[2]
        for h in range(ref.shape[0]):
            ref[h] = val[:, d * h:d * (h + 1)].astype(ref.dtype)
    else:
        d = ref.shape[1]
        assert ref.shape[2] == rows and d != rows, (ref.shape, val.shape)
        for h in range(ref.shape[0]):
            ref[h] = val[:, d * h:d * (h + 1)].T.astype(ref.dtype)


def _load_heads(ref, transposed=False):
    if len(ref.shape) == 2:
        return ref[...].astype(F32)
    parts = [ref[h].astype(F32) for h in range(ref.shape[0])]
    return jnp.concatenate([p.T for p in parts] if transposed else parts, axis=-1)


def stage_fwd(name, fn, ops, outs, grid):
    n_in = len(ops)

    def body(*refs):
        res = fn(*[_load(r) for r in refs[:n_in]])
        for r, o in zip(refs[n_in:], res):
            _store_heads(r, o)

    return pl.pallas_call(
        body, grid=grid, in_specs=[pl.BlockSpec(o.block, o.imap) for o in ops],
        out_specs=[pl.BlockSpec(b, im) for (_, _, b, im) in outs],
        out_shape=[jax.ShapeDtypeStruct(s, d) for (s, d, _, _) in outs],
        compiler_params=pltpu.CompilerParams(dimension_semantics=("parallel",) * len(grid),
                                             vmem_limit_bytes=VMEM_LIMIT),
        name=name)(*[o.arr for o in ops])


def stage_bwd(name, fn, ops, cts, grid, value_acc=False):
    n_in = len(ops)
    ct_flat = [(c + (False,))[:4] for group in cts if group is not None for c in group]
    n_ct = len(ct_flat)
    diff = [i for i, o in enumerate(ops) if o.grad]
    any_acc = value_acc or any(ops[i].acc for i in diff)
    ngrid = len(grid)

    def body(*refs):
        ids = [pl.program_id(a) for a in range(ngrid)]
        vals = [_load(r) for r in refs[:n_in]]
        ct_refs = refs[n_in:n_in + n_ct]
        out_refs = refs[n_in + n_ct:]

        def f(*dv):
            full = list(vals)
            for i, v in zip(diff, dv):
                full[i] = v
            return tuple(fn(*full))

        res, vjp = jax.vjp(f, *[vals[i] for i in diff])
        ct, pos = [], 0
        for group, r in zip(cts, res):
            if group is None:
                ct.append(jnp.ones_like(r))
            else:
                tot = None
                for _ in group:
                    c = _load_heads(ct_refs[pos], ct_flat[pos][3])
                    tot = c if tot is None else tot + c
                    pos += 1
                ct.append(tot)
        grads = vjp(tuple(ct))
        for i, g, r in zip(diff, grads, out_refs):
            if ops[i].acc:
                @pl.when(ops[i].first(ids))
                def _(r=r):
                    r[...] = jnp.zeros_like(r)

                r[...] += g.astype(r.dtype)
            else:
                r[...] = g.astype(r.dtype)
        if value_acc:
            r = out_refs[len(diff)]

            @pl.when(ids[0] == 0)
            def _():
                r[...] = jnp.zeros_like(r)

            r[...] += res[0]

    in_specs = [pl.BlockSpec(o.block, o.imap) for o in ops] + [pl.BlockSpec(b, im) for (_, b, im, _) in ct_flat]
    out_specs = [pl.BlockSpec(ops[i].block, ops[i].gimap) for i in diff]
    out_shape = [jax.ShapeDtypeStruct(ops[i].gshape, ops[i].gdtype) for i in diff]
    if value_acc:
        out_specs.append(pl.BlockSpec((1, 1), lambda *ids: (0, 0)))
        out_shape.append(jax.ShapeDtypeStruct((1, 1), F32))
    sem = ("arbitrary",) * ngrid if any_acc else ("parallel",) * ngrid
    return pl.pallas_call(
        body, grid=grid, in_specs=in_specs, out_specs=out_specs, out_shape=out_shape,
        compiler_params=pltpu.CompilerParams(dimension_semantics=sem, vmem_limit_bytes=VMEM_LIMIT),
        name=name)(*[o.arr for o in ops], *[a for (a, _, _, _) in ct_flat])


def _sigmoid(x):
    return 1.0 / (1.0 + jnp.exp(-x))


def _silu(x):
    return x * _sigmoid(x)


def _gelu_tanh(x):
    return 0.5 * x * (1.0 + jnp.tanh(math.sqrt(2.0 / math.pi) * (x + 0.044715 * (x * x * x))))


def _ln(x, g, b):
    mu = jnp.mean(x, axis=-1, keepdims=True)
    xc = x - mu
    var = jnp.mean(xc * xc, axis=-1, keepdims=True)
    return xc * lax.rsqrt(var + LN_EPS) * g + b


def _rms(x, g):
    ms = jnp.mean(x * x, axis=-1, keepdims=True)
    return x * lax.rsqrt(ms + RMS_EPS) * g


def _swap_halves(x, half):
    width = x.shape[-1]
    lane = lax.broadcasted_iota(jnp.int32, x.shape, 1)
    return jnp.where(lane % (2 * half) < half, pltpu.roll(x, width - half, 1), pltpu.roll(x, half, 1))


def _make_swap(half):
    @jax.custom_vjp
    def swap(x):
        return _swap_halves(x, half)

    swap.defvjp(lambda x: (_swap_halves(x, half), None), lambda _, dy: (_swap_halves(dy, half),))
    return swap


_swap16, _swap8 = _make_swap(16), _make_swap(8)


def _rope(x, cos, sin_signed, swap):
    return x * cos + swap(x) * sin_signed


def _dot_f32(a, b):
    return jnp.dot(a, b, preferred_element_type=F32, precision=lax.Precision.HIGHEST)


def fn_ln(x, g, b):
    return (_ln(x, g, b),)


def _twice(fn):
    def f(*a):
        (y,) = fn(*a)
        return y, y
    return f


PROJ_W = 2048
P_A, P_Q, P_K, P_V, P_C, P_CQ, P_CKV, P_KR = 0, 512, 768, 896, 1024, 1536, 1792, 1920
CQ_PAD = 256
_CQ_END = P_CQ + MLA_Q_LORA


def _pad_w_in(wt):
    z = lambda n: jnp.zeros((n, wt.shape[1]), wt.dtype)
    return jnp.concatenate([wt[:_CQ_END], z(P_CKV - _CQ_END), wt[_CQ_END:], z(PROJ_W - P_KR - MLA_ROPE)], axis=0)


def _unpad_w_in(gt):
    return jnp.concatenate([gt[:_CQ_END], gt[P_CKV:P_KR + MLA_ROPE]], axis=0)


def fn_pre(proj, tab_q, tab_d, seg, place, qng, kng, sg, sb, sw, sbt, mqn, wuq, mkvn, wukv):
    tm = proj.shape[0]
    aglu = proj[:, P_A:P_A + GROUP_W] * _sigmoid(proj[:, P_A + GROUP_W:P_Q])
    b_q, b_k, b_v = proj[:, P_Q:P_K], proj[:, P_K:P_V], proj[:, P_V:P_C]
    cos_q, sin_q = tab_q[:, :GROUP_W], tab_q[:, GROUP_W:]
    q = b_q * lax.rsqrt(_dot_f32(b_q * b_q, seg) + RMS_EPS) * qng
    q = _rope(q, cos_q, sin_q, _swap16)
    k = b_k * lax.rsqrt(_dot_f32(b_k * b_k, seg[:128, :128]) + RMS_EPS) * kng
    k = _rope(k, cos_q[:, :128], sin_q[:, :128], _swap16)
    c = _gelu_tanh(proj[:, P_C:P_CQ])
    u, sv = c[:, :GROUP_W], _ln(c[:, GROUP_W:], sg, sb)
    group = lax.broadcasted_iota(jnp.int32, (CHUNK, GROUP_W), 1) // HEAD_DIM
    rows = []
    for n in range(tm // CHUNK):
        svn = sv[CHUNK * n:CHUNK * (n + 1)]
        acc = jnp.zeros((CHUNK, GROUP_W), F32)
        for g in range(SGU_GROUPS):
            acc = acc + jnp.where(group == g, mm_nn(sw[CHUNK * g:CHUNK * (g + 1)], svn) + sbt[:, g:g + 1], 0.0)
        rows.append(acc)
    o_c = u * jnp.concatenate(rows, axis=0)
    d_cq, d_ckv, d_kr = proj[:, P_CQ:P_CKV], proj[:, P_CKV:P_KR], proj[:, P_KR:PROJ_W]
    cqn = d_cq * lax.rsqrt(jnp.sum(d_cq * d_cq, axis=-1, keepdims=True) * (1.0 / MLA_Q_LORA) + RMS_EPS) * mqn
    cos_d = jnp.concatenate([tab_d[:, :MLA_DK_PAD]] * MLA_HEADS, axis=-1)
    sin_d = jnp.concatenate([tab_d[:, MLA_DK_PAD:]] * MLA_HEADS, axis=-1)
    qf = _rope(mm_nn(cqn, wuq), cos_d, sin_d, _swap8)
    kvd = mm_nn(_rms(d_ckv, mkvn), wukv)
    kf = _rope(kvd[:, :MLA_HEADS * MLA_DK_PAD] + _dot_f32(d_kr, place), cos_d, sin_d, _swap8)
    return aglu, q, k, b_v, o_c, qf, kf, kvd[:, MLA_HEADS * MLA_DK_PAD:]


def fn_aconv(win, w, b, g, beta):
    tm = win.shape[0] - 2 * CONV_A_HALO
    off = CONV_A_HALO - CONV_A_WIDTH // 2
    acc = None
    for r in range(8):
        rolled = win if r == 0 else _roll_rows(win, -r)
        for kk in range(CONV_A_WIDTH):
            if (off + kk) % 8 == r:
                base = off + kk - r
                term = rolled[base:base + tm] * w[kk:kk + 1, :]
                acc = term if acc is None else acc + term
    return (_silu(_ln(acc + b, g, beta)),)


def fn_resln(h, r, g, b):
    return (_ln(DEEPNORM_ALPHA * h + r, g, b),)


@functools.partial(jax.custom_vjp, nondiff_argnums=(1,))
def _roll_rows(x, shift):
    return pltpu.roll(x, shift % x.shape[0], 0)


_roll_rows.defvjp(lambda x, shift: (pltpu.roll(x, shift % x.shape[0], 0), None),
                  lambda shift, _, dy: (pltpu.roll(dy, (-shift) % dy.shape[0], 0),))


def _shift_down(x):
    row = lax.broadcasted_iota(jnp.int32, x.shape, 0)
    return jnp.where(row == 0, 0.0, _roll_rows(x, 1))


def _shift_up(x):
    row = lax.broadcasted_iota(jnp.int32, x.shape, 0)
    return jnp.where(row == x.shape[0] - 1, 0.0, _roll_rows(x, -1))


def fn_ffnconv(u1, u2, w1, w2, b1, b2):
    c1 = _shift_down(u1) * w1[0:1] + u1 * w1[1:2] + _shift_up(u1) * w1[2:3] + b1
    c2 = _shift_down(u2) * w2[0:1] + u2 * w2[1:2] + _shift_up(u2) * w2[2:3] + b2
    return (_silu(c1) * c2,)


def fn_final(h, r, t, g, b):
    y = _ln(DEEPNORM_ALPHA * h + r, g, b)
    err = (y - t) * (y - t)
    return (0.5 * jnp.sum(jnp.mean(err, axis=-1, keepdims=True), axis=0, keepdims=True),)


def _rope_tables(seq):
    n_rows = seq // GRID_W
    lane128 = jnp.arange(128)

    def tile_tables(j, rotated, half):
        inv = ROPE_THETA ** (-(j % half).astype(F32) / half)
        by_row, by_col = rotated & (j < 2 * half), rotated & (j >= 2 * half)
        sign = jnp.where(j % (2 * half) < half, -1.0, 1.0)
        ar = jnp.arange(n_rows, dtype=F32)[:, None] * inv[None, :]
        ac = jnp.arange(GRID_W, dtype=F32)[:, None] * inv[None, :]
        grid = lambda r, c: (jnp.where(by_row, r, 0.0)[:, None, :] + jnp.where(by_col, c, 0.0)[None, :, :])
        cos = grid(jnp.cos(ar), jnp.cos(ac)) + jnp.where(rotated, 0.0, 1.0)
        sin = grid(sign * jnp.sin(ar), sign * jnp.sin(ac))
        return cos.reshape(seq, 128), sin.reshape(seq, 128)

    cos_b, sin_b = tile_tables(lane128 % HEAD_DIM, lane128 >= 0, HEAD_DIM // 4)
    tab_q = jnp.concatenate([cos_b] * (GROUP_W // 128) + [sin_b] * (GROUP_W // 128), axis=-1)
    tab_d = jnp.concatenate(tile_tables(lane128 - MLA_NOPE, (lane128 >= MLA_NOPE) & (lane128 < MLA_NOPE + MLA_ROPE),
                                        MLA_ROPE // 4), axis=-1)
    lane = jnp.arange(GROUP_W)
    seg = jnp.where(lane[:, None] // HEAD_DIM == lane[None, :] // HEAD_DIM, 1.0 / HEAD_DIM, 0.0).astype(F32)
    src, dst = jnp.arange(128)[:, None], jnp.arange(MLA_HEADS * MLA_DK_PAD)[None, :]
    place = jnp.where((src < MLA_ROPE) & (dst % MLA_DK_PAD == MLA_NOPE + src), 1.0, 0.0).astype(F32)
    return tab_q, tab_d, seg, place


def _pre_ops(proj, tabs, kp, grad):
    tm = ROW_TILE
    ops = [_row_op(proj, tm, grad=grad, gdtype=MXU_DTYPE), _row_op(tabs[0], tm), _row_op(tabs[1], tm),
           _par_op(tabs[2]), _par_op(tabs[3])]
    ops += [_par_op(kp[n], grad=grad) for n in ('qng', 'kng', 'sg', 'sb', 'sw', 'sbt', 'mqn', 'wuq', 'mkvn', 'wukv')]
    return ops


PRE_OUTS = ((0, GROUP_W), (GQA_HEADS, HEAD_DIM), (GQA_KV_HEADS, HEAD_DIM), (GQA_KV_HEADS, HEAD_DIM), (0, GROUP_W),
            (MLA_HEADS, MLA_DK_PAD), (MLA_HEADS, MLA_DK_PAD), (MLA_HEADS, MLA_V))


def _pre_out_specs(seq, tm):
    specs = []
    for j, (heads, w) in enumerate(PRE_OUTS):
        if heads:
            specs.append(((heads, seq, w), (heads, tm, w), lambda i: (0, i, 0)))
        elif j == 0:
            specs.append(((seq + 2 * tm, w), (tm, w), lambda i: (i + 1, 0)))
        else:
            specs.append(((seq, w), (tm, w), lambda i: (i, 0)))
    return specs


def _conv_window(i, tm, seq):
    rows = pl.ds(pl.multiple_of((i + 1) * tm - CONV_A_HALO, 8), tm + 2 * CONV_A_HALO)
    tok = i * tm - CONV_A_HALO + lax.broadcasted_iota(jnp.int32, (tm + 2 * CONV_A_HALO, 1), 0)
    return rows, jnp.logical_and(tok >= 0, tok < seq)


PRE_KV = (2, 3, 6, 7)


def _transposed_spec(heads, w, seq, tm):
    return (heads, w, seq), (heads, w, tm), lambda i: (0, 0, i)


def pre_fwd(proj, tabs, kp, tag):
    seq = proj.shape[0]
    tm = ROW_TILE
    dts = (F32,) + (MXU_DTYPE,) * 7
    outs = [(shape, dt, block, imap) for (shape, block, imap), dt in zip(_pre_out_specs(seq, tm), dts)]
    return stage_fwd("pre_fwd" + tag, fn_pre, _pre_ops(proj, tabs, kp, False), outs, (seq // tm,))


def pre_bwd(proj, tabs, kp, cts, tag):
    seq = proj.shape[0]
    tm = ROW_TILE
    ct = []
    for j, (c, (_, block, imap)) in enumerate(zip(cts, _pre_out_specs(seq, tm))):
        if isinstance(c, tuple):
            ct.append([c])
        elif j in PRE_KV:
            ct.append([(c,) + _transposed_spec(*PRE_OUTS[j], seq, tm)[1:] + (True,)])
        else:
            ct.append([(c, block, imap)])
    return stage_bwd("pre_bwd" + tag, fn_pre, _pre_ops(proj, tabs, kp, True), ct, (seq // tm,))


def _aconv_ops(kp, grad):
    return [_par_op(kp[n], grad=grad) for n in ('caw', 'cab', 'lag', 'lab')]


def aconv_fwd(aglu_pad, kp, tag):
    tm = ROW_TILE
    seq = aglu_pad.shape[0] - 2 * tm
    n_par = 4

    def body(x_ref, *refs):
        rows, is_token = _conv_window(pl.program_id(0), tm, seq)
        win = jnp.where(is_token, x_ref[rows, :], 0.0)
        (o,) = fn_aconv(win, *[_load(r) for r in refs[:n_par]])
        refs[n_par][...] = o.astype(refs[n_par].dtype)

    pars = _aconv_ops(kp, False)
    return pl.pallas_call(
        body, grid=(seq // tm,),
        in_specs=[pl.BlockSpec(aglu_pad.shape, lambda i: (0, 0))] + [pl.BlockSpec(o.block, o.imap) for o in pars],
        out_specs=pl.BlockSpec((tm, GROUP_W), lambda i: (i, 0)),
        out_shape=jax.ShapeDtypeStruct((seq, GROUP_W), MXU_DTYPE),
        compiler_params=pltpu.CompilerParams(dimension_semantics=("parallel",), vmem_limit_bytes=VMEM_LIMIT),
        name="aconv_fwd" + tag)(aglu_pad, *[o.arr for o in pars])


def aconv_bwd(aglu_pad, kp, d_oa, tag):
    tm = ROW_TILE
    seq = aglu_pad.shape[0] - 2 * tm
    n_par = 4

    def body(x_ref, *refs):
        i = pl.program_id(0)
        rows, is_token = _conv_window(i, tm, seq)
        pars = [_load(r) for r in refs[:n_par]]
        ct = refs[n_par][...].astype(F32)
        outs = refs[n_par + 1:]
        _, vjp = jax.vjp(lambda *a: fn_aconv(*a), jnp.where(is_token, x_ref[rows, :], 0.0), *pars)
        grads = vjp((ct,))

        @pl.when(i == 0)
        def _():
            for r in outs:
                r[...] = jnp.zeros_like(r)

        outs[0][rows, :] += grads[0]
        for r, g in zip(outs[1:], grads[1:]):
            r[...] += g

    pars = _aconv_ops(kp, True)
    whole = pl.BlockSpec(aglu_pad.shape, lambda i: (0, 0))
    par_specs = [pl.BlockSpec(o.block, o.imap) for o in pars]
    return pl.pallas_call(
        body, grid=(seq // tm,),
        in_specs=[whole] + par_specs + [pl.BlockSpec((tm, GROUP_W), lambda i: (i, 0))],
        out_specs=[whole] + par_specs,
        out_shape=[jax.ShapeDtypeStruct(aglu_pad.shape, F32)] + [jax.ShapeDtypeStruct(o.arr.shape, F32) for o in pars],
        compiler_params=pltpu.CompilerParams(dimension_semantics=("arbitrary",), vmem_limit_bytes=VMEM_LIMIT),
        name="aconv_bwd" + tag)(aglu_pad, *[o.arr for o in pars], d_oa)


ATTN_TQ_FWD = 512
ATTN_TQ = 256
ATTN_TK = 512


def attn_fwd(q3, k3, v3t, scale, tag):
    heads, seq, dk = q3.shape
    group = heads // k3.shape[0]
    kv_per_pair = 2 // group
    dv = v3t.shape[1]
    tq, tk = min(ATTN_TQ_FWD, seq), min(ATTN_TK, seq)
    n_chunks = seq // tk
    log2e = math.log2(math.e)

    def one_head(q, k_ref, vt_ref):
        scores = lambda c: _rawdot(k_ref[pl.ds(c * tk, tk), :], q, 1, 1)
        m, l, acc = jnp.full((1, tq), -jnp.inf, F32), jnp.zeros((1, tq), F32), jnp.zeros((dv, tq), F32)
        s_next = scores(0)
        for c in range(n_chunks):
            s_cur, s_next = s_next, (scores(c + 1) if c + 1 < n_chunks else None)
            t = s_cur * (scale * log2e)
            m_new = jnp.maximum(m, jnp.max(t, axis=0, keepdims=True))
            alpha = jnp.exp2(m - m_new)
            p = jnp.exp2(t - m_new)
            l = alpha * l + jnp.sum(p, axis=0, keepdims=True)
            acc = alpha * acc + _rawdot(vt_ref[:, c * tk:(c + 1) * tk], p, 1, 0)
            m = m_new
        return (acc * (1.0 / l)).T, (m * (1.0 / log2e) + jnp.log(l)).T

    def body(q_ref, k_ref, v_ref, o_ref, lse_ref):
        outs = []
        for h in range(2):
            o, lse = one_head(q_ref[h], k_ref.at[h // group], v_ref.at[h // group])
            lse_ref[h] = lse
            outs.append(o)
        o_ref[...] = jnp.concatenate(outs, axis=-1)

    return pl.pallas_call(
        body, grid=(heads // 2, seq // tq),
        in_specs=[pl.BlockSpec((2, tq, dk), lambda j, i: (j, i, 0)),
                  pl.BlockSpec((kv_per_pair, seq, dk), lambda j, i: (j, 0, 0)),
                  pl.BlockSpec((kv_per_pair, dv, seq), lambda j, i: (j, 0, 0))],
        out_specs=[pl.BlockSpec((tq, 2 * dv), lambda j, i: (i, j)),
                   pl.BlockSpec((2, tq, 1), lambda j, i: (j, i, 0))],
        out_shape=[jax.ShapeDtypeStruct((seq, heads * dv), F32), jax.ShapeDtypeStruct((heads, seq, 1), F32)],
        compiler_params=pltpu.CompilerParams(dimension_semantics=("parallel", "parallel"),
                                             vmem_limit_bytes=VMEM_LIMIT),
        name="attn_fwd" + tag)(q3, k3, v3t)


def attn_bwd(q3, k3, v3, o, lse3, do_all, do_col, scale, tag):
    heads, seq, dk = q3.shape
    group = heads // k3.shape[0]
    kv_per_pair = 2 // group
    dv = v3.shape[2]
    tq, tk = min(ATTN_TQ, seq), min(ATTN_TK, seq)
    n_chunks = seq // tk
    log2e = math.log2(math.e)

    def one_head(q, do, o_h, lse, k_ref, v_ref, dk_ref, dv_ref):
        dob = do.astype(MXU_DTYPE)
        do_t, q_t = do.T.astype(MXU_DTYPE), q.astype(F32).T.astype(MXU_DTYPE)
        delta = jnp.sum(do * o_h, axis=-1, keepdims=True)
        lse2 = lse * log2e
        rows = lambda c: pl.ds(c * tk, tk)
        products = lambda c: (_rawdot(q, k_ref[rows(c), :], 1, 1), _rawdot(dob, v_ref[rows(c), :], 1, 1))
        dq = jnp.zeros((tq, dk), F32)
        nxt = products(0)
        for c in range(n_chunks):
            (s_cur, dp_cur), nxt = nxt, (products(c + 1) if c + 1 < n_chunks else None)
            p = jnp.exp2(s_cur * (scale * log2e) - lse2)
            ds = (p * ((dp_cur - delta) * scale)).astype(MXU_DTYPE)
            dv_ref[:, c * tk:(c + 1) * tk] += _rawdot(do_t, p, 1, 0)
            dk_ref[:, c * tk:(c + 1) * tk] += _rawdot(q_t, ds, 1, 0)
            dq = dq + _rawdot(ds, k_ref[rows(c), :], 1, 0)
        return dq

    def body(q_ref, k_ref, v_ref, o_ref, lse_ref, do_ref, dq_ref, dk_ref, dv_ref):
        @pl.when(pl.program_id(1) == 0)
        def _():
            dk_ref[...] = jnp.zeros_like(dk_ref)
            dv_ref[...] = jnp.zeros_like(dv_ref)

        do_pair, o_pair = do_ref[...], o_ref[...]
        for h in range(2):
            kv = h // group
            dq_ref[h] = one_head(q_ref[h], do_pair[:, dv * h:dv * (h + 1)], o_pair[:, dv * h:dv * (h + 1)],
                                 lse_ref[h], k_ref.at[kv], v_ref.at[kv], dk_ref.at[kv], dv_ref.at[kv])

    qspec = lambda d: pl.BlockSpec((2, tq, d), lambda j, i: (j, i, 0))
    kvspec = lambda d: pl.BlockSpec((kv_per_pair, seq, d), lambda j, i: (j, 0, 0))
    kvt_spec = lambda d: pl.BlockSpec((kv_per_pair, d, seq), lambda j, i: (j, 0, 0))
    kvt_shape = lambda a: jax.ShapeDtypeStruct((a.shape[0], a.shape[2], a.shape[1]), F32)
    return pl.pallas_call(
        body, grid=(heads // 2, seq // tq),
        in_specs=[qspec(dk), kvspec(dk), kvspec(dv), pl.BlockSpec((tq, 2 * dv), lambda j, i: (i, j)), qspec(1),
                  pl.BlockSpec((tq, 2 * dv), lambda j, i: (i, do_col + j))],
        out_specs=[qspec(dk), kvt_spec(dk), kvt_spec(dv)],
        out_shape=[jax.ShapeDtypeStruct(q3.shape, F32), kvt_shape(k3), kvt_shape(v3)],
        compiler_params=pltpu.CompilerParams(dimension_semantics=("parallel", "arbitrary"),
                                             vmem_limit_bytes=VMEM_LIMIT),
        name="attn_bwd" + tag)(q3, k3, v3, o, lse3, do_all)


def resln_fwd(h, r, g, b, tag):
    seq, d = h.shape
    tm = min(LN_TILE, seq)
    ops = [_row_op(h, tm), _row_op(r, tm), _par_op(g), _par_op(b)]
    outs = [((seq, d), dt, (tm, d), lambda i: (i, 0)) for dt in (F32, MXU_DTYPE)]
    return stage_fwd("resln_fwd" + tag, _twice(fn_resln), ops, outs, (seq // tm,))


def resln_bwd(h, r, g, b, dys, tag):
    seq, d = h.shape
    tm = min(LN_TILE, seq)
    ops = [_row_op(h, tm, grad=True), _row_op(r, tm, grad=True, gdtype=MXU_DTYPE), _par_op(g, grad=True),
           _par_op(b, grad=True)]
    ct = [[(dy, (tm, d), lambda i: (i, 0)) for dy in dys]]
    return stage_bwd("resln_bwd" + tag, fn_resln, ops, ct, (seq // tm,))


def _ffnconv_ops(up1, up2, w, b, grad):
    seq = up1.shape[0]
    nblk = D_FF // 128
    lo, hi = (lambda j: (0, j)), (lambda j: (0, j + nblk))
    half = lambda a: dict(gshape=(a.shape[0], D_FF), gimap=lo)
    return [Op(up1, (seq, 128), lo, grad=grad, gdtype=MXU_DTYPE), Op(up2, (seq, 128), lo, grad=grad, gdtype=MXU_DTYPE),
            Op(w, (3, 128), lo, grad=grad, **half(w)), Op(w, (3, 128), hi, grad=grad, **half(w)),
            Op(b, (1, 128), lo, grad=grad, **half(b)), Op(b, (1, 128), hi, grad=grad, **half(b))]


def ffnconv_fwd(up1, up2, w, b, tag):
    seq = up1.shape[0]
    outs = [((seq, D_FF), MXU_DTYPE, (seq, 128), lambda j: (0, j))]
    return stage_fwd("ffnconv_fwd" + tag, fn_ffnconv, _ffnconv_ops(up1, up2, w, b, False), outs, (D_FF // 128,))[0]


def ffnconv_bwd(up1, up2, w, b, dact, tag):
    seq = up1.shape[0]
    ct = [[(dact, (seq, 128), lambda j: (0, j))]]
    du1, du2, dw1, dw2, db1, db2 = stage_bwd("ffnconv_bwd" + tag, fn_ffnconv, _ffnconv_ops(up1, up2, w, b, True),
                                             ct, (D_FF // 128,))
    cat = lambda a, b_: jnp.concatenate([a, b_], axis=-1)
    return du1, du2, cat(dw1, dw2), cat(db1, db2)


def final_bwd(h, r, t, g, b, tag):
    seq, d = h.shape
    tm = min(LN_TILE, seq)
    ops = [_row_op(h, tm, grad=True), _row_op(r, tm, grad=True, gdtype=MXU_DTYPE), _row_op(t, tm),
           _par_op(g, grad=True), _par_op(b, grad=True)]
    return stage_bwd("final_bwd" + tag, fn_final, ops, [None], (seq // tm,), value_acc=True)


def _layer_params(wts, l):
    row = lambda a: a.reshape(1, -1)
    wuq = wts['mla_w_uq'][l].reshape(MLA_Q_LORA, MLA_HEADS, MLA_NOPE + MLA_ROPE)
    wuq = jnp.pad(wuq, ((0, CQ_PAD - MLA_Q_LORA), (0, 0), (0, MLA_DK_PAD - MLA_NOPE - MLA_ROPE)))
    wukv = wts['mla_w_ukv'][l].reshape(MLA_KV_LORA, MLA_HEADS, MLA_NOPE + MLA_V)
    wuk = jnp.pad(wukv[:, :, :MLA_NOPE], ((0, 0), (0, 0), (0, MLA_DK_PAD - MLA_NOPE)))
    return dict(
        qng=jnp.tile(row(wts['qk_norm_q'][l]), (1, GQA_HEADS)), kng=jnp.tile(row(wts['qk_norm_k'][l]), (1, GQA_KV_HEADS)),
        sg=row(wts['sgu_ln_g'][l]), sb=row(wts['sgu_ln_b'][l]),
        sw=wts['sgu_w'][l].reshape(SGU_GROUPS * CHUNK, CHUNK), sbt=wts['sgu_b'][l].T,
        mqn=jnp.pad(row(wts['mla_q_norm'][l]), ((0, 0), (0, CQ_PAD - MLA_Q_LORA))),
        wuq=wuq.reshape(CQ_PAD, MLA_HEADS * MLA_DK_PAD), mkvn=row(wts['mla_kv_norm'][l]),
        wukv=jnp.concatenate([wuk.reshape(MLA_KV_LORA, -1), wukv[:, :, MLA_NOPE:].reshape(MLA_KV_LORA, -1)], axis=1),
        caw=wts['conv_a_w'][l], cab=row(wts['conv_a_b'][l]), lag=row(wts['ln_a_g'][l]), lab=row(wts['ln_a_b'][l]),
        lmg=row(wts['ln_mix_g'][l]), lmb=row(wts['ln_mix_b'][l]),
        fcw=wts['ffn_conv_w'][l], fcb=row(wts['ffn_conv_b'][l]),
        lfg=row(wts['ln_ffn_g'][l]), lfb=row(wts['ln_ffn_b'][l]))


def local_step(x, target, ln_in, get_wts, mat, hook):
    seq = x.shape[0]
    tm = min(LN_TILE, seq)
    tabs = _rope_tables(seq)
    scale_b = HEAD_DIM ** -0.5
    scale_d = (MLA_NOPE + MLA_ROPE) ** -0.5
    ln_in_g, ln_in_b = ln_in[0].reshape(1, -1), ln_in[1].reshape(1, -1)

    h, h_m = stage_fwd("ln_in_fwd", _twice(fn_ln), [_row_op(x, tm), _par_op(ln_in_g), _par_op(ln_in_b)],
                       [((seq, D_MODEL), dt, (tm, D_MODEL), lambda i: (i, 0)) for dt in (F32, MXU_DTYPE)],
                       (seq // tm,))
    wts = get_wts(h_m[:8, :128].astype(F32) + tabs[0][:8, :128] + tabs[1][:8, :128])
    saved = []
    for l in range(DEPTH):
        tag = f"_l{l}"
        kp, unprep = jax.vjp(lambda w: _layer_params(w, l), wts)
        m = {'w_in': mat(l, 'w_in', h_m)}
        proj = matmul(h_m, m['w_in'], 'nt', F32, "mm_proj" + tag, tn=PROJ_W, tm_max=512)
        aglu, q3, k3, v3, o_c, qd3, kd3, vd3 = pre_fwd(proj, tabs, kp, tag)
        aglu_pad = aglu
        o_a = aconv_fwd(aglu_pad, kp, tag)
        swap = lambda a: a.transpose(0, 2, 1)
        o_b3, lse_b3 = attn_fwd(q3, k3, swap(v3), scale_b, "_b" + tag)
        o_d3, lse_d3 = attn_fwd(qd3, kd3, swap(vd3), scale_d, "_d" + tag)
        o_cat = jnp.concatenate([o_a, o_b3.astype(MXU_DTYPE), o_c, o_d3.astype(MXU_DTYPE)], axis=-1)
        m['w_out'] = mat(l, 'w_out', o_cat)
        mix = matmul(o_cat, m['w_out'], 'nn', F32, "mm_mix" + tag, tn=D_MODEL, tm_max=512)
        h1, h1_m = resln_fwd(h, mix, kp['lmg'], kp['lmb'], "_mix" + tag)
        m['ffn_w_up'] = mat(l, 'ffn_w_up', h1_m)
        wide = dict(tn=D_FF, tm_max=512)
        up1 = matmul(h1_m, m['ffn_w_up'], 'nt', F32, "mm_up1" + tag, b_rows=(0, D_FF), **wide)
        up2 = matmul(h1_m, m['ffn_w_up'], 'nt', F32, "mm_up2" + tag, b_rows=(D_FF, D_FF), **wide)
        act = ffnconv_fwd(up1, up2, kp['fcw'], kp['fcb'], tag)
        m['ffn_w_down'] = mat(l, 'ffn_w_down', act)
        f = matmul(act, m['ffn_w_down'], 'nn', F32, "mm_down" + tag, tn=D_MODEL, tm_max=512)
        saved.append(dict(kp=kp, unprep=unprep, m=m, h=h, h_m=h_m, h1_m=h1_m, proj=proj, o_b3=o_b3, lse_b3=lse_b3,
                          o_d3=o_d3, lse_d3=lse_d3, aglu_pad=aglu_pad, q3=q3, k3=k3, v3=v3, qd3=qd3,
                          kd3=kd3, vd3=vd3, o_cat=o_cat, mix=mix, h1=h1, up1=up1, up2=up2, act=act, f=f))
        if l + 1 < DEPTH:
            h, h_m = resln_fwd(h1, f, kp['lfg'], kp['lfb'], "_ffn" + tag)

    after = lambda a, tok: a if tok is None else a + tok
    small_acc = None
    dh_parts = None
    loss = None
    tok = None
    g_mix = None
    for l in reversed(range(DEPTH)):
        tag = f"_l{l}"
        s = saved[l]
        kp, m = s['kp'], s['m']
        dkp = {}
        lfg = after(kp['lfg'], tok)
        if l == DEPTH - 1:
            dh1_a, df, dkp['lfg'], dkp['lfb'], loss = final_bwd(s['h1'], s['f'], target, lfg, kp['lfb'], tag)
        else:
            dh1_a, df, dkp['lfg'], dkp['lfb'] = resln_bwd(s['h1'], s['f'], lfg, kp['lfb'], dh_parts, "_ffn" + tag)
        g_down = matmul(s['act'], df, 'tn', COMM_DTYPE, "mm_gdown" + tag, tn=D_MODEL)
        dact = matmul(df, m['ffn_w_down'], 'nt', F32, "mm_dact" + tag, tn=D_FF, tm_max=512)
        dup1, dup2, dkp['fcw'], dkp['fcb'] = ffnconv_bwd(s['up1'], s['up2'], kp['fcw'], kp['fcb'], dact, tag)
        dh1 = matmul([dup1, dup2], [m['ffn_w_up']] * 2, 'nn', F32, "mm_dh1" + tag,
                     b_rows=[(0, D_FF), (D_FF, D_FF)], add=[dh1_a], tn=D_MODEL, tm_max=512)
        g_up = matmul(dup1, s['h1_m'], 'tn', COMM_DTYPE, "mm_gup1" + tag, tn=D_MODEL,
                      into=(lax.empty((2 * D_FF, D_MODEL), COMM_DTYPE), 0))
        g_up = matmul(dup2, s['h1_m'], 'tn', COMM_DTYPE, "mm_gup2" + tag, tn=D_MODEL, into=(g_up, D_FF))
        tok = hook(f"ffn{l}", {('ffn_w_down', l): g_down, ('ffn_w_up', l): g_up})
        dh_a, dmix, dkp['lmg'], dkp['lmb'] = resln_bwd(s['h'], s['mix'], after(kp['lmg'], tok), kp['lmb'],
                                                       [dh1], "_mix" + tag)
        g_out = matmul(s['o_cat'], dmix, 'tn', COMM_DTYPE, "mm_gout" + tag, tn=D_MODEL)
        w_out = m['w_out']
        if l == 0:
            w_out = w_out + hook("out0", {('w_out', l): g_out}).astype(w_out.dtype)
        do_cat = matmul(dmix, w_out, 'nt', F32, "mm_docat" + tag, tn=D_MODEL, tm_max=512)
        lse_b3 = s['lse_b3']
        do_c = (do_cat, (ROW_TILE, GROUP_W), lambda i: (i, 2))
        pair_w = 2 * HEAD_DIM
        dq3, dk3, dv3 = attn_bwd(s['q3'], s['k3'], s['v3'], s['o_b3'], lse_b3, do_cat, GROUP_W // pair_w,
                                 scale_b, "_b" + tag)
        dqd3, dkd3, dvd3 = attn_bwd(s['qd3'], s['kd3'], s['vd3'], s['o_d3'], s['lse_d3'], do_cat,
                                    3 * GROUP_W // pair_w, scale_d, "_d" + tag)
        daglu_pad, dkp['caw'], dkp['cab'], dkp['lag'], dkp['lab'] = aconv_bwd(s['aglu_pad'], kp, do_cat, tag)
        cts = [daglu_pad, dq3, dk3, dv3, do_c, dqd3, dkd3, dvd3]
        pre_g = pre_bwd(s['proj'], tabs, kp, cts, tag)
        dproj = pre_g[0]
        for n, g in zip(('qng', 'kng', 'sg', 'sb', 'sw', 'sbt', 'mqn', 'wuq', 'mkvn', 'wukv'), pre_g[1:]):
            dkp[n] = g
        dh_parts = [matmul(dproj, m['w_in'], 'nn', F32, "mm_dh" + tag, add=[dh_a], tn=D_MODEL,
                           tm_max=512)]
        g_in = matmul(dproj, s['h_m'], 'tn', COMM_DTYPE, "mm_gin" + tag, tn=D_MODEL)
        (dw,) = s['unprep'](dkp)
        small_acc = dw if small_acc is None else jax.tree.map(jnp.add, small_acc, dw)
        g_mix = {('w_out', l): g_out, ('w_in', l): _unpad_w_in(g_in)}
        if l > 0:
            tok = hook(f"mix{l}", g_mix)
        else:
            g_mix.pop(('w_out', l))

    g_mix.update({(n, None): small_acc[n] for n in SHARDED if n not in MATMUL_WEIGHTS})
    tok = hook("last", g_mix)
    dx, dg, db = stage_bwd("ln_in_bwd", fn_ln,
                           [_row_op(x, tm, grad=True), _par_op(after(ln_in_g, tok), grad=True),
                            _par_op(ln_in_b, grad=True)],
                           [[(p, (tm, D_MODEL), lambda i: (i, 0)) for p in dh_parts]], (seq // tm,))
    out = {n: small_acc[n] for n in REPLICATED}
    out['ln_in_g'], out['ln_in_b'] = dg.reshape(-1), db.reshape(-1)
    return loss, dx, out


def _peer(x, y, c, r):
    return ((1 - x) if r & 4 else x, (1 - y) if r & 2 else y, (1 - c) if r & 1 else c)


def _exchange_copy(src_ref, land_ref, send_sems, recv_sems, k, gather, x, y, c, r):
    px, py, pc = _peer(x, y, c, r)
    me, peer = 4 * x + 2 * y + c, 4 * px + 2 * py + pc
    src = src_ref if gather else src_ref.at[peer]
    mk = lambda dst: pltpu.make_async_remote_copy(
        src_ref=src, dst_ref=dst, send_sem=send_sems.at[k * (N_DEV - 1) + r - 1],
        recv_sem=recv_sems.at[k * (N_DEV - 1) + r - 1],
        device_id=(px, py, pc), device_id_type=pl.DeviceIdType.MESH)
    return mk(land_ref.at[me]), mk(land_ref.at[peer])


_HBM_SPEC = pl.BlockSpec(memory_space=pltpu.HBM)
_SEM_SPEC = pl.BlockSpec(memory_space=pltpu.SEMAPHORE)


def exchange_start(srcs, gather, groups, name):
    n_t = len(srcs)
    lands =[lax.empty(((N_DEV,) + s.shape) if gt else s.shape, s.dtype) for s, gt in zip(srcs, gather)]

    def body(*refs):
        src_refs, land_refs = refs[:n_t], refs[n_t:2 * n_t]
        sem_refs = refs[2 * n_t:2 * n_t + 2 * len(groups)]
        token = refs[-1]
        x, y, c = lax.axis_index("x"), lax.axis_index("y"), lax.axis_index("c")
        for gi, g in enumerate(groups):
            for k, t in enumerate(g):
                for r in range(1, N_DEV):
                    _exchange_copy(src_refs[t], land_refs[t], sem_refs[2 * gi], sem_refs[2 * gi + 1], k, gather[t],
                                   x, y, c, r)[0].start()
        token[...] = jnp.zeros_like(token)

    sem_shapes = []
    for g in groups:
        sem_shapes += [pltpu.SemaphoreType.DMA((len(g) * (N_DEV - 1),))] * 2
    hbm_shapes = [pltpu.HBM(a.shape, a.dtype) for a in list(srcs) + lands]
    n_sem = len(sem_shapes)
    res = pl.pallas_call(
        body, name=name,
        out_shape=tuple(sem_shapes + hbm_shapes + [jax.ShapeDtypeStruct((8, 128), F32)]),
        in_specs=[_HBM_SPEC] * (2 * n_t),
        out_specs=tuple([_SEM_SPEC] * n_sem + [_HBM_SPEC] * (2 * n_t) + [pl.BlockSpec(memory_space=pltpu.VMEM)]),
        input_output_aliases={i: n_sem + i for i in range(2 * n_t)},
        compiler_params=pltpu.CompilerParams(has_side_effects=pltpu.SideEffectType.DATAFLOW_SIDE_EFFECTING),
    )(*[pltpu.with_memory_space_constraint(a, pltpu.HBM) for a in list(srcs) + lands])
    sems = [(res[2 * gi], res[2 * gi + 1]) for gi in range(len(groups))]
    return sems, list(res[n_sem:n_sem + n_t]), list(res[n_sem + n_t:n_sem + 2 * n_t]), res[-1]


def exchange_wait(sems, srcs, lands, gather, after, name):
    n_t = len(srcs)

    def body(*refs):
        src_refs, land_refs = refs[:n_t], refs[n_t:2 * n_t]
        send_sems, recv_sems = refs[2 * n_t], refs[2 * n_t + 1]
        x, y, c = lax.axis_index("x"), lax.axis_index("y"), lax.axis_index("c")
        for k in range(n_t):
            for r in range(1, N_DEV):
                send, recv = _exchange_copy(src_refs[k], land_refs[k], send_sems, recv_sems, k, gather[k], x, y, c, r)
                send.wait_send()
                recv.wait_recv()

    res = pl.pallas_call(
        body, name=name,
        out_shape=tuple(pltpu.HBM(a.shape, a.dtype) for a in list(srcs) + list(lands)),
        in_specs=[_HBM_SPEC] * (2 * n_t) + [_SEM_SPEC, _SEM_SPEC, pl.BlockSpec(memory_space=pl.ANY)],
        out_specs=tuple([_HBM_SPEC] * (2 * n_t)),
        input_output_aliases={i: i for i in range(2 * n_t)},
        compiler_params=pltpu.CompilerParams(has_side_effects=pltpu.SideEffectType.DATAFLOW_SIDE_EFFECTING),
    )(*srcs, *lands, sems[0], sems[1], after)
    return list(res[:n_t]), list(res[n_t:])


def adamw(parts, own, me, w, m, v, name):
    n_l, n_r, n_c = w.shape
    tr = n_r
    if n_r % 8 == 0:
        for cand in (512, 256, 128, 64, 32, 16, 8):
            if n_r % cand == 0 and cand * n_c * 4 <= 512 * 1024:
                tr = cand
                break
    c1 = 1.0 - ADAM_B1 ** ADAM_STEP
    c2 = 1.0 - ADAM_B2 ** ADAM_STEP
    per_layer = isinstance(parts, (list, tuple))
    n_p = n_l if per_layer else 1
    n_rb = n_r // tr

    def update(g, w_ref, m_ref, v_ref, g_ref, d_ref, nm_ref, nv_ref):
        w_, m_, v_ = w_ref[0], m_ref[0], v_ref[0]
        nm = ADAM_B1 * m_ + (1.0 - ADAM_B1) * g
        nv = ADAM_B2 * v_ + (1.0 - ADAM_B2) * (g * g)
        g_ref[0] = g
        nm_ref[0] = nm
        nv_ref[0] = nv
        d_ref[0] = -ADAM_LR * ((nm / c1) / (jnp.sqrt(nv / c2) + ADAM_EPS) + ADAM_WD * w_)

    def total(me_ref, land, own_blk):
        g = None
        for s in range(N_DEV):
            term = jnp.where(me_ref[0] == s, own_blk, land(s)).astype(F32)
            g = term if g is None else g + term
        return g

    def body(me_ref, *refs):
        p_refs, o_refs, rest = refs[:n_p], refs[n_p:2 * n_p], refs[2 * n_p:]
        if not per_layer:
            update(total(me_ref, lambda s: p_refs[0][s, 0], o_refs[0][0]), *rest)
        else:
            for lay in range(n_l):
                @pl.when(pl.program_id(0) == lay)
                def _(lay=lay):
                    update(total(me_ref, lambda s: p_refs[lay][s], o_refs[lay][...]), *rest)

    blk = pl.BlockSpec((1, tr, n_c), lambda l, r, me_ref: (l, r, 0))
    if per_layer:
        def rows(lay):
            park = 0 if lay > 0 else n_rb - 1
            return lambda l, r: jnp.where(l == lay, r, park)
        p_specs = [pl.BlockSpec((N_DEV, tr, n_c), lambda l, r, me_ref, f=rows(lay): (0, f(l, r), 0))
                   for lay in range(n_l)]
        o_specs = [pl.BlockSpec((None, tr, n_c), lambda l, r, me_ref, f=rows(lay): (me_ref[0], f(l, r), 0))
                   for lay in range(n_l)]
        p_args, o_args = list(parts), list(own)
    else:
        p_specs = [pl.BlockSpec((N_DEV, 1, tr, n_c), lambda l, r, me_ref: (0, l, r, 0))]
        o_specs = [pl.BlockSpec((None, 1, tr, n_c), lambda l, r, me_ref: (me_ref[0], l, r, 0))]
        p_args, o_args = [parts], [own]
    return pl.pallas_call(
        body,
        grid_spec=pltpu.PrefetchScalarGridSpec(
            num_scalar_prefetch=1, grid=(n_l, n_rb), in_specs=p_specs + o_specs + [blk, blk, blk],
            out_specs=[blk] * 4),
        out_shape=[jax.ShapeDtypeStruct(w.shape, F32)] * 4,
        compiler_params=pltpu.CompilerParams(dimension_semantics=("arbitrary", "arbitrary"),
                                             vmem_limit_bytes=VMEM_LIMIT),
        name=name)(me.reshape(1).astype(jnp.int32), *p_args, *o_args, w, m, v)


def adamw_replicated(lands, own, ws, ms, vs, loss_land, loss_own):
    n_t = len(lands)
    c1 = 1.0 - ADAM_B1 ** ADAM_STEP
    c2 = 1.0 - ADAM_B2 ** ADAM_STEP

    def body(*refs):
        ins, outs = refs[:5 * n_t + 2], refs[5 * n_t + 2:]
        me = 4 * lax.axis_index("x") + 2 * lax.axis_index("y") + lax.axis_index("c")

        def total(land_ref, own_ref):
            g = None
            for s in range(N_DEV):
                term = jnp.where(me == s, own_ref[...], land_ref[s])
                g = term if g is None else g + term
            return g

        for t in range(n_t):
            land_ref, own_ref, w_ref, m_ref, v_ref = ins[5 * t:5 * t + 5]
            g = total(land_ref, own_ref)
            nm = ADAM_B1 * m_ref[...] + (1.0 - ADAM_B1) * g
            nv = ADAM_B2 * v_ref[...] + (1.0 - ADAM_B2) * (g * g)
            g_ref, d_ref, nm_ref, nv_ref = outs[4 * t:4 * t + 4]
            g_ref[...] = g
            nm_ref[...] = nm
            nv_ref[...] = nv
            d_ref[...] = -ADAM_LR * ((nm / c1) / (jnp.sqrt(nv / c2) + ADAM_EPS) + ADAM_WD * w_ref[...])
        outs[4 * n_t][...] = total(ins[5 * n_t], ins[5 * n_t + 1])

    args = []
    for t in range(n_t):
        args += [lands[t], own[t], ws[t], ms[t], vs[t]]
    out_shape = []
    for t in range(n_t):
        out_shape += [jax.ShapeDtypeStruct(ws[t].shape, F32)] * 4
    out_shape.append(jax.ShapeDtypeStruct(loss_own.shape, F32))
    res = pl.pallas_call(body, out_shape=out_shape,
                         compiler_params=pltpu.CompilerParams(vmem_limit_bytes=VMEM_LIMIT),
                         name="adamw_replicated")(*args, loss_land, loss_own)
    return [tuple(res[4 * t:4 * t + 4]) for t in range(n_t)], res[-1]


def _shard_slots(g, axis):
    if axis == 1:
        return g.reshape(g.shape[0], N_DEV, g.shape[1] // N_DEV, g.shape[2]).transpose(1, 0, 2, 3)
    return g.reshape(g.shape[0], g.shape[1], N_DEV, g.shape[2] // N_DEV).transpose(2, 0, 1, 3)


def _unshard(slots, axis):
    if axis == 1:
        return slots.transpose(1, 0, 2, 3).reshape(slots.shape[1], -1, slots.shape[3])
    return slots.transpose(1, 2, 0, 3).reshape(slots.shape[1], slots.shape[2], -1)


def kernel(x, ln_in_g, ln_in_b, w_in, conv_a_w, conv_a_b, ln_a_g, ln_a_b, qk_norm_q, qk_norm_k, sgu_ln_g, sgu_ln_b, sgu_w, sgu_b, mla_q_norm, mla_w_uq, mla_kv_norm, mla_w_ukv, w_out, ln_mix_g, ln_mix_b, ffn_w_up, ffn_conv_w, ffn_conv_b, ffn_w_down, ln_ffn_g, ln_ffn_b, loss_target, m_ln_in_g, m_ln_in_b, m_w_in, m_conv_a_w, m_conv_a_b, m_ln_a_g, m_ln_a_b, m_qk_norm_q, m_qk_norm_k, m_sgu_ln_g, m_sgu_ln_b, m_sgu_w, m_sgu_b, m_mla_q_norm, m_mla_w_uq, m_mla_kv_norm, m_mla_w_ukv, m_w_out, m_ln_mix_g, m_ln_mix_b, m_ffn_w_up, m_ffn_conv_w, m_ffn_conv_b, m_ffn_w_down, m_ln_ffn_g, m_ln_ffn_b, v_ln_in_g, v_ln_in_b, v_w_in, v_conv_a_w, v_conv_a_b, v_ln_a_g, v_ln_a_b, v_qk_norm_q, v_qk_norm_k, v_sgu_ln_g, v_sgu_ln_b, v_sgu_w, v_sgu_b, v_mla_q_norm, v_mla_w_uq, v_mla_kv_norm, v_mla_w_ukv, v_w_out, v_ln_mix_g, v_ln_mix_b, v_ffn_w_up, v_ffn_conv_w, v_ffn_conv_b, v_ffn_w_down, v_ln_ffn_g, v_ln_ffn_b):
    local = dict(ln_in_g=ln_in_g, ln_in_b=ln_in_b, w_in=w_in, conv_a_w=conv_a_w, conv_a_b=conv_a_b, ln_a_g=ln_a_g, ln_a_b=ln_a_b, qk_norm_q=qk_norm_q, qk_norm_k=qk_norm_k, sgu_ln_g=sgu_ln_g, sgu_ln_b=sgu_ln_b, sgu_w=sgu_w, sgu_b=sgu_b, mla_q_norm=mla_q_norm, mla_w_uq=mla_w_uq, mla_kv_norm=mla_kv_norm, mla_w_ukv=mla_w_ukv, w_out=w_out, ln_mix_g=ln_mix_g, ln_mix_b=ln_mix_b, ffn_w_up=ffn_w_up, ffn_conv_w=ffn_conv_w, ffn_conv_b=ffn_conv_b, ffn_w_down=ffn_w_down, ln_ffn_g=ln_ffn_g, ln_ffn_b=ln_ffn_b)
    mom = dict(ln_in_g=m_ln_in_g, ln_in_b=m_ln_in_b, w_in=m_w_in, conv_a_w=m_conv_a_w, conv_a_b=m_conv_a_b, ln_a_g=m_ln_a_g, ln_a_b=m_ln_a_b, qk_norm_q=m_qk_norm_q, qk_norm_k=m_qk_norm_k, sgu_ln_g=m_sgu_ln_g, sgu_ln_b=m_sgu_ln_b, sgu_w=m_sgu_w, sgu_b=m_sgu_b, mla_q_norm=m_mla_q_norm, mla_w_uq=m_mla_w_uq, mla_kv_norm=m_mla_kv_norm, mla_w_ukv=m_mla_w_ukv, w_out=m_w_out, ln_mix_g=m_ln_mix_g, ln_mix_b=m_ln_mix_b, ffn_w_up=m_ffn_w_up, ffn_conv_w=m_ffn_conv_w, ffn_conv_b=m_ffn_conv_b, ffn_w_down=m_ffn_w_down, ln_ffn_g=m_ln_ffn_g, ln_ffn_b=m_ln_ffn_b)
    var = dict(ln_in_g=v_ln_in_g, ln_in_b=v_ln_in_b, w_in=v_w_in, conv_a_w=v_conv_a_w, conv_a_b=v_conv_a_b, ln_a_g=v_ln_a_g, ln_a_b=v_ln_a_b, qk_norm_q=v_qk_norm_q, qk_norm_k=v_qk_norm_k, sgu_ln_g=v_sgu_ln_g, sgu_ln_b=v_sgu_ln_b, sgu_w=v_sgu_w, sgu_b=v_sgu_b, mla_q_norm=v_mla_q_norm, mla_w_uq=v_mla_w_uq, mla_kv_norm=v_mla_kv_norm, mla_w_ukv=v_mla_w_ukv, w_out=v_w_out, ln_mix_g=v_ln_mix_g, ln_mix_b=v_ln_mix_b, ffn_w_up=v_ffn_w_up, ffn_conv_w=v_ffn_conv_w, ffn_conv_b=v_ffn_conv_b, ffn_w_down=v_ffn_w_down, ln_ffn_g=v_ln_ffn_g, ln_ffn_b=v_ln_ffn_b)

    me = 4 * lax.axis_index("x") + 2 * lax.axis_index("y") + lax.axis_index("c")

    def own_slot(slots, block):
        return lax.dynamic_update_slice(slots, block[None], (me,) + (0,) * block.ndim)

    small_sharded = [n for n in SHARDED if n not in MATMUL_WEIGHTS]
    big_order = [(n, l) for l in range(DEPTH) for n in MATMUL_WEIGHTS]
    send_view = lambda n, l: (local[n].transpose(0, 2, 1)[l] if n in TRANSPOSED else local[n][l]).astype(COMM_DTYPE)
    srcs = [send_view(*big_order[0])] + [local[n] for n in small_sharded]
    srcs += [send_view(n, l) for (n, l) in big_order[1:]]
    n_first = 1 + len(small_sharded)
    groups = [list(range(n_first))] + [[n_first + j] for j in range(len(big_order) - 1)]
    g_sems, g_srcs, g_lands, tok0 = exchange_start(srcs, [True] * len(srcs), groups, "gather_start")
    tok0 = tok0[0, 0]
    pending = {key: gi for gi, key in enumerate(big_order)}

    def finish(gi, after):
        idx = groups[gi]
        _, lands = exchange_wait(g_sems[gi], [g_srcs[t] for t in idx], [g_lands[t] for t in idx], [True] * len(idx),
                                 after, f"gather_wait{gi}")
        return [own_slot(ld, srcs[t]) for ld, t in zip(lands, idx)]

    first = []

    opt_view = {n: tuple(a.transpose(0, 2, 1) for a in (local[n], mom[n], var[n])) for n in TRANSPOSED}

    def get_wts(after):
        for views in opt_view.values():
            after = after + sum(a[0, :8, :128] for a in views)
        first.extend(finish(0, after))
        wts = {n: local[n] for n in REPLICATED}
        for n, slots in zip(small_sharded, first[1:]):
            wts[n] = _unshard(slots, SHARDED[n])
        return wts

    def unshard_layer(slots, n):
        if SHARDED[n] == 1 or n in TRANSPOSED:
            return slots.reshape(-1, slots.shape[2])
        return slots.transpose(1, 0, 2).reshape(slots.shape[1], -1)

    def mat(l, n, after):
        gi = pending[(n, l)]
        slots = first[0] if gi == 0 else finish(gi, after)[0]
        w = unshard_layer(slots, n).astype(MXU_DTYPE)
        return _pad_w_in(w) if n == 'w_in' else w

    started = []

    def hook(key, grads):
        tensors = []
        for (n, l), g in grads.items():
            if l is None:
                tensors.append(((n, l), _shard_slots(g, SHARDED[n])))
            elif n in TRANSPOSED:
                tensors.append(((n, l), g.reshape(N_DEV, g.shape[0] // N_DEV, g.shape[1]).astype(COMM_DTYPE)))
            else:
                tensors.append(((n, l), _shard_slots(g[None], SHARDED[n])[:, 0].astype(COMM_DTYPE)))
        sems, s_srcs, s_lands, tok = exchange_start([a for _, a in tensors], [False] * len(tensors),
                                                    [list(range(len(tensors)))], "scatter_start_" + key)
        started.append((key, [k for k, _ in tensors], sems[0], s_srcs, s_lands))
        return tok[0, 0]

    loss, dx, grads = local_step(x[0], loss_target[0], (local['ln_in_g'] + tok0, local['ln_in_b']), get_wts, mat, hook)

    as2d = lambda a: a.reshape(-1, a.shape[-1]) if a.ndim > 1 else a.reshape(1, -1)
    small_g = [as2d(grads[n]) for n in REPLICATED] + [jnp.broadcast_to(loss, (8, 128))]
    p_sems, p_srcs, p_lands, p_tok = exchange_start(small_g, [True] * len(small_g), [list(range(len(small_g)))],
                                                    "gather_small_start")

    parts, res = {}, {}

    def finish_scatter(entries, after):
        for key, keys, sems, s_srcs, s_lands in entries:
            s_out, lands = exchange_wait(sems, s_srcs, s_lands, [False] * len(keys), after, "scatter_wait_" + key)
            for k, so, ld in zip(keys, s_out, lands):
                parts[k] = (ld, so)

    def update(names_):
        for n in names_:
            if n in MATMUL_WEIGHTS:
                p, o = zip(*[parts[(n, l)] for l in range(DEPTH)])
            else:
                p, o = parts[(n, None)]
            if n in TRANSPOSED:
                res[n] = tuple(a.transpose(0, 2, 1) for a in adamw(p, o, me, *opt_view[n], "adamw_" + n))
            else:
                res[n] = adamw(p, o, me, local[n], mom[n], var[n], "adamw_" + n)

    early = ('ffn_w_up', 'ffn_w_down', 'w_out')
    finish_scatter([e for e in started if e[0] != "last"], p_tok)
    update(early)
    finish_scatter([e for e in started if e[0] == "last"], res[early[-1]][1])
    update([n for n in SHARDED if n not in early])
    updated = jnp.zeros((8, 128), F32) + sum(res[n][1][0, 0, 0] for n in SHARDED)
    p_own, p_lands = exchange_wait(p_sems[0], p_srcs, p_lands, [True] * len(small_g), updated, "gather_small_wait")
    small, loss_sum = adamw_replicated(p_lands[:-1], p_own[:-1], [as2d(local[n]) for n in REPLICATED],
                                       [as2d(mom[n]) for n in REPLICATED], [as2d(var[n]) for n in REPLICATED],
                                       p_lands[-1], p_own[-1])
    for n, quad in zip(REPLICATED, small):
        res[n] = tuple(a.reshape(local[n].shape) for a in quad)
    loss_total = loss_sum[0, 0]

    return (loss_total, dx[None], *[res[n][0] for n in WEIGHTS], *[res[n][1] for n in WEIGHTS],
            *[res[n][2] for n in WEIGHTS], *[res[n][3] for n in WEIGHTS])
```

```python
import functools
import math

import jax
import jax.numpy as jnp
from jax import lax
from jax.experimental import pallas as pl
from jax.experimental.pallas import tpu as pltpu

F32 = jnp.float32
MXU_DTYPE = jnp.bfloat16
COMM_DTYPE = jnp.bfloat16

N_DEV = 8
D_MODEL = 1024
DEPTH = 2
GRID_W = 64
GROUP_W = 256
HEAD_DIM = 64
CONV_A_WIDTH = 31
CONV_A_HALO = 16
GQA_HEADS = 4
GQA_KV_HEADS = 2
CHUNK = 128
SGU_GROUPS = 4
MLA_HEADS = 4
MLA_Q_LORA = 192
MLA_KV_LORA = 128
MLA_NOPE = 64
MLA_ROPE = 32
MLA_V = 64
MLA_DK_PAD = 128
ROPE_THETA = 10000.0
D_FF = 2816
DEEPNORM_ALPHA = (2 * DEPTH) ** 0.25
LN_EPS = 1e-5
RMS_EPS = 1e-6
D_IN_PROJ = 1888

ADAM_LR = 0.001
ADAM_B1 = 0.9
ADAM_B2 = 0.999
ADAM_EPS = 1e-08
ADAM_WD = 0.01
ADAM_STEP = 10

WEIGHTS = ['ln_in_g', 'ln_in_b', 'w_in', 'conv_a_w', 'conv_a_b', 'ln_a_g', 'ln_a_b', 'qk_norm_q', 'qk_norm_k',
           'sgu_ln_g', 'sgu_ln_b', 'sgu_w', 'sgu_b', 'mla_q_norm', 'mla_w_uq', 'mla_kv_norm', 'mla_w_ukv', 'w_out',
           'ln_mix_g', 'ln_mix_b', 'ffn_w_up', 'ffn_conv_w', 'ffn_conv_b', 'ffn_w_down', 'ln_ffn_g', 'ln_ffn_b']
SHARDED = {'w_in': 2, 'conv_a_w': 2, 'mla_w_uq': 2, 'mla_w_ukv': 2, 'w_out': 1, 'ffn_w_up': 2, 'ffn_conv_w': 2,
           'ffn_w_down': 1}
MATMUL_WEIGHTS = ('w_in', 'w_out', 'ffn_w_up', 'ffn_w_down')
TRANSPOSED = ('w_in', 'ffn_w_up')
REPLICATED = [n for n in WEIGHTS if n not in SHARDED]

ROW_TILE = 256
LN_TILE = 512
VMEM_LIMIT = 56 * 1024 * 1024


def _rawdot(a, b, ca, cb):
    return lax.dot_general(a.astype(MXU_DTYPE), b.astype(MXU_DTYPE), (((ca,), (cb,)), ((), ())),
                           preferred_element_type=F32)


@jax.custom_vjp
def mm_nn(a, b):
    return _rawdot(a, b, 1, 0)


def _mm_nn_fwd(a, b):
    return _rawdot(a, b, 1, 0), (a, b)


def _mm_nn_bwd(res, dy):
    a, b = res
    return _rawdot(dy, b, 1, 1), _rawdot(a, dy, 0, 0)


mm_nn.defvjp(_mm_nn_fwd, _mm_nn_bwd)


@jax.custom_vjp
def mm_nt(a, b):
    return _rawdot(a, b, 1, 1)


def _mm_nt_fwd(a, b):
    return _rawdot(a, b, 1, 1), (a, b)


def _mm_nt_bwd(res, dy):
    a, b = res
    return _rawdot(dy, b, 1, 0), _rawdot(dy, a, 0, 0)


mm_nt.defvjp(_mm_nt_fwd, _mm_nt_bwd)


def _pick_tile(d, cands):
    for c in cands:
        if d % c == 0:
            return c
    return d


def matmul(a, b, mode, out_dtype, name, b_rows=None, into=None, add=(), tm_max=1408, tn=None):
    a_list = list(a) if isinstance(a, (list, tuple)) else [a]
    b_list = list(b) if isinstance(b, (list, tuple)) else [b]
    n_p = len(a_list)
    rows_list = [b_rows] if not isinstance(a, (list, tuple)) else (list(b_rows) if b_rows is not None else [None] * n_p)
    b_start, b_size = zip(*[(0, bb.shape[0]) if r is None else r for bb, r in zip(b_list, rows_list)])
    a0, b0 = a_list[0], b_list[0]
    if mode == 'nn':
        (m, k), (k2, n) = a0.shape, (b_size[0], b0.shape[1])
    elif mode == 'nt':
        (m, k), (n, k2) = a0.shape, (b_size[0], b0.shape[1])
    else:
        (k, m), (k2, n) = a0.shape, (b_size[0], b0.shape[1])
    assert k == k2 and all(x.shape == a0.shape for x in a_list) and len(set(b_size)) == 1, (a0.shape, b0.shape, mode)
    tm = _pick_tile(m, tuple(c for c in (1024, 1408, 512, 256, 128) if c <= tm_max))
    tn = _pick_tile(n, (512, 1408, 256, 128)) if tn is None else tn
    assert n % tn == 0, (n, tn)
    tk = _pick_tile(k, (2816, 2048, 1024, 512, 256, 128))
    nk = k // tk
    ca = 0 if mode == 'tn' else 1
    cb = 1 if mode == 'nt' else 0
    b_blk = tn if mode == 'nt' else tk
    assert all(s % b_blk == 0 for s in b_start), (b_rows, b_blk)
    a_spec = pl.BlockSpec((tk, tm), lambda i, j, kk: (kk, i)) if mode == 'tn' else pl.BlockSpec((tm, tk), lambda i, j, kk: (i, kk))

    def b_spec(off):
        if mode == 'nt':
            return pl.BlockSpec((tn, tk), lambda i, j, kk: (j + off, kk))
        return pl.BlockSpec((tk, tn), lambda i, j, kk: (kk + off, j))

    in_specs, args, aliases = [], [], {}
    for x, y, s in zip(a_list, b_list, b_start):
        in_specs += [a_spec, b_spec(s // b_blk)]
        args += [x, y]
    in_specs += [pl.BlockSpec((tm, tn), lambda i, j, kk: (i, j))] * len(add)
    args += list(add)
    out_off, out_shape = 0, jax.ShapeDtypeStruct((m, n), out_dtype)
    if into is not None:
        buf, row = into
        assert row % tm == 0 and buf.shape[1] == n and buf.dtype == out_dtype, (buf.shape, row, tm)
        out_off, out_shape = row // tm, jax.ShapeDtypeStruct(buf.shape, buf.dtype)
        aliases = {len(args): 0}
        in_specs, args = in_specs + [pl.BlockSpec(memory_space=pl.ANY)], args + [buf]
    n_add = len(add)

    def body(*refs):
        o_ref, acc_ref = refs[-2:]
        kk = pl.program_id(2)

        @pl.when(kk == 0)
        def _():
            acc_ref[...] = jnp.zeros_like(acc_ref)

        for p in range(n_p):
            acc_ref[...] += _rawdot(refs[2 * p][...], refs[2 * p + 1][...], ca, cb)

        @pl.when(kk == nk - 1)
        def _():
            total = acc_ref[...]
            for r in refs[2 * n_p:2 * n_p + n_add]:
                total = total + r[...]
            o_ref[...] = total.astype(o_ref.dtype)

    return pl.pallas_call(
        body, grid=(m // tm, n // tn, nk), in_specs=in_specs,
        out_specs=pl.BlockSpec((tm, tn), lambda i, j, kk: (i + out_off, j)),
        out_shape=out_shape, input_output_aliases=aliases,
        scratch_shapes=[pltpu.VMEM((tm, tn), F32)],
        compiler_params=pltpu.CompilerParams(dimension_semantics=("parallel", "parallel", "arbitrary"),
                                             vmem_limit_bytes=VMEM_LIMIT),
        name=name)(*args)


class Op:
    def __init__(self, arr, block, imap, grad=False, acc=False, first=None, gdtype=F32, gshape=None, gimap=None):
        self.arr, self.block, self.imap = arr, block, imap
        self.grad, self.acc, self.first, self.gdtype = grad, acc, first, gdtype
        self.gshape = arr.shape if gshape is None else gshape
        self.gimap = imap if gimap is None else gimap


def _row_op(arr, tm, grad=False, gdtype=F32):
    return Op(arr, (tm, arr.shape[1]), lambda i: (i, 0), grad=grad, gdtype=gdtype)


def _par_op(arr, grad=False):
    nd = arr.ndim
    return Op(arr, arr.shape, lambda i: (0,) * nd, grad=grad, acc=True, first=lambda ids: ids[0] == 0)


def _load(ref):
    v = ref[...]
    return v.astype(F32) if jnp.issubdtype(v.dtype, jnp.floating) else v


def _store_heads(ref, val):
    rows = val.shape[0]
    if len(ref.shape) == 2:
        ref[...] = val.astype(ref.dtype)
    elif ref.shape[1] == rows:
        d = ref.shape[2]
        for h in range(ref.shape[0]):
            ref[h] = val[:, d * h:d * (h + 1)].astype(ref.dtype)
    else:
        d = ref.shape[1]
        assert ref.shape[2] == rows and d != rows, (ref.shape, val.shape)
        for h in range(ref.shape[0]):
            ref[h] = val[:, d * h:d * (h + 1)].T.astype(ref.dtype)


def _load_heads(ref, transposed=False):
    if len(ref.shape) == 2:
        return ref[...].astype(F32)
    parts = [ref[h].astype(F32) for h in range(ref.shape[0])]
    return jnp.concatenate([p.T for p in parts] if transposed else parts, axis=-1)


def stage_fwd(name, fn, ops, outs, grid):
    n_in = len(ops)

    def body(*refs):
        res = fn(*[_load(r) for r in refs[:n_in]])
        for r, o in zip(refs[n_in:], res):
            _store_heads(r, o)

    return pl.pallas_call(
        body, grid=grid, in_specs=[pl.BlockSpec(o.block, o.imap) for o in ops],
        out_specs=[pl.BlockSpec(b, im) for (_, _, b, im) in outs],
        out_shape=[jax.ShapeDtypeStruct(s, d) for (s, d, _, _) in outs],
        compiler_params=pltpu.CompilerParams(dimension_semantics=("parallel",) * len(grid),
                                             vmem_limit_bytes=VMEM_LIMIT),
        name=name)(*[o.arr for o in ops])


def stage_bwd(name, fn, ops, cts, grid, value_acc=False):
    n_in = len(ops)
    ct_flat = [(c + (False,))[:4] for group in cts if group is not None for c in group]
    n_ct = len(ct_flat)
    diff = [i for i, o in enumerate(ops) if o.grad]
    any_acc = value_acc or any(ops[i].acc for i in diff)
    ngrid = len(grid)

    def body(*refs):
        ids = [pl.program_id(a) for a in range(ngrid)]
        vals = [_load(r) for r in refs[:n_in]]
        ct_refs = refs[n_in:n_in + n_ct]
        out_refs = refs[n_in + n_ct:]

        def f(*dv):
            full = list(vals)
            for i, v in zip(diff, dv):
                full[i] = v
            return tuple(fn(*full))

        res, vjp = jax.vjp(f, *[vals[i] for i in diff])
        ct, pos = [], 0
        for group, r in zip(cts, res):
            if group is None:
                ct.append(jnp.ones_like(r))
            else:
                tot = None
                for _ in group:
                    c = _load_heads(ct_refs[pos], ct_flat[pos][3])
                    tot = c if tot is None else tot + c
                    pos += 1
                ct.append(tot)
        grads = vjp(tuple(ct))
        for i, g, r in zip(diff, grads, out_refs):
            if ops[i].acc:
                @pl.when(ops[i].first(ids))
                def _(r=r):
                    r[...] = jnp.zeros_like(r)

                r[...] += g.astype(r.dtype)
            else:
                r[...] = g.astype(r.dtype)
        if value_acc:
            r = out_refs[len(diff)]

            @pl.when(ids[0] == 0)
            def _():
                r[...] = jnp.zeros_like(r)

            r[...] += res[0]

    in_specs = [pl.BlockSpec(o.block, o.imap) for o in ops] + [pl.BlockSpec(b, im) for (_, b, im, _) in ct_flat]
    out_specs = [pl.BlockSpec(ops[i].block, ops[i].gimap) for i in diff]
    out_shape = [jax.ShapeDtypeStruct(ops[i].gshape, ops[i].gdtype) for i in diff]
    if value_acc:
        out_specs.append(pl.BlockSpec((1, 1), lambda *ids: (0, 0)))
        out_shape.append(jax.ShapeDtypeStruct((1, 1), F32))
    sem = ("arbitrary",) * ngrid if any_acc else ("parallel",) * ngrid
    return pl.pallas_call(
        body, grid=grid, in_specs=in_specs, out_specs=out_specs, out_shape=out_shape,
        compiler_params=pltpu.CompilerParams(dimension_semantics=sem, vmem_limit_bytes=VMEM_LIMIT),
        name=name)(*[o.arr for o in ops], *[a for (a, _, _, _) in ct_flat])


def _sigmoid(x):
    return 0.5 * jnp.tanh(0.5 * x) + 0.5


def _silu(x):
    return x * _sigmoid(x)


def _gelu_tanh(x):
    return 0.5 * x * (1.0 + jnp.tanh(math.sqrt(2.0 / math.pi) * (x + 0.044715 * (x * x * x))))


def _ln(x, g, b):
    mu = jnp.mean(x, axis=-1, keepdims=True)
    xc = x - mu
    var = jnp.mean(xc * xc, axis=-1, keepdims=True)
    return xc * lax.rsqrt(var + LN_EPS) * g + b


def _rms(x, g):
    ms = jnp.mean(x * x, axis=-1, keepdims=True)
    return x * lax.rsqrt(ms + RMS_EPS) * g


def _swap_halves(x, half):
    width = x.shape[-1]
    lane = lax.broadcasted_iota(jnp.int32, x.shape, 1)
    return jnp.where(lane % (2 * half) < half, pltpu.roll(x, width - half, 1), pltpu.roll(x, half, 1))


def _make_swap(half):
    @jax.custom_vjp
    def swap(x):
        return _swap_halves(x, half)

    swap.defvjp(lambda x: (_swap_halves(x, half), None), lambda _, dy: (_swap_halves(dy, half),))
    return swap


_swap16, _swap8 = _make_swap(16), _make_swap(8)


def _rope(x, cos, sin_signed, swap):
    return x * cos + swap(x) * sin_signed


def _dot_f32(a, b):
    return jnp.dot(a, b, preferred_element_type=F32, precision=lax.Precision.HIGHEST)


def fn_ln(x, g, b):
    return (_ln(x, g, b),)


def _twice(fn):
    def f(*a):
        (y,) = fn(*a)
        return y, y
    return f


PROJ_W = 2048
P_A, P_Q, P_K, P_V, P_C, P_CQ, P_CKV, P_KR = 0, 512, 768, 896, 1024, 1536, 1792, 1920
CQ_PAD = 256
_CQ_END = P_CQ + MLA_Q_LORA


def _pad_w_in(wt):
    z = lambda n: jnp.zeros((n, wt.shape[1]), wt.dtype)
    return jnp.concatenate([wt[:_CQ_END], z(P_CKV - _CQ_END), wt[_CQ_END:], z(PROJ_W - P_KR - MLA_ROPE)], axis=0)


def _unpad_w_in(gt):
    return jnp.concatenate([gt[:_CQ_END], gt[P_CKV:P_KR + MLA_ROPE]], axis=0)


def fn_pre(proj, tab_q, tab_d, seg, place, qng, kng, sg, sb, sw, sbt, mqn, wuq, mkvn, wukv):
    tm = proj.shape[0]
    aglu = proj[:, P_A:P_A + GROUP_W] * _sigmoid(proj[:, P_A + GROUP_W:P_Q])
    b_q, b_k, b_v = proj[:, P_Q:P_K], proj[:, P_K:P_V], proj[:, P_V:P_C]
    cos_q, sin_q = tab_q[:, :GROUP_W], tab_q[:, GROUP_W:]
    q = b_q * lax.rsqrt(_dot_f32(b_q * b_q, seg) + RMS_EPS) * qng
    q = _rope(q, cos_q, sin_q, _swap16)
    k = b_k * lax.rsqrt(_dot_f32(b_k * b_k, seg[:128, :128]) + RMS_EPS) * kng
    k = _rope(k, cos_q[:, :128], sin_q[:, :128], _swap16)
    c = _gelu_tanh(proj[:, P_C:P_CQ])
    u, sv = c[:, :GROUP_W], _ln(c[:, GROUP_W:], sg, sb)
    group = lax.broadcasted_iota(jnp.int32, (CHUNK, GROUP_W), 1) // HEAD_DIM
    rows = []
    for n in range(tm // CHUNK):
        svn = sv[CHUNK * n:CHUNK * (n + 1)]
        acc = jnp.zeros((CHUNK, GROUP_W), F32)
        for g in range(SGU_GROUPS):
            acc = acc + jnp.where(group == g, mm_nn(sw[CHUNK * g:CHUNK * (g + 1)], svn) + sbt[:, g:g + 1], 0.0)
        rows.append(acc)
    o_c = u * jnp.concatenate(rows, axis=0)
    d_cq, d_ckv, d_kr = proj[:, P_CQ:P_CKV], proj[:, P_CKV:P_KR], proj[:, P_KR:PROJ_W]
    cqn = d_cq * lax.rsqrt(jnp.sum(d_cq * d_cq, axis=-1, keepdims=True) * (1.0 / MLA_Q_LORA) + RMS_EPS) * mqn
    cos_d = jnp.concatenate([tab_d[:, :MLA_DK_PAD]] * MLA_HEADS, axis=-1)
    sin_d = jnp.concatenate([tab_d[:, MLA_DK_PAD:]] * MLA_HEADS, axis=-1)
    qf = _rope(mm_nn(cqn, wuq), cos_d, sin_d, _swap8)
    kvd = mm_nn(_rms(d_ckv, mkvn), wukv)
    kf = _rope(kvd[:, :MLA_HEADS * MLA_DK_PAD] + _dot_f32(d_kr, place), cos_d, sin_d, _swap8)
    return aglu, q, k, b_v, o_c, qf, kf, kvd[:, MLA_HEADS * MLA_DK_PAD:]


def fn_aconv(win, w, b, g, beta):
    tm = win.shape[0] - 2 * CONV_A_HALO
    off = CONV_A_HALO - CONV_A_WIDTH // 2
    acc = None
    for r in range(8):
        rolled = win if r == 0 else _roll_rows(win, -r)
        for kk in range(CONV_A_WIDTH):
            if (off + kk) % 8 == r:
                base = off + kk - r
                term = rolled[base:base + tm] * w[kk:kk + 1, :]
                acc = term if acc is None else acc + term
    return (_silu(_ln(acc + b, g, beta)),)


def fn_resln(h, r, g, b):
    return (_ln(DEEPNORM_ALPHA * h + r, g, b),)


@functools.partial(jax.custom_vjp, nondiff_argnums=(1,))
def _roll_rows(x, shift):
    return pltpu.roll(x, shift % x.shape[0], 0)


_roll_rows.defvjp(lambda x, shift: (pltpu.roll(x, shift % x.shape[0], 0), None),
                  lambda shift, _, dy: (pltpu.roll(dy, (-shift) % dy.shape[0], 0),))


def _shift_down(x):
    row = lax.broadcasted_iota(jnp.int32, x.shape, 0)
    return jnp.where(row == 0, 0.0, _roll_rows(x, 1))


def _shift_up(x):
    row = lax.broadcasted_iota(jnp.int32, x.shape, 0)
    return jnp.where(row == x.shape[0] - 1, 0.0, _roll_rows(x, -1))


def fn_ffnconv(u1, u2, w1, w2, b1, b2):
    c1 = _shift_down(u1) * w1[0:1] + u1 * w1[1:2] + _shift_up(u1) * w1[2:3] + b1
    c2 = _shift_down(u2) * w2[0:1] + u2 * w2[1:2] + _shift_up(u2) * w2[2:3] + b2
    return (_silu(c1) * c2,)


def fn_final(h, r, t, g, b):
    y = _ln(DEEPNORM_ALPHA * h + r, g, b)
    err = (y - t) * (y - t)
    return (0.5 * jnp.sum(jnp.mean(err, axis=-1, keepdims=True), axis=0, keepdims=True),)


def _rope_tables(seq):
    n_rows = seq // GRID_W
    lane128 = jnp.arange(128)

    def tile_tables(j, rotated, half):
        inv = ROPE_THETA ** (-(j % half).astype(F32) / half)
        by_row, by_col = rotated & (j < 2 * half), rotated & (j >= 2 * half)
        sign = jnp.where(j % (2 * half) < half, -1.0, 1.0)
        ar = jnp.arange(n_rows, dtype=F32)[:, None] * inv[None, :]
        ac = jnp.arange(GRID_W, dtype=F32)[:, None] * inv[None, :]
        grid = lambda r, c: (jnp.where(by_row, r, 0.0)[:, None, :] + jnp.where(by_col, c, 0.0)[None, :, :])
        cos = grid(jnp.cos(ar), jnp.cos(ac)) + jnp.where(rotated, 0.0, 1.0)
        sin = grid(sign * jnp.sin(ar), sign * jnp.sin(ac))
        return cos.reshape(seq, 128), sin.reshape(seq, 128)

    cos_b, sin_b = tile_tables(lane128 % HEAD_DIM, lane128 >= 0, HEAD_DIM // 4)
    tab_q = jnp.concatenate([cos_b] * (GROUP_W // 128) + [sin_b] * (GROUP_W // 128), axis=-1)
    tab_d = jnp.concatenate(tile_tables(lane128 - MLA_NOPE, (lane128 >= MLA_NOPE) & (lane128 < MLA_NOPE + MLA_ROPE),
                                        MLA_ROPE // 4), axis=-1)
    lane = jnp.arange(GROUP_W)
    seg = jnp.where(lane[:, None] // HEAD_DIM == lane[None, :] // HEAD_DIM, 1.0 / HEAD_DIM, 0.0).astype(F32)
    src, dst = jnp.arange(128)[:, None], jnp.arange(MLA_HEADS * MLA_DK_PAD)[None, :]
    place = jnp.where((src < MLA_ROPE) & (dst % MLA_DK_PAD == MLA_NOPE + src), 1.0, 0.0).astype(F32)
    return tab_q, tab_d, seg, place


def _pre_ops(proj, tabs, kp, grad):
    tm = ROW_TILE
    ops = [_row_op(proj, tm, grad=grad, gdtype=MXU_DTYPE), _row_op(tabs[0], tm), _row_op(tabs[1], tm),
           _par_op(tabs[2]), _par_op(tabs[3])]
    ops += [_par_op(kp[n], grad=grad) for n in ('qng', 'kng', 'sg', 'sb', 'sw', 'sbt', 'mqn', 'wuq', 'mkvn', 'wukv')]
    return ops


PRE_OUTS = ((0, GROUP_W), (GQA_HEADS, HEAD_DIM), (GQA_KV_HEADS, HEAD_DIM), (GQA_KV_HEADS, HEAD_DIM), (0, GROUP_W),
            (MLA_HEADS, MLA_DK_PAD), (MLA_HEADS, MLA_DK_PAD), (MLA_HEADS, MLA_V))


def _pre_out_specs(seq, tm):
    specs = []
    for j, (heads, w) in enumerate(PRE_OUTS):
        if heads:
            specs.append(((heads, seq, w), (heads, tm, w), lambda i: (0, i, 0)))
        elif j == 0:
            specs.append(((seq + 2 * tm, w), (tm, w), lambda i: (i + 1, 0)))
        else:
            specs.append(((seq, w), (tm, w), lambda i: (i, 0)))
    return specs


def _conv_window(i, tm, seq):
    rows = pl.ds(pl.multiple_of((i + 1) * tm - CONV_A_HALO, 8), tm + 2 * CONV_A_HALO)
    tok = i * tm - CONV_A_HALO + lax.broadcasted_iota(jnp.int32, (tm + 2 * CONV_A_HALO, 1), 0)
    return rows, jnp.logical_and(tok >= 0, tok < seq)


PRE_KV = (2, 3, 6, 7)


def _transposed_spec(heads, w, seq, tm):
    return (heads, w, seq), (heads, w, tm), lambda i: (0, 0, i)


def pre_fwd(proj, tabs, kp, tag):
    seq = proj.shape[0]
    tm = ROW_TILE
    dts = (F32,) + (MXU_DTYPE,) * 7
    outs = [(shape, dt, block, imap) for (shape, block, imap), dt in zip(_pre_out_specs(seq, tm), dts)]
    return stage_fwd("pre_fwd" + tag, fn_pre, _pre_ops(proj, tabs, kp, False), outs, (seq // tm,))


def pre_bwd(proj, tabs, kp, cts, tag):
    seq = proj.shape[0]
    tm = ROW_TILE
    ct = []
    for j, (c, (_, block, imap)) in enumerate(zip(cts, _pre_out_specs(seq, tm))):
        if isinstance(c, tuple):
            ct.append([c])
        elif j in PRE_KV:
            ct.append([(c,) + _transposed_spec(*PRE_OUTS[j], seq, tm)[1:] + (True,)])
        else:
            ct.append([(c, block, imap)])
    return stage_bwd("pre_bwd" + tag, fn_pre, _pre_ops(proj, tabs, kp, True), ct, (seq // tm,))


def _aconv_ops(kp, grad):
    return [_par_op(kp[n], grad=grad) for n in ('caw', 'cab', 'lag', 'lab')]


def aconv_fwd(aglu_pad, kp, tag):
    tm = ROW_TILE
    seq = aglu_pad.shape[0] - 2 * tm
    n_par = 4

    def body(x_ref, *refs):
        rows, is_token = _conv_window(pl.program_id(0), tm, seq)
        win = jnp.where(is_token, x_ref[rows, :], 0.0)
        (o,) = fn_aconv(win, *[_load(r) for r in refs[:n_par]])
        refs[n_par][...] = o.astype(refs[n_par].dtype)

    pars = _aconv_ops(kp, False)
    return pl.pallas_call(
        body, grid=(seq // tm,),
        in_specs=[pl.BlockSpec(aglu_pad.shape, lambda i: (0, 0))] + [pl.BlockSpec(o.block, o.imap) for o in pars],
        out_specs=pl.BlockSpec((tm, GROUP_W), lambda i: (i, 0)),
        out_shape=jax.ShapeDtypeStruct((seq, GROUP_W), MXU_DTYPE),
        compiler_params=pltpu.CompilerParams(dimension_semantics=("parallel",), vmem_limit_bytes=VMEM_LIMIT),
        name="aconv_fwd" + tag)(aglu_pad, *[o.arr for o in pars])


def aconv_bwd(aglu_pad, kp, d_oa, tag):
    tm = ROW_TILE
    seq = aglu_pad.shape[0] - 2 * tm
    n_par = 4

    def body(x_ref, *refs):
        i = pl.program_id(0)
        rows, is_token = _conv_window(i, tm, seq)
        pars = [_load(r) for r in refs[:n_par]]
        ct = refs[n_par][...].astype(F32)
        outs = refs[n_par + 1:]
        _, vjp = jax.vjp(lambda *a: fn_aconv(*a), jnp.where(is_token, x_ref[rows, :], 0.0), *pars)
        grads = vjp((ct,))

        @pl.when(i == 0)
        def _():
            for r in outs:
                r[...] = jnp.zeros_like(r)

        outs[0][rows, :] += grads[0]
        for r, g in zip(outs[1:], grads[1:]):
            r[...] += g

    pars = _aconv_ops(kp, True)
    whole = pl.BlockSpec(aglu_pad.shape, lambda i: (0, 0))
    par_specs = [pl.BlockSpec(o.block, o.imap) for o in pars]
    return pl.pallas_call(
        body, grid=(seq // tm,),
        in_specs=[whole] + par_specs + [pl.BlockSpec((tm, GROUP_W), lambda i: (i, 0))],
        out_specs=[whole] + par_specs,
        out_shape=[jax.ShapeDtypeStruct(aglu_pad.shape, F32)] + [jax.ShapeDtypeStruct(o.arr.shape, F32) for o in pars],
        compiler_params=pltpu.CompilerParams(dimension_semantics=("arbitrary",), vmem_limit_bytes=VMEM_LIMIT),
        name="aconv_bwd" + tag)(aglu_pad, *[o.arr for o in pars], d_oa)


ATTN_TQ_FWD = 512
ATTN_TQ = 256
ATTN_TK = 512


def attn_fwd(q3, k3, v3t, scale, tag):
    heads, seq, dk = q3.shape
    group = heads // k3.shape[0]
    kv_per_pair = 2 // group
    dv = v3t.shape[1]
    tq, tk = min(ATTN_TQ_FWD, seq), min(ATTN_TK, seq)
    n_chunks = seq // tk
    log2e = math.log2(math.e)

    def one_head(q, k_ref, vt_ref):
        scores = lambda c: _rawdot(k_ref[pl.ds(c * tk, tk), :], q, 1, 1)
        m, l, acc = jnp.full((1, tq), -jnp.inf, F32), jnp.zeros((1, tq), F32), jnp.zeros((dv, tq), F32)
        s_next = scores(0)
        for c in range(n_chunks):
            s_cur, s_next = s_next, (scores(c + 1) if c + 1 < n_chunks else None)
            t = s_cur * (scale * log2e)
            m_new = jnp.maximum(m, jnp.max(t, axis=0, keepdims=True))
            alpha = jnp.exp2(m - m_new)
            p = jnp.exp2(t - m_new)
            l = alpha * l + jnp.sum(p, axis=0, keepdims=True)
            acc = alpha * acc + _rawdot(vt_ref[:, c * tk:(c + 1) * tk], p, 1, 0)
            m = m_new
        return (acc * (1.0 / l)).T, (m * (1.0 / log2e) + jnp.log(l)).T

    def body(q_ref, k_ref, v_ref, o_ref, lse_ref):
        outs = []
        for h in range(2):
            o, lse = one_head(q_ref[h], k_ref.at[h // group], v_ref.at[h // group])
            lse_ref[h] = lse
            outs.append(o)
        o_ref[...] = jnp.concatenate(outs, axis=-1)

    return pl.pallas_call(
        body, grid=(heads // 2, seq // tq),
        in_specs=[pl.BlockSpec((2, tq, dk), lambda j, i: (j, i, 0)),
                  pl.BlockSpec((kv_per_pair, seq, dk), lambda j, i: (j, 0, 0)),
                  pl.BlockSpec((kv_per_pair, dv, seq), lambda j, i: (j, 0, 0))],
        out_specs=[pl.BlockSpec((tq, 2 * dv), lambda j, i: (i, j)),
                   pl.BlockSpec((2, tq, 1), lambda j, i: (j, i, 0))],
        out_shape=[jax.ShapeDtypeStruct((seq, heads * dv), F32), jax.ShapeDtypeStruct((heads, seq, 1), F32)],
        compiler_params=pltpu.CompilerParams(dimension_semantics=("parallel", "parallel"),
                                             vmem_limit_bytes=VMEM_LIMIT),
        name="attn_fwd" + tag)(q3, k3, v3t)


def attn_bwd(q3, k3, v3, o, lse3, do_all, do_col, scale, tag):
    heads, seq, dk = q3.shape
    group = heads // k3.shape[0]
    kv_per_pair = 2 // group
    dv = v3.shape[2]
    tq, tk = min(ATTN_TQ, seq), min(ATTN_TK, seq)
    n_chunks = seq // tk
    log2e = math.log2(math.e)

    def one_head(q, do, o_h, lse, k_ref, v_ref, dk_ref, dv_ref):
        dob = do.astype(MXU_DTYPE)
        do_t, q_t = do.T.astype(MXU_DTYPE), q.astype(F32).T.astype(MXU_DTYPE)
        delta = jnp.sum(do * o_h, axis=-1, keepdims=True)
        lse2 = lse * log2e
        rows = lambda c: pl.ds(c * tk, tk)
        products = lambda c: (_rawdot(q, k_ref[rows(c), :], 1, 1), _rawdot(dob, v_ref[rows(c), :], 1, 1))
        dq = jnp.zeros((tq, dk), F32)
        nxt = products(0)
        for c in range(n_chunks):
            (s_cur, dp_cur), nxt = nxt, (products(c + 1) if c + 1 < n_chunks else None)
            p = jnp.exp2(s_cur * (scale * log2e) - lse2)
            ds = (p * ((dp_cur - delta) * scale)).astype(MXU_DTYPE)
            dv_ref[:, c * tk:(c + 1) * tk] += _rawdot(do_t, p, 1, 0)
            dk_ref[:, c * tk:(c + 1) * tk] += _rawdot(q_t, ds, 1, 0)
            dq = dq + _rawdot(ds, k_ref[rows(c), :], 1, 0)
        return dq

    def body(q_ref, k_ref, v_ref, o_ref, lse_ref, do_ref, dq_ref, dk_ref, dv_ref):
        @pl.when(pl.program_id(1) == 0)
        def _():
            dk_ref[...] = jnp.zeros_like(dk_ref)
            dv_ref[...] = jnp.zeros_like(dv_ref)

        do_pair, o_pair = do_ref[...], o_ref[...]
        for h in range(2):
            kv = h // group
            dq_ref[h] = one_head(q_ref[h], do_pair[:, dv * h:dv * (h + 1)], o_pair[:, dv * h:dv * (h + 1)],
                                 lse_ref[h], k_ref.at[kv], v_ref.at[kv], dk_ref.at[kv], dv_ref.at[kv])

    qspec = lambda d: pl.BlockSpec((2, tq, d), lambda j, i: (j, i, 0))
    kvspec = lambda d: pl.BlockSpec((kv_per_pair, seq, d), lambda j, i: (j, 0, 0))
    kvt_spec = lambda d: pl.BlockSpec((kv_per_pair, d, seq), lambda j, i: (j, 0, 0))
    kvt_shape = lambda a: jax.ShapeDtypeStruct((a.shape[0], a.shape[2], a.shape[1]), F32)
    return pl.pallas_call(
        body, grid=(heads // 2, seq // tq),
        in_specs=[qspec(dk), kvspec(dk), kvspec(dv), pl.BlockSpec((tq, 2 * dv), lambda j, i: (i, j)), qspec(1),
                  pl.BlockSpec((tq, 2 * dv), lambda j, i: (i, do_col + j))],
        out_specs=[qspec(dk), kvt_spec(dk), kvt_spec(dv)],
        out_shape=[jax.ShapeDtypeStruct(q3.shape, F32), kvt_shape(k3), kvt_shape(v3)],
        compiler_params=pltpu.CompilerParams(dimension_semantics=("parallel", "arbitrary"),
                                             vmem_limit_bytes=VMEM_LIMIT),
        name="attn_bwd" + tag)(q3, k3, v3, o, lse3, do_all)


def resln_fwd(h, r, g, b, tag):
    seq, d = h.shape
    tm = min(LN_TILE, seq)
    ops = [_row_op(h, tm), _row_op(r, tm), _par_op(g), _par_op(b)]
    outs = [((seq, d), dt, (tm, d), lambda i: (i, 0)) for dt in (F32, MXU_DTYPE)]
    return stage_fwd("resln_fwd" + tag, _twice(fn_resln), ops, outs, (seq // tm,))


def resln_bwd(h, r, g, b, dys, tag):
    seq, d = h.shape
    tm = min(LN_TILE, seq)
    ops = [_row_op(h, tm, grad=True), _row_op(r, tm, grad=True, gdtype=MXU_DTYPE), _par_op(g, grad=True),
           _par_op(b, grad=True)]
    ct = [[(dy, (tm, d), lambda i: (i, 0)) for dy in dys]]
    return stage_bwd("resln_bwd" + tag, fn_resln, ops, ct, (seq // tm,))


def _ffnconv_ops(up1, up2, w, b, grad):
    seq = up1.shape[0]
    nblk = D_FF // 128
    lo, hi = (lambda j: (0, j)), (lambda j: (0, j + nblk))
    half = lambda a: dict(gshape=(a.shape[0], D_FF), gimap=lo)
    return [Op(up1, (seq, 128), lo, grad=grad, gdtype=MXU_DTYPE), Op(up2, (seq, 128), lo, grad=grad, gdtype=MXU_DTYPE),
            Op(w, (3, 128), lo, grad=grad, **half(w)), Op(w, (3, 128), hi, grad=grad, **half(w)),
            Op(b, (1, 128), lo, grad=grad, **half(b)), Op(b, (1, 128), hi, grad=grad, **half(b))]


def ffnconv_fwd(up1, up2, w, b, tag):
    seq = up1.shape[0]
    outs = [((seq, D_FF), MXU_DTYPE, (seq, 128), lambda j: (0, j))]
    return stage_fwd("ffnconv_fwd" + tag, fn_ffnconv, _ffnconv_ops(up1, up2, w, b, False), outs, (D_FF // 128,))[0]


def ffnconv_bwd(up1, up2, w, b, dact, tag):
    seq = up1.shape[0]
    ct = [[(dact, (seq, 128), lambda j: (0, j))]]
    du1, du2, dw1, dw2, db1, db2 = stage_bwd("ffnconv_bwd" + tag, fn_ffnconv, _ffnconv_ops(up1, up2, w, b, True),
                                             ct, (D_FF // 128,))
    cat = lambda a, b_: jnp.concatenate([a, b_], axis=-1)
    return du1, du2, cat(dw1, dw2), cat(db1, db2)


def final_bwd(h, r, t, g, b, tag):
    seq, d = h.shape
    tm = min(LN_TILE, seq)
    ops = [_row_op(h, tm, grad=True), _row_op(r, tm, grad=True, gdtype=MXU_DTYPE), _row_op(t, tm),
           _par_op(g, grad=True), _par_op(b, grad=True)]
    return stage_bwd("final_bwd" + tag, fn_final, ops, [None], (seq // tm,), value_acc=True)


def _layer_params(wts, l):
    row = lambda a: a.reshape(1, -1)
    wuq = wts['mla_w_uq'][l].reshape(MLA_Q_LORA, MLA_HEADS, MLA_NOPE + MLA_ROPE)
    wuq = jnp.pad(wuq, ((0, CQ_PAD - MLA_Q_LORA), (0, 0), (0, MLA_DK_PAD - MLA_NOPE - MLA_ROPE)))
    wukv = wts['mla_w_ukv'][l].reshape(MLA_KV_LORA, MLA_HEADS, MLA_NOPE + MLA_V)
    wuk = jnp.pad(wukv[:, :, :MLA_NOPE], ((0, 0), (0, 0), (0, MLA_DK_PAD - MLA_NOPE)))
    return dict(
        qng=jnp.tile(row(wts['qk_norm_q'][l]), (1, GQA_HEADS)), kng=jnp.tile(row(wts['qk_norm_k'][l]), (1, GQA_KV_HEADS)),
        sg=row(wts['sgu_ln_g'][l]), sb=row(wts['sgu_ln_b'][l]),
        sw=wts['sgu_w'][l].reshape(SGU_GROUPS * CHUNK, CHUNK), sbt=wts['sgu_b'][l].T,
        mqn=jnp.pad(row(wts['mla_q_norm'][l]), ((0, 0), (0, CQ_PAD - MLA_Q_LORA))),
        wuq=wuq.reshape(CQ_PAD, MLA_HEADS * MLA_DK_PAD), mkvn=row(wts['mla_kv_norm'][l]),
        wukv=jnp.concatenate([wuk.reshape(MLA_KV_LORA, -1), wukv[:, :, MLA_NOPE:].reshape(MLA_KV_LORA, -1)], axis=1),
        caw=wts['conv_a_w'][l], cab=row(wts['conv_a_b'][l]), lag=row(wts['ln_a_g'][l]), lab=row(wts['ln_a_b'][l]),
        lmg=row(wts['ln_mix_g'][l]), lmb=row(wts['ln_mix_b'][l]),
        fcw=wts['ffn_conv_w'][l], fcb=row(wts['ffn_conv_b'][l]),
        lfg=row(wts['ln_ffn_g'][l]), lfb=row(wts['ln_ffn_b'][l]))


def local_step(x, target, ln_in, get_wts, mat, hook):
    seq = x.shape[0]
    tm = min(LN_TILE, seq)
    tabs = _rope_tables(seq)
    scale_b = HEAD_DIM ** -0.5
    scale_d = (MLA_NOPE + MLA_ROPE) ** -0.5
    ln_in_g, ln_in_b = ln_in[0].reshape(1, -1), ln_in[1].reshape(1, -1)

    h, h_m = stage_fwd("ln_in_fwd", _twice(fn_ln), [_row_op(x, tm), _par_op(ln_in_g), _par_op(ln_in_b)],
                       [((seq, D_MODEL), dt, (tm, D_MODEL), lambda i: (i, 0)) for dt in (F32, MXU_DTYPE)],
                       (seq // tm,))
    wts = get_wts(h_m[:8, :128].astype(F32) + tabs[0][:8, :128] + tabs[1][:8, :128])
    saved = []
    for l in range(DEPTH):
        tag = f"_l{l}"
        kp, unprep = jax.vjp(lambda w: _layer_params(w, l), wts)
        m = {'w_in': mat(l, 'w_in', h_m)}
        proj = matmul(h_m, m['w_in'], 'nt', F32, "mm_proj" + tag, tn=PROJ_W, tm_max=512)
        aglu_pad, q3, k3, v3, o_c, qd3, kd3, vd3 = pre_fwd(proj, tabs, kp, tag)
        o_a = aconv_fwd(aglu_pad, kp, tag)
        swap = lambda a: a.transpose(0, 2, 1)
        o_b3, lse_b3 = attn_fwd(q3, k3, swap(v3), scale_b, "_b" + tag)
        o_d3, lse_d3 = attn_fwd(qd3, kd3, swap(vd3), scale_d, "_d" + tag)
        o_cat = jnp.concatenate([o_a, o_b3.astype(MXU_DTYPE), o_c, o_d3.astype(MXU_DTYPE)], axis=-1)
        m['w_out'] = mat(l, 'w_out', o_cat)
        mix = matmul(o_cat, m['w_out'], 'nn', F32, "mm_mix" + tag, tn=D_MODEL, tm_max=512)
        h1, h1_m = resln_fwd(h, mix, kp['lmg'], kp['lmb'], "_mix" + tag)
        m['ffn_w_up'] = mat(l, 'ffn_w_up', h1_m)
        wide = dict(tn=D_FF, tm_max=512)
        up1 = matmul(h1_m, m['ffn_w_up'], 'nt', F32, "mm_up1" + tag, b_rows=(0, D_FF), **wide)
        up2 = matmul(h1_m, m['ffn_w_up'], 'nt', F32, "mm_up2" + tag, b_rows=(D_FF, D_FF), **wide)
        act = ffnconv_fwd(up1, up2, kp['fcw'], kp['fcb'], tag)
        m['ffn_w_down'] = mat(l, 'ffn_w_down', act)
        f = matmul(act, m['ffn_w_down'], 'nn', F32, "mm_down" + tag, tn=D_MODEL, tm_max=512)
        saved.append(dict(kp=kp, unprep=unprep, m=m, h=h, h_m=h_m, h1_m=h1_m, proj=proj, o_b3=o_b3, lse_b3=lse_b3,
                          o_d3=o_d3, lse_d3=lse_d3, aglu_pad=aglu_pad, q3=q3, k3=k3, v3=v3, qd3=qd3,
                          kd3=kd3, vd3=vd3, o_cat=o_cat, mix=mix, h1=h1, up1=up1, up2=up2, act=act, f=f))
        if l + 1 < DEPTH:
            h, h_m = resln_fwd(h1, f, kp['lfg'], kp['lfb'], "_ffn" + tag)

    after = lambda a, tok: a if tok is None else a + tok
    small_acc = None
    dh_parts = None
    loss = None
    tok = None
    g_mix = None
    for l in reversed(range(DEPTH)):
        tag = f"_l{l}"
        s = saved[l]
        kp, m = s['kp'], s['m']
        dkp = {}
        lfg = after(kp['lfg'], tok)
        if l == DEPTH - 1:
            dh1_a, df, dkp['lfg'], dkp['lfb'], loss = final_bwd(s['h1'], s['f'], target, lfg, kp['lfb'], tag)
        else:
            dh1_a, df, dkp['lfg'], dkp['lfb'] = resln_bwd(s['h1'], s['f'], lfg, kp['lfb'], dh_parts, "_ffn" + tag)
        g_down = matmul(s['act'], df, 'tn', COMM_DTYPE, "mm_gdown" + tag, tn=D_MODEL)
        dact = matmul(df, m['ffn_w_down'], 'nt', F32, "mm_dact" + tag, tn=D_FF, tm_max=512)
        dup1, dup2, dkp['fcw'], dkp['fcb'] = ffnconv_bwd(s['up1'], s['up2'], kp['fcw'], kp['fcb'], dact, tag)
        dh1 = matmul([dup1, dup2], [m['ffn_w_up']] * 2, 'nn', F32, "mm_dh1" + tag,
                     b_rows=[(0, D_FF), (D_FF, D_FF)], add=[dh1_a], tn=D_MODEL, tm_max=512)
        g_up = matmul(dup1, s['h1_m'], 'tn', COMM_DTYPE, "mm_gup1" + tag, tn=D_MODEL,
                      into=(lax.empty((2 * D_FF, D_MODEL), COMM_DTYPE), 0))
        g_up = matmul(dup2, s['h1_m'], 'tn', COMM_DTYPE, "mm_gup2" + tag, tn=D_MODEL, into=(g_up, D_FF))
        tok = hook(f"ffn{l}", {('ffn_w_down', l): g_down, ('ffn_w_up', l): g_up})
        dh_a, dmix, dkp['lmg'], dkp['lmb'] = resln_bwd(s['h'], s['mix'], after(kp['lmg'], tok), kp['lmb'],
                                                       [dh1], "_mix" + tag)
        g_out = matmul(s['o_cat'], dmix, 'tn', COMM_DTYPE, "mm_gout" + tag, tn=D_MODEL)
        w_out = m['w_out']
        if l == 0:
            w_out = w_out + hook("out0", {('w_out', l): g_out}).astype(w_out.dtype)
        do_cat = matmul(dmix, w_out, 'nt', F32, "mm_docat" + tag, tn=D_MODEL, tm_max=512)
        lse_b3 = s['lse_b3']
        do_c = (do_cat, (ROW_TILE, GROUP_W), lambda i: (i, 2))
        pair_w = 2 * HEAD_DIM
        dq3, dk3, dv3 = attn_bwd(s['q3'], s['k3'], s['v3'], s['o_b3'], lse_b3, do_cat, GROUP_W // pair_w,
                                 scale_b, "_b" + tag)
        dqd3, dkd3, dvd3 = attn_bwd(s['qd3'], s['kd3'], s['vd3'], s['o_d3'], s['lse_d3'], do_cat,
                                    3 * GROUP_W // pair_w, scale_d, "_d" + tag)
        daglu_pad, dkp['caw'], dkp['cab'], dkp['lag'], dkp['lab'] = aconv_bwd(s['aglu_pad'], kp, do_cat, tag)
        cts = [daglu_pad, dq3, dk3, dv3, do_c, dqd3, dkd3, dvd3]
        pre_g = pre_bwd(s['proj'], tabs, kp, cts, tag)
        dproj = pre_g[0]
        for n, g in zip(('qng', 'kng', 'sg', 'sb', 'sw', 'sbt', 'mqn', 'wuq', 'mkvn', 'wukv'), pre_g[1:]):
            dkp[n] = g
        dh_parts = [matmul(dproj, m['w_in'], 'nn', F32, "mm_dh" + tag, add=[dh_a], tn=D_MODEL,
                           tm_max=512)]
        g_in = matmul(dproj, s['h_m'], 'tn', COMM_DTYPE, "mm_gin" + tag, tn=D_MODEL)
        (dw,) = s['unprep'](dkp)
        small_acc = dw if small_acc is None else jax.tree.map(jnp.add, small_acc, dw)
        g_mix = {('w_out', l): g_out, ('w_in', l): _unpad_w_in(g_in)}
        if l > 0:
            tok = hook(f"mix{l}", g_mix)
        else:
            g_mix.pop(('w_out', l))

    g_mix.update({(n, None): small_acc[n] for n in SHARDED if n not in MATMUL_WEIGHTS})
    tok = hook("last", g_mix)
    dx, dg, db = stage_bwd("ln_in_bwd", fn_ln,
                           [_row_op(x, tm, grad=True), _par_op(after(ln_in_g, tok), grad=True),
                            _par_op(ln_in_b, grad=True)],
                           [[(p, (tm, D_MODEL), lambda i: (i, 0)) for p in dh_parts]], (seq // tm,))
    out = {n: small_acc[n] for n in REPLICATED}
    out['ln_in_g'], out['ln_in_b'] = dg.reshape(-1), db.reshape(-1)
    return loss, dx, out


def _peer(x, y, c, r):
    return ((1 - x) if r & 4 else x, (1 - y) if r & 2 else y, (1 - c) if r & 1 else c)


def _exchange_copy(src_ref, land_ref, send_sems, recv_sems, k, gather, x, y, c, r):
    px, py, pc = _peer(x, y, c, r)
    me, peer = 4 * x + 2 * y + c, 4 * px + 2 * py + pc
    src = src_ref if gather else src_ref.at[peer]
    mk = lambda dst: pltpu.make_async_remote_copy(
        src_ref=src, dst_ref=dst, send_sem=send_sems.at[k * (N_DEV - 1) + r - 1],
        recv_sem=recv_sems.at[k * (N_DEV - 1) + r - 1],
        device_id=(px, py, pc), device_id_type=pl.DeviceIdType.MESH)
    return mk(land_ref.at[me]), mk(land_ref.at[peer])


_HBM_SPEC = pl.BlockSpec(memory_space=pltpu.HBM)
_SEM_SPEC = pl.BlockSpec(memory_space=pltpu.SEMAPHORE)


def exchange_start(srcs, gather, groups, name):
    n_t = len(srcs)
    lands =[lax.empty(((N_DEV,) + s.shape) if gt else s.shape, s.dtype) for s, gt in zip(srcs, gather)]

    def body(*refs):
        src_refs, land_refs = refs[:n_t], refs[n_t:2 * n_t]
        sem_refs = refs[2 * n_t:2 * n_t + 2 * len(groups)]
        token = refs[-1]
        x, y, c = lax.axis_index("x"), lax.axis_index("y"), lax.axis_index("c")
        for gi, g in enumerate(groups):
            for k, t in enumerate(g):
                for r in range(1, N_DEV):
                    _exchange_copy(src_refs[t], land_refs[t], sem_refs[2 * gi], sem_refs[2 * gi + 1], k, gather[t],
                                   x, y, c, r)[0].start()
        token[...] = jnp.zeros_like(token)

    sem_shapes = []
    for g in groups:
        sem_shapes += [pltpu.SemaphoreType.DMA((len(g) * (N_DEV - 1),))] * 2
    hbm_shapes = [pltpu.HBM(a.shape, a.dtype) for a in list(srcs) + lands]
    n_sem = len(sem_shapes)
    res = pl.pallas_call(
        body, name=name,
        out_shape=tuple(sem_shapes + hbm_shapes + [jax.ShapeDtypeStruct((8, 128), F32)]),
        in_specs=[_HBM_SPEC] * (2 * n_t),
        out_specs=tuple([_SEM_SPEC] * n_sem + [_HBM_SPEC] * (2 * n_t) + [pl.BlockSpec(memory_space=pltpu.VMEM)]),
        input_output_aliases={i: n_sem + i for i in range(2 * n_t)},
        compiler_params=pltpu.CompilerParams(has_side_effects=pltpu.SideEffectType.DATAFLOW_SIDE_EFFECTING),
    )(*[pltpu.with_memory_space_constraint(a, pltpu.HBM) for a in list(srcs) + lands])
    sems = [(res[2 * gi], res[2 * gi + 1]) for gi in range(len(groups))]
    return sems, list(res[n_sem:n_sem + n_t]), list(res[n_sem + n_t:n_sem + 2 * n_t]), res[-1]


def exchange_wait(sems, srcs, lands, gather, after, name):
    n_t = len(srcs)

    def body(*refs):
        src_refs, land_refs = refs[:n_t], refs[n_t:2 * n_t]
        send_sems, recv_sems = refs[2 * n_t], refs[2 * n_t + 1]
        x, y, c = lax.axis_index("x"), lax.axis_index("y"), lax.axis_index("c")
        for k in range(n_t):
            for r in range(1, N_DEV):
                send, recv = _exchange_copy(src_refs[k], land_refs[k], send_sems, recv_sems, k, gather[k], x, y, c, r)
                send.wait_send()
                recv.wait_recv()

    res = pl.pallas_call(
        body, name=name,
        out_shape=tuple(pltpu.HBM(a.shape, a.dtype) for a in list(srcs) + list(lands)),
        in_specs=[_HBM_SPEC] * (2 * n_t) + [_SEM_SPEC, _SEM_SPEC, pl.BlockSpec(memory_space=pl.ANY)],
        out_specs=tuple([_HBM_SPEC] * (2 * n_t)),
        input_output_aliases={i: i for i in range(2 * n_t)},
        compiler_params=pltpu.CompilerParams(has_side_effects=pltpu.SideEffectType.DATAFLOW_SIDE_EFFECTING),
    )(*srcs, *lands, sems[0], sems[1], after)
    return list(res[:n_t]), list(res[n_t:])


def adamw(parts, own, me, w, m, v, name):
    n_l, n_r, n_c = w.shape
    tr = n_r
    if n_r % 8 == 0:
        for cand in (512, 256, 128, 64, 32, 16, 8):
            if n_r % cand == 0 and cand * n_c * 4 <= 512 * 1024:
                tr = cand
                break
    c1 = 1.0 - ADAM_B1 ** ADAM_STEP
    c2 = 1.0 - ADAM_B2 ** ADAM_STEP
    per_layer = isinstance(parts, (list, tuple))
    n_p = n_l if per_layer else 1
    n_rb = n_r // tr

    def update(g, w_ref, m_ref, v_ref, g_ref, d_ref, nm_ref, nv_ref):
        w_, m_, v_ = w_ref[0], m_ref[0], v_ref[0]
        nm = ADAM_B1 * m_ + (1.0 - ADAM_B1) * g
        nv = ADAM_B2 * v_ + (1.0 - ADAM_B2) * (g * g)
        g_ref[0] = g
        nm_ref[0] = nm
        nv_ref[0] = nv
        d_ref[0] = -ADAM_LR * ((nm / c1) / (jnp.sqrt(nv / c2) + ADAM_EPS) + ADAM_WD * w_)

    def total(me_ref, land, own_blk):
        g = None
        for s in range(N_DEV):
            term = jnp.where(me_ref[0] == s, own_blk, land(s)).astype(F32)
            g = term if g is None else g + term
        return g

    def body(me_ref, *refs):
        p_refs, o_refs, rest = refs[:n_p], refs[n_p:2 * n_p], refs[2 * n_p:]
        if not per_layer:
            update(total(me_ref, lambda s: p_refs[0][s, 0], o_refs[0][0]), *rest)
        else:
            for lay in range(n_l):
                @pl.when(pl.program_id(0) == lay)
                def _(lay=lay):
                    update(total(me_ref, lambda s: p_refs[lay][s], o_refs[lay][...]), *rest)

    blk = pl.BlockSpec((1, tr, n_c), lambda l, r, me_ref: (l, r, 0))
    if per_layer:
        def rows(lay):
            park = 0 if lay > 0 else n_rb - 1
            return lambda l, r: jnp.where(l == lay, r, park)
        p_specs = [pl.BlockSpec((N_DEV, tr, n_c), lambda l, r, me_ref, f=rows(lay): (0, f(l, r), 0))
                   for lay in range(n_l)]
        o_specs = [pl.BlockSpec((None, tr, n_c), lambda l, r, me_ref, f=rows(lay): (me_ref[0], f(l, r), 0))
                   for lay in range(n_l)]
        p_args, o_args = list(parts), list(own)
    else:
        p_specs = [pl.BlockSpec((N_DEV, 1, tr, n_c), lambda l, r, me_ref: (0, l, r, 0))]
        o_specs = [pl.BlockSpec((None, 1, tr, n_c), lambda l, r, me_ref: (me_ref[0], l, r, 0))]
        p_args, o_args = [parts], [own]
    return pl.pallas_call(
        body,
        grid_spec=pltpu.PrefetchScalarGridSpec(
            num_scalar_prefetch=1, grid=(n_l, n_rb), in_specs=p_specs + o_specs + [blk, blk, blk],
            out_specs=[blk] * 4),
        out_shape=[jax.ShapeDtypeStruct(w.shape, F32)] * 4,
        compiler_params=pltpu.CompilerParams(dimension_semantics=("arbitrary", "arbitrary"),
                                             vmem_limit_bytes=VMEM_LIMIT),
        name=name)(me.reshape(1).astype(jnp.int32), *p_args, *o_args, w, m, v)


def adamw_replicated(lands, own, ws, ms, vs, loss_land, loss_own):
    n_t = len(lands)
    c1 = 1.0 - ADAM_B1 ** ADAM_STEP
    c2 = 1.0 - ADAM_B2 ** ADAM_STEP

    def body(*refs):
        ins, outs = refs[:5 * n_t + 2], refs[5 * n_t + 2:]
        me = 4 * lax.axis_index("x") + 2 * lax.axis_index("y") + lax.axis_index("c")

        def total(land_ref, own_ref):
            g = None
            for s in range(N_DEV):
                term = jnp.where(me == s, own_ref[...], land_ref[s])
                g = term if g is None else g + term
            return g

        for t in range(n_t):
            land_ref, own_ref, w_ref, m_ref, v_ref = ins[5 * t:5 * t + 5]
            g = total(land_ref, own_ref)
            nm = ADAM_B1 * m_ref[...] + (1.0 - ADAM_B1) * g
            nv = ADAM_B2 * v_ref[...] + (1.0 - ADAM_B2) * (g * g)
            g_ref, d_ref, nm_ref, nv_ref = outs[4 * t:4 * t + 4]
            g_ref[...] = g
            nm_ref[...] = nm
            nv_ref[...] = nv
            d_ref[...] = -ADAM_LR * ((nm / c1) / (jnp.sqrt(nv / c2) + ADAM_EPS) + ADAM_WD * w_ref[...])
        outs[4 * n_t][...] = total(ins[5 * n_t], ins[5 * n_t + 1])

    args = []
    for t in range(n_t):
        args += [lands[t], own[t], ws[t], ms[t], vs[t]]
    out_shape = []
    for t in range(n_t):
        out_shape += [jax.ShapeDtypeStruct(ws[t].shape, F32)] * 4
    out_shape.append(jax.ShapeDtypeStruct(loss_own.shape, F32))
    res = pl.pallas_call(body, out_shape=out_shape,
                         compiler_params=pltpu.CompilerParams(vmem_limit_bytes=VMEM_LIMIT),
                         name="adamw_replicated")(*args, loss_land, loss_own)
    return [tuple(res[4 * t:4 * t + 4]) for t in range(n_t)], res[-1]


def _shard_slots(g, axis):
    if axis == 1:
        return g.reshape(g.shape[0], N_DEV, g.shape[1] // N_DEV, g.shape[2]).transpose(1, 0, 2, 3)
    return g.reshape(g.shape[0], g.shape[1], N_DEV, g.shape[2] // N_DEV).transpose(2, 0, 1, 3)


def _unshard(slots, axis):
    if axis == 1:
        return slots.transpose(1, 0, 2, 3).reshape(slots.shape[1], -1, slots.shape[3])
    return slots.transpose(1, 2, 0, 3).reshape(slots.shape[1], slots.shape[2], -1)


def kernel(x, ln_in_g, ln_in_b, w_in, conv_a_w, conv_a_b, ln_a_g, ln_a_b, qk_norm_q, qk_norm_k, sgu_ln_g, sgu_ln_b, sgu_w, sgu_b, mla_q_norm, mla_w_uq, mla_kv_norm, mla_w_ukv, w_out, ln_mix_g, ln_mix_b, ffn_w_up, ffn_conv_w, ffn_conv_b, ffn_w_down, ln_ffn_g, ln_ffn_b, loss_target, m_ln_in_g, m_ln_in_b, m_w_in, m_conv_a_w, m_conv_a_b, m_ln_a_g, m_ln_a_b, m_qk_norm_q, m_qk_norm_k, m_sgu_ln_g, m_sgu_ln_b, m_sgu_w, m_sgu_b, m_mla_q_norm, m_mla_w_uq, m_mla_kv_norm, m_mla_w_ukv, m_w_out, m_ln_mix_g, m_ln_mix_b, m_ffn_w_up, m_ffn_conv_w, m_ffn_conv_b, m_ffn_w_down, m_ln_ffn_g, m_ln_ffn_b, v_ln_in_g, v_ln_in_b, v_w_in, v_conv_a_w, v_conv_a_b, v_ln_a_g, v_ln_a_b, v_qk_norm_q, v_qk_norm_k, v_sgu_ln_g, v_sgu_ln_b, v_sgu_w, v_sgu_b, v_mla_q_norm, v_mla_w_uq, v_mla_kv_norm, v_mla_w_ukv, v_w_out, v_ln_mix_g, v_ln_mix_b, v_ffn_w_up, v_ffn_conv_w, v_ffn_conv_b, v_ffn_w_down, v_ln_ffn_g, v_ln_ffn_b):
    local = dict(ln_in_g=ln_in_g, ln_in_b=ln_in_b, w_in=w_in, conv_a_w=conv_a_w, conv_a_b=conv_a_b, ln_a_g=ln_a_g, ln_a_b=ln_a_b, qk_norm_q=qk_norm_q, qk_norm_k=qk_norm_k, sgu_ln_g=sgu_ln_g, sgu_ln_b=sgu_ln_b, sgu_w=sgu_w, sgu_b=sgu_b, mla_q_norm=mla_q_norm, mla_w_uq=mla_w_uq, mla_kv_norm=mla_kv_norm, mla_w_ukv=mla_w_ukv, w_out=w_out, ln_mix_g=ln_mix_g, ln_mix_b=ln_mix_b, ffn_w_up=ffn_w_up, ffn_conv_w=ffn_conv_w, ffn_conv_b=ffn_conv_b, ffn_w_down=ffn_w_down, ln_ffn_g=ln_ffn_g, ln_ffn_b=ln_ffn_b)
    mom = dict(ln_in_g=m_ln_in_g, ln_in_b=m_ln_in_b, w_in=m_w_in, conv_a_w=m_conv_a_w, conv_a_b=m_conv_a_b, ln_a_g=m_ln_a_g, ln_a_b=m_ln_a_b, qk_norm_q=m_qk_norm_q, qk_norm_k=m_qk_norm_k, sgu_ln_g=m_sgu_ln_g, sgu_ln_b=m_sgu_ln_b, sgu_w=m_sgu_w, sgu_b=m_sgu_b, mla_q_norm=m_mla_q_norm, mla_w_uq=m_mla_w_uq, mla_kv_norm=m_mla_kv_norm, mla_w_ukv=m_mla_w_ukv, w_out=m_w_out, ln_mix_g=m_ln_mix_g, ln_mix_b=m_ln_mix_b, ffn_w_up=m_ffn_w_up, ffn_conv_w=m_ffn_conv_w, ffn_conv_b=m_ffn_conv_b, ffn_w_down=m_ffn_w_down, ln_ffn_g=m_ln_ffn_g, ln_ffn_b=m_ln_ffn_b)
    var = dict(ln_in_g=v_ln_in_g, ln_in_b=v_ln_in_b, w_in=v_w_in, conv_a_w=v_conv_a_w, conv_a_b=v_conv_a_b, ln_a_g=v_ln_a_g, ln_a_b=v_ln_a_b, qk_norm_q=v_qk_norm_q, qk_norm_k=v_qk_norm_k, sgu_ln_g=v_sgu_ln_g, sgu_ln_b=v_sgu_ln_b, sgu_w=v_sgu_w, sgu_b=v_sgu_b, mla_q_norm=v_mla_q_norm, mla_w_uq=v_mla_w_uq, mla_kv_norm=v_mla_kv_norm, mla_w_ukv=v_mla_w_ukv, w_out=v_w_out, ln_mix_g=v_ln_mix_g, ln_mix_b=v_ln_mix_b, ffn_w_up=v_ffn_w_up, ffn_conv_w=v_ffn_conv_w, ffn_conv_b=v_ffn_conv_b, ffn_w_down=v_ffn_w_down, ln_ffn_g=v_ln_ffn_g, ln_ffn_b=v_ln_ffn_b)

    me = 4 * lax.axis_index("x") + 2 * lax.axis_index("y") + lax.axis_index("c")

    def own_slot(slots, block):
        return lax.dynamic_update_slice(slots, block[None], (me,) + (0,) * block.ndim)

    small_sharded = [n for n in SHARDED if n not in MATMUL_WEIGHTS]
    big_order = [(n, l) for l in range(DEPTH) for n in MATMUL_WEIGHTS]
    send_view = lambda n, l: (local[n].transpose(0, 2, 1)[l] if n in TRANSPOSED else local[n][l]).astype(COMM_DTYPE)
    srcs = [send_view(*big_order[0])] + [local[n] for n in small_sharded]
    srcs += [send_view(n, l) for (n, l) in big_order[1:]]
    n_first = 1 + len(small_sharded)
    groups = [list(range(n_first))] + [[n_first + j] for j in range(len(big_order) - 1)]
    g_sems, g_srcs, g_lands, tok0 = exchange_start(srcs, [True] * len(srcs), groups, "gather_start")
    tok0 = tok0[0, 0]
    pending = {key: gi for gi, key in enumerate(big_order)}

    def finish(gi, after):
        idx = groups[gi]
        _, lands = exchange_wait(g_sems[gi], [g_srcs[t] for t in idx], [g_lands[t] for t in idx], [True] * len(idx),
                                 after, f"gather_wait{gi}")
        return [own_slot(ld, srcs[t]) for ld, t in zip(lands, idx)]

    first = []

    opt_view = {n: tuple(a.transpose(0, 2, 1) for a in (local[n], mom[n], var[n])) for n in TRANSPOSED}

    def get_wts(after):
        for views in opt_view.values():
            after = after + sum(a[0, :8, :128] for a in views)
        first.extend(finish(0, after))
        wts = {n: local[n] for n in REPLICATED}
        for n, slots in zip(small_sharded, first[1:]):
            wts[n] = _unshard(slots, SHARDED[n])
        return wts

    def unshard_layer(slots, n):
        if SHARDED[n] == 1 or n in TRANSPOSED:
            return slots.reshape(-1, slots.shape[2])
        return slots.transpose(1, 0, 2).reshape(slots.shape[1], -1)

    def mat(l, n, after):
        gi = pending[(n, l)]
        slots = first[0] if gi == 0 else finish(gi, after)[0]
        w = unshard_layer(slots, n).astype(MXU_DTYPE)
        return _pad_w_in(w) if n == 'w_in' else w

    started = []

    def hook(key, grads):
        tensors = []
        for (n, l), g in grads.items():
            if l is None:
                tensors.append(((n, l), _shard_slots(g, SHARDED[n])))
            elif n in TRANSPOSED:
                tensors.append(((n, l), g.reshape(N_DEV, g.shape[0] // N_DEV, g.shape[1]).astype(COMM_DTYPE)))
            else:
                tensors.append(((n, l), _shard_slots(g[None], SHARDED[n])[:, 0].astype(COMM_DTYPE)))
        sems, s_srcs, s_lands, tok = exchange_start([a for _, a in tensors], [False] * len(tensors),
                                                    [list(range(len(tensors)))], "scatter_start_" + key)
        started.append((key, [k for k, _ in tensors], sems[0], s_srcs, s_lands))
        return tok[0, 0]

    loss, dx, grads = local_step(x[0], loss_target[0], (local['ln_in_g'] + tok0, local['ln_in_b']), get_wts, mat, hook)

    as2d = lambda a: a.reshape(-1, a.shape[-1]) if a.ndim > 1 else a.reshape(1, -1)
    small_g = [as2d(grads[n]) for n in REPLICATED] + [jnp.broadcast_to(loss, (8, 128))]
    p_sems, p_srcs, p_lands, p_tok = exchange_start(small_g, [True] * len(small_g), [list(range(len(small_g)))],
                                                    "gather_small_start")

    parts, res = {}, {}

    def finish_scatter(entries, after):
        for key, keys, sems, s_srcs, s_lands in entries:
            s_out, lands = exchange_wait(sems, s_srcs, s_lands, [False] * len(keys), after, "scatter_wait_" + key)
            for k, so, ld in zip(keys, s_out, lands):
                parts[k] = (ld, so)

    def update(names_):
        for n in names_:
            if n in MATMUL_WEIGHTS:
                p, o = zip(*[parts[(n, l)] for l in range(DEPTH)])
            else:
                p, o = parts[(n, None)]
            if n in TRANSPOSED:
                res[n] = tuple(a.transpose(0, 2, 1) for a in adamw(p, o, me, *opt_view[n], "adamw_" + n))
            else:
                res[n] = adamw(p, o, me, local[n], mom[n], var[n], "adamw_" + n)

    early = ('ffn_w_up', 'ffn_w_down', 'w_out')
    finish_scatter([e for e in started if e[0] != "last"], p_tok)
    update(early)
    finish_scatter([e for e in started if e[0] == "last"], res[early[-1]][1])
    update([n for n in SHARDED if n not in early])
    updated = jnp.zeros((8, 128), F32) + sum(res[n][1][0, 0, 0] for n in SHARDED)
    p_own, p_lands = exchange_wait(p_sems[0], p_srcs, p_lands, [True] * len(small_g), updated, "gather_small_wait")
    small, loss_sum = adamw_replicated(p_lands[:-1], p_own[:-1], [as2d(local[n]) for n in REPLICATED],
                                       [as2d(mom[n]) for n in REPLICATED], [as2d(var[n]) for n in REPLICATED],
                                       p_lands[-1], p_own[-1])
    for n, quad in zip(REPLICATED, small):
        res[n] = tuple(a.reshape(local[n].shape) for a in quad)
    loss_total = loss_sum[0, 0]

    return (loss_total, dx[None], *[res[n][0] for n in WEIGHTS], *[res[n][1] for n in WEIGHTS],
            *[res[n][2] for n in WEIGHTS], *[res[n][3] for n in WEIGHTS])
```
